```python
import math
import jax
import jax.numpy as jnp
from jax import lax
import numpy as np

D_MODEL = 2048
BATCH = 32
SEQ = 256
DEPTH = 2
DEC_BATCH = 2
DEC_SEQ = 2048
PAST_LEN = 256

GRID_W = 64
GROUP_W = D_MODEL // 4
HG_HEADS = 4
HG_DK = GROUP_W // HG_HEADS
HG_DV = GROUP_W // HG_HEADS
HG_CHUNK = 64
NA_HEADS = 4
NA_DH = GROUP_W // NA_HEADS
WIN_ROWS = 8
WIN_COLS = 16
GQA_HEADS = 4
GQA_KV_HEADS = 2
GQA_DH = GROUP_W // GQA_HEADS
DF_HEADS = 4
DF_DV = GROUP_W // DF_HEADS
DF_DQK = DF_DV // 2
N_EXPERTS = 16
N_EXP_GROUPS = 4
EXP_PER_GROUP = N_EXPERTS // N_EXP_GROUPS
TOP_K = 2
EXPERT_FF = D_MODEL // 4
MOE_BLOCK = 128
Q_BLOCK = 128
ROPE_THETA = 10000.0
EPS = 1e-6
NEG_INF = -1e30
PROJ_WIDTHS = (GROUP_W, GROUP_W, GROUP_W, GROUP_W, GROUP_W,
               NA_HEADS * NA_DH, NA_HEADS * NA_DH, NA_HEADS * NA_DH,
               GQA_HEADS * GQA_DH, GQA_KV_HEADS * GQA_DH, GQA_KV_HEADS * GQA_DH,
               DF_HEADS * 2 * DF_DQK, DF_HEADS * 2 * DF_DQK, DF_HEADS * DF_DV)
IN_WIDTH = sum(PROJ_WIDTHS)

kernel_name = 'hybrid_diffusion_trunk_step'


def rms_norm(x, gain):
    xf = x.astype(jnp.float32)
    y = xf * lax.rsqrt(jnp.mean(xf * xf, axis=-1, keepdims=True) + EPS)
    return (y * gain.astype(jnp.float32)).astype(x.dtype)


def to_heads(x, n_heads):
    b, l, w = x.shape
    return x.reshape(b, l, n_heads, w // n_heads).transpose(0, 2, 1, 3)


def from_heads(x):
    b, h, l, d = x.shape
    return x.transpose(0, 2, 1, 3).reshape(b, l, h * d)


def axial_rope(n_tokens, rot_dim):
    t = jnp.arange(n_tokens)
    row = (t // GRID_W).astype(jnp.float32)
    col = (t % GRID_W).astype(jnp.float32)
    n_freq = rot_dim // 4
    inv = ROPE_THETA ** (-jnp.arange(n_freq, dtype=jnp.float32) / n_freq)
    ang = jnp.concatenate([row[:, None] * inv, col[:, None] * inv], axis=-1)
    return jnp.cos(ang), jnp.sin(ang)


def apply_rope(x, cos, sin):
    xf = x.astype(jnp.float32)
    half = x.shape[-1] // 2
    x1, x2 = xf[..., :half], xf[..., half:]
    return jnp.concatenate([x1 * cos - x2 * sin, x1 * sin + x2 * cos], axis=-1).astype(x.dtype)


def block_softmax_attention(q, k, v):
    b, n, g, lq, d = q.shape
    nb = lq // Q_BLOCK
    scale = d ** -0.5
    qb = jnp.moveaxis(q.reshape(b, n, g, nb, Q_BLOCK, d), 3, 0)

    def one_block(qi):
        s = jnp.einsum('bngqd,bnkd->bngqk', qi, k).astype(jnp.float32) * scale
        p = jax.nn.softmax(s, axis=-1).astype(v.dtype)
        return jnp.einsum('bngqk,bnkd->bngqd', p, v)

    o = lax.map(one_block, qb)
    return jnp.moveaxis(o, 0, 3).reshape(b, n, g, lq, v.shape[-1])


def block_diff_attention(q, k, v, lam):
    b, h, m, lq, d = q.shape
    nb = lq // Q_BLOCK
    scale = d ** -0.5
    qb = jnp.moveaxis(q.reshape(b, h, m, nb, Q_BLOCK, d), 3, 0)

    def one_block(qi):
        s = jnp.einsum('bhmqd,bhmkd->bhmqk', qi, k).astype(jnp.float32) * scale
        p = jax.nn.softmax(s, axis=-1)
        w = (p[:, :, 0] - lam * p[:, :, 1]).astype(v.dtype)
        return jnp.einsum('bhqk,bhkd->bhqd', w, v)

    o = lax.map(one_block, qb)
    return jnp.moveaxis(o, 0, 2).reshape(b, h, lq, v.shape[-1])


def neighbourhood_attention(q, k, v, ctx_k, ctx_v, rpb):
    b, h, l, d = q.shape
    rows = l // GRID_W
    wr = min(WIN_ROWS, rows)
    scale = d ** -0.5
    r = jnp.arange(rows)
    row_idx = jnp.clip(r - wr // 2, 0, rows - wr)[:, None] + jnp.arange(wr)[None, :]
    col = jnp.arange(GRID_W)
    col_start = jnp.clip(col - WIN_COLS // 2, 0, GRID_W - WIN_COLS)
    col_ok = (col[None, :] >= col_start[:, None]) & (col[None, :] < col_start[:, None] + WIN_COLS)
    qg = q.reshape(b, h, rows, GRID_W, d)
    kw = k.reshape(b, h, rows, GRID_W, d)[:, :, row_idx]
    vw = v.reshape(b, h, rows, GRID_W, d)[:, :, row_idx]
    dr_idx = (row_idx - r[:, None] + WIN_ROWS - 1)[:, None, :, None]
    dc_idx = jnp.clip(col[None, :] - col[:, None] + WIN_COLS - 1, 0, 2 * WIN_COLS - 2)[None, :, None, :]
    bias = rpb[:, dr_idx, dc_idx].astype(jnp.float32)
    s_win = jnp.einsum('bhrqd,bhrwkd->bhrqwk', qg, kw).astype(jnp.float32) * scale + bias[None]
    s_win = jnp.where(col_ok[:, None, :], s_win, NEG_INF)
    s_ctx = jnp.einsum('bhrqd,bhsd->bhrqs', qg, ctx_k).astype(jnp.float32) * scale
    n_win = wr * GRID_W
    s = jnp.concatenate([s_win.reshape(b, h, rows, GRID_W, n_win), s_ctx], axis=-1)
    p = jax.nn.softmax(s, axis=-1).astype(v.dtype)
    p_win = p[..., :n_win].reshape(b, h, rows, GRID_W, wr, GRID_W)
    o = (jnp.einsum('bhrqwk,bhrwkd->bhrqd', p_win, vw)
         + jnp.einsum('bhrqs,bhsd->bhrqd', p[..., n_win:], ctx_v))
    return o.reshape(b, h, l, d)


def hgrn_chunk_scan(q, k, v, log_f, s0):
    b, h, l, dk = q.shape
    dv = v.shape[-1]
    n = l // HG_CHUNK

    def chunks(t):
        return jnp.moveaxis(t.reshape(b, h, n, HG_CHUNK, t.shape[-1]), 2, 0)

    incl = jnp.tril(jnp.ones((HG_CHUNK, HG_CHUNK), bool))[:, :, None]

    def step(state, inp):
        qi, ki, vi, gi = inp
        cum = jnp.cumsum(gi, axis=2)
        o_inter = jnp.einsum('bhik,bhkv->bhiv', qi * jnp.exp(cum), state)
        decay = jnp.exp(jnp.where(incl, cum[:, :, :, None, :] - cum[:, :, None, :, :], NEG_INF))
        scores = jnp.einsum('bhik,bhijk,bhjk->bhij', qi, decay, ki)
        o = o_inter + jnp.einsum('bhij,bhjv->bhiv', scores, vi)
        last = cum[:, :, -1:, :]
        state = (jnp.exp(last[:, :, 0, :])[..., None] * state
                 + jnp.einsum('bhjk,bhjv->bhkv', ki * jnp.exp(last - cum), vi))
        return state, o

    state, o = lax.scan(step, s0, (chunks(q), chunks(k), chunks(v), chunks(log_f)))
    return jnp.moveaxis(o, 0, 2).reshape(b, h, l, dv), state


def hgrn2_mixer(a_q, a_ff, a_fb, a_i, a_g, lb, onorm, s0_f, s0_b):
    q = jax.nn.silu(to_heads(a_q, HG_HEADS).astype(jnp.float32))
    v = to_heads(a_i, HG_HEADS).astype(jnp.float32)
    lb = lb.astype(jnp.float32).reshape(2, HG_HEADS, 1, HG_DK)
    f_f = lb[0] + (1.0 - lb[0]) * jax.nn.sigmoid(to_heads(a_ff, HG_HEADS).astype(jnp.float32))
    f_b = lb[1] + (1.0 - lb[1]) * jax.nn.sigmoid(to_heads(a_fb, HG_HEADS).astype(jnp.float32))
    o_f, s_f = hgrn_chunk_scan(q, 1.0 - f_f, v, jnp.log(f_f), s0_f)

    def rev(t):
        return t[:, :, ::-1]

    o_b, s_b = hgrn_chunk_scan(rev(q), rev(1.0 - f_b), rev(v), rev(jnp.log(f_b)), s0_b)
    o = rms_norm(o_f + rev(o_b), onorm) * jax.nn.silu(to_heads(a_g, HG_HEADS).astype(jnp.float32))
    return o, s_f, s_b


def moe_ffn(x, router_w, router_b, w_gate, w_up, w_down):
    t, d = x.shape
    aff = jax.nn.sigmoid(jnp.dot(x, router_w).astype(jnp.float32))
    sel = (aff + router_b.astype(jnp.float32)).reshape(t, N_EXP_GROUPS, EXP_PER_GROUP)
    group_score = lax.top_k(sel, TOP_K)[0].sum(-1)
    g_best = jnp.argmax(group_score, axis=-1)
    in_group = sel[jnp.arange(t), g_best]
    _, local = lax.top_k(in_group, TOP_K)
    idx = g_best[:, None] * EXP_PER_GROUP + local
    w = jnp.take_along_axis(aff, idx, axis=1)
    gates = w / jnp.sum(w, axis=-1, keepdims=True)
    n = t * TOP_K
    flat_e = idx.reshape(n)
    order = jnp.argsort(flat_e)
    sorted_e = flat_e[order]
    counts = jnp.bincount(flat_e, length=N_EXPERTS)
    padded = (counts + MOE_BLOCK - 1) // MOE_BLOCK * MOE_BLOCK
    pad_end = jnp.cumsum(padded)
    pad_start = pad_end - padded
    start = jnp.cumsum(counts) - counts
    dest_sorted = (pad_start[sorted_e] + jnp.arange(n) - start[sorted_e]).astype(jnp.int32)
    dest = jnp.zeros((n,), jnp.int32).at[order].set(dest_sorted)
    n_blocks = (n + N_EXPERTS * (MOE_BLOCK - 1) + MOE_BLOCK - 1) // MOE_BLOCK
    n_rows = n_blocks * MOE_BLOCK
    tok_of_row = jnp.full((n_rows,), t, jnp.int32).at[dest].set(
        jnp.repeat(jnp.arange(t, dtype=jnp.int32), TOP_K))
    block_expert = jnp.minimum(
        jnp.searchsorted(pad_end, jnp.arange(n_blocks) * MOE_BLOCK, side='right'), N_EXPERTS - 1)
    x_pad = jnp.concatenate([x, jnp.zeros((1, d), x.dtype)], axis=0)
    xb = x_pad[tok_of_row].reshape(n_blocks, MOE_BLOCK, d)

    def expert_block(args):
        xi, e = args
        hdn = jax.nn.silu(jnp.dot(xi, w_gate[e])) * jnp.dot(xi, w_up[e])
        return jnp.dot(hdn, w_down[e])

    yb = lax.map(expert_block, (xb, block_expert)).reshape(n_rows, d)
    return jnp.sum(yb[dest].reshape(t, TOP_K, d) * gates[..., None].astype(x.dtype), axis=1)


def trunk_layer(h, cond, layer, lb, ctx, w_mod, b_mod, norm1, norm2, w_in, w_out, hg_onorm,
                na_qn, na_kn, na_rpb, gqa_qn, gqa_kn, df_qn, df_kn, df_lam, df_subln,
                router_w, router_b, w_gate, w_up, w_down):
    b, l, _ = h.shape
    latent = ctx is not None
    mod = (jnp.dot(jax.nn.silu(cond), w_mod) + b_mod)[:, None, :]
    sh1, sc1, g1, sh2, sc2, g2 = jnp.split(mod, 6, axis=-1)
    x = rms_norm(h, norm1) * (1 + sc1) + sh1
    splits = [sum(PROJ_WIDTHS[:i + 1]) for i in range(len(PROJ_WIDTHS) - 1)]
    (a_q, a_ff, a_fb, a_i, a_g, b_q, b_k, b_v, c_q, c_k, c_v, d_q, d_k, d_v) = jnp.split(
        jnp.dot(x, w_in), splits, axis=-1)
    if latent:
        na_ck, na_cv, gqa_ck, gqa_cv, df_ck, df_cv, s0_f, s0_b = ctx
        s0_f = s0_f.astype(jnp.float32)
        s0_b = s0_b.astype(jnp.float32)
    else:
        s0_f = jnp.zeros((b, HG_HEADS, HG_DK, HG_DV), jnp.float32)
        s0_b = jnp.zeros((b, HG_HEADS, HG_DK, HG_DV), jnp.float32)
    o_a, s_f, s_b = hgrn2_mixer(a_q, a_ff, a_fb, a_i, a_g, lb, hg_onorm, s0_f, s0_b)
    qb = rms_norm(to_heads(b_q, NA_HEADS), na_qn)
    kb = rms_norm(to_heads(b_k, NA_HEADS), na_kn)
    vb = to_heads(b_v, NA_HEADS)
    if latent:
        o_b = neighbourhood_attention(qb, kb, vb, na_ck, na_cv, na_rpb)
    else:
        o_b = block_softmax_attention(qb[:, :, None], kb, vb)[:, :, 0]
    qc = rms_norm(to_heads(c_q, GQA_HEADS), gqa_qn).reshape(
        b, GQA_KV_HEADS, GQA_HEADS // GQA_KV_HEADS, l, GQA_DH)
    kc = rms_norm(to_heads(c_k, GQA_KV_HEADS), gqa_kn)
    vc = to_heads(c_v, GQA_KV_HEADS)
    if latent:
        cos_c, sin_c = axial_rope(l, GQA_DH)
        o_c = block_softmax_attention(apply_rope(qc, cos_c, sin_c),
                                      jnp.concatenate([apply_rope(kc, cos_c, sin_c), gqa_ck], axis=2),
                                      jnp.concatenate([vc, gqa_cv], axis=2))
    else:
        o_c = block_softmax_attention(qc, kc, vc)
    o_c = o_c.reshape(b, GQA_HEADS, l, GQA_DH)
    qd = rms_norm(d_q.reshape(b, l, DF_HEADS, 2, DF_DQK).transpose(0, 2, 3, 1, 4), df_qn)
    kd = rms_norm(d_k.reshape(b, l, DF_HEADS, 2, DF_DQK).transpose(0, 2, 3, 1, 4), df_kn)
    vd = to_heads(d_v, DF_HEADS)
    lam_init = 0.8 - 0.6 * math.exp(-0.3 * layer)
    lamf = df_lam.astype(jnp.float32)
    lam = jnp.exp(jnp.sum(lamf[0] * lamf[1])) - jnp.exp(jnp.sum(lamf[2] * lamf[3])) + lam_init
    if latent:
        cos_d, sin_d = axial_rope(l, DF_DQK)
        o_d = block_diff_attention(apply_rope(qd, cos_d, sin_d),
                                   jnp.concatenate([apply_rope(kd, cos_d, sin_d), df_ck], axis=3),
                                   jnp.concatenate([vd, df_cv], axis=2), lam)
    else:
        o_d = block_diff_attention(qd, kd, vd, lam)
    o_d = rms_norm(o_d, df_subln) * (1.0 - lam_init)
    mixed = jnp.concatenate([from_heads(o_a), from_heads(o_b), from_heads(o_c), from_heads(o_d)],
                            axis=-1).astype(h.dtype)
    h = h + g1 * jnp.dot(mixed, w_out)
    x2 = rms_norm(h, norm2) * (1 + sc2) + sh2
    y = moe_ffn(x2.reshape(b * l, -1), router_w, router_b, w_gate, w_up, w_down).reshape(h.shape)
    h = h + g2 * y
    ctx_out = None if latent else (kb, vb, kc, vc, kd, vd, s_f, s_b)
    return h, ctx_out


def setup_inputs(seed: int = 0) -> dict:
    key = jax.random.key(seed)
    keys = jax.random.split(key, 40)
    counter = [0]

    def normal(shape, scale):
        k = keys[counter[0]]
        counter[0] += 1
        return jax.random.normal(k, shape, jnp.float32) * scale

    def gain(shape):
        return 1.0 + normal(shape, 0.02)

    dsc = D_MODEL ** -0.5
    return {
        'x_prompt': normal((BATCH, SEQ, D_MODEL), 1.0),
        'x_sample': normal((DEC_BATCH, DEC_SEQ, D_MODEL), 1.0),
        'cache_na_k': normal((DEC_BATCH, DEPTH, NA_HEADS, PAST_LEN, NA_DH), 1.0),
        'cache_na_v': normal((DEC_BATCH, DEPTH, NA_HEADS, PAST_LEN, NA_DH), 1.0),
        'cache_gqa_k': normal((DEC_BATCH, DEPTH, GQA_KV_HEADS, PAST_LEN, GQA_DH), 1.0),
        'cache_gqa_v': normal((DEC_BATCH, DEPTH, GQA_KV_HEADS, PAST_LEN, GQA_DH), 1.0),
        'cache_diff_k': normal((DEC_BATCH, DEPTH, DF_HEADS, 2, PAST_LEN, DF_DQK), 1.0),
        'cache_diff_v': normal((DEC_BATCH, DEPTH, DF_HEADS, PAST_LEN, DF_DV), 1.0),
        'state_hgrn': normal((DEC_BATCH, DEPTH, 2, HG_HEADS, HG_DK, HG_DV), 0.3),
        'c': normal((DEC_BATCH, D_MODEL), 1.0),
        'c_ctx': normal((D_MODEL,), 1.0),
        'w_mod': normal((DEPTH, D_MODEL, 6 * D_MODEL), 0.5 * dsc),
        'b_mod': normal((DEPTH, 6 * D_MODEL), 0.02),
        'norm1': gain((DEPTH, D_MODEL)),
        'norm2': gain((DEPTH, D_MODEL)),
        'w_in': normal((DEPTH, D_MODEL, IN_WIDTH), dsc),
        'w_out': normal((DEPTH, D_MODEL, D_MODEL), dsc),
        'hg_lb_logits': normal((DEPTH, 2, HG_HEADS * HG_DK), 1.0),
        'hg_onorm': gain((DEPTH, HG_DV)),
        'na_qn': gain((DEPTH, NA_DH)),
        'na_kn': gain((DEPTH, NA_DH)),
        'na_rpb': normal((DEPTH, NA_HEADS, 2 * WIN_ROWS - 1, 2 * WIN_COLS - 1), 0.1),
        'gqa_qn': gain((DEPTH, GQA_DH)),
        'gqa_kn': gain((DEPTH, GQA_DH)),
        'df_qn': gain((DEPTH, DF_DQK)),
        'df_kn': gain((DEPTH, DF_DQK)),
        'df_lam': normal((DEPTH, 4, DF_DQK), 0.1),
        'df_subln': gain((DEPTH, DF_DV)),
        'router_w': normal((D_MODEL, N_EXPERTS), dsc),
        'router_b': normal((N_EXPERTS,), 0.01),
        'w_gate': normal((DEPTH, N_EXPERTS, D_MODEL, EXPERT_FF), dsc),
        'w_up': normal((DEPTH, N_EXPERTS, D_MODEL, EXPERT_FF), dsc),
        'w_down': normal((DEPTH, N_EXPERTS, EXPERT_FF, D_MODEL), EXPERT_FF ** -0.5),
    }


def reference(x_prompt, x_sample, cache_na_k, cache_na_v, cache_gqa_k, cache_gqa_v, cache_diff_k,
              cache_diff_v, state_hgrn, c, c_ctx, w_mod, b_mod, norm1, norm2, w_in, w_out, hg_lb_logits,
              hg_onorm, na_qn, na_kn, na_rpb, gqa_qn, gqa_kn, df_qn, df_kn, df_lam, df_subln,
              router_w, router_b, w_gate, w_up, w_down):
    sm = jax.nn.softmax(hg_lb_logits.astype(jnp.float32), axis=0)
    lower = jnp.cumsum(sm, axis=0) - sm[0:1]
    y_prompt = x_prompt
    y_sample = x_sample
    per_layer = ([], [], [], [], [], [], [], [])
    for layer in range(DEPTH):
        weights = (w_mod[layer], b_mod[layer], norm1[layer], norm2[layer], w_in[layer], w_out[layer],
                   hg_onorm[layer], na_qn[layer], na_kn[layer], na_rpb[layer], gqa_qn[layer],
                   gqa_kn[layer], df_qn[layer], df_kn[layer], df_lam[layer], df_subln[layer],
                   router_w, router_b, w_gate[layer], w_up[layer], w_down[layer])
        y_prompt, ctx_out = trunk_layer(y_prompt, c_ctx[None], layer, lower[layer], None, *weights)
        for store, tensor in zip(per_layer, ctx_out):
            store.append(tensor)
        cached = (cache_na_k[:, layer], cache_na_v[:, layer], cache_gqa_k[:, layer], cache_gqa_v[:, layer],
                  cache_diff_k[:, layer], cache_diff_v[:, layer], state_hgrn[:, layer, 0],
                  state_hgrn[:, layer, 1])
        y_sample, _ = trunk_layer(y_sample, c, layer, lower[layer], cached, *weights)
    new_na_k = jnp.stack(per_layer[0], axis=1)
    new_na_v = jnp.stack(per_layer[1], axis=1)
    new_gqa_k = jnp.stack(per_layer[2], axis=1)
    new_gqa_v = jnp.stack(per_layer[3], axis=1)
    new_diff_k = jnp.stack(per_layer[4], axis=1)
    new_diff_v = jnp.stack(per_layer[5], axis=1)
    new_state_hgrn = jnp.stack([jnp.stack([sf, sb], axis=1) for sf, sb in zip(per_layer[6], per_layer[7])],
                               axis=1)
    return (y_prompt, y_sample, new_na_k, new_na_v, new_gqa_k, new_gqa_v, new_diff_k, new_diff_v, new_state_hgrn)
```

```python
import functools
import math

import numpy as np
import jax
import jax.numpy as jnp
from jax import lax
from jax.experimental import pallas as pl
from jax.experimental.pallas import tpu as pltpu

D_MODEL = 2048
GRID_W = 64
GROUP_W = D_MODEL // 4
N_HEADS = 4
HEAD_W = GROUP_W // N_HEADS
GQA_KV_HEADS = 2
DF_DQK = HEAD_W // 2
WIN_ROWS = 8
WIN_COLS = 16
N_EXPERTS = 16
N_EXP_GROUPS = 4
EXP_PER_GROUP = N_EXPERTS // N_EXP_GROUPS
EXPERT_FF = D_MODEL // 4
ROPE_THETA = 10000.0
EPS = 1e-6
NEG_INF = -1e30
IN_WIDTH = 13 * GROUP_W

COL_A_Q, COL_A_FF, COL_A_FB, COL_A_I, COL_A_G = 0, 4, 8, 12, 16
COL_B_Q, COL_B_K, COL_B_V = 20, 24, 28
COL_C_Q, COL_C_K, COL_C_V = 32, 36, 38
COL_D_Q, COL_D_K, COL_D_V = 40, 44, 48

HGRN_CHUNK = 64
MOE_ROWS = 256
VMEM_LIMIT = 48 * 1024 * 1024

F32 = jnp.float32
BF16 = jnp.bfloat16


def _params(*sem):
    return pltpu.CompilerParams(dimension_semantics=sem, vmem_limit_bytes=VMEM_LIMIT)


def _sigmoid(x):
    return 1.0 / (1.0 + jnp.exp(-x))


def _silu(x):
    return x * _sigmoid(x)


def _rms(x, gain, n):
    return x * lax.rsqrt(jnp.sum(x * x, axis=-1, keepdims=True) * (1.0 / n) + EPS) * gain


def _dot(a, b):
    return jnp.dot(a, b, preferred_element_type=F32)


def _dot_nt(a, b):
    return lax.dot_general(a, b, (((1,), (1,)), ((), ())), preferred_element_type=F32)


def _dot_tn(a, b):
    return lax.dot_general(a, b, (((0,), (0,)), ((), ())), preferred_element_type=F32)


def _mod_body(cond_ref, w_ref, b_ref, o_ref):
    w = w_ref[...]
    for c in range(cond_ref.shape[0]):
        s = _silu(cond_ref[c])
        o_ref[c:c + 1, :] = jnp.sum(w * s, axis=0, keepdims=True) + b_ref[...]


def _modulation(cond, w_mod, b_mod):
    depth, d, n6 = w_mod.shape
    nc = cond.shape[0]
    tn = 512
    return pl.pallas_call(
        _mod_body,
        grid=(depth, n6 // tn),
        in_specs=[pl.BlockSpec((nc, d, 1), lambda l, j: (0, 0, 0)),
                  pl.BlockSpec((None, d, tn), lambda l, j: (l, 0, j)),
                  pl.BlockSpec((None, 1, tn), lambda l, j: (l, 0, j))],
        out_specs=pl.BlockSpec((None, nc, tn), lambda l, j: (l, 0, j)),
        out_shape=jax.ShapeDtypeStruct((depth, nc, n6), F32),
        compiler_params=_params("parallel", "parallel"),
        name="adaln_modulation",
    )(cond[:, :, None], w_mod, b_mod[:, None, :])


def _inproj_body(h_ref, mod_ref, n1_ref, w_ref, o_ref, xn_ref):
    @pl.when(pl.program_id(1) == 0)
    def _():
        y = _rms(h_ref[...], n1_ref[...], D_MODEL)
        xn_ref[...] = (y * (1.0 + mod_ref[1:2, :]) + mod_ref[0:1, :]).astype(BF16)

    o_ref[...] = _dot(xn_ref[...], w_ref[...])


def _input_projection(h, mod, norm1, w_in_bf16, cond_of_tile, tm):
    t, d = h.shape
    n = w_in_bf16.shape[1]
    tn = 512
    return pl.pallas_call(
        _inproj_body,
        grid=(t // tm, n // tn),
        in_specs=[pl.BlockSpec((tm, d), lambda i, j: (i, 0)),
                  pl.BlockSpec((None, 6, d), lambda i, j: (cond_of_tile(i), 0, 0)),
                  pl.BlockSpec((1, d), lambda i, j: (0, 0)),
                  pl.BlockSpec((d, tn), lambda i, j: (0, j))],
        out_specs=pl.BlockSpec((tm, tn), lambda i, j: (i, j)),
        out_shape=jax.ShapeDtypeStruct((t, n), F32),
        scratch_shapes=[pltpu.VMEM((tm, d), BF16)],
        compiler_params=_params("parallel", "arbitrary"),
        name="norm_modulate_in_proj",
    )(h, mod, norm1[None, :], w_in_bf16)


def _hgrn_constants(c):
    nl = int(math.log2(c))
    idx = np.arange(c)
    e = np.zeros((nl + 2, c, c), np.float32)
    m = np.zeros((nl + 1, c, c), np.float32)
    e[0] = idx[None, :] <= idx[:, None]
    e[1] = idx[None, :] > idx[:, None]
    m[0] = np.eye(c)
    for li in range(nl):
        s = c >> (li + 1)
        parent = idx // (2 * s)
        right = (idx % (2 * s)) >= s
        ref = parent * 2 * s + s - 1
        for i in range(c):
            if right[i]:
                e[2 + li, i, ref[i] + 1:i + 1] = 1.0
            else:
                e[2 + li, i, i + 1:ref[i] + 1] = 1.0
        m[1 + li] = right[:, None] & ~right[None, :] & (parent[:, None] == parent[None, :])
    e2 = np.stack([e, e[:, ::-1, ::-1]]).reshape(2, (nl + 2) * c, c)
    m2 = np.stack([m, m[:, ::-1, ::-1]])
    return jnp.asarray(e2, BF16), jnp.asarray(m2, F32)


def _hgrn_body(*refs, seq, chunk, has_s0, emit_state):
    q_ref, ff_ref, fb_ref, i_ref, g_ref, lb_ref, on_ref, e_ref, m_ref = refs[:9]
    pos = 9
    s0_ref = None
    if has_s0:
        s0_ref = refs[pos]
        pos += 1
    o_ref = refs[pos]
    pos += 1
    if emit_state:
        st_ref = refs[pos]
        pos += 1
    of_ref, ob_ref = refs[pos], refs[pos + 1]
    c = chunk
    n_chunks = seq // c
    n_levels = m_ref.shape[1] - 1
    gate_refs = (ff_ref, fb_ref)
    out_refs = (of_ref, ob_ref)

    def chunk_step(c0, d, st):
        rows = pl.ds(c0, c)
        lb = lb_ref[d:d + 1, :]
        f = lb + (1.0 - lb) * _sigmoid(gate_refs[d][rows, :])
        g = jnp.log(f)
        k = 1.0 - f
        q = _silu(q_ref[rows, :])
        v = i_ref[rows, :].astype(BF16)
        g_hi = g.astype(BF16)
        r1 = g - g_hi.astype(F32)
        g_mid = r1.astype(BF16)
        g_lo = (r1 - g_mid.astype(F32)).astype(BF16)
        g3 = _dot(e_ref[d], jnp.concatenate([g_hi, g_mid, g_lo], axis=1))
        x = jnp.exp(g3[:, 0:HEAD_W] + g3[:, HEAD_W:2 * HEAD_W] + g3[:, 2 * HEAD_W:3 * HEAD_W])
        x_cum = x[0:c]
        x_tail = x[c:2 * c]
        s = m_ref[d, 0] * _dot_nt(q.astype(BF16), k.astype(BF16))
        for lv in range(n_levels):
            x_l = x[(2 + lv) * c:(3 + lv) * c]
            s = s + m_ref[d, 1 + lv] * _dot_nt((q * x_l).astype(BF16), (k * x_l).astype(BF16))
        o = _dot_nt((q * x_cum).astype(BF16), st.astype(BF16)) + _dot(s.astype(BF16), v)
        out_refs[d][rows, :] = o
        total = x_cum[c - 1:c, :] if d == 0 else x_cum[0:1, :]
        return st * total + _dot_tn(v, (k * x_tail).astype(BF16))

    if has_s0:
        st_f0 = s0_ref[0].T
        st_b0 = s0_ref[1].T
    else:
        st_f0 = jnp.zeros((HEAD_W, HEAD_W), F32)
        st_b0 = jnp.zeros((HEAD_W, HEAD_W), F32)

    def loop(t, carry):
        st_f, st_b = carry
        st_f = chunk_step(pl.multiple_of(t * c, c), 0, st_f)
        st_b = chunk_step(pl.multiple_of((n_chunks - 1 - t) * c, c), 1, st_b)
        return st_f, st_b

    st_f, st_b = lax.fori_loop(0, n_chunks, loop, (st_f0, st_b0))
    o = of_ref[...] + ob_ref[...]
    o_ref[...] = (_rms(o, on_ref[...], HEAD_W) * _silu(g_ref[...])).astype(o_ref.dtype)
    if emit_state:
        st_ref[0] = st_f.T
        st_ref[1] = st_b.T


def _hgrn(proj, row_block0, n_seq, seq, lower, onorm, consts, s0, emit_state):
    e_mat, masks = consts

    def col(cb):
        return pl.BlockSpec((seq, HEAD_W), lambda b, h, cb=cb: (row_block0 + b, cb + h))

    in_specs = [col(COL_A_Q), col(COL_A_FF), col(COL_A_FB), col(COL_A_I), col(COL_A_G),
                pl.BlockSpec((2, HEAD_W), lambda b, h: (0, h)),
                pl.BlockSpec((1, HEAD_W), lambda b, h: (0, 0)),
                pl.BlockSpec(e_mat.shape, lambda b, h: (0, 0, 0)),
                pl.BlockSpec(masks.shape, lambda b, h: (0, 0, 0, 0))]
    args = [proj, proj, proj, proj, proj, lower, onorm[None, :], e_mat, masks]
    if s0 is not None:
        in_specs.append(pl.BlockSpec((None, 2, None, HEAD_W, HEAD_W), lambda b, h: (b, 0, h, 0, 0)))
        args.append(s0)
    out_specs = [pl.BlockSpec((seq, HEAD_W), lambda b, h: (b, h))]
    out_shape = [jax.ShapeDtypeStruct((n_seq * seq, GROUP_W), BF16)]
    if emit_state:
        out_specs.append(pl.BlockSpec((None, 2, None, HEAD_W, HEAD_W), lambda b, h: (b, 0, h, 0, 0)))
        out_shape.append(jax.ShapeDtypeStruct((n_seq, 2, N_HEADS, HEAD_W, HEAD_W), F32))
    res = pl.pallas_call(
        functools.partial(_hgrn_body, seq=seq, chunk=HGRN_CHUNK, has_s0=s0 is not None,
                          emit_state=emit_state),
        grid=(n_seq, N_HEADS),
        in_specs=in_specs, out_specs=out_specs, out_shape=out_shape,
        scratch_shapes=[pltpu.VMEM((seq, HEAD_W), F32), pltpu.VMEM((seq, HEAD_W), F32)],
        compiler_params=_params("parallel", "parallel"),
        name="hgrn2_latent" if s0 is not None else "hgrn2_context",
    )(*args)
    return res if emit_state else (res[0], None)


def _softmax_pv(scores, values):
    mx = functools.reduce(jnp.maximum, [jnp.max(s, axis=-1, keepdims=True) for s in scores])
    es = [jnp.exp(s - mx) for s in scores]
    den = functools.reduce(lambda a, b: a + b, [jnp.sum(e, axis=-1, keepdims=True) for e in es])
    num = functools.reduce(lambda a, b: a + b, [_dot(e.astype(BF16), v) for e, v in zip(es, values)])
    return num / den


def _diff_weights(s0, s1, lam):
    e0 = jnp.exp(s0 - jnp.max(s0, axis=-1, keepdims=True))
    e1 = jnp.exp(s1 - jnp.max(s1, axis=-1, keepdims=True))
    p0 = e0 / jnp.sum(e0, axis=-1, keepdims=True)
    p1 = e1 / jnp.sum(e1, axis=-1, keepdims=True)
    return (p0 - lam * p1).astype(BF16)


def _lane_lt(shape, n):
    return lax.broadcasted_iota(jnp.int32, shape, len(shape) - 1) < n


def _rms_halves(x, gain2):
    lo = _lane_lt(x.shape, DF_DQK)
    sq = x * x
    ss_lo = jnp.sum(jnp.where(lo, sq, 0.0), axis=-1, keepdims=True)
    ss_hi = jnp.sum(sq, axis=-1, keepdims=True) - ss_lo
    inv = jnp.where(lo, lax.rsqrt(ss_lo * (1.0 / DF_DQK) + EPS), lax.rsqrt(ss_hi * (1.0 / DF_DQK) + EPS))
    return x * inv * gain2


def _lambda(lam_ref, lam_init):
    l = lam_ref[...]
    return (jnp.exp(jnp.sum(l[0:1] * l[1:2], axis=-1, keepdims=True))
            - jnp.exp(jnp.sum(l[2:3] * l[3:4], axis=-1, keepdims=True)) + lam_init)


def _ctx_attn_body(bq_ref, bk_ref, bv_ref, cq_ref, ck_ref, cv_ref, dq_ref, dk_ref, dv_ref,
                   naq_ref, nak_ref, gq_ref, gk_ref, dfq_ref, dfk_ref, sub_ref, lam_ref,
                   ob_ref, oc_ref, od_ref, kb_ref, vb_ref, kc_ref, vc_ref, kd_ref, vd_ref, *, lam_init):
    scale = HEAD_W ** -0.5
    kb = _rms(bk_ref[...], nak_ref[...], HEAD_W)
    vb = bv_ref[...]
    kb_ref[...] = kb
    vb_ref[...] = vb
    qb = (_rms(bq_ref[...], naq_ref[...], HEAD_W) * scale).astype(BF16)
    ob_ref[...] = _softmax_pv([_dot_nt(qb, kb.astype(BF16))], [vb.astype(BF16)]).astype(ob_ref.dtype)
    kc = _rms(ck_ref[...], gk_ref[...], HEAD_W)
    vc = cv_ref[...]
    kc_ref[...] = kc
    vc_ref[...] = vc
    qc = (_rms(cq_ref[...], gq_ref[...], HEAD_W) * scale).astype(BF16)
    oc_ref[...] = _softmax_pv([_dot_nt(qc, kc.astype(BF16))], [vc.astype(BF16)]).astype(oc_ref.dtype)
    kd = _rms_halves(dk_ref[...], dfk_ref[...])
    vd = dv_ref[...]
    kd_ref[...] = kd
    vd_ref[...] = vd
    qd = (_rms_halves(dq_ref[...], dfq_ref[...]) * (DF_DQK ** -0.5)).astype(BF16)
    lo = _lane_lt(kd.shape, DF_DQK)
    s0 = _dot_nt(qd, jnp.where(lo, kd, 0.0).astype(BF16))
    s1 = _dot_nt(qd, jnp.where(lo, 0.0, kd).astype(BF16))
    od = _dot(_diff_weights(s0, s1, _lambda(lam_ref, lam_init)), vd.astype(BF16))
    od_ref[...] = (_rms(od, sub_ref[...], HEAD_W) * (1.0 - lam_init)).astype(od_ref.dtype)


def _context_attention(proj, n_seq, seq, gains, lam_init):
    na_qn, na_kn, gqa_qn, gqa_kn, df_qn, df_kn, df_subln, df_lam = gains

    def col(cb, div=1):
        return pl.BlockSpec((seq, HEAD_W), lambda b, h, cb=cb, div=div: (b, cb + h // div))

    def vec(width=HEAD_W):
        return pl.BlockSpec((1, width), lambda b, h: (0, 0))

    def cache(div=1):
        return pl.BlockSpec((None, None, seq, HEAD_W), lambda b, h, div=div: (b, h // div, 0, 0))

    mixed = pl.BlockSpec((seq, HEAD_W), lambda b, h: (b, h))
    mixed_shape = jax.ShapeDtypeStruct((n_seq * seq, GROUP_W), BF16)
    cache4 = jax.ShapeDtypeStruct((n_seq, N_HEADS, seq, HEAD_W), F32)
    cache2 = jax.ShapeDtypeStruct((n_seq, GQA_KV_HEADS, seq, HEAD_W), F32)
    group = N_HEADS // GQA_KV_HEADS
    return pl.pallas_call(
        functools.partial(_ctx_attn_body, lam_init=lam_init),
        grid=(n_seq, N_HEADS),
        in_specs=[col(COL_B_Q), col(COL_B_K), col(COL_B_V),
                  col(COL_C_Q), col(COL_C_K, group), col(COL_C_V, group),
                  col(COL_D_Q), col(COL_D_K), col(COL_D_V),
                  vec(), vec(), vec(), vec(), vec(), vec(), vec(),
                  pl.BlockSpec((4, DF_DQK), lambda b, h: (0, 0))],
        out_specs=[mixed, mixed, mixed, cache(), cache(), cache(group), cache(group), cache(), cache()],
        out_shape=[mixed_shape, mixed_shape, mixed_shape, cache4, cache4, cache2, cache2, cache4, cache4],
        compiler_params=_params("parallel", "arbitrary"),
        name="context_attention",
    )(proj, proj, proj, proj, proj, proj, proj, proj, proj,
      na_qn[None, :], na_kn[None, :], gqa_qn[None, :], gqa_kn[None, :],
      jnp.tile(df_qn, 2)[None, :], jnp.tile(df_kn, 2)[None, :], df_subln[None, :], df_lam)


def _rope_tables(n_tokens, rot_dim):
    t = np.arange(n_tokens)
    row = (t // GRID_W).astype(np.float32)
    col = (t % GRID_W).astype(np.float32)
    n_freq = rot_dim // 4
    inv = (np.float32(ROPE_THETA) ** (-np.arange(n_freq, dtype=np.float32) / np.float32(n_freq))).astype(np.float32)
    ang = np.concatenate([row[:, None] * inv, col[:, None] * inv], axis=-1).astype(np.float32)
    cos, sin, zero = np.cos(ang), np.sin(ang), np.zeros_like(ang)
    reps = HEAD_W // rot_dim
    a = np.tile(np.concatenate([cos, cos], axis=-1), (1, reps))
    b = np.tile(np.concatenate([-sin, zero], axis=-1), (1, reps))
    c = np.tile(np.concatenate([zero, sin], axis=-1), (1, reps))
    return jnp.asarray(np.stack([a, b, c]), F32)


def _rope(x, tab_ref, half):
    return (x * tab_ref[0] + pltpu.roll(x, HEAD_W - half, 1) * tab_ref[1]
            + pltpu.roll(x, half, 1) * tab_ref[2])


def _na_body(q_ref, k_ref, v_ref, ck_ref, cv_ref, bias_ref, qn_ref, kn_ref, o_ref, qs_ref, ks_ref, vs_ref,
             *, seq):
    rows = seq // GRID_W
    n_win = WIN_ROWS * GRID_W
    qs_ref[...] = (_rms(q_ref[...], qn_ref[...], HEAD_W) * (HEAD_W ** -0.5)).astype(BF16)
    ks_ref[...] = _rms(k_ref[...], kn_ref[...], HEAD_W).astype(BF16)
    vs_ref[...] = v_ref[...].astype(BF16)
    ck = ck_ref[...].astype(BF16)
    cv = cv_ref[...].astype(BF16)

    def row_step(r, carry):
        start = jnp.clip(r - WIN_ROWS // 2, 0, rows - WIN_ROWS)
        win = pl.ds(pl.multiple_of(start * GRID_W, GRID_W), n_win)
        qrows = pl.ds(pl.multiple_of(r * GRID_W, GRID_W), GRID_W)
        q = qs_ref[qrows, :]
        s_win = _dot_nt(q, ks_ref[win, :]) + bias_ref[r]
        s_ctx = _dot_nt(q, ck)
        o_ref[qrows, :] = _softmax_pv([s_win, s_ctx], [vs_ref[win, :], cv]).astype(o_ref.dtype)
        return carry

    lax.fori_loop(0, rows, row_step, 0)


def _na_bias(rpb, rows):
    r = np.arange(rows)
    row_idx = np.clip(r - WIN_ROWS // 2, 0, rows - WIN_ROWS)[:, None] + np.arange(WIN_ROWS)[None, :]
    col = np.arange(GRID_W)
    col_start = np.clip(col - WIN_COLS // 2, 0, GRID_W - WIN_COLS)
    col_ok = (col[None, :] >= col_start[:, None]) & (col[None, :] < col_start[:, None] + WIN_COLS)
    dr_idx = (row_idx - r[:, None] + WIN_ROWS - 1)[:, None, :, None]
    dc_idx = np.clip(col[None, :] - col[:, None] + WIN_COLS - 1, 0, 2 * WIN_COLS - 2)[None, :, None, :]
    bias = rpb[:, dr_idx, dc_idx].astype(F32)
    bias = jnp.where(col_ok[None, None, :, None, :], bias, NEG_INF)
    return bias.reshape(rpb.shape[0], rows, GRID_W, WIN_ROWS * GRID_W)


def _latent_na(proj, row_block0, n_seq, seq, cache_k, cache_v, layer, bias, na_qn, na_kn):
    past = cache_k.shape[3]
    rows = seq // GRID_W

    def col(cb):
        return pl.BlockSpec((seq, HEAD_W), lambda b, h, cb=cb: (row_block0 + b, cb + h))

    cache = pl.BlockSpec((None, None, None, past, HEAD_W), lambda b, h: (b, layer, h, 0, 0))
    vec = pl.BlockSpec((1, HEAD_W), lambda b, h: (0, 0))
    return pl.pallas_call(
        functools.partial(_na_body, seq=seq),
        grid=(n_seq, N_HEADS),
        in_specs=[col(COL_B_Q), col(COL_B_K), col(COL_B_V), cache, cache,
                  pl.BlockSpec((None, rows, GRID_W, WIN_ROWS * GRID_W), lambda b, h: (h, 0, 0, 0)),
                  vec, vec],
        out_specs=pl.BlockSpec((seq, HEAD_W), lambda b, h: (b, h)),
        out_shape=jax.ShapeDtypeStruct((n_seq * seq, GROUP_W), BF16),
        scratch_shapes=[pltpu.VMEM((seq, HEAD_W), BF16)] * 3,
        compiler_params=_params("parallel", "parallel"),
        name="latent_neighbourhood_attention",
    )(proj, proj, proj, cache_k, cache_v, bias, na_qn[None, :], na_kn[None, :])


def _gqa_body(q_ref, k_ref, v_ref, ck_ref, cv_ref, rope_ref, qn_ref, kn_ref, o_ref, qs_ref, ks_ref, vs_ref,
              *, seq, tq):
    group = N_HEADS // GQA_KV_HEADS
    half = HEAD_W // 2
    ks_ref[0:seq, :] = _rope(_rms(k_ref[...], kn_ref[...], HEAD_W), rope_ref, half).astype(BF16)
    ks_ref[seq:, :] = ck_ref[...].astype(BF16)
    vs_ref[0:seq, :] = v_ref[...].astype(BF16)
    vs_ref[seq:, :] = cv_ref[...].astype(BF16)
    for g in range(group):
        q = _rms(q_ref[:, g * HEAD_W:(g + 1) * HEAD_W], qn_ref[...], HEAD_W) * (HEAD_W ** -0.5)
        qs_ref[g] = _rope(q, rope_ref, half).astype(BF16)
    kk = ks_ref[...]
    vv = vs_ref[...]
    for g in range(group):
        def q_step(i, carry, g=g):
            qrows = pl.ds(pl.multiple_of(i * tq, tq), tq)
            o = _softmax_pv([_dot_nt(qs_ref[g, qrows, :], kk)], [vv])
            o_ref[qrows, g * HEAD_W:(g + 1) * HEAD_W] = o.astype(o_ref.dtype)
            return carry

        lax.fori_loop(0, seq // tq, q_step, 0)


def _latent_gqa(proj, row_block0, n_seq, seq, cache_k, cache_v, layer, rope, gqa_qn, gqa_kn):
    past = cache_k.shape[3]
    group = N_HEADS // GQA_KV_HEADS
    tq = 256
    cache = pl.BlockSpec((None, None, None, past, HEAD_W), lambda b, n: (b, layer, n, 0, 0))
    vec = pl.BlockSpec((1, HEAD_W), lambda b, n: (0, 0))
    return pl.pallas_call(
        functools.partial(_gqa_body, seq=seq, tq=tq),
        grid=(n_seq, GQA_KV_HEADS),
        in_specs=[pl.BlockSpec((seq, group * HEAD_W), lambda b, n: (row_block0 + b, COL_C_Q // group + n)),
                  pl.BlockSpec((seq, HEAD_W), lambda b, n: (row_block0 + b, COL_C_K + n)),
                  pl.BlockSpec((seq, HEAD_W), lambda b, n: (row_block0 + b, COL_C_V + n)),
                  cache, cache,
                  pl.BlockSpec((3, seq, HEAD_W), lambda b, n: (0, 0, 0)),
                  vec, vec],
        out_specs=pl.BlockSpec((seq, group * HEAD_W), lambda b, n: (b, n)),
        out_shape=jax.ShapeDtypeStruct((n_seq * seq, GROUP_W), BF16),
        scratch_shapes=[pltpu.VMEM((group, seq, HEAD_W), BF16),
                        pltpu.VMEM((seq + past, HEAD_W), BF16),
                        pltpu.VMEM((seq + past, HEAD_W), BF16)],
        compiler_params=_params("parallel", "parallel"),
        name="latent_gqa_attention",
    )(proj, proj, proj, cache_k, cache_v, rope, gqa_qn[None, :], gqa_kn[None, :])


def _diff_body(q_ref, k_ref, v_ref, ck_ref, cv_ref, rope_ref, qn_ref, kn_ref, sub_ref, lam_ref, o_ref,
               qs_ref, k0_ref, k1_ref, vs_ref, *, seq, tq, lam_init):
    half = DF_DQK // 2
    k = _rope(_rms_halves(k_ref[...], kn_ref[...]), rope_ref, half)
    lo = _lane_lt(k.shape, DF_DQK)
    k0_ref[0:seq, :] = jnp.where(lo, k, 0.0).astype(BF16)
    k1_ref[0:seq, :] = jnp.where(lo, 0.0, k).astype(BF16)
    ck = ck_ref[...]
    lo_c = _lane_lt(ck.shape, DF_DQK)
    k0_ref[seq:, :] = jnp.where(lo_c, ck, 0.0).astype(BF16)
    k1_ref[seq:, :] = jnp.where(lo_c, 0.0, ck).astype(BF16)
    vs_ref[0:seq, :] = v_ref[...].astype(BF16)
    vs_ref[seq:, :] = cv_ref[...].astype(BF16)
    q = _rms_halves(q_ref[...], qn_ref[...]) * (DF_DQK ** -0.5)
    qs_ref[...] = _rope(q, rope_ref, half).astype(BF16)
    lam = _lambda(lam_ref, lam_init)
    k0 = k0_ref[...]
    k1 = k1_ref[...]
    vv = vs_ref[...]

    def q_step(i, carry):
        qrows = pl.ds(pl.multiple_of(i * tq, tq), tq)
        qb = qs_ref[qrows, :]
        o = _dot(_diff_weights(_dot_nt(qb, k0), _dot_nt(qb, k1), lam), vv)
        o_ref[qrows, :] = (_rms(o, sub_ref[...], HEAD_W) * (1.0 - lam_init)).astype(o_ref.dtype)
        return carry

    lax.fori_loop(0, seq // tq, q_step, 0)


def _latent_diff(proj, row_block0, n_seq, seq, cache_k2, cache_v, layer, rope, df_qn, df_kn, df_subln, df_lam,
                 lam_init):
    past = cache_k2.shape[3]
    tq = 256

    def col(cb):
        return pl.BlockSpec((seq, HEAD_W), lambda b, h, cb=cb: (row_block0 + b, cb + h))

    cache = pl.BlockSpec((None, None, None, past, HEAD_W), lambda b, h: (b, layer, h, 0, 0))
    vec = pl.BlockSpec((1, HEAD_W), lambda b, h: (0, 0))
    kv_scratch = pltpu.VMEM((seq + past, HEAD_W), BF16)
    return pl.pallas_call(
        functools.partial(_diff_body, seq=seq, tq=tq, lam_init=lam_init),
        grid=(n_seq, N_HEADS),
        in_specs=[col(COL_D_Q), col(COL_D_K), col(COL_D_V), cache, cache,
                  pl.BlockSpec((3, seq, HEAD_W), lambda b, h: (0, 0, 0)),
                  vec, vec, vec, pl.BlockSpec((4, DF_DQK), lambda b, h: (0, 0))],
        out_specs=pl.BlockSpec((seq, HEAD_W), lambda b, h: (b, h)),
        out_shape=jax.ShapeDtypeStruct((n_seq * seq, GROUP_W), BF16),
        scratch_shapes=[pltpu.VMEM((seq, HEAD_W), BF16), kv_scratch, kv_scratch, kv_scratch],
        compiler_params=_params("parallel", "parallel"),
        name="latent_diff_attention",
    )(proj, proj, proj, cache_k2, cache_v, rope, jnp.tile(df_qn, 2)[None, :], jnp.tile(df_kn, 2)[None, :],
      df_subln[None, :], df_lam)


def _first_max(vals):
    best = vals[0]
    idx = jnp.zeros(best.shape, jnp.int32)
    for i in range(1, len(vals)):
        better = vals[i] > best
        best = jnp.where(better, vals[i], best)
        idx = jnp.where(better, i, idx)
    return best, idx


def _pick(vals, idx):
    out = vals[0]
    for i in range(1, len(vals)):
        out = jnp.where(idx == i, vals[i], out)
    return out


def _outproj_body(mix_ref, w_ref, h_ref, mod_ref, n2_ref, rw_ref, rb_ref, h1_ref, x2_ref, idx_ref, gate_ref):
    h1 = h_ref[...] + mod_ref[2:3, :] * _dot(mix_ref[...], w_ref[...])
    h1_ref[...] = h1
    x2 = _rms(h1, n2_ref[...], D_MODEL) * (1.0 + mod_ref[4:5, :]) + mod_ref[3:4, :]
    x2_ref[...] = x2.astype(BF16)
    logits = lax.dot_general(rw_ref[...], x2, (((1,), (1,)), ((), ())), precision=lax.Precision.HIGHEST,
                             preferred_element_type=F32)
    aff_all = _sigmoid(logits)
    sel_all = aff_all + rb_ref[...]
    aff = [aff_all[e:e + 1, :] for e in range(N_EXPERTS)]
    sel = [sel_all[e:e + 1, :] for e in range(N_EXPERTS)]
    neg = jnp.full(sel[0].shape, -jnp.inf, F32)
    scores = []
    for g in range(N_EXP_GROUPS):
        grp = sel[g * EXP_PER_GROUP:(g + 1) * EXP_PER_GROUP]
        m1, i1 = _first_max(grp)
        m2, _ = _first_max([jnp.where(i1 == j, neg, grp[j]) for j in range(EXP_PER_GROUP)])
        scores.append(m1 + m2)
    _, g_best = _first_max(scores)
    in_sel = [_pick([sel[g * EXP_PER_GROUP + j] for g in range(N_EXP_GROUPS)], g_best)
              for j in range(EXP_PER_GROUP)]
    in_aff = [_pick([aff[g * EXP_PER_GROUP + j] for g in range(N_EXP_GROUPS)], g_best)
              for j in range(EXP_PER_GROUP)]
    _, l1 = _first_max(in_sel)
    _, l2 = _first_max([jnp.where(l1 == j, neg, in_sel[j]) for j in range(EXP_PER_GROUP)])
    w1 = _pick(in_aff, l1)
    w2 = _pick(in_aff, l2)
    idx_ref[0:1, :] = g_best * EXP_PER_GROUP + l1
    idx_ref[1:2, :] = g_best * EXP_PER_GROUP + l2
    gate_ref[0:1, :] = w1 / (w1 + w2)
    gate_ref[1:2, :] = w2 / (w1 + w2)


def _output_projection(mixed, w_out_bf16, h, mod, norm2, router_w, router_b, cond_of_tile, tm):
    t, d = h.shape
    return pl.pallas_call(
        _outproj_body,
        grid=(t // tm,),
        in_specs=[pl.BlockSpec((tm, d), lambda i: (i, 0)),
                  pl.BlockSpec((d, d), lambda i: (0, 0)),
                  pl.BlockSpec((tm, d), lambda i: (i, 0)),
                  pl.BlockSpec((None, 6, d), lambda i: (cond_of_tile(i), 0, 0)),
                  pl.BlockSpec((1, d), lambda i: (0, 0)),
                  pl.BlockSpec((N_EXPERTS, d), lambda i: (0, 0)),
                  pl.BlockSpec((N_EXPERTS, 1), lambda i: (0, 0))],
        out_specs=[pl.BlockSpec((tm, d), lambda i: (i, 0)),
                   pl.BlockSpec((tm, d), lambda i: (i, 0)),
                   pl.BlockSpec((2, tm), lambda i: (0, i)),
                   pl.BlockSpec((2, tm), lambda i: (0, i))],
        out_shape=[jax.ShapeDtypeStruct((t, d), F32), jax.ShapeDtypeStruct((t, d), BF16),
                   jax.ShapeDtypeStruct((2, t), jnp.int32), jax.ShapeDtypeStruct((2, t), F32)],
        compiler_params=_params("parallel"),
        name="out_proj_residual_router",
    )(mixed, w_out_bf16, h, mod, norm2[None, :], router_w.T, router_b[:, None])


def _expert_body(be_ref, nb_ref, x_ref, wg_ref, wu_ref, wd_ref, gate_ref, o_ref):
    del be_ref

    @pl.when(pl.program_id(0) < nb_ref[0])
    def _():
        x = x_ref[...]
        hdn = _silu(_dot(x, wg_ref[...])) * _dot(x, wu_ref[...])
        o_ref[...] = _dot(hdn.astype(BF16), wd_ref[...]) * gate_ref[...]


def _expert_blocks(xs, gate_rows, block_expert, n_used, w_gate, w_up, w_down):
    n_rows, d = xs.shape
    ff = w_gate.shape[-1]
    n_blocks = n_rows // MOE_ROWS
    grid_spec = pltpu.PrefetchScalarGridSpec(
        num_scalar_prefetch=2,
        grid=(n_blocks,),
        in_specs=[pl.BlockSpec((MOE_ROWS, d), lambda i, be, nb: (i, 0)),
                  pl.BlockSpec((None, d, ff), lambda i, be, nb: (be[i], 0, 0)),
                  pl.BlockSpec((None, d, ff), lambda i, be, nb: (be[i], 0, 0)),
                  pl.BlockSpec((None, ff, d), lambda i, be, nb: (be[i], 0, 0)),
                  pl.BlockSpec((MOE_ROWS, 1), lambda i, be, nb: (i, 0))],
        out_specs=pl.BlockSpec((MOE_ROWS, d), lambda i, be, nb: (i, 0)),
    )
    return pl.pallas_call(
        _expert_body,
        grid_spec=grid_spec,
        out_shape=jax.ShapeDtypeStruct((n_rows, d), F32),
        compiler_params=_params("arbitrary"),
        name="moe_expert_blocks",
    )(block_expert, n_used, xs, w_gate, w_up, w_down, gate_rows)


def _combine_body(h_ref, y0_ref, y1_ref, mod_ref, o_ref):
    o_ref[...] = h_ref[...] + mod_ref[5:6, :] * (y0_ref[...] + y1_ref[...])


def _combine(h1, y0, y1, mod, cond_of_tile, tm):
    t, d = h1.shape
    tile = pl.BlockSpec((tm, d), lambda i: (i, 0))
    return pl.pallas_call(
        _combine_body,
        grid=(t // tm,),
        in_specs=[tile, tile, tile, pl.BlockSpec((None, 6, d), lambda i: (cond_of_tile(i), 0, 0))],
        out_specs=tile,
        out_shape=jax.ShapeDtypeStruct((t, d), F32),
        compiler_params=_params("parallel"),
        name="moe_gated_residual",
    )(h1, y0, y1, mod)


def _moe(h1, x2, idx_t, gate_t, mod, w_gate, w_up, w_down, cond_of_tile, tm):
    t, d = h1.shape
    n = 2 * t
    experts = idx_t.T.reshape(n)
    gates = gate_t.T.reshape(n)
    onehot = (experts[:, None] == jnp.arange(N_EXPERTS, dtype=jnp.int32)[None, :]).astype(jnp.int32)
    csum = jnp.cumsum(onehot, axis=0)
    counts = csum[-1]
    rank = jnp.sum((csum - onehot) * onehot, axis=1)
    padded = (counts + MOE_ROWS - 1) // MOE_ROWS * MOE_ROWS
    pad_end = jnp.cumsum(padded)
    dest = (pad_end - padded)[experts] + rank
    n_blocks = (n + N_EXPERTS * (MOE_ROWS - 1) + MOE_ROWS - 1) // MOE_ROWS
    n_rows = n_blocks * MOE_ROWS
    tok_of_row = jnp.zeros((n_rows,), jnp.int32).at[dest].set(jnp.arange(n, dtype=jnp.int32) // 2)
    gate_rows = jnp.zeros((n_rows,), F32).at[dest].set(gates)
    block_expert = jnp.minimum(
        jnp.searchsorted(pad_end, jnp.arange(n_blocks, dtype=jnp.int32) * MOE_ROWS, side='right'),
        N_EXPERTS - 1).astype(jnp.int32)
    n_used = (pad_end[-1:] // MOE_ROWS).astype(jnp.int32)
    yb = _expert_blocks(x2[tok_of_row], gate_rows[:, None], block_expert, n_used, w_gate, w_up, w_down)
    dest2 = dest.reshape(t, 2)
    return _combine(h1, yb[dest2[:, 0]], yb[dest2[:, 1]], mod, cond_of_tile, tm)


def kernel(x_prompt, x_sample, cache_na_k, cache_na_v, cache_gqa_k, cache_gqa_v, cache_diff_k, cache_diff_v, state_hgrn, c, c_ctx, w_mod, b_mod, norm1, norm2, w_in, w_out, hg_lb_logits, hg_onorm, na_qn, na_kn, na_rpb, gqa_qn, gqa_kn, df_qn, df_kn, df_lam, df_subln, router_w, router_b, w_gate, w_up, w_down):
    n_ctx, ctx_len, d = x_prompt.shape
    n_lat, lat_len, _ = x_sample.shape
    depth = w_in.shape[0]
    t_ctx = n_ctx * ctx_len
    t_all = t_ctx + n_lat * lat_len
    assert t_ctx % lat_len == 0 and lat_len % GRID_W == 0 and lat_len // GRID_W >= WIN_ROWS
    tm = next(m for m in (1024, 512, 256) if t_ctx % m == 0 and lat_len % m == 0)
    tm2 = min(tm, 512)
    lat_block0 = t_ctx // lat_len

    def cond_tile(tile_rows):
        def cond_of_tile(i):
            return jnp.where(i < t_ctx // tile_rows, 0, 1 + (i - t_ctx // tile_rows) // (lat_len // tile_rows))
        return cond_of_tile

    sm = jax.nn.softmax(hg_lb_logits.astype(F32), axis=0)
    lower = jnp.cumsum(sm, axis=0) - sm[0:1]
    mod_all = _modulation(jnp.concatenate([c_ctx[None, :], c], axis=0), w_mod, b_mod)
    mod_all = mod_all.reshape(depth, 1 + n_lat, 6, d)
    hgrn_consts = _hgrn_constants(HGRN_CHUNK)
    rope_c = _rope_tables(lat_len, HEAD_W)
    rope_d = _rope_tables(lat_len, DF_DQK)
    past = cache_diff_k.shape[4]
    cache_diff_k2 = cache_diff_k.transpose(0, 1, 2, 4, 3, 5).reshape(n_lat, depth, N_HEADS, past, HEAD_W)

    h = jnp.concatenate([x_prompt.reshape(t_ctx, d), x_sample.reshape(n_lat * lat_len, d)], axis=0)
    new_caches = [[] for _ in range(7)]
    for layer in range(depth):
        mod = mod_all[layer]
        lam_init = 0.8 - 0.6 * math.exp(-0.3 * layer)
        proj = _input_projection(h, mod, norm1[layer], w_in[layer].astype(BF16), cond_tile(tm), tm)
        a_ctx, st_ctx = _hgrn(proj, 0, n_ctx, ctx_len, lower[layer], hg_onorm[layer], hgrn_consts, None, True)
        gains = (na_qn[layer], na_kn[layer], gqa_qn[layer], gqa_kn[layer], df_qn[layer], df_kn[layer],
                 df_subln[layer], df_lam[layer])
        b_ctx, c_ctx_o, d_ctx, kb, vb, kc, vc, kd, vd = _context_attention(proj, n_ctx, ctx_len, gains, lam_init)
        kd = kd.reshape(n_ctx, N_HEADS, ctx_len, 2, DF_DQK).transpose(0, 1, 3, 2, 4)
        for store, tensor in zip(new_caches, (kb, vb, kc, vc, kd, vd, st_ctx)):
            store.append(tensor)
        a_lat, _ = _hgrn(proj, lat_block0, n_lat, lat_len, lower[layer], hg_onorm[layer], hgrn_consts,
                         state_hgrn[:, layer], False)
        b_lat = _latent_na(proj, lat_block0, n_lat, lat_len, cache_na_k, cache_na_v, layer,
                           _na_bias(na_rpb[layer], lat_len // GRID_W), na_qn[layer], na_kn[layer])
        c_lat = _latent_gqa(proj, lat_block0, n_lat, lat_len, cache_gqa_k, cache_gqa_v, layer, rope_c,
                            gqa_qn[layer], gqa_kn[layer])
        d_lat = _latent_diff(proj, lat_block0, n_lat, lat_len, cache_diff_k2, cache_diff_v, layer, rope_d,
                             df_qn[layer], df_kn[layer], df_subln[layer], df_lam[layer], lam_init)
        mixed = jnp.concatenate([jnp.concatenate([a_ctx, b_ctx, c_ctx_o, d_ctx], axis=1),
                                 jnp.concatenate([a_lat, b_lat, c_lat, d_lat], axis=1)], axis=0)
        h1, x2, idx_t, gate_t = _output_projection(mixed, w_out[layer].astype(BF16), h, mod, norm2[layer],
                                                   router_w, router_b, cond_tile(tm2), tm2)
        h = _moe(h1, x2, idx_t, gate_t, mod, w_gate[layer].astype(BF16), w_up[layer].astype(BF16),
                 w_down[layer].astype(BF16), cond_tile(tm2), tm2)
    y_prompt = h[:t_ctx].reshape(n_ctx, ctx_len, d)
    y_sample = h[t_ctx:].reshape(n_lat, lat_len, d)
    stacked = [jnp.stack(per_layer, axis=1) for per_layer in new_caches]
    return (y_prompt, y_sample, *stacked)
```

```python
import functools
import math

import numpy as np
import jax
import jax.numpy as jnp
from jax import lax
from jax.experimental import pallas as pl
from jax.experimental.pallas import tpu as pltpu

D_MODEL = 2048
GRID_W = 64
GROUP_W = D_MODEL // 4
N_HEADS = 4
HEAD_W = GROUP_W // N_HEADS
GQA_KV_HEADS = 2
DF_DQK = HEAD_W // 2
WIN_ROWS = 8
WIN_COLS = 16
N_EXPERTS = 16
N_EXP_GROUPS = 4
EXP_PER_GROUP = N_EXPERTS // N_EXP_GROUPS
EXPERT_FF = D_MODEL // 4
ROPE_THETA = 10000.0
EPS = 1e-6
NEG_INF = -1e30
IN_WIDTH = 13 * GROUP_W

COL_A_Q, COL_A_FF, COL_A_FB, COL_A_I, COL_A_G = 0, 4, 8, 12, 16
COL_B_Q, COL_B_K, COL_B_V = 20, 24, 28
COL_C_Q, COL_C_K, COL_C_V = 32, 36, 38
COL_D_Q, COL_D_K, COL_D_V = 40, 44, 48

HGRN_CHUNK = 64
MOE_ROWS = 256
VMEM_LIMIT = 48 * 1024 * 1024

F32 = jnp.float32
BF16 = jnp.bfloat16


def _params(*sem):
    return pltpu.CompilerParams(dimension_semantics=sem, vmem_limit_bytes=VMEM_LIMIT)


def _sigmoid(x):
    return 1.0 / (1.0 + jnp.exp(-x))


def _silu(x):
    return x * _sigmoid(x)


def _rms(x, gain, n):
    return x * lax.rsqrt(jnp.sum(x * x, axis=-1, keepdims=True) * (1.0 / n) + EPS) * gain


def _dot(a, b):
    return jnp.dot(a, b, preferred_element_type=F32)


def _dot_nt(a, b):
    return lax.dot_general(a, b, (((1,), (1,)), ((), ())), preferred_element_type=F32)


def _dot_tn(a, b):
    return lax.dot_general(a, b, (((0,), (0,)), ((), ())), preferred_element_type=F32)


def _mod_body(cond_ref, w_ref, b_ref, o_ref):
    w = w_ref[...]
    for c in range(cond_ref.shape[0]):
        s = _silu(cond_ref[c])
        o_ref[c:c + 1, :] = jnp.sum(w * s, axis=0, keepdims=True) + b_ref[...]


def _modulation(cond, w_mod, b_mod):
    depth, d, n6 = w_mod.shape
    nc = cond.shape[0]
    tn = 512
    return pl.pallas_call(
        _mod_body,
        grid=(depth, n6 // tn),
        in_specs=[pl.BlockSpec((nc, d, 1), lambda l, j: (0, 0, 0)),
                  pl.BlockSpec((None, d, tn), lambda l, j: (l, 0, j)),
                  pl.BlockSpec((None, 1, tn), lambda l, j: (l, 0, j))],
        out_specs=pl.BlockSpec((None, nc, tn), lambda l, j: (l, 0, j)),
        out_shape=jax.ShapeDtypeStruct((depth, nc, n6), F32),
        compiler_params=_params("parallel", "parallel"),
        name="adaln_modulation",
    )(cond[:, :, None], w_mod, b_mod[:, None, :])


def _inproj_body(h_ref, mod_ref, n1_ref, w_ref, o_ref, xn_ref):
    @pl.when(pl.program_id(1) == 0)
    def _():
        y = _rms(h_ref[...], n1_ref[...], D_MODEL)
        xn_ref[...] = (y * (1.0 + mod_ref[1:2, :]) + mod_ref[0:1, :]).astype(BF16)

    o_ref[...] = _dot(xn_ref[...], w_ref[...])


def _input_projection(h, mod, norm1, w_in_bf16, cond_of_tile, tm):
    t, d = h.shape
    n = w_in_bf16.shape[1]
    tn = 512
    return pl.pallas_call(
        _inproj_body,
        grid=(t // tm, n // tn),
        in_specs=[pl.BlockSpec((tm, d), lambda i, j: (i, 0)),
                  pl.BlockSpec((None, 6, d), lambda i, j: (cond_of_tile(i), 0, 0)),
                  pl.BlockSpec((1, d), lambda i, j: (0, 0)),
                  pl.BlockSpec((d, tn), lambda i, j: (0, j))],
        out_specs=pl.BlockSpec((tm, tn), lambda i, j: (i, j)),
        out_shape=jax.ShapeDtypeStruct((t, n), F32),
        scratch_shapes=[pltpu.VMEM((tm, d), BF16)],
        compiler_params=_params("parallel", "arbitrary"),
        name="norm_modulate_in_proj",
    )(h, mod, norm1[None, :], w_in_bf16)


def _hgrn_constants(c):
    nl = int(math.log2(c))
    idx = np.arange(c)
    e = np.zeros((nl + 2, c, c), np.float32)
    m = np.zeros((nl + 1, c, c), np.float32)
    e[0] = idx[None, :] <= idx[:, None]
    e[1] = idx[None, :] > idx[:, None]
    m[0] = np.eye(c)
    for li in range(nl):
        s = c >> (li + 1)
        parent = idx // (2 * s)
        right = (idx % (2 * s)) >= s
        ref = parent * 2 * s + s - 1
        for i in range(c):
            if right[i]:
                e[2 + li, i, ref[i] + 1:i + 1] = 1.0
            else:
                e[2 + li, i, i + 1:ref[i] + 1] = 1.0
        m[1 + li] = right[:, None] & ~right[None, :] & (parent[:, None] == parent[None, :])
    e2 = np.stack([e, e[:, ::-1, ::-1]]).reshape(2, (nl + 2) * c, c)
    m2 = np.stack([m, m[:, ::-1, ::-1]])
    return jnp.asarray(e2, BF16), jnp.asarray(m2, F32)


def _hgrn_body(*refs, seq, chunk, has_s0, emit_state):
    q_ref, ff_ref, fb_ref, i_ref, g_ref, lb_ref, on_ref, e_ref, m_ref = refs[:9]
    pos = 9
    s0_ref = None
    if has_s0:
        s0_ref = refs[pos]
        pos += 1
    o_ref = refs[pos]
    pos += 1
    if emit_state:
        st_ref = refs[pos]
        pos += 1
    of_ref, ob_ref = refs[pos], refs[pos + 1]
    c = chunk
    n_chunks = seq // c
    n_levels = m_ref.shape[1] - 1
    gate_refs = (ff_ref, fb_ref)
    out_refs = (of_ref, ob_ref)

    def chunk_step(c0, d, st):
        rows = pl.ds(c0, c)
        lb = lb_ref[d:d + 1, :]
        f = lb + (1.0 - lb) * _sigmoid(gate_refs[d][rows, :])
        g = jnp.log(f)
        k = 1.0 - f
        q = _silu(q_ref[rows, :])
        v = i_ref[rows, :].astype(BF16)
        g_hi = g.astype(BF16)
        r1 = g - g_hi.astype(F32)
        g_mid = r1.astype(BF16)
        g_lo = (r1 - g_mid.astype(F32)).astype(BF16)
        g3 = _dot(e_ref[d], jnp.concatenate([g_hi, g_mid, g_lo], axis=1))
        x = jnp.exp(g3[:, 0:HEAD_W] + g3[:, HEAD_W:2 * HEAD_W] + g3[:, 2 * HEAD_W:3 * HEAD_W])
        x_cum = x[0:c]
        x_tail = x[c:2 * c]
        s = m_ref[d, 0] * _dot_nt(q.astype(BF16), k.astype(BF16))
        for lv in range(n_levels):
            x_l = x[(2 + lv) * c:(3 + lv) * c]
            s = s + m_ref[d, 1 + lv] * _dot_nt((q * x_l).astype(BF16), (k * x_l).astype(BF16))
        o = _dot_nt((q * x_cum).astype(BF16), st.astype(BF16)) + _dot(s.astype(BF16), v)
        out_refs[d][rows, :] = o
        total = x_cum[c - 1:c, :] if d == 0 else x_cum[0:1, :]
        return st * total + _dot_tn(v, (k * x_tail).astype(BF16))

    if has_s0:
        st_f0 = s0_ref[0].T
        st_b0 = s0_ref[1].T
    else:
        st_f0 = jnp.zeros((HEAD_W, HEAD_W), F32)
        st_b0 = jnp.zeros((HEAD_W, HEAD_W), F32)

    def loop(t, carry):
        st_f, st_b = carry
        st_f = chunk_step(pl.multiple_of(t * c, c), 0, st_f)
        st_b = chunk_step(pl.multiple_of((n_chunks - 1 - t) * c, c), 1, st_b)
        return st_f, st_b

    st_f, st_b = lax.fori_loop(0, n_chunks, loop, (st_f0, st_b0))
    o = of_ref[...] + ob_ref[...]
    o_ref[...] = (_rms(o, on_ref[...], HEAD_W) * _silu(g_ref[...])).astype(o_ref.dtype)
    if emit_state:
        st_ref[0] = st_f.T
        st_ref[1] = st_b.T


def _hgrn(proj, row_block0, n_seq, seq, lower, onorm, consts, s0, emit_state):
    e_mat, masks = consts

    def col(cb):
        return pl.BlockSpec((seq, HEAD_W), lambda b, h, cb=cb: (row_block0 + b, cb + h))

    in_specs = [col(COL_A_Q), col(COL_A_FF), col(COL_A_FB), col(COL_A_I), col(COL_A_G),
                pl.BlockSpec((2, HEAD_W), lambda b, h: (0, h)),
                pl.BlockSpec((1, HEAD_W), lambda b, h: (0, 0)),
                pl.BlockSpec(e_mat.shape, lambda b, h: (0, 0, 0)),
                pl.BlockSpec(masks.shape, lambda b, h: (0, 0, 0, 0))]
    args = [proj, proj, proj, proj, proj, lower, onorm[None, :], e_mat, masks]
    if s0 is not None:
        in_specs.append(pl.BlockSpec((None, 2, None, HEAD_W, HEAD_W), lambda b, h: (b, 0, h, 0, 0)))
        args.append(s0)
    out_specs = [pl.BlockSpec((seq, HEAD_W), lambda b, h: (b, h))]
    out_shape = [jax.ShapeDtypeStruct((n_seq * seq, GROUP_W), BF16)]
    if emit_state:
        out_specs.append(pl.BlockSpec((None, 2, None, HEAD_W, HEAD_W), lambda b, h: (b, 0, h, 0, 0)))
        out_shape.append(jax.ShapeDtypeStruct((n_seq, 2, N_HEADS, HEAD_W, HEAD_W), F32))
    res = pl.pallas_call(
        functools.partial(_hgrn_body, seq=seq, chunk=HGRN_CHUNK, has_s0=s0 is not None,
                          emit_state=emit_state),
        grid=(n_seq, N_HEADS),
        in_specs=in_specs, out_specs=out_specs, out_shape=out_shape,
        scratch_shapes=[pltpu.VMEM((seq, HEAD_W), F32), pltpu.VMEM((seq, HEAD_W), F32)],
        compiler_params=_params("parallel", "parallel"),
        name="hgrn2_latent" if s0 is not None else "hgrn2_context",
    )(*args)
    return res if emit_state else (res[0], None)


def _softmax_pv(scores, values):
    mx = functools.reduce(jnp.maximum, [jnp.max(s, axis=-1, keepdims=True) for s in scores])
    es = [jnp.exp(s - mx) for s in scores]
    den = functools.reduce(lambda a, b: a + b, [jnp.sum(e, axis=-1, keepdims=True) for e in es])
    num = functools.reduce(lambda a, b: a + b, [_dot(e.astype(BF16), v) for e, v in zip(es, values)])
    return num / den


def _diff_weights(s0, s1, lam):
    e0 = jnp.exp(s0 - jnp.max(s0, axis=-1, keepdims=True))
    e1 = jnp.exp(s1 - jnp.max(s1, axis=-1, keepdims=True))
    p0 = e0 / jnp.sum(e0, axis=-1, keepdims=True)
    p1 = e1 / jnp.sum(e1, axis=-1, keepdims=True)
    return (p0 - lam * p1).astype(BF16)


def _lane_lt(shape, n):
    return lax.broadcasted_iota(jnp.int32, shape, len(shape) - 1) < n


def _rms_halves(x, gain2):
    lo = _lane_lt(x.shape, DF_DQK)
    sq = x * x
    ss_lo = jnp.sum(jnp.where(lo, sq, 0.0), axis=-1, keepdims=True)
    ss_hi = jnp.sum(sq, axis=-1, keepdims=True) - ss_lo
    inv = jnp.where(lo, lax.rsqrt(ss_lo * (1.0 / DF_DQK) + EPS), lax.rsqrt(ss_hi * (1.0 / DF_DQK) + EPS))
    return x * inv * gain2


def _lambda(lam_ref, lam_init):
    l = lam_ref[...]
    return (jnp.exp(jnp.sum(l[0:1] * l[1:2], axis=-1, keepdims=True))
            - jnp.exp(jnp.sum(l[2:3] * l[3:4], axis=-1, keepdims=True)) + lam_init)


def _ctx_attn_body(bq_ref, bk_ref, bv_ref, cq_ref, ck_ref, cv_ref, dq_ref, dk_ref, dv_ref,
                   naq_ref, nak_ref, gq_ref, gk_ref, dfq_ref, dfk_ref, sub_ref, lam_ref,
                   ob_ref, oc_ref, od_ref, kb_ref, vb_ref, kc_ref, vc_ref, kd_ref, vd_ref, *, lam_init):
    scale = HEAD_W ** -0.5
    kb = _rms(bk_ref[...], nak_ref[...], HEAD_W)
    vb = bv_ref[...]
    kb_ref[...] = kb
    vb_ref[...] = vb
    qb = (_rms(bq_ref[...], naq_ref[...], HEAD_W) * scale).astype(BF16)
    ob_ref[...] = _softmax_pv([_dot_nt(qb, kb.astype(BF16))], [vb.astype(BF16)]).astype(ob_ref.dtype)
    kc = _rms(ck_ref[...], gk_ref[...], HEAD_W)
    vc = cv_ref[...]
    kc_ref[...] = kc
    vc_ref[...] = vc
    qc = (_rms(cq_ref[...], gq_ref[...], HEAD_W) * scale).astype(BF16)
    oc_ref[...] = _softmax_pv([_dot_nt(qc, kc.astype(BF16))], [vc.astype(BF16)]).astype(oc_ref.dtype)
    kd = _rms_halves(dk_ref[...], dfk_ref[...])
    vd = dv_ref[...]
    kd_ref[...] = kd
    vd_ref[...] = vd
    qd = (_rms_halves(dq_ref[...], dfq_ref[...]) * (DF_DQK ** -0.5)).astype(BF16)
    lo = _lane_lt(kd.shape, DF_DQK)
    s0 = _dot_nt(qd, jnp.where(lo, kd, 0.0).astype(BF16))
    s1 = _dot_nt(qd, jnp.where(lo, 0.0, kd).astype(BF16))
    od = _dot(_diff_weights(s0, s1, _lambda(lam_ref, lam_init)), vd.astype(BF16))
    od_ref[...] = (_rms(od, sub_ref[...], HEAD_W) * (1.0 - lam_init)).astype(od_ref.dtype)


def _context_attention(proj, n_seq, seq, gains, lam_init):
    na_qn, na_kn, gqa_qn, gqa_kn, df_qn, df_kn, df_subln, df_lam = gains

    def col(cb, div=1):
        return pl.BlockSpec((seq, HEAD_W), lambda b, h, cb=cb, div=div: (b, cb + h // div))

    def vec(width=HEAD_W):
        return pl.BlockSpec((1, width), lambda b, h: (0, 0))

    def cache(div=1):
        return pl.BlockSpec((None, None, seq, HEAD_W), lambda b, h, div=div: (b, h // div, 0, 0))

    mixed = pl.BlockSpec((seq, HEAD_W), lambda b, h: (b, h))
    mixed_shape = jax.ShapeDtypeStruct((n_seq * seq, GROUP_W), BF16)
    cache4 = jax.ShapeDtypeStruct((n_seq, N_HEADS, seq, HEAD_W), F32)
    cache2 = jax.ShapeDtypeStruct((n_seq, GQA_KV_HEADS, seq, HEAD_W), F32)
    group = N_HEADS // GQA_KV_HEADS
    return pl.pallas_call(
        functools.partial(_ctx_attn_body, lam_init=lam_init),
        grid=(n_seq, N_HEADS),
        in_specs=[col(COL_B_Q), col(COL_B_K), col(COL_B_V),
                  col(COL_C_Q), col(COL_C_K, group), col(COL_C_V, group),
                  col(COL_D_Q), col(COL_D_K), col(COL_D_V),
                  vec(), vec(), vec(), vec(), vec(), vec(), vec(),
                  pl.BlockSpec((4, DF_DQK), lambda b, h: (0, 0))],
        out_specs=[mixed, mixed, mixed, cache(), cache(), cache(group), cache(group), cache(), cache()],
        out_shape=[mixed_shape, mixed_shape, mixed_shape, cache4, cache4, cache2, cache2, cache4, cache4],
        compiler_params=_params("parallel", "arbitrary"),
        name="context_attention",
    )(proj, proj, proj, proj, proj, proj, proj, proj, proj,
      na_qn[None, :], na_kn[None, :], gqa_qn[None, :], gqa_kn[None, :],
      jnp.tile(df_qn, 2)[None, :], jnp.tile(df_kn, 2)[None, :], df_subln[None, :], df_lam)


def _rope_tables(n_tokens, rot_dim):
    t = np.arange(n_tokens)
    row = (t // GRID_W).astype(np.float32)
    col = (t % GRID_W).astype(np.float32)
    n_freq = rot_dim // 4
    inv = (np.float32(ROPE_THETA) ** (-np.arange(n_freq, dtype=np.float32) / np.float32(n_freq))).astype(np.float32)
    ang = np.concatenate([row[:, None] * inv, col[:, None] * inv], axis=-1).astype(np.float32)
    cos, sin, zero = np.cos(ang), np.sin(ang), np.zeros_like(ang)
    reps = HEAD_W // rot_dim
    a = np.tile(np.concatenate([cos, cos], axis=-1), (1, reps))
    b = np.tile(np.concatenate([-sin, zero], axis=-1), (1, reps))
    c = np.tile(np.concatenate([zero, sin], axis=-1), (1, reps))
    return jnp.asarray(np.stack([a, b, c]), F32)


def _rope(x, tab_ref, half):
    return (x * tab_ref[0] + pltpu.roll(x, HEAD_W - half, 1) * tab_ref[1]
            + pltpu.roll(x, half, 1) * tab_ref[2])


def _na_body(q_ref, k_ref, v_ref, ck_ref, cv_ref, bias_ref, qn_ref, kn_ref, o_ref, qs_ref, ks_ref, vs_ref,
             *, seq):
    rows = seq // GRID_W
    n_win = WIN_ROWS * GRID_W
    qs_ref[...] = (_rms(q_ref[...], qn_ref[...], HEAD_W) * (HEAD_W ** -0.5)).astype(BF16)
    ks_ref[...] = _rms(k_ref[...], kn_ref[...], HEAD_W).astype(BF16)
    vs_ref[...] = v_ref[...].astype(BF16)
    ck = ck_ref[...].astype(BF16)
    cv = cv_ref[...].astype(BF16)

    def row_step(r, carry):
        start = jnp.clip(r - WIN_ROWS // 2, 0, rows - WIN_ROWS)
        win = pl.ds(pl.multiple_of(start * GRID_W, GRID_W), n_win)
        qrows = pl.ds(pl.multiple_of(r * GRID_W, GRID_W), GRID_W)
        q = qs_ref[qrows, :]
        s_win = _dot_nt(q, ks_ref[win, :]) + bias_ref[r]
        s_ctx = _dot_nt(q, ck)
        o_ref[qrows, :] = _softmax_pv([s_win, s_ctx], [vs_ref[win, :], cv]).astype(o_ref.dtype)
        return carry

    lax.fori_loop(0, rows, row_step, 0)


def _na_bias(rpb, rows):
    r = np.arange(rows)
    row_idx = np.clip(r - WIN_ROWS // 2, 0, rows - WIN_ROWS)[:, None] + np.arange(WIN_ROWS)[None, :]
    col = np.arange(GRID_W)
    col_start = np.clip(col - WIN_COLS // 2, 0, GRID_W - WIN_COLS)
    col_ok = (col[None, :] >= col_start[:, None]) & (col[None, :] < col_start[:, None] + WIN_COLS)
    dr_idx = (row_idx - r[:, None] + WIN_ROWS - 1)[:, None, :, None]
    dc_idx = np.clip(col[None, :] - col[:, None] + WIN_COLS - 1, 0, 2 * WIN_COLS - 2)[None, :, None, :]
    bias = rpb[:, dr_idx, dc_idx].astype(F32)
    bias = jnp.where(col_ok[None, None, :, None, :], bias, NEG_INF)
    return bias.reshape(rpb.shape[0], rows, GRID_W, WIN_ROWS * GRID_W)


def _latent_na(proj, row_block0, n_seq, seq, cache_k, cache_v, layer, bias, na_qn, na_kn):
    past = cache_k.shape[3]
    rows = seq // GRID_W

    def col(cb):
        return pl.BlockSpec((seq, HEAD_W), lambda b, h, cb=cb: (row_block0 + b, cb + h))

    cache = pl.BlockSpec((None, None, None, past, HEAD_W), lambda b, h: (b, layer, h, 0, 0))
    vec = pl.BlockSpec((1, HEAD_W), lambda b, h: (0, 0))
    return pl.pallas_call(
        functools.partial(_na_body, seq=seq),
        grid=(n_seq, N_HEADS),
        in_specs=[col(COL_B_Q), col(COL_B_K), col(COL_B_V), cache, cache,
                  pl.BlockSpec((None, rows, GRID_W, WIN_ROWS * GRID_W), lambda b, h: (h, 0, 0, 0)),
                  vec, vec],
        out_specs=pl.BlockSpec((seq, HEAD_W), lambda b, h: (b, h)),
        out_shape=jax.ShapeDtypeStruct((n_seq * seq, GROUP_W), BF16),
        scratch_shapes=[pltpu.VMEM((seq, HEAD_W), BF16)] * 3,
        compiler_params=_params("parallel", "parallel"),
        name="latent_neighbourhood_attention",
    )(proj, proj, proj, cache_k, cache_v, bias, na_qn[None, :], na_kn[None, :])


def _gqa_body(q_ref, k_ref, v_ref, ck_ref, cv_ref, rope_ref, qn_ref, kn_ref, o_ref, qs_ref, ks_ref, vs_ref,
              *, seq, tq):
    group = N_HEADS // GQA_KV_HEADS
    half = HEAD_W // 2
    ks_ref[0:seq, :] = _rope(_rms(k_ref[...], kn_ref[...], HEAD_W), rope_ref, half).astype(BF16)
    ks_ref[seq:, :] = ck_ref[...].astype(BF16)
    vs_ref[0:seq, :] = v_ref[...].astype(BF16)
    vs_ref[seq:, :] = cv_ref[...].astype(BF16)
    for g in range(group):
        q = _rms(q_ref[:, g * HEAD_W:(g + 1) * HEAD_W], qn_ref[...], HEAD_W) * (HEAD_W ** -0.5)
        qs_ref[g] = _rope(q, rope_ref, half).astype(BF16)
    kk = ks_ref[...]
    vv = vs_ref[...]
    for g in range(group):
        def q_step(i, carry, g=g):
            qrows = pl.ds(pl.multiple_of(i * tq, tq), tq)
            o = _softmax_pv([_dot_nt(qs_ref[g, qrows, :], kk)], [vv])
            o_ref[qrows, g * HEAD_W:(g + 1) * HEAD_W] = o.astype(o_ref.dtype)
            return carry

        lax.fori_loop(0, seq // tq, q_step, 0)


def _latent_gqa(proj, row_block0, n_seq, seq, cache_k, cache_v, layer, rope, gqa_qn, gqa_kn):
    past = cache_k.shape[3]
    group = N_HEADS // GQA_KV_HEADS
    tq = 256
    cache = pl.BlockSpec((None, None, None, past, HEAD_W), lambda b, n: (b, layer, n, 0, 0))
    vec = pl.BlockSpec((1, HEAD_W), lambda b, n: (0, 0))
    return pl.pallas_call(
        functools.partial(_gqa_body, seq=seq, tq=tq),
        grid=(n_seq, GQA_KV_HEADS),
        in_specs=[pl.BlockSpec((seq, group * HEAD_W), lambda b, n: (row_block0 + b, COL_C_Q // group + n)),
                  pl.BlockSpec((seq, HEAD_W), lambda b, n: (row_block0 + b, COL_C_K + n)),
                  pl.BlockSpec((seq, HEAD_W), lambda b, n: (row_block0 + b, COL_C_V + n)),
                  cache, cache,
                  pl.BlockSpec((3, seq, HEAD_W), lambda b, n: (0, 0, 0)),
                  vec, vec],
        out_specs=pl.BlockSpec((seq, group * HEAD_W), lambda b, n: (b, n)),
        out_shape=jax.ShapeDtypeStruct((n_seq * seq, GROUP_W), BF16),
        scratch_shapes=[pltpu.VMEM((group, seq, HEAD_W), BF16),
                        pltpu.VMEM((seq + past, HEAD_W), BF16),
                        pltpu.VMEM((seq + past, HEAD_W), BF16)],
        compiler_params=_params("parallel", "parallel"),
        name="latent_gqa_attention",
    )(proj, proj, proj, cache_k, cache_v, rope, gqa_qn[None, :], gqa_kn[None, :])


def _diff_body(q_ref, k_ref, v_ref, ck_ref, cv_ref, rope_ref, qn_ref, kn_ref, sub_ref, lam_ref, o_ref,
               qs_ref, k0_ref, k1_ref, vs_ref, *, seq, tq, lam_init):
    half = DF_DQK // 2
    k = _rope(_rms_halves(k_ref[...], kn_ref[...]), rope_ref, half)
    lo = _lane_lt(k.shape, DF_DQK)
    k0_ref[0:seq, :] = jnp.where(lo, k, 0.0).astype(BF16)
    k1_ref[0:seq, :] = jnp.where(lo, 0.0, k).astype(BF16)
    ck = ck_ref[...]
    lo_c = _lane_lt(ck.shape, DF_DQK)
    k0_ref[seq:, :] = jnp.where(lo_c, ck, 0.0).astype(BF16)
    k1_ref[seq:, :] = jnp.where(lo_c, 0.0, ck).astype(BF16)
    vs_ref[0:seq, :] = v_ref[...].astype(BF16)
    vs_ref[seq:, :] = cv_ref[...].astype(BF16)
    q = _rms_halves(q_ref[...], qn_ref[...]) * (DF_DQK ** -0.5)
    qs_ref[...] = _rope(q, rope_ref, half).astype(BF16)
    lam = _lambda(lam_ref, lam_init)
    k0 = k0_ref[...]
    k1 = k1_ref[...]
    vv = vs_ref[...]

    def q_step(i, carry):
        qrows = pl.ds(pl.multiple_of(i * tq, tq), tq)
        qb = qs_ref[qrows, :]
        o = _dot(_diff_weights(_dot_nt(qb, k0), _dot_nt(qb, k1), lam), vv)
        o_ref[qrows, :] = (_rms(o, sub_ref[...], HEAD_W) * (1.0 - lam_init)).astype(o_ref.dtype)
        return carry

    lax.fori_loop(0, seq // tq, q_step, 0)


def _latent_diff(proj, row_block0, n_seq, seq, cache_k2, cache_v, layer, rope, df_qn, df_kn, df_subln, df_lam,
                 lam_init):
    past = cache_k2.shape[3]
    tq = 256

    def col(cb):
        return pl.BlockSpec((seq, HEAD_W), lambda b, h, cb=cb: (row_block0 + b, cb + h))

    cache = pl.BlockSpec((None, None, None, past, HEAD_W), lambda b, h: (b, layer, h, 0, 0))
    vec = pl.BlockSpec((1, HEAD_W), lambda b, h: (0, 0))
    kv_scratch = pltpu.VMEM((seq + past, HEAD_W), BF16)
    return pl.pallas_call(
        functools.partial(_diff_body, seq=seq, tq=tq, lam_init=lam_init),
        grid=(n_seq, N_HEADS),
        in_specs=[col(COL_D_Q), col(COL_D_K), col(COL_D_V), cache, cache,
                  pl.BlockSpec((3, seq, HEAD_W), lambda b, h: (0, 0, 0)),
                  vec, vec, vec, pl.BlockSpec((4, DF_DQK), lambda b, h: (0, 0))],
        out_specs=pl.BlockSpec((seq, HEAD_W), lambda b, h: (b, h)),
        out_shape=jax.ShapeDtypeStruct((n_seq * seq, GROUP_W), BF16),
        scratch_shapes=[pltpu.VMEM((seq, HEAD_W), BF16), kv_scratch, kv_scratch, kv_scratch],
        compiler_params=_params("parallel", "parallel"),
        name="latent_diff_attention",
    )(proj, proj, proj, cache_k2, cache_v, rope, jnp.tile(df_qn, 2)[None, :], jnp.tile(df_kn, 2)[None, :],
      df_subln[None, :], df_lam)


def _first_max(vals):
    best = vals[0]
    idx = jnp.zeros(best.shape, jnp.int32)
    for i in range(1, len(vals)):
        better = vals[i] > best
        best = jnp.where(better, vals[i], best)
        idx = jnp.where(better, i, idx)
    return best, idx


def _pick(vals, idx):
    out = vals[0]
    for i in range(1, len(vals)):
        out = jnp.where(idx == i, vals[i], out)
    return out


def _outproj_body(mix_ref, w_ref, h_ref, mod_ref, n2_ref, rw_ref, rb_ref, h1_ref, x2_ref, idx_ref, gate_ref):
    h1 = h_ref[...] + mod_ref[2:3, :] * _dot(mix_ref[...], w_ref[...])
    h1_ref[...] = h1
    x2 = _rms(h1, n2_ref[...], D_MODEL) * (1.0 + mod_ref[4:5, :]) + mod_ref[3:4, :]
    x2_ref[...] = x2
    logits = lax.dot_general(rw_ref[...], x2, (((1,), (1,)), ((), ())), precision=lax.Precision.HIGHEST,
                             preferred_element_type=F32)
    aff_all = _sigmoid(logits)
    sel_all = aff_all + rb_ref[...]
    aff = [aff_all[e:e + 1, :] for e in range(N_EXPERTS)]
    sel = [sel_all[e:e + 1, :] for e in range(N_EXPERTS)]
    neg = jnp.full(sel[0].shape, -jnp.inf, F32)
    scores = []
    for g in range(N_EXP_GROUPS):
        grp = sel[g * EXP_PER_GROUP:(g + 1) * EXP_PER_GROUP]
        m1, i1 = _first_max(grp)
        m2, _ = _first_max([jnp.where(i1 == j, neg, grp[j]) for j in range(EXP_PER_GROUP)])
        scores.append(m1 + m2)
    _, g_best = _first_max(scores)
    in_sel = [_pick([sel[g * EXP_PER_GROUP + j] for g in range(N_EXP_GROUPS)], g_best)
              for j in range(EXP_PER_GROUP)]
    in_aff = [_pick([aff[g * EXP_PER_GROUP + j] for g in range(N_EXP_GROUPS)], g_best)
              for j in range(EXP_PER_GROUP)]
    _, l1 = _first_max(in_sel)
    _, l2 = _first_max([jnp.where(l1 == j, neg, in_sel[j]) for j in range(EXP_PER_GROUP)])
    w1 = _pick(in_aff, l1)
    w2 = _pick(in_aff, l2)
    idx_ref[0:1, :] = g_best * EXP_PER_GROUP + l1
    idx_ref[1:2, :] = g_best * EXP_PER_GROUP + l2
    gate_ref[0:1, :] = w1 / (w1 + w2)
    gate_ref[1:2, :] = w2 / (w1 + w2)


def _output_projection(mixed, w_out_bf16, h, mod, norm2, router_w, router_b, cond_of_tile, tm):
    t, d = h.shape
    return pl.pallas_call(
        _outproj_body,
        grid=(t // tm,),
        in_specs=[pl.BlockSpec((tm, d), lambda i: (i, 0)),
                  pl.BlockSpec((d, d), lambda i: (0, 0)),
                  pl.BlockSpec((tm, d), lambda i: (i, 0)),
                  pl.BlockSpec((None, 6, d), lambda i: (cond_of_tile(i), 0, 0)),
                  pl.BlockSpec((1, d), lambda i: (0, 0)),
                  pl.BlockSpec((N_EXPERTS, d), lambda i: (0, 0)),
                  pl.BlockSpec((N_EXPERTS, 1), lambda i: (0, 0))],
        out_specs=[pl.BlockSpec((tm, d), lambda i: (i, 0)),
                   pl.BlockSpec((tm, d), lambda i: (i, 0)),
                   pl.BlockSpec((2, tm), lambda i: (0, i)),
                   pl.BlockSpec((2, tm), lambda i: (0, i))],
        out_shape=[jax.ShapeDtypeStruct((t, d), F32), jax.ShapeDtypeStruct((t, d), F32),
                   jax.ShapeDtypeStruct((2, t), jnp.int32), jax.ShapeDtypeStruct((2, t), F32)],
        compiler_params=_params("parallel"),
        name="out_proj_residual_router",
    )(mixed, w_out_bf16, h, mod, norm2[None, :], router_w.T, router_b[:, None])


def _row_gather(src_hbm, dst, sem, n_rows, row_of):
    def body(r, carry):
        pltpu.make_async_copy(src_hbm.at[pl.ds(row_of(r), 1), :], dst.at[pl.ds(r, 1), :], sem).start()
        return carry

    lax.fori_loop(0, n_rows, body, 0)


def _row_gather_wait(src_hbm, dst, sem, n_rows):
    pltpu.make_async_copy(src_hbm.at[pl.ds(0, n_rows), :], dst, sem).wait()


def _expert_body(be_ref, nb_ref, tok_ref, x_hbm, wg_ref, wu_ref, wd_ref, gate_ref, o_ref, xbuf, sem):
    del be_ref
    i = pl.program_id(0)
    n_used = nb_ref[0]

    def start(blk, slot):
        _row_gather(x_hbm, xbuf.at[slot], sem.at[slot], MOE_ROWS, lambda r: tok_ref[blk * MOE_ROWS + r])

    @pl.when(i == 0)
    def _():
        start(0, 0)

    @pl.when(i + 1 < n_used)
    def _():
        start(i + 1, (i + 1) % 2)

    @pl.when(i < n_used)
    def _():
        slot = i % 2
        _row_gather_wait(x_hbm, xbuf.at[slot], sem.at[slot], MOE_ROWS)
        x = xbuf[slot].astype(BF16)
        hdn = _silu(_dot(x, wg_ref[...])) * _dot(x, wu_ref[...])
        o_ref[...] = _dot(hdn.astype(BF16), wd_ref[...]) * gate_ref[...]

    @pl.when(i >= n_used)
    def _():
        o_ref[...] = jnp.zeros(o_ref.shape, o_ref.dtype)


def _expert_blocks(x2, tok_of_row, gate_rows, block_expert, n_used, w_gate, w_up, w_down):
    d = x2.shape[1]
    n_rows = tok_of_row.shape[0]
    ff = w_gate.shape[-1]
    grid_spec = pltpu.PrefetchScalarGridSpec(
        num_scalar_prefetch=3,
        grid=(n_rows // MOE_ROWS,),
        in_specs=[pl.BlockSpec(memory_space=pl.ANY),
                  pl.BlockSpec((None, d, ff), lambda i, be, nb, tok: (be[i], 0, 0)),
                  pl.BlockSpec((None, d, ff), lambda i, be, nb, tok: (be[i], 0, 0)),
                  pl.BlockSpec((None, ff, d), lambda i, be, nb, tok: (be[i], 0, 0)),
                  pl.BlockSpec((MOE_ROWS, 1), lambda i, be, nb, tok: (i, 0))],
        out_specs=pl.BlockSpec((MOE_ROWS, d), lambda i, be, nb, tok: (i, 0)),
        scratch_shapes=[pltpu.VMEM((2, MOE_ROWS, d), F32), pltpu.SemaphoreType.DMA((2,))],
    )
    return pl.pallas_call(
        _expert_body,
        grid_spec=grid_spec,
        out_shape=jax.ShapeDtypeStruct((n_rows, d), F32),
        compiler_params=_params("arbitrary"),
        name="moe_expert_blocks",
    )(block_expert, n_used, tok_of_row, x2, w_gate, w_up, w_down, gate_rows)


def _combine_body(dest_ref, h_ref, y_hbm, mod_ref, o_ref, ybuf, sem, *, n_tok, tile):
    i = pl.program_id(0)
    n_tiles = pl.num_programs(0)

    def start(blk, slot):
        for k in range(2):
            _row_gather(y_hbm, ybuf.at[slot, k], sem.at[slot], tile,
                        lambda r, k=k: dest_ref[k * n_tok + blk * tile + r])

    @pl.when(i == 0)
    def _():
        start(0, 0)

    @pl.when(i + 1 < n_tiles)
    def _():
        start(i + 1, (i + 1) % 2)

    slot = i % 2
    for k in range(2):
        _row_gather_wait(y_hbm, ybuf.at[slot, k], sem.at[slot], tile)
    o_ref[...] = h_ref[...] + mod_ref[5:6, :] * (ybuf[slot, 0] + ybuf[slot, 1])


def _combine(h1, yb, dest, mod, cond_of_tile, tile):
    t, d = h1.shape
    grid_spec = pltpu.PrefetchScalarGridSpec(
        num_scalar_prefetch=1,
        grid=(t // tile,),
        in_specs=[pl.BlockSpec((tile, d), lambda i, dst: (i, 0)),
                  pl.BlockSpec(memory_space=pl.ANY),
                  pl.BlockSpec((None, 6, d), lambda i, dst: (cond_of_tile(i), 0, 0))],
        out_specs=pl.BlockSpec((tile, d), lambda i, dst: (i, 0)),
        scratch_shapes=[pltpu.VMEM((2, 2, tile, d), F32), pltpu.SemaphoreType.DMA((2,))],
    )
    return pl.pallas_call(
        functools.partial(_combine_body, n_tok=t, tile=tile),
        grid_spec=grid_spec,
        out_shape=jax.ShapeDtypeStruct((t, d), F32),
        compiler_params=_params("arbitrary"),
        name="moe_gated_residual",
    )(dest, h1, yb, mod)


def _moe(h1, x2, idx_t, gate_t, mod, w_gate, w_up, w_down, cond_of_tile):
    t, d = h1.shape
    n = 2 * t
    experts = idx_t.T.reshape(n)
    gates = gate_t.T.reshape(n)
    onehot = (experts[:, None] == jnp.arange(N_EXPERTS, dtype=jnp.int32)[None, :]).astype(jnp.int32)
    csum = jnp.cumsum(onehot, axis=0)
    counts = csum[-1]
    rank = jnp.sum((csum - onehot) * onehot, axis=1)
    padded = (counts + MOE_ROWS - 1) // MOE_ROWS * MOE_ROWS
    pad_end = jnp.cumsum(padded)
    dest = (pad_end - padded)[experts] + rank
    n_blocks = (n + N_EXPERTS * (MOE_ROWS - 1) + MOE_ROWS - 1) // MOE_ROWS
    n_rows = n_blocks * MOE_ROWS
    tok_of_row = jnp.zeros((n_rows,), jnp.int32).at[dest].set(jnp.arange(n, dtype=jnp.int32) // 2)
    gate_rows = jnp.zeros((n_rows,), F32).at[dest].set(gates)
    block_expert = jnp.minimum(
        jnp.searchsorted(pad_end, jnp.arange(n_blocks, dtype=jnp.int32) * MOE_ROWS, side='right'),
        N_EXPERTS - 1).astype(jnp.int32)
    n_used = (pad_end[-1:] // MOE_ROWS).astype(jnp.int32)
    yb = _expert_blocks(x2, tok_of_row, gate_rows[:, None], block_expert, n_used, w_gate, w_up, w_down)
    dest_slot_major = dest.reshape(t, 2).T.reshape(n)
    return _combine(h1, yb, dest_slot_major, mod, cond_of_tile, MOE_ROWS)


def kernel(x_prompt, x_sample, cache_na_k, cache_na_v, cache_gqa_k, cache_gqa_v, cache_diff_k, cache_diff_v, state_hgrn, c, c_ctx, w_mod, b_mod, norm1, norm2, w_in, w_out, hg_lb_logits, hg_onorm, na_qn, na_kn, na_rpb, gqa_qn, gqa_kn, df_qn, df_kn, df_lam, df_subln, router_w, router_b, w_gate, w_up, w_down):
    n_ctx, ctx_len, d = x_prompt.shape
    n_lat, lat_len, _ = x_sample.shape
    depth = w_in.shape[0]
    t_ctx = n_ctx * ctx_len
    t_all = t_ctx + n_lat * lat_len
    assert t_ctx % lat_len == 0 and lat_len % GRID_W == 0 and lat_len // GRID_W >= WIN_ROWS
    tm = next(m for m in (1024, 512, 256) if t_ctx % m == 0 and lat_len % m == 0)
    tm2 = min(tm, 512)
    lat_block0 = t_ctx // lat_len

    def cond_tile(tile_rows):
        def cond_of_tile(i):
            return jnp.where(i < t_ctx // tile_rows, 0, 1 + (i - t_ctx // tile_rows) // (lat_len // tile_rows))
        return cond_of_tile

    sm = jax.nn.softmax(hg_lb_logits.astype(F32), axis=0)
    lower = jnp.cumsum(sm, axis=0) - sm[0:1]
    mod_all = _modulation(jnp.concatenate([c_ctx[None, :], c], axis=0), w_mod, b_mod)
    mod_all = mod_all.reshape(depth, 1 + n_lat, 6, d)
    hgrn_consts = _hgrn_constants(HGRN_CHUNK)
    rope_c = _rope_tables(lat_len, HEAD_W)
    rope_d = _rope_tables(lat_len, DF_DQK)
    past = cache_diff_k.shape[4]
    cache_diff_k2 = cache_diff_k.transpose(0, 1, 2, 4, 3, 5).reshape(n_lat, depth, N_HEADS, past, HEAD_W)

    h = jnp.concatenate([x_prompt.reshape(t_ctx, d), x_sample.reshape(n_lat * lat_len, d)], axis=0)
    new_caches = [[] for _ in range(7)]
    for layer in range(depth):
        mod = mod_all[layer]
        lam_init = 0.8 - 0.6 * math.exp(-0.3 * layer)
        proj = _input_projection(h, mod, norm1[layer], w_in[layer].astype(BF16), cond_tile(tm), tm)
        a_ctx, st_ctx = _hgrn(proj, 0, n_ctx, ctx_len, lower[layer], hg_onorm[layer], hgrn_consts, None, True)
        gains = (na_qn[layer], na_kn[layer], gqa_qn[layer], gqa_kn[layer], df_qn[layer], df_kn[layer],
                 df_subln[layer], df_lam[layer])
        b_ctx, c_ctx_o, d_ctx, kb, vb, kc, vc, kd, vd = _context_attention(proj, n_ctx, ctx_len, gains, lam_init)
        kd = kd.reshape(n_ctx, N_HEADS, ctx_len, 2, DF_DQK).transpose(0, 1, 3, 2, 4)
        for store, tensor in zip(new_caches, (kb, vb, kc, vc, kd, vd, st_ctx)):
            store.append(tensor)
        a_lat, _ = _hgrn(proj, lat_block0, n_lat, lat_len, lower[layer], hg_onorm[layer], hgrn_consts,
                         state_hgrn[:, layer], False)
        b_lat = _latent_na(proj, lat_block0, n_lat, lat_len, cache_na_k, cache_na_v, layer,
                           _na_bias(na_rpb[layer], lat_len // GRID_W), na_qn[layer], na_kn[layer])
        c_lat = _latent_gqa(proj, lat_block0, n_lat, lat_len, cache_gqa_k, cache_gqa_v, layer, rope_c,
                            gqa_qn[layer], gqa_kn[layer])
        d_lat = _latent_diff(proj, lat_block0, n_lat, lat_len, cache_diff_k2, cache_diff_v, layer, rope_d,
                             df_qn[layer], df_kn[layer], df_subln[layer], df_lam[layer], lam_init)
        mixed = jnp.concatenate([jnp.concatenate([a_ctx, b_ctx, c_ctx_o, d_ctx], axis=1),
                                 jnp.concatenate([a_lat, b_lat, c_lat, d_lat], axis=1)], axis=0)
        h1, x2, idx_t, gate_t = _output_projection(mixed, w_out[layer].astype(BF16), h, mod, norm2[layer],
                                                   router_w, router_b, cond_tile(tm2), tm2)
        h = _moe(h1, x2, idx_t, gate_t, mod, w_gate[layer].astype(BF16), w_up[layer].astype(BF16),
                 w_down[layer].astype(BF16), cond_tile(MOE_ROWS))
    y_prompt = h[:t_ctx].reshape(n_ctx, ctx_len, d)
    y_sample = h[t_ctx:].reshape(n_lat, lat_len, d)
    stacked = [jnp.stack(per_layer, axis=1) for per_layer in new_caches]
    return (y_prompt, y_sample, *stacked)
```

```python
import functools
import math

import numpy as np
import jax
import jax.numpy as jnp
from jax import lax
from jax.experimental import pallas as pl
from jax.experimental.pallas import tpu as pltpu

D_MODEL = 2048
GRID_W = 64
GROUP_W = D_MODEL // 4
N_HEADS = 4
HEAD_W = GROUP_W // N_HEADS
GQA_KV_HEADS = 2
DF_DQK = HEAD_W // 2
WIN_ROWS = 8
WIN_COLS = 16
N_EXPERTS = 16
N_EXP_GROUPS = 4
EXP_PER_GROUP = N_EXPERTS // N_EXP_GROUPS
EXPERT_FF = D_MODEL // 4
ROPE_THETA = 10000.0
EPS = 1e-6
NEG_INF = -1e30
IN_WIDTH = 13 * GROUP_W

COL_A_Q, COL_A_FF, COL_A_FB, COL_A_I, COL_A_G = 0, 4, 8, 12, 16
COL_B_Q, COL_B_K, COL_B_V = 20, 24, 28
COL_C_Q, COL_C_K, COL_C_V = 32, 36, 38
COL_D_Q, COL_D_K, COL_D_V = 40, 44, 48

HGRN_CHUNK = 64
MOE_ROWS = 256
VMEM_LIMIT = 48 * 1024 * 1024

F32 = jnp.float32
BF16 = jnp.bfloat16


def _params(*sem):
    return pltpu.CompilerParams(dimension_semantics=sem, vmem_limit_bytes=VMEM_LIMIT)


def _sigmoid(x):
    return 1.0 / (1.0 + jnp.exp(-x))


def _silu(x):
    return x * _sigmoid(x)


def _rms(x, gain, n):
    return x * lax.rsqrt(jnp.sum(x * x, axis=-1, keepdims=True) * (1.0 / n) + EPS) * gain


def _dot(a, b):
    return jnp.dot(a, b, preferred_element_type=F32)


def _dot_nt(a, b):
    return lax.dot_general(a, b, (((1,), (1,)), ((), ())), preferred_element_type=F32)


def _dot_tn(a, b):
    return lax.dot_general(a, b, (((0,), (0,)), ((), ())), preferred_element_type=F32)


def _mod_body(cond_ref, w_ref, b_ref, o_ref):
    w = w_ref[...]
    for c in range(cond_ref.shape[0]):
        s = _silu(cond_ref[c])
        o_ref[c:c + 1, :] = jnp.sum(w * s, axis=0, keepdims=True) + b_ref[...]


def _modulation(cond, w_mod, b_mod):
    depth, d, n6 = w_mod.shape
    nc = cond.shape[0]
    tn = 512
    return pl.pallas_call(
        _mod_body,
        grid=(depth, n6 // tn),
        in_specs=[pl.BlockSpec((nc, d, 1), lambda l, j: (0, 0, 0)),
                  pl.BlockSpec((None, d, tn), lambda l, j: (l, 0, j)),
                  pl.BlockSpec((None, 1, tn), lambda l, j: (l, 0, j))],
        out_specs=pl.BlockSpec((None, nc, tn), lambda l, j: (l, 0, j)),
        out_shape=jax.ShapeDtypeStruct((depth, nc, n6), F32),
        compiler_params=_params("parallel", "parallel"),
        name="adaln_modulation",
    )(cond[:, :, None], w_mod, b_mod[:, None, :])


def _inproj_body(h_ref, mod_ref, n1_ref, w_ref, o_ref, xn_ref):
    @pl.when(pl.program_id(1) == 0)
    def _():
        y = _rms(h_ref[...], n1_ref[...], D_MODEL)
        xn_ref[...] = (y * (1.0 + mod_ref[1:2, :]) + mod_ref[0:1, :]).astype(BF16)

    o_ref[...] = _dot(xn_ref[...], w_ref[...])


def _input_projection(h, mod, norm1, w_in_bf16, cond_of_tile, tm):
    t, d = h.shape
    n = w_in_bf16.shape[1]
    tn = 512
    return pl.pallas_call(
        _inproj_body,
        grid=(t // tm, n // tn),
        in_specs=[pl.BlockSpec((tm, d), lambda i, j: (i, 0)),
                  pl.BlockSpec((None, 6, d), lambda i, j: (cond_of_tile(i), 0, 0)),
                  pl.BlockSpec((1, d), lambda i, j: (0, 0)),
                  pl.BlockSpec((d, tn), lambda i, j: (0, j))],
        out_specs=pl.BlockSpec((tm, tn), lambda i, j: (i, j)),
        out_shape=jax.ShapeDtypeStruct((t, n), F32),
        scratch_shapes=[pltpu.VMEM((tm, d), BF16)],
        compiler_params=_params("parallel", "arbitrary"),
        name="norm_modulate_in_proj",
    )(h, mod, norm1[None, :], w_in_bf16)


def _hgrn_constants(c):
    nl = int(math.log2(c))
    idx = np.arange(c)
    e = np.zeros((nl + 2, c, c), np.float32)
    m = np.zeros((nl + 1, c, c), np.float32)
    e[0] = idx[None, :] <= idx[:, None]
    e[1] = idx[None, :] > idx[:, None]
    m[0] = np.eye(c)
    for li in range(nl):
        s = c >> (li + 1)
        parent = idx // (2 * s)
        right = (idx % (2 * s)) >= s
        ref = parent * 2 * s + s - 1
        for i in range(c):
            if right[i]:
                e[2 + li, i, ref[i] + 1:i + 1] = 1.0
            else:
                e[2 + li, i, i + 1:ref[i] + 1] = 1.0
        m[1 + li] = right[:, None] & ~right[None, :] & (parent[:, None] == parent[None, :])
    e2 = np.stack([e, e[:, ::-1, ::-1]]).reshape(2, (nl + 2) * c, c)
    m2 = np.stack([m, m[:, ::-1, ::-1]])
    return jnp.asarray(e2, BF16), jnp.asarray(m2, F32)


def _hgrn_body(*refs, seq, chunk, has_s0, emit_state):
    q_ref, ff_ref, fb_ref, i_ref, g_ref, lb_ref, on_ref, e_ref, m_ref = refs[:9]
    pos = 9
    s0_ref = None
    if has_s0:
        s0_ref = refs[pos]
        pos += 1
    o_ref = refs[pos]
    pos += 1
    if emit_state:
        st_ref = refs[pos]
        pos += 1
    of_ref, ob_ref = refs[pos], refs[pos + 1]
    c = chunk
    n_chunks = seq // c
    n_levels = m_ref.shape[1] - 1
    gate_refs = (ff_ref, fb_ref)
    out_refs = (of_ref, ob_ref)

    def chunk_step(c0, d, st):
        rows = pl.ds(c0, c)
        lb = lb_ref[d:d + 1, :]
        f = lb + (1.0 - lb) * _sigmoid(gate_refs[d][rows, :])
        g = jnp.log(f)
        k = 1.0 - f
        q = _silu(q_ref[rows, :])
        v = i_ref[rows, :].astype(BF16)
        g_hi = g.astype(BF16)
        r1 = g - g_hi.astype(F32)
        g_mid = r1.astype(BF16)
        g_lo = (r1 - g_mid.astype(F32)).astype(BF16)
        g3 = _dot(e_ref[d], jnp.concatenate([g_hi, g_mid, g_lo], axis=1))
        x = jnp.exp(g3[:, 0:HEAD_W] + g3[:, HEAD_W:2 * HEAD_W] + g3[:, 2 * HEAD_W:3 * HEAD_W])
        x_cum = x[0:c]
        x_tail = x[c:2 * c]
        s = m_ref[d, 0] * _dot_nt(q.astype(BF16), k.astype(BF16))
        for lv in range(n_levels):
            x_l = x[(2 + lv) * c:(3 + lv) * c]
            s = s + m_ref[d, 1 + lv] * _dot_nt((q * x_l).astype(BF16), (k * x_l).astype(BF16))
        o = _dot_nt((q * x_cum).astype(BF16), st.astype(BF16)) + _dot(s.astype(BF16), v)
        out_refs[d][rows, :] = o
        total = x_cum[c - 1:c, :] if d == 0 else x_cum[0:1, :]
        return st * total + _dot_tn(v, (k * x_tail).astype(BF16))

    if has_s0:
        st_f0 = s0_ref[0].T
        st_b0 = s0_ref[1].T
    else:
        st_f0 = jnp.zeros((HEAD_W, HEAD_W), F32)
        st_b0 = jnp.zeros((HEAD_W, HEAD_W), F32)

    def loop(t, carry):
        st_f, st_b = carry
        st_f = chunk_step(pl.multiple_of(t * c, c), 0, st_f)
        st_b = chunk_step(pl.multiple_of((n_chunks - 1 - t) * c, c), 1, st_b)
        return st_f, st_b

    st_f, st_b = lax.fori_loop(0, n_chunks, loop, (st_f0, st_b0))
    o = of_ref[...] + ob_ref[...]
    o_ref[...] = (_rms(o, on_ref[...], HEAD_W) * _silu(g_ref[...])).astype(o_ref.dtype)
    if emit_state:
        st_ref[0] = st_f.T
        st_ref[1] = st_b.T


def _hgrn(proj, row_block0, n_seq, seq, lower, onorm, consts, s0, emit_state):
    e_mat, masks = consts

    def col(cb):
        return pl.BlockSpec((seq, HEAD_W), lambda b, h, cb=cb: (row_block0 + b, cb + h))

    in_specs = [col(COL_A_Q), col(COL_A_FF), col(COL_A_FB), col(COL_A_I), col(COL_A_G),
                pl.BlockSpec((2, HEAD_W), lambda b, h: (0, h)),
                pl.BlockSpec((1, HEAD_W), lambda b, h: (0, 0)),
                pl.BlockSpec(e_mat.shape, lambda b, h: (0, 0, 0)),
                pl.BlockSpec(masks.shape, lambda b, h: (0, 0, 0, 0))]
    args = [proj, proj, proj, proj, proj, lower, onorm[None, :], e_mat, masks]
    if s0 is not None:
        in_specs.append(pl.BlockSpec((None, 2, None, HEAD_W, HEAD_W), lambda b, h: (b, 0, h, 0, 0)))
        args.append(s0)
    out_specs = [pl.BlockSpec((seq, HEAD_W), lambda b, h: (b, h))]
    out_shape = [jax.ShapeDtypeStruct((n_seq * seq, GROUP_W), BF16)]
    if emit_state:
        out_specs.append(pl.BlockSpec((None, 2, None, HEAD_W, HEAD_W), lambda b, h: (b, 0, h, 0, 0)))
        out_shape.append(jax.ShapeDtypeStruct((n_seq, 2, N_HEADS, HEAD_W, HEAD_W), F32))
    res = pl.pallas_call(
        functools.partial(_hgrn_body, seq=seq, chunk=HGRN_CHUNK, has_s0=s0 is not None,
                          emit_state=emit_state),
        grid=(n_seq, N_HEADS),
        in_specs=in_specs, out_specs=out_specs, out_shape=out_shape,
        scratch_shapes=[pltpu.VMEM((seq, HEAD_W), F32), pltpu.VMEM((seq, HEAD_W), F32)],
        compiler_params=_params("parallel", "parallel"),
        name="hgrn2_latent" if s0 is not None else "hgrn2_context",
    )(*args)
    return res if emit_state else (res[0], None)


def _softmax_pv(scores, values):
    mx = functools.reduce(jnp.maximum, [jnp.max(s, axis=-1, keepdims=True) for s in scores])
    es = [jnp.exp(s - mx) for s in scores]
    den = functools.reduce(lambda a, b: a + b, [jnp.sum(e, axis=-1, keepdims=True) for e in es])
    num = functools.reduce(lambda a, b: a + b, [_dot(e.astype(BF16), v) for e, v in zip(es, values)])
    return num / den


def _diff_weights(s0, s1, lam):
    e0 = jnp.exp(s0 - jnp.max(s0, axis=-1, keepdims=True))
    e1 = jnp.exp(s1 - jnp.max(s1, axis=-1, keepdims=True))
    p0 = e0 / jnp.sum(e0, axis=-1, keepdims=True)
    p1 = e1 / jnp.sum(e1, axis=-1, keepdims=True)
    return (p0 - lam * p1).astype(BF16)


def _lane_lt(shape, n):
    return lax.broadcasted_iota(jnp.int32, shape, len(shape) - 1) < n


def _rms_halves(x, gain2):
    lo = _lane_lt(x.shape, DF_DQK)
    sq = x * x
    ss_lo = jnp.sum(jnp.where(lo, sq, 0.0), axis=-1, keepdims=True)
    ss_hi = jnp.sum(sq, axis=-1, keepdims=True) - ss_lo
    inv = jnp.where(lo, lax.rsqrt(ss_lo * (1.0 / DF_DQK) + EPS), lax.rsqrt(ss_hi * (1.0 / DF_DQK) + EPS))
    return x * inv * gain2


def _lambda(lam_ref, lam_init):
    l = lam_ref[...]
    return (jnp.exp(jnp.sum(l[0:1] * l[1:2], axis=-1, keepdims=True))
            - jnp.exp(jnp.sum(l[2:3] * l[3:4], axis=-1, keepdims=True)) + lam_init)


def _ctx_attn_body(bq_ref, bk_ref, bv_ref, cq_ref, ck_ref, cv_ref, dq_ref, dk_ref, dv_ref,
                   naq_ref, nak_ref, gq_ref, gk_ref, dfq_ref, dfk_ref, sub_ref, lam_ref,
                   ob_ref, oc_ref, od_ref, kb_ref, vb_ref, kc_ref, vc_ref, kd_ref, vd_ref, *, lam_init):
    scale = HEAD_W ** -0.5
    kb = _rms(bk_ref[...], nak_ref[...], HEAD_W)
    vb = bv_ref[...]
    kb_ref[...] = kb
    vb_ref[...] = vb
    qb = (_rms(bq_ref[...], naq_ref[...], HEAD_W) * scale).astype(BF16)
    ob_ref[...] = _softmax_pv([_dot_nt(qb, kb.astype(BF16))], [vb.astype(BF16)]).astype(ob_ref.dtype)
    kc = _rms(ck_ref[...], gk_ref[...], HEAD_W)
    vc = cv_ref[...]
    kc_ref[...] = kc
    vc_ref[...] = vc
    qc = (_rms(cq_ref[...], gq_ref[...], HEAD_W) * scale).astype(BF16)
    oc_ref[...] = _softmax_pv([_dot_nt(qc, kc.astype(BF16))], [vc.astype(BF16)]).astype(oc_ref.dtype)
    kd = _rms_halves(dk_ref[...], dfk_ref[...])
    vd = dv_ref[...]
    kd_ref[...] = kd
    vd_ref[...] = vd
    qd = (_rms_halves(dq_ref[...], dfq_ref[...]) * (DF_DQK ** -0.5)).astype(BF16)
    lo = _lane_lt(kd.shape, DF_DQK)
    s0 = _dot_nt(qd, jnp.where(lo, kd, 0.0).astype(BF16))
    s1 = _dot_nt(qd, jnp.where(lo, 0.0, kd).astype(BF16))
    od = _dot(_diff_weights(s0, s1, _lambda(lam_ref, lam_init)), vd.astype(BF16))
    od_ref[...] = (_rms(od, sub_ref[...], HEAD_W) * (1.0 - lam_init)).astype(od_ref.dtype)


def _context_attention(proj, n_seq, seq, gains, lam_init):
    na_qn, na_kn, gqa_qn, gqa_kn, df_qn, df_kn, df_subln, df_lam = gains

    def col(cb, div=1):
        return pl.BlockSpec((seq, HEAD_W), lambda b, h, cb=cb, div=div: (b, cb + h // div))

    def vec(width=HEAD_W):
        return pl.BlockSpec((1, width), lambda b, h: (0, 0))

    def cache(div=1):
        return pl.BlockSpec((None, None, seq, HEAD_W), lambda b, h, div=div: (b, h // div, 0, 0))

    mixed = pl.BlockSpec((seq, HEAD_W), lambda b, h: (b, h))
    mixed_shape = jax.ShapeDtypeStruct((n_seq * seq, GROUP_W), BF16)
    cache4 = jax.ShapeDtypeStruct((n_seq, N_HEADS, seq, HEAD_W), F32)
    cache2 = jax.ShapeDtypeStruct((n_seq, GQA_KV_HEADS, seq, HEAD_W), F32)
    group = N_HEADS // GQA_KV_HEADS
    return pl.pallas_call(
        functools.partial(_ctx_attn_body, lam_init=lam_init),
        grid=(n_seq, N_HEADS),
        in_specs=[col(COL_B_Q), col(COL_B_K), col(COL_B_V),
                  col(COL_C_Q), col(COL_C_K, group), col(COL_C_V, group),
                  col(COL_D_Q), col(COL_D_K), col(COL_D_V),
                  vec(), vec(), vec(), vec(), vec(), vec(), vec(),
                  pl.BlockSpec((4, DF_DQK), lambda b, h: (0, 0))],
        out_specs=[mixed, mixed, mixed, cache(), cache(), cache(group), cache(group), cache(), cache()],
        out_shape=[mixed_shape, mixed_shape, mixed_shape, cache4, cache4, cache2, cache2, cache4, cache4],
        compiler_params=_params("parallel", "arbitrary"),
        name="context_attention",
    )(proj, proj, proj, proj, proj, proj, proj, proj, proj,
      na_qn[None, :], na_kn[None, :], gqa_qn[None, :], gqa_kn[None, :],
      jnp.tile(df_qn, 2)[None, :], jnp.tile(df_kn, 2)[None, :], df_subln[None, :], df_lam)


def _rope_tables(n_tokens, rot_dim):
    t = np.arange(n_tokens)
    row = (t // GRID_W).astype(np.float32)
    col = (t % GRID_W).astype(np.float32)
    n_freq = rot_dim // 4
    inv = (np.float32(ROPE_THETA) ** (-np.arange(n_freq, dtype=np.float32) / np.float32(n_freq))).astype(np.float32)
    ang = np.concatenate([row[:, None] * inv, col[:, None] * inv], axis=-1).astype(np.float32)
    cos, sin, zero = np.cos(ang), np.sin(ang), np.zeros_like(ang)
    reps = HEAD_W // rot_dim
    a = np.tile(np.concatenate([cos, cos], axis=-1), (1, reps))
    b = np.tile(np.concatenate([-sin, zero], axis=-1), (1, reps))
    c = np.tile(np.concatenate([zero, sin], axis=-1), (1, reps))
    return jnp.asarray(np.stack([a, b, c]), F32)


def _rope(x, tab_ref, half):
    return (x * tab_ref[0] + pltpu.roll(x, HEAD_W - half, 1) * tab_ref[1]
            + pltpu.roll(x, half, 1) * tab_ref[2])


def _na_body(q_ref, k_ref, v_ref, ck_ref, cv_ref, bias_ref, qn_ref, kn_ref, o_ref, qs_ref, ks_ref, vs_ref,
             *, seq):
    rows = seq // GRID_W
    n_win = WIN_ROWS * GRID_W
    qs_ref[...] = (_rms(q_ref[...], qn_ref[...], HEAD_W) * (HEAD_W ** -0.5)).astype(BF16)
    ks_ref[...] = _rms(k_ref[...], kn_ref[...], HEAD_W).astype(BF16)
    vs_ref[...] = v_ref[...].astype(BF16)
    ck = ck_ref[...].astype(BF16)
    cv = cv_ref[...].astype(BF16)

    def row_step(r, carry):
        start = jnp.clip(r - WIN_ROWS // 2, 0, rows - WIN_ROWS)
        win = pl.ds(pl.multiple_of(start * GRID_W, GRID_W), n_win)
        qrows = pl.ds(pl.multiple_of(r * GRID_W, GRID_W), GRID_W)
        q = qs_ref[qrows, :]
        s_win = _dot_nt(q, ks_ref[win, :]) + bias_ref[start - r + (WIN_ROWS - 1)]
        s_ctx = _dot_nt(q, ck)
        o_ref[qrows, :] = _softmax_pv([s_win, s_ctx], [vs_ref[win, :], cv]).astype(o_ref.dtype)
        return carry

    lax.fori_loop(0, rows, row_step, 0)


def _na_bias(rpb):
    col = np.arange(GRID_W)
    col_start = np.clip(col - WIN_COLS // 2, 0, GRID_W - WIN_COLS)
    col_ok = (col[None, :] >= col_start[:, None]) & (col[None, :] < col_start[:, None] + WIN_COLS)
    dc = np.clip(col[None, :] - col[:, None] + WIN_COLS - 1, 0, 2 * WIN_COLS - 2).reshape(-1)
    onehot = (np.arange(2 * WIN_COLS - 1)[:, None] == dc[None, :]).astype(np.float32)
    per_dr = jnp.einsum('hdc,cn->hdn', rpb.astype(F32), jnp.asarray(onehot), precision=lax.Precision.HIGHEST)
    per_dr = jnp.where(col_ok[None, None], per_dr.reshape(rpb.shape[0], -1, GRID_W, GRID_W), NEG_INF)
    wins = jnp.stack([per_dr[:, o:o + WIN_ROWS] for o in range(WIN_ROWS)], axis=1)
    return wins.transpose(0, 1, 3, 2, 4).reshape(rpb.shape[0], WIN_ROWS, GRID_W, WIN_ROWS * GRID_W)


def _latent_na(proj, row_block0, n_seq, seq, cache_k, cache_v, layer, bias, na_qn, na_kn):
    past = cache_k.shape[3]

    def col(cb):
        return pl.BlockSpec((seq, HEAD_W), lambda b, h, cb=cb: (row_block0 + b, cb + h))

    cache = pl.BlockSpec((None, None, None, past, HEAD_W), lambda b, h: (b, layer, h, 0, 0))
    vec = pl.BlockSpec((1, HEAD_W), lambda b, h: (0, 0))
    return pl.pallas_call(
        functools.partial(_na_body, seq=seq),
        grid=(n_seq, N_HEADS),
        in_specs=[col(COL_B_Q), col(COL_B_K), col(COL_B_V), cache, cache,
                  pl.BlockSpec((None, WIN_ROWS, GRID_W, WIN_ROWS * GRID_W), lambda b, h: (h, 0, 0, 0)),
                  vec, vec],
        out_specs=pl.BlockSpec((seq, HEAD_W), lambda b, h: (b, h)),
        out_shape=jax.ShapeDtypeStruct((n_seq * seq, GROUP_W), BF16),
        scratch_shapes=[pltpu.VMEM((seq, HEAD_W), BF16)] * 3,
        compiler_params=_params("parallel", "parallel"),
        name="latent_neighbourhood_attention",
    )(proj, proj, proj, cache_k, cache_v, bias, na_qn[None, :], na_kn[None, :])


def _gqa_body(q_ref, k_ref, v_ref, ck_ref, cv_ref, rope_ref, qn_ref, kn_ref, o_ref, qs_ref, ks_ref, vs_ref,
              *, seq, tq):
    group = N_HEADS // GQA_KV_HEADS
    half = HEAD_W // 2
    ks_ref[0:seq, :] = _rope(_rms(k_ref[...], kn_ref[...], HEAD_W), rope_ref, half).astype(BF16)
    ks_ref[seq:, :] = ck_ref[...].astype(BF16)
    vs_ref[0:seq, :] = v_ref[...].astype(BF16)
    vs_ref[seq:, :] = cv_ref[...].astype(BF16)
    for g in range(group):
        q = _rms(q_ref[:, g * HEAD_W:(g + 1) * HEAD_W], qn_ref[...], HEAD_W) * (HEAD_W ** -0.5)
        qs_ref[g] = _rope(q, rope_ref, half).astype(BF16)
    kk = ks_ref[...]
    vv = vs_ref[...]
    for g in range(group):
        def q_step(i, carry, g=g):
            qrows = pl.ds(pl.multiple_of(i * tq, tq), tq)
            o = _softmax_pv([_dot_nt(qs_ref[g, qrows, :], kk)], [vv])
            o_ref[qrows, g * HEAD_W:(g + 1) * HEAD_W] = o.astype(o_ref.dtype)
            return carry

        lax.fori_loop(0, seq // tq, q_step, 0)


def _latent_gqa(proj, row_block0, n_seq, seq, cache_k, cache_v, layer, rope, gqa_qn, gqa_kn):
    past = cache_k.shape[3]
    group = N_HEADS // GQA_KV_HEADS
    tq = 256
    cache = pl.BlockSpec((None, None, None, past, HEAD_W), lambda b, n: (b, layer, n, 0, 0))
    vec = pl.BlockSpec((1, HEAD_W), lambda b, n: (0, 0))
    return pl.pallas_call(
        functools.partial(_gqa_body, seq=seq, tq=tq),
        grid=(n_seq, GQA_KV_HEADS),
        in_specs=[pl.BlockSpec((seq, group * HEAD_W), lambda b, n: (row_block0 + b, COL_C_Q // group + n)),
                  pl.BlockSpec((seq, HEAD_W), lambda b, n: (row_block0 + b, COL_C_K + n)),
                  pl.BlockSpec((seq, HEAD_W), lambda b, n: (row_block0 + b, COL_C_V + n)),
                  cache, cache,
                  pl.BlockSpec((3, seq, HEAD_W), lambda b, n: (0, 0, 0)),
                  vec, vec],
        out_specs=pl.BlockSpec((seq, group * HEAD_W), lambda b, n: (b, n)),
        out_shape=jax.ShapeDtypeStruct((n_seq * seq, GROUP_W), BF16),
        scratch_shapes=[pltpu.VMEM((group, seq, HEAD_W), BF16),
                        pltpu.VMEM((seq + past, HEAD_W), BF16),
                        pltpu.VMEM((seq + past, HEAD_W), BF16)],
        compiler_params=_params("parallel", "parallel"),
        name="latent_gqa_attention",
    )(proj, proj, proj, cache_k, cache_v, rope, gqa_qn[None, :], gqa_kn[None, :])


def _diff_body(q_ref, k_ref, v_ref, ck_ref, cv_ref, rope_ref, qn_ref, kn_ref, sub_ref, lam_ref, o_ref,
               qs_ref, k0_ref, k1_ref, vs_ref, *, seq, tq, lam_init):
    half = DF_DQK // 2
    k = _rope(_rms_halves(k_ref[...], kn_ref[...]), rope_ref, half)
    lo = _lane_lt(k.shape, DF_DQK)
    k0_ref[0:seq, :] = jnp.where(lo, k, 0.0).astype(BF16)
    k1_ref[0:seq, :] = jnp.where(lo, 0.0, k).astype(BF16)
    ck = ck_ref[...]
    lo_c = _lane_lt(ck.shape, DF_DQK)
    k0_ref[seq:, :] = jnp.where(lo_c, ck, 0.0).astype(BF16)
    k1_ref[seq:, :] = jnp.where(lo_c, 0.0, ck).astype(BF16)
    vs_ref[0:seq, :] = v_ref[...].astype(BF16)
    vs_ref[seq:, :] = cv_ref[...].astype(BF16)
    q = _rms_halves(q_ref[...], qn_ref[...]) * (DF_DQK ** -0.5)
    qs_ref[...] = _rope(q, rope_ref, half).astype(BF16)
    lam = _lambda(lam_ref, lam_init)
    k0 = k0_ref[...]
    k1 = k1_ref[...]
    vv = vs_ref[...]

    def q_step(i, carry):
        qrows = pl.ds(pl.multiple_of(i * tq, tq), tq)
        qb = qs_ref[qrows, :]
        o = _dot(_diff_weights(_dot_nt(qb, k0), _dot_nt(qb, k1), lam), vv)
        o_ref[qrows, :] = (_rms(o, sub_ref[...], HEAD_W) * (1.0 - lam_init)).astype(o_ref.dtype)
        return carry

    lax.fori_loop(0, seq // tq, q_step, 0)


def _latent_diff(proj, row_block0, n_seq, seq, cache_k2, cache_v, layer, rope, df_qn, df_kn, df_subln, df_lam,
                 lam_init):
    past = cache_k2.shape[3]
    tq = 256

    def col(cb):
        return pl.BlockSpec((seq, HEAD_W), lambda b, h, cb=cb: (row_block0 + b, cb + h))

    cache = pl.BlockSpec((None, None, None, past, HEAD_W), lambda b, h: (b, layer, h, 0, 0))
    vec = pl.BlockSpec((1, HEAD_W), lambda b, h: (0, 0))
    kv_scratch = pltpu.VMEM((seq + past, HEAD_W), BF16)
    return pl.pallas_call(
        functools.partial(_diff_body, seq=seq, tq=tq, lam_init=lam_init),
        grid=(n_seq, N_HEADS),
        in_specs=[col(COL_D_Q), col(COL_D_K), col(COL_D_V), cache, cache,
                  pl.BlockSpec((3, seq, HEAD_W), lambda b, h: (0, 0, 0)),
                  vec, vec, vec, pl.BlockSpec((4, DF_DQK), lambda b, h: (0, 0))],
        out_specs=pl.BlockSpec((seq, HEAD_W), lambda b, h: (b, h)),
        out_shape=jax.ShapeDtypeStruct((n_seq * seq, GROUP_W), BF16),
        scratch_shapes=[pltpu.VMEM((seq, HEAD_W), BF16), kv_scratch, kv_scratch, kv_scratch],
        compiler_params=_params("parallel", "parallel"),
        name="latent_diff_attention",
    )(proj, proj, proj, cache_k2, cache_v, rope, jnp.tile(df_qn, 2)[None, :], jnp.tile(df_kn, 2)[None, :],
      df_subln[None, :], df_lam)


def _first_max(vals):
    best = vals[0]
    idx = jnp.zeros(best.shape, jnp.int32)
    for i in range(1, len(vals)):
        better = vals[i] > best
        best = jnp.where(better, vals[i], best)
        idx = jnp.where(better, i, idx)
    return best, idx


def _pick(vals, idx):
    out = vals[0]
    for i in range(1, len(vals)):
        out = jnp.where(idx == i, vals[i], out)
    return out


def _outproj_body(mix_ref, w_ref, h_ref, mod_ref, n2_ref, rw_ref, rb_ref, h1_ref, x2_ref, idx_ref, gate_ref):
    h1 = h_ref[...] + mod_ref[2:3, :] * _dot(mix_ref[...], w_ref[...])
    h1_ref[...] = h1
    x2 = _rms(h1, n2_ref[...], D_MODEL) * (1.0 + mod_ref[4:5, :]) + mod_ref[3:4, :]
    x2_ref[...] = x2
    logits = lax.dot_general(rw_ref[...], x2, (((1,), (1,)), ((), ())), precision=lax.Precision.HIGHEST,
                             preferred_element_type=F32)
    aff_all = _sigmoid(logits)
    sel_all = aff_all + rb_ref[...]
    aff = [aff_all[e:e + 1, :] for e in range(N_EXPERTS)]
    sel = [sel_all[e:e + 1, :] for e in range(N_EXPERTS)]
    neg = jnp.full(sel[0].shape, -jnp.inf, F32)
    scores = []
    for g in range(N_EXP_GROUPS):
        grp = sel[g * EXP_PER_GROUP:(g + 1) * EXP_PER_GROUP]
        m1, i1 = _first_max(grp)
        m2, _ = _first_max([jnp.where(i1 == j, neg, grp[j]) for j in range(EXP_PER_GROUP)])
        scores.append(m1 + m2)
    _, g_best = _first_max(scores)
    in_sel = [_pick([sel[g * EXP_PER_GROUP + j] for g in range(N_EXP_GROUPS)], g_best)
              for j in range(EXP_PER_GROUP)]
    in_aff = [_pick([aff[g * EXP_PER_GROUP + j] for g in range(N_EXP_GROUPS)], g_best)
              for j in range(EXP_PER_GROUP)]
    _, l1 = _first_max(in_sel)
    _, l2 = _first_max([jnp.where(l1 == j, neg, in_sel[j]) for j in range(EXP_PER_GROUP)])
    w1 = _pick(in_aff, l1)
    w2 = _pick(in_aff, l2)
    idx_ref[0:1, :] = g_best * EXP_PER_GROUP + l1
    idx_ref[1:2, :] = g_best * EXP_PER_GROUP + l2
    gate_ref[0:1, :] = w1 / (w1 + w2)
    gate_ref[1:2, :] = w2 / (w1 + w2)


def _output_projection(mixed, w_out_bf16, h, mod, norm2, router_w, router_b, cond_of_tile, tm):
    t, d = h.shape
    return pl.pallas_call(
        _outproj_body,
        grid=(t // tm,),
        in_specs=[pl.BlockSpec((tm, d), lambda i: (i, 0)),
                  pl.BlockSpec((d, d), lambda i: (0, 0)),
                  pl.BlockSpec((tm, d), lambda i: (i, 0)),
                  pl.BlockSpec((None, 6, d), lambda i: (cond_of_tile(i), 0, 0)),
                  pl.BlockSpec((1, d), lambda i: (0, 0)),
                  pl.BlockSpec((N_EXPERTS, d), lambda i: (0, 0)),
                  pl.BlockSpec((N_EXPERTS, 1), lambda i: (0, 0))],
        out_specs=[pl.BlockSpec((tm, d), lambda i: (i, 0)),
                   pl.BlockSpec((tm, d), lambda i: (i, 0)),
                   pl.BlockSpec((2, tm), lambda i: (0, i)),
                   pl.BlockSpec((2, tm), lambda i: (0, i))],
        out_shape=[jax.ShapeDtypeStruct((t, d), F32), jax.ShapeDtypeStruct((t, d), F32),
                   jax.ShapeDtypeStruct((2, t), jnp.int32), jax.ShapeDtypeStruct((2, t), F32)],
        compiler_params=_params("parallel"),
        name="out_proj_residual_router",
    )(mixed, w_out_bf16, h, mod, norm2[None, :], router_w.T, router_b[:, None])


def _row_gather(src_hbm, dst, sem, n_rows, row_of):
    def body(r, carry):
        pltpu.make_async_copy(src_hbm.at[pl.ds(row_of(r), 1), :], dst.at[pl.ds(r, 1), :], sem).start()
        return carry

    lax.fori_loop(0, n_rows, body, 0)


def _row_gather_wait(src_hbm, dst, sem, n_rows):
    pltpu.make_async_copy(src_hbm.at[pl.ds(0, n_rows), :], dst, sem).wait()


def _expert_body(be_ref, nb_ref, tok_ref, x_hbm, wg_ref, wu_ref, wd_ref, gate_ref, o_ref, xbuf, sem):
    del be_ref
    i = pl.program_id(0)
    n_used = nb_ref[0]

    def start(blk, slot):
        _row_gather(x_hbm, xbuf.at[slot], sem.at[slot], MOE_ROWS, lambda r: tok_ref[blk * MOE_ROWS + r])

    @pl.when(i == 0)
    def _():
        start(0, 0)

    @pl.when(i + 1 < n_used)
    def _():
        start(i + 1, (i + 1) % 2)

    @pl.when(i < n_used)
    def _():
        slot = i % 2
        _row_gather_wait(x_hbm, xbuf.at[slot], sem.at[slot], MOE_ROWS)
        x = xbuf[slot].astype(BF16)
        hdn = _silu(_dot(x, wg_ref[...])) * _dot(x, wu_ref[...])
        o_ref[...] = _dot(hdn.astype(BF16), wd_ref[...]) * gate_ref[...]

    @pl.when(i >= n_used)
    def _():
        o_ref[...] = jnp.zeros(o_ref.shape, o_ref.dtype)


def _expert_blocks(x2, tok_of_row, gate_rows, block_expert, n_used, w_gate, w_up, w_down):
    d = x2.shape[1]
    n_rows = tok_of_row.shape[0]
    ff = w_gate.shape[-1]
    grid_spec = pltpu.PrefetchScalarGridSpec(
        num_scalar_prefetch=3,
        grid=(n_rows // MOE_ROWS,),
        in_specs=[pl.BlockSpec(memory_space=pl.ANY),
                  pl.BlockSpec((None, d, ff), lambda i, be, nb, tok: (be[i], 0, 0)),
                  pl.BlockSpec((None, d, ff), lambda i, be, nb, tok: (be[i], 0, 0)),
                  pl.BlockSpec((None, ff, d), lambda i, be, nb, tok: (be[i], 0, 0)),
                  pl.BlockSpec((MOE_ROWS, 1), lambda i, be, nb, tok: (i, 0))],
        out_specs=pl.BlockSpec((MOE_ROWS, d), lambda i, be, nb, tok: (i, 0)),
        scratch_shapes=[pltpu.VMEM((2, MOE_ROWS, d), F32), pltpu.SemaphoreType.DMA((2,))],
    )
    return pl.pallas_call(
        _expert_body,
        grid_spec=grid_spec,
        out_shape=jax.ShapeDtypeStruct((n_rows, d), F32),
        compiler_params=_params("arbitrary"),
        name="moe_expert_blocks",
    )(block_expert, n_used, tok_of_row, x2, w_gate, w_up, w_down, gate_rows)


def _combine_body(dest_ref, h_ref, y_hbm, mod_ref, o_ref, ybuf, sem, *, n_tok, tile):
    i = pl.program_id(0)
    n_tiles = pl.num_programs(0)

    def start(blk, slot):
        for k in range(2):
            _row_gather(y_hbm, ybuf.at[slot, k], sem.at[slot], tile,
                        lambda r, k=k: dest_ref[k * n_tok + blk * tile + r])

    @pl.when(i == 0)
    def _():
        start(0, 0)

    @pl.when(i + 1 < n_tiles)
    def _():
        start(i + 1, (i + 1) % 2)

    slot = i % 2
    for k in range(2):
        _row_gather_wait(y_hbm, ybuf.at[slot, k], sem.at[slot], tile)
    o_ref[...] = h_ref[...] + mod_ref[5:6, :] * (ybuf[slot, 0] + ybuf[slot, 1])


def _combine(h1, yb, dest, mod, cond_of_tile, tile):
    t, d = h1.shape
    grid_spec = pltpu.PrefetchScalarGridSpec(
        num_scalar_prefetch=1,
        grid=(t // tile,),
        in_specs=[pl.BlockSpec((tile, d), lambda i, dst: (i, 0)),
                  pl.BlockSpec(memory_space=pl.ANY),
                  pl.BlockSpec((None, 6, d), lambda i, dst: (cond_of_tile(i), 0, 0))],
        out_specs=pl.BlockSpec((tile, d), lambda i, dst: (i, 0)),
        scratch_shapes=[pltpu.VMEM((2, 2, tile, d), F32), pltpu.SemaphoreType.DMA((2,))],
    )
    return pl.pallas_call(
        functools.partial(_combine_body, n_tok=t, tile=tile),
        grid_spec=grid_spec,
        out_shape=jax.ShapeDtypeStruct((t, d), F32),
        compiler_params=_params("arbitrary"),
        name="moe_gated_residual",
    )(dest, h1, yb, mod)


def _moe(h1, x2, idx_t, gate_t, mod, w_gate, w_up, w_down, cond_of_tile):
    t, d = h1.shape
    n = 2 * t
    experts = idx_t.T.reshape(n)
    gates = gate_t.T.reshape(n)
    onehot = (experts[:, None] == jnp.arange(N_EXPERTS, dtype=jnp.int32)[None, :]).astype(jnp.int32)
    csum = jnp.cumsum(onehot, axis=0)
    counts = csum[-1]
    rank = jnp.sum((csum - onehot) * onehot, axis=1)
    padded = (counts + MOE_ROWS - 1) // MOE_ROWS * MOE_ROWS
    pad_end = jnp.cumsum(padded)
    dest = (pad_end - padded)[experts] + rank
    n_blocks = (n + N_EXPERTS * (MOE_ROWS - 1) + MOE_ROWS - 1) // MOE_ROWS
    n_rows = n_blocks * MOE_ROWS
    tok_of_row = jnp.zeros((n_rows,), jnp.int32).at[dest].set(jnp.arange(n, dtype=jnp.int32) // 2)
    gate_rows = jnp.zeros((n_rows,), F32).at[dest].set(gates)
    block_expert = jnp.minimum(
        jnp.searchsorted(pad_end, jnp.arange(n_blocks, dtype=jnp.int32) * MOE_ROWS, side='right'),
        N_EXPERTS - 1).astype(jnp.int32)
    n_used = (pad_end[-1:] // MOE_ROWS).astype(jnp.int32)
    yb = _expert_blocks(x2, tok_of_row, gate_rows[:, None], block_expert, n_used, w_gate, w_up, w_down)
    dest_slot_major = dest.reshape(t, 2).T.reshape(n)
    return _combine(h1, yb, dest_slot_major, mod, cond_of_tile, MOE_ROWS)


def kernel(x_prompt, x_sample, cache_na_k, cache_na_v, cache_gqa_k, cache_gqa_v, cache_diff_k, cache_diff_v, state_hgrn, c, c_ctx, w_mod, b_mod, norm1, norm2, w_in, w_out, hg_lb_logits, hg_onorm, na_qn, na_kn, na_rpb, gqa_qn, gqa_kn, df_qn, df_kn, df_lam, df_subln, router_w, router_b, w_gate, w_up, w_down):
    n_ctx, ctx_len, d = x_prompt.shape
    n_lat, lat_len, _ = x_sample.shape
    depth = w_in.shape[0]
    t_ctx = n_ctx * ctx_len
    t_all = t_ctx + n_lat * lat_len
    assert t_ctx % lat_len == 0 and lat_len % GRID_W == 0 and lat_len // GRID_W >= WIN_ROWS
    tm = next(m for m in (1024, 512, 256) if t_ctx % m == 0 and lat_len % m == 0)
    tm2 = min(tm, 512)
    lat_block0 = t_ctx // lat_len

    def cond_tile(tile_rows):
        def cond_of_tile(i):
            return jnp.where(i < t_ctx // tile_rows, 0, 1 + (i - t_ctx // tile_rows) // (lat_len // tile_rows))
        return cond_of_tile

    sm = jax.nn.softmax(hg_lb_logits.astype(F32), axis=0)
    lower = jnp.cumsum(sm, axis=0) - sm[0:1]
    mod_all = _modulation(jnp.concatenate([c_ctx[None, :], c], axis=0), w_mod, b_mod)
    mod_all = mod_all.reshape(depth, 1 + n_lat, 6, d)
    hgrn_consts = _hgrn_constants(HGRN_CHUNK)
    rope_c = _rope_tables(lat_len, HEAD_W)
    rope_d = _rope_tables(lat_len, DF_DQK)
    past = cache_diff_k.shape[4]
    cache_diff_k2 = cache_diff_k.transpose(0, 1, 2, 4, 3, 5).reshape(n_lat, depth, N_HEADS, past, HEAD_W)

    h = jnp.concatenate([x_prompt.reshape(t_ctx, d), x_sample.reshape(n_lat * lat_len, d)], axis=0)
    new_caches = [[] for _ in range(7)]
    for layer in range(depth):
        mod = mod_all[layer]
        lam_init = 0.8 - 0.6 * math.exp(-0.3 * layer)
        proj = _input_projection(h, mod, norm1[layer], w_in[layer].astype(BF16), cond_tile(tm), tm)
        a_ctx, st_ctx = _hgrn(proj, 0, n_ctx, ctx_len, lower[layer], hg_onorm[layer], hgrn_consts, None, True)
        gains = (na_qn[layer], na_kn[layer], gqa_qn[layer], gqa_kn[layer], df_qn[layer], df_kn[layer],
                 df_subln[layer], df_lam[layer])
        b_ctx, c_ctx_o, d_ctx, kb, vb, kc, vc, kd, vd = _context_attention(proj, n_ctx, ctx_len, gains, lam_init)
        kd = kd.reshape(n_ctx, N_HEADS, ctx_len, 2, DF_DQK).transpose(0, 1, 3, 2, 4)
        for store, tensor in zip(new_caches, (kb, vb, kc, vc, kd, vd, st_ctx)):
            store.append(tensor)
        a_lat, _ = _hgrn(proj, lat_block0, n_lat, lat_len, lower[layer], hg_onorm[layer], hgrn_consts,
                         state_hgrn[:, layer], False)
        b_lat = _latent_na(proj, lat_block0, n_lat, lat_len, cache_na_k, cache_na_v, layer,
                           _na_bias(na_rpb[layer]), na_qn[layer], na_kn[layer])
        c_lat = _latent_gqa(proj, lat_block0, n_lat, lat_len, cache_gqa_k, cache_gqa_v, layer, rope_c,
                            gqa_qn[layer], gqa_kn[layer])
        d_lat = _latent_diff(proj, lat_block0, n_lat, lat_len, cache_diff_k2, cache_diff_v, layer, rope_d,
                             df_qn[layer], df_kn[layer], df_subln[layer], df_lam[layer], lam_init)
        mixed = jnp.concatenate([jnp.concatenate([a_ctx, b_ctx, c_ctx_o, d_ctx], axis=1),
                                 jnp.concatenate([a_lat, b_lat, c_lat, d_lat], axis=1)], axis=0)
        h1, x2, idx_t, gate_t = _output_projection(mixed, w_out[layer].astype(BF16), h, mod, norm2[layer],
                                                   router_w, router_b, cond_tile(tm2), tm2)
        h = _moe(h1, x2, idx_t, gate_t, mod, w_gate[layer].astype(BF16), w_up[layer].astype(BF16),
                 w_down[layer].astype(BF16), cond_tile(MOE_ROWS))
    y_prompt = h[:t_ctx].reshape(n_ctx, ctx_len, d)
    y_sample = h[t_ctx:].reshape(n_lat, lat_len, d)
    stacked = [jnp.stack(per_layer, axis=1) for per_layer in new_caches]
    return (y_prompt, y_sample, *stacked)
```

```python
import functools
import math

import numpy as np
import jax
import jax.numpy as jnp
from jax import lax
from jax.experimental import pallas as pl
from jax.experimental.pallas import tpu as pltpu

D_MODEL = 2048
GRID_W = 64
GROUP_W = D_MODEL // 4
N_HEADS = 4
HEAD_W = GROUP_W // N_HEADS
GQA_KV_HEADS = 2
DF_DQK = HEAD_W // 2
WIN_ROWS = 8
WIN_COLS = 16
N_EXPERTS = 16
N_EXP_GROUPS = 4
EXP_PER_GROUP = N_EXPERTS // N_EXP_GROUPS
EXPERT_FF = D_MODEL // 4
ROPE_THETA = 10000.0
EPS = 1e-6
NEG_INF = -1e30
IN_WIDTH = 13 * GROUP_W

COL_A_Q, COL_A_FF, COL_A_FB, COL_A_I, COL_A_G = 0, 4, 8, 12, 16
COL_B_Q, COL_B_K, COL_B_V = 20, 24, 28
COL_C_Q, COL_C_K, COL_C_V = 32, 36, 38
COL_D_Q, COL_D_K, COL_D_V = 40, 44, 48

HGRN_CHUNK = 128
HGRN_UNROLL = 2
MOE_ROWS = 256
DISPATCH_CHUNK = 512
VMEM_LIMIT = 48 * 1024 * 1024

F32 = jnp.float32
BF16 = jnp.bfloat16
ANY_SPEC = pl.BlockSpec(memory_space=pl.ANY)


def _params(*sem):
    return pltpu.CompilerParams(dimension_semantics=sem, vmem_limit_bytes=VMEM_LIMIT)


def _sigmoid(x):
    return 1.0 / (1.0 + jnp.exp(-x))


def _silu(x):
    return x * _sigmoid(x)


def _rms(x, gain, n):
    return x * lax.rsqrt(jnp.sum(x * x, axis=-1, keepdims=True) * (1.0 / n) + EPS) * gain


def _dot(a, b):
    return jnp.dot(a, b, preferred_element_type=F32)


def _dot_nt(a, b):
    return lax.dot_general(a, b, (((1,), (1,)), ((), ())), preferred_element_type=F32)


def _dot_tn(a, b):
    return lax.dot_general(a, b, (((0,), (0,)), ((), ())), preferred_element_type=F32)


def _aligned(x, m):
    return x if isinstance(x, int) else pl.multiple_of(x, m)


def _alias_kwargs(n_inputs, prev, first_out):
    return ([ANY_SPEC] * len(prev), list(prev), {n_inputs + k: first_out + k for k in range(len(prev))})


def _mod_body(cond_ref, w_ref, b_ref, o_ref):
    w = w_ref[...]
    for c in range(cond_ref.shape[0]):
        s = _silu(cond_ref[c])
        o_ref[c:c + 1, :] = jnp.sum(w * s, axis=0, keepdims=True) + b_ref[...]


def _modulation(cond, w_mod, b_mod):
    depth, d, n6 = w_mod.shape
    nc = cond.shape[0]
    tn = 512
    return pl.pallas_call(
        _mod_body,
        grid=(depth, n6 // tn),
        in_specs=[pl.BlockSpec((nc, d, 1), lambda l, j: (0, 0, 0)),
                  pl.BlockSpec((None, d, tn), lambda l, j: (l, 0, j)),
                  pl.BlockSpec((None, 1, tn), lambda l, j: (l, 0, j))],
        out_specs=pl.BlockSpec((None, nc, tn), lambda l, j: (l, 0, j)),
        out_shape=jax.ShapeDtypeStruct((depth, nc, n6), F32),
        compiler_params=_params("parallel", "parallel"),
        name="adaln_modulation",
    )(cond[:, :, None], w_mod, b_mod[:, None, :])


def _inproj_body(h_ref, mod_ref, n1_ref, w_ref, o_ref, xn_ref):
    @pl.when(pl.program_id(1) == 0)
    def _():
        y = _rms(h_ref[...], n1_ref[...], D_MODEL)
        xn_ref[...] = (y * (1.0 + mod_ref[1:2, :]) + mod_ref[0:1, :]).astype(BF16)

    o_ref[...] = _dot(xn_ref[...], w_ref[...])


def _input_projection(h, mod, norm1, w_in_bf16, cond_of_tile, tm):
    t, d = h.shape
    n = w_in_bf16.shape[1]
    tn = 512
    return pl.pallas_call(
        _inproj_body,
        grid=(t // tm, n // tn),
        in_specs=[pl.BlockSpec((tm, d), lambda i, j: (i, 0)),
                  pl.BlockSpec((None, 6, d), lambda i, j: (cond_of_tile(i), 0, 0)),
                  pl.BlockSpec((1, d), lambda i, j: (0, 0)),
                  pl.BlockSpec((d, tn), lambda i, j: (0, j))],
        out_specs=pl.BlockSpec((tm, tn), lambda i, j: (i, j)),
        out_shape=jax.ShapeDtypeStruct((t, n), F32),
        scratch_shapes=[pltpu.VMEM((tm, d), BF16)],
        compiler_params=_params("parallel", "arbitrary"),
        name="norm_modulate_in_proj",
    )(h, mod, norm1[None, :], w_in_bf16)


def _hgrn_constants(c):
    nl = int(math.log2(c))
    idx = np.arange(c)
    e = np.zeros((nl + 2, c, c), np.float32)
    m = np.zeros((nl + 1, c, c), np.float32)
    e[0] = idx[None, :] <= idx[:, None]
    e[1] = idx[None, :] > idx[:, None]
    m[0] = np.eye(c)
    for li in range(nl):
        s = c >> (li + 1)
        parent = idx // (2 * s)
        right = (idx % (2 * s)) >= s
        ref = parent * 2 * s + s - 1
        for i in range(c):
            if right[i]:
                e[2 + li, i, ref[i] + 1:i + 1] = 1.0
            else:
                e[2 + li, i, i + 1:ref[i] + 1] = 1.0
        m[1 + li] = right[:, None] & ~right[None, :] & (parent[:, None] == parent[None, :])
    e2 = np.stack([e, e[:, ::-1, ::-1]]).reshape(2, (nl + 2) * c, c)
    m2 = np.stack([m, m[:, ::-1, ::-1]])
    return jnp.asarray(e2, BF16), jnp.asarray(m2, F32)


def _hgrn_body(*refs, seq, chunk, unroll, has_s0, emit_state, n_alias):
    q_ref, ff_ref, fb_ref, i_ref, g_ref, lb_ref, on_ref, e_ref, m_ref = refs[:9]
    pos = 9
    s0_ref = None
    if has_s0:
        s0_ref = refs[pos]
        pos += 1
    pos += n_alias
    o_ref = refs[pos]
    pos += 1
    if emit_state:
        st_ref = refs[pos]
        pos += 1
    of_ref, ob_ref = refs[pos], refs[pos + 1]
    c = chunk
    n_chunks = seq // c
    assert seq % c == 0 and n_chunks % unroll == 0
    n_levels = m_ref.shape[1] - 1
    gate_refs = (ff_ref, fb_ref)
    out_refs = (of_ref, ob_ref)

    def chunk_step(c0, d, st):
        rows = pl.ds(c0, c)
        lb = lb_ref[d:d + 1, :]
        f = lb + (1.0 - lb) * _sigmoid(gate_refs[d][rows, :])
        g = jnp.log(f)
        k = 1.0 - f
        q = _silu(q_ref[rows, :])
        v = i_ref[rows, :].astype(BF16)
        g_hi = g.astype(BF16)
        g_lo = (g - g_hi.astype(F32)).astype(BF16)
        g2 = _dot(e_ref[d], jnp.concatenate([g_hi, g_lo], axis=1))
        x = jnp.exp(g2[:, 0:HEAD_W] + g2[:, HEAD_W:2 * HEAD_W])
        x_cum = x[0:c]
        x_tail = x[c:2 * c]
        s = m_ref[d, 0] * _dot_nt(q.astype(BF16), k.astype(BF16))
        for lv in range(n_levels):
            x_l = x[(2 + lv) * c:(3 + lv) * c]
            s = s + m_ref[d, 1 + lv] * _dot_nt((q * x_l).astype(BF16), (k * x_l).astype(BF16))
        o = _dot_nt((q * x_cum).astype(BF16), st.astype(BF16)) + _dot(s.astype(BF16), v)
        out_refs[d][rows, :] = o
        total = x_cum[c - 1:c, :] if d == 0 else x_cum[0:1, :]
        return st * total + _dot_tn(v, (k * x_tail).astype(BF16))

    if has_s0:
        st_f0 = s0_ref[0].T
        st_b0 = s0_ref[1].T
    else:
        st_f0 = jnp.zeros((HEAD_W, HEAD_W), F32)
        st_b0 = jnp.zeros((HEAD_W, HEAD_W), F32)

    def loop(t, carry):
        st_f, st_b = carry
        for u in range(unroll):
            j = t * unroll + u
            st_f = chunk_step(_aligned(j * c, c), 0, st_f)
            st_b = chunk_step(_aligned((n_chunks - 1 - j) * c, c), 1, st_b)
        return st_f, st_b

    if n_chunks == unroll:
        st_f, st_b = loop(0, (st_f0, st_b0))
    else:
        st_f, st_b = lax.fori_loop(0, n_chunks // unroll, loop, (st_f0, st_b0))
    o = of_ref[...] + ob_ref[...]
    o_ref[...] = (_rms(o, on_ref[...], HEAD_W) * _silu(g_ref[...])).astype(o_ref.dtype)
    if emit_state:
        st_ref[0] = st_f.T
        st_ref[1] = st_b.T


def _hgrn(proj, row_block0, n_seq, seq, lower, onorm, consts, s0, mixed_prev, state_prev, layer, depth):
    e_mat, masks = consts
    latent = s0 is not None

    def col(cb):
        return pl.BlockSpec((seq, HEAD_W), lambda b, h, cb=cb: (row_block0 + b, cb + h))

    in_specs = [col(COL_A_Q), col(COL_A_FF), col(COL_A_FB), col(COL_A_I), col(COL_A_G),
                pl.BlockSpec((2, HEAD_W), lambda b, h: (0, h)),
                pl.BlockSpec((1, HEAD_W), lambda b, h: (0, 0)),
                pl.BlockSpec(e_mat.shape, lambda b, h: (0, 0, 0)),
                pl.BlockSpec(masks.shape, lambda b, h: (0, 0, 0, 0))]
    args = [proj, proj, proj, proj, proj, lower, onorm[None, :], e_mat, masks]
    if latent:
        in_specs.append(pl.BlockSpec((None, 2, None, HEAD_W, HEAD_W), lambda b, h: (b, 0, h, 0, 0)))
        args.append(s0)
        prev = [mixed_prev]
    else:
        prev = [] if state_prev is None else [state_prev]
    alias_specs, alias_args, aliases = _alias_kwargs(len(args), prev, 0 if latent else 1)
    out_specs = [pl.BlockSpec((seq, HEAD_W), lambda b, h: (row_block0 + b, h))]
    out_shape = [jax.ShapeDtypeStruct((proj.shape[0], GROUP_W), BF16)]
    if not latent:
        out_specs.append(pl.BlockSpec((None, None, 2, None, HEAD_W, HEAD_W), lambda b, h: (b, layer, 0, h, 0, 0)))
        out_shape.append(jax.ShapeDtypeStruct((n_seq, depth, 2, N_HEADS, HEAD_W, HEAD_W), F32))
    res = pl.pallas_call(
        functools.partial(_hgrn_body, seq=seq, chunk=HGRN_CHUNK, unroll=HGRN_UNROLL, has_s0=latent,
                          emit_state=not latent, n_alias=len(prev)),
        grid=(n_seq, N_HEADS),
        in_specs=in_specs + alias_specs, out_specs=out_specs, out_shape=out_shape,
        input_output_aliases=aliases,
        scratch_shapes=[pltpu.VMEM((seq, HEAD_W), F32), pltpu.VMEM((seq, HEAD_W), F32)],
        compiler_params=_params("parallel", "parallel"),
        name="hgrn2_latent" if latent else "hgrn2_context",
    )(*args, *alias_args)
    return (res[0], None) if latent else res


def _softmax_pv(scores, values):
    mx = functools.reduce(jnp.maximum, [jnp.max(s, axis=-1, keepdims=True) for s in scores])
    es = [jnp.exp(s - mx) for s in scores]
    den = functools.reduce(lambda a, b: a + b, [jnp.sum(e, axis=-1, keepdims=True) for e in es])
    num = functools.reduce(lambda a, b: a + b, [_dot(e.astype(BF16), v) for e, v in zip(es, values)])
    return num / den


def _diff_weights(s0, s1, lam):
    e0 = jnp.exp(s0 - jnp.max(s0, axis=-1, keepdims=True))
    e1 = jnp.exp(s1 - jnp.max(s1, axis=-1, keepdims=True))
    p0 = e0 / jnp.sum(e0, axis=-1, keepdims=True)
    p1 = e1 / jnp.sum(e1, axis=-1, keepdims=True)
    return (p0 - lam * p1).astype(BF16)


def _lane_lt(shape, n):
    return lax.broadcasted_iota(jnp.int32, shape, len(shape) - 1) < n


def _rms_halves(x, gain2):
    lo = _lane_lt(x.shape, DF_DQK)
    sq = x * x
    ss_lo = jnp.sum(jnp.where(lo, sq, 0.0), axis=-1, keepdims=True)
    ss_hi = jnp.sum(sq, axis=-1, keepdims=True) - ss_lo
    inv = jnp.where(lo, lax.rsqrt(ss_lo * (1.0 / DF_DQK) + EPS), lax.rsqrt(ss_hi * (1.0 / DF_DQK) + EPS))
    return x * inv * gain2


def _lambda(lam_ref, lam_init):
    l = lam_ref[...]
    return (jnp.exp(jnp.sum(l[0:1] * l[1:2], axis=-1, keepdims=True))
            - jnp.exp(jnp.sum(l[2:3] * l[3:4], axis=-1, keepdims=True)) + lam_init)


N_CTX_ATTN_INPUTS = 17


def _ctx_attn_body(*refs, lam_init):
    (bq_ref, bk_ref, bv_ref, cq_ref, ck_ref, cv_ref, dq_ref, dk_ref, dv_ref,
     naq_ref, nak_ref, gq_ref, gk_ref, dfq_ref, dfk_ref, sub_ref, lam_ref) = refs[:N_CTX_ATTN_INPUTS]
    ob_ref, oc_ref, od_ref, kb_ref, vb_ref, kc_ref, vc_ref, kd_ref, vd_ref = refs[-9:]
    scale = HEAD_W ** -0.5
    kb = _rms(bk_ref[...], nak_ref[...], HEAD_W)
    vb = bv_ref[...]
    kb_ref[...] = kb
    vb_ref[...] = vb
    qb = (_rms(bq_ref[...], naq_ref[...], HEAD_W) * scale).astype(BF16)
    ob_ref[...] = _softmax_pv([_dot_nt(qb, kb.astype(BF16))], [vb.astype(BF16)]).astype(ob_ref.dtype)
    kc = _rms(ck_ref[...], gk_ref[...], HEAD_W)
    vc = cv_ref[...]
    kc_ref[...] = kc
    vc_ref[...] = vc
    qc = (_rms(cq_ref[...], gq_ref[...], HEAD_W) * scale).astype(BF16)
    oc_ref[...] = _softmax_pv([_dot_nt(qc, kc.astype(BF16))], [vc.astype(BF16)]).astype(oc_ref.dtype)
    kd = _rms_halves(dk_ref[...], dfk_ref[...])
    vd = dv_ref[...]
    kd_ref[0] = kd[:, 0:DF_DQK]
    kd_ref[1] = kd[:, DF_DQK:2 * DF_DQK]
    vd_ref[...] = vd
    qd = (_rms_halves(dq_ref[...], dfq_ref[...]) * (DF_DQK ** -0.5)).astype(BF16)
    lo = _lane_lt(kd.shape, DF_DQK)
    s0 = _dot_nt(qd, jnp.where(lo, kd, 0.0).astype(BF16))
    s1 = _dot_nt(qd, jnp.where(lo, 0.0, kd).astype(BF16))
    od = _dot(_diff_weights(s0, s1, _lambda(lam_ref, lam_init)), vd.astype(BF16))
    od_ref[...] = (_rms(od, sub_ref[...], HEAD_W) * (1.0 - lam_init)).astype(od_ref.dtype)


def _context_attention(proj, n_seq, seq, gains, lam_init, layer, depth, caches_prev):
    na_qn, na_kn, gqa_qn, gqa_kn, df_qn, df_kn, df_subln, df_lam = gains
    group = N_HEADS // GQA_KV_HEADS

    def col(cb, div=1):
        return pl.BlockSpec((seq, HEAD_W), lambda b, h, cb=cb, div=div: (b, cb + h // div))

    def vec(width=HEAD_W):
        return pl.BlockSpec((1, width), lambda b, h: (0, 0))

    def cache(div=1):
        return pl.BlockSpec((None, None, None, seq, HEAD_W), lambda b, h, div=div: (b, layer, h // div, 0, 0))

    mixed = pl.BlockSpec((seq, HEAD_W), lambda b, h: (b, h))
    mixed_shape = jax.ShapeDtypeStruct((proj.shape[0], GROUP_W), BF16)
    cache4 = jax.ShapeDtypeStruct((n_seq, depth, N_HEADS, seq, HEAD_W), F32)
    cache2 = jax.ShapeDtypeStruct((n_seq, depth, GQA_KV_HEADS, seq, HEAD_W), F32)
    cache_dk = jax.ShapeDtypeStruct((n_seq, depth, N_HEADS, 2, seq, DF_DQK), F32)
    args = [proj] * 9 + [na_qn[None, :], na_kn[None, :], gqa_qn[None, :], gqa_kn[None, :],
                         jnp.tile(df_qn, 2)[None, :], jnp.tile(df_kn, 2)[None, :], df_subln[None, :], df_lam]
    assert len(args) == N_CTX_ATTN_INPUTS
    alias_specs, alias_args, aliases = _alias_kwargs(len(args), caches_prev, 3)
    return pl.pallas_call(
        functools.partial(_ctx_attn_body, lam_init=lam_init),
        grid=(n_seq, N_HEADS),
        in_specs=[col(COL_B_Q), col(COL_B_K), col(COL_B_V),
                  col(COL_C_Q), col(COL_C_K, group), col(COL_C_V, group),
                  col(COL_D_Q), col(COL_D_K), col(COL_D_V),
                  vec(), vec(), vec(), vec(), vec(), vec(), vec(),
                  pl.BlockSpec((4, DF_DQK), lambda b, h: (0, 0))] + alias_specs,
        out_specs=[mixed, mixed, mixed, cache(), cache(), cache(group), cache(group),
                   pl.BlockSpec((None, None, None, 2, seq, DF_DQK), lambda b, h: (b, layer, h, 0, 0, 0)),
                   cache()],
        out_shape=[mixed_shape, mixed_shape, mixed_shape, cache4, cache4, cache2, cache2, cache_dk, cache4],
        input_output_aliases=aliases,
        compiler_params=_params("parallel", "arbitrary"),
        name="context_attention",
    )(*args, *alias_args)


def _rope_tables(n_tokens, rot_dim):
    t = np.arange(n_tokens)
    row = (t // GRID_W).astype(np.float32)
    col = (t % GRID_W).astype(np.float32)
    n_freq = rot_dim // 4
    inv = (np.float32(ROPE_THETA) ** (-np.arange(n_freq, dtype=np.float32) / np.float32(n_freq))).astype(np.float32)
    ang = np.concatenate([row[:, None] * inv, col[:, None] * inv], axis=-1).astype(np.float32)
    cos, sin, zero = np.cos(ang), np.sin(ang), np.zeros_like(ang)
    reps = HEAD_W // rot_dim
    a = np.tile(np.concatenate([cos, cos], axis=-1), (1, reps))
    b = np.tile(np.concatenate([-sin, zero], axis=-1), (1, reps))
    c = np.tile(np.concatenate([zero, sin], axis=-1), (1, reps))
    return jnp.asarray(np.stack([a, b, c]), F32)


def _rope(x, tab_ref, half):
    return (x * tab_ref[0] + pltpu.roll(x, HEAD_W - half, 1) * tab_ref[1]
            + pltpu.roll(x, half, 1) * tab_ref[2])


def _na_body(q_ref, k_ref, v_ref, ck_ref, cv_ref, bias_ref, qn_ref, kn_ref, prev_ref, o_ref,
             qs_ref, ks_ref, vs_ref, *, seq):
    del prev_ref
    rows = seq // GRID_W
    n_win = WIN_ROWS * GRID_W
    qs_ref[...] = (_rms(q_ref[...], qn_ref[...], HEAD_W) * (HEAD_W ** -0.5)).astype(BF16)
    ks_ref[...] = _rms(k_ref[...], kn_ref[...], HEAD_W).astype(BF16)
    vs_ref[...] = v_ref[...].astype(BF16)
    ck = ck_ref[...].astype(BF16)
    cv = cv_ref[...].astype(BF16)

    def row_step(r, carry):
        start = jnp.clip(r - WIN_ROWS // 2, 0, rows - WIN_ROWS)
        win = pl.ds(pl.multiple_of(start * GRID_W, GRID_W), n_win)
        qrows = pl.ds(pl.multiple_of(r * GRID_W, GRID_W), GRID_W)
        q = qs_ref[qrows, :]
        s_win = _dot_nt(q, ks_ref[win, :]) + bias_ref[start - r + (WIN_ROWS - 1)]
        s_ctx = _dot_nt(q, ck)
        o_ref[qrows, :] = _softmax_pv([s_win, s_ctx], [vs_ref[win, :], cv]).astype(o_ref.dtype)
        return carry

    lax.fori_loop(0, rows, row_step, 0)


def _na_bias(rpb):
    col = np.arange(GRID_W)
    col_start = np.clip(col - WIN_COLS // 2, 0, GRID_W - WIN_COLS)
    col_ok = (col[None, :] >= col_start[:, None]) & (col[None, :] < col_start[:, None] + WIN_COLS)
    dc = np.clip(col[None, :] - col[:, None] + WIN_COLS - 1, 0, 2 * WIN_COLS - 2).reshape(-1)
    onehot = (np.arange(2 * WIN_COLS - 1)[:, None] == dc[None, :]).astype(np.float32)
    per_dr = jnp.einsum('hdc,cn->hdn', rpb.astype(F32), jnp.asarray(onehot), precision=lax.Precision.HIGHEST)
    per_dr = jnp.where(col_ok[None, None], per_dr.reshape(rpb.shape[0], -1, GRID_W, GRID_W), NEG_INF)
    wins = jnp.stack([per_dr[:, o:o + WIN_ROWS] for o in range(WIN_ROWS)], axis=1)
    return wins.transpose(0, 1, 3, 2, 4).reshape(rpb.shape[0], WIN_ROWS, GRID_W, WIN_ROWS * GRID_W)


def _latent_na(proj, row_block0, n_seq, seq, cache_k, cache_v, layer, bias, na_qn, na_kn, mixed_prev):
    past = cache_k.shape[3]

    def col(cb):
        return pl.BlockSpec((seq, HEAD_W), lambda b, h, cb=cb: (row_block0 + b, cb + h))

    cache = pl.BlockSpec((None, None, None, past, HEAD_W), lambda b, h: (b, layer, h, 0, 0))
    vec = pl.BlockSpec((1, HEAD_W), lambda b, h: (0, 0))
    return pl.pallas_call(
        functools.partial(_na_body, seq=seq),
        grid=(n_seq, N_HEADS),
        in_specs=[col(COL_B_Q), col(COL_B_K), col(COL_B_V), cache, cache,
                  pl.BlockSpec((None, WIN_ROWS, GRID_W, WIN_ROWS * GRID_W), lambda b, h: (h, 0, 0, 0)),
                  vec, vec, ANY_SPEC],
        out_specs=pl.BlockSpec((seq, HEAD_W), lambda b, h: (row_block0 + b, h)),
        out_shape=jax.ShapeDtypeStruct(mixed_prev.shape, BF16),
        input_output_aliases={8: 0},
        scratch_shapes=[pltpu.VMEM((seq, HEAD_W), BF16)] * 3,
        compiler_params=_params("parallel", "parallel"),
        name="latent_neighbourhood_attention",
    )(proj, proj, proj, cache_k, cache_v, bias, na_qn[None, :], na_kn[None, :], mixed_prev)


def _gqa_body(q_ref, k_ref, v_ref, ck_ref, cv_ref, rope_ref, qn_ref, kn_ref, prev_ref, o_ref,
              qs_ref, ks_ref, vs_ref, *, seq, tq):
    del prev_ref
    group = N_HEADS // GQA_KV_HEADS
    half = HEAD_W // 2
    ks_ref[0:seq, :] = _rope(_rms(k_ref[...], kn_ref[...], HEAD_W), rope_ref, half).astype(BF16)
    ks_ref[seq:, :] = ck_ref[...].astype(BF16)
    vs_ref[0:seq, :] = v_ref[...].astype(BF16)
    vs_ref[seq:, :] = cv_ref[...].astype(BF16)
    for g in range(group):
        q = _rms(q_ref[:, g * HEAD_W:(g + 1) * HEAD_W], qn_ref[...], HEAD_W) * (HEAD_W ** -0.5)
        qs_ref[g] = _rope(q, rope_ref, half).astype(BF16)
    kk = ks_ref[...]
    vv = vs_ref[...]
    for g in range(group):
        def q_step(i, carry, g=g):
            qrows = pl.ds(pl.multiple_of(i * tq, tq), tq)
            o = _softmax_pv([_dot_nt(qs_ref[g, qrows, :], kk)], [vv])
            o_ref[qrows, g * HEAD_W:(g + 1) * HEAD_W] = o.astype(o_ref.dtype)
            return carry

        lax.fori_loop(0, seq // tq, q_step, 0)


def _latent_gqa(proj, row_block0, n_seq, seq, cache_k, cache_v, layer, rope, gqa_qn, gqa_kn, mixed_prev):
    past = cache_k.shape[3]
    group = N_HEADS // GQA_KV_HEADS
    tq = 256
    cache = pl.BlockSpec((None, None, None, past, HEAD_W), lambda b, n: (b, layer, n, 0, 0))
    vec = pl.BlockSpec((1, HEAD_W), lambda b, n: (0, 0))
    return pl.pallas_call(
        functools.partial(_gqa_body, seq=seq, tq=tq),
        grid=(n_seq, GQA_KV_HEADS),
        in_specs=[pl.BlockSpec((seq, group * HEAD_W), lambda b, n: (row_block0 + b, COL_C_Q // group + n)),
                  pl.BlockSpec((seq, HEAD_W), lambda b, n: (row_block0 + b, COL_C_K + n)),
                  pl.BlockSpec((seq, HEAD_W), lambda b, n: (row_block0 + b, COL_C_V + n)),
                  cache, cache,
                  pl.BlockSpec((3, seq, HEAD_W), lambda b, n: (0, 0, 0)),
                  vec, vec, ANY_SPEC],
        out_specs=pl.BlockSpec((seq, group * HEAD_W), lambda b, n: (row_block0 + b, n)),
        out_shape=jax.ShapeDtypeStruct(mixed_prev.shape, BF16),
        input_output_aliases={8: 0},
        scratch_shapes=[pltpu.VMEM((group, seq, HEAD_W), BF16),
                        pltpu.VMEM((seq + past, HEAD_W), BF16),
                        pltpu.VMEM((seq + past, HEAD_W), BF16)],
        compiler_params=_params("parallel", "parallel"),
        name="latent_gqa_attention",
    )(proj, proj, proj, cache_k, cache_v, rope, gqa_qn[None, :], gqa_kn[None, :], mixed_prev)


def _diff_body(q_ref, k_ref, v_ref, ck_ref, cv_ref, rope_ref, qn_ref, kn_ref, sub_ref, lam_ref, prev_ref, o_ref,
               qs_ref, k0_ref, k1_ref, vs_ref, *, seq, tq, lam_init):
    del prev_ref
    half = DF_DQK // 2
    k = _rope(_rms_halves(k_ref[...], kn_ref[...]), rope_ref, half)
    lo = _lane_lt(k.shape, DF_DQK)
    k0_ref[0:seq, :] = jnp.where(lo, k, 0.0).astype(BF16)
    k1_ref[0:seq, :] = jnp.where(lo, 0.0, k).astype(BF16)
    ck = ck_ref[...]
    lo_c = _lane_lt(ck.shape, DF_DQK)
    k0_ref[seq:, :] = jnp.where(lo_c, ck, 0.0).astype(BF16)
    k1_ref[seq:, :] = jnp.where(lo_c, 0.0, ck).astype(BF16)
    vs_ref[0:seq, :] = v_ref[...].astype(BF16)
    vs_ref[seq:, :] = cv_ref[...].astype(BF16)
    q = _rms_halves(q_ref[...], qn_ref[...]) * (DF_DQK ** -0.5)
    qs_ref[...] = _rope(q, rope_ref, half).astype(BF16)
    lam = _lambda(lam_ref, lam_init)
    k0 = k0_ref[...]
    k1 = k1_ref[...]
    vv = vs_ref[...]

    def q_step(i, carry):
        qrows = pl.ds(pl.multiple_of(i * tq, tq), tq)
        qb = qs_ref[qrows, :]
        o = _dot(_diff_weights(_dot_nt(qb, k0), _dot_nt(qb, k1), lam), vv)
        o_ref[qrows, :] = (_rms(o, sub_ref[...], HEAD_W) * (1.0 - lam_init)).astype(o_ref.dtype)
        return carry

    lax.fori_loop(0, seq // tq, q_step, 0)


def _latent_diff(proj, row_block0, n_seq, seq, cache_k2, cache_v, layer, rope, df_qn, df_kn, df_subln, df_lam,
                 lam_init, mixed_prev):
    past = cache_k2.shape[3]
    tq = 256

    def col(cb):
        return pl.BlockSpec((seq, HEAD_W), lambda b, h, cb=cb: (row_block0 + b, cb + h))

    cache = pl.BlockSpec((None, None, None, past, HEAD_W), lambda b, h: (b, layer, h, 0, 0))
    vec = pl.BlockSpec((1, HEAD_W), lambda b, h: (0, 0))
    kv_scratch = pltpu.VMEM((seq + past, HEAD_W), BF16)
    return pl.pallas_call(
        functools.partial(_diff_body, seq=seq, tq=tq, lam_init=lam_init),
        grid=(n_seq, N_HEADS),
        in_specs=[col(COL_D_Q), col(COL_D_K), col(COL_D_V), cache, cache,
                  pl.BlockSpec((3, seq, HEAD_W), lambda b, h: (0, 0, 0)),
                  vec, vec, vec, pl.BlockSpec((4, DF_DQK), lambda b, h: (0, 0)), ANY_SPEC],
        out_specs=pl.BlockSpec((seq, HEAD_W), lambda b, h: (row_block0 + b, h)),
        out_shape=jax.ShapeDtypeStruct(mixed_prev.shape, BF16),
        input_output_aliases={10: 0},
        scratch_shapes=[pltpu.VMEM((seq, HEAD_W), BF16), kv_scratch, kv_scratch, kv_scratch],
        compiler_params=_params("parallel", "parallel"),
        name="latent_diff_attention",
    )(proj, proj, proj, cache_k2, cache_v, rope, jnp.tile(df_qn, 2)[None, :], jnp.tile(df_kn, 2)[None, :],
      df_subln[None, :], df_lam, mixed_prev)


def _first_max(vals):
    best = vals[0]
    idx = jnp.zeros(best.shape, jnp.int32)
    for i in range(1, len(vals)):
        better = vals[i] > best
        best = jnp.where(better, vals[i], best)
        idx = jnp.where(better, i, idx)
    return best, idx


def _pick(vals, idx):
    out = vals[0]
    for i in range(1, len(vals)):
        out = jnp.where(idx == i, vals[i], out)
    return out


def _outproj_body(ma_ref, mb_ref, mc_ref, md_ref, w_ref, h_ref, mod_ref, n2_ref, rw_ref, rb_ref,
                  h1_ref, x2_ref, idx_ref, gate_ref):
    y = _dot(ma_ref[...], w_ref[0:GROUP_W, :])
    for g, m_ref in enumerate((mb_ref, mc_ref, md_ref), start=1):
        y = y + _dot(m_ref[...], w_ref[g * GROUP_W:(g + 1) * GROUP_W, :])
    h1 = h_ref[...] + mod_ref[2:3, :] * y
    h1_ref[...] = h1
    x2 = _rms(h1, n2_ref[...], D_MODEL) * (1.0 + mod_ref[4:5, :]) + mod_ref[3:4, :]
    x2_ref[...] = x2
    logits = lax.dot_general(rw_ref[...], x2, (((1,), (1,)), ((), ())), precision=lax.Precision.HIGHEST,
                             preferred_element_type=F32)
    aff_all = _sigmoid(logits)
    sel_all = aff_all + rb_ref[...]
    aff = [aff_all[e:e + 1, :] for e in range(N_EXPERTS)]
    sel = [sel_all[e:e + 1, :] for e in range(N_EXPERTS)]
    neg = jnp.full(sel[0].shape, -jnp.inf, F32)
    scores = []
    for g in range(N_EXP_GROUPS):
        grp = sel[g * EXP_PER_GROUP:(g + 1) * EXP_PER_GROUP]
        m1, i1 = _first_max(grp)
        m2, _ = _first_max([jnp.where(i1 == j, neg, grp[j]) for j in range(EXP_PER_GROUP)])
        scores.append(m1 + m2)
    _, g_best = _first_max(scores)
    in_sel = [_pick([sel[g * EXP_PER_GROUP + j] for g in range(N_EXP_GROUPS)], g_best)
              for j in range(EXP_PER_GROUP)]
    in_aff = [_pick([aff[g * EXP_PER_GROUP + j] for g in range(N_EXP_GROUPS)], g_best)
              for j in range(EXP_PER_GROUP)]
    _, l1 = _first_max(in_sel)
    _, l2 = _first_max([jnp.where(l1 == j, neg, in_sel[j]) for j in range(EXP_PER_GROUP)])
    w1 = _pick(in_aff, l1)
    w2 = _pick(in_aff, l2)
    idx_ref[0:1, :] = g_best * EXP_PER_GROUP + l1
    idx_ref[1:2, :] = g_best * EXP_PER_GROUP + l2
    gate_ref[0:1, :] = w1 / (w1 + w2)
    gate_ref[1:2, :] = w2 / (w1 + w2)


def _output_projection(mixed4, w_out_bf16, h, mod, norm2, router_w, router_b, cond_of_tile, tm):
    t, d = h.shape
    slab = pl.BlockSpec((tm, GROUP_W), lambda i: (i, 0))
    return pl.pallas_call(
        _outproj_body,
        grid=(t // tm,),
        in_specs=[slab, slab, slab, slab,
                  pl.BlockSpec((d, d), lambda i: (0, 0)),
                  pl.BlockSpec((tm, d), lambda i: (i, 0)),
                  pl.BlockSpec((None, 6, d), lambda i: (cond_of_tile(i), 0, 0)),
                  pl.BlockSpec((1, d), lambda i: (0, 0)),
                  pl.BlockSpec((N_EXPERTS, d), lambda i: (0, 0)),
                  pl.BlockSpec((N_EXPERTS, 1), lambda i: (0, 0))],
        out_specs=[pl.BlockSpec((tm, d), lambda i: (i, 0)),
                   pl.BlockSpec((tm, d), lambda i: (i, 0)),
                   pl.BlockSpec((2, tm), lambda i: (0, i)),
                   pl.BlockSpec((2, tm), lambda i: (0, i))],
        out_shape=[jax.ShapeDtypeStruct((t, d), F32), jax.ShapeDtypeStruct((t, d), F32),
                   jax.ShapeDtypeStruct((2, t), jnp.int32), jax.ShapeDtypeStruct((2, t), F32)],
        compiler_params=_params("parallel"),
        name="out_proj_residual_router",
    )(*mixed4, w_out_bf16, h, mod, norm2[None, :], router_w.T, router_b[:, None])


def _row_copy(src_hbm, src_row, dst, dst_row, sem):
    return pltpu.make_async_copy(src_hbm.at[pl.ds(src_row, 1), :], dst.at[pl.ds(dst_row, 1), :], sem)


def _rows_wait(src_hbm, dst, sem, n_rows):
    pltpu.make_async_copy(src_hbm.at[pl.ds(0, n_rows), :], dst.at[pl.ds(0, n_rows), :], sem).wait()


def _dispatch_body(dest_ref, ps_ref, pe_ref, x_hbm, xs_hbm, zero_ref, sem, zsem, *, n_tok):
    zero_ref[...] = jnp.zeros(zero_ref.shape, zero_ref.dtype)

    def zero_copy(e):
        first = pl.multiple_of(pe_ref[e] - MOE_ROWS, MOE_ROWS)
        return pltpu.make_async_copy(zero_ref, xs_hbm.at[pl.ds(first, MOE_ROWS), :], zsem)

    for e in range(N_EXPERTS):
        @pl.when(pe_ref[e] > ps_ref[e])
        def _(e=e):
            zero_copy(e).start()
    for e in range(N_EXPERTS):
        @pl.when(pe_ref[e] > ps_ref[e])
        def _(e=e):
            zero_copy(e).wait()

    n_chunks = 2 * n_tok // DISPATCH_CHUNK

    def issue(ci):
        def body(r, carry):
            a = ci * DISPATCH_CHUNK + r
            tok = jnp.where(a >= n_tok, a - n_tok, a)
            _row_copy(x_hbm, tok, xs_hbm, dest_ref[a], sem).start()
            return carry

        lax.fori_loop(0, DISPATCH_CHUNK, body, 0, unroll=8)

    issue(0)

    def chunk_step(ci, carry):
        issue(ci)
        _rows_wait(x_hbm, xs_hbm, sem, DISPATCH_CHUNK)
        return carry

    lax.fori_loop(1, n_chunks, chunk_step, 0)
    _rows_wait(x_hbm, xs_hbm, sem, DISPATCH_CHUNK)


def _dispatch(x2, dest, pad_start, pad_end, n_rows):
    t, d = x2.shape
    assert (2 * t) % DISPATCH_CHUNK == 0
    grid_spec = pltpu.PrefetchScalarGridSpec(
        num_scalar_prefetch=3,
        grid=(1,),
        in_specs=[ANY_SPEC],
        out_specs=ANY_SPEC,
        scratch_shapes=[pltpu.VMEM((MOE_ROWS, d), F32), pltpu.SemaphoreType.DMA, pltpu.SemaphoreType.DMA],
    )
    return pl.pallas_call(
        functools.partial(_dispatch_body, n_tok=t),
        grid_spec=grid_spec,
        out_shape=jax.ShapeDtypeStruct((n_rows, d), F32),
        compiler_params=_params("arbitrary"),
        name="moe_dispatch",
    )(dest, pad_start, pad_end, x2)


def _expert_body(be_ref, nb_ref, x_ref, wg_ref, wu_ref, wd_ref, o_ref, wg_bf, wu_bf, wd_bf):
    i = pl.program_id(0)

    @pl.when(i < nb_ref[0])
    def _():
        @pl.when((i == 0) | (be_ref[i] != be_ref[jnp.maximum(i - 1, 0)]))
        def _():
            wg_bf[...] = wg_ref[...].astype(BF16)
            wu_bf[...] = wu_ref[...].astype(BF16)
            wd_bf[...] = wd_ref[...].astype(BF16)

        x = x_ref[...].astype(BF16)
        hdn = _silu(_dot(x, wg_bf[...])) * _dot(x, wu_bf[...])
        o_ref[...] = _dot(hdn.astype(BF16), wd_bf[...])

    @pl.when(i >= nb_ref[0])
    def _():
        o_ref[...] = jnp.zeros(o_ref.shape, o_ref.dtype)


def _expert_blocks(xs, block_expert, n_used, w_gate, w_up, w_down):
    n_rows, d = xs.shape
    ff = w_gate.shape[-1]
    grid_spec = pltpu.PrefetchScalarGridSpec(
        num_scalar_prefetch=2,
        grid=(n_rows // MOE_ROWS,),
        in_specs=[pl.BlockSpec((MOE_ROWS, d), lambda i, be, nb: (jnp.minimum(i, nb[0] - 1), 0)),
                  pl.BlockSpec((None, d, ff), lambda i, be, nb: (be[i], 0, 0)),
                  pl.BlockSpec((None, d, ff), lambda i, be, nb: (be[i], 0, 0)),
                  pl.BlockSpec((None, ff, d), lambda i, be, nb: (be[i], 0, 0))],
        out_specs=pl.BlockSpec((MOE_ROWS, d), lambda i, be, nb: (i, 0)),
        scratch_shapes=[pltpu.VMEM((d, ff), BF16), pltpu.VMEM((d, ff), BF16), pltpu.VMEM((ff, d), BF16)],
    )
    return pl.pallas_call(
        _expert_body,
        grid_spec=grid_spec,
        out_shape=jax.ShapeDtypeStruct((n_rows, d), F32),
        compiler_params=_params("arbitrary"),
        name="moe_expert_blocks",
    )(block_expert, n_used, xs, w_gate, w_up, w_down)


def _combine_body(dest_ref, h_ref, gate_ref, y_hbm, mod_ref, *rest, n_tok, tile, ctx_tiles):
    out_refs, (ybuf, sem) = rest[:-2], rest[-2:]
    i = pl.program_id(0)
    n_tiles = pl.num_programs(0)

    def start(blk, slot):
        for k in range(2):
            def body(r, carry, k=k):
                _row_copy(y_hbm, dest_ref[k * n_tok + blk * tile + r], ybuf.at[slot, k], r, sem.at[slot]).start()
                return carry

            lax.fori_loop(0, tile, body, 0, unroll=8)

    @pl.when(i == 0)
    def _():
        start(0, 0)

    @pl.when(i + 1 < n_tiles)
    def _():
        start(i + 1, (i + 1) % 2)

    slot = i % 2
    for k in range(2):
        _rows_wait(y_hbm, ybuf.at[slot, k], sem.at[slot], tile)
    gate = gate_ref[...]
    out = h_ref[...] + mod_ref[5:6, :] * (gate[:, 0:1] * ybuf[slot, 0] + gate[:, 1:2] * ybuf[slot, 1])
    if ctx_tiles is None:
        out_refs[0][...] = out
    else:
        @pl.when(i < ctx_tiles)
        def _():
            out_refs[0][...] = out

        @pl.when(i >= ctx_tiles)
        def _():
            out_refs[1][...] = out


def _combine(h1, yb, dest, gates, mod, cond_of_tile, tile, split_rows):
    t, d = h1.shape
    row_tile = pl.BlockSpec((tile, d), lambda i, dst: (i, 0))
    if split_rows is None:
        ctx_tiles = None
        out_specs = row_tile
        out_shape = jax.ShapeDtypeStruct((t, d), F32)
    else:
        ctx_tiles = split_rows // tile
        out_specs = [pl.BlockSpec((tile, d), lambda i, dst: (jnp.minimum(i, ctx_tiles - 1), 0)),
                     pl.BlockSpec((tile, d), lambda i, dst: (jnp.maximum(i - ctx_tiles, 0), 0))]
        out_shape = [jax.ShapeDtypeStruct((split_rows, d), F32), jax.ShapeDtypeStruct((t - split_rows, d), F32)]
    grid_spec = pltpu.PrefetchScalarGridSpec(
        num_scalar_prefetch=1,
        grid=(t // tile,),
        in_specs=[row_tile,
                  pl.BlockSpec((tile, 2), lambda i, dst: (i, 0)),
                  ANY_SPEC,
                  pl.BlockSpec((None, 6, d), lambda i, dst: (cond_of_tile(i), 0, 0))],
        out_specs=out_specs,
        scratch_shapes=[pltpu.VMEM((2, 2, tile, d), F32), pltpu.SemaphoreType.DMA((2,))],
    )
    return pl.pallas_call(
        functools.partial(_combine_body, n_tok=t, tile=tile, ctx_tiles=ctx_tiles),
        grid_spec=grid_spec,
        out_shape=out_shape,
        compiler_params=_params("arbitrary"),
        name="moe_gated_residual",
    )(dest, h1, gates, yb, mod)


def _moe(h1, x2, idx_t, gate_t, mod, w_gate, w_up, w_down, cond_of_tile, split_rows):
    t, d = h1.shape
    n = 2 * t
    experts = idx_t.reshape(n)
    onehot = (experts[:, None] == jnp.arange(N_EXPERTS, dtype=jnp.int32)[None, :]).astype(jnp.int32)
    csum = jnp.cumsum(onehot, axis=0)
    counts = csum[-1]
    rank = jnp.sum((csum - onehot) * onehot, axis=1)
    padded = (counts + MOE_ROWS - 1) // MOE_ROWS * MOE_ROWS
    pad_end = jnp.cumsum(padded).astype(jnp.int32)
    pad_start = pad_end - padded
    dest = (pad_start[experts] + rank).astype(jnp.int32)
    n_blocks = (n + N_EXPERTS * (MOE_ROWS - 1) + MOE_ROWS - 1) // MOE_ROWS
    block_expert = jnp.minimum(
        jnp.searchsorted(pad_end, jnp.arange(n_blocks, dtype=jnp.int32) * MOE_ROWS, side='right'),
        N_EXPERTS - 1).astype(jnp.int32)
    n_used = (pad_end[-1:] // MOE_ROWS).astype(jnp.int32)
    xs = _dispatch(x2, dest, pad_start, pad_end, n_blocks * MOE_ROWS)
    yb = _expert_blocks(xs, block_expert, n_used, w_gate, w_up, w_down)
    return _combine(h1, yb, dest, gate_t.T, mod, cond_of_tile, MOE_ROWS, split_rows)


def kernel(x_prompt, x_sample, cache_na_k, cache_na_v, cache_gqa_k, cache_gqa_v, cache_diff_k, cache_diff_v, state_hgrn, c, c_ctx, w_mod, b_mod, norm1, norm2, w_in, w_out, hg_lb_logits, hg_onorm, na_qn, na_kn, na_rpb, gqa_qn, gqa_kn, df_qn, df_kn, df_lam, df_subln, router_w, router_b, w_gate, w_up, w_down):
    n_ctx, ctx_len, d = x_prompt.shape
    n_lat, lat_len, _ = x_sample.shape
    depth = w_in.shape[0]
    t_ctx = n_ctx * ctx_len
    assert t_ctx % lat_len == 0 and lat_len % GRID_W == 0 and lat_len // GRID_W >= WIN_ROWS
    tm = next(m for m in (1024, 512, 256) if t_ctx % m == 0 and lat_len % m == 0)
    tm2 = min(tm, 512)
    lat_block0 = t_ctx // lat_len

    def cond_tile(tile_rows):
        def cond_of_tile(i):
            return jnp.where(i < t_ctx // tile_rows, 0, 1 + (i - t_ctx // tile_rows) // (lat_len // tile_rows))
        return cond_of_tile

    sm = jax.nn.softmax(hg_lb_logits.astype(F32), axis=0)
    lower = jnp.cumsum(sm, axis=0) - sm[0:1]
    mod_all = _modulation(jnp.concatenate([c_ctx[None, :], c], axis=0), w_mod, b_mod)
    mod_all = mod_all.reshape(depth, 1 + n_lat, 6, d)
    hgrn_consts = _hgrn_constants(HGRN_CHUNK)
    rope_c = _rope_tables(lat_len, HEAD_W)
    rope_d = _rope_tables(lat_len, DF_DQK)
    past = cache_diff_k.shape[4]
    cache_diff_k2 = cache_diff_k.transpose(0, 1, 2, 4, 3, 5).reshape(n_lat, depth, N_HEADS, past, HEAD_W)

    h = jnp.concatenate([x_prompt.reshape(t_ctx, d), x_sample.reshape(n_lat * lat_len, d)], axis=0)
    caches, states = [], None
    for layer in range(depth):
        mod = mod_all[layer]
        lam_init = 0.8 - 0.6 * math.exp(-0.3 * layer)
        proj = _input_projection(h, mod, norm1[layer], w_in[layer].astype(BF16), cond_tile(tm), tm)
        mix_a, states = _hgrn(proj, 0, n_ctx, ctx_len, lower[layer], hg_onorm[layer], hgrn_consts, None, None,
                              states, layer, depth)
        gains = (na_qn[layer], na_kn[layer], gqa_qn[layer], gqa_kn[layer], df_qn[layer], df_kn[layer],
                 df_subln[layer], df_lam[layer])
        mix_b, mix_c, mix_d, *caches = _context_attention(proj, n_ctx, ctx_len, gains, lam_init, layer, depth,
                                                          caches)
        mix_a, _ = _hgrn(proj, lat_block0, n_lat, lat_len, lower[layer], hg_onorm[layer], hgrn_consts,
                         state_hgrn[:, layer], mix_a, None, layer, depth)
        mix_b = _latent_na(proj, lat_block0, n_lat, lat_len, cache_na_k, cache_na_v, layer,
                           _na_bias(na_rpb[layer]), na_qn[layer], na_kn[layer], mix_b)
        mix_c = _latent_gqa(proj, lat_block0, n_lat, lat_len, cache_gqa_k, cache_gqa_v, layer, rope_c,
                            gqa_qn[layer], gqa_kn[layer], mix_c)
        mix_d = _latent_diff(proj, lat_block0, n_lat, lat_len, cache_diff_k2, cache_diff_v, layer, rope_d,
                             df_qn[layer], df_kn[layer], df_subln[layer], df_lam[layer], lam_init, mix_d)
        h1, x2, idx_t, gate_t = _output_projection((mix_a, mix_b, mix_c, mix_d), w_out[layer].astype(BF16), h, mod,
                                                   norm2[layer], router_w, router_b, cond_tile(tm2), tm2)
        h = _moe(h1, x2, idx_t, gate_t, mod, w_gate[layer], w_up[layer], w_down[layer], cond_tile(MOE_ROWS),
                 t_ctx if layer == depth - 1 else None)
    y_prompt = h[0].reshape(n_ctx, ctx_len, d)
    y_sample = h[1].reshape(n_lat, lat_len, d)
    return (y_prompt, y_sample, *caches, states)
```

```python
import functools
import math

import numpy as np
import jax
import jax.numpy as jnp
from jax import lax
from jax.experimental import pallas as pl
from jax.experimental.pallas import tpu as pltpu

D_MODEL = 2048
GRID_W = 64
GROUP_W = D_MODEL // 4
N_HEADS = 4
HEAD_W = GROUP_W // N_HEADS
GQA_KV_HEADS = 2
DF_DQK = HEAD_W // 2
WIN_ROWS = 8
WIN_COLS = 16
N_EXPERTS = 16
N_EXP_GROUPS = 4
EXP_PER_GROUP = N_EXPERTS // N_EXP_GROUPS
EXPERT_FF = D_MODEL // 4
ROPE_THETA = 10000.0
EPS = 1e-6
NEG_INF = -1e30
IN_WIDTH = 13 * GROUP_W

COL_A_Q, COL_A_FF, COL_A_FB, COL_A_I, COL_A_G = 0, 4, 8, 12, 16
COL_B_Q, COL_B_K, COL_B_V = 20, 24, 28
COL_C_Q, COL_C_K, COL_C_V = 32, 36, 38
COL_D_Q, COL_D_K, COL_D_V = 40, 44, 48

HGRN_CHUNK = 128
HGRN_UNROLL = 2
MOE_ROWS = 256
VMEM_LIMIT = 48 * 1024 * 1024

F32 = jnp.float32
BF16 = jnp.bfloat16
ANY_SPEC = pl.BlockSpec(memory_space=pl.ANY)


def _params(*sem):
    return pltpu.CompilerParams(dimension_semantics=sem, vmem_limit_bytes=VMEM_LIMIT)


def _sigmoid(x):
    return 1.0 / (1.0 + jnp.exp(-x))


def _silu(x):
    return x * _sigmoid(x)


def _rms(x, gain, n):
    return x * lax.rsqrt(jnp.sum(x * x, axis=-1, keepdims=True) * (1.0 / n) + EPS) * gain


def _dot(a, b):
    return jnp.dot(a, b, preferred_element_type=F32)


def _dot_nt(a, b):
    return lax.dot_general(a, b, (((1,), (1,)), ((), ())), preferred_element_type=F32)


def _dot_tn(a, b):
    return lax.dot_general(a, b, (((0,), (0,)), ((), ())), preferred_element_type=F32)


def _aligned(x, m):
    return x if isinstance(x, int) else pl.multiple_of(x, m)


def _alias_kwargs(n_inputs, prev, first_out):
    return ([ANY_SPEC] * len(prev), list(prev), {n_inputs + k: first_out + k for k in range(len(prev))})


def _mod_body(cond_ref, w_ref, b_ref, o_ref):
    w = w_ref[...]
    for c in range(cond_ref.shape[0]):
        s = _silu(cond_ref[c])
        o_ref[c:c + 1, :] = jnp.sum(w * s, axis=0, keepdims=True) + b_ref[...]


def _modulation(cond, w_mod, b_mod):
    depth, d, n6 = w_mod.shape
    nc = cond.shape[0]
    tn = 512
    return pl.pallas_call(
        _mod_body,
        grid=(depth, n6 // tn),
        in_specs=[pl.BlockSpec((nc, d, 1), lambda l, j: (0, 0, 0)),
                  pl.BlockSpec((None, d, tn), lambda l, j: (l, 0, j)),
                  pl.BlockSpec((None, 1, tn), lambda l, j: (l, 0, j))],
        out_specs=pl.BlockSpec((None, nc, tn), lambda l, j: (l, 0, j)),
        out_shape=jax.ShapeDtypeStruct((depth, nc, n6), F32),
        compiler_params=_params("parallel", "parallel"),
        name="adaln_modulation",
    )(cond[:, :, None], w_mod, b_mod[:, None, :])


def _inproj_body(h_ref, mod_ref, n1_ref, w_ref, o_ref, xn_ref):
    @pl.when(pl.program_id(1) == 0)
    def _():
        y = _rms(h_ref[...], n1_ref[...], D_MODEL)
        xn_ref[...] = (y * (1.0 + mod_ref[1:2, :]) + mod_ref[0:1, :]).astype(BF16)

    o_ref[...] = _dot(xn_ref[...], w_ref[...])


def _input_projection(h, mod, norm1, w_in_bf16, cond_of_tile, tm):
    t, d = h.shape
    n = w_in_bf16.shape[1]
    tn = 512
    return pl.pallas_call(
        _inproj_body,
        grid=(t // tm, n // tn),
        in_specs=[pl.BlockSpec((tm, d), lambda i, j: (i, 0)),
                  pl.BlockSpec((None, 6, d), lambda i, j: (cond_of_tile(i), 0, 0)),
                  pl.BlockSpec((1, d), lambda i, j: (0, 0)),
                  pl.BlockSpec((d, tn), lambda i, j: (0, j))],
        out_specs=pl.BlockSpec((tm, tn), lambda i, j: (i, j)),
        out_shape=jax.ShapeDtypeStruct((t, n), F32),
        scratch_shapes=[pltpu.VMEM((tm, d), BF16)],
        compiler_params=_params("parallel", "arbitrary"),
        name="norm_modulate_in_proj",
    )(h, mod, norm1[None, :], w_in_bf16)


def _hgrn_constants(c):
    nl = int(math.log2(c))
    idx = np.arange(c)
    e = np.zeros((nl + 2, c, c), np.float32)
    m = np.zeros((nl + 1, c, c), np.float32)
    e[0] = idx[None, :] <= idx[:, None]
    e[1] = idx[None, :] > idx[:, None]
    m[0] = np.eye(c)
    for li in range(nl):
        s = c >> (li + 1)
        parent = idx // (2 * s)
        right = (idx % (2 * s)) >= s
        ref = parent * 2 * s + s - 1
        for i in range(c):
            if right[i]:
                e[2 + li, i, ref[i] + 1:i + 1] = 1.0
            else:
                e[2 + li, i, i + 1:ref[i] + 1] = 1.0
        m[1 + li] = right[:, None] & ~right[None, :] & (parent[:, None] == parent[None, :])
    e2 = np.stack([e, e[:, ::-1, ::-1]]).reshape(2, (nl + 2) * c, c)
    m2 = np.stack([m, m[:, ::-1, ::-1]])
    return jnp.asarray(e2, BF16), jnp.asarray(m2, F32)


def _hgrn_body(*refs, seq, chunk, unroll, has_s0, emit_state, n_alias):
    q_ref, ff_ref, fb_ref, i_ref, g_ref, lb_ref, on_ref, e_ref, m_ref = refs[:9]
    pos = 9
    s0_ref = None
    if has_s0:
        s0_ref = refs[pos]
        pos += 1
    pos += n_alias
    o_ref = refs[pos]
    pos += 1
    if emit_state:
        st_ref = refs[pos]
        pos += 1
    of_ref, ob_ref = refs[pos], refs[pos + 1]
    c = chunk
    n_chunks = seq // c
    assert seq % c == 0 and n_chunks % unroll == 0
    n_levels = m_ref.shape[1] - 1
    gate_refs = (ff_ref, fb_ref)
    out_refs = (of_ref, ob_ref)

    def chunk_step(c0, d, st):
        rows = pl.ds(c0, c)
        lb = lb_ref[d:d + 1, :]
        f = lb + (1.0 - lb) * _sigmoid(gate_refs[d][rows, :])
        g = jnp.log(f)
        k = 1.0 - f
        q = _silu(q_ref[rows, :])
        v = i_ref[rows, :].astype(BF16)
        g_hi = g.astype(BF16)
        g_lo = (g - g_hi.astype(F32)).astype(BF16)
        g2 = _dot(e_ref[d], jnp.concatenate([g_hi, g_lo], axis=1))
        x = jnp.exp(g2[:, 0:HEAD_W] + g2[:, HEAD_W:2 * HEAD_W])
        x_cum = x[0:c]
        x_tail = x[c:2 * c]
        s = m_ref[d, 0] * _dot_nt(q.astype(BF16), k.astype(BF16))
        for lv in range(n_levels):
            x_l = x[(2 + lv) * c:(3 + lv) * c]
            s = s + m_ref[d, 1 + lv] * _dot_nt((q * x_l).astype(BF16), (k * x_l).astype(BF16))
        o = _dot_nt((q * x_cum).astype(BF16), st.astype(BF16)) + _dot(s.astype(BF16), v)
        out_refs[d][rows, :] = o
        total = x_cum[c - 1:c, :] if d == 0 else x_cum[0:1, :]
        return st * total + _dot_tn(v, (k * x_tail).astype(BF16))

    if has_s0:
        st_f0 = s0_ref[0].T
        st_b0 = s0_ref[1].T
    else:
        st_f0 = jnp.zeros((HEAD_W, HEAD_W), F32)
        st_b0 = jnp.zeros((HEAD_W, HEAD_W), F32)

    def loop(t, carry):
        st_f, st_b = carry
        for u in range(unroll):
            j = t * unroll + u
            st_f = chunk_step(_aligned(j * c, c), 0, st_f)
            st_b = chunk_step(_aligned((n_chunks - 1 - j) * c, c), 1, st_b)
        return st_f, st_b

    if n_chunks == unroll:
        st_f, st_b = loop(0, (st_f0, st_b0))
    else:
        st_f, st_b = lax.fori_loop(0, n_chunks // unroll, loop, (st_f0, st_b0))
    o = of_ref[...] + ob_ref[...]
    o_ref[...] = (_rms(o, on_ref[...], HEAD_W) * _silu(g_ref[...])).astype(o_ref.dtype)
    if emit_state:
        st_ref[0] = st_f.T
        st_ref[1] = st_b.T


def _hgrn(proj, row_block0, n_seq, seq, lower, onorm, consts, s0, mixed_prev, state_prev, layer, depth):
    e_mat, masks = consts
    latent = s0 is not None

    def col(cb):
        return pl.BlockSpec((seq, HEAD_W), lambda b, h, cb=cb: (row_block0 + b, cb + h))

    in_specs = [col(COL_A_Q), col(COL_A_FF), col(COL_A_FB), col(COL_A_I), col(COL_A_G),
                pl.BlockSpec((2, HEAD_W), lambda b, h: (0, h)),
                pl.BlockSpec((1, HEAD_W), lambda b, h: (0, 0)),
                pl.BlockSpec(e_mat.shape, lambda b, h: (0, 0, 0)),
                pl.BlockSpec(masks.shape, lambda b, h: (0, 0, 0, 0))]
    args = [proj, proj, proj, proj, proj, lower, onorm[None, :], e_mat, masks]
    if latent:
        in_specs.append(pl.BlockSpec((None, None, 2, None, HEAD_W, HEAD_W), lambda b, h: (b, layer, 0, h, 0, 0)))
        args.append(s0)
        prev = [mixed_prev]
    else:
        prev = [] if state_prev is None else [state_prev]
    alias_specs, alias_args, aliases = _alias_kwargs(len(args), prev, 0 if latent else 1)
    out_specs = [pl.BlockSpec((seq, HEAD_W), lambda b, h: (row_block0 + b, h))]
    out_shape = [jax.ShapeDtypeStruct((proj.shape[0], GROUP_W), BF16)]
    if not latent:
        out_specs.append(pl.BlockSpec((None, None, 2, None, HEAD_W, HEAD_W), lambda b, h: (b, layer, 0, h, 0, 0)))
        out_shape.append(jax.ShapeDtypeStruct((n_seq, depth, 2, N_HEADS, HEAD_W, HEAD_W), F32))
    res = pl.pallas_call(
        functools.partial(_hgrn_body, seq=seq, chunk=HGRN_CHUNK, unroll=HGRN_UNROLL, has_s0=latent,
                          emit_state=not latent, n_alias=len(prev)),
        grid=(n_seq, N_HEADS),
        in_specs=in_specs + alias_specs, out_specs=out_specs, out_shape=out_shape,
        input_output_aliases=aliases,
        scratch_shapes=[pltpu.VMEM((seq, HEAD_W), F32), pltpu.VMEM((seq, HEAD_W), F32)],
        compiler_params=_params("parallel", "parallel"),
        name="hgrn2_latent" if latent else "hgrn2_context",
    )(*args, *alias_args)
    return (res[0], None) if latent else res


def _softmax_pv(scores, values):
    mx = functools.reduce(jnp.maximum, [jnp.max(s, axis=-1, keepdims=True) for s in scores])
    es = [jnp.exp(s - mx) for s in scores]
    den = functools.reduce(lambda a, b: a + b, [jnp.sum(e, axis=-1, keepdims=True) for e in es])
    num = functools.reduce(lambda a, b: a + b, [_dot(e.astype(BF16), v) for e, v in zip(es, values)])
    return num / den


def _diff_weights(s0, s1, lam):
    e0 = jnp.exp(s0 - jnp.max(s0, axis=-1, keepdims=True))
    e1 = jnp.exp(s1 - jnp.max(s1, axis=-1, keepdims=True))
    p0 = e0 / jnp.sum(e0, axis=-1, keepdims=True)
    p1 = e1 / jnp.sum(e1, axis=-1, keepdims=True)
    return (p0 - lam * p1).astype(BF16)


def _lane_lt(shape, n):
    return lax.broadcasted_iota(jnp.int32, shape, len(shape) - 1) < n


def _rms_halves(x, gain2):
    lo = _lane_lt(x.shape, DF_DQK)
    sq = x * x
    ss_lo = jnp.sum(jnp.where(lo, sq, 0.0), axis=-1, keepdims=True)
    ss_hi = jnp.sum(sq, axis=-1, keepdims=True) - ss_lo
    inv = jnp.where(lo, lax.rsqrt(ss_lo * (1.0 / DF_DQK) + EPS), lax.rsqrt(ss_hi * (1.0 / DF_DQK) + EPS))
    return x * inv * gain2


def _lambda(lam_ref, lam_init):
    l = lam_ref[...]
    return (jnp.exp(jnp.sum(l[0:1] * l[1:2], axis=-1, keepdims=True))
            - jnp.exp(jnp.sum(l[2:3] * l[3:4], axis=-1, keepdims=True)) + lam_init)


N_CTX_ATTN_INPUTS = 17


def _ctx_attn_body(*refs, lam_init):
    (bq_ref, bk_ref, bv_ref, cq_ref, ck_ref, cv_ref, dq_ref, dk_ref, dv_ref,
     naq_ref, nak_ref, gq_ref, gk_ref, dfq_ref, dfk_ref, sub_ref, lam_ref) = refs[:N_CTX_ATTN_INPUTS]
    ob_ref, oc_ref, od_ref, kb_ref, vb_ref, kc_ref, vc_ref, kd_ref, vd_ref = refs[-9:]
    scale = HEAD_W ** -0.5
    kb = _rms(bk_ref[...], nak_ref[...], HEAD_W)
    vb = bv_ref[...]
    kb_ref[...] = kb
    vb_ref[...] = vb
    qb = (_rms(bq_ref[...], naq_ref[...], HEAD_W) * scale).astype(BF16)
    ob_ref[...] = _softmax_pv([_dot_nt(qb, kb.astype(BF16))], [vb.astype(BF16)]).astype(ob_ref.dtype)
    kc = _rms(ck_ref[...], gk_ref[...], HEAD_W)
    vc = cv_ref[...]
    kc_ref[...] = kc
    vc_ref[...] = vc
    qc = (_rms(cq_ref[...], gq_ref[...], HEAD_W) * scale).astype(BF16)
    oc_ref[...] = _softmax_pv([_dot_nt(qc, kc.astype(BF16))], [vc.astype(BF16)]).astype(oc_ref.dtype)
    kd = _rms_halves(dk_ref[...], dfk_ref[...])
    vd = dv_ref[...]
    kd_ref[0] = kd[:, 0:DF_DQK]
    kd_ref[1] = kd[:, DF_DQK:2 * DF_DQK]
    vd_ref[...] = vd
    qd = (_rms_halves(dq_ref[...], dfq_ref[...]) * (DF_DQK ** -0.5)).astype(BF16)
    lo = _lane_lt(kd.shape, DF_DQK)
    s0 = _dot_nt(qd, jnp.where(lo, kd, 0.0).astype(BF16))
    s1 = _dot_nt(qd, jnp.where(lo, 0.0, kd).astype(BF16))
    od = _dot(_diff_weights(s0, s1, _lambda(lam_ref, lam_init)), vd.astype(BF16))
    od_ref[...] = (_rms(od, sub_ref[...], HEAD_W) * (1.0 - lam_init)).astype(od_ref.dtype)


def _context_attention(proj, n_seq, seq, gains, lam_init, layer, depth, caches_prev):
    na_qn, na_kn, gqa_qn, gqa_kn, df_qn, df_kn, df_subln, df_lam = gains
    group = N_HEADS // GQA_KV_HEADS

    def col(cb, div=1):
        return pl.BlockSpec((seq, HEAD_W), lambda b, h, cb=cb, div=div: (b, cb + h // div))

    def vec(width=HEAD_W):
        return pl.BlockSpec((1, width), lambda b, h: (0, 0))

    def cache(div=1):
        return pl.BlockSpec((None, None, None, seq, HEAD_W), lambda b, h, div=div: (b, layer, h // div, 0, 0))

    mixed = pl.BlockSpec((seq, HEAD_W), lambda b, h: (b, h))
    mixed_shape = jax.ShapeDtypeStruct((proj.shape[0], GROUP_W), BF16)
    cache4 = jax.ShapeDtypeStruct((n_seq, depth, N_HEADS, seq, HEAD_W), F32)
    cache2 = jax.ShapeDtypeStruct((n_seq, depth, GQA_KV_HEADS, seq, HEAD_W), F32)
    cache_dk = jax.ShapeDtypeStruct((n_seq, depth, N_HEADS, 2, seq, DF_DQK), F32)
    args = [proj] * 9 + [na_qn[None, :], na_kn[None, :], gqa_qn[None, :], gqa_kn[None, :],
                         jnp.tile(df_qn, 2)[None, :], jnp.tile(df_kn, 2)[None, :], df_subln[None, :], df_lam]
    assert len(args) == N_CTX_ATTN_INPUTS
    alias_specs, alias_args, aliases = _alias_kwargs(len(args), caches_prev, 3)
    return pl.pallas_call(
        functools.partial(_ctx_attn_body, lam_init=lam_init),
        grid=(n_seq, N_HEADS),
        in_specs=[col(COL_B_Q), col(COL_B_K), col(COL_B_V),
                  col(COL_C_Q), col(COL_C_K, group), col(COL_C_V, group),
                  col(COL_D_Q), col(COL_D_K), col(COL_D_V),
                  vec(), vec(), vec(), vec(), vec(), vec(), vec(),
                  pl.BlockSpec((4, DF_DQK), lambda b, h: (0, 0))] + alias_specs,
        out_specs=[mixed, mixed, mixed, cache(), cache(), cache(group), cache(group),
                   pl.BlockSpec((None, None, None, 2, seq, DF_DQK), lambda b, h: (b, layer, h, 0, 0, 0)),
                   cache()],
        out_shape=[mixed_shape, mixed_shape, mixed_shape, cache4, cache4, cache2, cache2, cache_dk, cache4],
        input_output_aliases=aliases,
        compiler_params=_params("parallel", "arbitrary"),
        name="context_attention",
    )(*args, *alias_args)


def _rope_tables(n_tokens, rot_dim):
    t = np.arange(n_tokens)
    row = (t // GRID_W).astype(np.float32)
    col = (t % GRID_W).astype(np.float32)
    n_freq = rot_dim // 4
    inv = (np.float32(ROPE_THETA) ** (-np.arange(n_freq, dtype=np.float32) / np.float32(n_freq))).astype(np.float32)
    ang = np.concatenate([row[:, None] * inv, col[:, None] * inv], axis=-1).astype(np.float32)
    cos, sin, zero = np.cos(ang), np.sin(ang), np.zeros_like(ang)
    reps = HEAD_W // rot_dim
    a = np.tile(np.concatenate([cos, cos], axis=-1), (1, reps))
    b = np.tile(np.concatenate([-sin, zero], axis=-1), (1, reps))
    c = np.tile(np.concatenate([zero, sin], axis=-1), (1, reps))
    return jnp.asarray(np.stack([a, b, c]), F32)


def _rope(x, tab_ref, half):
    return (x * tab_ref[0] + pltpu.roll(x, HEAD_W - half, 1) * tab_ref[1]
            + pltpu.roll(x, half, 1) * tab_ref[2])


def _na_body(q_ref, k_ref, v_ref, ck_ref, cv_ref, bias_ref, qn_ref, kn_ref, prev_ref, o_ref,
             qs_ref, ks_ref, vs_ref, *, seq):
    del prev_ref
    rows = seq // GRID_W
    n_win = WIN_ROWS * GRID_W
    qs_ref[...] = (_rms(q_ref[...], qn_ref[...], HEAD_W) * (HEAD_W ** -0.5)).astype(BF16)
    ks_ref[...] = _rms(k_ref[...], kn_ref[...], HEAD_W).astype(BF16)
    vs_ref[...] = v_ref[...].astype(BF16)
    ck = ck_ref[...].astype(BF16)
    cv = cv_ref[...].astype(BF16)

    def row_step(r, carry):
        start = jnp.clip(r - WIN_ROWS // 2, 0, rows - WIN_ROWS)
        win = pl.ds(pl.multiple_of(start * GRID_W, GRID_W), n_win)
        qrows = pl.ds(pl.multiple_of(r * GRID_W, GRID_W), GRID_W)
        q = qs_ref[qrows, :]
        s_win = _dot_nt(q, ks_ref[win, :]) + bias_ref[start - r + (WIN_ROWS - 1)]
        s_ctx = _dot_nt(q, ck)
        o_ref[qrows, :] = _softmax_pv([s_win, s_ctx], [vs_ref[win, :], cv]).astype(o_ref.dtype)
        return carry

    lax.fori_loop(0, rows, row_step, 0)


def _na_bias(rpb):
    col = np.arange(GRID_W)
    col_start = np.clip(col - WIN_COLS // 2, 0, GRID_W - WIN_COLS)
    col_ok = (col[None, :] >= col_start[:, None]) & (col[None, :] < col_start[:, None] + WIN_COLS)
    dc = np.clip(col[None, :] - col[:, None] + WIN_COLS - 1, 0, 2 * WIN_COLS - 2).reshape(-1)
    onehot = (np.arange(2 * WIN_COLS - 1)[:, None] == dc[None, :]).astype(np.float32)
    per_dr = jnp.einsum('hdc,cn->hdn', rpb.astype(F32), jnp.asarray(onehot), precision=lax.Precision.HIGHEST)
    per_dr = jnp.where(col_ok[None, None], per_dr.reshape(rpb.shape[0], -1, GRID_W, GRID_W), NEG_INF)
    wins = jnp.stack([per_dr[:, o:o + WIN_ROWS] for o in range(WIN_ROWS)], axis=1)
    return wins.transpose(0, 1, 3, 2, 4).reshape(rpb.shape[0], WIN_ROWS, GRID_W, WIN_ROWS * GRID_W)


def _latent_na(proj, row_block0, n_seq, seq, cache_k, cache_v, layer, bias, na_qn, na_kn, mixed_prev):
    past = cache_k.shape[3]

    def col(cb):
        return pl.BlockSpec((seq, HEAD_W), lambda b, h, cb=cb: (row_block0 + b, cb + h))

    cache = pl.BlockSpec((None, None, None, past, HEAD_W), lambda b, h: (b, layer, h, 0, 0))
    vec = pl.BlockSpec((1, HEAD_W), lambda b, h: (0, 0))
    return pl.pallas_call(
        functools.partial(_na_body, seq=seq),
        grid=(n_seq, N_HEADS),
        in_specs=[col(COL_B_Q), col(COL_B_K), col(COL_B_V), cache, cache,
                  pl.BlockSpec((None, WIN_ROWS, GRID_W, WIN_ROWS * GRID_W), lambda b, h: (h, 0, 0, 0)),
                  vec, vec, ANY_SPEC],
        out_specs=pl.BlockSpec((seq, HEAD_W), lambda b, h: (row_block0 + b, h)),
        out_shape=jax.ShapeDtypeStruct(mixed_prev.shape, BF16),
        input_output_aliases={8: 0},
        scratch_shapes=[pltpu.VMEM((seq, HEAD_W), BF16)] * 3,
        compiler_params=_params("parallel", "parallel"),
        name="latent_neighbourhood_attention",
    )(proj, proj, proj, cache_k, cache_v, bias, na_qn[None, :], na_kn[None, :], mixed_prev)


def _gqa_body(q_ref, k_ref, v_ref, ck_ref, cv_ref, rope_ref, qn_ref, kn_ref, prev_ref, o_ref,
              qs_ref, ks_ref, vs_ref, *, seq, tq):
    del prev_ref
    group = N_HEADS // GQA_KV_HEADS
    half = HEAD_W // 2
    ks_ref[0:seq, :] = _rope(_rms(k_ref[...], kn_ref[...], HEAD_W), rope_ref, half).astype(BF16)
    ks_ref[seq:, :] = ck_ref[...].astype(BF16)
    vs_ref[0:seq, :] = v_ref[...].astype(BF16)
    vs_ref[seq:, :] = cv_ref[...].astype(BF16)
    for g in range(group):
        q = _rms(q_ref[:, g * HEAD_W:(g + 1) * HEAD_W], qn_ref[...], HEAD_W) * (HEAD_W ** -0.5)
        qs_ref[g] = _rope(q, rope_ref, half).astype(BF16)
    kk = ks_ref[...]
    vv = vs_ref[...]
    for g in range(group):
        def q_step(i, carry, g=g):
            qrows = pl.ds(pl.multiple_of(i * tq, tq), tq)
            o = _softmax_pv([_dot_nt(qs_ref[g, qrows, :], kk)], [vv])
            o_ref[qrows, g * HEAD_W:(g + 1) * HEAD_W] = o.astype(o_ref.dtype)
            return carry

        lax.fori_loop(0, seq // tq, q_step, 0)


def _latent_gqa(proj, row_block0, n_seq, seq, cache_k, cache_v, layer, rope, gqa_qn, gqa_kn, mixed_prev):
    past = cache_k.shape[3]
    group = N_HEADS // GQA_KV_HEADS
    tq = 256
    cache = pl.BlockSpec((None, None, None, past, HEAD_W), lambda b, n: (b, layer, n, 0, 0))
    vec = pl.BlockSpec((1, HEAD_W), lambda b, n: (0, 0))
    return pl.pallas_call(
        functools.partial(_gqa_body, seq=seq, tq=tq),
        grid=(n_seq, GQA_KV_HEADS),
        in_specs=[pl.BlockSpec((seq, group * HEAD_W), lambda b, n: (row_block0 + b, COL_C_Q // group + n)),
                  pl.BlockSpec((seq, HEAD_W), lambda b, n: (row_block0 + b, COL_C_K + n)),
                  pl.BlockSpec((seq, HEAD_W), lambda b, n: (row_block0 + b, COL_C_V + n)),
                  cache, cache,
                  pl.BlockSpec((3, seq, HEAD_W), lambda b, n: (0, 0, 0)),
                  vec, vec, ANY_SPEC],
        out_specs=pl.BlockSpec((seq, group * HEAD_W), lambda b, n: (row_block0 + b, n)),
        out_shape=jax.ShapeDtypeStruct(mixed_prev.shape, BF16),
        input_output_aliases={8: 0},
        scratch_shapes=[pltpu.VMEM((group, seq, HEAD_W), BF16),
                        pltpu.VMEM((seq + past, HEAD_W), BF16),
                        pltpu.VMEM((seq + past, HEAD_W), BF16)],
        compiler_params=_params("parallel", "parallel"),
        name="latent_gqa_attention",
    )(proj, proj, proj, cache_k, cache_v, rope, gqa_qn[None, :], gqa_kn[None, :], mixed_prev)


def _diff_body(q_ref, k_ref, v_ref, ck_ref, cv_ref, rope_ref, qn_ref, kn_ref, sub_ref, lam_ref, prev_ref, o_ref,
               qs_ref, k0_ref, k1_ref, vs_ref, *, seq, tq, lam_init):
    del prev_ref
    half = DF_DQK // 2
    k = _rope(_rms_halves(k_ref[...], kn_ref[...]), rope_ref, half)
    lo = _lane_lt(k.shape, DF_DQK)
    k0_ref[0:seq, :] = jnp.where(lo, k, 0.0).astype(BF16)
    k1_ref[0:seq, :] = jnp.where(lo, 0.0, k).astype(BF16)
    ck = ck_ref[...]
    lo_c = _lane_lt(ck.shape, DF_DQK)
    k0_ref[seq:, :] = jnp.where(lo_c, ck, 0.0).astype(BF16)
    k1_ref[seq:, :] = jnp.where(lo_c, 0.0, ck).astype(BF16)
    vs_ref[0:seq, :] = v_ref[...].astype(BF16)
    vs_ref[seq:, :] = cv_ref[...].astype(BF16)
    q = _rms_halves(q_ref[...], qn_ref[...]) * (DF_DQK ** -0.5)
    qs_ref[...] = _rope(q, rope_ref, half).astype(BF16)
    lam = _lambda(lam_ref, lam_init)
    k0 = k0_ref[...]
    k1 = k1_ref[...]
    vv = vs_ref[...]

    def q_step(i, carry):
        qrows = pl.ds(pl.multiple_of(i * tq, tq), tq)
        qb = qs_ref[qrows, :]
        o = _dot(_diff_weights(_dot_nt(qb, k0), _dot_nt(qb, k1), lam), vv)
        o_ref[qrows, :] = (_rms(o, sub_ref[...], HEAD_W) * (1.0 - lam_init)).astype(o_ref.dtype)
        return carry

    lax.fori_loop(0, seq // tq, q_step, 0)


def _latent_diff(proj, row_block0, n_seq, seq, cache_k2, cache_v, layer, rope, df_qn, df_kn, df_subln, df_lam,
                 lam_init, mixed_prev):
    past = cache_k2.shape[3]
    tq = 256

    def col(cb):
        return pl.BlockSpec((seq, HEAD_W), lambda b, h, cb=cb: (row_block0 + b, cb + h))

    cache = pl.BlockSpec((None, None, None, past, HEAD_W), lambda b, h: (b, layer, h, 0, 0))
    vec = pl.BlockSpec((1, HEAD_W), lambda b, h: (0, 0))
    kv_scratch = pltpu.VMEM((seq + past, HEAD_W), BF16)
    return pl.pallas_call(
        functools.partial(_diff_body, seq=seq, tq=tq, lam_init=lam_init),
        grid=(n_seq, N_HEADS),
        in_specs=[col(COL_D_Q), col(COL_D_K), col(COL_D_V), cache, cache,
                  pl.BlockSpec((3, seq, HEAD_W), lambda b, h: (0, 0, 0)),
                  vec, vec, vec, pl.BlockSpec((4, DF_DQK), lambda b, h: (0, 0)), ANY_SPEC],
        out_specs=pl.BlockSpec((seq, HEAD_W), lambda b, h: (row_block0 + b, h)),
        out_shape=jax.ShapeDtypeStruct(mixed_prev.shape, BF16),
        input_output_aliases={10: 0},
        scratch_shapes=[pltpu.VMEM((seq, HEAD_W), BF16), kv_scratch, kv_scratch, kv_scratch],
        compiler_params=_params("parallel", "parallel"),
        name="latent_diff_attention",
    )(proj, proj, proj, cache_k2, cache_v, rope, jnp.tile(df_qn, 2)[None, :], jnp.tile(df_kn, 2)[None, :],
      df_subln[None, :], df_lam, mixed_prev)


def _first_max(vals):
    best = vals[0]
    idx = jnp.zeros(best.shape, jnp.int32)
    for i in range(1, len(vals)):
        better = vals[i] > best
        best = jnp.where(better, vals[i], best)
        idx = jnp.where(better, i, idx)
    return best, idx


def _pick(vals, idx):
    out = vals[0]
    for i in range(1, len(vals)):
        out = jnp.where(idx == i, vals[i], out)
    return out


def _outproj_body(ma_ref, mb_ref, mc_ref, md_ref, w_ref, h_ref, mod_ref, n2_ref, rw_ref, rb_ref,
                  h1_ref, x2_ref, idx_ref, gate_ref):
    y = _dot(ma_ref[...], w_ref[0:GROUP_W, :])
    for g, m_ref in enumerate((mb_ref, mc_ref, md_ref), start=1):
        y = y + _dot(m_ref[...], w_ref[g * GROUP_W:(g + 1) * GROUP_W, :])
    h1 = h_ref[...] + mod_ref[2:3, :] * y
    h1_ref[...] = h1
    x2 = _rms(h1, n2_ref[...], D_MODEL) * (1.0 + mod_ref[4:5, :]) + mod_ref[3:4, :]
    x2_ref[...] = x2
    logits = lax.dot_general(rw_ref[...], x2, (((1,), (1,)), ((), ())), precision=lax.Precision.HIGHEST,
                             preferred_element_type=F32)
    aff_all = _sigmoid(logits)
    sel_all = aff_all + rb_ref[...]
    aff = [aff_all[e:e + 1, :] for e in range(N_EXPERTS)]
    sel = [sel_all[e:e + 1, :] for e in range(N_EXPERTS)]
    neg = jnp.full(sel[0].shape, -jnp.inf, F32)
    scores = []
    for g in range(N_EXP_GROUPS):
        grp = sel[g * EXP_PER_GROUP:(g + 1) * EXP_PER_GROUP]
        m1, i1 = _first_max(grp)
        m2, _ = _first_max([jnp.where(i1 == j, neg, grp[j]) for j in range(EXP_PER_GROUP)])
        scores.append(m1 + m2)
    _, g_best = _first_max(scores)
    in_sel = [_pick([sel[g * EXP_PER_GROUP + j] for g in range(N_EXP_GROUPS)], g_best)
              for j in range(EXP_PER_GROUP)]
    in_aff = [_pick([aff[g * EXP_PER_GROUP + j] for g in range(N_EXP_GROUPS)], g_best)
              for j in range(EXP_PER_GROUP)]
    _, l1 = _first_max(in_sel)
    _, l2 = _first_max([jnp.where(l1 == j, neg, in_sel[j]) for j in range(EXP_PER_GROUP)])
    w1 = _pick(in_aff, l1)
    w2 = _pick(in_aff, l2)
    idx_ref[0:1, :] = g_best * EXP_PER_GROUP + l1
    idx_ref[1:2, :] = g_best * EXP_PER_GROUP + l2
    gate_ref[0:1, :] = w1 / (w1 + w2)
    gate_ref[1:2, :] = w2 / (w1 + w2)


def _output_projection(mixed4, w_out_bf16, h, mod, norm2, router_w, router_b, cond_of_tile, tm):
    t, d = h.shape
    slab = pl.BlockSpec((tm, GROUP_W), lambda i: (i, 0))
    return pl.pallas_call(
        _outproj_body,
        grid=(t // tm,),
        in_specs=[slab, slab, slab, slab,
                  pl.BlockSpec((d, d), lambda i: (0, 0)),
                  pl.BlockSpec((tm, d), lambda i: (i, 0)),
                  pl.BlockSpec((None, 6, d), lambda i: (cond_of_tile(i), 0, 0)),
                  pl.BlockSpec((1, d), lambda i: (0, 0)),
                  pl.BlockSpec((N_EXPERTS, d), lambda i: (0, 0)),
                  pl.BlockSpec((N_EXPERTS, 1), lambda i: (0, 0))],
        out_specs=[pl.BlockSpec((tm, d), lambda i: (i, 0)),
                   pl.BlockSpec((tm, d), lambda i: (i, 0)),
                   pl.BlockSpec((2, tm), lambda i: (0, i)),
                   pl.BlockSpec((2, tm), lambda i: (0, i))],
        out_shape=[jax.ShapeDtypeStruct((t, d), F32), jax.ShapeDtypeStruct((t, d), F32),
                   jax.ShapeDtypeStruct((2, t), jnp.int32), jax.ShapeDtypeStruct((2, t), F32)],
        compiler_params=_params("parallel"),
        name="out_proj_residual_router",
    )(*mixed4, w_out_bf16, h, mod, norm2[None, :], router_w.T, router_b[:, None])


def _row_copy(src_hbm, src_row, dst, dst_row, sem):
    return pltpu.make_async_copy(src_hbm.at[pl.ds(src_row, 1), :], dst.at[pl.ds(dst_row, 1), :], sem)


def _rows_wait(src_hbm, dst, sem, n_rows):
    pltpu.make_async_copy(src_hbm.at[pl.ds(0, n_rows), :], dst.at[pl.ds(0, n_rows), :], sem).wait()


def _dispatch_body(dest_ref, ps_ref, pe_ref, x_ref, xs_hbm, zero_ref, sem, zsem, *, n_tok, tile):
    i = pl.program_id(0)

    def zero_copy(e):
        first = pl.multiple_of(pe_ref[e] - MOE_ROWS, MOE_ROWS)
        return pltpu.make_async_copy(zero_ref, xs_hbm.at[pl.ds(first, MOE_ROWS), :], zsem)

    @pl.when(i == 0)
    def _():
        zero_ref[...] = jnp.zeros(zero_ref.shape, zero_ref.dtype)
        for e in range(N_EXPERTS):
            @pl.when(pe_ref[e] > ps_ref[e])
            def _(e=e):
                zero_copy(e).start()
        for e in range(N_EXPERTS):
            @pl.when(pe_ref[e] > ps_ref[e])
            def _(e=e):
                zero_copy(e).wait()

    for k in range(2):
        def body(r, carry, k=k):
            _row_copy(x_ref, r, xs_hbm, dest_ref[k * n_tok + i * tile + r], sem).start()
            return carry

        lax.fori_loop(0, tile, body, 0, unroll=8)
    for k in range(2):
        _rows_wait(x_ref, xs_hbm, sem, tile)


def _dispatch(x2, dest, pad_start, pad_end, n_rows):
    t, d = x2.shape
    tile = MOE_ROWS
    grid_spec = pltpu.PrefetchScalarGridSpec(
        num_scalar_prefetch=3,
        grid=(t // tile,),
        in_specs=[pl.BlockSpec((tile, d), lambda i, dst, ps, pe: (i, 0))],
        out_specs=ANY_SPEC,
        scratch_shapes=[pltpu.VMEM((MOE_ROWS, d), F32), pltpu.SemaphoreType.DMA, pltpu.SemaphoreType.DMA],
    )
    return pl.pallas_call(
        functools.partial(_dispatch_body, n_tok=t, tile=tile),
        grid_spec=grid_spec,
        out_shape=jax.ShapeDtypeStruct((n_rows, d), F32),
        compiler_params=_params("arbitrary"),
        name="moe_dispatch",
    )(dest, pad_start, pad_end, x2)


def _expert_body(be_ref, nb_ref, x_ref, wg_ref, wu_ref, wd_ref, o_ref, wg_bf, wu_bf, wd_bf):
    i = pl.program_id(0)

    @pl.when(i < nb_ref[0])
    def _():
        @pl.when((i == 0) | (be_ref[i] != be_ref[jnp.maximum(i - 1, 0)]))
        def _():
            wg_bf[...] = wg_ref[...].astype(BF16)
            wu_bf[...] = wu_ref[...].astype(BF16)
            wd_bf[...] = wd_ref[...].astype(BF16)

        x = x_ref[...].astype(BF16)
        hdn = _silu(_dot(x, wg_bf[...])) * _dot(x, wu_bf[...])
        o_ref[...] = _dot(hdn.astype(BF16), wd_bf[...])

    @pl.when(i >= nb_ref[0])
    def _():
        o_ref[...] = jnp.zeros(o_ref.shape, o_ref.dtype)


def _expert_blocks(xs, block_expert, n_used, w_gate, w_up, w_down, layer):
    n_rows, d = xs.shape
    ff = w_gate.shape[-1]
    grid_spec = pltpu.PrefetchScalarGridSpec(
        num_scalar_prefetch=2,
        grid=(n_rows // MOE_ROWS,),
        in_specs=[pl.BlockSpec((MOE_ROWS, d), lambda i, be, nb: (jnp.minimum(i, nb[0] - 1), 0)),
                  pl.BlockSpec((None, None, d, ff), lambda i, be, nb: (layer, be[i], 0, 0)),
                  pl.BlockSpec((None, None, d, ff), lambda i, be, nb: (layer, be[i], 0, 0)),
                  pl.BlockSpec((None, None, ff, d), lambda i, be, nb: (layer, be[i], 0, 0))],
        out_specs=pl.BlockSpec((MOE_ROWS, d), lambda i, be, nb: (i, 0)),
        scratch_shapes=[pltpu.VMEM((d, ff), BF16), pltpu.VMEM((d, ff), BF16), pltpu.VMEM((ff, d), BF16)],
    )
    return pl.pallas_call(
        _expert_body,
        grid_spec=grid_spec,
        out_shape=jax.ShapeDtypeStruct((n_rows, d), F32),
        compiler_params=_params("arbitrary"),
        name="moe_expert_blocks",
    )(block_expert, n_used, xs, w_gate, w_up, w_down)


def _combine_body(dest_ref, h_ref, gate_ref, y_hbm, mod_ref, *rest, n_tok, tile, ctx_tiles):
    out_refs, (ybuf, sem) = rest[:-2], rest[-2:]
    i = pl.program_id(0)
    n_tiles = pl.num_programs(0)

    def start(blk, slot):
        for k in range(2):
            def body(r, carry, k=k):
                _row_copy(y_hbm, dest_ref[k * n_tok + blk * tile + r], ybuf.at[slot, k], r, sem.at[slot]).start()
                return carry

            lax.fori_loop(0, tile, body, 0, unroll=8)

    @pl.when(i == 0)
    def _():
        start(0, 0)

    @pl.when(i + 1 < n_tiles)
    def _():
        start(i + 1, (i + 1) % 2)

    slot = i % 2
    for k in range(2):
        _rows_wait(y_hbm, ybuf.at[slot, k], sem.at[slot], tile)
    gate = gate_ref[...]
    out = h_ref[...] + mod_ref[5:6, :] * (gate[:, 0:1] * ybuf[slot, 0] + gate[:, 1:2] * ybuf[slot, 1])
    if ctx_tiles is None:
        out_refs[0][...] = out
    else:
        @pl.when(i < ctx_tiles)
        def _():
            out_refs[0][...] = out

        @pl.when(i >= ctx_tiles)
        def _():
            out_refs[1][...] = out


def _combine(h1, yb, dest, gates, mod, cond_of_tile, tile, split_rows):
    t, d = h1.shape
    row_tile = pl.BlockSpec((tile, d), lambda i, dst: (i, 0))
    if split_rows is None:
        ctx_tiles = None
        out_specs = row_tile
        out_shape = jax.ShapeDtypeStruct((t, d), F32)
    else:
        ctx_tiles = split_rows // tile
        out_specs = [pl.BlockSpec((tile, d), lambda i, dst: (jnp.minimum(i, ctx_tiles - 1), 0)),
                     pl.BlockSpec((tile, d), lambda i, dst: (jnp.maximum(i - ctx_tiles, 0), 0))]
        out_shape = [jax.ShapeDtypeStruct((split_rows, d), F32), jax.ShapeDtypeStruct((t - split_rows, d), F32)]
    grid_spec = pltpu.PrefetchScalarGridSpec(
        num_scalar_prefetch=1,
        grid=(t // tile,),
        in_specs=[row_tile,
                  pl.BlockSpec((tile, 2), lambda i, dst: (i, 0)),
                  ANY_SPEC,
                  pl.BlockSpec((None, 6, d), lambda i, dst: (cond_of_tile(i), 0, 0))],
        out_specs=out_specs,
        scratch_shapes=[pltpu.VMEM((2, 2, tile, d), F32), pltpu.SemaphoreType.DMA((2,))],
    )
    return pl.pallas_call(
        functools.partial(_combine_body, n_tok=t, tile=tile, ctx_tiles=ctx_tiles),
        grid_spec=grid_spec,
        out_shape=out_shape,
        compiler_params=_params("arbitrary"),
        name="moe_gated_residual",
    )(dest, h1, gates, yb, mod)


def _moe(h1, x2, idx_t, gate_t, mod, w_gate, w_up, w_down, layer, cond_of_tile, split_rows):
    t, d = h1.shape
    n = 2 * t
    experts = idx_t.reshape(n)
    onehot = (experts[:, None] == jnp.arange(N_EXPERTS, dtype=jnp.int32)[None, :]).astype(BF16)
    blocks = onehot.reshape(n // MOE_ROWS, MOE_ROWS, N_EXPERTS)
    tri = jnp.asarray(np.tril(np.ones((MOE_ROWS, MOE_ROWS), np.float32)), BF16)
    within = jnp.einsum('ij,bjk->bik', tri, blocks, preferred_element_type=F32)
    block_total = within[:, -1, :]
    block_first = jnp.cumsum(block_total, axis=0) - block_total
    counts = (block_first[-1] + block_total[-1]).astype(jnp.int32)
    before = (within + block_first[:, None, :]).reshape(n, N_EXPERTS) - 1.0
    rank = jnp.sum(before * onehot.astype(F32), axis=1).astype(jnp.int32)
    padded = (counts + MOE_ROWS - 1) // MOE_ROWS * MOE_ROWS
    pad_end = jnp.cumsum(padded).astype(jnp.int32)
    pad_start = pad_end - padded
    dest = (pad_start[experts] + rank).astype(jnp.int32)
    n_blocks = (n + N_EXPERTS * (MOE_ROWS - 1) + MOE_ROWS - 1) // MOE_ROWS
    block_expert = jnp.minimum(
        jnp.searchsorted(pad_end, jnp.arange(n_blocks, dtype=jnp.int32) * MOE_ROWS, side='right'),
        N_EXPERTS - 1).astype(jnp.int32)
    n_used = (pad_end[-1:] // MOE_ROWS).astype(jnp.int32)
    xs = _dispatch(x2, dest, pad_start, pad_end, n_blocks * MOE_ROWS)
    yb = _expert_blocks(xs, block_expert, n_used, w_gate, w_up, w_down, layer)
    return _combine(h1, yb, dest, gate_t.T, mod, cond_of_tile, MOE_ROWS, split_rows)


def kernel(x_prompt, x_sample, cache_na_k, cache_na_v, cache_gqa_k, cache_gqa_v, cache_diff_k, cache_diff_v, state_hgrn, c, c_ctx, w_mod, b_mod, norm1, norm2, w_in, w_out, hg_lb_logits, hg_onorm, na_qn, na_kn, na_rpb, gqa_qn, gqa_kn, df_qn, df_kn, df_lam, df_subln, router_w, router_b, w_gate, w_up, w_down):
    n_ctx, ctx_len, d = x_prompt.shape
    n_lat, lat_len, _ = x_sample.shape
    depth = w_in.shape[0]
    t_ctx = n_ctx * ctx_len
    assert t_ctx % lat_len == 0 and lat_len % GRID_W == 0 and lat_len // GRID_W >= WIN_ROWS
    tm = next(m for m in (1024, 512, 256) if t_ctx % m == 0 and lat_len % m == 0)
    tm2 = min(tm, 512)
    lat_block0 = t_ctx // lat_len

    def cond_tile(tile_rows):
        def cond_of_tile(i):
            return jnp.where(i < t_ctx // tile_rows, 0, 1 + (i - t_ctx // tile_rows) // (lat_len // tile_rows))
        return cond_of_tile

    sm = jax.nn.softmax(hg_lb_logits.astype(F32), axis=0)
    lower = jnp.cumsum(sm, axis=0) - sm[0:1]
    mod_all = _modulation(jnp.concatenate([c_ctx[None, :], c], axis=0), w_mod, b_mod)
    mod_all = mod_all.reshape(depth, 1 + n_lat, 6, d)
    hgrn_consts = _hgrn_constants(HGRN_CHUNK)
    rope_c = _rope_tables(lat_len, HEAD_W)
    rope_d = _rope_tables(lat_len, DF_DQK)
    past = cache_diff_k.shape[4]
    cache_diff_k2 = cache_diff_k.transpose(0, 1, 2, 4, 3, 5).reshape(n_lat, depth, N_HEADS, past, HEAD_W)

    h = jnp.concatenate([x_prompt.reshape(t_ctx, d), x_sample.reshape(n_lat * lat_len, d)], axis=0)
    caches, states = [], None
    for layer in range(depth):
        mod = mod_all[layer]
        lam_init = 0.8 - 0.6 * math.exp(-0.3 * layer)
        proj = _input_projection(h, mod, norm1[layer], w_in[layer].astype(BF16), cond_tile(tm), tm)
        mix_a, states = _hgrn(proj, 0, n_ctx, ctx_len, lower[layer], hg_onorm[layer], hgrn_consts, None, None,
                              states, layer, depth)
        gains = (na_qn[layer], na_kn[layer], gqa_qn[layer], gqa_kn[layer], df_qn[layer], df_kn[layer],
                 df_subln[layer], df_lam[layer])
        mix_b, mix_c, mix_d, *caches = _context_attention(proj, n_ctx, ctx_len, gains, lam_init, layer, depth,
                                                          caches)
        mix_a, _ = _hgrn(proj, lat_block0, n_lat, lat_len, lower[layer], hg_onorm[layer], hgrn_consts,
                         state_hgrn, mix_a, None, layer, depth)
        mix_b = _latent_na(proj, lat_block0, n_lat, lat_len, cache_na_k, cache_na_v, layer,
                           _na_bias(na_rpb[layer]), na_qn[layer], na_kn[layer], mix_b)
        mix_c = _latent_gqa(proj, lat_block0, n_lat, lat_len, cache_gqa_k, cache_gqa_v, layer, rope_c,
                            gqa_qn[layer], gqa_kn[layer], mix_c)
        mix_d = _latent_diff(proj, lat_block0, n_lat, lat_len, cache_diff_k2, cache_diff_v, layer, rope_d,
                             df_qn[layer], df_kn[layer], df_subln[layer], df_lam[layer], lam_init, mix_d)
        h1, x2, idx_t, gate_t = _output_projection((mix_a, mix_b, mix_c, mix_d), w_out[layer].astype(BF16), h, mod,
                                                   norm2[layer], router_w, router_b, cond_tile(tm2), tm2)
        h = _moe(h1, x2, idx_t, gate_t, mod, w_gate, w_up, w_down, layer, cond_tile(MOE_ROWS),
                 t_ctx if layer == depth - 1 else None)
    y_prompt = h[0].reshape(n_ctx, ctx_len, d)
    y_sample = h[1].reshape(n_lat, lat_len, d)
    return (y_prompt, y_sample, *caches, states)
```

```python
import functools
import math

import numpy as np
import jax
import jax.numpy as jnp
from jax import lax
from jax.experimental import pallas as pl
from jax.experimental.pallas import tpu as pltpu

D_MODEL = 2048
GRID_W = 64
GROUP_W = D_MODEL // 4
N_HEADS = 4
HEAD_W = GROUP_W // N_HEADS
GQA_KV_HEADS = 2
DF_DQK = HEAD_W // 2
WIN_ROWS = 8
WIN_COLS = 16
N_EXPERTS = 16
N_EXP_GROUPS = 4
EXP_PER_GROUP = N_EXPERTS // N_EXP_GROUPS
EXPERT_FF = D_MODEL // 4
ROPE_THETA = 10000.0
EPS = 1e-6
NEG_INF = -1e30
IN_WIDTH = 13 * GROUP_W

COL_A_Q, COL_A_FF, COL_A_FB, COL_A_I, COL_A_G = 0, 4, 8, 12, 16
COL_B_Q, COL_B_K, COL_B_V = 20, 24, 28
COL_C_Q, COL_C_K, COL_C_V = 32, 36, 38
COL_D_Q, COL_D_K, COL_D_V = 40, 44, 48

HGRN_CHUNK = 128
HGRN_UNROLL = 2
HGRN_HEADS_PER_STEP = 2
MOE_ROWS = 256
VMEM_LIMIT = 48 * 1024 * 1024

F32 = jnp.float32
BF16 = jnp.bfloat16
ANY_SPEC = pl.BlockSpec(memory_space=pl.ANY)


def _params(*sem):
    return pltpu.CompilerParams(dimension_semantics=sem, vmem_limit_bytes=VMEM_LIMIT)


def _sigmoid(x):
    return 1.0 / (1.0 + jnp.exp(-x))


def _silu(x):
    return x * _sigmoid(x)


def _rms(x, gain, n):
    return x * lax.rsqrt(jnp.sum(x * x, axis=-1, keepdims=True) * (1.0 / n) + EPS) * gain


def _dot(a, b):
    return jnp.dot(a, b, preferred_element_type=F32)


def _dot_nt(a, b):
    return lax.dot_general(a, b, (((1,), (1,)), ((), ())), preferred_element_type=F32)


def _dot_tn(a, b):
    return lax.dot_general(a, b, (((0,), (0,)), ((), ())), preferred_element_type=F32)


def _aligned(x, m):
    return x if isinstance(x, int) else pl.multiple_of(x, m)


def _alias_kwargs(n_inputs, prev, first_out):
    return ([ANY_SPEC] * len(prev), list(prev), {n_inputs + k: first_out + k for k in range(len(prev))})


def _mod_body(cond_ref, w_ref, b_ref, o_ref):
    w = w_ref[...]
    for c in range(cond_ref.shape[0]):
        s = _silu(cond_ref[c])
        o_ref[c:c + 1, :] = jnp.sum(w * s, axis=0, keepdims=True) + b_ref[...]


def _modulation(cond, w_mod, b_mod):
    depth, d, n6 = w_mod.shape
    nc = cond.shape[0]
    tn = 512
    return pl.pallas_call(
        _mod_body,
        grid=(depth, n6 // tn),
        in_specs=[pl.BlockSpec((nc, d, 1), lambda l, j: (0, 0, 0)),
                  pl.BlockSpec((None, d, tn), lambda l, j: (l, 0, j)),
                  pl.BlockSpec((None, 1, tn), lambda l, j: (l, 0, j))],
        out_specs=pl.BlockSpec((None, nc, tn), lambda l, j: (l, 0, j)),
        out_shape=jax.ShapeDtypeStruct((depth, nc, n6), F32),
        compiler_params=_params("parallel", "parallel"),
        name="adaln_modulation",
    )(cond[:, :, None], w_mod, b_mod[:, None, :])


def _inproj_body(h_ref, mod_ref, n1_ref, w_ref, o_ref, xn_ref):
    @pl.when(pl.program_id(1) == 0)
    def _():
        y = _rms(h_ref[...], n1_ref[...], D_MODEL)
        xn_ref[...] = (y * (1.0 + mod_ref[1:2, :]) + mod_ref[0:1, :]).astype(BF16)

    o_ref[...] = _dot(xn_ref[...], w_ref[...])


def _input_projection(h, mod, norm1, w_in_bf16, cond_of_tile, tm):
    t, d = h.shape
    n = w_in_bf16.shape[1]
    tn = 512
    return pl.pallas_call(
        _inproj_body,
        grid=(t // tm, n // tn),
        in_specs=[pl.BlockSpec((tm, d), lambda i, j: (i, 0)),
                  pl.BlockSpec((None, 6, d), lambda i, j: (cond_of_tile(i), 0, 0)),
                  pl.BlockSpec((1, d), lambda i, j: (0, 0)),
                  pl.BlockSpec((d, tn), lambda i, j: (0, j))],
        out_specs=pl.BlockSpec((tm, tn), lambda i, j: (i, j)),
        out_shape=jax.ShapeDtypeStruct((t, n), F32),
        scratch_shapes=[pltpu.VMEM((tm, d), BF16)],
        compiler_params=_params("parallel", "arbitrary"),
        name="norm_modulate_in_proj",
    )(h, mod, norm1[None, :], w_in_bf16)


def _hgrn_constants(c):
    nl = int(math.log2(c))
    idx = np.arange(c)
    e = np.zeros((nl + 2, c, c), np.float32)
    m = np.zeros((nl + 1, c, c), np.float32)
    e[0] = idx[None, :] <= idx[:, None]
    e[1] = idx[None, :] > idx[:, None]
    m[0] = np.eye(c)
    for li in range(nl):
        s = c >> (li + 1)
        parent = idx // (2 * s)
        right = (idx % (2 * s)) >= s
        ref = parent * 2 * s + s - 1
        for i in range(c):
            if right[i]:
                e[2 + li, i, ref[i] + 1:i + 1] = 1.0
            else:
                e[2 + li, i, i + 1:ref[i] + 1] = 1.0
        m[1 + li] = right[:, None] & ~right[None, :] & (parent[:, None] == parent[None, :])
    e2 = np.stack([e, e[:, ::-1, ::-1]]).reshape(2, (nl + 2) * c, c)
    m2 = np.stack([m, m[:, ::-1, ::-1]])
    return jnp.asarray(e2, BF16), jnp.asarray(m2, F32)


def _hgrn_body(*refs, seq, chunk, unroll, heads, has_s0, emit_state, n_alias):
    q_ref, ff_ref, fb_ref, i_ref, g_ref, lb_ref, on_ref, e_ref, m_ref = refs[:9]
    pos = 9
    s0_ref = None
    if has_s0:
        s0_ref = refs[pos]
        pos += 1
    pos += n_alias
    o_ref = refs[pos]
    pos += 1
    if emit_state:
        st_ref = refs[pos]
        pos += 1
    of_ref, ob_ref = refs[pos], refs[pos + 1]
    c = chunk
    n_chunks = seq // c
    assert seq % c == 0 and n_chunks % unroll == 0
    n_levels = m_ref.shape[1] - 1
    gate_refs = (ff_ref, fb_ref)
    out_refs = (of_ref, ob_ref)

    def chunk_step(c0, d, hh, st):
        rows = pl.ds(c0, c)
        lanes = slice(hh * HEAD_W, (hh + 1) * HEAD_W)
        lb = lb_ref[d:d + 1, lanes]
        f = lb + (1.0 - lb) * _sigmoid(gate_refs[d][rows, lanes])
        g = jnp.log(f)
        k = 1.0 - f
        q = _silu(q_ref[rows, lanes])
        v = i_ref[rows, lanes].astype(BF16)
        g_hi = g.astype(BF16)
        g_lo = (g - g_hi.astype(F32)).astype(BF16)
        g2 = _dot(e_ref[d], jnp.concatenate([g_hi, g_lo], axis=1))
        x = jnp.exp(g2[:, 0:HEAD_W] + g2[:, HEAD_W:2 * HEAD_W])
        x_cum = x[0:c]
        x_tail = x[c:2 * c]
        s = m_ref[d, 0] * _dot_nt(q.astype(BF16), k.astype(BF16))
        for lv in range(n_levels):
            x_l = x[(2 + lv) * c:(3 + lv) * c]
            s = s + m_ref[d, 1 + lv] * _dot_nt((q * x_l).astype(BF16), (k * x_l).astype(BF16))
        o = _dot_nt((q * x_cum).astype(BF16), st.astype(BF16)) + _dot(s.astype(BF16), v)
        out_refs[d][rows, lanes] = o
        total = x_cum[c - 1:c, :] if d == 0 else x_cum[0:1, :]
        return st * total + _dot_tn(v, (k * x_tail).astype(BF16))

    if has_s0:
        states0 = tuple(s0_ref[d, hh].T for hh in range(heads) for d in range(2))
    else:
        states0 = tuple(jnp.zeros((HEAD_W, HEAD_W), F32) for _ in range(2 * heads))

    def loop(t, states):
        states = list(states)
        for u in range(unroll):
            j = t * unroll + u
            for hh in range(heads):
                states[2 * hh] = chunk_step(_aligned(j * c, c), 0, hh, states[2 * hh])
                states[2 * hh + 1] = chunk_step(_aligned((n_chunks - 1 - j) * c, c), 1, hh, states[2 * hh + 1])
        return tuple(states)

    if n_chunks == unroll:
        states = loop(0, states0)
    else:
        states = lax.fori_loop(0, n_chunks // unroll, loop, states0)
    for hh in range(heads):
        lanes = slice(hh * HEAD_W, (hh + 1) * HEAD_W)
        o = of_ref[:, lanes] + ob_ref[:, lanes]
        o_ref[:, lanes] = (_rms(o, on_ref[...], HEAD_W) * _silu(g_ref[:, lanes])).astype(o_ref.dtype)
        if emit_state:
            st_ref[0, hh] = states[2 * hh].T
            st_ref[1, hh] = states[2 * hh + 1].T


def _hgrn(proj, row_block0, n_seq, seq, lower, onorm, consts, s0, mixed_prev, state_prev, layer, depth):
    e_mat, masks = consts
    latent = s0 is not None
    hps = HGRN_HEADS_PER_STEP
    width = hps * HEAD_W

    def col(cb):
        return pl.BlockSpec((seq, width), lambda b, h, cb=cb: (row_block0 + b, cb // hps + h))

    state_spec = pl.BlockSpec((None, None, 2, hps, HEAD_W, HEAD_W), lambda b, h: (b, layer, 0, h, 0, 0))
    in_specs = [col(COL_A_Q), col(COL_A_FF), col(COL_A_FB), col(COL_A_I), col(COL_A_G),
                pl.BlockSpec((2, width), lambda b, h: (0, h)),
                pl.BlockSpec((1, HEAD_W), lambda b, h: (0, 0)),
                pl.BlockSpec(e_mat.shape, lambda b, h: (0, 0, 0)),
                pl.BlockSpec(masks.shape, lambda b, h: (0, 0, 0, 0))]
    args = [proj, proj, proj, proj, proj, lower, onorm[None, :], e_mat, masks]
    if latent:
        in_specs.append(state_spec)
        args.append(s0)
        prev = [mixed_prev]
    else:
        prev = [] if state_prev is None else [state_prev]
    alias_specs, alias_args, aliases = _alias_kwargs(len(args), prev, 0 if latent else 1)
    out_specs = [pl.BlockSpec((seq, width), lambda b, h: (row_block0 + b, h))]
    out_shape = [jax.ShapeDtypeStruct((proj.shape[0], GROUP_W), BF16)]
    if not latent:
        out_specs.append(state_spec)
        out_shape.append(jax.ShapeDtypeStruct((n_seq, depth, 2, N_HEADS, HEAD_W, HEAD_W), F32))
    res = pl.pallas_call(
        functools.partial(_hgrn_body, seq=seq, chunk=HGRN_CHUNK, unroll=HGRN_UNROLL, heads=hps, has_s0=latent,
                          emit_state=not latent, n_alias=len(prev)),
        grid=(n_seq, N_HEADS // hps),
        in_specs=in_specs + alias_specs, out_specs=out_specs, out_shape=out_shape,
        input_output_aliases=aliases,
        scratch_shapes=[pltpu.VMEM((seq, width), F32), pltpu.VMEM((seq, width), F32)],
        compiler_params=_params("parallel", "parallel"),
        name="hgrn2_latent" if latent else "hgrn2_context",
    )(*args, *alias_args)
    return (res[0], None) if latent else res


def _with_ones(v):
    return jnp.concatenate([v, jnp.ones_like(v)], axis=1)


def _softmax_pv(scores, values1):
    mx = functools.reduce(jnp.maximum, [jnp.max(s, axis=-1, keepdims=True) for s in scores])
    acc = functools.reduce(lambda a, b: a + b,
                           [_dot(jnp.exp(s - mx).astype(BF16), v) for s, v in zip(scores, values1)])
    return acc[:, 0:HEAD_W] / acc[:, HEAD_W:HEAD_W + 1]


def _diff_pv(s0, s1, lam, values1):
    return _softmax_pv([s0], [values1]) - lam * _softmax_pv([s1], [values1])


def _lane_lt(shape, n):
    return lax.broadcasted_iota(jnp.int32, shape, len(shape) - 1) < n


def _rms_halves(x, gain2):
    lo = _lane_lt(x.shape, DF_DQK)
    sq = x * x
    ss_lo = jnp.sum(jnp.where(lo, sq, 0.0), axis=-1, keepdims=True)
    ss_hi = jnp.sum(sq, axis=-1, keepdims=True) - ss_lo
    inv = jnp.where(lo, lax.rsqrt(ss_lo * (1.0 / DF_DQK) + EPS), lax.rsqrt(ss_hi * (1.0 / DF_DQK) + EPS))
    return x * inv * gain2


def _lambda(lam_ref, lam_init):
    l = lam_ref[...]
    return (jnp.exp(jnp.sum(l[0:1] * l[1:2], axis=-1, keepdims=True))
            - jnp.exp(jnp.sum(l[2:3] * l[3:4], axis=-1, keepdims=True)) + lam_init)


N_CTX_ATTN_INPUTS = 17


def _ctx_attn_body(*refs, lam_init):
    (bq_ref, bk_ref, bv_ref, cq_ref, ck_ref, cv_ref, dq_ref, dk_ref, dv_ref,
     naq_ref, nak_ref, gq_ref, gk_ref, dfq_ref, dfk_ref, sub_ref, lam_ref) = refs[:N_CTX_ATTN_INPUTS]
    ob_ref, oc_ref, od_ref, kb_ref, vb_ref, kc_ref, vc_ref, kd_ref, vd_ref = refs[-9:]
    scale = HEAD_W ** -0.5
    kb = _rms(bk_ref[...], nak_ref[...], HEAD_W)
    vb = bv_ref[...]
    kb_ref[...] = kb
    vb_ref[...] = vb
    qb = (_rms(bq_ref[...], naq_ref[...], HEAD_W) * scale).astype(BF16)
    ob_ref[...] = _softmax_pv([_dot_nt(qb, kb.astype(BF16))], [_with_ones(vb.astype(BF16))]).astype(ob_ref.dtype)
    kc = _rms(ck_ref[...], gk_ref[...], HEAD_W)
    vc = cv_ref[...]
    kc_ref[...] = kc
    vc_ref[...] = vc
    qc = (_rms(cq_ref[...], gq_ref[...], HEAD_W) * scale).astype(BF16)
    oc_ref[...] = _softmax_pv([_dot_nt(qc, kc.astype(BF16))], [_with_ones(vc.astype(BF16))]).astype(oc_ref.dtype)
    kd = _rms_halves(dk_ref[...], dfk_ref[...])
    vd = dv_ref[...]
    kd_ref[0] = kd[:, 0:DF_DQK]
    kd_ref[1] = kd[:, DF_DQK:2 * DF_DQK]
    vd_ref[...] = vd
    qd = (_rms_halves(dq_ref[...], dfq_ref[...]) * (DF_DQK ** -0.5)).astype(BF16)
    lo = _lane_lt(kd.shape, DF_DQK)
    s0 = _dot_nt(qd, jnp.where(lo, kd, 0.0).astype(BF16))
    s1 = _dot_nt(qd, jnp.where(lo, 0.0, kd).astype(BF16))
    od = _diff_pv(s0, s1, _lambda(lam_ref, lam_init), _with_ones(vd.astype(BF16)))
    od_ref[...] = (_rms(od, sub_ref[...], HEAD_W) * (1.0 - lam_init)).astype(od_ref.dtype)


def _context_attention(proj, n_seq, seq, gains, lam_init, layer, depth, caches_prev):
    na_qn, na_kn, gqa_qn, gqa_kn, df_qn, df_kn, df_subln, df_lam = gains
    group = N_HEADS // GQA_KV_HEADS

    def col(cb, div=1):
        return pl.BlockSpec((seq, HEAD_W), lambda b, h, cb=cb, div=div: (b, cb + h // div))

    def vec(width=HEAD_W):
        return pl.BlockSpec((1, width), lambda b, h: (0, 0))

    def cache(div=1):
        return pl.BlockSpec((None, None, None, seq, HEAD_W), lambda b, h, div=div: (b, layer, h // div, 0, 0))

    mixed = pl.BlockSpec((seq, HEAD_W), lambda b, h: (b, h))
    mixed_shape = jax.ShapeDtypeStruct((proj.shape[0], GROUP_W), BF16)
    cache4 = jax.ShapeDtypeStruct((n_seq, depth, N_HEADS, seq, HEAD_W), F32)
    cache2 = jax.ShapeDtypeStruct((n_seq, depth, GQA_KV_HEADS, seq, HEAD_W), F32)
    cache_dk = jax.ShapeDtypeStruct((n_seq, depth, N_HEADS, 2, seq, DF_DQK), F32)
    args = [proj] * 9 + [na_qn[None, :], na_kn[None, :], gqa_qn[None, :], gqa_kn[None, :],
                         jnp.tile(df_qn, 2)[None, :], jnp.tile(df_kn, 2)[None, :], df_subln[None, :], df_lam]
    assert len(args) == N_CTX_ATTN_INPUTS
    alias_specs, alias_args, aliases = _alias_kwargs(len(args), caches_prev, 3)
    return pl.pallas_call(
        functools.partial(_ctx_attn_body, lam_init=lam_init),
        grid=(n_seq, N_HEADS),
        in_specs=[col(COL_B_Q), col(COL_B_K), col(COL_B_V),
                  col(COL_C_Q), col(COL_C_K, group), col(COL_C_V, group),
                  col(COL_D_Q), col(COL_D_K), col(COL_D_V),
                  vec(), vec(), vec(), vec(), vec(), vec(), vec(),
                  pl.BlockSpec((4, DF_DQK), lambda b, h: (0, 0))] + alias_specs,
        out_specs=[mixed, mixed, mixed, cache(), cache(), cache(group), cache(group),
                   pl.BlockSpec((None, None, None, 2, seq, DF_DQK), lambda b, h: (b, layer, h, 0, 0, 0)),
                   cache()],
        out_shape=[mixed_shape, mixed_shape, mixed_shape, cache4, cache4, cache2, cache2, cache_dk, cache4],
        input_output_aliases=aliases,
        compiler_params=_params("parallel", "arbitrary"),
        name="context_attention",
    )(*args, *alias_args)


def _rope_tables(n_tokens, rot_dim):
    t = np.arange(n_tokens)
    row = (t // GRID_W).astype(np.float32)
    col = (t % GRID_W).astype(np.float32)
    n_freq = rot_dim // 4
    inv = (np.float32(ROPE_THETA) ** (-np.arange(n_freq, dtype=np.float32) / np.float32(n_freq))).astype(np.float32)
    ang = np.concatenate([row[:, None] * inv, col[:, None] * inv], axis=-1).astype(np.float32)
    cos, sin, zero = np.cos(ang), np.sin(ang), np.zeros_like(ang)
    reps = HEAD_W // rot_dim
    a = np.tile(np.concatenate([cos, cos], axis=-1), (1, reps))
    b = np.tile(np.concatenate([-sin, zero], axis=-1), (1, reps))
    c = np.tile(np.concatenate([zero, sin], axis=-1), (1, reps))
    return jnp.asarray(np.stack([a, b, c]), F32)


def _rope(x, tab_ref, half):
    return (x * tab_ref[0] + pltpu.roll(x, HEAD_W - half, 1) * tab_ref[1]
            + pltpu.roll(x, half, 1) * tab_ref[2])


def _na_body(q_ref, k_ref, v_ref, ck_ref, cv_ref, bias_ref, qn_ref, kn_ref, prev_ref, o_ref,
             qs_ref, ks_ref, vs_ref, *, seq):
    del prev_ref
    rows = seq // GRID_W
    n_win = WIN_ROWS * GRID_W
    qs_ref[...] = (_rms(q_ref[...], qn_ref[...], HEAD_W) * (HEAD_W ** -0.5)).astype(BF16)
    ks_ref[...] = _rms(k_ref[...], kn_ref[...], HEAD_W).astype(BF16)
    vs_ref[...] = _with_ones(v_ref[...].astype(BF16))
    ck = ck_ref[...].astype(BF16)
    cv = _with_ones(cv_ref[...].astype(BF16))

    def row_step(r, carry):
        start = jnp.clip(r - WIN_ROWS // 2, 0, rows - WIN_ROWS)
        win = pl.ds(pl.multiple_of(start * GRID_W, GRID_W), n_win)
        qrows = pl.ds(pl.multiple_of(r * GRID_W, GRID_W), GRID_W)
        q = qs_ref[qrows, :]
        s_win = _dot_nt(q, ks_ref[win, :]) + bias_ref[start - r + (WIN_ROWS - 1)]
        s_ctx = _dot_nt(q, ck)
        o_ref[qrows, :] = _softmax_pv([s_win, s_ctx], [vs_ref[win, :], cv]).astype(o_ref.dtype)
        return carry

    lax.fori_loop(0, rows, row_step, 0, unroll=2)


def _na_bias(rpb):
    col = np.arange(GRID_W)
    col_start = np.clip(col - WIN_COLS // 2, 0, GRID_W - WIN_COLS)
    col_ok = (col[None, :] >= col_start[:, None]) & (col[None, :] < col_start[:, None] + WIN_COLS)
    dc = np.clip(col[None, :] - col[:, None] + WIN_COLS - 1, 0, 2 * WIN_COLS - 2).reshape(-1)
    onehot = (np.arange(2 * WIN_COLS - 1)[:, None] == dc[None, :]).astype(np.float32)
    per_dr = jnp.einsum('hdc,cn->hdn', rpb.astype(F32), jnp.asarray(onehot), precision=lax.Precision.HIGHEST)
    per_dr = jnp.where(col_ok[None, None], per_dr.reshape(rpb.shape[0], -1, GRID_W, GRID_W), NEG_INF)
    wins = jnp.stack([per_dr[:, o:o + WIN_ROWS] for o in range(WIN_ROWS)], axis=1)
    return wins.transpose(0, 1, 3, 2, 4).reshape(rpb.shape[0], WIN_ROWS, GRID_W, WIN_ROWS * GRID_W)


def _latent_na(proj, row_block0, n_seq, seq, cache_k, cache_v, layer, bias, na_qn, na_kn, mixed_prev):
    past = cache_k.shape[3]

    def col(cb):
        return pl.BlockSpec((seq, HEAD_W), lambda b, h, cb=cb: (row_block0 + b, cb + h))

    cache = pl.BlockSpec((None, None, None, past, HEAD_W), lambda b, h: (b, layer, h, 0, 0))
    vec = pl.BlockSpec((1, HEAD_W), lambda b, h: (0, 0))
    return pl.pallas_call(
        functools.partial(_na_body, seq=seq),
        grid=(n_seq, N_HEADS),
        in_specs=[col(COL_B_Q), col(COL_B_K), col(COL_B_V), cache, cache,
                  pl.BlockSpec((None, WIN_ROWS, GRID_W, WIN_ROWS * GRID_W), lambda b, h: (h, 0, 0, 0)),
                  vec, vec, ANY_SPEC],
        out_specs=pl.BlockSpec((seq, HEAD_W), lambda b, h: (row_block0 + b, h)),
        out_shape=jax.ShapeDtypeStruct(mixed_prev.shape, BF16),
        input_output_aliases={8: 0},
        scratch_shapes=[pltpu.VMEM((seq, HEAD_W), BF16), pltpu.VMEM((seq, HEAD_W), BF16),
                        pltpu.VMEM((seq, 2 * HEAD_W), BF16)],
        compiler_params=_params("parallel", "parallel"),
        name="latent_neighbourhood_attention",
    )(proj, proj, proj, cache_k, cache_v, bias, na_qn[None, :], na_kn[None, :], mixed_prev)


def _gqa_body(q_ref, k_ref, v_ref, ck_ref, cv_ref, rope_ref, qn_ref, kn_ref, prev_ref, o_ref,
              qs_ref, ks_ref, vs_ref, *, seq, tq):
    del prev_ref
    group = N_HEADS // GQA_KV_HEADS
    half = HEAD_W // 2
    ks_ref[0:seq, :] = _rope(_rms(k_ref[...], kn_ref[...], HEAD_W), rope_ref, half).astype(BF16)
    ks_ref[seq:, :] = ck_ref[...].astype(BF16)
    vs_ref[0:seq, :] = _with_ones(v_ref[...].astype(BF16))
    vs_ref[seq:, :] = _with_ones(cv_ref[...].astype(BF16))
    for g in range(group):
        q = _rms(q_ref[:, g * HEAD_W:(g + 1) * HEAD_W], qn_ref[...], HEAD_W) * (HEAD_W ** -0.5)
        qs_ref[g] = _rope(q, rope_ref, half).astype(BF16)
    kk = ks_ref[...]
    vv = vs_ref[...]
    for g in range(group):
        def q_step(i, carry, g=g):
            qrows = pl.ds(pl.multiple_of(i * tq, tq), tq)
            o = _softmax_pv([_dot_nt(qs_ref[g, qrows, :], kk)], [vv])
            o_ref[qrows, g * HEAD_W:(g + 1) * HEAD_W] = o.astype(o_ref.dtype)
            return carry

        lax.fori_loop(0, seq // tq, q_step, 0, unroll=2)


def _latent_gqa(proj, row_block0, n_seq, seq, cache_k, cache_v, layer, rope, gqa_qn, gqa_kn, mixed_prev):
    past = cache_k.shape[3]
    group = N_HEADS // GQA_KV_HEADS
    tq = 256
    cache = pl.BlockSpec((None, None, None, past, HEAD_W), lambda b, n: (b, layer, n, 0, 0))
    vec = pl.BlockSpec((1, HEAD_W), lambda b, n: (0, 0))
    return pl.pallas_call(
        functools.partial(_gqa_body, seq=seq, tq=tq),
        grid=(n_seq, GQA_KV_HEADS),
        in_specs=[pl.BlockSpec((seq, group * HEAD_W), lambda b, n: (row_block0 + b, COL_C_Q // group + n)),
                  pl.BlockSpec((seq, HEAD_W), lambda b, n: (row_block0 + b, COL_C_K + n)),
                  pl.BlockSpec((seq, HEAD_W), lambda b, n: (row_block0 + b, COL_C_V + n)),
                  cache, cache,
                  pl.BlockSpec((3, seq, HEAD_W), lambda b, n: (0, 0, 0)),
                  vec, vec, ANY_SPEC],
        out_specs=pl.BlockSpec((seq, group * HEAD_W), lambda b, n: (row_block0 + b, n)),
        out_shape=jax.ShapeDtypeStruct(mixed_prev.shape, BF16),
        input_output_aliases={8: 0},
        scratch_shapes=[pltpu.VMEM((group, seq, HEAD_W), BF16),
                        pltpu.VMEM((seq + past, HEAD_W), BF16),
                        pltpu.VMEM((seq + past, 2 * HEAD_W), BF16)],
        compiler_params=_params("parallel", "parallel"),
        name="latent_gqa_attention",
    )(proj, proj, proj, cache_k, cache_v, rope, gqa_qn[None, :], gqa_kn[None, :], mixed_prev)


def _diff_body(q_ref, k_ref, v_ref, ck_ref, cv_ref, rope_ref, qn_ref, kn_ref, sub_ref, lam_ref, prev_ref, o_ref,
               qs_ref, k0_ref, k1_ref, vs_ref, *, seq, tq, lam_init):
    del prev_ref
    half = DF_DQK // 2
    k = _rope(_rms_halves(k_ref[...], kn_ref[...]), rope_ref, half)
    lo = _lane_lt(k.shape, DF_DQK)
    k0_ref[0:seq, :] = jnp.where(lo, k, 0.0).astype(BF16)
    k1_ref[0:seq, :] = jnp.where(lo, 0.0, k).astype(BF16)
    ck = ck_ref[...]
    lo_c = _lane_lt(ck.shape, DF_DQK)
    k0_ref[seq:, :] = jnp.where(lo_c, ck, 0.0).astype(BF16)
    k1_ref[seq:, :] = jnp.where(lo_c, 0.0, ck).astype(BF16)
    vs_ref[0:seq, :] = _with_ones(v_ref[...].astype(BF16))
    vs_ref[seq:, :] = _with_ones(cv_ref[...].astype(BF16))
    q = _rms_halves(q_ref[...], qn_ref[...]) * (DF_DQK ** -0.5)
    qs_ref[...] = _rope(q, rope_ref, half).astype(BF16)
    lam = _lambda(lam_ref, lam_init)
    k0 = k0_ref[...]
    k1 = k1_ref[...]
    vv = vs_ref[...]

    def q_step(i, carry):
        qrows = pl.ds(pl.multiple_of(i * tq, tq), tq)
        qb = qs_ref[qrows, :]
        o = _diff_pv(_dot_nt(qb, k0), _dot_nt(qb, k1), lam, vv)
        o_ref[qrows, :] = (_rms(o, sub_ref[...], HEAD_W) * (1.0 - lam_init)).astype(o_ref.dtype)
        return carry

    lax.fori_loop(0, seq // tq, q_step, 0, unroll=2)


def _latent_diff(proj, row_block0, n_seq, seq, cache_k2, cache_v, layer, rope, df_qn, df_kn, df_subln, df_lam,
                 lam_init, mixed_prev):
    past = cache_k2.shape[3]
    tq = 256

    def col(cb):
        return pl.BlockSpec((seq, HEAD_W), lambda b, h, cb=cb: (row_block0 + b, cb + h))

    cache = pl.BlockSpec((None, None, None, past, HEAD_W), lambda b, h: (b, layer, h, 0, 0))
    vec = pl.BlockSpec((1, HEAD_W), lambda b, h: (0, 0))
    kv_scratch = pltpu.VMEM((seq + past, HEAD_W), BF16)
    return pl.pallas_call(
        functools.partial(_diff_body, seq=seq, tq=tq, lam_init=lam_init),
        grid=(n_seq, N_HEADS),
        in_specs=[col(COL_D_Q), col(COL_D_K), col(COL_D_V), cache, cache,
                  pl.BlockSpec((3, seq, HEAD_W), lambda b, h: (0, 0, 0)),
                  vec, vec, vec, pl.BlockSpec((4, DF_DQK), lambda b, h: (0, 0)), ANY_SPEC],
        out_specs=pl.BlockSpec((seq, HEAD_W), lambda b, h: (row_block0 + b, h)),
        out_shape=jax.ShapeDtypeStruct(mixed_prev.shape, BF16),
        input_output_aliases={10: 0},
        scratch_shapes=[pltpu.VMEM((seq, HEAD_W), BF16), kv_scratch, kv_scratch,
                        pltpu.VMEM((seq + past, 2 * HEAD_W), BF16)],
        compiler_params=_params("parallel", "parallel"),
        name="latent_diff_attention",
    )(proj, proj, proj, cache_k2, cache_v, rope, jnp.tile(df_qn, 2)[None, :], jnp.tile(df_kn, 2)[None, :],
      df_subln[None, :], df_lam, mixed_prev)


def _first_max(vals):
    best = vals[0]
    idx = jnp.zeros(best.shape, jnp.int32)
    for i in range(1, len(vals)):
        better = vals[i] > best
        best = jnp.where(better, vals[i], best)
        idx = jnp.where(better, i, idx)
    return best, idx


def _pick(vals, idx):
    out = vals[0]
    for i in range(1, len(vals)):
        out = jnp.where(idx == i, vals[i], out)
    return out


def _outproj_body(ma_ref, mb_ref, mc_ref, md_ref, w_ref, h_ref, mod_ref, n2_ref, rw_ref, rb_ref,
                  h1_ref, x2_ref, idx_ref, gate_ref, mixed_ref):
    for g, m_ref in enumerate((ma_ref, mb_ref, mc_ref, md_ref)):
        mixed_ref[:, g * GROUP_W:(g + 1) * GROUP_W] = m_ref[...]
    h1 = h_ref[...] + mod_ref[2:3, :] * _dot(mixed_ref[...], w_ref[...])
    h1_ref[...] = h1
    x2 = _rms(h1, n2_ref[...], D_MODEL) * (1.0 + mod_ref[4:5, :]) + mod_ref[3:4, :]
    x2_ref[...] = x2
    logits = lax.dot_general(rw_ref[...], x2, (((1,), (1,)), ((), ())), precision=lax.Precision.HIGHEST,
                             preferred_element_type=F32)
    aff_all = _sigmoid(logits)
    sel_all = aff_all + rb_ref[...]
    aff = [aff_all[e:e + 1, :] for e in range(N_EXPERTS)]
    sel = [sel_all[e:e + 1, :] for e in range(N_EXPERTS)]
    neg = jnp.full(sel[0].shape, -jnp.inf, F32)
    scores = []
    for g in range(N_EXP_GROUPS):
        grp = sel[g * EXP_PER_GROUP:(g + 1) * EXP_PER_GROUP]
        m1, i1 = _first_max(grp)
        m2, _ = _first_max([jnp.where(i1 == j, neg, grp[j]) for j in range(EXP_PER_GROUP)])
        scores.append(m1 + m2)
    _, g_best = _first_max(scores)
    in_sel = [_pick([sel[g * EXP_PER_GROUP + j] for g in range(N_EXP_GROUPS)], g_best)
              for j in range(EXP_PER_GROUP)]
    in_aff = [_pick([aff[g * EXP_PER_GROUP + j] for g in range(N_EXP_GROUPS)], g_best)
              for j in range(EXP_PER_GROUP)]
    _, l1 = _first_max(in_sel)
    _, l2 = _first_max([jnp.where(l1 == j, neg, in_sel[j]) for j in range(EXP_PER_GROUP)])
    w1 = _pick(in_aff, l1)
    w2 = _pick(in_aff, l2)
    idx_ref[0:1, :] = g_best * EXP_PER_GROUP + l1
    idx_ref[1:2, :] = g_best * EXP_PER_GROUP + l2
    gate_ref[0:1, :] = w1 / (w1 + w2)
    gate_ref[1:2, :] = w2 / (w1 + w2)


def _output_projection(mixed4, w_out_bf16, h, mod, norm2, router_w, router_b, cond_of_tile, tm):
    t, d = h.shape
    slab = pl.BlockSpec((tm, GROUP_W), lambda i: (i, 0))
    return pl.pallas_call(
        _outproj_body,
        grid=(t // tm,),
        in_specs=[slab, slab, slab, slab,
                  pl.BlockSpec((d, d), lambda i: (0, 0)),
                  pl.BlockSpec((tm, d), lambda i: (i, 0)),
                  pl.BlockSpec((None, 6, d), lambda i: (cond_of_tile(i), 0, 0)),
                  pl.BlockSpec((1, d), lambda i: (0, 0)),
                  pl.BlockSpec((N_EXPERTS, d), lambda i: (0, 0)),
                  pl.BlockSpec((N_EXPERTS, 1), lambda i: (0, 0))],
        out_specs=[pl.BlockSpec((tm, d), lambda i: (i, 0)),
                   pl.BlockSpec((tm, d), lambda i: (i, 0)),
                   pl.BlockSpec((2, tm), lambda i: (0, i)),
                   pl.BlockSpec((2, tm), lambda i: (0, i))],
        out_shape=[jax.ShapeDtypeStruct((t, d), F32), jax.ShapeDtypeStruct((t, d), F32),
                   jax.ShapeDtypeStruct((2, t), jnp.int32), jax.ShapeDtypeStruct((2, t), F32)],
        scratch_shapes=[pltpu.VMEM((tm, d), BF16)],
        compiler_params=_params("parallel"),
        name="out_proj_residual_router",
    )(*mixed4, w_out_bf16, h, mod, norm2[None, :], router_w.T, router_b[:, None])


SUBLANES = 8


def _vmem_row(ref, base, u):
    return ref.at[pl.ds(base, SUBLANES), :].at[pl.ds(u, 1), :]


def _for_row_groups(n_rows, fn):
    def body(g, carry):
        base = pl.multiple_of(g * SUBLANES, SUBLANES)
        for u in range(SUBLANES):
            fn(base, u)
        return carry

    lax.fori_loop(0, n_rows // SUBLANES, body, 0)


def _rows_wait(src_hbm, dst, sem, n_rows):
    pltpu.make_async_copy(src_hbm.at[pl.ds(0, n_rows), :], dst.at[pl.ds(0, n_rows), :], sem).wait()


def _dispatch_body(dest_ref, ps_ref, pe_ref, x_ref, xs_hbm, zero_ref, sem, zsem, *, n_tok, tile):
    i = pl.program_id(0)

    def zero_copy(e):
        first = pl.multiple_of(pe_ref[e] - MOE_ROWS, MOE_ROWS)
        return pltpu.make_async_copy(zero_ref, xs_hbm.at[pl.ds(first, MOE_ROWS), :], zsem)

    @pl.when(i == 0)
    def _():
        zero_ref[...] = jnp.zeros(zero_ref.shape, zero_ref.dtype)
        for e in range(N_EXPERTS):
            @pl.when(pe_ref[e] > ps_ref[e])
            def _(e=e):
                zero_copy(e).start()
        for e in range(N_EXPERTS):
            @pl.when(pe_ref[e] > ps_ref[e])
            def _(e=e):
                zero_copy(e).wait()

    for k in range(2):
        def scatter_row(base, u, k=k):
            row = dest_ref[k * n_tok + i * tile + base + u]
            pltpu.make_async_copy(_vmem_row(x_ref, base, u), xs_hbm.at[pl.ds(row, 1), :], sem).start()

        _for_row_groups(tile, scatter_row)
    for k in range(2):
        _rows_wait(x_ref, xs_hbm, sem, tile)


def _dispatch(x2, dest, pad_start, pad_end, n_rows):
    t, d = x2.shape
    tile = MOE_ROWS
    grid_spec = pltpu.PrefetchScalarGridSpec(
        num_scalar_prefetch=3,
        grid=(t // tile,),
        in_specs=[pl.BlockSpec((tile, d), lambda i, dst, ps, pe: (i, 0))],
        out_specs=ANY_SPEC,
        scratch_shapes=[pltpu.VMEM((MOE_ROWS, d), F32), pltpu.SemaphoreType.DMA, pltpu.SemaphoreType.DMA],
    )
    return pl.pallas_call(
        functools.partial(_dispatch_body, n_tok=t, tile=tile),
        grid_spec=grid_spec,
        out_shape=jax.ShapeDtypeStruct((n_rows, d), F32),
        compiler_params=_params("arbitrary"),
        name="moe_dispatch",
    )(dest, pad_start, pad_end, x2)


def _expert_body(be_ref, nb_ref, x_ref, wg_ref, wu_ref, wd_ref, o_ref, wg_bf, wu_bf, wd_bf):
    i = pl.program_id(0)

    @pl.when(i < nb_ref[0])
    def _():
        @pl.when((i == 0) | (be_ref[i] != be_ref[jnp.maximum(i - 1, 0)]))
        def _():
            wg_bf[...] = wg_ref[...].astype(BF16)
            wu_bf[...] = wu_ref[...].astype(BF16)
            wd_bf[...] = wd_ref[...].astype(BF16)

        x = x_ref[...].astype(BF16)
        hdn = _silu(_dot(x, wg_bf[...])) * _dot(x, wu_bf[...])
        o_ref[...] = _dot(hdn.astype(BF16), wd_bf[...])

    @pl.when(i >= nb_ref[0])
    def _():
        o_ref[...] = jnp.zeros(o_ref.shape, o_ref.dtype)


def _expert_blocks(xs, block_expert, n_used, w_gate, w_up, w_down, layer):
    n_rows, d = xs.shape
    ff = w_gate.shape[-1]
    grid_spec = pltpu.PrefetchScalarGridSpec(
        num_scalar_prefetch=2,
        grid=(n_rows // MOE_ROWS,),
        in_specs=[pl.BlockSpec((MOE_ROWS, d), lambda i, be, nb: (jnp.minimum(i, nb[0] - 1), 0)),
                  pl.BlockSpec((None, None, d, ff), lambda i, be, nb: (layer, be[i], 0, 0)),
                  pl.BlockSpec((None, None, d, ff), lambda i, be, nb: (layer, be[i], 0, 0)),
                  pl.BlockSpec((None, None, ff, d), lambda i, be, nb: (layer, be[i], 0, 0))],
        out_specs=pl.BlockSpec((MOE_ROWS, d), lambda i, be, nb: (i, 0)),
        scratch_shapes=[pltpu.VMEM((d, ff), BF16), pltpu.VMEM((d, ff), BF16), pltpu.VMEM((ff, d), BF16)],
    )
    return pl.pallas_call(
        _expert_body,
        grid_spec=grid_spec,
        out_shape=jax.ShapeDtypeStruct((n_rows, d), F32),
        compiler_params=_params("arbitrary"),
        name="moe_expert_blocks",
    )(block_expert, n_used, xs, w_gate, w_up, w_down)


def _combine_body(dest_ref, h_ref, gate_ref, y_hbm, mod_ref, *rest, n_tok, tile, ctx_tiles):
    out_refs, (ybuf, sem) = rest[:-2], rest[-2:]
    i = pl.program_id(0)
    n_tiles = pl.num_programs(0)

    def start(blk, slot):
        for k in range(2):
            def gather_row(base, u, k=k):
                row = dest_ref[k * n_tok + blk * tile + base + u]
                pltpu.make_async_copy(y_hbm.at[pl.ds(row, 1), :], _vmem_row(ybuf.at[slot, k], base, u),
                                      sem.at[slot]).start()

            _for_row_groups(tile, gather_row)

    @pl.when(i == 0)
    def _():
        start(0, 0)

    @pl.when(i + 1 < n_tiles)
    def _():
        start(i + 1, (i + 1) % 2)

    slot = i % 2
    for k in range(2):
        _rows_wait(y_hbm, ybuf.at[slot, k], sem.at[slot], tile)
    gate = gate_ref[...]
    out = h_ref[...] + mod_ref[5:6, :] * (gate[:, 0:1] * ybuf[slot, 0] + gate[:, 1:2] * ybuf[slot, 1])
    if ctx_tiles is None:
        out_refs[0][...] = out
    else:
        @pl.when(i < ctx_tiles)
        def _():
            out_refs[0][...] = out

        @pl.when(i >= ctx_tiles)
        def _():
            out_refs[1][...] = out


def _combine(h1, yb, dest, gates, mod, cond_of_tile, tile, split_rows):
    t, d = h1.shape
    row_tile = pl.BlockSpec((tile, d), lambda i, dst: (i, 0))
    if split_rows is None:
        ctx_tiles = None
        out_specs = row_tile
        out_shape = jax.ShapeDtypeStruct((t, d), F32)
    else:
        ctx_tiles = split_rows // tile
        out_specs = [pl.BlockSpec((tile, d), lambda i, dst: (jnp.minimum(i, ctx_tiles - 1), 0)),
                     pl.BlockSpec((tile, d), lambda i, dst: (jnp.maximum(i - ctx_tiles, 0), 0))]
        out_shape = [jax.ShapeDtypeStruct((split_rows, d), F32), jax.ShapeDtypeStruct((t - split_rows, d), F32)]
    grid_spec = pltpu.PrefetchScalarGridSpec(
        num_scalar_prefetch=1,
        grid=(t // tile,),
        in_specs=[row_tile,
                  pl.BlockSpec((tile, 2), lambda i, dst: (i, 0)),
                  ANY_SPEC,
                  pl.BlockSpec((None, 6, d), lambda i, dst: (cond_of_tile(i), 0, 0))],
        out_specs=out_specs,
        scratch_shapes=[pltpu.VMEM((2, 2, tile, d), F32), pltpu.SemaphoreType.DMA((2,))],
    )
    return pl.pallas_call(
        functools.partial(_combine_body, n_tok=t, tile=tile, ctx_tiles=ctx_tiles),
        grid_spec=grid_spec,
        out_shape=out_shape,
        compiler_params=_params("arbitrary"),
        name="moe_gated_residual",
    )(dest, h1, gates, yb, mod)


def _moe(h1, x2, idx_t, gate_t, mod, w_gate, w_up, w_down, layer, cond_of_tile, split_rows):
    t, d = h1.shape
    n = 2 * t
    experts = idx_t.reshape(n)
    onehot = (experts[:, None] == jnp.arange(N_EXPERTS, dtype=jnp.int32)[None, :]).astype(BF16)
    blocks = onehot.reshape(n // MOE_ROWS, MOE_ROWS, N_EXPERTS)
    tri = jnp.asarray(np.tril(np.ones((MOE_ROWS, MOE_ROWS), np.float32)), BF16)
    within = jnp.einsum('ij,bjk->bik', tri, blocks, preferred_element_type=F32)
    block_total = within[:, -1, :]
    block_first = jnp.cumsum(block_total, axis=0) - block_total
    counts = (block_first[-1] + block_total[-1]).astype(jnp.int32)
    before = (within + block_first[:, None, :]).reshape(n, N_EXPERTS) - 1.0
    rank = jnp.sum(before * onehot.astype(F32), axis=1).astype(jnp.int32)
    padded = (counts + MOE_ROWS - 1) // MOE_ROWS * MOE_ROWS
    pad_end = jnp.cumsum(padded).astype(jnp.int32)
    pad_start = pad_end - padded
    dest = (pad_start[experts] + rank).astype(jnp.int32)
    n_blocks = (n + N_EXPERTS * (MOE_ROWS - 1) + MOE_ROWS - 1) // MOE_ROWS
    block_first_row = jnp.arange(n_blocks, dtype=jnp.int32) * MOE_ROWS
    block_expert = jnp.minimum(jnp.sum((pad_end[None, :] <= block_first_row[:, None]).astype(jnp.int32), axis=1),
                               N_EXPERTS - 1)
    n_used = (pad_end[-1:] // MOE_ROWS).astype(jnp.int32)
    xs = _dispatch(x2, dest, pad_start, pad_end, n_blocks * MOE_ROWS)
    yb = _expert_blocks(xs, block_expert, n_used, w_gate, w_up, w_down, layer)
    return _combine(h1, yb, dest, gate_t.T, mod, cond_of_tile, MOE_ROWS, split_rows)


def kernel(x_prompt, x_sample, cache_na_k, cache_na_v, cache_gqa_k, cache_gqa_v, cache_diff_k, cache_diff_v, state_hgrn, c, c_ctx, w_mod, b_mod, norm1, norm2, w_in, w_out, hg_lb_logits, hg_onorm, na_qn, na_kn, na_rpb, gqa_qn, gqa_kn, df_qn, df_kn, df_lam, df_subln, router_w, router_b, w_gate, w_up, w_down):
    n_ctx, ctx_len, d = x_prompt.shape
    n_lat, lat_len, _ = x_sample.shape
    depth = w_in.shape[0]
    t_ctx = n_ctx * ctx_len
    assert t_ctx % lat_len == 0 and lat_len % GRID_W == 0 and lat_len // GRID_W >= WIN_ROWS
    tm = next(m for m in (1024, 512, 256) if t_ctx % m == 0 and lat_len % m == 0)
    tm2 = min(tm, 512)
    lat_block0 = t_ctx // lat_len

    def cond_tile(tile_rows):
        def cond_of_tile(i):
            return jnp.where(i < t_ctx // tile_rows, 0, 1 + (i - t_ctx // tile_rows) // (lat_len // tile_rows))
        return cond_of_tile

    sm = jax.nn.softmax(hg_lb_logits.astype(F32), axis=0)
    lower = jnp.cumsum(sm, axis=0) - sm[0:1]
    mod_all = _modulation(jnp.concatenate([c_ctx[None, :], c], axis=0), w_mod, b_mod)
    mod_all = mod_all.reshape(depth, 1 + n_lat, 6, d)
    hgrn_consts = _hgrn_constants(HGRN_CHUNK)
    rope_c = _rope_tables(lat_len, HEAD_W)
    rope_d = _rope_tables(lat_len, DF_DQK)
    past = cache_diff_k.shape[4]
    cache_diff_k2 = cache_diff_k.transpose(0, 1, 2, 4, 3, 5).reshape(n_lat, depth, N_HEADS, past, HEAD_W)

    h = jnp.concatenate([x_prompt.reshape(t_ctx, d), x_sample.reshape(n_lat * lat_len, d)], axis=0)
    caches, states = [], None
    for layer in range(depth):
        mod = mod_all[layer]
        lam_init = 0.8 - 0.6 * math.exp(-0.3 * layer)
        proj = _input_projection(h, mod, norm1[layer], w_in[layer].astype(BF16), cond_tile(tm), tm)
        mix_a, states = _hgrn(proj, 0, n_ctx, ctx_len, lower[layer], hg_onorm[layer], hgrn_consts, None, None,
                              states, layer, depth)
        gains = (na_qn[layer], na_kn[layer], gqa_qn[layer], gqa_kn[layer], df_qn[layer], df_kn[layer],
                 df_subln[layer], df_lam[layer])
        mix_b, mix_c, mix_d, *caches = _context_attention(proj, n_ctx, ctx_len, gains, lam_init, layer, depth,
                                                          caches)
        mix_a, _ = _hgrn(proj, lat_block0, n_lat, lat_len, lower[layer], hg_onorm[layer], hgrn_consts,
                         state_hgrn, mix_a, None, layer, depth)
        mix_b = _latent_na(proj, lat_block0, n_lat, lat_len, cache_na_k, cache_na_v, layer,
                           _na_bias(na_rpb[layer]), na_qn[layer], na_kn[layer], mix_b)
        mix_c = _latent_gqa(proj, lat_block0, n_lat, lat_len, cache_gqa_k, cache_gqa_v, layer, rope_c,
                            gqa_qn[layer], gqa_kn[layer], mix_c)
        mix_d = _latent_diff(proj, lat_block0, n_lat, lat_len, cache_diff_k2, cache_diff_v, layer, rope_d,
                             df_qn[layer], df_kn[layer], df_subln[layer], df_lam[layer], lam_init, mix_d)
        h1, x2, idx_t, gate_t = _output_projection((mix_a, mix_b, mix_c, mix_d), w_out[layer].astype(BF16), h, mod,
                                                   norm2[layer], router_w, router_b, cond_tile(tm2), tm2)
        h = _moe(h1, x2, idx_t, gate_t, mod, w_gate, w_up, w_down, layer, cond_tile(MOE_ROWS),
                 t_ctx if layer == depth - 1 else None)
    y_prompt = h[0].reshape(n_ctx, ctx_len, d)
    y_sample = h[1].reshape(n_lat, lat_len, d)
    return (y_prompt, y_sample, *caches, states)
```

```python
import functools
import math

import numpy as np
import jax
import jax.numpy as jnp
from jax import lax
from jax.experimental import pallas as pl
from jax.experimental.pallas import tpu as pltpu

D_MODEL = 2048
GRID_W = 64
GROUP_W = D_MODEL // 4
N_HEADS = 4
HEAD_W = GROUP_W // N_HEADS
GQA_KV_HEADS = 2
DF_DQK = HEAD_W // 2
WIN_ROWS = 8
WIN_COLS = 16
N_EXPERTS = 16
N_EXP_GROUPS = 4
EXP_PER_GROUP = N_EXPERTS // N_EXP_GROUPS
EXPERT_FF = D_MODEL // 4
ROPE_THETA = 10000.0
EPS = 1e-6
NEG_INF = -1e30
IN_WIDTH = 13 * GROUP_W

COL_A_Q, COL_A_FF, COL_A_FB, COL_A_I, COL_A_G = 0, 4, 8, 12, 16
COL_B_Q, COL_B_K, COL_B_V = 20, 24, 28
COL_C_Q, COL_C_K, COL_C_V = 32, 36, 38
COL_D_Q, COL_D_K, COL_D_V = 40, 44, 48

HGRN_CHUNK = 128
HGRN_UNROLL = 2
HGRN_INPUT_VMEM = 24 * 1024 * 1024
MOE_ROWS = 256
VMEM_LIMIT = 48 * 1024 * 1024
VMEM_LIMIT_PROJ = 56 * 1024 * 1024

F32 = jnp.float32
BF16 = jnp.bfloat16
ANY_SPEC = pl.BlockSpec(memory_space=pl.ANY)


def _params(*sem, vmem=VMEM_LIMIT):
    return pltpu.CompilerParams(dimension_semantics=sem, vmem_limit_bytes=vmem)


def _sigmoid(x):
    return 1.0 / (1.0 + jnp.exp(-x))


def _silu(x):
    return x * _sigmoid(x)


def _rms(x, gain, n):
    return x * lax.rsqrt(jnp.sum(x * x, axis=-1, keepdims=True) * (1.0 / n) + EPS) * gain


def _dot(a, b):
    return jnp.dot(a, b, preferred_element_type=F32)


def _dot_nt(a, b):
    return lax.dot_general(a, b, (((1,), (1,)), ((), ())), preferred_element_type=F32)


def _dot_tn(a, b):
    return lax.dot_general(a, b, (((0,), (0,)), ((), ())), preferred_element_type=F32)


def _aligned(x, m):
    return x if isinstance(x, int) else pl.multiple_of(x, m)


def _alias_kwargs(n_inputs, prev, first_out):
    return ([ANY_SPEC] * len(prev), list(prev), {n_inputs + k: first_out + k for k in range(len(prev))})


def _mod_body(cond_ref, w_ref, b_ref, o_ref):
    w = w_ref[...]
    for c in range(cond_ref.shape[0]):
        s = _silu(cond_ref[c])
        o_ref[c:c + 1, :] = jnp.sum(w * s, axis=0, keepdims=True) + b_ref[...]


def _modulation(cond, w_mod, b_mod):
    depth, d, n6 = w_mod.shape
    nc = cond.shape[0]
    tn = 1024
    return pl.pallas_call(
        _mod_body,
        grid=(depth, n6 // tn),
        in_specs=[pl.BlockSpec((nc, d, 1), lambda l, j: (0, 0, 0)),
                  pl.BlockSpec((None, d, tn), lambda l, j: (l, 0, j)),
                  pl.BlockSpec((None, 1, tn), lambda l, j: (l, 0, j))],
        out_specs=pl.BlockSpec((None, nc, tn), lambda l, j: (l, 0, j)),
        out_shape=jax.ShapeDtypeStruct((depth, nc, n6), F32),
        compiler_params=_params("parallel", "parallel"),
        name="adaln_modulation",
    )(cond[:, :, None], w_mod, b_mod[:, None, :])


def _group_specs(tile, d, ctx_tiles):
    return [pl.BlockSpec((tile, d), lambda i, *_: (jnp.minimum(i, ctx_tiles - 1), 0)),
            pl.BlockSpec((tile, d), lambda i, *_: (jnp.maximum(i - ctx_tiles, 0), 0))]


def _inproj_body(hc_ref, hl_ref, mod_ref, n1_ref, w_ref, o_ref, xn_ref, *, ctx_tiles):
    def normalise(h_ref):
        y = _rms(h_ref[...], n1_ref[...], D_MODEL)
        xn_ref[...] = (y * (1.0 + mod_ref[1:2, :]) + mod_ref[0:1, :]).astype(BF16)

    first = pl.program_id(1) == 0
    is_ctx = pl.program_id(0) < ctx_tiles
    pl.when(first & is_ctx)(lambda: normalise(hc_ref))
    pl.when(first & jnp.logical_not(is_ctx))(lambda: normalise(hl_ref))
    o_ref[...] = _dot(xn_ref[...], w_ref[...])


def _input_projection(h_ctx, h_lat, mod, norm1, w_in_bf16, cond_of_tile, tm):
    d = h_ctx.shape[1]
    t = h_ctx.shape[0] + h_lat.shape[0]
    ctx_tiles = h_ctx.shape[0] // tm
    n = w_in_bf16.shape[1]
    tn = 512
    return pl.pallas_call(
        functools.partial(_inproj_body, ctx_tiles=ctx_tiles),
        grid=(t // tm, n // tn),
        in_specs=_group_specs(tm, d, ctx_tiles) + [
                  pl.BlockSpec((None, 6, d), lambda i, j: (cond_of_tile(i), 0, 0)),
                  pl.BlockSpec((1, d), lambda i, j: (0, 0)),
                  pl.BlockSpec((d, tn), lambda i, j: (0, j))],
        out_specs=pl.BlockSpec((tm, tn), lambda i, j: (i, j)),
        out_shape=jax.ShapeDtypeStruct((t, n), F32),
        scratch_shapes=[pltpu.VMEM((tm, d), BF16)],
        compiler_params=_params("parallel", "arbitrary", vmem=VMEM_LIMIT_PROJ),
        name="norm_modulate_in_proj",
    )(h_ctx, h_lat, mod, norm1[None, :], w_in_bf16)


def _hgrn_constants(c):
    nl = int(math.log2(c))
    idx = np.arange(c)
    e = np.zeros((nl + 2, c, c), np.float32)
    m = np.zeros((nl + 1, c, c), np.float32)
    e[0] = idx[None, :] <= idx[:, None]
    e[1] = idx[None, :] > idx[:, None]
    m[0] = np.eye(c)
    for li in range(nl):
        s = c >> (li + 1)
        parent = idx // (2 * s)
        right = (idx % (2 * s)) >= s
        ref = parent * 2 * s + s - 1
        for i in range(c):
            if right[i]:
                e[2 + li, i, ref[i] + 1:i + 1] = 1.0
            else:
                e[2 + li, i, i + 1:ref[i] + 1] = 1.0
        m[1 + li] = right[:, None] & ~right[None, :] & (parent[:, None] == parent[None, :])
    e2 = np.stack([e, e[:, ::-1, ::-1]]).reshape(2, (nl + 2) * c, c)
    m2 = np.stack([m, m[:, ::-1, ::-1]])
    return jnp.asarray(e2, BF16), jnp.asarray(m2, F32)


def _hgrn_body(*refs, seq, chunk, unroll, heads, has_s0, emit_state, n_alias):
    q_ref, ff_ref, fb_ref, i_ref, g_ref, lb_ref, on_ref, e_ref, m_ref = refs[:9]
    pos = 9
    s0_ref = None
    if has_s0:
        s0_ref = refs[pos]
        pos += 1
    pos += n_alias
    o_ref = refs[pos]
    pos += 1
    if emit_state:
        st_ref = refs[pos]
        pos += 1
    of_ref, ob_ref = refs[pos], refs[pos + 1]
    c = chunk
    n_chunks = seq // c
    assert seq % c == 0 and n_chunks % unroll == 0
    n_levels = m_ref.shape[1] - 1
    gate_refs = (ff_ref, fb_ref)
    out_refs = (of_ref, ob_ref)

    def chunk_step(c0, d, hh, st):
        rows = pl.ds(c0, c)
        lanes = slice(hh * HEAD_W, (hh + 1) * HEAD_W)
        lb = lb_ref[d:d + 1, lanes]
        f = lb + (1.0 - lb) * _sigmoid(gate_refs[d][rows, lanes])
        g = jnp.log(f)
        k = 1.0 - f
        q = _silu(q_ref[rows, lanes])
        v = i_ref[rows, lanes].astype(BF16)
        g_hi = g.astype(BF16)
        g_lo = (g - g_hi.astype(F32)).astype(BF16)
        g2 = _dot(e_ref[d], jnp.concatenate([g_hi, g_lo], axis=1))
        x = jnp.exp(g2[:, 0:HEAD_W] + g2[:, HEAD_W:2 * HEAD_W])
        x_cum = x[0:c]
        x_tail = x[c:2 * c]
        s = m_ref[d, 0] * _dot_nt(q.astype(BF16), k.astype(BF16))
        for lv in range(n_levels):
            x_l = x[(2 + lv) * c:(3 + lv) * c]
            s = s + m_ref[d, 1 + lv] * _dot_nt((q * x_l).astype(BF16), (k * x_l).astype(BF16))
        o = _dot_nt((q * x_cum).astype(BF16), st.astype(BF16)) + _dot(s.astype(BF16), v)
        out_refs[d][rows, lanes] = o
        total = x_cum[c - 1:c, :] if d == 0 else x_cum[0:1, :]
        return st * total + _dot_tn(v, (k * x_tail).astype(BF16))

    if has_s0:
        states0 = tuple(s0_ref[d, hh].T for hh in range(heads) for d in range(2))
    else:
        states0 = tuple(jnp.zeros((HEAD_W, HEAD_W), F32) for _ in range(2 * heads))

    def loop(t, states):
        states = list(states)
        for u in range(unroll):
            j = t * unroll + u
            for hh in range(heads):
                states[2 * hh] = chunk_step(_aligned(j * c, c), 0, hh, states[2 * hh])
                states[2 * hh + 1] = chunk_step(_aligned((n_chunks - 1 - j) * c, c), 1, hh, states[2 * hh + 1])
        return tuple(states)

    if n_chunks == unroll:
        states = loop(0, states0)
    else:
        states = lax.fori_loop(0, n_chunks // unroll, loop, states0)
    for hh in range(heads):
        lanes = slice(hh * HEAD_W, (hh + 1) * HEAD_W)
        o = of_ref[:, lanes] + ob_ref[:, lanes]
        o_ref[:, lanes] = (_rms(o, on_ref[...], HEAD_W) * _silu(g_ref[:, lanes])).astype(o_ref.dtype)
        if emit_state:
            st_ref[0, hh] = states[2 * hh].T
            st_ref[1, hh] = states[2 * hh + 1].T


def _hgrn(proj, row_block0, n_seq, seq, lower, onorm, consts, s0, mixed_prev, state_prev, layer, depth):
    e_mat, masks = consts
    latent = s0 is not None
    hps = next(n for n in (4, 2, 1) if 5 * 2 * seq * n * HEAD_W * 4 <= HGRN_INPUT_VMEM)
    width = hps * HEAD_W

    def col(cb):
        return pl.BlockSpec((seq, width), lambda b, h, cb=cb: (row_block0 + b, cb // hps + h))

    state_spec = pl.BlockSpec((None, None, 2, hps, HEAD_W, HEAD_W), lambda b, h: (b, layer, 0, h, 0, 0))
    in_specs = [col(COL_A_Q), col(COL_A_FF), col(COL_A_FB), col(COL_A_I), col(COL_A_G),
                pl.BlockSpec((2, width), lambda b, h: (0, h)),
                pl.BlockSpec((1, HEAD_W), lambda b, h: (0, 0)),
                pl.BlockSpec(e_mat.shape, lambda b, h: (0, 0, 0)),
                pl.BlockSpec(masks.shape, lambda b, h: (0, 0, 0, 0))]
    args = [proj, proj, proj, proj, proj, lower, onorm[None, :], e_mat, masks]
    if latent:
        in_specs.append(state_spec)
        args.append(s0)
        prev = [mixed_prev]
    else:
        prev = [] if state_prev is None else [state_prev]
    alias_specs, alias_args, aliases = _alias_kwargs(len(args), prev, 0 if latent else 1)
    out_specs = [pl.BlockSpec((seq, width), lambda b, h: (row_block0 + b, h))]
    out_shape = [jax.ShapeDtypeStruct((proj.shape[0], GROUP_W), BF16)]
    if not latent:
        out_specs.append(state_spec)
        out_shape.append(jax.ShapeDtypeStruct((n_seq, depth, 2, N_HEADS, HEAD_W, HEAD_W), F32))
    res = pl.pallas_call(
        functools.partial(_hgrn_body, seq=seq, chunk=HGRN_CHUNK, unroll=HGRN_UNROLL, heads=hps, has_s0=latent,
                          emit_state=not latent, n_alias=len(prev)),
        grid=(n_seq, N_HEADS // hps),
        in_specs=in_specs + alias_specs, out_specs=out_specs, out_shape=out_shape,
        input_output_aliases=aliases,
        scratch_shapes=[pltpu.VMEM((seq, width), F32), pltpu.VMEM((seq, width), F32)],
        compiler_params=_params("parallel", "parallel"),
        name="hgrn2_latent" if latent else "hgrn2_context",
    )(*args, *alias_args)
    return (res[0], None) if latent else res


def _with_ones(v):
    return jnp.concatenate([v, jnp.ones_like(v)], axis=1)


def _softmax_pv(scores, values1):
    mx = functools.reduce(jnp.maximum, [jnp.max(s, axis=-1, keepdims=True) for s in scores])
    acc = functools.reduce(lambda a, b: a + b,
                           [_dot(jnp.exp(s - mx).astype(BF16), v) for s, v in zip(scores, values1)])
    return acc[:, 0:HEAD_W] / acc[:, HEAD_W:HEAD_W + 1]


def _diff_pv(s0, s1, lam, values1):
    return _softmax_pv([s0], [values1]) - lam * _softmax_pv([s1], [values1])


def _lane_lt(shape, n):
    return lax.broadcasted_iota(jnp.int32, shape, len(shape) - 1) < n


def _rms_halves(x, gain2):
    lo = _lane_lt(x.shape, DF_DQK)
    sq = x * x
    ss_lo = jnp.sum(jnp.where(lo, sq, 0.0), axis=-1, keepdims=True)
    ss_hi = jnp.sum(sq, axis=-1, keepdims=True) - ss_lo
    inv = jnp.where(lo, lax.rsqrt(ss_lo * (1.0 / DF_DQK) + EPS), lax.rsqrt(ss_hi * (1.0 / DF_DQK) + EPS))
    return x * inv * gain2


def _lambda(lam_ref, lam_init):
    l = lam_ref[...]
    return (jnp.exp(jnp.sum(l[0:1] * l[1:2], axis=-1, keepdims=True))
            - jnp.exp(jnp.sum(l[2:3] * l[3:4], axis=-1, keepdims=True)) + lam_init)


N_CTX_ATTN_INPUTS = 17


def _ctx_attn_body(*refs, lam_init):
    (bq_ref, bk_ref, bv_ref, cq_ref, ck_ref, cv_ref, dq_ref, dk_ref, dv_ref,
     naq_ref, nak_ref, gq_ref, gk_ref, dfq_ref, dfk_ref, sub_ref, lam_ref) = refs[:N_CTX_ATTN_INPUTS]
    ob_ref, oc_ref, od_ref, kb_ref, vb_ref, kc_ref, vc_ref, kd_ref, vd_ref = refs[-9:]
    group = N_HEADS // GQA_KV_HEADS
    scale = HEAD_W ** -0.5
    lam = _lambda(lam_ref, lam_init)
    kc = _rms(ck_ref[...], gk_ref[...], HEAD_W)
    vc = cv_ref[...]
    kc_ref[...] = kc
    vc_ref[...] = vc
    kc16 = kc.astype(BF16)
    vc1 = _with_ones(vc.astype(BF16))
    for g in range(group):
        lanes = slice(g * HEAD_W, (g + 1) * HEAD_W)
        qc = (_rms(cq_ref[:, lanes], gq_ref[...], HEAD_W) * scale).astype(BF16)
        oc_ref[:, lanes] = _softmax_pv([_dot_nt(qc, kc16)], [vc1]).astype(oc_ref.dtype)
        kb = _rms(bk_ref[:, lanes], nak_ref[...], HEAD_W)
        vb = bv_ref[:, lanes]
        kb_ref[g] = kb
        vb_ref[g] = vb
        qb = (_rms(bq_ref[:, lanes], naq_ref[...], HEAD_W) * scale).astype(BF16)
        ob_ref[:, lanes] = _softmax_pv([_dot_nt(qb, kb.astype(BF16))],
                                       [_with_ones(vb.astype(BF16))]).astype(ob_ref.dtype)
        kd = _rms_halves(dk_ref[:, lanes], dfk_ref[...])
        vd = dv_ref[:, lanes]
        kd_ref[g, 0] = kd[:, 0:DF_DQK]
        kd_ref[g, 1] = kd[:, DF_DQK:2 * DF_DQK]
        vd_ref[g] = vd
        qd = (_rms_halves(dq_ref[:, lanes], dfq_ref[...]) * (DF_DQK ** -0.5)).astype(BF16)
        lo = _lane_lt(kd.shape, DF_DQK)
        s0 = _dot_nt(qd, jnp.where(lo, kd, 0.0).astype(BF16))
        s1 = _dot_nt(qd, jnp.where(lo, 0.0, kd).astype(BF16))
        od = _diff_pv(s0, s1, lam, _with_ones(vd.astype(BF16)))
        od_ref[:, lanes] = (_rms(od, sub_ref[...], HEAD_W) * (1.0 - lam_init)).astype(od_ref.dtype)


def _context_attention(proj, n_seq, seq, gains, lam_init, layer, depth, caches_prev):
    na_qn, na_kn, gqa_qn, gqa_kn, df_qn, df_kn, df_subln, df_lam = gains
    group = N_HEADS // GQA_KV_HEADS
    width = group * HEAD_W

    def heads(cb):
        return pl.BlockSpec((seq, width), lambda b, n, cb=cb: (b, cb // group + n))

    def kv_head(cb):
        return pl.BlockSpec((seq, HEAD_W), lambda b, n, cb=cb: (b, cb + n))

    vec = pl.BlockSpec((1, HEAD_W), lambda b, n: (0, 0))
    cache_heads = pl.BlockSpec((None, None, group, seq, HEAD_W), lambda b, n: (b, layer, n, 0, 0))
    cache_kv = pl.BlockSpec((None, None, None, seq, HEAD_W), lambda b, n: (b, layer, n, 0, 0))
    mixed = pl.BlockSpec((seq, width), lambda b, n: (b, n))
    mixed_shape = jax.ShapeDtypeStruct((proj.shape[0], GROUP_W), BF16)
    cache4 = jax.ShapeDtypeStruct((n_seq, depth, N_HEADS, seq, HEAD_W), F32)
    cache2 = jax.ShapeDtypeStruct((n_seq, depth, GQA_KV_HEADS, seq, HEAD_W), F32)
    cache_dk = jax.ShapeDtypeStruct((n_seq, depth, N_HEADS, 2, seq, DF_DQK), F32)
    args = [proj] * 9 + [na_qn[None, :], na_kn[None, :], gqa_qn[None, :], gqa_kn[None, :],
                         jnp.tile(df_qn, 2)[None, :], jnp.tile(df_kn, 2)[None, :], df_subln[None, :], df_lam]
    assert len(args) == N_CTX_ATTN_INPUTS
    alias_specs, alias_args, aliases = _alias_kwargs(len(args), caches_prev, 3)
    return pl.pallas_call(
        functools.partial(_ctx_attn_body, lam_init=lam_init),
        grid=(n_seq, GQA_KV_HEADS),
        in_specs=[heads(COL_B_Q), heads(COL_B_K), heads(COL_B_V),
                  heads(COL_C_Q), kv_head(COL_C_K), kv_head(COL_C_V),
                  heads(COL_D_Q), heads(COL_D_K), heads(COL_D_V),
                  vec, vec, vec, vec, vec, vec, vec,
                  pl.BlockSpec((4, DF_DQK), lambda b, n: (0, 0))] + alias_specs,
        out_specs=[mixed, mixed, mixed, cache_heads, cache_heads, cache_kv, cache_kv,
                   pl.BlockSpec((None, None, group, 2, seq, DF_DQK), lambda b, n: (b, layer, n, 0, 0, 0)),
                   cache_heads],
        out_shape=[mixed_shape, mixed_shape, mixed_shape, cache4, cache4, cache2, cache2, cache_dk, cache4],
        input_output_aliases=aliases,
        compiler_params=_params("parallel", "parallel"),
        name="context_attention",
    )(*args, *alias_args)


def _rope_tables(n_tokens, rot_dim):
    t = np.arange(n_tokens)
    row = (t // GRID_W).astype(np.float32)
    col = (t % GRID_W).astype(np.float32)
    n_freq = rot_dim // 4
    inv = (np.float32(ROPE_THETA) ** (-np.arange(n_freq, dtype=np.float32) / np.float32(n_freq))).astype(np.float32)
    ang = np.concatenate([row[:, None] * inv, col[:, None] * inv], axis=-1).astype(np.float32)
    cos, sin, zero = np.cos(ang), np.sin(ang), np.zeros_like(ang)
    reps = HEAD_W // rot_dim
    a = np.tile(np.concatenate([cos, cos], axis=-1), (1, reps))
    b = np.tile(np.concatenate([-sin, zero], axis=-1), (1, reps))
    c = np.tile(np.concatenate([zero, sin], axis=-1), (1, reps))
    return jnp.asarray(np.stack([a, b, c]), F32)


def _rope(x, tab_ref, half):
    return (x * tab_ref[0] + pltpu.roll(x, HEAD_W - half, 1) * tab_ref[1]
            + pltpu.roll(x, half, 1) * tab_ref[2])


def _na_body(q_ref, k_ref, v_ref, ck_ref, cv_ref, bias_ref, qn_ref, kn_ref, prev_ref, o_ref,
             qs_ref, ks_ref, vs_ref, *, seq):
    del prev_ref
    rows = seq // GRID_W
    n_win = WIN_ROWS * GRID_W
    qs_ref[...] = (_rms(q_ref[...], qn_ref[...], HEAD_W) * (HEAD_W ** -0.5)).astype(BF16)
    ks_ref[...] = _rms(k_ref[...], kn_ref[...], HEAD_W).astype(BF16)
    vs_ref[...] = _with_ones(v_ref[...].astype(BF16))
    ck = ck_ref[...].astype(BF16)
    cv = _with_ones(cv_ref[...].astype(BF16))

    def row_step(r, carry):
        start = jnp.clip(r - WIN_ROWS // 2, 0, rows - WIN_ROWS)
        win = pl.ds(pl.multiple_of(start * GRID_W, GRID_W), n_win)
        qrows = pl.ds(pl.multiple_of(r * GRID_W, GRID_W), GRID_W)
        q = qs_ref[qrows, :]
        s_win = _dot_nt(q, ks_ref[win, :]) + bias_ref[start - r + (WIN_ROWS - 1)]
        s_ctx = _dot_nt(q, ck)
        o_ref[qrows, :] = _softmax_pv([s_win, s_ctx], [vs_ref[win, :], cv]).astype(o_ref.dtype)
        return carry

    lax.fori_loop(0, rows, row_step, 0, unroll=2)


def _na_bias(rpb):
    col = np.arange(GRID_W)
    col_start = np.clip(col - WIN_COLS // 2, 0, GRID_W - WIN_COLS)
    col_ok = (col[None, :] >= col_start[:, None]) & (col[None, :] < col_start[:, None] + WIN_COLS)
    dc = np.clip(col[None, :] - col[:, None] + WIN_COLS - 1, 0, 2 * WIN_COLS - 2).reshape(-1)
    onehot = (np.arange(2 * WIN_COLS - 1)[:, None] == dc[None, :]).astype(np.float32)
    per_dr = jnp.einsum('hdc,cn->hdn', rpb.astype(F32), jnp.asarray(onehot), precision=lax.Precision.HIGHEST)
    per_dr = jnp.where(col_ok[None, None], per_dr.reshape(rpb.shape[0], -1, GRID_W, GRID_W), NEG_INF)
    wins = jnp.stack([per_dr[:, o:o + WIN_ROWS] for o in range(WIN_ROWS)], axis=1)
    return wins.transpose(0, 1, 3, 2, 4).reshape(rpb.shape[0], WIN_ROWS, GRID_W, WIN_ROWS * GRID_W)


def _latent_na(proj, row_block0, n_seq, seq, cache_k, cache_v, layer, bias, na_qn, na_kn, mixed_prev):
    past = cache_k.shape[3]

    def col(cb):
        return pl.BlockSpec((seq, HEAD_W), lambda b, h, cb=cb: (row_block0 + b, cb + h))

    cache = pl.BlockSpec((None, None, None, past, HEAD_W), lambda b, h: (b, layer, h, 0, 0))
    vec = pl.BlockSpec((1, HEAD_W), lambda b, h: (0, 0))
    return pl.pallas_call(
        functools.partial(_na_body, seq=seq),
        grid=(n_seq, N_HEADS),
        in_specs=[col(COL_B_Q), col(COL_B_K), col(COL_B_V), cache, cache,
                  pl.BlockSpec((None, WIN_ROWS, GRID_W, WIN_ROWS * GRID_W), lambda b, h: (h, 0, 0, 0)),
                  vec, vec, ANY_SPEC],
        out_specs=pl.BlockSpec((seq, HEAD_W), lambda b, h: (row_block0 + b, h)),
        out_shape=jax.ShapeDtypeStruct(mixed_prev.shape, BF16),
        input_output_aliases={8: 0},
        scratch_shapes=[pltpu.VMEM((seq, HEAD_W), BF16), pltpu.VMEM((seq, HEAD_W), BF16),
                        pltpu.VMEM((seq, 2 * HEAD_W), BF16)],
        compiler_params=_params("parallel", "parallel"),
        name="latent_neighbourhood_attention",
    )(proj, proj, proj, cache_k, cache_v, bias, na_qn[None, :], na_kn[None, :], mixed_prev)


def _gqa_body(q_ref, k_ref, v_ref, ck_ref, cv_ref, rope_ref, qn_ref, kn_ref, prev_ref, o_ref,
              qs_ref, ks_ref, vs_ref, *, seq, tq):
    del prev_ref
    group = N_HEADS // GQA_KV_HEADS
    half = HEAD_W // 2
    ks_ref[0:seq, :] = _rope(_rms(k_ref[...], kn_ref[...], HEAD_W), rope_ref, half).astype(BF16)
    ks_ref[seq:, :] = ck_ref[...].astype(BF16)
    vs_ref[0:seq, :] = _with_ones(v_ref[...].astype(BF16))
    vs_ref[seq:, :] = _with_ones(cv_ref[...].astype(BF16))
    for g in range(group):
        q = _rms(q_ref[:, g * HEAD_W:(g + 1) * HEAD_W], qn_ref[...], HEAD_W) * (HEAD_W ** -0.5)
        qs_ref[g] = _rope(q, rope_ref, half).astype(BF16)
    kk = ks_ref[...]
    vv = vs_ref[...]
    for g in range(group):
        def q_step(i, carry, g=g):
            qrows = pl.ds(pl.multiple_of(i * tq, tq), tq)
            o = _softmax_pv([_dot_nt(qs_ref[g, qrows, :], kk)], [vv])
            o_ref[qrows, g * HEAD_W:(g + 1) * HEAD_W] = o.astype(o_ref.dtype)
            return carry

        lax.fori_loop(0, seq // tq, q_step, 0, unroll=2)


def _latent_gqa(proj, row_block0, n_seq, seq, cache_k, cache_v, layer, rope, gqa_qn, gqa_kn, mixed_prev):
    past = cache_k.shape[3]
    group = N_HEADS // GQA_KV_HEADS
    tq = 256
    cache = pl.BlockSpec((None, None, None, past, HEAD_W), lambda b, n: (b, layer, n, 0, 0))
    vec = pl.BlockSpec((1, HEAD_W), lambda b, n: (0, 0))
    return pl.pallas_call(
        functools.partial(_gqa_body, seq=seq, tq=tq),
        grid=(n_seq, GQA_KV_HEADS),
        in_specs=[pl.BlockSpec((seq, group * HEAD_W), lambda b, n: (row_block0 + b, COL_C_Q // group + n)),
                  pl.BlockSpec((seq, HEAD_W), lambda b, n: (row_block0 + b, COL_C_K + n)),
                  pl.BlockSpec((seq, HEAD_W), lambda b, n: (row_block0 + b, COL_C_V + n)),
                  cache, cache,
                  pl.BlockSpec((3, seq, HEAD_W), lambda b, n: (0, 0, 0)),
                  vec, vec, ANY_SPEC],
        out_specs=pl.BlockSpec((seq, group * HEAD_W), lambda b, n: (row_block0 + b, n)),
        out_shape=jax.ShapeDtypeStruct(mixed_prev.shape, BF16),
        input_output_aliases={8: 0},
        scratch_shapes=[pltpu.VMEM((group, seq, HEAD_W), BF16),
                        pltpu.VMEM((seq + past, HEAD_W), BF16),
                        pltpu.VMEM((seq + past, 2 * HEAD_W), BF16)],
        compiler_params=_params("parallel", "parallel"),
        name="latent_gqa_attention",
    )(proj, proj, proj, cache_k, cache_v, rope, gqa_qn[None, :], gqa_kn[None, :], mixed_prev)


def _diff_body(q_ref, k_ref, v_ref, ck_ref, cv_ref, rope_ref, qn_ref, kn_ref, sub_ref, lam_ref, prev_ref, o_ref,
               qs_ref, k0_ref, k1_ref, vs_ref, *, seq, tq, lam_init):
    del prev_ref
    half = DF_DQK // 2
    k = _rope(_rms_halves(k_ref[...], kn_ref[...]), rope_ref, half)
    lo = _lane_lt(k.shape, DF_DQK)
    k0_ref[0:seq, :] = jnp.where(lo, k, 0.0).astype(BF16)
    k1_ref[0:seq, :] = jnp.where(lo, 0.0, k).astype(BF16)
    ck = ck_ref[...]
    lo_c = _lane_lt(ck.shape, DF_DQK)
    k0_ref[seq:, :] = jnp.where(lo_c, ck, 0.0).astype(BF16)
    k1_ref[seq:, :] = jnp.where(lo_c, 0.0, ck).astype(BF16)
    vs_ref[0:seq, :] = _with_ones(v_ref[...].astype(BF16))
    vs_ref[seq:, :] = _with_ones(cv_ref[...].astype(BF16))
    q = _rms_halves(q_ref[...], qn_ref[...]) * (DF_DQK ** -0.5)
    qs_ref[...] = _rope(q, rope_ref, half).astype(BF16)
    lam = _lambda(lam_ref, lam_init)
    k0 = k0_ref[...]
    k1 = k1_ref[...]
    vv = vs_ref[...]

    def q_step(i, carry):
        qrows = pl.ds(pl.multiple_of(i * tq, tq), tq)
        qb = qs_ref[qrows, :]
        o = _diff_pv(_dot_nt(qb, k0), _dot_nt(qb, k1), lam, vv)
        o_ref[qrows, :] = (_rms(o, sub_ref[...], HEAD_W) * (1.0 - lam_init)).astype(o_ref.dtype)
        return carry

    lax.fori_loop(0, seq // tq, q_step, 0, unroll=2)


def _latent_diff(proj, row_block0, n_seq, seq, cache_k2, cache_v, layer, rope, df_qn, df_kn, df_subln, df_lam,
                 lam_init, mixed_prev):
    past = cache_k2.shape[3]
    tq = 256

    def col(cb):
        return pl.BlockSpec((seq, HEAD_W), lambda b, h, cb=cb: (row_block0 + b, cb + h))

    cache = pl.BlockSpec((None, None, None, past, HEAD_W), lambda b, h: (b, layer, h, 0, 0))
    vec = pl.BlockSpec((1, HEAD_W), lambda b, h: (0, 0))
    kv_scratch = pltpu.VMEM((seq + past, HEAD_W), BF16)
    return pl.pallas_call(
        functools.partial(_diff_body, seq=seq, tq=tq, lam_init=lam_init),
        grid=(n_seq, N_HEADS),
        in_specs=[col(COL_D_Q), col(COL_D_K), col(COL_D_V), cache, cache,
                  pl.BlockSpec((3, seq, HEAD_W), lambda b, h: (0, 0, 0)),
                  vec, vec, vec, pl.BlockSpec((4, DF_DQK), lambda b, h: (0, 0)), ANY_SPEC],
        out_specs=pl.BlockSpec((seq, HEAD_W), lambda b, h: (row_block0 + b, h)),
        out_shape=jax.ShapeDtypeStruct(mixed_prev.shape, BF16),
        input_output_aliases={10: 0},
        scratch_shapes=[pltpu.VMEM((seq, HEAD_W), BF16), kv_scratch, kv_scratch,
                        pltpu.VMEM((seq + past, 2 * HEAD_W), BF16)],
        compiler_params=_params("parallel", "parallel"),
        name="latent_diff_attention",
    )(proj, proj, proj, cache_k2, cache_v, rope, jnp.tile(df_qn, 2)[None, :], jnp.tile(df_kn, 2)[None, :],
      df_subln[None, :], df_lam, mixed_prev)


def _first_max(vals):
    best = vals[0]
    idx = jnp.zeros(best.shape, jnp.int32)
    for i in range(1, len(vals)):
        better = vals[i] > best
        best = jnp.where(better, vals[i], best)
        idx = jnp.where(better, i, idx)
    return best, idx


def _pick(vals, idx):
    out = vals[0]
    for i in range(1, len(vals)):
        out = jnp.where(idx == i, vals[i], out)
    return out


def _outproj_body(ma_ref, mb_ref, mc_ref, md_ref, w_ref, hc_ref, hl_ref, mod_ref, n2_ref, rw_ref, rb_ref,
                  h1_ref, x2_ref, idx_ref, gate_ref, mixed_ref, *, ctx_tiles):
    for g, m_ref in enumerate((ma_ref, mb_ref, mc_ref, md_ref)):
        mixed_ref[:, g * GROUP_W:(g + 1) * GROUP_W] = m_ref[...]
    y = mod_ref[2:3, :] * _dot(mixed_ref[...], w_ref[...])

    def residual(h_ref):
        h1_ref[...] = h_ref[...] + y

    is_ctx = pl.program_id(0) < ctx_tiles
    pl.when(is_ctx)(lambda: residual(hc_ref))
    pl.when(jnp.logical_not(is_ctx))(lambda: residual(hl_ref))
    h1 = h1_ref[...]
    x2 = _rms(h1, n2_ref[...], D_MODEL) * (1.0 + mod_ref[4:5, :]) + mod_ref[3:4, :]
    x2_ref[...] = x2
    logits = lax.dot_general(rw_ref[...], x2, (((1,), (1,)), ((), ())), precision=lax.Precision.HIGHEST,
                             preferred_element_type=F32)
    aff_all = _sigmoid(logits)
    sel_all = aff_all + rb_ref[...]
    aff = [aff_all[e:e + 1, :] for e in range(N_EXPERTS)]
    sel = [sel_all[e:e + 1, :] for e in range(N_EXPERTS)]
    neg = jnp.full(sel[0].shape, -jnp.inf, F32)
    scores = []
    for g in range(N_EXP_GROUPS):
        grp = sel[g * EXP_PER_GROUP:(g + 1) * EXP_PER_GROUP]
        m1, i1 = _first_max(grp)
        m2, _ = _first_max([jnp.where(i1 == j, neg, grp[j]) for j in range(EXP_PER_GROUP)])
        scores.append(m1 + m2)
    _, g_best = _first_max(scores)
    in_sel = [_pick([sel[g * EXP_PER_GROUP + j] for g in range(N_EXP_GROUPS)], g_best)
              for j in range(EXP_PER_GROUP)]
    in_aff = [_pick([aff[g * EXP_PER_GROUP + j] for g in range(N_EXP_GROUPS)], g_best)
              for j in range(EXP_PER_GROUP)]
    _, l1 = _first_max(in_sel)
    _, l2 = _first_max([jnp.where(l1 == j, neg, in_sel[j]) for j in range(EXP_PER_GROUP)])
    w1 = _pick(in_aff, l1)
    w2 = _pick(in_aff, l2)
    idx_ref[0:1, :] = g_best * EXP_PER_GROUP + l1
    idx_ref[1:2, :] = g_best * EXP_PER_GROUP + l2
    gate_ref[0:1, :] = w1 / (w1 + w2)
    gate_ref[1:2, :] = w2 / (w1 + w2)


def _output_projection(mixed4, w_out_bf16, h_ctx, h_lat, mod, norm2, router_w, router_b, cond_of_tile, tm):
    d = h_ctx.shape[1]
    t = h_ctx.shape[0] + h_lat.shape[0]
    ctx_tiles = h_ctx.shape[0] // tm
    slab = pl.BlockSpec((tm, GROUP_W), lambda i: (i, 0))
    return pl.pallas_call(
        functools.partial(_outproj_body, ctx_tiles=ctx_tiles),
        grid=(t // tm,),
        in_specs=[slab, slab, slab, slab,
                  pl.BlockSpec((d, d), lambda i: (0, 0))] + _group_specs(tm, d, ctx_tiles) + [
                  pl.BlockSpec((None, 6, d), lambda i: (cond_of_tile(i), 0, 0)),
                  pl.BlockSpec((1, d), lambda i: (0, 0)),
                  pl.BlockSpec((N_EXPERTS, d), lambda i: (0, 0)),
                  pl.BlockSpec((N_EXPERTS, 1), lambda i: (0, 0))],
        out_specs=[pl.BlockSpec((tm, d), lambda i: (i, 0)),
                   pl.BlockSpec((tm, d), lambda i: (i, 0)),
                   pl.BlockSpec((2, tm), lambda i: (0, i)),
                   pl.BlockSpec((2, tm), lambda i: (0, i))],
        out_shape=[jax.ShapeDtypeStruct((t, d), F32), jax.ShapeDtypeStruct((t, d), F32),
                   jax.ShapeDtypeStruct((2, t), jnp.int32), jax.ShapeDtypeStruct((2, t), F32)],
        scratch_shapes=[pltpu.VMEM((tm, d), BF16)],
        compiler_params=_params("parallel", vmem=VMEM_LIMIT_PROJ),
        name="out_proj_residual_router",
    )(*mixed4, w_out_bf16, h_ctx, h_lat, mod, norm2[None, :], router_w.T, router_b[:, None])


SUBLANES = 8


def _vmem_row(ref, base, u):
    return ref.at[pl.ds(base, SUBLANES), :].at[pl.ds(u, 1), :]


def _for_row_groups(n_rows, fn):
    def body(g, carry):
        base = pl.multiple_of(g * SUBLANES, SUBLANES)
        for u in range(SUBLANES):
            fn(base, u)
        return carry

    lax.fori_loop(0, n_rows // SUBLANES, body, 0)


def _rows_wait(src_hbm, dst, sem, n_rows):
    pltpu.make_async_copy(src_hbm.at[pl.ds(0, n_rows), :], dst.at[pl.ds(0, n_rows), :], sem).wait()


def _dispatch_body(dest_ref, ps_ref, pe_ref, x_ref, xs_hbm, zero_ref, sem, zsem, *, n_tok, tile):
    i = pl.program_id(0)

    def zero_copy(e):
        first = pl.multiple_of(pe_ref[e] - MOE_ROWS, MOE_ROWS)
        return pltpu.make_async_copy(zero_ref, xs_hbm.at[pl.ds(first, MOE_ROWS), :], zsem)

    @pl.when(i == 0)
    def _():
        zero_ref[...] = jnp.zeros(zero_ref.shape, zero_ref.dtype)
        for e in range(N_EXPERTS):
            @pl.when(pe_ref[e] > ps_ref[e])
            def _(e=e):
                zero_copy(e).start()
        for e in range(N_EXPERTS):
            @pl.when(pe_ref[e] > ps_ref[e])
            def _(e=e):
                zero_copy(e).wait()

    for k in range(2):
        def scatter_row(base, u, k=k):
            row = dest_ref[k * n_tok + i * tile + base + u]
            pltpu.make_async_copy(_vmem_row(x_ref, base, u), xs_hbm.at[pl.ds(row, 1), :], sem).start()

        _for_row_groups(tile, scatter_row)
    for k in range(2):
        _rows_wait(x_ref, xs_hbm, sem, tile)


def _dispatch(x2, dest, pad_start, pad_end, n_rows):
    t, d = x2.shape
    tile = MOE_ROWS
    grid_spec = pltpu.PrefetchScalarGridSpec(
        num_scalar_prefetch=3,
        grid=(t // tile,),
        in_specs=[pl.BlockSpec((tile, d), lambda i, dst, ps, pe: (i, 0))],
        out_specs=ANY_SPEC,
        scratch_shapes=[pltpu.VMEM((MOE_ROWS, d), F32), pltpu.SemaphoreType.DMA, pltpu.SemaphoreType.DMA],
    )
    return pl.pallas_call(
        functools.partial(_dispatch_body, n_tok=t, tile=tile),
        grid_spec=grid_spec,
        out_shape=jax.ShapeDtypeStruct((n_rows, d), F32),
        compiler_params=_params("arbitrary"),
        name="moe_dispatch",
    )(dest, pad_start, pad_end, x2)


def _expert_body(be_ref, nb_ref, x_ref, wg_ref, wu_ref, wd_ref, o_ref, wg_bf, wu_bf, wd_bf):
    i = pl.program_id(0)

    @pl.when(i < nb_ref[0])
    def _():
        @pl.when((i == 0) | (be_ref[i] != be_ref[jnp.maximum(i - 1, 0)]))
        def _():
            wg_bf[...] = wg_ref[...].astype(BF16)
            wu_bf[...] = wu_ref[...].astype(BF16)
            wd_bf[...] = wd_ref[...].astype(BF16)

        x = x_ref[...].astype(BF16)
        hdn = _silu(_dot(x, wg_bf[...])) * _dot(x, wu_bf[...])
        o_ref[...] = _dot(hdn.astype(BF16), wd_bf[...])

    @pl.when(i >= nb_ref[0])
    def _():
        o_ref[...] = jnp.zeros(o_ref.shape, o_ref.dtype)


def _expert_blocks(xs, block_expert, n_used, w_gate, w_up, w_down, layer):
    n_rows, d = xs.shape
    ff = w_gate.shape[-1]
    grid_spec = pltpu.PrefetchScalarGridSpec(
        num_scalar_prefetch=2,
        grid=(n_rows // MOE_ROWS,),
        in_specs=[pl.BlockSpec((MOE_ROWS, d), lambda i, be, nb: (jnp.minimum(i, nb[0] - 1), 0)),
                  pl.BlockSpec((None, None, d, ff), lambda i, be, nb: (layer, be[i], 0, 0)),
                  pl.BlockSpec((None, None, d, ff), lambda i, be, nb: (layer, be[i], 0, 0)),
                  pl.BlockSpec((None, None, ff, d), lambda i, be, nb: (layer, be[i], 0, 0))],
        out_specs=pl.BlockSpec((MOE_ROWS, d), lambda i, be, nb: (i, 0)),
        scratch_shapes=[pltpu.VMEM((d, ff), BF16), pltpu.VMEM((d, ff), BF16), pltpu.VMEM((ff, d), BF16)],
    )
    return pl.pallas_call(
        _expert_body,
        grid_spec=grid_spec,
        out_shape=jax.ShapeDtypeStruct((n_rows, d), F32),
        compiler_params=_params("arbitrary"),
        name="moe_expert_blocks",
    )(block_expert, n_used, xs, w_gate, w_up, w_down)


def _combine_body(dest_ref, h_ref, gate_ref, y_hbm, mod_ref, *rest, n_tok, tile, ctx_tiles):
    out_refs, (ybuf, sem) = rest[:-2], rest[-2:]
    i = pl.program_id(0)
    n_tiles = pl.num_programs(0)

    def start(blk, slot):
        for k in range(2):
            def gather_row(base, u, k=k):
                row = dest_ref[k * n_tok + blk * tile + base + u]
                pltpu.make_async_copy(y_hbm.at[pl.ds(row, 1), :], _vmem_row(ybuf.at[slot, k], base, u),
                                      sem.at[slot]).start()

            _for_row_groups(tile, gather_row)

    @pl.when(i == 0)
    def _():
        start(0, 0)

    @pl.when(i + 1 < n_tiles)
    def _():
        start(i + 1, (i + 1) % 2)

    slot = i % 2
    for k in range(2):
        _rows_wait(y_hbm, ybuf.at[slot, k], sem.at[slot], tile)
    gate = gate_ref[...]
    out = h_ref[...] + mod_ref[5:6, :] * (gate[:, 0:1] * ybuf[slot, 0] + gate[:, 1:2] * ybuf[slot, 1])
    @pl.when(i < ctx_tiles)
    def _():
        out_refs[0][...] = out

    @pl.when(i >= ctx_tiles)
    def _():
        out_refs[1][...] = out


def _combine(h1, yb, dest, gates, mod, cond_of_tile, tile, split_rows):
    t, d = h1.shape
    row_tile = pl.BlockSpec((tile, d), lambda i, dst: (i, 0))
    ctx_tiles = split_rows // tile
    out_specs = _group_specs(tile, d, ctx_tiles)
    out_shape = [jax.ShapeDtypeStruct((split_rows, d), F32), jax.ShapeDtypeStruct((t - split_rows, d), F32)]
    grid_spec = pltpu.PrefetchScalarGridSpec(
        num_scalar_prefetch=1,
        grid=(t // tile,),
        in_specs=[row_tile,
                  pl.BlockSpec((tile, 2), lambda i, dst: (i, 0)),
                  ANY_SPEC,
                  pl.BlockSpec((None, 6, d), lambda i, dst: (cond_of_tile(i), 0, 0))],
        out_specs=out_specs,
        scratch_shapes=[pltpu.VMEM((2, 2, tile, d), F32), pltpu.SemaphoreType.DMA((2,))],
    )
    return pl.pallas_call(
        functools.partial(_combine_body, n_tok=t, tile=tile, ctx_tiles=ctx_tiles),
        grid_spec=grid_spec,
        out_shape=out_shape,
        compiler_params=_params("arbitrary"),
        name="moe_gated_residual",
    )(dest, h1, gates, yb, mod)


def _moe(h1, x2, idx_t, gate_t, mod, w_gate, w_up, w_down, layer, cond_of_tile, split_rows):
    t, d = h1.shape
    n = 2 * t
    experts = idx_t.reshape(n)
    onehot = (experts[:, None] == jnp.arange(N_EXPERTS, dtype=jnp.int32)[None, :]).astype(BF16)
    blocks = onehot.reshape(n // MOE_ROWS, MOE_ROWS, N_EXPERTS)
    tri = jnp.asarray(np.tril(np.ones((MOE_ROWS, MOE_ROWS), np.float32)), BF16)
    within = jnp.einsum('ij,bjk->bik', tri, blocks, preferred_element_type=F32)
    block_total = within[:, -1, :]
    block_first = jnp.cumsum(block_total, axis=0) - block_total
    counts = (block_first[-1] + block_total[-1]).astype(jnp.int32)
    before = (within + block_first[:, None, :]).reshape(n, N_EXPERTS) - 1.0
    rank = jnp.sum(before * onehot.astype(F32), axis=1).astype(jnp.int32)
    padded = (counts + MOE_ROWS - 1) // MOE_ROWS * MOE_ROWS
    pad_end = jnp.cumsum(padded).astype(jnp.int32)
    pad_start = pad_end - padded
    dest = (pad_start[experts] + rank).astype(jnp.int32)
    n_blocks = (n + N_EXPERTS * (MOE_ROWS - 1) + MOE_ROWS - 1) // MOE_ROWS
    block_first_row = jnp.arange(n_blocks, dtype=jnp.int32) * MOE_ROWS
    block_expert = jnp.minimum(jnp.sum((pad_end[None, :] <= block_first_row[:, None]).astype(jnp.int32), axis=1),
                               N_EXPERTS - 1)
    n_used = (pad_end[-1:] // MOE_ROWS).astype(jnp.int32)
    xs = _dispatch(x2, dest, pad_start, pad_end, n_blocks * MOE_ROWS)
    yb = _expert_blocks(xs, block_expert, n_used, w_gate, w_up, w_down, layer)
    return _combine(h1, yb, dest, gate_t.T, mod, cond_of_tile, MOE_ROWS, split_rows)


def kernel(x_prompt, x_sample, cache_na_k, cache_na_v, cache_gqa_k, cache_gqa_v, cache_diff_k, cache_diff_v, state_hgrn, c, c_ctx, w_mod, b_mod, norm1, norm2, w_in, w_out, hg_lb_logits, hg_onorm, na_qn, na_kn, na_rpb, gqa_qn, gqa_kn, df_qn, df_kn, df_lam, df_subln, router_w, router_b, w_gate, w_up, w_down):
    n_ctx, ctx_len, d = x_prompt.shape
    n_lat, lat_len, _ = x_sample.shape
    depth = w_in.shape[0]
    t_ctx = n_ctx * ctx_len
    assert t_ctx % lat_len == 0 and lat_len % GRID_W == 0 and lat_len // GRID_W >= WIN_ROWS
    tm = next(m for m in (1024, 512, 256) if t_ctx % m == 0 and lat_len % m == 0)
    tm2 = min(tm, 512)
    lat_block0 = t_ctx // lat_len

    def cond_tile(tile_rows):
        def cond_of_tile(i):
            return jnp.where(i < t_ctx // tile_rows, 0, 1 + (i - t_ctx // tile_rows) // (lat_len // tile_rows))
        return cond_of_tile

    sm = jax.nn.softmax(hg_lb_logits.astype(F32), axis=0)
    lower = jnp.cumsum(sm, axis=0) - sm[0:1]
    mod_all = _modulation(jnp.concatenate([c_ctx[None, :], c], axis=0), w_mod, b_mod)
    mod_all = mod_all.reshape(depth, 1 + n_lat, 6, d)
    hgrn_consts = _hgrn_constants(HGRN_CHUNK)
    rope_c = _rope_tables(lat_len, HEAD_W)
    rope_d = _rope_tables(lat_len, DF_DQK)
    past = cache_diff_k.shape[4]
    cache_diff_k2 = cache_diff_k.transpose(0, 1, 2, 4, 3, 5).reshape(n_lat, depth, N_HEADS, past, HEAD_W)

    h_ctx, h_lat = x_prompt.reshape(t_ctx, d), x_sample.reshape(n_lat * lat_len, d)
    caches, states = [], None
    for layer in range(depth):
        mod = mod_all[layer]
        lam_init = 0.8 - 0.6 * math.exp(-0.3 * layer)
        proj = _input_projection(h_ctx, h_lat, mod, norm1[layer], w_in[layer].astype(BF16), cond_tile(tm), tm)
        mix_a, states = _hgrn(proj, 0, n_ctx, ctx_len, lower[layer], hg_onorm[layer], hgrn_consts, None, None,
                              states, layer, depth)
        gains = (na_qn[layer], na_kn[layer], gqa_qn[layer], gqa_kn[layer], df_qn[layer], df_kn[layer],
                 df_subln[layer], df_lam[layer])
        mix_b, mix_c, mix_d, *caches = _context_attention(proj, n_ctx, ctx_len, gains, lam_init, layer, depth,
                                                          caches)
        mix_a, _ = _hgrn(proj, lat_block0, n_lat, lat_len, lower[layer], hg_onorm[layer], hgrn_consts,
                         state_hgrn, mix_a, None, layer, depth)
        mix_b = _latent_na(proj, lat_block0, n_lat, lat_len, cache_na_k, cache_na_v, layer,
                           _na_bias(na_rpb[layer]), na_qn[layer], na_kn[layer], mix_b)
        mix_c = _latent_gqa(proj, lat_block0, n_lat, lat_len, cache_gqa_k, cache_gqa_v, layer, rope_c,
                            gqa_qn[layer], gqa_kn[layer], mix_c)
        mix_d = _latent_diff(proj, lat_block0, n_lat, lat_len, cache_diff_k2, cache_diff_v, layer, rope_d,
                             df_qn[layer], df_kn[layer], df_subln[layer], df_lam[layer], lam_init, mix_d)
        h1, x2, idx_t, gate_t = _output_projection((mix_a, mix_b, mix_c, mix_d), w_out[layer].astype(BF16), h_ctx,
                                                   h_lat, mod, norm2[layer], router_w, router_b, cond_tile(tm2), tm2)
        h_ctx, h_lat = _moe(h1, x2, idx_t, gate_t, mod, w_gate, w_up, w_down, layer, cond_tile(MOE_ROWS), t_ctx)
    y_prompt = h_ctx.reshape(n_ctx, ctx_len, d)
    y_sample = h_lat.reshape(n_lat, lat_len, d)
    return (y_prompt, y_sample, *caches, states)
```

```python
import functools
import math

import numpy as np
import jax
import jax.numpy as jnp
from jax import lax
from jax.experimental import pallas as pl
from jax.experimental.pallas import tpu as pltpu

D_MODEL = 2048
GRID_W = 64
GROUP_W = D_MODEL // 4
N_HEADS = 4
HEAD_W = GROUP_W // N_HEADS
GQA_KV_HEADS = 2
DF_DQK = HEAD_W // 2
WIN_ROWS = 8
WIN_COLS = 16
N_EXPERTS = 16
N_EXP_GROUPS = 4
EXP_PER_GROUP = N_EXPERTS // N_EXP_GROUPS
EXPERT_FF = D_MODEL // 4
ROPE_THETA = 10000.0
EPS = 1e-6
NEG_INF = -1e30
IN_WIDTH = 13 * GROUP_W

COL_A_Q, COL_A_FF, COL_A_FB, COL_A_I, COL_A_G = 0, 4, 8, 12, 16
COL_B_Q, COL_B_K, COL_B_V = 20, 24, 28
COL_C_Q, COL_C_K, COL_C_V = 32, 36, 38
COL_D_Q, COL_D_K, COL_D_V = 40, 44, 48

HGRN_CHUNK = 128
HGRN_UNROLL = 2
HGRN_INPUT_VMEM = 24 * 1024 * 1024
MOE_ROWS = 256
TOKEN_TILE = 256
VMEM_LIMIT = 48 * 1024 * 1024
VMEM_LIMIT_PROJ = 56 * 1024 * 1024

F32 = jnp.float32
BF16 = jnp.bfloat16
ANY_SPEC = pl.BlockSpec(memory_space=pl.ANY)


def _params(*sem, vmem=VMEM_LIMIT):
    return pltpu.CompilerParams(dimension_semantics=sem, vmem_limit_bytes=vmem)


def _sigmoid(x):
    return 1.0 / (1.0 + jnp.exp(-x))


def _silu(x):
    return x * _sigmoid(x)


def _rms(x, gain, n):
    return x * lax.rsqrt(jnp.sum(x * x, axis=-1, keepdims=True) * (1.0 / n) + EPS) * gain


def _dot(a, b):
    return jnp.dot(a, b, preferred_element_type=F32)


def _dot_nt(a, b):
    return lax.dot_general(a, b, (((1,), (1,)), ((), ())), preferred_element_type=F32)


def _dot_tn(a, b):
    return lax.dot_general(a, b, (((0,), (0,)), ((), ())), preferred_element_type=F32)


def _pack_bf16_pairs(x):
    k = x.shape[1] // 2
    lo = lax.bitcast_convert_type(x[:, :k].astype(BF16).astype(F32), jnp.uint32) >> 16
    hi = lax.bitcast_convert_type(x[:, k:].astype(BF16).astype(F32), jnp.uint32)
    return hi | lo


def _unpack_bf16_pairs(w):
    lo = lax.bitcast_convert_type(w << 16, F32)
    hi = lax.bitcast_convert_type(w & jnp.uint32(0xFFFF0000), F32)
    return jnp.concatenate([lo, hi], axis=1)


def _aligned(x, m):
    return x if isinstance(x, int) else pl.multiple_of(x, m)


def _alias_kwargs(n_inputs, prev, first_out):
    return ([ANY_SPEC] * len(prev), list(prev), {n_inputs + k: first_out + k for k in range(len(prev))})


def _mod_body(cond_ref, w_ref, b_ref, o_ref):
    w = w_ref[...]
    for c in range(cond_ref.shape[0]):
        s = _silu(cond_ref[c])
        o_ref[c:c + 1, :] = jnp.sum(w * s, axis=0, keepdims=True) + b_ref[...]


def _modulation(cond, w_mod, b_mod):
    depth, d, n6 = w_mod.shape
    nc = cond.shape[0]
    tn = 1024
    return pl.pallas_call(
        _mod_body,
        grid=(depth, n6 // tn),
        in_specs=[pl.BlockSpec((nc, d, 1), lambda l, j: (0, 0, 0)),
                  pl.BlockSpec((None, d, tn), lambda l, j: (l, 0, j)),
                  pl.BlockSpec((None, 1, tn), lambda l, j: (l, 0, j))],
        out_specs=pl.BlockSpec((None, nc, tn), lambda l, j: (l, 0, j)),
        out_shape=jax.ShapeDtypeStruct((depth, nc, n6), F32),
        compiler_params=_params("parallel", "parallel"),
        name="adaln_modulation",
    )(cond[:, :, None], w_mod, b_mod[:, None, :])


def _group_specs(tile, d, ctx_tiles):
    return [pl.BlockSpec((tile, d), lambda i, *_: (jnp.minimum(i, ctx_tiles - 1), 0)),
            pl.BlockSpec((tile, d), lambda i, *_: (jnp.maximum(i - ctx_tiles, 0), 0))]


def _inproj_body(hc_ref, hl_ref, mod_ref, n1_ref, w_ref, o_ref, xn_ref, *, ctx_tiles):
    def normalise(h_ref):
        y = _rms(h_ref[...], n1_ref[...], D_MODEL)
        xn_ref[...] = (y * (1.0 + mod_ref[1:2, :]) + mod_ref[0:1, :]).astype(BF16)

    first = pl.program_id(1) == 0
    is_ctx = pl.program_id(0) < ctx_tiles
    pl.when(first & is_ctx)(lambda: normalise(hc_ref))
    pl.when(first & jnp.logical_not(is_ctx))(lambda: normalise(hl_ref))
    o_ref[...] = _dot(xn_ref[...], w_ref[...])


def _input_projection(h_ctx, h_lat, mod, norm1, w_in_bf16, cond_of_tile, tm):
    d = h_ctx.shape[1]
    t = h_ctx.shape[0] + h_lat.shape[0]
    ctx_tiles = h_ctx.shape[0] // tm
    n = w_in_bf16.shape[1]
    tn = 512
    return pl.pallas_call(
        functools.partial(_inproj_body, ctx_tiles=ctx_tiles),
        grid=(t // tm, n // tn),
        in_specs=_group_specs(tm, d, ctx_tiles) + [
                  pl.BlockSpec((None, 6, d), lambda i, j: (cond_of_tile(i), 0, 0)),
                  pl.BlockSpec((1, d), lambda i, j: (0, 0)),
                  pl.BlockSpec((d, tn), lambda i, j: (0, j))],
        out_specs=pl.BlockSpec((tm, tn), lambda i, j: (i, j)),
        out_shape=jax.ShapeDtypeStruct((t, n), F32),
        scratch_shapes=[pltpu.VMEM((tm, d), BF16)],
        compiler_params=_params("parallel", "arbitrary", vmem=VMEM_LIMIT_PROJ),
        name="norm_modulate_in_proj",
    )(h_ctx, h_lat, mod, norm1[None, :], w_in_bf16)


def _hgrn_constants(c):
    nl = int(math.log2(c))
    idx = np.arange(c)
    e = np.zeros((nl + 2, c, c), np.float32)
    m = np.zeros((nl + 1, c, c), np.float32)
    e[0] = idx[None, :] <= idx[:, None]
    e[1] = idx[None, :] > idx[:, None]
    m[0] = np.eye(c)
    for li in range(nl):
        s = c >> (li + 1)
        parent = idx // (2 * s)
        right = (idx % (2 * s)) >= s
        ref = parent * 2 * s + s - 1
        for i in range(c):
            if right[i]:
                e[2 + li, i, ref[i] + 1:i + 1] = 1.0
            else:
                e[2 + li, i, i + 1:ref[i] + 1] = 1.0
        m[1 + li] = right[:, None] & ~right[None, :] & (parent[:, None] == parent[None, :])
    e2 = np.stack([e, e[:, ::-1, ::-1]]).reshape(2, (nl + 2) * c, c)
    m2 = np.stack([m, m[:, ::-1, ::-1]])
    return jnp.asarray(e2, BF16), jnp.asarray(m2, F32)


def _hgrn_body(*refs, seq, chunk, unroll, heads, has_s0, emit_state, n_alias):
    q_ref, ff_ref, fb_ref, i_ref, g_ref, lb_ref, on_ref, e_ref, m_ref = refs[:9]
    pos = 9
    s0_ref = None
    if has_s0:
        s0_ref = refs[pos]
        pos += 1
    pos += n_alias
    o_ref = refs[pos]
    pos += 1
    if emit_state:
        st_ref = refs[pos]
        pos += 1
    of_ref, ob_ref = refs[pos], refs[pos + 1]
    c = chunk
    n_chunks = seq // c
    assert seq % c == 0 and n_chunks % unroll == 0
    n_levels = m_ref.shape[1] - 1
    gate_refs = (ff_ref, fb_ref)
    out_refs = (of_ref, ob_ref)

    def chunk_step(c0, d, hh, st):
        rows = pl.ds(c0, c)
        lanes = slice(hh * HEAD_W, (hh + 1) * HEAD_W)
        lb = lb_ref[d:d + 1, lanes]
        f = lb + (1.0 - lb) * _sigmoid(gate_refs[d][rows, lanes])
        g = jnp.log(f)
        k = 1.0 - f
        q = _silu(q_ref[rows, lanes])
        v = i_ref[rows, lanes].astype(BF16)
        g_hi = g.astype(BF16)
        g_lo = (g - g_hi.astype(F32)).astype(BF16)
        g2 = _dot(e_ref[d], jnp.concatenate([g_hi, g_lo], axis=1))
        x = jnp.exp(g2[:, 0:HEAD_W] + g2[:, HEAD_W:2 * HEAD_W])
        x_cum = x[0:c]
        x_tail = x[c:2 * c]
        s = m_ref[d, 0] * _dot_nt(q.astype(BF16), k.astype(BF16))
        for lv in range(n_levels):
            x_l = x[(2 + lv) * c:(3 + lv) * c]
            s = s + m_ref[d, 1 + lv] * _dot_nt((q * x_l).astype(BF16), (k * x_l).astype(BF16))
        o = _dot_nt((q * x_cum).astype(BF16), st.astype(BF16)) + _dot(s.astype(BF16), v)
        out_refs[d][rows, lanes] = o
        total = x_cum[c - 1:c, :] if d == 0 else x_cum[0:1, :]
        return st * total + _dot_tn(v, (k * x_tail).astype(BF16))

    if has_s0:
        states0 = tuple(s0_ref[d, hh].T for hh in range(heads) for d in range(2))
    else:
        states0 = tuple(jnp.zeros((HEAD_W, HEAD_W), F32) for _ in range(2 * heads))

    def loop(t, states):
        states = list(states)
        for u in range(unroll):
            j = t * unroll + u
            for hh in range(heads):
                states[2 * hh] = chunk_step(_aligned(j * c, c), 0, hh, states[2 * hh])
                states[2 * hh + 1] = chunk_step(_aligned((n_chunks - 1 - j) * c, c), 1, hh, states[2 * hh + 1])
        return tuple(states)

    if n_chunks == unroll:
        states = loop(0, states0)
    else:
        states = lax.fori_loop(0, n_chunks // unroll, loop, states0)
    for hh in range(heads):
        lanes = slice(hh * HEAD_W, (hh + 1) * HEAD_W)
        o = of_ref[:, lanes] + ob_ref[:, lanes]
        o_ref[:, lanes] = (_rms(o, on_ref[...], HEAD_W) * _silu(g_ref[:, lanes])).astype(o_ref.dtype)
        if emit_state:
            st_ref[0, hh] = states[2 * hh].T
            st_ref[1, hh] = states[2 * hh + 1].T


def _hgrn(proj, row_block0, n_seq, seq, lower, onorm, consts, s0, mixed_prev, state_prev, layer, depth):
    e_mat, masks = consts
    latent = s0 is not None
    hps = next(n for n in (4, 2, 1) if 5 * 2 * seq * n * HEAD_W * 4 <= HGRN_INPUT_VMEM)
    width = hps * HEAD_W

    def col(cb):
        return pl.BlockSpec((seq, width), lambda b, h, cb=cb: (row_block0 + b, cb // hps + h))

    state_spec = pl.BlockSpec((None, None, 2, hps, HEAD_W, HEAD_W), lambda b, h: (b, layer, 0, h, 0, 0))
    in_specs = [col(COL_A_Q), col(COL_A_FF), col(COL_A_FB), col(COL_A_I), col(COL_A_G),
                pl.BlockSpec((2, width), lambda b, h: (0, h)),
                pl.BlockSpec((1, HEAD_W), lambda b, h: (0, 0)),
                pl.BlockSpec(e_mat.shape, lambda b, h: (0, 0, 0)),
                pl.BlockSpec(masks.shape, lambda b, h: (0, 0, 0, 0))]
    args = [proj, proj, proj, proj, proj, lower, onorm[None, :], e_mat, masks]
    if latent:
        in_specs.append(state_spec)
        args.append(s0)
        prev = [mixed_prev]
    else:
        prev = [] if state_prev is None else [state_prev]
    alias_specs, alias_args, aliases = _alias_kwargs(len(args), prev, 0 if latent else 1)
    out_specs = [pl.BlockSpec((seq, width), lambda b, h: (row_block0 + b, h))]
    out_shape = [jax.ShapeDtypeStruct((proj.shape[0], GROUP_W), BF16)]
    if not latent:
        out_specs.append(state_spec)
        out_shape.append(jax.ShapeDtypeStruct((n_seq, depth, 2, N_HEADS, HEAD_W, HEAD_W), F32))
    res = pl.pallas_call(
        functools.partial(_hgrn_body, seq=seq, chunk=HGRN_CHUNK, unroll=HGRN_UNROLL, heads=hps, has_s0=latent,
                          emit_state=not latent, n_alias=len(prev)),
        grid=(n_seq, N_HEADS // hps),
        in_specs=in_specs + alias_specs, out_specs=out_specs, out_shape=out_shape,
        input_output_aliases=aliases,
        scratch_shapes=[pltpu.VMEM((seq, width), F32), pltpu.VMEM((seq, width), F32)],
        compiler_params=_params("parallel", "parallel"),
        name="hgrn2_latent" if latent else "hgrn2_context",
    )(*args, *alias_args)
    return (res[0], None) if latent else res


def _with_ones(v):
    return jnp.concatenate([v, jnp.ones_like(v)], axis=1)


def _softmax_pv(scores, values1):
    mx = functools.reduce(jnp.maximum, [jnp.max(s, axis=-1, keepdims=True) for s in scores])
    acc = functools.reduce(lambda a, b: a + b,
                           [_dot(jnp.exp(s - mx).astype(BF16), v) for s, v in zip(scores, values1)])
    return acc[:, 0:HEAD_W] / acc[:, HEAD_W:HEAD_W + 1]


def _diff_pv(s0, s1, lam, values1):
    return _softmax_pv([s0], [values1]) - lam * _softmax_pv([s1], [values1])


def _lane_lt(shape, n):
    return lax.broadcasted_iota(jnp.int32, shape, len(shape) - 1) < n


def _rms_halves(x, gain2):
    lo = _lane_lt(x.shape, DF_DQK)
    sq = x * x
    ss_lo = jnp.sum(jnp.where(lo, sq, 0.0), axis=-1, keepdims=True)
    ss_hi = jnp.sum(sq, axis=-1, keepdims=True) - ss_lo
    inv = jnp.where(lo, lax.rsqrt(ss_lo * (1.0 / DF_DQK) + EPS), lax.rsqrt(ss_hi * (1.0 / DF_DQK) + EPS))
    return x * inv * gain2


def _lambda(lam_ref, lam_init):
    l = lam_ref[...]
    return (jnp.exp(jnp.sum(l[0:1] * l[1:2], axis=-1, keepdims=True))
            - jnp.exp(jnp.sum(l[2:3] * l[3:4], axis=-1, keepdims=True)) + lam_init)


N_CTX_ATTN_INPUTS = 17


def _ctx_attn_body(*refs, lam_init):
    (bq_ref, bk_ref, bv_ref, cq_ref, ck_ref, cv_ref, dq_ref, dk_ref, dv_ref,
     naq_ref, nak_ref, gq_ref, gk_ref, dfq_ref, dfk_ref, sub_ref, lam_ref) = refs[:N_CTX_ATTN_INPUTS]
    ob_ref, oc_ref, od_ref, kb_ref, vb_ref, kc_ref, vc_ref, kd_ref, vd_ref = refs[-9:]
    group = N_HEADS // GQA_KV_HEADS
    scale = HEAD_W ** -0.5
    lam = _lambda(lam_ref, lam_init)
    kc = _rms(ck_ref[...], gk_ref[...], HEAD_W)
    vc = cv_ref[...]
    kc_ref[...] = kc
    vc_ref[...] = vc
    kc16 = kc.astype(BF16)
    vc1 = _with_ones(vc.astype(BF16))
    for g in range(group):
        lanes = slice(g * HEAD_W, (g + 1) * HEAD_W)
        qc = (_rms(cq_ref[:, lanes], gq_ref[...], HEAD_W) * scale).astype(BF16)
        oc_ref[:, lanes] = _softmax_pv([_dot_nt(qc, kc16)], [vc1]).astype(oc_ref.dtype)
        kb = _rms(bk_ref[:, lanes], nak_ref[...], HEAD_W)
        vb = bv_ref[:, lanes]
        kb_ref[g] = kb
        vb_ref[g] = vb
        qb = (_rms(bq_ref[:, lanes], naq_ref[...], HEAD_W) * scale).astype(BF16)
        ob_ref[:, lanes] = _softmax_pv([_dot_nt(qb, kb.astype(BF16))],
                                       [_with_ones(vb.astype(BF16))]).astype(ob_ref.dtype)
        kd = _rms_halves(dk_ref[:, lanes], dfk_ref[...])
        vd = dv_ref[:, lanes]
        kd_ref[g, 0] = kd[:, 0:DF_DQK]
        kd_ref[g, 1] = kd[:, DF_DQK:2 * DF_DQK]
        vd_ref[g] = vd
        qd = (_rms_halves(dq_ref[:, lanes], dfq_ref[...]) * (DF_DQK ** -0.5)).astype(BF16)
        lo = _lane_lt(kd.shape, DF_DQK)
        s0 = _dot_nt(qd, jnp.where(lo, kd, 0.0).astype(BF16))
        s1 = _dot_nt(qd, jnp.where(lo, 0.0, kd).astype(BF16))
        od = _diff_pv(s0, s1, lam, _with_ones(vd.astype(BF16)))
        od_ref[:, lanes] = (_rms(od, sub_ref[...], HEAD_W) * (1.0 - lam_init)).astype(od_ref.dtype)


def _context_attention(proj, n_seq, seq, gains, lam_init, layer, depth, caches_prev):
    na_qn, na_kn, gqa_qn, gqa_kn, df_qn, df_kn, df_subln, df_lam = gains
    group = N_HEADS // GQA_KV_HEADS
    width = group * HEAD_W

    def heads(cb):
        return pl.BlockSpec((seq, width), lambda b, n, cb=cb: (b, cb // group + n))

    def kv_head(cb):
        return pl.BlockSpec((seq, HEAD_W), lambda b, n, cb=cb: (b, cb + n))

    vec = pl.BlockSpec((1, HEAD_W), lambda b, n: (0, 0))
    cache_heads = pl.BlockSpec((None, None, group, seq, HEAD_W), lambda b, n: (b, layer, n, 0, 0))
    cache_kv = pl.BlockSpec((None, None, None, seq, HEAD_W), lambda b, n: (b, layer, n, 0, 0))
    mixed = pl.BlockSpec((seq, width), lambda b, n: (b, n))
    mixed_shape = jax.ShapeDtypeStruct((proj.shape[0], GROUP_W), BF16)
    cache4 = jax.ShapeDtypeStruct((n_seq, depth, N_HEADS, seq, HEAD_W), F32)
    cache2 = jax.ShapeDtypeStruct((n_seq, depth, GQA_KV_HEADS, seq, HEAD_W), F32)
    cache_dk = jax.ShapeDtypeStruct((n_seq, depth, N_HEADS, 2, seq, DF_DQK), F32)
    args = [proj] * 9 + [na_qn[None, :], na_kn[None, :], gqa_qn[None, :], gqa_kn[None, :],
                         jnp.tile(df_qn, 2)[None, :], jnp.tile(df_kn, 2)[None, :], df_subln[None, :], df_lam]
    assert len(args) == N_CTX_ATTN_INPUTS
    alias_specs, alias_args, aliases = _alias_kwargs(len(args), caches_prev, 3)
    return pl.pallas_call(
        functools.partial(_ctx_attn_body, lam_init=lam_init),
        grid=(n_seq, GQA_KV_HEADS),
        in_specs=[heads(COL_B_Q), heads(COL_B_K), heads(COL_B_V),
                  heads(COL_C_Q), kv_head(COL_C_K), kv_head(COL_C_V),
                  heads(COL_D_Q), heads(COL_D_K), heads(COL_D_V),
                  vec, vec, vec, vec, vec, vec, vec,
                  pl.BlockSpec((4, DF_DQK), lambda b, n: (0, 0))] + alias_specs,
        out_specs=[mixed, mixed, mixed, cache_heads, cache_heads, cache_kv, cache_kv,
                   pl.BlockSpec((None, None, group, 2, seq, DF_DQK), lambda b, n: (b, layer, n, 0, 0, 0)),
                   cache_heads],
        out_shape=[mixed_shape, mixed_shape, mixed_shape, cache4, cache4, cache2, cache2, cache_dk, cache4],
        input_output_aliases=aliases,
        compiler_params=_params("parallel", "parallel"),
        name="context_attention",
    )(*args, *alias_args)


def _rope_tables(n_tokens, rot_dim):
    t = np.arange(n_tokens)
    row = (t // GRID_W).astype(np.float32)
    col = (t % GRID_W).astype(np.float32)
    n_freq = rot_dim // 4
    inv = (np.float32(ROPE_THETA) ** (-np.arange(n_freq, dtype=np.float32) / np.float32(n_freq))).astype(np.float32)
    ang = np.concatenate([row[:, None] * inv, col[:, None] * inv], axis=-1).astype(np.float32)
    cos, sin, zero = np.cos(ang), np.sin(ang), np.zeros_like(ang)
    reps = HEAD_W // rot_dim
    a = np.tile(np.concatenate([cos, cos], axis=-1), (1, reps))
    b = np.tile(np.concatenate([-sin, zero], axis=-1), (1, reps))
    c = np.tile(np.concatenate([zero, sin], axis=-1), (1, reps))
    return jnp.asarray(np.stack([a, b, c]), F32)


def _rope(x, tab_ref, half):
    return (x * tab_ref[0] + pltpu.roll(x, HEAD_W - half, 1) * tab_ref[1]
            + pltpu.roll(x, half, 1) * tab_ref[2])


def _na_body(q_ref, k_ref, v_ref, ck_ref, cv_ref, bias_ref, qn_ref, kn_ref, prev_ref, o_ref,
             qs_ref, ks_ref, vs_ref, *, seq):
    del prev_ref
    rows = seq // GRID_W
    n_win = WIN_ROWS * GRID_W
    qs_ref[...] = (_rms(q_ref[...], qn_ref[...], HEAD_W) * (HEAD_W ** -0.5)).astype(BF16)
    ks_ref[...] = _rms(k_ref[...], kn_ref[...], HEAD_W).astype(BF16)
    vs_ref[...] = _with_ones(v_ref[...].astype(BF16))
    ck = ck_ref[...].astype(BF16)
    cv = _with_ones(cv_ref[...].astype(BF16))

    def row_step(r, carry):
        start = jnp.clip(r - WIN_ROWS // 2, 0, rows - WIN_ROWS)
        win = pl.ds(pl.multiple_of(start * GRID_W, GRID_W), n_win)
        qrows = pl.ds(pl.multiple_of(r * GRID_W, GRID_W), GRID_W)
        q = qs_ref[qrows, :]
        s_win = _dot_nt(q, ks_ref[win, :]) + bias_ref[start - r + (WIN_ROWS - 1)]
        s_ctx = _dot_nt(q, ck)
        o_ref[qrows, :] = _softmax_pv([s_win, s_ctx], [vs_ref[win, :], cv]).astype(o_ref.dtype)
        return carry

    lax.fori_loop(0, rows, row_step, 0, unroll=2)


def _na_bias(rpb):
    col = np.arange(GRID_W)
    col_start = np.clip(col - WIN_COLS // 2, 0, GRID_W - WIN_COLS)
    col_ok = (col[None, :] >= col_start[:, None]) & (col[None, :] < col_start[:, None] + WIN_COLS)
    dc = np.clip(col[None, :] - col[:, None] + WIN_COLS - 1, 0, 2 * WIN_COLS - 2).reshape(-1)
    onehot = (np.arange(2 * WIN_COLS - 1)[:, None] == dc[None, :]).astype(np.float32)
    per_dr = jnp.einsum('hdc,cn->hdn', rpb.astype(F32), jnp.asarray(onehot), precision=lax.Precision.HIGHEST)
    per_dr = jnp.where(col_ok[None, None], per_dr.reshape(rpb.shape[0], -1, GRID_W, GRID_W), NEG_INF)
    wins = jnp.stack([per_dr[:, o:o + WIN_ROWS] for o in range(WIN_ROWS)], axis=1)
    return wins.transpose(0, 1, 3, 2, 4).reshape(rpb.shape[0], WIN_ROWS, GRID_W, WIN_ROWS * GRID_W)


def _latent_na(proj, row_block0, n_seq, seq, cache_k, cache_v, layer, bias, na_qn, na_kn, mixed_prev):
    past = cache_k.shape[3]

    def col(cb):
        return pl.BlockSpec((seq, HEAD_W), lambda b, h, cb=cb: (row_block0 + b, cb + h))

    cache = pl.BlockSpec((None, None, None, past, HEAD_W), lambda b, h: (b, layer, h, 0, 0))
    vec = pl.BlockSpec((1, HEAD_W), lambda b, h: (0, 0))
    return pl.pallas_call(
        functools.partial(_na_body, seq=seq),
        grid=(n_seq, N_HEADS),
        in_specs=[col(COL_B_Q), col(COL_B_K), col(COL_B_V), cache, cache,
                  pl.BlockSpec((None, WIN_ROWS, GRID_W, WIN_ROWS * GRID_W), lambda b, h: (h, 0, 0, 0)),
                  vec, vec, ANY_SPEC],
        out_specs=pl.BlockSpec((seq, HEAD_W), lambda b, h: (row_block0 + b, h)),
        out_shape=jax.ShapeDtypeStruct(mixed_prev.shape, BF16),
        input_output_aliases={8: 0},
        scratch_shapes=[pltpu.VMEM((seq, HEAD_W), BF16), pltpu.VMEM((seq, HEAD_W), BF16),
                        pltpu.VMEM((seq, 2 * HEAD_W), BF16)],
        compiler_params=_params("parallel", "parallel"),
        name="latent_neighbourhood_attention",
    )(proj, proj, proj, cache_k, cache_v, bias, na_qn[None, :], na_kn[None, :], mixed_prev)


def _gqa_body(q_ref, k_ref, v_ref, ck_ref, cv_ref, rope_ref, qn_ref, kn_ref, prev_ref, o_ref,
              qs_ref, ks_ref, vs_ref, *, seq, tq):
    del prev_ref
    group = N_HEADS // GQA_KV_HEADS
    half = HEAD_W // 2
    ks_ref[0:seq, :] = _rope(_rms(k_ref[...], kn_ref[...], HEAD_W), rope_ref, half).astype(BF16)
    ks_ref[seq:, :] = ck_ref[...].astype(BF16)
    vs_ref[0:seq, :] = _with_ones(v_ref[...].astype(BF16))
    vs_ref[seq:, :] = _with_ones(cv_ref[...].astype(BF16))
    for g in range(group):
        q = _rms(q_ref[:, g * HEAD_W:(g + 1) * HEAD_W], qn_ref[...], HEAD_W) * (HEAD_W ** -0.5)
        qs_ref[g] = _rope(q, rope_ref, half).astype(BF16)
    kk = ks_ref[...]
    vv = vs_ref[...]
    for g in range(group):
        def q_step(i, carry, g=g):
            qrows = pl.ds(pl.multiple_of(i * tq, tq), tq)
            o = _softmax_pv([_dot_nt(qs_ref[g, qrows, :], kk)], [vv])
            o_ref[qrows, g * HEAD_W:(g + 1) * HEAD_W] = o.astype(o_ref.dtype)
            return carry

        lax.fori_loop(0, seq // tq, q_step, 0, unroll=2)


def _latent_gqa(proj, row_block0, n_seq, seq, cache_k, cache_v, layer, rope, gqa_qn, gqa_kn, mixed_prev):
    past = cache_k.shape[3]
    group = N_HEADS // GQA_KV_HEADS
    tq = 256
    cache = pl.BlockSpec((None, None, None, past, HEAD_W), lambda b, n: (b, layer, n, 0, 0))
    vec = pl.BlockSpec((1, HEAD_W), lambda b, n: (0, 0))
    return pl.pallas_call(
        functools.partial(_gqa_body, seq=seq, tq=tq),
        grid=(n_seq, GQA_KV_HEADS),
        in_specs=[pl.BlockSpec((seq, group * HEAD_W), lambda b, n: (row_block0 + b, COL_C_Q // group + n)),
                  pl.BlockSpec((seq, HEAD_W), lambda b, n: (row_block0 + b, COL_C_K + n)),
                  pl.BlockSpec((seq, HEAD_W), lambda b, n: (row_block0 + b, COL_C_V + n)),
                  cache, cache,
                  pl.BlockSpec((3, seq, HEAD_W), lambda b, n: (0, 0, 0)),
                  vec, vec, ANY_SPEC],
        out_specs=pl.BlockSpec((seq, group * HEAD_W), lambda b, n: (row_block0 + b, n)),
        out_shape=jax.ShapeDtypeStruct(mixed_prev.shape, BF16),
        input_output_aliases={8: 0},
        scratch_shapes=[pltpu.VMEM((group, seq, HEAD_W), BF16),
                        pltpu.VMEM((seq + past, HEAD_W), BF16),
                        pltpu.VMEM((seq + past, 2 * HEAD_W), BF16)],
        compiler_params=_params("parallel", "parallel"),
        name="latent_gqa_attention",
    )(proj, proj, proj, cache_k, cache_v, rope, gqa_qn[None, :], gqa_kn[None, :], mixed_prev)


def _diff_body(q_ref, k_ref, v_ref, ck_ref, cv_ref, rope_ref, qn_ref, kn_ref, sub_ref, lam_ref, prev_ref, o_ref,
               qs_ref, k0_ref, k1_ref, vs_ref, *, seq, tq, lam_init):
    del prev_ref
    half = DF_DQK // 2
    k = _rope(_rms_halves(k_ref[...], kn_ref[...]), rope_ref, half)
    lo = _lane_lt(k.shape, DF_DQK)
    k0_ref[0:seq, :] = jnp.where(lo, k, 0.0).astype(BF16)
    k1_ref[0:seq, :] = jnp.where(lo, 0.0, k).astype(BF16)
    ck = ck_ref[...]
    lo_c = _lane_lt(ck.shape, DF_DQK)
    k0_ref[seq:, :] = jnp.where(lo_c, ck, 0.0).astype(BF16)
    k1_ref[seq:, :] = jnp.where(lo_c, 0.0, ck).astype(BF16)
    vs_ref[0:seq, :] = _with_ones(v_ref[...].astype(BF16))
    vs_ref[seq:, :] = _with_ones(cv_ref[...].astype(BF16))
    q = _rms_halves(q_ref[...], qn_ref[...]) * (DF_DQK ** -0.5)
    qs_ref[...] = _rope(q, rope_ref, half).astype(BF16)
    lam = _lambda(lam_ref, lam_init)
    k0 = k0_ref[...]
    k1 = k1_ref[...]
    vv = vs_ref[...]

    def q_step(i, carry):
        qrows = pl.ds(pl.multiple_of(i * tq, tq), tq)
        qb = qs_ref[qrows, :]
        o = _diff_pv(_dot_nt(qb, k0), _dot_nt(qb, k1), lam, vv)
        o_ref[qrows, :] = (_rms(o, sub_ref[...], HEAD_W) * (1.0 - lam_init)).astype(o_ref.dtype)
        return carry

    lax.fori_loop(0, seq // tq, q_step, 0, unroll=2)


def _latent_diff(proj, row_block0, n_seq, seq, cache_k2, cache_v, layer, rope, df_qn, df_kn, df_subln, df_lam,
                 lam_init, mixed_prev):
    past = cache_k2.shape[3]
    tq = 256

    def col(cb):
        return pl.BlockSpec((seq, HEAD_W), lambda b, h, cb=cb: (row_block0 + b, cb + h))

    cache = pl.BlockSpec((None, None, None, past, HEAD_W), lambda b, h: (b, layer, h, 0, 0))
    vec = pl.BlockSpec((1, HEAD_W), lambda b, h: (0, 0))
    kv_scratch = pltpu.VMEM((seq + past, HEAD_W), BF16)
    return pl.pallas_call(
        functools.partial(_diff_body, seq=seq, tq=tq, lam_init=lam_init),
        grid=(n_seq, N_HEADS),
        in_specs=[col(COL_D_Q), col(COL_D_K), col(COL_D_V), cache, cache,
                  pl.BlockSpec((3, seq, HEAD_W), lambda b, h: (0, 0, 0)),
                  vec, vec, vec, pl.BlockSpec((4, DF_DQK), lambda b, h: (0, 0)), ANY_SPEC],
        out_specs=pl.BlockSpec((seq, HEAD_W), lambda b, h: (row_block0 + b, h)),
        out_shape=jax.ShapeDtypeStruct(mixed_prev.shape, BF16),
        input_output_aliases={10: 0},
        scratch_shapes=[pltpu.VMEM((seq, HEAD_W), BF16), kv_scratch, kv_scratch,
                        pltpu.VMEM((seq + past, 2 * HEAD_W), BF16)],
        compiler_params=_params("parallel", "parallel"),
        name="latent_diff_attention",
    )(proj, proj, proj, cache_k2, cache_v, rope, jnp.tile(df_qn, 2)[None, :], jnp.tile(df_kn, 2)[None, :],
      df_subln[None, :], df_lam, mixed_prev)


def _first_max(vals):
    best = vals[0]
    idx = jnp.zeros(best.shape, jnp.int32)
    for i in range(1, len(vals)):
        better = vals[i] > best
        best = jnp.where(better, vals[i], best)
        idx = jnp.where(better, i, idx)
    return best, idx


def _pick(vals, idx):
    out = vals[0]
    for i in range(1, len(vals)):
        out = jnp.where(idx == i, vals[i], out)
    return out


def _outproj_body(ma_ref, mb_ref, mc_ref, md_ref, w_ref, hc_ref, hl_ref, mod_ref, n2_ref, rw_ref, rb_ref,
                  h1_ref, x2_ref, idx_ref, gate_ref, mixed_ref, *, ctx_tiles):
    for g, m_ref in enumerate((ma_ref, mb_ref, mc_ref, md_ref)):
        mixed_ref[:, g * GROUP_W:(g + 1) * GROUP_W] = m_ref[...]
    y = mod_ref[2:3, :] * _dot(mixed_ref[...], w_ref[...])

    def residual(h_ref):
        h1_ref[...] = h_ref[...] + y

    is_ctx = pl.program_id(0) < ctx_tiles
    pl.when(is_ctx)(lambda: residual(hc_ref))
    pl.when(jnp.logical_not(is_ctx))(lambda: residual(hl_ref))
    h1 = h1_ref[...]
    x2 = _rms(h1, n2_ref[...], D_MODEL) * (1.0 + mod_ref[4:5, :]) + mod_ref[3:4, :]
    x2_ref[...] = _pack_bf16_pairs(x2)
    logits = lax.dot_general(rw_ref[...], x2, (((1,), (1,)), ((), ())), precision=lax.Precision.HIGHEST,
                             preferred_element_type=F32)
    aff_all = _sigmoid(logits)
    sel_all = aff_all + rb_ref[...]
    aff = [aff_all[e:e + 1, :] for e in range(N_EXPERTS)]
    sel = [sel_all[e:e + 1, :] for e in range(N_EXPERTS)]
    neg = jnp.full(sel[0].shape, -jnp.inf, F32)
    scores = []
    for g in range(N_EXP_GROUPS):
        grp = sel[g * EXP_PER_GROUP:(g + 1) * EXP_PER_GROUP]
        m1, i1 = _first_max(grp)
        m2, _ = _first_max([jnp.where(i1 == j, neg, grp[j]) for j in range(EXP_PER_GROUP)])
        scores.append(m1 + m2)
    _, g_best = _first_max(scores)
    in_sel = [_pick([sel[g * EXP_PER_GROUP + j] for g in range(N_EXP_GROUPS)], g_best)
              for j in range(EXP_PER_GROUP)]
    in_aff = [_pick([aff[g * EXP_PER_GROUP + j] for g in range(N_EXP_GROUPS)], g_best)
              for j in range(EXP_PER_GROUP)]
    _, l1 = _first_max(in_sel)
    _, l2 = _first_max([jnp.where(l1 == j, neg, in_sel[j]) for j in range(EXP_PER_GROUP)])
    w1 = _pick(in_aff, l1)
    w2 = _pick(in_aff, l2)
    idx_ref[0:1, :] = g_best * EXP_PER_GROUP + l1
    idx_ref[1:2, :] = g_best * EXP_PER_GROUP + l2
    gate_ref[0:1, :] = w1 / (w1 + w2)
    gate_ref[1:2, :] = w2 / (w1 + w2)


def _output_projection(mixed4, w_out_bf16, h_ctx, h_lat, mod, norm2, router_w, router_b, cond_of_tile, tm):
    d = h_ctx.shape[1]
    t = h_ctx.shape[0] + h_lat.shape[0]
    ctx_tiles = h_ctx.shape[0] // tm
    slab = pl.BlockSpec((tm, GROUP_W), lambda i: (i, 0))
    return pl.pallas_call(
        functools.partial(_outproj_body, ctx_tiles=ctx_tiles),
        grid=(t // tm,),
        in_specs=[slab, slab, slab, slab,
                  pl.BlockSpec((d, d), lambda i: (0, 0))] + _group_specs(tm, d, ctx_tiles) + [
                  pl.BlockSpec((None, 6, d), lambda i: (cond_of_tile(i), 0, 0)),
                  pl.BlockSpec((1, d), lambda i: (0, 0)),
                  pl.BlockSpec((N_EXPERTS, d), lambda i: (0, 0)),
                  pl.BlockSpec((N_EXPERTS, 1), lambda i: (0, 0))],
        out_specs=[pl.BlockSpec((tm, d), lambda i: (i, 0)),
                   pl.BlockSpec((tm, d // 2), lambda i: (i, 0)),
                   pl.BlockSpec((2, tm), lambda i: (0, i)),
                   pl.BlockSpec((2, tm), lambda i: (0, i))],
        out_shape=[jax.ShapeDtypeStruct((t, d), F32), jax.ShapeDtypeStruct((t, d // 2), jnp.uint32),
                   jax.ShapeDtypeStruct((2, t), jnp.int32), jax.ShapeDtypeStruct((2, t), F32)],
        scratch_shapes=[pltpu.VMEM((tm, d), BF16)],
        compiler_params=_params("parallel", vmem=VMEM_LIMIT_PROJ),
        name="out_proj_residual_router",
    )(*mixed4, w_out_bf16, h_ctx, h_lat, mod, norm2[None, :], router_w.T, router_b[:, None])


SUBLANES = 8


def _vmem_row(ref, base, u):
    return ref.at[pl.ds(base, SUBLANES), :].at[pl.ds(u, 1), :]


def _for_row_groups(n_rows, fn):
    def body(g, carry):
        base = pl.multiple_of(g * SUBLANES, SUBLANES)
        for u in range(SUBLANES):
            fn(base, u)
        return carry

    lax.fori_loop(0, n_rows // SUBLANES, body, 0)


def _rows_wait(src_hbm, dst, sem, n_rows):
    pltpu.make_async_copy(src_hbm.at[pl.ds(0, n_rows), :], dst.at[pl.ds(0, n_rows), :], sem).wait()


def _dispatch_body(dest_ref, ps_ref, pe_ref, x_ref, xs_hbm, zero_ref, sem, zsem, *, n_tok, tile):
    i = pl.program_id(0)

    def zero_copy(e):
        first = pl.multiple_of(pe_ref[e] - MOE_ROWS, MOE_ROWS)
        return pltpu.make_async_copy(zero_ref, xs_hbm.at[pl.ds(first, MOE_ROWS), :], zsem)

    @pl.when(i == 0)
    def _():
        zero_ref[...] = jnp.zeros(zero_ref.shape, zero_ref.dtype)
        for e in range(N_EXPERTS):
            @pl.when(pe_ref[e] > ps_ref[e])
            def _(e=e):
                zero_copy(e).start()
        for e in range(N_EXPERTS):
            @pl.when(pe_ref[e] > ps_ref[e])
            def _(e=e):
                zero_copy(e).wait()

    for k in range(2):
        def scatter_row(base, u, k=k):
            row = dest_ref[k * n_tok + i * tile + base + u]
            pltpu.make_async_copy(_vmem_row(x_ref, base, u), xs_hbm.at[pl.ds(row, 1), :], sem).start()

        _for_row_groups(tile, scatter_row)
    for k in range(2):
        _rows_wait(x_ref, xs_hbm, sem, tile)


def _dispatch(x2, dest, pad_start, pad_end, n_rows):
    t, d = x2.shape
    tile = TOKEN_TILE
    grid_spec = pltpu.PrefetchScalarGridSpec(
        num_scalar_prefetch=3,
        grid=(t // tile,),
        in_specs=[pl.BlockSpec((tile, d), lambda i, dst, ps, pe: (i, 0))],
        out_specs=ANY_SPEC,
        scratch_shapes=[pltpu.VMEM((MOE_ROWS, d), x2.dtype), pltpu.SemaphoreType.DMA, pltpu.SemaphoreType.DMA],
    )
    return pl.pallas_call(
        functools.partial(_dispatch_body, n_tok=t, tile=tile),
        grid_spec=grid_spec,
        out_shape=jax.ShapeDtypeStruct((n_rows, d), x2.dtype),
        compiler_params=_params("arbitrary"),
        name="moe_dispatch",
    )(dest, pad_start, pad_end, x2)


def _expert_body(be_ref, nb_ref, x_ref, wg_ref, wu_ref, wd_ref, o_ref, wg_bf, wu_bf, wd_bf):
    i = pl.program_id(0)

    @pl.when(i < nb_ref[0])
    def _():
        @pl.when((i == 0) | (be_ref[i] != be_ref[jnp.maximum(i - 1, 0)]))
        def _():
            wg_bf[...] = wg_ref[...].astype(BF16)
            wu_bf[...] = wu_ref[...].astype(BF16)
            wd_bf[...] = wd_ref[...].astype(BF16)

        x = _unpack_bf16_pairs(x_ref[...]).astype(BF16)
        hdn = _silu(_dot(x, wg_bf[...])) * _dot(x, wu_bf[...])
        o_ref[...] = _pack_bf16_pairs(_dot(hdn.astype(BF16), wd_bf[...]))

    @pl.when(i >= nb_ref[0])
    def _():
        o_ref[...] = jnp.zeros(o_ref.shape, o_ref.dtype)


def _expert_blocks(xs, block_expert, n_used, w_gate, w_up, w_down, layer):
    n_rows, half = xs.shape
    d = 2 * half
    ff = w_gate.shape[-1]
    grid_spec = pltpu.PrefetchScalarGridSpec(
        num_scalar_prefetch=2,
        grid=(n_rows // MOE_ROWS,),
        in_specs=[pl.BlockSpec((MOE_ROWS, half), lambda i, be, nb: (jnp.minimum(i, nb[0] - 1), 0)),
                  pl.BlockSpec((None, None, d, ff), lambda i, be, nb: (layer, be[i], 0, 0)),
                  pl.BlockSpec((None, None, d, ff), lambda i, be, nb: (layer, be[i], 0, 0)),
                  pl.BlockSpec((None, None, ff, d), lambda i, be, nb: (layer, be[i], 0, 0))],
        out_specs=pl.BlockSpec((MOE_ROWS, half), lambda i, be, nb: (i, 0)),
        scratch_shapes=[pltpu.VMEM((d, ff), BF16), pltpu.VMEM((d, ff), BF16), pltpu.VMEM((ff, d), BF16)],
    )
    return pl.pallas_call(
        _expert_body,
        grid_spec=grid_spec,
        out_shape=jax.ShapeDtypeStruct((n_rows, half), jnp.uint32),
        compiler_params=_params("arbitrary"),
        name="moe_expert_blocks",
    )(block_expert, n_used, xs, w_gate, w_up, w_down)


def _combine_body(dest_ref, h_ref, gate_ref, y_hbm, mod_ref, *rest, n_tok, tile, ctx_tiles):
    out_refs, (ybuf, sem) = rest[:-2], rest[-2:]
    i = pl.program_id(0)
    n_tiles = pl.num_programs(0)

    def start(blk, slot):
        for k in range(2):
            def gather_row(base, u, k=k):
                row = dest_ref[k * n_tok + blk * tile + base + u]
                pltpu.make_async_copy(y_hbm.at[pl.ds(row, 1), :], _vmem_row(ybuf.at[slot, k], base, u),
                                      sem.at[slot]).start()

            _for_row_groups(tile, gather_row)

    @pl.when(i == 0)
    def _():
        start(0, 0)

    @pl.when(i + 1 < n_tiles)
    def _():
        start(i + 1, (i + 1) % 2)

    slot = i % 2
    for k in range(2):
        _rows_wait(y_hbm, ybuf.at[slot, k], sem.at[slot], tile)
    gate = gate_ref[...]
    y0 = _unpack_bf16_pairs(ybuf[slot, 0])
    y1 = _unpack_bf16_pairs(ybuf[slot, 1])
    out = h_ref[...] + mod_ref[5:6, :] * (gate[:, 0:1] * y0 + gate[:, 1:2] * y1)
    @pl.when(i < ctx_tiles)
    def _():
        out_refs[0][...] = out

    @pl.when(i >= ctx_tiles)
    def _():
        out_refs[1][...] = out


def _combine(h1, yb, dest, gates, mod, cond_of_tile, tile, split_rows):
    t, d = h1.shape
    row_tile = pl.BlockSpec((tile, d), lambda i, dst: (i, 0))
    ctx_tiles = split_rows // tile
    out_specs = _group_specs(tile, d, ctx_tiles)
    out_shape = [jax.ShapeDtypeStruct((split_rows, d), F32), jax.ShapeDtypeStruct((t - split_rows, d), F32)]
    grid_spec = pltpu.PrefetchScalarGridSpec(
        num_scalar_prefetch=1,
        grid=(t // tile,),
        in_specs=[row_tile,
                  pl.BlockSpec((tile, 2), lambda i, dst: (i, 0)),
                  ANY_SPEC,
                  pl.BlockSpec((None, 6, d), lambda i, dst: (cond_of_tile(i), 0, 0))],
        out_specs=out_specs,
        scratch_shapes=[pltpu.VMEM((2, 2, tile, yb.shape[1]), yb.dtype), pltpu.SemaphoreType.DMA((2,))],
    )
    return pl.pallas_call(
        functools.partial(_combine_body, n_tok=t, tile=tile, ctx_tiles=ctx_tiles),
        grid_spec=grid_spec,
        out_shape=out_shape,
        compiler_params=_params("arbitrary"),
        name="moe_gated_residual",
    )(dest, h1, gates, yb, mod)


def _moe(h1, x2, idx_t, gate_t, mod, w_gate, w_up, w_down, layer, cond_of_tile, split_rows):
    t, d = h1.shape
    n = 2 * t
    experts = idx_t.reshape(n)
    onehot = (experts[:, None] == jnp.arange(N_EXPERTS, dtype=jnp.int32)[None, :]).astype(BF16)
    blocks = onehot.reshape(n // TOKEN_TILE, TOKEN_TILE, N_EXPERTS)
    tri = jnp.asarray(np.tril(np.ones((TOKEN_TILE, TOKEN_TILE), np.float32)), BF16)
    within = jnp.einsum('ij,bjk->bik', tri, blocks, preferred_element_type=F32)
    block_total = within[:, -1, :]
    block_first = jnp.cumsum(block_total, axis=0) - block_total
    counts = (block_first[-1] + block_total[-1]).astype(jnp.int32)
    before = (within + block_first[:, None, :]).reshape(n, N_EXPERTS) - 1.0
    rank = jnp.sum(before * onehot.astype(F32), axis=1).astype(jnp.int32)
    padded = (counts + MOE_ROWS - 1) // MOE_ROWS * MOE_ROWS
    pad_end = jnp.cumsum(padded).astype(jnp.int32)
    pad_start = pad_end - padded
    dest = (pad_start[experts] + rank).astype(jnp.int32)
    n_blocks = (n + N_EXPERTS * (MOE_ROWS - 1) + MOE_ROWS - 1) // MOE_ROWS
    block_first_row = jnp.arange(n_blocks, dtype=jnp.int32) * MOE_ROWS
    block_expert = jnp.minimum(jnp.sum((pad_end[None, :] <= block_first_row[:, None]).astype(jnp.int32), axis=1),
                               N_EXPERTS - 1)
    n_used = (pad_end[-1:] // MOE_ROWS).astype(jnp.int32)
    xs = _dispatch(x2, dest, pad_start, pad_end, n_blocks * MOE_ROWS)
    yb = _expert_blocks(xs, block_expert, n_used, w_gate, w_up, w_down, layer)
    return _combine(h1, yb, dest, gate_t.T, mod, cond_of_tile, TOKEN_TILE, split_rows)


def kernel(x_prompt, x_sample, cache_na_k, cache_na_v, cache_gqa_k, cache_gqa_v, cache_diff_k, cache_diff_v, state_hgrn, c, c_ctx, w_mod, b_mod, norm1, norm2, w_in, w_out, hg_lb_logits, hg_onorm, na_qn, na_kn, na_rpb, gqa_qn, gqa_kn, df_qn, df_kn, df_lam, df_subln, router_w, router_b, w_gate, w_up, w_down):
    n_ctx, ctx_len, d = x_prompt.shape
    n_lat, lat_len, _ = x_sample.shape
    depth = w_in.shape[0]
    t_ctx = n_ctx * ctx_len
    assert t_ctx % lat_len == 0 and lat_len % GRID_W == 0 and lat_len // GRID_W >= WIN_ROWS
    tm = next(m for m in (1024, 512, 256) if t_ctx % m == 0 and lat_len % m == 0)
    tm2 = min(tm, 512)
    lat_block0 = t_ctx // lat_len

    def cond_tile(tile_rows):
        def cond_of_tile(i):
            return jnp.where(i < t_ctx // tile_rows, 0, 1 + (i - t_ctx // tile_rows) // (lat_len // tile_rows))
        return cond_of_tile

    sm = jax.nn.softmax(hg_lb_logits.astype(F32), axis=0)
    lower = jnp.cumsum(sm, axis=0) - sm[0:1]
    mod_all = _modulation(jnp.concatenate([c_ctx[None, :], c], axis=0), w_mod, b_mod)
    mod_all = mod_all.reshape(depth, 1 + n_lat, 6, d)
    hgrn_consts = _hgrn_constants(HGRN_CHUNK)
    rope_c = _rope_tables(lat_len, HEAD_W)
    rope_d = _rope_tables(lat_len, DF_DQK)
    past = cache_diff_k.shape[4]
    cache_diff_k2 = cache_diff_k.transpose(0, 1, 2, 4, 3, 5).reshape(n_lat, depth, N_HEADS, past, HEAD_W)

    h_ctx, h_lat = x_prompt.reshape(t_ctx, d), x_sample.reshape(n_lat * lat_len, d)
    caches, states = [], None
    for layer in range(depth):
        mod = mod_all[layer]
        lam_init = 0.8 - 0.6 * math.exp(-0.3 * layer)
        proj = _input_projection(h_ctx, h_lat, mod, norm1[layer], w_in[layer].astype(BF16), cond_tile(tm), tm)
        mix_a, states = _hgrn(proj, 0, n_ctx, ctx_len, lower[layer], hg_onorm[layer], hgrn_consts, None, None,
                              states, layer, depth)
        gains = (na_qn[layer], na_kn[layer], gqa_qn[layer], gqa_kn[layer], df_qn[layer], df_kn[layer],
                 df_subln[layer], df_lam[layer])
        mix_b, mix_c, mix_d, *caches = _context_attention(proj, n_ctx, ctx_len, gains, lam_init, layer, depth,
                                                          caches)
        mix_a, _ = _hgrn(proj, lat_block0, n_lat, lat_len, lower[layer], hg_onorm[layer], hgrn_consts,
                         state_hgrn, mix_a, None, layer, depth)
        mix_b = _latent_na(proj, lat_block0, n_lat, lat_len, cache_na_k, cache_na_v, layer,
                           _na_bias(na_rpb[layer]), na_qn[layer], na_kn[layer], mix_b)
        mix_c = _latent_gqa(proj, lat_block0, n_lat, lat_len, cache_gqa_k, cache_gqa_v, layer, rope_c,
                            gqa_qn[layer], gqa_kn[layer], mix_c)
        mix_d = _latent_diff(proj, lat_block0, n_lat, lat_len, cache_diff_k2, cache_diff_v, layer, rope_d,
                             df_qn[layer], df_kn[layer], df_subln[layer], df_lam[layer], lam_init, mix_d)
        h1, x2, idx_t, gate_t = _output_projection((mix_a, mix_b, mix_c, mix_d), w_out[layer].astype(BF16), h_ctx,
                                                   h_lat, mod, norm2[layer], router_w, router_b, cond_tile(tm2), tm2)
        h_ctx, h_lat = _moe(h1, x2, idx_t, gate_t, mod, w_gate, w_up, w_down, layer, cond_tile(TOKEN_TILE), t_ctx)
    y_prompt = h_ctx.reshape(n_ctx, ctx_len, d)
    y_sample = h_lat.reshape(n_lat, lat_len, d)
    return (y_prompt, y_sample, *caches, states)
```

```python
import functools
import math

import numpy as np
import jax
import jax.numpy as jnp
from jax import lax
from jax.experimental import pallas as pl
from jax.experimental.pallas import tpu as pltpu

D_MODEL = 2048
GRID_W = 64
GROUP_W = D_MODEL // 4
N_HEADS = 4
HEAD_W = GROUP_W // N_HEADS
GQA_KV_HEADS = 2
DF_DQK = HEAD_W // 2
WIN_ROWS = 8
WIN_COLS = 16
N_EXPERTS = 16
N_EXP_GROUPS = 4
EXP_PER_GROUP = N_EXPERTS // N_EXP_GROUPS
EXPERT_FF = D_MODEL // 4
ROPE_THETA = 10000.0
EPS = 1e-6
NEG_INF = -1e30
IN_WIDTH = 13 * GROUP_W

COL_A_Q, COL_A_FF, COL_A_FB, COL_A_I, COL_A_G = 0, 4, 8, 12, 16
COL_B_Q, COL_B_K, COL_B_V = 20, 24, 28
COL_C_Q, COL_C_K, COL_C_V = 32, 36, 38
COL_D_Q, COL_D_K, COL_D_V = 40, 44, 48

HGRN_CHUNK = 128
HGRN_UNROLL = 2
HGRN_INPUT_VMEM = 24 * 1024 * 1024
MOE_ROWS = 256
TOKEN_TILE = 256
VMEM_LIMIT = 48 * 1024 * 1024
VMEM_LIMIT_PROJ = 56 * 1024 * 1024

F32 = jnp.float32
BF16 = jnp.bfloat16
ANY_SPEC = pl.BlockSpec(memory_space=pl.ANY)


def _params(*sem, vmem=VMEM_LIMIT):
    return pltpu.CompilerParams(dimension_semantics=sem, vmem_limit_bytes=vmem)


def _sigmoid(x):
    return 1.0 / (1.0 + jnp.exp(-x))


def _silu(x):
    return x * _sigmoid(x)


def _rms(x, gain, n):
    return x * lax.rsqrt(jnp.sum(x * x, axis=-1, keepdims=True) * (1.0 / n) + EPS) * gain


def _dot(a, b):
    return jnp.dot(a, b, preferred_element_type=F32)


def _dot_nt(a, b):
    return lax.dot_general(a, b, (((1,), (1,)), ((), ())), preferred_element_type=F32)


def _dot_tn(a, b):
    return lax.dot_general(a, b, (((0,), (0,)), ((), ())), preferred_element_type=F32)


def _pack_bf16_pairs(x):
    k = x.shape[1] // 2
    lo = lax.bitcast_convert_type(x[:, :k].astype(BF16).astype(F32), jnp.uint32) >> 16
    hi = lax.bitcast_convert_type(x[:, k:].astype(BF16).astype(F32), jnp.uint32)
    return hi | lo


def _unpack_bf16_pairs(w):
    lo = lax.bitcast_convert_type(w << 16, F32)
    hi = lax.bitcast_convert_type(w & jnp.uint32(0xFFFF0000), F32)
    return jnp.concatenate([lo, hi], axis=1)


def _aligned(x, m):
    return x if isinstance(x, int) else pl.multiple_of(x, m)


def _alias_kwargs(n_inputs, prev, first_out):
    return ([ANY_SPEC] * len(prev), list(prev), {n_inputs + k: first_out + k for k in range(len(prev))})


def _mod_body(cond_ref, w_ref, b_ref, o_ref):
    w = w_ref[...]
    for c in range(cond_ref.shape[0]):
        s = _silu(cond_ref[c])
        o_ref[c:c + 1, :] = jnp.sum(w * s, axis=0, keepdims=True) + b_ref[...]


def _modulation(cond, w_mod, b_mod):
    depth, d, n6 = w_mod.shape
    nc = cond.shape[0]
    tn = 1024
    return pl.pallas_call(
        _mod_body,
        grid=(depth, n6 // tn),
        in_specs=[pl.BlockSpec((nc, d, 1), lambda l, j: (0, 0, 0)),
                  pl.BlockSpec((None, d, tn), lambda l, j: (l, 0, j)),
                  pl.BlockSpec((None, 1, tn), lambda l, j: (l, 0, j))],
        out_specs=pl.BlockSpec((None, nc, tn), lambda l, j: (l, 0, j)),
        out_shape=jax.ShapeDtypeStruct((depth, nc, n6), F32),
        compiler_params=_params("parallel", "parallel"),
        name="adaln_modulation",
    )(cond[:, :, None], w_mod, b_mod[:, None, :])


def _group_specs(tile, d, ctx_tiles):
    return [pl.BlockSpec((tile, d), lambda i, *_: (jnp.minimum(i, ctx_tiles - 1), 0)),
            pl.BlockSpec((tile, d), lambda i, *_: (jnp.maximum(i - ctx_tiles, 0), 0))]


def _inproj_body(hc_ref, hl_ref, mod_ref, n1_ref, w_ref, o_ref, xn_ref, *, ctx_tiles):
    def normalise(h_ref):
        y = _rms(h_ref[...], n1_ref[...], D_MODEL)
        xn_ref[...] = (y * (1.0 + mod_ref[1:2, :]) + mod_ref[0:1, :]).astype(BF16)

    first = pl.program_id(1) == 0
    is_ctx = pl.program_id(0) < ctx_tiles
    pl.when(first & is_ctx)(lambda: normalise(hc_ref))
    pl.when(first & jnp.logical_not(is_ctx))(lambda: normalise(hl_ref))
    o_ref[...] = _dot(xn_ref[...], w_ref[...])


def _input_projection(h_ctx, h_lat, mod, norm1, w_in_bf16, cond_of_tile, tm):
    d = h_ctx.shape[1]
    t = h_ctx.shape[0] + h_lat.shape[0]
    ctx_tiles = h_ctx.shape[0] // tm
    n = w_in_bf16.shape[1]
    tn = 512
    return pl.pallas_call(
        functools.partial(_inproj_body, ctx_tiles=ctx_tiles),
        grid=(t // tm, n // tn),
        in_specs=_group_specs(tm, d, ctx_tiles) + [
                  pl.BlockSpec((None, 6, d), lambda i, j: (cond_of_tile(i), 0, 0)),
                  pl.BlockSpec((1, d), lambda i, j: (0, 0)),
                  pl.BlockSpec((d, tn), lambda i, j: (0, j))],
        out_specs=pl.BlockSpec((tm, tn), lambda i, j: (i, j)),
        out_shape=jax.ShapeDtypeStruct((t, n), F32),
        scratch_shapes=[pltpu.VMEM((tm, d), BF16)],
        compiler_params=_params("parallel", "arbitrary", vmem=VMEM_LIMIT_PROJ),
        name="norm_modulate_in_proj",
    )(h_ctx, h_lat, mod, norm1[None, :], w_in_bf16)


def _hgrn_constants(c):
    nl = int(math.log2(c))
    idx = np.arange(c)
    e = np.zeros((nl + 2, c, c), np.float32)
    m = np.zeros((nl + 1, c, c), np.float32)
    e[0] = idx[None, :] <= idx[:, None]
    e[1] = idx[None, :] > idx[:, None]
    m[0] = np.eye(c)
    for li in range(nl):
        s = c >> (li + 1)
        parent = idx // (2 * s)
        right = (idx % (2 * s)) >= s
        ref = parent * 2 * s + s - 1
        for i in range(c):
            if right[i]:
                e[2 + li, i, ref[i] + 1:i + 1] = 1.0
            else:
                e[2 + li, i, i + 1:ref[i] + 1] = 1.0
        m[1 + li] = right[:, None] & ~right[None, :] & (parent[:, None] == parent[None, :])
    e2 = np.stack([e, e[:, ::-1, ::-1]]).reshape(2, (nl + 2) * c, c)
    m2 = np.stack([m, m[:, ::-1, ::-1]])
    return jnp.asarray(e2, BF16), jnp.asarray(m2, F32)


def _hgrn_body(*refs, seq, chunk, unroll, heads, has_s0, emit_state, n_alias):
    q_ref, ff_ref, fb_ref, i_ref, g_ref, lb_ref, on_ref, e_ref, m_ref = refs[:9]
    pos = 9
    s0_ref = None
    if has_s0:
        s0_ref = refs[pos]
        pos += 1
    pos += n_alias
    o_ref = refs[pos]
    pos += 1
    if emit_state:
        st_ref = refs[pos]
        pos += 1
    of_ref, ob_ref = refs[pos], refs[pos + 1]
    c = chunk
    n_chunks = seq // c
    assert seq % c == 0 and n_chunks % unroll == 0
    n_levels = m_ref.shape[1] - 1
    gate_refs = (ff_ref, fb_ref)
    out_refs = (of_ref, ob_ref)

    def chunk_step(c0, d, hh, st):
        rows = pl.ds(c0, c)
        lanes = slice(hh * HEAD_W, (hh + 1) * HEAD_W)
        lb = lb_ref[d:d + 1, lanes]
        f = lb + (1.0 - lb) * _sigmoid(gate_refs[d][rows, lanes])
        g = jnp.log(f)
        k = 1.0 - f
        q = _silu(q_ref[rows, lanes])
        v = i_ref[rows, lanes].astype(BF16)
        g_hi = g.astype(BF16)
        g_lo = (g - g_hi.astype(F32)).astype(BF16)
        g2 = _dot(e_ref[d], jnp.concatenate([g_hi, g_lo], axis=1))
        x = jnp.exp(g2[:, 0:HEAD_W] + g2[:, HEAD_W:2 * HEAD_W])
        x_cum = x[0:c]
        x_tail = x[c:2 * c]
        s = m_ref[d, 0] * _dot_nt(q.astype(BF16), k.astype(BF16))
        for lv in range(n_levels):
            x_l = x[(2 + lv) * c:(3 + lv) * c]
            s = s + m_ref[d, 1 + lv] * _dot_nt((q * x_l).astype(BF16), (k * x_l).astype(BF16))
        o = _dot_nt((q * x_cum).astype(BF16), st.astype(BF16)) + _dot(s.astype(BF16), v)
        out_refs[d][rows, lanes] = o
        total = x_cum[c - 1:c, :] if d == 0 else x_cum[0:1, :]
        return st * total + _dot_tn(v, (k * x_tail).astype(BF16))

    if has_s0:
        states0 = tuple(s0_ref[d, hh].T for hh in range(heads) for d in range(2))
    else:
        states0 = tuple(jnp.zeros((HEAD_W, HEAD_W), F32) for _ in range(2 * heads))

    def loop(t, states):
        states = list(states)
        for u in range(unroll):
            j = t * unroll + u
            for hh in range(heads):
                states[2 * hh] = chunk_step(_aligned(j * c, c), 0, hh, states[2 * hh])
                states[2 * hh + 1] = chunk_step(_aligned((n_chunks - 1 - j) * c, c), 1, hh, states[2 * hh + 1])
        return tuple(states)

    if n_chunks == unroll:
        states = loop(0, states0)
    else:
        states = lax.fori_loop(0, n_chunks // unroll, loop, states0)
    for hh in range(heads):
        lanes = slice(hh * HEAD_W, (hh + 1) * HEAD_W)
        o = of_ref[:, lanes] + ob_ref[:, lanes]
        o_ref[:, lanes] = (_rms(o, on_ref[...], HEAD_W) * _silu(g_ref[:, lanes])).astype(o_ref.dtype)
        if emit_state:
            st_ref[0, hh] = states[2 * hh].T
            st_ref[1, hh] = states[2 * hh + 1].T


def _hgrn(proj, row_block0, n_seq, seq, lower, onorm, consts, s0, mixed_prev, state_prev, layer, depth):
    e_mat, masks = consts
    latent = s0 is not None
    hps = next(n for n in (4, 2, 1) if 5 * 2 * seq * n * HEAD_W * 4 <= HGRN_INPUT_VMEM)
    width = hps * HEAD_W

    def col(cb):
        return pl.BlockSpec((seq, width), lambda b, h, cb=cb: (row_block0 + b, cb // hps + h))

    state_spec = pl.BlockSpec((None, None, 2, hps, HEAD_W, HEAD_W), lambda b, h: (b, layer, 0, h, 0, 0))
    in_specs = [col(COL_A_Q), col(COL_A_FF), col(COL_A_FB), col(COL_A_I), col(COL_A_G),
                pl.BlockSpec((2, width), lambda b, h: (0, h)),
                pl.BlockSpec((1, HEAD_W), lambda b, h: (0, 0)),
                pl.BlockSpec(e_mat.shape, lambda b, h: (0, 0, 0)),
                pl.BlockSpec(masks.shape, lambda b, h: (0, 0, 0, 0))]
    args = [proj, proj, proj, proj, proj, lower, onorm[None, :], e_mat, masks]
    if latent:
        in_specs.append(state_spec)
        args.append(s0)
        prev = [mixed_prev]
    else:
        prev = [] if state_prev is None else [state_prev]
    alias_specs, alias_args, aliases = _alias_kwargs(len(args), prev, 0 if latent else 1)
    out_specs = [pl.BlockSpec((seq, width), lambda b, h: (row_block0 + b, h))]
    out_shape = [jax.ShapeDtypeStruct((proj.shape[0], GROUP_W), BF16)]
    if not latent:
        out_specs.append(state_spec)
        out_shape.append(jax.ShapeDtypeStruct((n_seq, depth, 2, N_HEADS, HEAD_W, HEAD_W), F32))
    res = pl.pallas_call(
        functools.partial(_hgrn_body, seq=seq, chunk=HGRN_CHUNK, unroll=HGRN_UNROLL, heads=hps, has_s0=latent,
                          emit_state=not latent, n_alias=len(prev)),
        grid=(n_seq, N_HEADS // hps),
        in_specs=in_specs + alias_specs, out_specs=out_specs, out_shape=out_shape,
        input_output_aliases=aliases,
        scratch_shapes=[pltpu.VMEM((seq, width), F32), pltpu.VMEM((seq, width), F32)],
        compiler_params=_params("parallel", "parallel"),
        name="hgrn2_latent" if latent else "hgrn2_context",
    )(*args, *alias_args)
    return (res[0], None) if latent else res


def _with_ones(v):
    return jnp.concatenate([v, jnp.ones_like(v)], axis=1)


def _softmax_pv(scores, values1):
    mx = functools.reduce(jnp.maximum, [jnp.max(s, axis=-1, keepdims=True) for s in scores])
    acc = functools.reduce(lambda a, b: a + b,
                           [_dot(jnp.exp(s - mx).astype(BF16), v) for s, v in zip(scores, values1)])
    return acc[:, 0:HEAD_W] / acc[:, HEAD_W:HEAD_W + 1]


def _diff_pv(s0, s1, lam, values1):
    return _softmax_pv([s0], [values1]) - lam * _softmax_pv([s1], [values1])


def _lane_lt(shape, n):
    return lax.broadcasted_iota(jnp.int32, shape, len(shape) - 1) < n


def _rms_halves(x, gain2):
    lo = _lane_lt(x.shape, DF_DQK)
    sq = x * x
    ss_lo = jnp.sum(jnp.where(lo, sq, 0.0), axis=-1, keepdims=True)
    ss_hi = jnp.sum(sq, axis=-1, keepdims=True) - ss_lo
    inv = jnp.where(lo, lax.rsqrt(ss_lo * (1.0 / DF_DQK) + EPS), lax.rsqrt(ss_hi * (1.0 / DF_DQK) + EPS))
    return x * inv * gain2


def _lambda(lam_ref, lam_init):
    l = lam_ref[...]
    return (jnp.exp(jnp.sum(l[0:1] * l[1:2], axis=-1, keepdims=True))
            - jnp.exp(jnp.sum(l[2:3] * l[3:4], axis=-1, keepdims=True)) + lam_init)


N_CTX_ATTN_INPUTS = 17


def _ctx_attn_body(*refs, lam_init):
    (bq_ref, bk_ref, bv_ref, cq_ref, ck_ref, cv_ref, dq_ref, dk_ref, dv_ref,
     naq_ref, nak_ref, gq_ref, gk_ref, dfq_ref, dfk_ref, sub_ref, lam_ref) = refs[:N_CTX_ATTN_INPUTS]
    ob_ref, oc_ref, od_ref, kb_ref, vb_ref, kc_ref, vc_ref, kd_ref, vd_ref = refs[-9:]
    group = N_HEADS // GQA_KV_HEADS
    scale = HEAD_W ** -0.5
    lam = _lambda(lam_ref, lam_init)
    kc = _rms(ck_ref[...], gk_ref[...], HEAD_W)
    vc = cv_ref[...]
    kc_ref[...] = kc
    vc_ref[...] = vc
    kc16 = kc.astype(BF16)
    vc1 = _with_ones(vc.astype(BF16))
    for g in range(group):
        lanes = slice(g * HEAD_W, (g + 1) * HEAD_W)
        qc = (_rms(cq_ref[:, lanes], gq_ref[...], HEAD_W) * scale).astype(BF16)
        oc_ref[:, lanes] = _softmax_pv([_dot_nt(qc, kc16)], [vc1]).astype(oc_ref.dtype)
        kb = _rms(bk_ref[:, lanes], nak_ref[...], HEAD_W)
        vb = bv_ref[:, lanes]
        kb_ref[g] = kb
        vb_ref[g] = vb
        qb = (_rms(bq_ref[:, lanes], naq_ref[...], HEAD_W) * scale).astype(BF16)
        ob_ref[:, lanes] = _softmax_pv([_dot_nt(qb, kb.astype(BF16))],
                                       [_with_ones(vb.astype(BF16))]).astype(ob_ref.dtype)
        kd = _rms_halves(dk_ref[:, lanes], dfk_ref[...])
        vd = dv_ref[:, lanes]
        kd_ref[g, 0] = kd[:, 0:DF_DQK]
        kd_ref[g, 1] = kd[:, DF_DQK:2 * DF_DQK]
        vd_ref[g] = vd
        qd = (_rms_halves(dq_ref[:, lanes], dfq_ref[...]) * (DF_DQK ** -0.5)).astype(BF16)
        lo = _lane_lt(kd.shape, DF_DQK)
        s0 = _dot_nt(qd, jnp.where(lo, kd, 0.0).astype(BF16))
        s1 = _dot_nt(qd, jnp.where(lo, 0.0, kd).astype(BF16))
        od = _diff_pv(s0, s1, lam, _with_ones(vd.astype(BF16)))
        od_ref[:, lanes] = (_rms(od, sub_ref[...], HEAD_W) * (1.0 - lam_init)).astype(od_ref.dtype)


def _context_attention(proj, n_seq, seq, gains, lam_init, layer, depth, caches_prev):
    na_qn, na_kn, gqa_qn, gqa_kn, df_qn, df_kn, df_subln, df_lam = gains
    group = N_HEADS // GQA_KV_HEADS
    width = group * HEAD_W

    def heads(cb):
        return pl.BlockSpec((seq, width), lambda b, n, cb=cb: (b, cb // group + n))

    def kv_head(cb):
        return pl.BlockSpec((seq, HEAD_W), lambda b, n, cb=cb: (b, cb + n))

    vec = pl.BlockSpec((1, HEAD_W), lambda b, n: (0, 0))
    cache_heads = pl.BlockSpec((None, None, group, seq, HEAD_W), lambda b, n: (b, layer, n, 0, 0))
    cache_kv = pl.BlockSpec((None, None, None, seq, HEAD_W), lambda b, n: (b, layer, n, 0, 0))
    mixed = pl.BlockSpec((seq, width), lambda b, n: (b, n))
    mixed_shape = jax.ShapeDtypeStruct((proj.shape[0], GROUP_W), BF16)
    cache4 = jax.ShapeDtypeStruct((n_seq, depth, N_HEADS, seq, HEAD_W), F32)
    cache2 = jax.ShapeDtypeStruct((n_seq, depth, GQA_KV_HEADS, seq, HEAD_W), F32)
    cache_dk = jax.ShapeDtypeStruct((n_seq, depth, N_HEADS, 2, seq, DF_DQK), F32)
    args = [proj] * 9 + [na_qn[None, :], na_kn[None, :], gqa_qn[None, :], gqa_kn[None, :],
                         jnp.tile(df_qn, 2)[None, :], jnp.tile(df_kn, 2)[None, :], df_subln[None, :], df_lam]
    assert len(args) == N_CTX_ATTN_INPUTS
    alias_specs, alias_args, aliases = _alias_kwargs(len(args), caches_prev, 3)
    return pl.pallas_call(
        functools.partial(_ctx_attn_body, lam_init=lam_init),
        grid=(n_seq, GQA_KV_HEADS),
        in_specs=[heads(COL_B_Q), heads(COL_B_K), heads(COL_B_V),
                  heads(COL_C_Q), kv_head(COL_C_K), kv_head(COL_C_V),
                  heads(COL_D_Q), heads(COL_D_K), heads(COL_D_V),
                  vec, vec, vec, vec, vec, vec, vec,
                  pl.BlockSpec((4, DF_DQK), lambda b, n: (0, 0))] + alias_specs,
        out_specs=[mixed, mixed, mixed, cache_heads, cache_heads, cache_kv, cache_kv,
                   pl.BlockSpec((None, None, group, 2, seq, DF_DQK), lambda b, n: (b, layer, n, 0, 0, 0)),
                   cache_heads],
        out_shape=[mixed_shape, mixed_shape, mixed_shape, cache4, cache4, cache2, cache2, cache_dk, cache4],
        input_output_aliases=aliases,
        compiler_params=_params("parallel", "parallel"),
        name="context_attention",
    )(*args, *alias_args)


def _rope_tables(n_tokens, rot_dim):
    t = np.arange(n_tokens)
    row = (t // GRID_W).astype(np.float32)
    col = (t % GRID_W).astype(np.float32)
    n_freq = rot_dim // 4
    inv = (np.float32(ROPE_THETA) ** (-np.arange(n_freq, dtype=np.float32) / np.float32(n_freq))).astype(np.float32)
    ang = np.concatenate([row[:, None] * inv, col[:, None] * inv], axis=-1).astype(np.float32)
    cos, sin, zero = np.cos(ang), np.sin(ang), np.zeros_like(ang)
    reps = HEAD_W // rot_dim
    a = np.tile(np.concatenate([cos, cos], axis=-1), (1, reps))
    b = np.tile(np.concatenate([-sin, zero], axis=-1), (1, reps))
    c = np.tile(np.concatenate([zero, sin], axis=-1), (1, reps))
    return jnp.asarray(np.stack([a, b, c]), F32)


def _rope(x, tab_ref, half):
    return (x * tab_ref[0] + pltpu.roll(x, HEAD_W - half, 1) * tab_ref[1]
            + pltpu.roll(x, half, 1) * tab_ref[2])


def _na_body(q_ref, k_ref, v_ref, ck_ref, cv_ref, bias_ref, qn_ref, kn_ref, prev_ref, o_ref,
             qs_ref, ks_ref, vs_ref, *, seq):
    del prev_ref
    rows = seq // GRID_W
    n_win = WIN_ROWS * GRID_W
    qs_ref[...] = (_rms(q_ref[...], qn_ref[...], HEAD_W) * (HEAD_W ** -0.5)).astype(BF16)
    ks_ref[...] = _rms(k_ref[...], kn_ref[...], HEAD_W).astype(BF16)
    vs_ref[...] = _with_ones(v_ref[...].astype(BF16))
    ck = ck_ref[...].astype(BF16)
    cv = _with_ones(cv_ref[...].astype(BF16))

    def row_step(r, carry):
        start = jnp.clip(r - WIN_ROWS // 2, 0, rows - WIN_ROWS)
        win = pl.ds(pl.multiple_of(start * GRID_W, GRID_W), n_win)
        qrows = pl.ds(pl.multiple_of(r * GRID_W, GRID_W), GRID_W)
        q = qs_ref[qrows, :]
        s_win = _dot_nt(q, ks_ref[win, :]) + bias_ref[start - r + (WIN_ROWS - 1)]
        s_ctx = _dot_nt(q, ck)
        o_ref[qrows, :] = _softmax_pv([s_win, s_ctx], [vs_ref[win, :], cv]).astype(o_ref.dtype)
        return carry

    lax.fori_loop(0, rows, row_step, 0, unroll=2)


def _na_bias(rpb):
    col = np.arange(GRID_W)
    col_start = np.clip(col - WIN_COLS // 2, 0, GRID_W - WIN_COLS)
    col_ok = (col[None, :] >= col_start[:, None]) & (col[None, :] < col_start[:, None] + WIN_COLS)
    dc = np.clip(col[None, :] - col[:, None] + WIN_COLS - 1, 0, 2 * WIN_COLS - 2).reshape(-1)
    onehot = (np.arange(2 * WIN_COLS - 1)[:, None] == dc[None, :]).astype(np.float32)
    per_dr = jnp.einsum('hdc,cn->hdn', rpb.astype(F32), jnp.asarray(onehot), precision=lax.Precision.HIGHEST)
    per_dr = jnp.where(col_ok[None, None], per_dr.reshape(rpb.shape[0], -1, GRID_W, GRID_W), NEG_INF)
    wins = jnp.stack([per_dr[:, o:o + WIN_ROWS] for o in range(WIN_ROWS)], axis=1)
    return wins.transpose(0, 1, 3, 2, 4).reshape(rpb.shape[0], WIN_ROWS, GRID_W, WIN_ROWS * GRID_W)


def _latent_na(proj, row_block0, n_seq, seq, cache_k, cache_v, layer, bias, na_qn, na_kn, mixed_prev):
    past = cache_k.shape[3]

    def col(cb):
        return pl.BlockSpec((seq, HEAD_W), lambda b, h, cb=cb: (row_block0 + b, cb + h))

    cache = pl.BlockSpec((None, None, None, past, HEAD_W), lambda b, h: (b, layer, h, 0, 0))
    vec = pl.BlockSpec((1, HEAD_W), lambda b, h: (0, 0))
    return pl.pallas_call(
        functools.partial(_na_body, seq=seq),
        grid=(n_seq, N_HEADS),
        in_specs=[col(COL_B_Q), col(COL_B_K), col(COL_B_V), cache, cache,
                  pl.BlockSpec((None, WIN_ROWS, GRID_W, WIN_ROWS * GRID_W), lambda b, h: (h, 0, 0, 0)),
                  vec, vec, ANY_SPEC],
        out_specs=pl.BlockSpec((seq, HEAD_W), lambda b, h: (row_block0 + b, h)),
        out_shape=jax.ShapeDtypeStruct(mixed_prev.shape, BF16),
        input_output_aliases={8: 0},
        scratch_shapes=[pltpu.VMEM((seq, HEAD_W), BF16), pltpu.VMEM((seq, HEAD_W), BF16),
                        pltpu.VMEM((seq, 2 * HEAD_W), BF16)],
        compiler_params=_params("parallel", "parallel"),
        name="latent_neighbourhood_attention",
    )(proj, proj, proj, cache_k, cache_v, bias, na_qn[None, :], na_kn[None, :], mixed_prev)


def _gqa_body(q_ref, k_ref, v_ref, ck_ref, cv_ref, rope_ref, qn_ref, kn_ref, prev_ref, o_ref,
              qs_ref, ks_ref, vs_ref, *, seq, tq):
    del prev_ref
    group = N_HEADS // GQA_KV_HEADS
    half = HEAD_W // 2
    ks_ref[0:seq, :] = _rope(_rms(k_ref[...], kn_ref[...], HEAD_W), rope_ref, half).astype(BF16)
    ks_ref[seq:, :] = ck_ref[...].astype(BF16)
    vs_ref[0:seq, :] = _with_ones(v_ref[...].astype(BF16))
    vs_ref[seq:, :] = _with_ones(cv_ref[...].astype(BF16))
    for g in range(group):
        q = _rms(q_ref[:, g * HEAD_W:(g + 1) * HEAD_W], qn_ref[...], HEAD_W) * (HEAD_W ** -0.5)
        qs_ref[g] = _rope(q, rope_ref, half).astype(BF16)
    kk = ks_ref[...]
    vv = vs_ref[...]
    for g in range(group):
        def q_step(i, carry, g=g):
            qrows = pl.ds(pl.multiple_of(i * tq, tq), tq)
            o = _softmax_pv([_dot_nt(qs_ref[g, qrows, :], kk)], [vv])
            o_ref[qrows, g * HEAD_W:(g + 1) * HEAD_W] = o.astype(o_ref.dtype)
            return carry

        lax.fori_loop(0, seq // tq, q_step, 0, unroll=2)


def _latent_gqa(proj, row_block0, n_seq, seq, cache_k, cache_v, layer, rope, gqa_qn, gqa_kn, mixed_prev):
    past = cache_k.shape[3]
    group = N_HEADS // GQA_KV_HEADS
    tq = 256
    cache = pl.BlockSpec((None, None, None, past, HEAD_W), lambda b, n: (b, layer, n, 0, 0))
    vec = pl.BlockSpec((1, HEAD_W), lambda b, n: (0, 0))
    return pl.pallas_call(
        functools.partial(_gqa_body, seq=seq, tq=tq),
        grid=(n_seq, GQA_KV_HEADS),
        in_specs=[pl.BlockSpec((seq, group * HEAD_W), lambda b, n: (row_block0 + b, COL_C_Q // group + n)),
                  pl.BlockSpec((seq, HEAD_W), lambda b, n: (row_block0 + b, COL_C_K + n)),
                  pl.BlockSpec((seq, HEAD_W), lambda b, n: (row_block0 + b, COL_C_V + n)),
                  cache, cache,
                  pl.BlockSpec((3, seq, HEAD_W), lambda b, n: (0, 0, 0)),
                  vec, vec, ANY_SPEC],
        out_specs=pl.BlockSpec((seq, group * HEAD_W), lambda b, n: (row_block0 + b, n)),
        out_shape=jax.ShapeDtypeStruct(mixed_prev.shape, BF16),
        input_output_aliases={8: 0},
        scratch_shapes=[pltpu.VMEM((group, seq, HEAD_W), BF16),
                        pltpu.VMEM((seq + past, HEAD_W), BF16),
                        pltpu.VMEM((seq + past, 2 * HEAD_W), BF16)],
        compiler_params=_params("parallel", "parallel"),
        name="latent_gqa_attention",
    )(proj, proj, proj, cache_k, cache_v, rope, gqa_qn[None, :], gqa_kn[None, :], mixed_prev)


def _diff_body(q_ref, k_ref, v_ref, ck_ref, cv_ref, rope_ref, qn_ref, kn_ref, sub_ref, lam_ref, prev_ref, o_ref,
               qs_ref, k0_ref, k1_ref, vs_ref, *, seq, tq, lam_init):
    del prev_ref
    half = DF_DQK // 2
    k = _rope(_rms_halves(k_ref[...], kn_ref[...]), rope_ref, half)
    lo = _lane_lt(k.shape, DF_DQK)
    k0_ref[0:seq, :] = jnp.where(lo, k, 0.0).astype(BF16)
    k1_ref[0:seq, :] = jnp.where(lo, 0.0, k).astype(BF16)
    ck = ck_ref[...]
    lo_c = _lane_lt(ck.shape, DF_DQK)
    k0_ref[seq:, :] = jnp.where(lo_c, ck, 0.0).astype(BF16)
    k1_ref[seq:, :] = jnp.where(lo_c, 0.0, ck).astype(BF16)
    vs_ref[0:seq, :] = _with_ones(v_ref[...].astype(BF16))
    vs_ref[seq:, :] = _with_ones(cv_ref[...].astype(BF16))
    q = _rms_halves(q_ref[...], qn_ref[...]) * (DF_DQK ** -0.5)
    qs_ref[...] = _rope(q, rope_ref, half).astype(BF16)
    lam = _lambda(lam_ref, lam_init)
    k0 = k0_ref[...]
    k1 = k1_ref[...]
    vv = vs_ref[...]

    def q_step(i, carry):
        qrows = pl.ds(pl.multiple_of(i * tq, tq), tq)
        qb = qs_ref[qrows, :]
        o = _diff_pv(_dot_nt(qb, k0), _dot_nt(qb, k1), lam, vv)
        o_ref[qrows, :] = (_rms(o, sub_ref[...], HEAD_W) * (1.0 - lam_init)).astype(o_ref.dtype)
        return carry

    lax.fori_loop(0, seq // tq, q_step, 0, unroll=2)


def _latent_diff(proj, row_block0, n_seq, seq, cache_k2, cache_v, layer, rope, df_qn, df_kn, df_subln, df_lam,
                 lam_init, mixed_prev):
    past = cache_k2.shape[3]
    tq = 256

    def col(cb):
        return pl.BlockSpec((seq, HEAD_W), lambda b, h, cb=cb: (row_block0 + b, cb + h))

    cache = pl.BlockSpec((None, None, None, past, HEAD_W), lambda b, h: (b, layer, h, 0, 0))
    vec = pl.BlockSpec((1, HEAD_W), lambda b, h: (0, 0))
    kv_scratch = pltpu.VMEM((seq + past, HEAD_W), BF16)
    return pl.pallas_call(
        functools.partial(_diff_body, seq=seq, tq=tq, lam_init=lam_init),
        grid=(n_seq, N_HEADS),
        in_specs=[col(COL_D_Q), col(COL_D_K), col(COL_D_V), cache, cache,
                  pl.BlockSpec((3, seq, HEAD_W), lambda b, h: (0, 0, 0)),
                  vec, vec, vec, pl.BlockSpec((4, DF_DQK), lambda b, h: (0, 0)), ANY_SPEC],
        out_specs=pl.BlockSpec((seq, HEAD_W), lambda b, h: (row_block0 + b, h)),
        out_shape=jax.ShapeDtypeStruct(mixed_prev.shape, BF16),
        input_output_aliases={10: 0},
        scratch_shapes=[pltpu.VMEM((seq, HEAD_W), BF16), kv_scratch, kv_scratch,
                        pltpu.VMEM((seq + past, 2 * HEAD_W), BF16)],
        compiler_params=_params("parallel", "parallel"),
        name="latent_diff_attention",
    )(proj, proj, proj, cache_k2, cache_v, rope, jnp.tile(df_qn, 2)[None, :], jnp.tile(df_kn, 2)[None, :],
      df_subln[None, :], df_lam, mixed_prev)


def _first_max(vals):
    best = vals[0]
    idx = jnp.zeros(best.shape, jnp.int32)
    for i in range(1, len(vals)):
        better = vals[i] > best
        best = jnp.where(better, vals[i], best)
        idx = jnp.where(better, i, idx)
    return best, idx


def _pick(vals, idx):
    out = vals[0]
    for i in range(1, len(vals)):
        out = jnp.where(idx == i, vals[i], out)
    return out


def _outproj_body(ma_ref, mb_ref, mc_ref, md_ref, w_ref, hc_ref, hl_ref, mod_ref, n2_ref, rw_ref, rb_ref,
                  h1_ref, x2_ref, idx_ref, gate_ref, mixed_ref, *, ctx_tiles):
    for g, m_ref in enumerate((ma_ref, mb_ref, mc_ref, md_ref)):
        mixed_ref[:, g * GROUP_W:(g + 1) * GROUP_W] = m_ref[...]
    y = mod_ref[2:3, :] * _dot(mixed_ref[...], w_ref[...])

    def residual(h_ref):
        h1_ref[...] = h_ref[...] + y

    is_ctx = pl.program_id(0) < ctx_tiles
    pl.when(is_ctx)(lambda: residual(hc_ref))
    pl.when(jnp.logical_not(is_ctx))(lambda: residual(hl_ref))
    h1 = h1_ref[...]
    x2 = _rms(h1, n2_ref[...], D_MODEL) * (1.0 + mod_ref[4:5, :]) + mod_ref[3:4, :]
    x2_ref[...] = _pack_bf16_pairs(x2)
    logits = lax.dot_general(rw_ref[...], x2, (((1,), (1,)), ((), ())), precision=lax.Precision.HIGHEST,
                             preferred_element_type=F32)
    aff_all = _sigmoid(logits)
    sel_all = aff_all + rb_ref[...]
    aff = [aff_all[e:e + 1, :] for e in range(N_EXPERTS)]
    sel = [sel_all[e:e + 1, :] for e in range(N_EXPERTS)]
    neg = jnp.full(sel[0].shape, -jnp.inf, F32)
    scores = []
    for g in range(N_EXP_GROUPS):
        grp = sel[g * EXP_PER_GROUP:(g + 1) * EXP_PER_GROUP]
        m1, i1 = _first_max(grp)
        m2, _ = _first_max([jnp.where(i1 == j, neg, grp[j]) for j in range(EXP_PER_GROUP)])
        scores.append(m1 + m2)
    _, g_best = _first_max(scores)
    in_sel = [_pick([sel[g * EXP_PER_GROUP + j] for g in range(N_EXP_GROUPS)], g_best)
              for j in range(EXP_PER_GROUP)]
    in_aff = [_pick([aff[g * EXP_PER_GROUP + j] for g in range(N_EXP_GROUPS)], g_best)
              for j in range(EXP_PER_GROUP)]
    _, l1 = _first_max(in_sel)
    _, l2 = _first_max([jnp.where(l1 == j, neg, in_sel[j]) for j in range(EXP_PER_GROUP)])
    w1 = _pick(in_aff, l1)
    w2 = _pick(in_aff, l2)
    idx_ref[0:1, :] = g_best * EXP_PER_GROUP + l1
    idx_ref[1:2, :] = g_best * EXP_PER_GROUP + l2
    gate_ref[0:1, :] = w1 / (w1 + w2)
    gate_ref[1:2, :] = w2 / (w1 + w2)


def _output_projection(mixed4, w_out_bf16, h_ctx, h_lat, mod, norm2, router_w, router_b, cond_of_tile, tm):
    d = h_ctx.shape[1]
    t = h_ctx.shape[0] + h_lat.shape[0]
    ctx_tiles = h_ctx.shape[0] // tm
    slab = pl.BlockSpec((tm, GROUP_W), lambda i: (i, 0))
    return pl.pallas_call(
        functools.partial(_outproj_body, ctx_tiles=ctx_tiles),
        grid=(t // tm,),
        in_specs=[slab, slab, slab, slab,
                  pl.BlockSpec((d, d), lambda i: (0, 0))] + _group_specs(tm, d, ctx_tiles) + [
                  pl.BlockSpec((None, 6, d), lambda i: (cond_of_tile(i), 0, 0)),
                  pl.BlockSpec((1, d), lambda i: (0, 0)),
                  pl.BlockSpec((N_EXPERTS, d), lambda i: (0, 0)),
                  pl.BlockSpec((N_EXPERTS, 1), lambda i: (0, 0))],
        out_specs=[pl.BlockSpec((tm, d), lambda i: (i, 0)),
                   pl.BlockSpec((tm, d // 2), lambda i: (i, 0)),
                   pl.BlockSpec((2, tm), lambda i: (0, i)),
                   pl.BlockSpec((2, tm), lambda i: (0, i))],
        out_shape=[jax.ShapeDtypeStruct((t, d), F32), jax.ShapeDtypeStruct((t, d // 2), jnp.uint32),
                   jax.ShapeDtypeStruct((2, t), jnp.int32), jax.ShapeDtypeStruct((2, t), F32)],
        scratch_shapes=[pltpu.VMEM((tm, d), BF16)],
        compiler_params=_params("parallel", vmem=VMEM_LIMIT_PROJ),
        name="out_proj_residual_router",
    )(*mixed4, w_out_bf16, h_ctx, h_lat, mod, norm2[None, :], router_w.T, router_b[:, None])


SUBLANES = 8


def _vmem_row(ref, base, u):
    return ref.at[pl.ds(base, SUBLANES), :].at[pl.ds(u, 1), :]


def _for_row_groups(n_rows, fn):
    def body(g, carry):
        base = pl.multiple_of(g * SUBLANES, SUBLANES)
        for u in range(SUBLANES):
            fn(base, u)
        return carry

    lax.fori_loop(0, n_rows // SUBLANES, body, 0)


def _rows_wait(src_hbm, dst, sem, n_rows):
    pltpu.make_async_copy(src_hbm.at[pl.ds(0, n_rows), :], dst.at[pl.ds(0, n_rows), :], sem).wait()


def _dispatch_body(dest_ref, ps_ref, pe_ref, x_ref, xs_hbm, zero_ref, sem, zsem, *, n_tok, tile):
    i = pl.program_id(0)

    def zero_copy(e):
        first = pl.multiple_of(pe_ref[e] - MOE_ROWS, MOE_ROWS)
        return pltpu.make_async_copy(zero_ref, xs_hbm.at[pl.ds(first, MOE_ROWS), :], zsem)

    @pl.when(i == 0)
    def _():
        zero_ref[...] = jnp.zeros(zero_ref.shape, zero_ref.dtype)
        for e in range(N_EXPERTS):
            @pl.when(pe_ref[e] > ps_ref[e])
            def _(e=e):
                zero_copy(e).start()
        for e in range(N_EXPERTS):
            @pl.when(pe_ref[e] > ps_ref[e])
            def _(e=e):
                zero_copy(e).wait()

    for k in range(2):
        def scatter_row(base, u, k=k):
            row = dest_ref[k * n_tok + i * tile + base + u]
            pltpu.make_async_copy(_vmem_row(x_ref, base, u), xs_hbm.at[pl.ds(row, 1), :], sem).start()

        _for_row_groups(tile, scatter_row)
    for k in range(2):
        _rows_wait(x_ref, xs_hbm, sem, tile)


def _dispatch(x2, dest, pad_start, pad_end, n_rows):
    t, d = x2.shape
    tile = TOKEN_TILE
    grid_spec = pltpu.PrefetchScalarGridSpec(
        num_scalar_prefetch=3,
        grid=(t // tile,),
        in_specs=[pl.BlockSpec((tile, d), lambda i, dst, ps, pe: (i, 0))],
        out_specs=ANY_SPEC,
        scratch_shapes=[pltpu.VMEM((MOE_ROWS, d), x2.dtype), pltpu.SemaphoreType.DMA, pltpu.SemaphoreType.DMA],
    )
    return pl.pallas_call(
        functools.partial(_dispatch_body, n_tok=t, tile=tile),
        grid_spec=grid_spec,
        out_shape=jax.ShapeDtypeStruct((n_rows, d), x2.dtype),
        compiler_params=_params("arbitrary"),
        name="moe_dispatch",
    )(dest, pad_start, pad_end, x2)


def _expert_body(be_ref, nb_ref, nx_ref, x_ref, wg_hbm, wu_hbm, wd_hbm, o_ref,
                 stage_g, stage_u, stage_d, wg_bf, wu_bf, wd_bf, sem, *, layer):
    i = pl.program_id(0)

    def weight_copies(e):
        return (pltpu.make_async_copy(wg_hbm.at[layer, e], stage_g, sem.at[0]),
                pltpu.make_async_copy(wu_hbm.at[layer, e], stage_u, sem.at[1]),
                pltpu.make_async_copy(wd_hbm.at[layer, e], stage_d, sem.at[2]))

    @pl.when(i < nb_ref[0])
    def _():
        e = be_ref[i]

        @pl.when(i == 0)
        def _():
            for c in weight_copies(e):
                c.start()

        @pl.when((i == 0) | (e != be_ref[jnp.maximum(i - 1, 0)]))
        def _():
            for c in weight_copies(e):
                c.wait()
            wg_bf[...] = stage_g[...].astype(BF16)
            wu_bf[...] = stage_u[...].astype(BF16)
            wd_bf[...] = stage_d[...].astype(BF16)
            nxt = nx_ref[e]

            @pl.when(nxt < N_EXPERTS)
            def _():
                for c in weight_copies(nxt):
                    c.start()

        x = _unpack_bf16_pairs(x_ref[...]).astype(BF16)
        hdn = _silu(_dot(x, wg_bf[...])) * _dot(x, wu_bf[...])
        o_ref[...] = _pack_bf16_pairs(_dot(hdn.astype(BF16), wd_bf[...]))

    @pl.when(i >= nb_ref[0])
    def _():
        o_ref[...] = jnp.zeros(o_ref.shape, o_ref.dtype)


def _expert_blocks(xs, block_expert, n_used, next_expert, w_gate, w_up, w_down, layer):
    n_rows, half = xs.shape
    d = 2 * half
    ff = w_gate.shape[-1]
    grid_spec = pltpu.PrefetchScalarGridSpec(
        num_scalar_prefetch=3,
        grid=(n_rows // MOE_ROWS,),
        in_specs=[pl.BlockSpec((MOE_ROWS, half), lambda i, be, nb, nx: (jnp.minimum(i, nb[0] - 1), 0)),
                  ANY_SPEC, ANY_SPEC, ANY_SPEC],
        out_specs=pl.BlockSpec((MOE_ROWS, half), lambda i, be, nb, nx: (i, 0)),
        scratch_shapes=[pltpu.VMEM((d, ff), F32), pltpu.VMEM((d, ff), F32), pltpu.VMEM((ff, d), F32),
                        pltpu.VMEM((d, ff), BF16), pltpu.VMEM((d, ff), BF16), pltpu.VMEM((ff, d), BF16),
                        pltpu.SemaphoreType.DMA((3,))],
    )
    return pl.pallas_call(
        functools.partial(_expert_body, layer=layer),
        grid_spec=grid_spec,
        out_shape=jax.ShapeDtypeStruct((n_rows, half), jnp.uint32),
        compiler_params=_params("arbitrary"),
        name="moe_expert_blocks",
    )(block_expert, n_used, next_expert, xs, w_gate, w_up, w_down)


def _combine_body(dest_ref, h_ref, gate_ref, y_hbm, mod_ref, *rest, n_tok, tile, ctx_tiles):
    out_refs, (ybuf, sem) = rest[:-2], rest[-2:]
    i = pl.program_id(0)
    n_tiles = pl.num_programs(0)

    def start(blk, slot):
        for k in range(2):
            def gather_row(base, u, k=k):
                row = dest_ref[k * n_tok + blk * tile + base + u]
                pltpu.make_async_copy(y_hbm.at[pl.ds(row, 1), :], _vmem_row(ybuf.at[slot, k], base, u),
                                      sem.at[slot]).start()

            _for_row_groups(tile, gather_row)

    @pl.when(i == 0)
    def _():
        start(0, 0)

    @pl.when(i + 1 < n_tiles)
    def _():
        start(i + 1, (i + 1) % 2)

    slot = i % 2
    for k in range(2):
        _rows_wait(y_hbm, ybuf.at[slot, k], sem.at[slot], tile)
    gate = gate_ref[...]
    y0 = _unpack_bf16_pairs(ybuf[slot, 0])
    y1 = _unpack_bf16_pairs(ybuf[slot, 1])
    out = h_ref[...] + mod_ref[5:6, :] * (gate[:, 0:1] * y0 + gate[:, 1:2] * y1)
    @pl.when(i < ctx_tiles)
    def _():
        out_refs[0][...] = out

    @pl.when(i >= ctx_tiles)
    def _():
        out_refs[1][...] = out


def _combine(h1, yb, dest, gates, mod, cond_of_tile, tile, split_rows):
    t, d = h1.shape
    row_tile = pl.BlockSpec((tile, d), lambda i, dst: (i, 0))
    ctx_tiles = split_rows // tile
    out_specs = _group_specs(tile, d, ctx_tiles)
    out_shape = [jax.ShapeDtypeStruct((split_rows, d), F32), jax.ShapeDtypeStruct((t - split_rows, d), F32)]
    grid_spec = pltpu.PrefetchScalarGridSpec(
        num_scalar_prefetch=1,
        grid=(t // tile,),
        in_specs=[row_tile,
                  pl.BlockSpec((tile, 2), lambda i, dst: (i, 0)),
                  ANY_SPEC,
                  pl.BlockSpec((None, 6, d), lambda i, dst: (cond_of_tile(i), 0, 0))],
        out_specs=out_specs,
        scratch_shapes=[pltpu.VMEM((2, 2, tile, yb.shape[1]), yb.dtype), pltpu.SemaphoreType.DMA((2,))],
    )
    return pl.pallas_call(
        functools.partial(_combine_body, n_tok=t, tile=tile, ctx_tiles=ctx_tiles),
        grid_spec=grid_spec,
        out_shape=out_shape,
        compiler_params=_params("arbitrary"),
        name="moe_gated_residual",
    )(dest, h1, gates, yb, mod)


def _moe(h1, x2, idx_t, gate_t, mod, w_gate, w_up, w_down, layer, cond_of_tile, split_rows):
    t, d = h1.shape
    n = 2 * t
    experts = idx_t.reshape(n)
    onehot = (experts[:, None] == jnp.arange(N_EXPERTS, dtype=jnp.int32)[None, :]).astype(BF16)
    blocks = onehot.reshape(n // TOKEN_TILE, TOKEN_TILE, N_EXPERTS)
    tri = jnp.asarray(np.tril(np.ones((TOKEN_TILE, TOKEN_TILE), np.float32)), BF16)
    within = jnp.einsum('ij,bjk->bik', tri, blocks, preferred_element_type=F32)
    block_total = within[:, -1, :]
    block_first = jnp.cumsum(block_total, axis=0) - block_total
    counts = (block_first[-1] + block_total[-1]).astype(jnp.int32)
    before = (within + block_first[:, None, :]).reshape(n, N_EXPERTS) - 1.0
    rank = jnp.sum(before * onehot.astype(F32), axis=1).astype(jnp.int32)
    padded = (counts + MOE_ROWS - 1) // MOE_ROWS * MOE_ROWS
    pad_end = jnp.cumsum(padded).astype(jnp.int32)
    pad_start = pad_end - padded
    dest = (pad_start[experts] + rank).astype(jnp.int32)
    n_blocks = (n + N_EXPERTS * (MOE_ROWS - 1) + MOE_ROWS - 1) // MOE_ROWS
    block_first_row = jnp.arange(n_blocks, dtype=jnp.int32) * MOE_ROWS
    block_expert = jnp.minimum(jnp.sum((pad_end[None, :] <= block_first_row[:, None]).astype(jnp.int32), axis=1),
                               N_EXPERTS - 1)
    n_used = (pad_end[-1:] // MOE_ROWS).astype(jnp.int32)
    xs = _dispatch(x2, dest, pad_start, pad_end, n_blocks * MOE_ROWS)
    ids = jnp.arange(N_EXPERTS, dtype=jnp.int32)
    later_with_rows = (counts[None, :] > 0) & (ids[None, :] > ids[:, None])
    next_expert = jnp.min(jnp.where(later_with_rows, ids[None, :], N_EXPERTS), axis=1).astype(jnp.int32)
    yb = _expert_blocks(xs, block_expert, n_used, next_expert, w_gate, w_up, w_down, layer)
    return _combine(h1, yb, dest, gate_t.T, mod, cond_of_tile, TOKEN_TILE, split_rows)


def kernel(x_prompt, x_sample, cache_na_k, cache_na_v, cache_gqa_k, cache_gqa_v, cache_diff_k, cache_diff_v, state_hgrn, c, c_ctx, w_mod, b_mod, norm1, norm2, w_in, w_out, hg_lb_logits, hg_onorm, na_qn, na_kn, na_rpb, gqa_qn, gqa_kn, df_qn, df_kn, df_lam, df_subln, router_w, router_b, w_gate, w_up, w_down):
    n_ctx, ctx_len, d = x_prompt.shape
    n_lat, lat_len, _ = x_sample.shape
    depth = w_in.shape[0]
    t_ctx = n_ctx * ctx_len
    assert t_ctx % lat_len == 0 and lat_len % GRID_W == 0 and lat_len // GRID_W >= WIN_ROWS
    tm = next(m for m in (1024, 512, 256) if t_ctx % m == 0 and lat_len % m == 0)
    tm2 = min(tm, 512)
    lat_block0 = t_ctx // lat_len

    def cond_tile(tile_rows):
        def cond_of_tile(i):
            return jnp.where(i < t_ctx // tile_rows, 0, 1 + (i - t_ctx // tile_rows) // (lat_len // tile_rows))
        return cond_of_tile

    sm = jax.nn.softmax(hg_lb_logits.astype(F32), axis=0)
    lower = jnp.cumsum(sm, axis=0) - sm[0:1]
    mod_all = _modulation(jnp.concatenate([c_ctx[None, :], c], axis=0), w_mod, b_mod)
    mod_all = mod_all.reshape(depth, 1 + n_lat, 6, d)
    hgrn_consts = _hgrn_constants(HGRN_CHUNK)
    rope_c = _rope_tables(lat_len, HEAD_W)
    rope_d = _rope_tables(lat_len, DF_DQK)
    past = cache_diff_k.shape[4]
    cache_diff_k2 = cache_diff_k.transpose(0, 1, 2, 4, 3, 5).reshape(n_lat, depth, N_HEADS, past, HEAD_W)

    h_ctx, h_lat = x_prompt.reshape(t_ctx, d), x_sample.reshape(n_lat * lat_len, d)
    caches, states = [], None
    for layer in range(depth):
        mod = mod_all[layer]
        lam_init = 0.8 - 0.6 * math.exp(-0.3 * layer)
        proj = _input_projection(h_ctx, h_lat, mod, norm1[layer], w_in[layer].astype(BF16), cond_tile(tm), tm)
        mix_a, states = _hgrn(proj, 0, n_ctx, ctx_len, lower[layer], hg_onorm[layer], hgrn_consts, None, None,
                              states, layer, depth)
        gains = (na_qn[layer], na_kn[layer], gqa_qn[layer], gqa_kn[layer], df_qn[layer], df_kn[layer],
                 df_subln[layer], df_lam[layer])
        mix_b, mix_c, mix_d, *caches = _context_attention(proj, n_ctx, ctx_len, gains, lam_init, layer, depth,
                                                          caches)
        mix_a, _ = _hgrn(proj, lat_block0, n_lat, lat_len, lower[layer], hg_onorm[layer], hgrn_consts,
                         state_hgrn, mix_a, None, layer, depth)
        mix_b = _latent_na(proj, lat_block0, n_lat, lat_len, cache_na_k, cache_na_v, layer,
                           _na_bias(na_rpb[layer]), na_qn[layer], na_kn[layer], mix_b)
        mix_c = _latent_gqa(proj, lat_block0, n_lat, lat_len, cache_gqa_k, cache_gqa_v, layer, rope_c,
                            gqa_qn[layer], gqa_kn[layer], mix_c)
        mix_d = _latent_diff(proj, lat_block0, n_lat, lat_len, cache_diff_k2, cache_diff_v, layer, rope_d,
                             df_qn[layer], df_kn[layer], df_subln[layer], df_lam[layer], lam_init, mix_d)
        h1, x2, idx_t, gate_t = _output_projection((mix_a, mix_b, mix_c, mix_d), w_out[layer].astype(BF16), h_ctx,
                                                   h_lat, mod, norm2[layer], router_w, router_b, cond_tile(tm2), tm2)
        h_ctx, h_lat = _moe(h1, x2, idx_t, gate_t, mod, w_gate, w_up, w_down, layer, cond_tile(TOKEN_TILE), t_ctx)
    y_prompt = h_ctx.reshape(n_ctx, ctx_len, d)
    y_sample = h_lat.reshape(n_lat, lat_len, d)
    return (y_prompt, y_sample, *caches, states)
```

```python
import functools
import math

import numpy as np
import jax
import jax.numpy as jnp
from jax import lax
from jax.experimental import pallas as pl
from jax.experimental.pallas import tpu as pltpu

D_MODEL = 2048
GRID_W = 64
GROUP_W = D_MODEL // 4
N_HEADS = 4
HEAD_W = GROUP_W // N_HEADS
SUBLANES = 8
GQA_KV_HEADS = 2
DF_DQK = HEAD_W // 2
WIN_ROWS = 8
WIN_COLS = 16
N_EXPERTS = 16
N_EXP_GROUPS = 4
EXP_PER_GROUP = N_EXPERTS // N_EXP_GROUPS
EXPERT_FF = D_MODEL // 4
ROPE_THETA = 10000.0
EPS = 1e-6
NEG_INF = -1e30
IN_WIDTH = 13 * GROUP_W

COL_A_Q, COL_A_FF, COL_A_FB, COL_A_I, COL_A_G = 0, 4, 8, 12, 16
COL_B_Q, COL_B_K, COL_B_V = 20, 24, 28
COL_C_Q, COL_C_K, COL_C_V = 32, 36, 38
COL_D_Q, COL_D_K, COL_D_V = 40, 44, 48

HGRN_CHUNK = 128
HGRN_UNROLL = 2
HGRN_INPUT_VMEM = 24 * 1024 * 1024
MOE_ROWS = 256
TOKEN_TILE = 256
VMEM_LIMIT = 48 * 1024 * 1024
VMEM_LIMIT_PROJ = 56 * 1024 * 1024

F32 = jnp.float32
BF16 = jnp.bfloat16
ANY_SPEC = pl.BlockSpec(memory_space=pl.ANY)


def _params(*sem, vmem=VMEM_LIMIT):
    return pltpu.CompilerParams(dimension_semantics=sem, vmem_limit_bytes=vmem)


def _sigmoid(x):
    return 1.0 / (1.0 + jnp.exp(-x))


def _silu(x):
    return x * _sigmoid(x)


def _rms(x, gain, n):
    return x * lax.rsqrt(jnp.sum(x * x, axis=-1, keepdims=True) * (1.0 / n) + EPS) * gain


def _dot(a, b):
    return jnp.dot(a, b, preferred_element_type=F32)


def _dot_nt(a, b):
    return lax.dot_general(a, b, (((1,), (1,)), ((), ())), preferred_element_type=F32)


def _dot_tn(a, b):
    return lax.dot_general(a, b, (((0,), (0,)), ((), ())), preferred_element_type=F32)


def _pack_bf16_pairs(x):
    k = x.shape[1] // 2
    lo = lax.bitcast_convert_type(x[:, :k].astype(BF16).astype(F32), jnp.uint32) >> 16
    hi = lax.bitcast_convert_type(x[:, k:].astype(BF16).astype(F32), jnp.uint32)
    return hi | lo


def _unpack_bf16_pairs(w):
    lo = lax.bitcast_convert_type(w << 16, F32)
    hi = lax.bitcast_convert_type(w & jnp.uint32(0xFFFF0000), F32)
    return jnp.concatenate([lo, hi], axis=1)


def _aligned(x, m):
    return x if isinstance(x, int) else pl.multiple_of(x, m)


def _alias_kwargs(n_inputs, prev, first_out):
    return ([ANY_SPEC] * len(prev), list(prev), {n_inputs + k: first_out + k for k in range(len(prev))})


def _mod_body(cond_ref, w_ref, b_ref, o_ref):
    w = w_ref[...]
    for c in range(cond_ref.shape[0]):
        s = _silu(cond_ref[c])
        o_ref[c:c + 1, :] = jnp.sum(w * s, axis=0, keepdims=True) + b_ref[...]


def _modulation(cond, w_mod, b_mod):
    depth, d, n6 = w_mod.shape
    nc = cond.shape[0]
    tn = 1024
    return pl.pallas_call(
        _mod_body,
        grid=(depth, n6 // tn),
        in_specs=[pl.BlockSpec((nc, d, 1), lambda l, j: (0, 0, 0)),
                  pl.BlockSpec((None, d, tn), lambda l, j: (l, 0, j)),
                  pl.BlockSpec((None, 1, tn), lambda l, j: (l, 0, j))],
        out_specs=pl.BlockSpec((None, nc, tn), lambda l, j: (l, 0, j)),
        out_shape=jax.ShapeDtypeStruct((depth, nc, n6), F32),
        compiler_params=_params("parallel", "parallel"),
        name="adaln_modulation",
    )(cond[:, :, None], w_mod, b_mod[:, None, :])


def _group_specs(tile, d, ctx_tiles):
    return [pl.BlockSpec((tile, d), lambda i, *_: (jnp.minimum(i, ctx_tiles - 1), 0)),
            pl.BlockSpec((tile, d), lambda i, *_: (jnp.maximum(i - ctx_tiles, 0), 0))]


def _inproj_body(hc_ref, hl_ref, mod_ref, n1_ref, w_ref, o_ref, xn_ref, *, ctx_tiles):
    def normalise(h_ref):
        y = _rms(h_ref[...], n1_ref[...], D_MODEL)
        xn_ref[...] = (y * (1.0 + mod_ref[1:2, :]) + mod_ref[0:1, :]).astype(BF16)

    first = pl.program_id(1) == 0
    is_ctx = pl.program_id(0) < ctx_tiles
    pl.when(first & is_ctx)(lambda: normalise(hc_ref))
    pl.when(first & jnp.logical_not(is_ctx))(lambda: normalise(hl_ref))
    o_ref[...] = _dot(xn_ref[...], w_ref[...])


def _input_projection(h_ctx, h_lat, mod, norm1, w_in_bf16, cond_of_tile, tm):
    d = h_ctx.shape[1]
    t = h_ctx.shape[0] + h_lat.shape[0]
    ctx_tiles = h_ctx.shape[0] // tm
    n = w_in_bf16.shape[1]
    tn = 512
    return pl.pallas_call(
        functools.partial(_inproj_body, ctx_tiles=ctx_tiles),
        grid=(t // tm, n // tn),
        in_specs=_group_specs(tm, d, ctx_tiles) + [
                  pl.BlockSpec((None, 6, d), lambda i, j: (cond_of_tile(i), 0, 0)),
                  pl.BlockSpec((1, d), lambda i, j: (0, 0)),
                  pl.BlockSpec((d, tn), lambda i, j: (0, j))],
        out_specs=pl.BlockSpec((tm, tn), lambda i, j: (i, j)),
        out_shape=jax.ShapeDtypeStruct((t, n), F32),
        scratch_shapes=[pltpu.VMEM((tm, d), BF16)],
        compiler_params=_params("parallel", "arbitrary", vmem=VMEM_LIMIT_PROJ),
        name="norm_modulate_in_proj",
    )(h_ctx, h_lat, mod, norm1[None, :], w_in_bf16)


def _hgrn_constants(c):
    nl = int(math.log2(c))
    idx = np.arange(c)
    e = np.zeros((nl + 2, c, c), np.float32)
    m = np.zeros((nl + 1, c, c), np.float32)
    e[0] = idx[None, :] <= idx[:, None]
    e[1] = idx[None, :] > idx[:, None]
    m[0] = np.eye(c)
    for li in range(nl):
        s = c >> (li + 1)
        parent = idx // (2 * s)
        right = (idx % (2 * s)) >= s
        ref = parent * 2 * s + s - 1
        for i in range(c):
            if right[i]:
                e[2 + li, i, ref[i] + 1:i + 1] = 1.0
            else:
                e[2 + li, i, i + 1:ref[i] + 1] = 1.0
        m[1 + li] = right[:, None] & ~right[None, :] & (parent[:, None] == parent[None, :])
    keep = [0, 1] + [2 + li for li in range(nl) if (c >> (li + 1)) < SUBLANES]
    e = e[keep]
    e2 = np.stack([e, e[:, ::-1, ::-1]]).reshape(2, len(keep) * c, c)
    m2 = np.stack([m, m[:, ::-1, ::-1]])
    return jnp.asarray(e2, BF16), jnp.asarray(m2, F32)


def _hgrn_body(*refs, seq, chunk, unroll, heads, has_s0, emit_state, n_alias):
    q_ref, ff_ref, fb_ref, i_ref, g_ref, lb_ref, on_ref, e_ref, m_ref = refs[:9]
    pos = 9
    s0_ref = None
    if has_s0:
        s0_ref = refs[pos]
        pos += 1
    pos += n_alias
    o_ref = refs[pos]
    pos += 1
    if emit_state:
        st_ref = refs[pos]
        pos += 1
    of_ref, ob_ref = refs[pos], refs[pos + 1]
    c = chunk
    n_chunks = seq // c
    assert seq % c == 0 and n_chunks % unroll == 0
    n_levels = m_ref.shape[1] - 1
    gate_refs = (ff_ref, fb_ref)
    out_refs = (of_ref, ob_ref)

    def chunk_step(c0, d, hh, st):
        rows = pl.ds(c0, c)
        lanes = slice(hh * HEAD_W, (hh + 1) * HEAD_W)
        lb = lb_ref[d:d + 1, lanes]
        f = lb + (1.0 - lb) * _sigmoid(gate_refs[d][rows, lanes])
        g = jnp.log(f)
        k = 1.0 - f
        q = _silu(q_ref[rows, lanes])
        v = i_ref[rows, lanes].astype(BF16)
        g_hi = g.astype(BF16)
        g_lo = (g - g_hi.astype(F32)).astype(BF16)
        g2 = _dot(e_ref[d], jnp.concatenate([g_hi, g_lo], axis=1))
        gsum = g2[:, 0:HEAD_W] + g2[:, HEAD_W:2 * HEAD_W]
        cum = gsum[0:c]
        x_cum = jnp.exp(cum)
        x_tail = jnp.exp(gsum[c:2 * c])
        row = lax.broadcasted_iota(jnp.int32, (c, HEAD_W), 0)
        s = m_ref[d, 0] * _dot_nt(q.astype(BF16), k.astype(BF16))
        n_matmul_levels = 0
        for lv in range(n_levels):
            half = c >> (lv + 1)
            if half >= SUBLANES:
                ref_row = half - 1 if d == 0 else half
                ref = jnp.concatenate(
                    [jnp.broadcast_to(cum[b * 2 * half + ref_row:b * 2 * half + ref_row + 1, :], (2 * half, HEAD_W))
                     for b in range(c // (2 * half))], axis=0)
                near = ((row % (2 * half)) >= half) == (d == 0)
                x_l = jnp.exp(jnp.where(near, cum - ref, ref - cum))
            else:
                x_l = jnp.exp(gsum[(2 + n_matmul_levels) * c:(3 + n_matmul_levels) * c])
                n_matmul_levels += 1
            s = s + m_ref[d, 1 + lv] * _dot_nt((q * x_l).astype(BF16), (k * x_l).astype(BF16))
        o = _dot_nt((q * x_cum).astype(BF16), st.astype(BF16)) + _dot(s.astype(BF16), v)
        out_refs[d][rows, lanes] = o
        total = x_cum[c - 1:c, :] if d == 0 else x_cum[0:1, :]
        return st * total + _dot_tn(v, (k * x_tail).astype(BF16))

    if has_s0:
        states0 = tuple(s0_ref[d, hh].T for hh in range(heads) for d in range(2))
    else:
        states0 = tuple(jnp.zeros((HEAD_W, HEAD_W), F32) for _ in range(2 * heads))

    def loop(t, states):
        states = list(states)
        for u in range(unroll):
            j = t * unroll + u
            for hh in range(heads):
                states[2 * hh] = chunk_step(_aligned(j * c, c), 0, hh, states[2 * hh])
                states[2 * hh + 1] = chunk_step(_aligned((n_chunks - 1 - j) * c, c), 1, hh, states[2 * hh + 1])
        return tuple(states)

    if n_chunks == unroll:
        states = loop(0, states0)
    else:
        states = lax.fori_loop(0, n_chunks // unroll, loop, states0)
    for hh in range(heads):
        lanes = slice(hh * HEAD_W, (hh + 1) * HEAD_W)
        o = of_ref[:, lanes] + ob_ref[:, lanes]
        o_ref[:, lanes] = (_rms(o, on_ref[...], HEAD_W) * _silu(g_ref[:, lanes])).astype(o_ref.dtype)
        if emit_state:
            st_ref[0, hh] = states[2 * hh].T
            st_ref[1, hh] = states[2 * hh + 1].T


def _hgrn(proj, row_block0, n_seq, seq, lower, onorm, consts, s0, mixed_prev, state_prev, layer, depth):
    e_mat, masks = consts
    latent = s0 is not None
    hps = next(n for n in (4, 2, 1) if 5 * 2 * seq * n * HEAD_W * 4 <= HGRN_INPUT_VMEM)
    width = hps * HEAD_W

    def col(cb):
        return pl.BlockSpec((seq, width), lambda b, h, cb=cb: (row_block0 + b, cb // hps + h))

    state_spec = pl.BlockSpec((None, None, 2, hps, HEAD_W, HEAD_W), lambda b, h: (b, layer, 0, h, 0, 0))
    in_specs = [col(COL_A_Q), col(COL_A_FF), col(COL_A_FB), col(COL_A_I), col(COL_A_G),
                pl.BlockSpec((2, width), lambda b, h: (0, h)),
                pl.BlockSpec((1, HEAD_W), lambda b, h: (0, 0)),
                pl.BlockSpec(e_mat.shape, lambda b, h: (0, 0, 0)),
                pl.BlockSpec(masks.shape, lambda b, h: (0, 0, 0, 0))]
    args = [proj, proj, proj, proj, proj, lower, onorm[None, :], e_mat, masks]
    if latent:
        in_specs.append(state_spec)
        args.append(s0)
        prev = [mixed_prev]
    else:
        prev = [] if state_prev is None else [state_prev]
    alias_specs, alias_args, aliases = _alias_kwargs(len(args), prev, 0 if latent else 1)
    out_specs = [pl.BlockSpec((seq, width), lambda b, h: (row_block0 + b, h))]
    out_shape = [jax.ShapeDtypeStruct((proj.shape[0], GROUP_W), BF16)]
    if not latent:
        out_specs.append(state_spec)
        out_shape.append(jax.ShapeDtypeStruct((n_seq, depth, 2, N_HEADS, HEAD_W, HEAD_W), F32))
    res = pl.pallas_call(
        functools.partial(_hgrn_body, seq=seq, chunk=HGRN_CHUNK, unroll=HGRN_UNROLL, heads=hps, has_s0=latent,
                          emit_state=not latent, n_alias=len(prev)),
        grid=(n_seq, N_HEADS // hps),
        in_specs=in_specs + alias_specs, out_specs=out_specs, out_shape=out_shape,
        input_output_aliases=aliases,
        scratch_shapes=[pltpu.VMEM((seq, width), F32), pltpu.VMEM((seq, width), F32)],
        compiler_params=_params("parallel", "parallel"),
        name="hgrn2_latent" if latent else "hgrn2_context",
    )(*args, *alias_args)
    return (res[0], None) if latent else res


def _with_ones(v):
    return jnp.concatenate([v, jnp.ones_like(v)], axis=1)


def _softmax_pv(scores, values1):
    mx = functools.reduce(jnp.maximum, [jnp.max(s, axis=-1, keepdims=True) for s in scores])
    acc = functools.reduce(lambda a, b: a + b,
                           [_dot(jnp.exp(s - mx).astype(BF16), v) for s, v in zip(scores, values1)])
    return acc[:, 0:HEAD_W] / acc[:, HEAD_W:HEAD_W + 1]


def _diff_pv(s0, s1, lam, values1):
    return _softmax_pv([s0], [values1]) - lam * _softmax_pv([s1], [values1])


def _lane_lt(shape, n):
    return lax.broadcasted_iota(jnp.int32, shape, len(shape) - 1) < n


def _rms_head(x, gain):
    return _rms(x, gain, HEAD_W)


def _rms_halves(x, gain2):
    lo = _lane_lt(x.shape, DF_DQK)
    sq = x * x
    ss_lo = jnp.sum(jnp.where(lo, sq, 0.0), axis=-1, keepdims=True)
    ss_hi = jnp.sum(sq, axis=-1, keepdims=True) - ss_lo
    inv = jnp.where(lo, lax.rsqrt(ss_lo * (1.0 / DF_DQK) + EPS), lax.rsqrt(ss_hi * (1.0 / DF_DQK) + EPS))
    return x * inv * gain2


def _lambda(lam_ref, lam_init):
    l = lam_ref[...]
    return (jnp.exp(jnp.sum(l[0:1] * l[1:2], axis=-1, keepdims=True))
            - jnp.exp(jnp.sum(l[2:3] * l[3:4], axis=-1, keepdims=True)) + lam_init)


N_CTX_ATTN_INPUTS = 17


def _ctx_attn_body(*refs, lam_init):
    (bq_ref, bk_ref, bv_ref, cq_ref, ck_ref, cv_ref, dq_ref, dk_ref, dv_ref,
     naq_ref, nak_ref, gq_ref, gk_ref, dfq_ref, dfk_ref, sub_ref, lam_ref) = refs[:N_CTX_ATTN_INPUTS]
    ob_ref, oc_ref, od_ref, kb_ref, vb_ref, kc_ref, vc_ref, kd_ref, vd_ref = refs[-9:]
    group = N_HEADS // GQA_KV_HEADS
    scale = HEAD_W ** -0.5
    lam = _lambda(lam_ref, lam_init)
    kc = _rms_head(ck_ref[...], gk_ref[...])
    vc = cv_ref[...]
    kc_ref[...] = kc
    vc_ref[...] = vc
    kc16 = kc.astype(BF16)
    vc1 = _with_ones(vc.astype(BF16))
    for g in range(group):
        lanes = slice(g * HEAD_W, (g + 1) * HEAD_W)
        qc = (_rms_head(cq_ref[:, lanes], gq_ref[...]) * scale).astype(BF16)
        oc_ref[:, lanes] = _softmax_pv([_dot_nt(qc, kc16)], [vc1]).astype(oc_ref.dtype)
        kb = _rms_head(bk_ref[:, lanes], nak_ref[...])
        vb = bv_ref[:, lanes]
        kb_ref[g] = kb
        vb_ref[g] = vb
        qb = (_rms_head(bq_ref[:, lanes], naq_ref[...]) * scale).astype(BF16)
        ob_ref[:, lanes] = _softmax_pv([_dot_nt(qb, kb.astype(BF16))],
                                       [_with_ones(vb.astype(BF16))]).astype(ob_ref.dtype)
        kd = _rms_halves(dk_ref[:, lanes], dfk_ref[...])
        vd = dv_ref[:, lanes]
        kd_ref[g, 0] = kd[:, 0:DF_DQK]
        kd_ref[g, 1] = kd[:, DF_DQK:2 * DF_DQK]
        vd_ref[g] = vd
        qd = (_rms_halves(dq_ref[:, lanes], dfq_ref[...]) * (DF_DQK ** -0.5)).astype(BF16)
        lo = _lane_lt(kd.shape, DF_DQK)
        s0 = _dot_nt(qd, jnp.where(lo, kd, 0.0).astype(BF16))
        s1 = _dot_nt(qd, jnp.where(lo, 0.0, kd).astype(BF16))
        od = _diff_pv(s0, s1, lam, _with_ones(vd.astype(BF16)))
        od_ref[:, lanes] = (_rms_head(od, sub_ref[...]) * (1.0 - lam_init)).astype(od_ref.dtype)


def _context_attention(proj, n_seq, seq, gains, lam_init, layer, depth, caches_prev):
    na_qn, na_kn, gqa_qn, gqa_kn, df_qn, df_kn, df_subln, df_lam = gains
    group = N_HEADS // GQA_KV_HEADS
    width = group * HEAD_W

    def heads(cb):
        return pl.BlockSpec((seq, width), lambda b, n, cb=cb: (b, cb // group + n))

    def kv_head(cb):
        return pl.BlockSpec((seq, HEAD_W), lambda b, n, cb=cb: (b, cb + n))

    vec = pl.BlockSpec((1, HEAD_W), lambda b, n: (0, 0))
    cache_heads = pl.BlockSpec((None, None, group, seq, HEAD_W), lambda b, n: (b, layer, n, 0, 0))
    cache_kv = pl.BlockSpec((None, None, None, seq, HEAD_W), lambda b, n: (b, layer, n, 0, 0))
    mixed = pl.BlockSpec((seq, width), lambda b, n: (b, n))
    mixed_shape = jax.ShapeDtypeStruct((proj.shape[0], GROUP_W), BF16)
    cache4 = jax.ShapeDtypeStruct((n_seq, depth, N_HEADS, seq, HEAD_W), F32)
    cache2 = jax.ShapeDtypeStruct((n_seq, depth, GQA_KV_HEADS, seq, HEAD_W), F32)
    cache_dk = jax.ShapeDtypeStruct((n_seq, depth, N_HEADS, 2, seq, DF_DQK), F32)
    args = [proj] * 9 + [na_qn[None, :], na_kn[None, :], gqa_qn[None, :], gqa_kn[None, :],
                         jnp.tile(df_qn, 2)[None, :], jnp.tile(df_kn, 2)[None, :], df_subln[None, :], df_lam]
    assert len(args) == N_CTX_ATTN_INPUTS
    alias_specs, alias_args, aliases = _alias_kwargs(len(args), caches_prev, 3)
    return pl.pallas_call(
        functools.partial(_ctx_attn_body, lam_init=lam_init),
        grid=(n_seq, GQA_KV_HEADS),
        in_specs=[heads(COL_B_Q), heads(COL_B_K), heads(COL_B_V),
                  heads(COL_C_Q), kv_head(COL_C_K), kv_head(COL_C_V),
                  heads(COL_D_Q), heads(COL_D_K), heads(COL_D_V),
                  vec, vec, vec, vec, vec, vec, vec,
                  pl.BlockSpec((4, DF_DQK), lambda b, n: (0, 0))] + alias_specs,
        out_specs=[mixed, mixed, mixed, cache_heads, cache_heads, cache_kv, cache_kv,
                   pl.BlockSpec((None, None, group, 2, seq, DF_DQK), lambda b, n: (b, layer, n, 0, 0, 0)),
                   cache_heads],
        out_shape=[mixed_shape, mixed_shape, mixed_shape, cache4, cache4, cache2, cache2, cache_dk, cache4],
        input_output_aliases=aliases,
        compiler_params=_params("parallel", "parallel"),
        name="context_attention",
    )(*args, *alias_args)


def _rope_tables(n_tokens, rot_dim):
    t = np.arange(n_tokens)
    row = (t // GRID_W).astype(np.float32)
    col = (t % GRID_W).astype(np.float32)
    n_freq = rot_dim // 4
    inv = (np.float32(ROPE_THETA) ** (-np.arange(n_freq, dtype=np.float32) / np.float32(n_freq))).astype(np.float32)
    ang = np.concatenate([row[:, None] * inv, col[:, None] * inv], axis=-1).astype(np.float32)
    cos, sin, zero = np.cos(ang), np.sin(ang), np.zeros_like(ang)
    reps = HEAD_W // rot_dim
    a = np.tile(np.concatenate([cos, cos], axis=-1), (1, reps))
    b = np.tile(np.concatenate([-sin, zero], axis=-1), (1, reps))
    c = np.tile(np.concatenate([zero, sin], axis=-1), (1, reps))
    return jnp.asarray(np.stack([a, b, c]), F32)


def _rope(x, tab_ref, half):
    return (x * tab_ref[0] + pltpu.roll(x, HEAD_W - half, 1) * tab_ref[1]
            + pltpu.roll(x, half, 1) * tab_ref[2])


def _na_body(q_ref, k_ref, v_ref, ck_ref, cv_ref, bias_ref, qn_ref, kn_ref, prev_ref, o_ref,
             qs_ref, ks_ref, vs_ref, *, seq):
    del prev_ref
    rows = seq // GRID_W
    n_win = WIN_ROWS * GRID_W
    qs_ref[...] = (_rms_head(q_ref[...], qn_ref[...]) * (HEAD_W ** -0.5)).astype(BF16)
    ks_ref[...] = _rms_head(k_ref[...], kn_ref[...]).astype(BF16)
    vs_ref[...] = _with_ones(v_ref[...].astype(BF16))
    ck = ck_ref[...].astype(BF16)
    cv = _with_ones(cv_ref[...].astype(BF16))

    def row_step(r, carry):
        start = jnp.clip(r - WIN_ROWS // 2, 0, rows - WIN_ROWS)
        win = pl.ds(pl.multiple_of(start * GRID_W, GRID_W), n_win)
        qrows = pl.ds(pl.multiple_of(r * GRID_W, GRID_W), GRID_W)
        q = qs_ref[qrows, :]
        s_win = _dot_nt(q, ks_ref[win, :]) + bias_ref[start - r + (WIN_ROWS - 1)]
        s_ctx = _dot_nt(q, ck)
        o_ref[qrows, :] = _softmax_pv([s_win, s_ctx], [vs_ref[win, :], cv]).astype(o_ref.dtype)
        return carry

    lax.fori_loop(0, rows, row_step, 0, unroll=2)


def _na_bias(rpb):
    col = np.arange(GRID_W)
    col_start = np.clip(col - WIN_COLS // 2, 0, GRID_W - WIN_COLS)
    col_ok = (col[None, :] >= col_start[:, None]) & (col[None, :] < col_start[:, None] + WIN_COLS)
    dc = np.clip(col[None, :] - col[:, None] + WIN_COLS - 1, 0, 2 * WIN_COLS - 2).reshape(-1)
    onehot = (np.arange(2 * WIN_COLS - 1)[:, None] == dc[None, :]).astype(np.float32)
    per_dr = jnp.einsum('hdc,cn->hdn', rpb.astype(F32), jnp.asarray(onehot), precision=lax.Precision.HIGHEST)
    per_dr = jnp.where(col_ok[None, None], per_dr.reshape(rpb.shape[0], -1, GRID_W, GRID_W), NEG_INF)
    wins = jnp.stack([per_dr[:, o:o + WIN_ROWS] for o in range(WIN_ROWS)], axis=1)
    return wins.transpose(0, 1, 3, 2, 4).reshape(rpb.shape[0], WIN_ROWS, GRID_W, WIN_ROWS * GRID_W)


def _latent_na(proj, row_block0, n_seq, seq, cache_k, cache_v, layer, bias, na_qn, na_kn, mixed_prev):
    past = cache_k.shape[3]

    def col(cb):
        return pl.BlockSpec((seq, HEAD_W), lambda b, h, cb=cb: (row_block0 + b, cb + h))

    cache = pl.BlockSpec((None, None, None, past, HEAD_W), lambda b, h: (b, layer, h, 0, 0))
    vec = pl.BlockSpec((1, HEAD_W), lambda b, h: (0, 0))
    return pl.pallas_call(
        functools.partial(_na_body, seq=seq),
        grid=(n_seq, N_HEADS),
        in_specs=[col(COL_B_Q), col(COL_B_K), col(COL_B_V), cache, cache,
                  pl.BlockSpec((None, WIN_ROWS, GRID_W, WIN_ROWS * GRID_W), lambda b, h: (h, 0, 0, 0)),
                  vec, vec, ANY_SPEC],
        out_specs=pl.BlockSpec((seq, HEAD_W), lambda b, h: (row_block0 + b, h)),
        out_shape=jax.ShapeDtypeStruct(mixed_prev.shape, BF16),
        input_output_aliases={8: 0},
        scratch_shapes=[pltpu.VMEM((seq, HEAD_W), BF16), pltpu.VMEM((seq, HEAD_W), BF16),
                        pltpu.VMEM((seq, 2 * HEAD_W), BF16)],
        compiler_params=_params("parallel", "parallel"),
        name="latent_neighbourhood_attention",
    )(proj, proj, proj, cache_k, cache_v, bias, na_qn[None, :], na_kn[None, :], mixed_prev)


def _gqa_body(q_ref, k_ref, v_ref, ck_ref, cv_ref, rope_ref, qn_ref, kn_ref, prev_ref, o_ref,
              qs_ref, ks_ref, vs_ref, *, seq, tq):
    del prev_ref
    group = N_HEADS // GQA_KV_HEADS
    half = HEAD_W // 2
    ks_ref[0:seq, :] = _rope(_rms_head(k_ref[...], kn_ref[...]), rope_ref, half).astype(BF16)
    ks_ref[seq:, :] = ck_ref[...].astype(BF16)
    vs_ref[0:seq, :] = _with_ones(v_ref[...].astype(BF16))
    vs_ref[seq:, :] = _with_ones(cv_ref[...].astype(BF16))
    for g in range(group):
        q = _rms_head(q_ref[:, g * HEAD_W:(g + 1) * HEAD_W], qn_ref[...]) * (HEAD_W ** -0.5)
        qs_ref[g] = _rope(q, rope_ref, half).astype(BF16)
    kk = ks_ref[...]
    vv = vs_ref[...]
    for g in range(group):
        def q_step(i, carry, g=g):
            qrows = pl.ds(pl.multiple_of(i * tq, tq), tq)
            o = _softmax_pv([_dot_nt(qs_ref[g, qrows, :], kk)], [vv])
            o_ref[qrows, g * HEAD_W:(g + 1) * HEAD_W] = o.astype(o_ref.dtype)
            return carry

        lax.fori_loop(0, seq // tq, q_step, 0, unroll=2)


def _latent_gqa(proj, row_block0, n_seq, seq, cache_k, cache_v, layer, rope, gqa_qn, gqa_kn, mixed_prev):
    past = cache_k.shape[3]
    group = N_HEADS // GQA_KV_HEADS
    tq = 256
    cache = pl.BlockSpec((None, None, None, past, HEAD_W), lambda b, n: (b, layer, n, 0, 0))
    vec = pl.BlockSpec((1, HEAD_W), lambda b, n: (0, 0))
    return pl.pallas_call(
        functools.partial(_gqa_body, seq=seq, tq=tq),
        grid=(n_seq, GQA_KV_HEADS),
        in_specs=[pl.BlockSpec((seq, group * HEAD_W), lambda b, n: (row_block0 + b, COL_C_Q // group + n)),
                  pl.BlockSpec((seq, HEAD_W), lambda b, n: (row_block0 + b, COL_C_K + n)),
                  pl.BlockSpec((seq, HEAD_W), lambda b, n: (row_block0 + b, COL_C_V + n)),
                  cache, cache,
                  pl.BlockSpec((3, seq, HEAD_W), lambda b, n: (0, 0, 0)),
                  vec, vec, ANY_SPEC],
        out_specs=pl.BlockSpec((seq, group * HEAD_W), lambda b, n: (row_block0 + b, n)),
        out_shape=jax.ShapeDtypeStruct(mixed_prev.shape, BF16),
        input_output_aliases={8: 0},
        scratch_shapes=[pltpu.VMEM((group, seq, HEAD_W), BF16),
                        pltpu.VMEM((seq + past, HEAD_W), BF16),
                        pltpu.VMEM((seq + past, 2 * HEAD_W), BF16)],
        compiler_params=_params("parallel", "parallel"),
        name="latent_gqa_attention",
    )(proj, proj, proj, cache_k, cache_v, rope, gqa_qn[None, :], gqa_kn[None, :], mixed_prev)


def _diff_body(q_ref, k_ref, v_ref, ck_ref, cv_ref, rope_ref, qn_ref, kn_ref, sub_ref, lam_ref, prev_ref, o_ref,
               qs_ref, k0_ref, k1_ref, vs_ref, *, seq, tq, lam_init):
    del prev_ref
    half = DF_DQK // 2
    k = _rope(_rms_halves(k_ref[...], kn_ref[...]), rope_ref, half)
    lo = _lane_lt(k.shape, DF_DQK)
    k0_ref[0:seq, :] = jnp.where(lo, k, 0.0).astype(BF16)
    k1_ref[0:seq, :] = jnp.where(lo, 0.0, k).astype(BF16)
    ck = ck_ref[...]
    lo_c = _lane_lt(ck.shape, DF_DQK)
    k0_ref[seq:, :] = jnp.where(lo_c, ck, 0.0).astype(BF16)
    k1_ref[seq:, :] = jnp.where(lo_c, 0.0, ck).astype(BF16)
    vs_ref[0:seq, :] = _with_ones(v_ref[...].astype(BF16))
    vs_ref[seq:, :] = _with_ones(cv_ref[...].astype(BF16))
    q = _rms_halves(q_ref[...], qn_ref[...]) * (DF_DQK ** -0.5)
    qs_ref[...] = _rope(q, rope_ref, half).astype(BF16)
    lam = _lambda(lam_ref, lam_init)
    k0 = k0_ref[...]
    k1 = k1_ref[...]
    vv = vs_ref[...]

    def q_step(i, carry):
        qrows = pl.ds(pl.multiple_of(i * tq, tq), tq)
        qb = qs_ref[qrows, :]
        o = _diff_pv(_dot_nt(qb, k0), _dot_nt(qb, k1), lam, vv)
        o_ref[qrows, :] = (_rms_head(o, sub_ref[...]) * (1.0 - lam_init)).astype(o_ref.dtype)
        return carry

    lax.fori_loop(0, seq // tq, q_step, 0, unroll=2)


def _latent_diff(proj, row_block0, n_seq, seq, cache_k2, cache_v, layer, rope, df_qn, df_kn, df_subln, df_lam,
                 lam_init, mixed_prev):
    past = cache_k2.shape[3]
    tq = 256

    def col(cb):
        return pl.BlockSpec((seq, HEAD_W), lambda b, h, cb=cb: (row_block0 + b, cb + h))

    cache = pl.BlockSpec((None, None, None, past, HEAD_W), lambda b, h: (b, layer, h, 0, 0))
    vec = pl.BlockSpec((1, HEAD_W), lambda b, h: (0, 0))
    kv_scratch = pltpu.VMEM((seq + past, HEAD_W), BF16)
    return pl.pallas_call(
        functools.partial(_diff_body, seq=seq, tq=tq, lam_init=lam_init),
        grid=(n_seq, N_HEADS),
        in_specs=[col(COL_D_Q), col(COL_D_K), col(COL_D_V), cache, cache,
                  pl.BlockSpec((3, seq, HEAD_W), lambda b, h: (0, 0, 0)),
                  vec, vec, vec, pl.BlockSpec((4, DF_DQK), lambda b, h: (0, 0)), ANY_SPEC],
        out_specs=pl.BlockSpec((seq, HEAD_W), lambda b, h: (row_block0 + b, h)),
        out_shape=jax.ShapeDtypeStruct(mixed_prev.shape, BF16),
        input_output_aliases={10: 0},
        scratch_shapes=[pltpu.VMEM((seq, HEAD_W), BF16), kv_scratch, kv_scratch,
                        pltpu.VMEM((seq + past, 2 * HEAD_W), BF16)],
        compiler_params=_params("parallel", "parallel"),
        name="latent_diff_attention",
    )(proj, proj, proj, cache_k2, cache_v, rope, jnp.tile(df_qn, 2)[None, :], jnp.tile(df_kn, 2)[None, :],
      df_subln[None, :], df_lam, mixed_prev)


def _first_max(vals):
    best = vals[0]
    idx = jnp.zeros(best.shape, jnp.int32)
    for i in range(1, len(vals)):
        better = vals[i] > best
        best = jnp.where(better, vals[i], best)
        idx = jnp.where(better, i, idx)
    return best, idx


def _pick(vals, idx):
    out = vals[0]
    for i in range(1, len(vals)):
        out = jnp.where(idx == i, vals[i], out)
    return out


def _outproj_body(ma_ref, mb_ref, mc_ref, md_ref, w_ref, hc_ref, hl_ref, mod_ref, n2_ref, rw_ref, rb_ref,
                  h1_ref, x2_ref, idx_ref, gate_ref, mixed_ref, *, ctx_tiles):
    for g, m_ref in enumerate((ma_ref, mb_ref, mc_ref, md_ref)):
        mixed_ref[:, g * GROUP_W:(g + 1) * GROUP_W] = m_ref[...]
    y = mod_ref[2:3, :] * _dot(mixed_ref[...], w_ref[...])

    def residual(h_ref):
        h1_ref[...] = h_ref[...] + y

    is_ctx = pl.program_id(0) < ctx_tiles
    pl.when(is_ctx)(lambda: residual(hc_ref))
    pl.when(jnp.logical_not(is_ctx))(lambda: residual(hl_ref))
    h1 = h1_ref[...]
    x2 = _rms(h1, n2_ref[...], D_MODEL) * (1.0 + mod_ref[4:5, :]) + mod_ref[3:4, :]
    x2_ref[...] = _pack_bf16_pairs(x2)
    x_hi = x2.astype(BF16)
    x_lo = (x2 - x_hi.astype(F32)).astype(BF16)
    acc = _dot(x_hi, rw_ref[...])
    logits_tok = acc[:, 0:HEAD_W] + acc[:, HEAD_W:2 * HEAD_W] + _dot(x_lo, rw_ref[:, 0:HEAD_W])
    logits = logits_tok.T[0:N_EXPERTS, :]
    aff_all = _sigmoid(logits)
    sel_all = aff_all + rb_ref[...]
    aff = [aff_all[e:e + 1, :] for e in range(N_EXPERTS)]
    sel = [sel_all[e:e + 1, :] for e in range(N_EXPERTS)]
    neg = jnp.full(sel[0].shape, -jnp.inf, F32)
    scores = []
    for g in range(N_EXP_GROUPS):
        grp = sel[g * EXP_PER_GROUP:(g + 1) * EXP_PER_GROUP]
        m1, i1 = _first_max(grp)
        m2, _ = _first_max([jnp.where(i1 == j, neg, grp[j]) for j in range(EXP_PER_GROUP)])
        scores.append(m1 + m2)
    _, g_best = _first_max(scores)
    in_sel = [_pick([sel[g * EXP_PER_GROUP + j] for g in range(N_EXP_GROUPS)], g_best)
              for j in range(EXP_PER_GROUP)]
    in_aff = [_pick([aff[g * EXP_PER_GROUP + j] for g in range(N_EXP_GROUPS)], g_best)
              for j in range(EXP_PER_GROUP)]
    _, l1 = _first_max(in_sel)
    _, l2 = _first_max([jnp.where(l1 == j, neg, in_sel[j]) for j in range(EXP_PER_GROUP)])
    w1 = _pick(in_aff, l1)
    w2 = _pick(in_aff, l2)
    idx_ref[0:1, :] = g_best * EXP_PER_GROUP + l1
    idx_ref[1:2, :] = g_best * EXP_PER_GROUP + l2
    gate_ref[0:1, :] = w1 / (w1 + w2)
    gate_ref[1:2, :] = w2 / (w1 + w2)


def _output_projection(mixed4, w_out_bf16, h_ctx, h_lat, mod, norm2, router_w, router_b, cond_of_tile, tm):
    d = h_ctx.shape[1]
    t = h_ctx.shape[0] + h_lat.shape[0]
    ctx_tiles = h_ctx.shape[0] // tm
    slab = pl.BlockSpec((tm, GROUP_W), lambda i: (i, 0))
    rw_hi = router_w.astype(BF16)
    rw_lo = (router_w - rw_hi.astype(F32)).astype(BF16)
    pad = ((0, 0), (0, HEAD_W - N_EXPERTS))
    router_split = jnp.concatenate([jnp.pad(rw_hi, pad), jnp.pad(rw_lo, pad)], axis=1)
    return pl.pallas_call(
        functools.partial(_outproj_body, ctx_tiles=ctx_tiles),
        grid=(t // tm,),
        in_specs=[slab, slab, slab, slab,
                  pl.BlockSpec((d, d), lambda i: (0, 0))] + _group_specs(tm, d, ctx_tiles) + [
                  pl.BlockSpec((None, 6, d), lambda i: (cond_of_tile(i), 0, 0)),
                  pl.BlockSpec((1, d), lambda i: (0, 0)),
                  pl.BlockSpec((d, 2 * HEAD_W), lambda i: (0, 0)),
                  pl.BlockSpec((N_EXPERTS, 1), lambda i: (0, 0))],
        out_specs=[pl.BlockSpec((tm, d), lambda i: (i, 0)),
                   pl.BlockSpec((tm, d // 2), lambda i: (i, 0)),
                   pl.BlockSpec((2, tm), lambda i: (0, i)),
                   pl.BlockSpec((2, tm), lambda i: (0, i))],
        out_shape=[jax.ShapeDtypeStruct((t, d), F32), jax.ShapeDtypeStruct((t, d // 2), jnp.uint32),
                   jax.ShapeDtypeStruct((2, t), jnp.int32), jax.ShapeDtypeStruct((2, t), F32)],
        scratch_shapes=[pltpu.VMEM((tm, d), BF16)],
        compiler_params=_params("parallel", vmem=VMEM_LIMIT_PROJ),
        name="out_proj_residual_router",
    )(*mixed4, w_out_bf16, h_ctx, h_lat, mod, norm2[None, :], router_split, router_b[:, None])


def _vmem_row(ref, base, u):
    return ref.at[pl.ds(base, SUBLANES), :].at[pl.ds(u, 1), :]


def _for_row_groups(n_rows, fn):
    def body(g, carry):
        base = pl.multiple_of(g * SUBLANES, SUBLANES)
        for u in range(SUBLANES):
            fn(base, u)
        return carry

    lax.fori_loop(0, n_rows // SUBLANES, body, 0)


def _rows_wait(src_hbm, dst, sem, n_rows):
    pltpu.make_async_copy(src_hbm.at[pl.ds(0, n_rows), :], dst.at[pl.ds(0, n_rows), :], sem).wait()


def _dispatch_body(dest_ref, ps_ref, pe_ref, x_ref, xs_hbm, zero_ref, sem, zsem, *, n_tok, tile):
    i = pl.program_id(0)

    def zero_copy(e):
        first = pl.multiple_of(pe_ref[e] - MOE_ROWS, MOE_ROWS)
        return pltpu.make_async_copy(zero_ref, xs_hbm.at[pl.ds(first, MOE_ROWS), :], zsem)

    @pl.when(i == 0)
    def _():
        zero_ref[...] = jnp.zeros(zero_ref.shape, zero_ref.dtype)
        for e in range(N_EXPERTS):
            @pl.when(pe_ref[e] > ps_ref[e])
            def _(e=e):
                zero_copy(e).start()
        for e in range(N_EXPERTS):
            @pl.when(pe_ref[e] > ps_ref[e])
            def _(e=e):
                zero_copy(e).wait()

    for k in range(2):
        def scatter_row(base, u, k=k):
            row = dest_ref[k * n_tok + i * tile + base + u]
            pltpu.make_async_copy(_vmem_row(x_ref, base, u), xs_hbm.at[pl.ds(row, 1), :], sem).start()

        _for_row_groups(tile, scatter_row)
    for k in range(2):
        _rows_wait(x_ref, xs_hbm, sem, tile)


def _dispatch(x2, dest, pad_start, pad_end, n_rows):
    t, d = x2.shape
    tile = TOKEN_TILE
    grid_spec = pltpu.PrefetchScalarGridSpec(
        num_scalar_prefetch=3,
        grid=(t // tile,),
        in_specs=[pl.BlockSpec((tile, d), lambda i, dst, ps, pe: (i, 0))],
        out_specs=ANY_SPEC,
        scratch_shapes=[pltpu.VMEM((MOE_ROWS, d), x2.dtype), pltpu.SemaphoreType.DMA, pltpu.SemaphoreType.DMA],
    )
    return pl.pallas_call(
        functools.partial(_dispatch_body, n_tok=t, tile=tile),
        grid_spec=grid_spec,
        out_shape=jax.ShapeDtypeStruct((n_rows, d), x2.dtype),
        compiler_params=_params("arbitrary"),
        name="moe_dispatch",
    )(dest, pad_start, pad_end, x2)


def _expert_body(be_ref, nb_ref, nx_ref, x_ref, wg_hbm, wu_hbm, wd_hbm, o_ref,
                 stage_g, stage_u, stage_d, wg_bf, wu_bf, wd_bf, sem, *, layer):
    i = pl.program_id(0)

    def weight_copies(e):
        return (pltpu.make_async_copy(wg_hbm.at[layer, e], stage_g, sem.at[0]),
                pltpu.make_async_copy(wu_hbm.at[layer, e], stage_u, sem.at[1]),
                pltpu.make_async_copy(wd_hbm.at[layer, e], stage_d, sem.at[2]))

    @pl.when(i < nb_ref[0])
    def _():
        e = be_ref[i]

        @pl.when(i == 0)
        def _():
            for c in weight_copies(e):
                c.start()

        @pl.when((i == 0) | (e != be_ref[jnp.maximum(i - 1, 0)]))
        def _():
            for c in weight_copies(e):
                c.wait()
            wg_bf[...] = stage_g[...].astype(BF16)
            wu_bf[...] = stage_u[...].astype(BF16)
            wd_bf[...] = stage_d[...].astype(BF16)
            nxt = nx_ref[e]

            @pl.when(nxt < N_EXPERTS)
            def _():
                for c in weight_copies(nxt):
                    c.start()

        x = _unpack_bf16_pairs(x_ref[...]).astype(BF16)
        hdn = _silu(_dot(x, wg_bf[...])) * _dot(x, wu_bf[...])
        o_ref[...] = _pack_bf16_pairs(_dot(hdn.astype(BF16), wd_bf[...]))

    @pl.when(i >= nb_ref[0])
    def _():
        o_ref[...] = jnp.zeros(o_ref.shape, o_ref.dtype)


def _expert_blocks(xs, block_expert, n_used, next_expert, w_gate, w_up, w_down, layer):
    n_rows, half = xs.shape
    d = 2 * half
    ff = w_gate.shape[-1]
    grid_spec = pltpu.PrefetchScalarGridSpec(
        num_scalar_prefetch=3,
        grid=(n_rows // MOE_ROWS,),
        in_specs=[pl.BlockSpec((MOE_ROWS, half), lambda i, be, nb, nx: (jnp.minimum(i, nb[0] - 1), 0)),
                  ANY_SPEC, ANY_SPEC, ANY_SPEC],
        out_specs=pl.BlockSpec((MOE_ROWS, half), lambda i, be, nb, nx: (i, 0)),
        scratch_shapes=[pltpu.VMEM((d, ff), F32), pltpu.VMEM((d, ff), F32), pltpu.VMEM((ff, d), F32),
                        pltpu.VMEM((d, ff), BF16), pltpu.VMEM((d, ff), BF16), pltpu.VMEM((ff, d), BF16),
                        pltpu.SemaphoreType.DMA((3,))],
    )
    return pl.pallas_call(
        functools.partial(_expert_body, layer=layer),
        grid_spec=grid_spec,
        out_shape=jax.ShapeDtypeStruct((n_rows, half), jnp.uint32),
        compiler_params=_params("arbitrary"),
        name="moe_expert_blocks",
    )(block_expert, n_used, next_expert, xs, w_gate, w_up, w_down)


def _combine_body(dest_ref, h_ref, gate_ref, y_hbm, mod_ref, *rest, n_tok, tile, ctx_tiles):
    out_refs, (ybuf, sem) = rest[:-2], rest[-2:]
    i = pl.program_id(0)
    n_tiles = pl.num_programs(0)

    def start(blk, slot):
        for k in range(2):
            def gather_row(base, u, k=k):
                row = dest_ref[k * n_tok + blk * tile + base + u]
                pltpu.make_async_copy(y_hbm.at[pl.ds(row, 1), :], _vmem_row(ybuf.at[slot, k], base, u),
                                      sem.at[slot]).start()

            _for_row_groups(tile, gather_row)

    @pl.when(i == 0)
    def _():
        start(0, 0)

    @pl.when(i + 1 < n_tiles)
    def _():
        start(i + 1, (i + 1) % 2)

    slot = i % 2
    for k in range(2):
        _rows_wait(y_hbm, ybuf.at[slot, k], sem.at[slot], tile)
    gate = gate_ref[...]
    y0 = _unpack_bf16_pairs(ybuf[slot, 0])
    y1 = _unpack_bf16_pairs(ybuf[slot, 1])
    out = h_ref[...] + mod_ref[5:6, :] * (gate[:, 0:1] * y0 + gate[:, 1:2] * y1)
    @pl.when(i < ctx_tiles)
    def _():
        out_refs[0][...] = out

    @pl.when(i >= ctx_tiles)
    def _():
        out_refs[1][...] = out


def _combine(h1, yb, dest, gates, mod, cond_of_tile, tile, split_rows):
    t, d = h1.shape
    row_tile = pl.BlockSpec((tile, d), lambda i, dst: (i, 0))
    ctx_tiles = split_rows // tile
    out_specs = _group_specs(tile, d, ctx_tiles)
    out_shape = [jax.ShapeDtypeStruct((split_rows, d), F32), jax.ShapeDtypeStruct((t - split_rows, d), F32)]
    grid_spec = pltpu.PrefetchScalarGridSpec(
        num_scalar_prefetch=1,
        grid=(t // tile,),
        in_specs=[row_tile,
                  pl.BlockSpec((tile, 2), lambda i, dst: (i, 0)),
                  ANY_SPEC,
                  pl.BlockSpec((None, 6, d), lambda i, dst: (cond_of_tile(i), 0, 0))],
        out_specs=out_specs,
        scratch_shapes=[pltpu.VMEM((2, 2, tile, yb.shape[1]), yb.dtype), pltpu.SemaphoreType.DMA((2,))],
    )
    return pl.pallas_call(
        functools.partial(_combine_body, n_tok=t, tile=tile, ctx_tiles=ctx_tiles),
        grid_spec=grid_spec,
        out_shape=out_shape,
        compiler_params=_params("arbitrary"),
        name="moe_gated_residual",
    )(dest, h1, gates, yb, mod)


def _moe(h1, x2, idx_t, gate_t, mod, w_gate, w_up, w_down, layer, cond_of_tile, split_rows):
    t, d = h1.shape
    n = 2 * t
    experts = idx_t.reshape(n)
    onehot = (experts[:, None] == jnp.arange(N_EXPERTS, dtype=jnp.int32)[None, :]).astype(BF16)
    blocks = onehot.reshape(n // TOKEN_TILE, TOKEN_TILE, N_EXPERTS)
    tri = jnp.asarray(np.tril(np.ones((TOKEN_TILE, TOKEN_TILE), np.float32)), BF16)
    within = jnp.einsum('ij,bjk->bik', tri, blocks, preferred_element_type=F32)
    block_total = within[:, -1, :]
    block_first = jnp.cumsum(block_total, axis=0) - block_total
    counts = (block_first[-1] + block_total[-1]).astype(jnp.int32)
    before = (within + block_first[:, None, :]).reshape(n, N_EXPERTS) - 1.0
    rank = jnp.sum(before * onehot.astype(F32), axis=1).astype(jnp.int32)
    padded = (counts + MOE_ROWS - 1) // MOE_ROWS * MOE_ROWS
    pad_end = jnp.cumsum(padded).astype(jnp.int32)
    pad_start = pad_end - padded
    dest = (pad_start[experts] + rank).astype(jnp.int32)
    n_blocks = (n + N_EXPERTS * (MOE_ROWS - 1) + MOE_ROWS - 1) // MOE_ROWS
    block_first_row = jnp.arange(n_blocks, dtype=jnp.int32) * MOE_ROWS
    block_expert = jnp.minimum(jnp.sum((pad_end[None, :] <= block_first_row[:, None]).astype(jnp.int32), axis=1),
                               N_EXPERTS - 1)
    n_used = (pad_end[-1:] // MOE_ROWS).astype(jnp.int32)
    xs = _dispatch(x2, dest, pad_start, pad_end, n_blocks * MOE_ROWS)
    ids = jnp.arange(N_EXPERTS, dtype=jnp.int32)
    later_with_rows = (counts[None, :] > 0) & (ids[None, :] > ids[:, None])
    next_expert = jnp.min(jnp.where(later_with_rows, ids[None, :], N_EXPERTS), axis=1).astype(jnp.int32)
    yb = _expert_blocks(xs, block_expert, n_used, next_expert, w_gate, w_up, w_down, layer)
    return _combine(h1, yb, dest, gate_t.T, mod, cond_of_tile, TOKEN_TILE, split_rows)


def kernel(x_prompt, x_sample, cache_na_k, cache_na_v, cache_gqa_k, cache_gqa_v, cache_diff_k, cache_diff_v, state_hgrn, c, c_ctx, w_mod, b_mod, norm1, norm2, w_in, w_out, hg_lb_logits, hg_onorm, na_qn, na_kn, na_rpb, gqa_qn, gqa_kn, df_qn, df_kn, df_lam, df_subln, router_w, router_b, w_gate, w_up, w_down):
    n_ctx, ctx_len, d = x_prompt.shape
    n_lat, lat_len, _ = x_sample.shape
    depth = w_in.shape[0]
    t_ctx = n_ctx * ctx_len
    assert t_ctx % lat_len == 0 and lat_len % GRID_W == 0 and lat_len // GRID_W >= WIN_ROWS
    tm = next(m for m in (1024, 512, 256) if t_ctx % m == 0 and lat_len % m == 0)
    tm2 = min(tm, 512)
    lat_block0 = t_ctx // lat_len

    def cond_tile(tile_rows):
        def cond_of_tile(i):
            return jnp.where(i < t_ctx // tile_rows, 0, 1 + (i - t_ctx // tile_rows) // (lat_len // tile_rows))
        return cond_of_tile

    sm = jax.nn.softmax(hg_lb_logits.astype(F32), axis=0)
    lower = jnp.cumsum(sm, axis=0) - sm[0:1]
    mod_all = _modulation(jnp.concatenate([c_ctx[None, :], c], axis=0), w_mod, b_mod)
    mod_all = mod_all.reshape(depth, 1 + n_lat, 6, d)
    hgrn_consts = _hgrn_constants(HGRN_CHUNK)
    rope_c = _rope_tables(lat_len, HEAD_W)
    rope_d = _rope_tables(lat_len, DF_DQK)
    past = cache_diff_k.shape[4]
    cache_diff_k2 = cache_diff_k.transpose(0, 1, 2, 4, 3, 5).reshape(n_lat, depth, N_HEADS, past, HEAD_W)

    h_ctx, h_lat = x_prompt.reshape(t_ctx, d), x_sample.reshape(n_lat * lat_len, d)
    caches, states = [], None
    for layer in range(depth):
        mod = mod_all[layer]
        lam_init = 0.8 - 0.6 * math.exp(-0.3 * layer)
        proj = _input_projection(h_ctx, h_lat, mod, norm1[layer], w_in[layer].astype(BF16), cond_tile(tm), tm)
        mix_a, states = _hgrn(proj, 0, n_ctx, ctx_len, lower[layer], hg_onorm[layer], hgrn_consts, None, None,
                              states, layer, depth)
        gains = (na_qn[layer], na_kn[layer], gqa_qn[layer], gqa_kn[layer], df_qn[layer], df_kn[layer],
                 df_subln[layer], df_lam[layer])
        mix_b, mix_c, mix_d, *caches = _context_attention(proj, n_ctx, ctx_len, gains, lam_init, layer, depth,
                                                          caches)
        mix_a, _ = _hgrn(proj, lat_block0, n_lat, lat_len, lower[layer], hg_onorm[layer], hgrn_consts,
                         state_hgrn, mix_a, None, layer, depth)
        mix_b = _latent_na(proj, lat_block0, n_lat, lat_len, cache_na_k, cache_na_v, layer,
                           _na_bias(na_rpb[layer]), na_qn[layer], na_kn[layer], mix_b)
        mix_c = _latent_gqa(proj, lat_block0, n_lat, lat_len, cache_gqa_k, cache_gqa_v, layer, rope_c,
                            gqa_qn[layer], gqa_kn[layer], mix_c)
        mix_d = _latent_diff(proj, lat_block0, n_lat, lat_len, cache_diff_k2, cache_diff_v, layer, rope_d,
                             df_qn[layer], df_kn[layer], df_subln[layer], df_lam[layer], lam_init, mix_d)
        h1, x2, idx_t, gate_t = _output_projection((mix_a, mix_b, mix_c, mix_d), w_out[layer].astype(BF16), h_ctx,
                                                   h_lat, mod, norm2[layer], router_w, router_b, cond_tile(tm2), tm2)
        h_ctx, h_lat = _moe(h1, x2, idx_t, gate_t, mod, w_gate, w_up, w_down, layer, cond_tile(TOKEN_TILE), t_ctx)
    y_prompt = h_ctx.reshape(n_ctx, ctx_len, d)
    y_sample = h_lat.reshape(n_lat, lat_len, d)
    return (y_prompt, y_sample, *caches, states)
```

```python
import functools
import math

import numpy as np
import jax
import jax.numpy as jnp
from jax import lax
from jax.experimental import pallas as pl
from jax.experimental.pallas import tpu as pltpu

D_MODEL = 2048
GRID_W = 64
GROUP_W = D_MODEL // 4
N_HEADS = 4
HEAD_W = GROUP_W // N_HEADS
SUBLANES = 8
GQA_KV_HEADS = 2
DF_DQK = HEAD_W // 2
WIN_ROWS = 8
WIN_COLS = 16
N_EXPERTS = 16
N_EXP_GROUPS = 4
EXP_PER_GROUP = N_EXPERTS // N_EXP_GROUPS
EXPERT_FF = D_MODEL // 4
ROPE_THETA = 10000.0
EPS = 1e-6
NEG_INF = -1e30
LOG2E = math.log2(math.e)
IN_WIDTH = 13 * GROUP_W

COL_A_Q, COL_A_FF, COL_A_FB, COL_A_I, COL_A_G = 0, 4, 8, 12, 16
COL_B_Q, COL_B_K, COL_B_V = 20, 24, 28
COL_C_Q, COL_C_K, COL_C_V = 32, 36, 38
COL_D_Q, COL_D_K, COL_D_V = 40, 44, 48

HGRN_CHUNK = 128
HGRN_UNROLL = 4
HGRN_INPUT_VMEM = 24 * 1024 * 1024
CTX_KV_HEADS_PER_STEP = 1
MOE_ROWS = 256
TOKEN_TILE = 256
VMEM_LIMIT = 48 * 1024 * 1024
VMEM_LIMIT_PROJ = 56 * 1024 * 1024

F32 = jnp.float32
BF16 = jnp.bfloat16
ANY_SPEC = pl.BlockSpec(memory_space=pl.ANY)


def _params(*sem, vmem=VMEM_LIMIT):
    return pltpu.CompilerParams(dimension_semantics=sem, vmem_limit_bytes=vmem)


def _sigmoid(x):
    return 1.0 / (1.0 + jnp.exp(-x))


def _silu(x):
    return x * _sigmoid(x)


def _rms(x, gain, n):
    return x * lax.rsqrt(jnp.sum(x * x, axis=-1, keepdims=True) * (1.0 / n) + EPS) * gain


def _dot(a, b):
    return jnp.dot(a, b, preferred_element_type=F32)


def _dot_nt(a, b):
    return lax.dot_general(a, b, (((1,), (1,)), ((), ())), preferred_element_type=F32)


def _dot_tn(a, b):
    return lax.dot_general(a, b, (((0,), (0,)), ((), ())), preferred_element_type=F32)


def _pack_bf16_pairs(x):
    k = x.shape[1] // 2
    lo = lax.bitcast_convert_type(x[:, :k].astype(BF16).astype(F32), jnp.uint32) >> 16
    hi = lax.bitcast_convert_type(x[:, k:].astype(BF16).astype(F32), jnp.uint32)
    return hi | lo


def _unpack_bf16_pairs(w):
    lo = lax.bitcast_convert_type(w << 16, F32)
    hi = lax.bitcast_convert_type(w & jnp.uint32(0xFFFF0000), F32)
    return jnp.concatenate([lo, hi], axis=1)


def _aligned(x, m):
    return x if isinstance(x, int) else pl.multiple_of(x, m)


def _alias_kwargs(n_inputs, prev, first_out):
    return ([ANY_SPEC] * len(prev), list(prev), {n_inputs + k: first_out + k for k in range(len(prev))})


def _mod_body(cond_ref, w_ref, b_ref, o_ref):
    w = w_ref[...]
    for c in range(cond_ref.shape[0]):
        s = _silu(cond_ref[c])
        o_ref[c:c + 1, :] = jnp.sum(w * s, axis=0, keepdims=True) + b_ref[...]


def _modulation(cond, w_mod, b_mod):
    depth, d, n6 = w_mod.shape
    nc = cond.shape[0]
    tn = 1024
    return pl.pallas_call(
        _mod_body,
        grid=(depth, n6 // tn),
        in_specs=[pl.BlockSpec((nc, d, 1), lambda l, j: (0, 0, 0)),
                  pl.BlockSpec((None, d, tn), lambda l, j: (l, 0, j)),
                  pl.BlockSpec((None, 1, tn), lambda l, j: (l, 0, j))],
        out_specs=pl.BlockSpec((None, nc, tn), lambda l, j: (l, 0, j)),
        out_shape=jax.ShapeDtypeStruct((depth, nc, n6), F32),
        compiler_params=_params("parallel", "parallel"),
        name="adaln_modulation",
    )(cond[:, :, None], w_mod, b_mod[:, None, :])


def _group_specs(tile, d, ctx_tiles):
    return [pl.BlockSpec((tile, d), lambda i, *_: (jnp.minimum(i, ctx_tiles - 1), 0)),
            pl.BlockSpec((tile, d), lambda i, *_: (jnp.maximum(i - ctx_tiles, 0), 0))]


def _inproj_body(hc_ref, hl_ref, mod_ref, n1_ref, w_ref, o_ref, xn_ref, *, ctx_tiles):
    def normalise(h_ref):
        y = _rms(h_ref[...], n1_ref[...], D_MODEL)
        xn_ref[...] = (y * (1.0 + mod_ref[1:2, :]) + mod_ref[0:1, :]).astype(BF16)

    first = pl.program_id(1) == 0
    is_ctx = pl.program_id(0) < ctx_tiles
    pl.when(first & is_ctx)(lambda: normalise(hc_ref))
    pl.when(first & jnp.logical_not(is_ctx))(lambda: normalise(hl_ref))
    o_ref[...] = _dot(xn_ref[...], w_ref[...])


def _input_projection(h_ctx, h_lat, mod, norm1, w_in_bf16, cond_of_tile, tm):
    d = h_ctx.shape[1]
    t = h_ctx.shape[0] + h_lat.shape[0]
    ctx_tiles = h_ctx.shape[0] // tm
    n = w_in_bf16.shape[1]
    tn = 512
    return pl.pallas_call(
        functools.partial(_inproj_body, ctx_tiles=ctx_tiles),
        grid=(t // tm, n // tn),
        in_specs=_group_specs(tm, d, ctx_tiles) + [
                  pl.BlockSpec((None, 6, d), lambda i, j: (cond_of_tile(i), 0, 0)),
                  pl.BlockSpec((1, d), lambda i, j: (0, 0)),
                  pl.BlockSpec((d, tn), lambda i, j: (0, j))],
        out_specs=pl.BlockSpec((tm, tn), lambda i, j: (i, j)),
        out_shape=jax.ShapeDtypeStruct((t, n), F32),
        scratch_shapes=[pltpu.VMEM((tm, d), BF16)],
        compiler_params=_params("parallel", "arbitrary", vmem=VMEM_LIMIT_PROJ),
        name="norm_modulate_in_proj",
    )(h_ctx, h_lat, mod, norm1[None, :], w_in_bf16)


def _hgrn_constants(c):
    nl = int(math.log2(c))
    idx = np.arange(c)
    e = np.zeros((nl + 2, c, c), np.float32)
    m = np.zeros((nl + 1, c, c), np.float32)
    e[0] = idx[None, :] <= idx[:, None]
    e[1] = idx[None, :] > idx[:, None]
    m[0] = np.eye(c)
    for li in range(nl):
        s = c >> (li + 1)
        parent = idx // (2 * s)
        right = (idx % (2 * s)) >= s
        ref = parent * 2 * s + s - 1
        for i in range(c):
            if right[i]:
                e[2 + li, i, ref[i] + 1:i + 1] = 1.0
            else:
                e[2 + li, i, i + 1:ref[i] + 1] = 1.0
        m[1 + li] = right[:, None] & ~right[None, :] & (parent[:, None] == parent[None, :])
    keep = [0, 1] + [2 + li for li in range(nl) if (c >> (li + 1)) < SUBLANES]
    e = e[keep]
    e2 = np.stack([e, e[:, ::-1, ::-1]]).reshape(2, len(keep) * c, c)
    m2 = np.stack([m, m[:, ::-1, ::-1]])
    return jnp.asarray(e2, BF16), jnp.asarray(m2, F32)


def _hgrn_body(*refs, seq, chunk, unroll, heads, has_s0, emit_state, n_alias):
    q_ref, ff_ref, fb_ref, i_ref, g_ref, lb_ref, on_ref, e_ref, m_ref = refs[:9]
    pos = 9
    s0_ref = None
    if has_s0:
        s0_ref = refs[pos]
        pos += 1
    pos += n_alias
    o_ref = refs[pos]
    pos += 1
    if emit_state:
        st_ref = refs[pos]
        pos += 1
    of_ref, ob_ref = refs[pos], refs[pos + 1]
    c = chunk
    n_chunks = seq // c
    assert seq % c == 0 and n_chunks % unroll == 0
    n_levels = m_ref.shape[1] - 1
    gate_refs = (ff_ref, fb_ref)
    out_refs = (of_ref, ob_ref)

    def chunk_step(c0, d, hh, st):
        rows = pl.ds(c0, c)
        lanes = slice(hh * HEAD_W, (hh + 1) * HEAD_W)
        lb = lb_ref[d:d + 1, lanes]
        f = lb + (1.0 - lb) * _sigmoid(gate_refs[d][rows, lanes])
        g = jnp.log2(f)
        k = 1.0 - f
        q = _silu(q_ref[rows, lanes])
        v = i_ref[rows, lanes].astype(BF16)
        g_hi = g.astype(BF16)
        g_lo = (g - g_hi.astype(F32)).astype(BF16)
        g2 = _dot(e_ref[d], jnp.concatenate([g_hi, g_lo], axis=1))
        gsum = g2[:, 0:HEAD_W] + g2[:, HEAD_W:2 * HEAD_W]
        cum = gsum[0:c]
        x_cum = jnp.exp2(cum)
        x_tail = jnp.exp2(gsum[c:2 * c])
        row = lax.broadcasted_iota(jnp.int32, (c, HEAD_W), 0)
        s = m_ref[d, 0] * _dot_nt(q.astype(BF16), k.astype(BF16))
        n_matmul_levels = 0
        for lv in range(n_levels):
            half = c >> (lv + 1)
            if half >= SUBLANES:
                ref_row = half - 1 if d == 0 else half
                ref = jnp.concatenate(
                    [jnp.broadcast_to(cum[b * 2 * half + ref_row:b * 2 * half + ref_row + 1, :], (2 * half, HEAD_W))
                     for b in range(c // (2 * half))], axis=0)
                near = ((row % (2 * half)) >= half) == (d == 0)
                x_l = jnp.exp2(jnp.where(near, cum - ref, ref - cum))
            else:
                x_l = jnp.exp2(gsum[(2 + n_matmul_levels) * c:(3 + n_matmul_levels) * c])
                n_matmul_levels += 1
            s = s + m_ref[d, 1 + lv] * _dot_nt((q * x_l).astype(BF16), (k * x_l).astype(BF16))
        o = _dot_nt((q * x_cum).astype(BF16), st.astype(BF16)) + _dot(s.astype(BF16), v)
        out_refs[d][rows, lanes] = o
        total = x_cum[c - 1:c, :] if d == 0 else x_cum[0:1, :]
        return st * total + _dot_tn(v, (k * x_tail).astype(BF16))

    if has_s0:
        states0 = tuple(s0_ref[d, hh].T for hh in range(heads) for d in range(2))
    else:
        states0 = tuple(jnp.zeros((HEAD_W, HEAD_W), F32) for _ in range(2 * heads))

    def loop(t, states):
        states = list(states)
        for u in range(unroll):
            j = t * unroll + u
            for hh in range(heads):
                states[2 * hh] = chunk_step(_aligned(j * c, c), 0, hh, states[2 * hh])
                states[2 * hh + 1] = chunk_step(_aligned((n_chunks - 1 - j) * c, c), 1, hh, states[2 * hh + 1])
        return tuple(states)

    if n_chunks == unroll:
        states = loop(0, states0)
    else:
        states = lax.fori_loop(0, n_chunks // unroll, loop, states0)
    for hh in range(heads):
        lanes = slice(hh * HEAD_W, (hh + 1) * HEAD_W)
        o = of_ref[:, lanes] + ob_ref[:, lanes]
        o_ref[:, lanes] = (_rms(o, on_ref[...], HEAD_W) * _silu(g_ref[:, lanes])).astype(o_ref.dtype)
        if emit_state:
            st_ref[0, hh] = states[2 * hh].T
            st_ref[1, hh] = states[2 * hh + 1].T


def _hgrn(proj, row_block0, n_seq, seq, lower, onorm, consts, s0, mixed_prev, state_prev, layer, depth):
    e_mat, masks = consts
    latent = s0 is not None
    hps = next(n for n in (4, 2, 1) if 5 * 2 * seq * n * HEAD_W * 4 <= HGRN_INPUT_VMEM)
    width = hps * HEAD_W

    def col(cb):
        return pl.BlockSpec((seq, width), lambda b, h, cb=cb: (row_block0 + b, cb // hps + h))

    state_spec = pl.BlockSpec((None, None, 2, hps, HEAD_W, HEAD_W), lambda b, h: (b, layer, 0, h, 0, 0))
    in_specs = [col(COL_A_Q), col(COL_A_FF), col(COL_A_FB), col(COL_A_I), col(COL_A_G),
                pl.BlockSpec((2, width), lambda b, h: (0, h)),
                pl.BlockSpec((1, HEAD_W), lambda b, h: (0, 0)),
                pl.BlockSpec(e_mat.shape, lambda b, h: (0, 0, 0)),
                pl.BlockSpec(masks.shape, lambda b, h: (0, 0, 0, 0))]
    args = [proj, proj, proj, proj, proj, lower, onorm[None, :], e_mat, masks]
    if latent:
        in_specs.append(state_spec)
        args.append(s0)
        prev = [mixed_prev]
    else:
        prev = [] if state_prev is None else [state_prev]
    alias_specs, alias_args, aliases = _alias_kwargs(len(args), prev, 0 if latent else 1)
    out_specs = [pl.BlockSpec((seq, width), lambda b, h: (row_block0 + b, h))]
    out_shape = [jax.ShapeDtypeStruct((proj.shape[0], GROUP_W), BF16)]
    if not latent:
        out_specs.append(state_spec)
        out_shape.append(jax.ShapeDtypeStruct((n_seq, depth, 2, N_HEADS, HEAD_W, HEAD_W), F32))
    res = pl.pallas_call(
        functools.partial(_hgrn_body, seq=seq, chunk=HGRN_CHUNK, unroll=min(HGRN_UNROLL, seq // HGRN_CHUNK),
                          heads=hps, has_s0=latent,
                          emit_state=not latent, n_alias=len(prev)),
        grid=(n_seq, N_HEADS // hps),
        in_specs=in_specs + alias_specs, out_specs=out_specs, out_shape=out_shape,
        input_output_aliases=aliases,
        scratch_shapes=[pltpu.VMEM((seq, width), F32), pltpu.VMEM((seq, width), F32)],
        compiler_params=_params("parallel", "parallel"),
        name="hgrn2_latent" if latent else "hgrn2_context",
    )(*args, *alias_args)
    return (res[0], None) if latent else res


def _with_ones(v):
    return jnp.concatenate([v, jnp.ones_like(v)], axis=1)


def _softmax_pv(scores, values1):
    mx = functools.reduce(jnp.maximum, [jnp.max(s, axis=-1, keepdims=True) for s in scores])
    acc = functools.reduce(lambda a, b: a + b,
                           [_dot(jnp.exp2(s - mx).astype(BF16), v) for s, v in zip(scores, values1)])
    return acc[:, 0:HEAD_W] / acc[:, HEAD_W:HEAD_W + 1]


def _diff_pv(s0, s1, lam, values1):
    return _softmax_pv([s0], [values1]) - lam * _softmax_pv([s1], [values1])


def _lane_lt(shape, n):
    return lax.broadcasted_iota(jnp.int32, shape, len(shape) - 1) < n


def _rms_head(x, gain):
    return _rms(x, gain, HEAD_W)


def _rms_halves(x, gain2):
    lo = _lane_lt(x.shape, DF_DQK)
    sq = x * x
    ss_lo = jnp.sum(jnp.where(lo, sq, 0.0), axis=-1, keepdims=True)
    ss_hi = jnp.sum(sq, axis=-1, keepdims=True) - ss_lo
    inv = jnp.where(lo, lax.rsqrt(ss_lo * (1.0 / DF_DQK) + EPS), lax.rsqrt(ss_hi * (1.0 / DF_DQK) + EPS))
    return x * inv * gain2


def _lambda(lam_ref, lam_init):
    l = lam_ref[...]
    return (jnp.exp(jnp.sum(l[0:1] * l[1:2], axis=-1, keepdims=True))
            - jnp.exp(jnp.sum(l[2:3] * l[3:4], axis=-1, keepdims=True)) + lam_init)


N_CTX_ATTN_INPUTS = 17


def _ctx_attn_body(*refs, lam_init, kv_heads):
    (bq_ref, bk_ref, bv_ref, cq_ref, ck_ref, cv_ref, dq_ref, dk_ref, dv_ref,
     naq_ref, nak_ref, gq_ref, gk_ref, dfq_ref, dfk_ref, sub_ref, lam_ref) = refs[:N_CTX_ATTN_INPUTS]
    ob_ref, oc_ref, od_ref, kb_ref, vb_ref, kc_ref, vc_ref, kd_ref, vd_ref = refs[-9:]
    group = N_HEADS // GQA_KV_HEADS
    scale = LOG2E * HEAD_W ** -0.5
    lam = _lambda(lam_ref, lam_init)
    for n in range(kv_heads):
        kv_lanes = slice(n * HEAD_W, (n + 1) * HEAD_W)
        kc = _rms_head(ck_ref[:, kv_lanes], gk_ref[...])
        vc = cv_ref[:, kv_lanes]
        kc_ref[n] = kc
        vc_ref[n] = vc
        kc16 = kc.astype(BF16)
        vc1 = _with_ones(vc.astype(BF16))
        for g in range(group):
            h = n * group + g
            lanes = slice(h * HEAD_W, (h + 1) * HEAD_W)
            qc = (_rms_head(cq_ref[:, lanes], gq_ref[...]) * scale).astype(BF16)
            oc_ref[:, lanes] = _softmax_pv([_dot_nt(qc, kc16)], [vc1]).astype(oc_ref.dtype)
            kb = _rms_head(bk_ref[:, lanes], nak_ref[...])
            vb = bv_ref[:, lanes]
            kb_ref[h] = kb
            vb_ref[h] = vb
            qb = (_rms_head(bq_ref[:, lanes], naq_ref[...]) * scale).astype(BF16)
            ob_ref[:, lanes] = _softmax_pv([_dot_nt(qb, kb.astype(BF16))],
                                           [_with_ones(vb.astype(BF16))]).astype(ob_ref.dtype)
            kd = _rms_halves(dk_ref[:, lanes], dfk_ref[...])
            vd = dv_ref[:, lanes]
            kd_ref[h, 0] = kd[:, 0:DF_DQK]
            kd_ref[h, 1] = kd[:, DF_DQK:2 * DF_DQK]
            vd_ref[h] = vd
            qd = (_rms_halves(dq_ref[:, lanes], dfq_ref[...]) * (LOG2E * DF_DQK ** -0.5)).astype(BF16)
            lo = _lane_lt(kd.shape, DF_DQK)
            s0 = _dot_nt(qd, jnp.where(lo, kd, 0.0).astype(BF16))
            s1 = _dot_nt(qd, jnp.where(lo, 0.0, kd).astype(BF16))
            od = _diff_pv(s0, s1, lam, _with_ones(vd.astype(BF16)))
            od_ref[:, lanes] = (_rms_head(od, sub_ref[...]) * (1.0 - lam_init)).astype(od_ref.dtype)


def _context_attention(proj, n_seq, seq, gains, lam_init, layer, depth, caches_prev):
    na_qn, na_kn, gqa_qn, gqa_kn, df_qn, df_kn, df_subln, df_lam = gains
    group = N_HEADS // GQA_KV_HEADS
    kvs = CTX_KV_HEADS_PER_STEP
    n_heads = kvs * group
    width = n_heads * HEAD_W

    def heads(cb):
        return pl.BlockSpec((seq, width), lambda b, n, cb=cb: (b, cb // n_heads + n))

    def kv_head(cb):
        return pl.BlockSpec((seq, kvs * HEAD_W), lambda b, n, cb=cb: (b, cb // kvs + n))

    vec = pl.BlockSpec((1, HEAD_W), lambda b, n: (0, 0))
    cache_heads = pl.BlockSpec((None, None, n_heads, seq, HEAD_W), lambda b, n: (b, layer, n, 0, 0))
    cache_kv = pl.BlockSpec((None, None, kvs, seq, HEAD_W), lambda b, n: (b, layer, n, 0, 0))
    mixed = pl.BlockSpec((seq, width), lambda b, n: (b, n))
    mixed_shape = jax.ShapeDtypeStruct((proj.shape[0], GROUP_W), BF16)
    cache4 = jax.ShapeDtypeStruct((n_seq, depth, N_HEADS, seq, HEAD_W), F32)
    cache2 = jax.ShapeDtypeStruct((n_seq, depth, GQA_KV_HEADS, seq, HEAD_W), F32)
    cache_dk = jax.ShapeDtypeStruct((n_seq, depth, N_HEADS, 2, seq, DF_DQK), F32)
    args = [proj] * 9 + [na_qn[None, :], na_kn[None, :], gqa_qn[None, :], gqa_kn[None, :],
                         jnp.tile(df_qn, 2)[None, :], jnp.tile(df_kn, 2)[None, :], df_subln[None, :], df_lam]
    assert len(args) == N_CTX_ATTN_INPUTS
    alias_specs, alias_args, aliases = _alias_kwargs(len(args), caches_prev, 3)
    return pl.pallas_call(
        functools.partial(_ctx_attn_body, lam_init=lam_init, kv_heads=kvs),
        grid=(n_seq, GQA_KV_HEADS // kvs),
        in_specs=[heads(COL_B_Q), heads(COL_B_K), heads(COL_B_V),
                  heads(COL_C_Q), kv_head(COL_C_K), kv_head(COL_C_V),
                  heads(COL_D_Q), heads(COL_D_K), heads(COL_D_V),
                  vec, vec, vec, vec, vec, vec, vec,
                  pl.BlockSpec((4, DF_DQK), lambda b, n: (0, 0))] + alias_specs,
        out_specs=[mixed, mixed, mixed, cache_heads, cache_heads, cache_kv, cache_kv,
                   pl.BlockSpec((None, None, n_heads, 2, seq, DF_DQK), lambda b, n: (b, layer, n, 0, 0, 0)),
                   cache_heads],
        out_shape=[mixed_shape, mixed_shape, mixed_shape, cache4, cache4, cache2, cache2, cache_dk, cache4],
        input_output_aliases=aliases,
        compiler_params=_params("parallel", "parallel"),
        name="context_attention",
    )(*args, *alias_args)


def _rope_tables(n_tokens, rot_dim):
    t = np.arange(n_tokens)
    row = (t // GRID_W).astype(np.float32)
    col = (t % GRID_W).astype(np.float32)
    n_freq = rot_dim // 4
    inv = (np.float32(ROPE_THETA) ** (-np.arange(n_freq, dtype=np.float32) / np.float32(n_freq))).astype(np.float32)
    ang = np.concatenate([row[:, None] * inv, col[:, None] * inv], axis=-1).astype(np.float32)
    cos, sin, zero = np.cos(ang), np.sin(ang), np.zeros_like(ang)
    reps = HEAD_W // rot_dim
    a = np.tile(np.concatenate([cos, cos], axis=-1), (1, reps))
    b = np.tile(np.concatenate([-sin, zero], axis=-1), (1, reps))
    c = np.tile(np.concatenate([zero, sin], axis=-1), (1, reps))
    return jnp.asarray(np.stack([a, b, c]), F32)


def _rope(x, tab_ref, half):
    return (x * tab_ref[0] + pltpu.roll(x, HEAD_W - half, 1) * tab_ref[1]
            + pltpu.roll(x, half, 1) * tab_ref[2])


def _na_body(q_ref, k_ref, v_ref, ck_ref, cv_ref, bias_ref, qn_ref, kn_ref, prev_ref, o_ref,
             qs_ref, ks_ref, vs_ref, *, seq):
    del prev_ref
    rows = seq // GRID_W
    n_win = WIN_ROWS * GRID_W
    qs_ref[...] = (_rms_head(q_ref[...], qn_ref[...]) * (LOG2E * HEAD_W ** -0.5)).astype(BF16)
    ks_ref[...] = _rms_head(k_ref[...], kn_ref[...]).astype(BF16)
    vs_ref[...] = _with_ones(v_ref[...].astype(BF16))
    ck = ck_ref[...].astype(BF16)
    cv = _with_ones(cv_ref[...].astype(BF16))

    def row_step(r, carry):
        start = jnp.clip(r - WIN_ROWS // 2, 0, rows - WIN_ROWS)
        win = pl.ds(pl.multiple_of(start * GRID_W, GRID_W), n_win)
        qrows = pl.ds(pl.multiple_of(r * GRID_W, GRID_W), GRID_W)
        q = qs_ref[qrows, :]
        s_win = _dot_nt(q, ks_ref[win, :]) + bias_ref[start - r + (WIN_ROWS - 1)]
        s_ctx = _dot_nt(q, ck)
        o_ref[qrows, :] = _softmax_pv([s_win, s_ctx], [vs_ref[win, :], cv]).astype(o_ref.dtype)
        return carry

    lax.fori_loop(0, rows, row_step, 0, unroll=min(rows, 16))


def _na_bias(rpb):
    col = np.arange(GRID_W)
    col_start = np.clip(col - WIN_COLS // 2, 0, GRID_W - WIN_COLS)
    col_ok = (col[None, :] >= col_start[:, None]) & (col[None, :] < col_start[:, None] + WIN_COLS)
    dc = np.clip(col[None, :] - col[:, None] + WIN_COLS - 1, 0, 2 * WIN_COLS - 2).reshape(-1)
    onehot = (np.arange(2 * WIN_COLS - 1)[:, None] == dc[None, :]).astype(np.float32)
    per_dr = jnp.einsum('hdc,cn->hdn', rpb.astype(F32), jnp.asarray(onehot), precision=lax.Precision.HIGHEST)
    per_dr = jnp.where(col_ok[None, None], LOG2E * per_dr.reshape(rpb.shape[0], -1, GRID_W, GRID_W), NEG_INF)
    wins = jnp.stack([per_dr[:, o:o + WIN_ROWS] for o in range(WIN_ROWS)], axis=1)
    return wins.transpose(0, 1, 3, 2, 4).reshape(rpb.shape[0], WIN_ROWS, GRID_W, WIN_ROWS * GRID_W)


def _latent_na(proj, row_block0, n_seq, seq, cache_k, cache_v, layer, bias, na_qn, na_kn, mixed_prev):
    past = cache_k.shape[3]

    def col(cb):
        return pl.BlockSpec((seq, HEAD_W), lambda b, h, cb=cb: (row_block0 + b, cb + h))

    cache = pl.BlockSpec((None, None, None, past, HEAD_W), lambda b, h: (b, layer, h, 0, 0))
    vec = pl.BlockSpec((1, HEAD_W), lambda b, h: (0, 0))
    return pl.pallas_call(
        functools.partial(_na_body, seq=seq),
        grid=(n_seq, N_HEADS),
        in_specs=[col(COL_B_Q), col(COL_B_K), col(COL_B_V), cache, cache,
                  pl.BlockSpec((None, WIN_ROWS, GRID_W, WIN_ROWS * GRID_W), lambda b, h: (h, 0, 0, 0)),
                  vec, vec, ANY_SPEC],
        out_specs=pl.BlockSpec((seq, HEAD_W), lambda b, h: (row_block0 + b, h)),
        out_shape=jax.ShapeDtypeStruct(mixed_prev.shape, BF16),
        input_output_aliases={8: 0},
        scratch_shapes=[pltpu.VMEM((seq, HEAD_W), BF16), pltpu.VMEM((seq, HEAD_W), BF16),
                        pltpu.VMEM((seq, 2 * HEAD_W), BF16)],
        compiler_params=_params("parallel", "parallel"),
        name="latent_neighbourhood_attention",
    )(proj, proj, proj, cache_k, cache_v, bias, na_qn[None, :], na_kn[None, :], mixed_prev)


def _gqa_body(q_ref, k_ref, v_ref, ck_ref, cv_ref, rope_ref, qn_ref, kn_ref, prev_ref, o_ref,
              qs_ref, ks_ref, vs_ref, *, seq, tq):
    del prev_ref
    group = N_HEADS // GQA_KV_HEADS
    half = HEAD_W // 2
    ks_ref[0:seq, :] = _rope(_rms_head(k_ref[...], kn_ref[...]), rope_ref, half).astype(BF16)
    ks_ref[seq:, :] = ck_ref[...].astype(BF16)
    vs_ref[0:seq, :] = _with_ones(v_ref[...].astype(BF16))
    vs_ref[seq:, :] = _with_ones(cv_ref[...].astype(BF16))
    for g in range(group):
        q = _rms_head(q_ref[:, g * HEAD_W:(g + 1) * HEAD_W], qn_ref[...]) * (LOG2E * HEAD_W ** -0.5)
        qs_ref[g] = _rope(q, rope_ref, half).astype(BF16)
    kk = ks_ref[...]
    vv = vs_ref[...]
    for g in range(group):
        def q_step(i, carry, g=g):
            qrows = pl.ds(pl.multiple_of(i * tq, tq), tq)
            o = _softmax_pv([_dot_nt(qs_ref[g, qrows, :], kk)], [vv])
            o_ref[qrows, g * HEAD_W:(g + 1) * HEAD_W] = o.astype(o_ref.dtype)
            return carry

        lax.fori_loop(0, seq // tq, q_step, 0, unroll=min(seq // tq, 4))


def _latent_gqa(proj, row_block0, n_seq, seq, cache_k, cache_v, layer, rope, gqa_qn, gqa_kn, mixed_prev):
    past = cache_k.shape[3]
    group = N_HEADS // GQA_KV_HEADS
    tq = 256
    cache = pl.BlockSpec((None, None, None, past, HEAD_W), lambda b, n: (b, layer, n, 0, 0))
    vec = pl.BlockSpec((1, HEAD_W), lambda b, n: (0, 0))
    return pl.pallas_call(
        functools.partial(_gqa_body, seq=seq, tq=tq),
        grid=(n_seq, GQA_KV_HEADS),
        in_specs=[pl.BlockSpec((seq, group * HEAD_W), lambda b, n: (row_block0 + b, COL_C_Q // group + n)),
                  pl.BlockSpec((seq, HEAD_W), lambda b, n: (row_block0 + b, COL_C_K + n)),
                  pl.BlockSpec((seq, HEAD_W), lambda b, n: (row_block0 + b, COL_C_V + n)),
                  cache, cache,
                  pl.BlockSpec((3, seq, HEAD_W), lambda b, n: (0, 0, 0)),
                  vec, vec, ANY_SPEC],
        out_specs=pl.BlockSpec((seq, group * HEAD_W), lambda b, n: (row_block0 + b, n)),
        out_shape=jax.ShapeDtypeStruct(mixed_prev.shape, BF16),
        input_output_aliases={8: 0},
        scratch_shapes=[pltpu.VMEM((group, seq, HEAD_W), BF16),
                        pltpu.VMEM((seq + past, HEAD_W), BF16),
                        pltpu.VMEM((seq + past, 2 * HEAD_W), BF16)],
        compiler_params=_params("parallel", "parallel"),
        name="latent_gqa_attention",
    )(proj, proj, proj, cache_k, cache_v, rope, gqa_qn[None, :], gqa_kn[None, :], mixed_prev)


def _diff_body(q_ref, k_ref, v_ref, ck_ref, cv_ref, rope_ref, qn_ref, kn_ref, sub_ref, lam_ref, prev_ref, o_ref,
               qs_ref, k0_ref, k1_ref, vs_ref, *, seq, tq, lam_init):
    del prev_ref
    half = DF_DQK // 2
    k = _rope(_rms_halves(k_ref[...], kn_ref[...]), rope_ref, half)
    lo = _lane_lt(k.shape, DF_DQK)
    k0_ref[0:seq, :] = jnp.where(lo, k, 0.0).astype(BF16)
    k1_ref[0:seq, :] = jnp.where(lo, 0.0, k).astype(BF16)
    ck = ck_ref[...]
    lo_c = _lane_lt(ck.shape, DF_DQK)
    k0_ref[seq:, :] = jnp.where(lo_c, ck, 0.0).astype(BF16)
    k1_ref[seq:, :] = jnp.where(lo_c, 0.0, ck).astype(BF16)
    vs_ref[0:seq, :] = _with_ones(v_ref[...].astype(BF16))
    vs_ref[seq:, :] = _with_ones(cv_ref[...].astype(BF16))
    q = _rms_halves(q_ref[...], qn_ref[...]) * (LOG2E * DF_DQK ** -0.5)
    qs_ref[...] = _rope(q, rope_ref, half).astype(BF16)
    lam = _lambda(lam_ref, lam_init)
    k0 = k0_ref[...]
    k1 = k1_ref[...]
    vv = vs_ref[...]

    def q_step(i, carry):
        qrows = pl.ds(pl.multiple_of(i * tq, tq), tq)
        qb = qs_ref[qrows, :]
        o = _diff_pv(_dot_nt(qb, k0), _dot_nt(qb, k1), lam, vv)
        o_ref[qrows, :] = (_rms_head(o, sub_ref[...]) * (1.0 - lam_init)).astype(o_ref.dtype)
        return carry

    lax.fori_loop(0, seq // tq, q_step, 0, unroll=min(seq // tq, 4))


def _latent_diff(proj, row_block0, n_seq, seq, cache_k2, cache_v, layer, rope, df_qn, df_kn, df_subln, df_lam,
                 lam_init, mixed_prev):
    past = cache_k2.shape[3]
    tq = 256

    def col(cb):
        return pl.BlockSpec((seq, HEAD_W), lambda b, h, cb=cb: (row_block0 + b, cb + h))

    cache = pl.BlockSpec((None, None, None, past, HEAD_W), lambda b, h: (b, layer, h, 0, 0))
    vec = pl.BlockSpec((1, HEAD_W), lambda b, h: (0, 0))
    kv_scratch = pltpu.VMEM((seq + past, HEAD_W), BF16)
    return pl.pallas_call(
        functools.partial(_diff_body, seq=seq, tq=tq, lam_init=lam_init),
        grid=(n_seq, N_HEADS),
        in_specs=[col(COL_D_Q), col(COL_D_K), col(COL_D_V), cache, cache,
                  pl.BlockSpec((3, seq, HEAD_W), lambda b, h: (0, 0, 0)),
                  vec, vec, vec, pl.BlockSpec((4, DF_DQK), lambda b, h: (0, 0)), ANY_SPEC],
        out_specs=pl.BlockSpec((seq, HEAD_W), lambda b, h: (row_block0 + b, h)),
        out_shape=jax.ShapeDtypeStruct(mixed_prev.shape, BF16),
        input_output_aliases={10: 0},
        scratch_shapes=[pltpu.VMEM((seq, HEAD_W), BF16), kv_scratch, kv_scratch,
                        pltpu.VMEM((seq + past, 2 * HEAD_W), BF16)],
        compiler_params=_params("parallel", "parallel"),
        name="latent_diff_attention",
    )(proj, proj, proj, cache_k2, cache_v, rope, jnp.tile(df_qn, 2)[None, :], jnp.tile(df_kn, 2)[None, :],
      df_subln[None, :], df_lam, mixed_prev)


def _first_max(vals):
    best = vals[0]
    idx = jnp.zeros(best.shape, jnp.int32)
    for i in range(1, len(vals)):
        better = vals[i] > best
        best = jnp.where(better, vals[i], best)
        idx = jnp.where(better, i, idx)
    return best, idx


def _pick(vals, idx):
    out = vals[0]
    for i in range(1, len(vals)):
        out = jnp.where(idx == i, vals[i], out)
    return out


def _outproj_body(ma_ref, mb_ref, mc_ref, md_ref, w_ref, hc_ref, hl_ref, mod_ref, n2_ref, rw_ref, rb_ref,
                  h1_ref, x2_ref, idx_ref, gate_ref, mixed_ref, *, ctx_tiles):
    for g, m_ref in enumerate((ma_ref, mb_ref, mc_ref, md_ref)):
        mixed_ref[:, g * GROUP_W:(g + 1) * GROUP_W] = m_ref[...]
    y = mod_ref[2:3, :] * _dot(mixed_ref[...], w_ref[...])

    def residual(h_ref):
        h1_ref[...] = h_ref[...] + y

    is_ctx = pl.program_id(0) < ctx_tiles
    pl.when(is_ctx)(lambda: residual(hc_ref))
    pl.when(jnp.logical_not(is_ctx))(lambda: residual(hl_ref))
    h1 = h1_ref[...]
    x2 = _rms(h1, n2_ref[...], D_MODEL) * (1.0 + mod_ref[4:5, :]) + mod_ref[3:4, :]
    x2_ref[...] = _pack_bf16_pairs(x2)
    x_hi = x2.astype(BF16)
    x_lo = (x2 - x_hi.astype(F32)).astype(BF16)
    acc = _dot(x_hi, rw_ref[...])
    logits_tok = acc[:, 0:HEAD_W] + acc[:, HEAD_W:2 * HEAD_W] + _dot(x_lo, rw_ref[:, 0:HEAD_W])
    logits = logits_tok.T[0:N_EXPERTS, :]
    aff_all = _sigmoid(logits)
    sel_all = aff_all + rb_ref[...]
    aff = [aff_all[e:e + 1, :] for e in range(N_EXPERTS)]
    sel = [sel_all[e:e + 1, :] for e in range(N_EXPERTS)]
    neg = jnp.full(sel[0].shape, -jnp.inf, F32)
    scores = []
    for g in range(N_EXP_GROUPS):
        grp = sel[g * EXP_PER_GROUP:(g + 1) * EXP_PER_GROUP]
        m1, i1 = _first_max(grp)
        m2, _ = _first_max([jnp.where(i1 == j, neg, grp[j]) for j in range(EXP_PER_GROUP)])
        scores.append(m1 + m2)
    _, g_best = _first_max(scores)
    in_sel = [_pick([sel[g * EXP_PER_GROUP + j] for g in range(N_EXP_GROUPS)], g_best)
              for j in range(EXP_PER_GROUP)]
    in_aff = [_pick([aff[g * EXP_PER_GROUP + j] for g in range(N_EXP_GROUPS)], g_best)
              for j in range(EXP_PER_GROUP)]
    _, l1 = _first_max(in_sel)
    _, l2 = _first_max([jnp.where(l1 == j, neg, in_sel[j]) for j in range(EXP_PER_GROUP)])
    w1 = _pick(in_aff, l1)
    w2 = _pick(in_aff, l2)
    idx_ref[0:1, :] = g_best * EXP_PER_GROUP + l1
    idx_ref[1:2, :] = g_best * EXP_PER_GROUP + l2
    gate_ref[0:1, :] = w1 / (w1 + w2)
    gate_ref[1:2, :] = w2 / (w1 + w2)


def _output_projection(mixed4, w_out_bf16, h_ctx, h_lat, mod, norm2, router_w, router_b, cond_of_tile, tm):
    d = h_ctx.shape[1]
    t = h_ctx.shape[0] + h_lat.shape[0]
    ctx_tiles = h_ctx.shape[0] // tm
    slab = pl.BlockSpec((tm, GROUP_W), lambda i: (i, 0))
    rw_hi = router_w.astype(BF16)
    rw_lo = (router_w - rw_hi.astype(F32)).astype(BF16)
    pad = ((0, 0), (0, HEAD_W - N_EXPERTS))
    router_split = jnp.concatenate([jnp.pad(rw_hi, pad), jnp.pad(rw_lo, pad)], axis=1)
    return pl.pallas_call(
        functools.partial(_outproj_body, ctx_tiles=ctx_tiles),
        grid=(t // tm,),
        in_specs=[slab, slab, slab, slab,
                  pl.BlockSpec((d, d), lambda i: (0, 0))] + _group_specs(tm, d, ctx_tiles) + [
                  pl.BlockSpec((None, 6, d), lambda i: (cond_of_tile(i), 0, 0)),
                  pl.BlockSpec((1, d), lambda i: (0, 0)),
                  pl.BlockSpec((d, 2 * HEAD_W), lambda i: (0, 0)),
                  pl.BlockSpec((N_EXPERTS, 1), lambda i: (0, 0))],
        out_specs=[pl.BlockSpec((tm, d), lambda i: (i, 0)),
                   pl.BlockSpec((tm, d // 2), lambda i: (i, 0)),
                   pl.BlockSpec((2, tm), lambda i: (0, i)),
                   pl.BlockSpec((2, tm), lambda i: (0, i))],
        out_shape=[jax.ShapeDtypeStruct((t, d), F32), jax.ShapeDtypeStruct((t, d // 2), jnp.uint32),
                   jax.ShapeDtypeStruct((2, t), jnp.int32), jax.ShapeDtypeStruct((2, t), F32)],
        scratch_shapes=[pltpu.VMEM((tm, d), BF16)],
        compiler_params=_params("parallel", vmem=VMEM_LIMIT_PROJ),
        name="out_proj_residual_router",
    )(*mixed4, w_out_bf16, h_ctx, h_lat, mod, norm2[None, :], router_split, router_b[:, None])


def _vmem_row(ref, base, u):
    return ref.at[pl.ds(base, SUBLANES), :].at[pl.ds(u, 1), :]


def _for_row_groups(n_rows, fn):
    def body(g, carry):
        base = pl.multiple_of(g * SUBLANES, SUBLANES)
        for u in range(SUBLANES):
            fn(base, u)
        return carry

    lax.fori_loop(0, n_rows // SUBLANES, body, 0)


def _rows_wait(src_hbm, dst, sem, n_rows):
    pltpu.make_async_copy(src_hbm.at[pl.ds(0, n_rows), :], dst.at[pl.ds(0, n_rows), :], sem).wait()


def _dispatch_body(dest_ref, ps_ref, pe_ref, x_ref, xs_hbm, zero_ref, sem, zsem, *, n_tok, tile):
    i = pl.program_id(0)

    def zero_copy(e):
        first = pl.multiple_of(pe_ref[e] - MOE_ROWS, MOE_ROWS)
        return pltpu.make_async_copy(zero_ref, xs_hbm.at[pl.ds(first, MOE_ROWS), :], zsem)

    @pl.when(i == 0)
    def _():
        zero_ref[...] = jnp.zeros(zero_ref.shape, zero_ref.dtype)
        for e in range(N_EXPERTS):
            @pl.when(pe_ref[e] > ps_ref[e])
            def _(e=e):
                zero_copy(e).start()
        for e in range(N_EXPERTS):
            @pl.when(pe_ref[e] > ps_ref[e])
            def _(e=e):
                zero_copy(e).wait()

    for k in range(2):
        def scatter_row(base, u, k=k):
            row = dest_ref[k * n_tok + i * tile + base + u]
            pltpu.make_async_copy(_vmem_row(x_ref, base, u), xs_hbm.at[pl.ds(row, 1), :], sem).start()

        _for_row_groups(tile, scatter_row)
    for k in range(2):
        _rows_wait(x_ref, xs_hbm, sem, tile)


def _dispatch(x2, dest, pad_start, pad_end, n_rows):
    t, d = x2.shape
    tile = TOKEN_TILE
    grid_spec = pltpu.PrefetchScalarGridSpec(
        num_scalar_prefetch=3,
        grid=(t // tile,),
        in_specs=[pl.BlockSpec((tile, d), lambda i, dst, ps, pe: (i, 0))],
        out_specs=ANY_SPEC,
        scratch_shapes=[pltpu.VMEM((MOE_ROWS, d), x2.dtype), pltpu.SemaphoreType.DMA, pltpu.SemaphoreType.DMA],
    )
    return pl.pallas_call(
        functools.partial(_dispatch_body, n_tok=t, tile=tile),
        grid_spec=grid_spec,
        out_shape=jax.ShapeDtypeStruct((n_rows, d), x2.dtype),
        compiler_params=_params("arbitrary"),
        name="moe_dispatch",
    )(dest, pad_start, pad_end, x2)


def _expert_body(be_ref, nb_ref, nx_ref, x_ref, wg_hbm, wu_hbm, wd_hbm, o_ref,
                 stage_g, stage_u, stage_d, wg_bf, wu_bf, wd_bf, sem, *, layer):
    i = pl.program_id(0)

    def weight_copies(e):
        return (pltpu.make_async_copy(wg_hbm.at[layer, e], stage_g, sem.at[0]),
                pltpu.make_async_copy(wu_hbm.at[layer, e], stage_u, sem.at[1]),
                pltpu.make_async_copy(wd_hbm.at[layer, e], stage_d, sem.at[2]))

    @pl.when(i < nb_ref[0])
    def _():
        e = be_ref[i]

        @pl.when(i == 0)
        def _():
            for c in weight_copies(e):
                c.start()

        @pl.when((i == 0) | (e != be_ref[jnp.maximum(i - 1, 0)]))
        def _():
            for c in weight_copies(e):
                c.wait()
            wg_bf[...] = stage_g[...].astype(BF16)
            wu_bf[...] = stage_u[...].astype(BF16)
            wd_bf[...] = stage_d[...].astype(BF16)
            nxt = nx_ref[e]

            @pl.when(nxt < N_EXPERTS)
            def _():
                for c in weight_copies(nxt):
                    c.start()

        x = _unpack_bf16_pairs(x_ref[...]).astype(BF16)
        hdn = _silu(_dot(x, wg_bf[...])) * _dot(x, wu_bf[...])
        o_ref[...] = _pack_bf16_pairs(_dot(hdn.astype(BF16), wd_bf[...]))

    @pl.when(i >= nb_ref[0])
    def _():
        o_ref[...] = jnp.zeros(o_ref.shape, o_ref.dtype)


def _expert_blocks(xs, block_expert, n_used, next_expert, w_gate, w_up, w_down, layer):
    n_rows, half = xs.shape
    d = 2 * half
    ff = w_gate.shape[-1]
    grid_spec = pltpu.PrefetchScalarGridSpec(
        num_scalar_prefetch=3,
        grid=(n_rows // MOE_ROWS,),
        in_specs=[pl.BlockSpec((MOE_ROWS, half), lambda i, be, nb, nx: (jnp.minimum(i, nb[0] - 1), 0)),
                  ANY_SPEC, ANY_SPEC, ANY_SPEC],
        out_specs=pl.BlockSpec((MOE_ROWS, half), lambda i, be, nb, nx: (i, 0)),
        scratch_shapes=[pltpu.VMEM((d, ff), F32), pltpu.VMEM((d, ff), F32), pltpu.VMEM((ff, d), F32),
                        pltpu.VMEM((d, ff), BF16), pltpu.VMEM((d, ff), BF16), pltpu.VMEM((ff, d), BF16),
                        pltpu.SemaphoreType.DMA((3,))],
    )
    return pl.pallas_call(
        functools.partial(_expert_body, layer=layer),
        grid_spec=grid_spec,
        out_shape=jax.ShapeDtypeStruct((n_rows, half), jnp.uint32),
        compiler_params=_params("arbitrary"),
        name="moe_expert_blocks",
    )(block_expert, n_used, next_expert, xs, w_gate, w_up, w_down)


def _combine_body(dest_ref, h_ref, gate_ref, y_hbm, mod_ref, *rest, n_tok, tile, ctx_tiles):
    out_refs, (ybuf, sem) = rest[:-2], rest[-2:]
    i = pl.program_id(0)
    n_tiles = pl.num_programs(0)

    def start(blk, slot):
        for k in range(2):
            def gather_row(base, u, k=k):
                row = dest_ref[k * n_tok + blk * tile + base + u]
                pltpu.make_async_copy(y_hbm.at[pl.ds(row, 1), :], _vmem_row(ybuf.at[slot, k], base, u),
                                      sem.at[slot]).start()

            _for_row_groups(tile, gather_row)

    @pl.when(i == 0)
    def _():
        start(0, 0)

    @pl.when(i + 1 < n_tiles)
    def _():
        start(i + 1, (i + 1) % 2)

    slot = i % 2
    for k in range(2):
        _rows_wait(y_hbm, ybuf.at[slot, k], sem.at[slot], tile)
    gate = gate_ref[...]
    y0 = _unpack_bf16_pairs(ybuf[slot, 0])
    y1 = _unpack_bf16_pairs(ybuf[slot, 1])
    out = h_ref[...] + mod_ref[5:6, :] * (gate[:, 0:1] * y0 + gate[:, 1:2] * y1)
    @pl.when(i < ctx_tiles)
    def _():
        out_refs[0][...] = out

    @pl.when(i >= ctx_tiles)
    def _():
        out_refs[1][...] = out


def _combine(h1, yb, dest, gates, mod, cond_of_tile, tile, split_rows):
    t, d = h1.shape
    row_tile = pl.BlockSpec((tile, d), lambda i, dst: (i, 0))
    ctx_tiles = split_rows // tile
    out_specs = _group_specs(tile, d, ctx_tiles)
    out_shape = [jax.ShapeDtypeStruct((split_rows, d), F32), jax.ShapeDtypeStruct((t - split_rows, d), F32)]
    grid_spec = pltpu.PrefetchScalarGridSpec(
        num_scalar_prefetch=1,
        grid=(t // tile,),
        in_specs=[row_tile,
                  pl.BlockSpec((tile, 2), lambda i, dst: (i, 0)),
                  ANY_SPEC,
                  pl.BlockSpec((None, 6, d), lambda i, dst: (cond_of_tile(i), 0, 0))],
        out_specs=out_specs,
        scratch_shapes=[pltpu.VMEM((2, 2, tile, yb.shape[1]), yb.dtype), pltpu.SemaphoreType.DMA((2,))],
    )
    return pl.pallas_call(
        functools.partial(_combine_body, n_tok=t, tile=tile, ctx_tiles=ctx_tiles),
        grid_spec=grid_spec,
        out_shape=out_shape,
        compiler_params=_params("arbitrary"),
        name="moe_gated_residual",
    )(dest, h1, gates, yb, mod)


def _moe(h1, x2, idx_t, gate_t, mod, w_gate, w_up, w_down, layer, cond_of_tile, split_rows):
    t, d = h1.shape
    n = 2 * t
    experts = idx_t.reshape(n)
    onehot = (experts[:, None] == jnp.arange(N_EXPERTS, dtype=jnp.int32)[None, :]).astype(BF16)
    blocks = onehot.reshape(n // TOKEN_TILE, TOKEN_TILE, N_EXPERTS)
    tri = jnp.asarray(np.tril(np.ones((TOKEN_TILE, TOKEN_TILE), np.float32)), BF16)
    within = jnp.einsum('ij,bjk->bik', tri, blocks, preferred_element_type=F32)
    block_total = within[:, -1, :]
    block_first = jnp.cumsum(block_total, axis=0) - block_total
    counts = (block_first[-1] + block_total[-1]).astype(jnp.int32)
    before = (within + block_first[:, None, :]).reshape(n, N_EXPERTS) - 1.0
    rank = jnp.sum(before * onehot.astype(F32), axis=1).astype(jnp.int32)
    padded = (counts + MOE_ROWS - 1) // MOE_ROWS * MOE_ROWS
    pad_end = jnp.cumsum(padded).astype(jnp.int32)
    pad_start = pad_end - padded
    dest = (pad_start[experts] + rank).astype(jnp.int32)
    n_blocks = (n + N_EXPERTS * (MOE_ROWS - 1) + MOE_ROWS - 1) // MOE_ROWS
    block_first_row = jnp.arange(n_blocks, dtype=jnp.int32) * MOE_ROWS
    block_expert = jnp.minimum(jnp.sum((pad_end[None, :] <= block_first_row[:, None]).astype(jnp.int32), axis=1),
                               N_EXPERTS - 1)
    n_used = (pad_end[-1:] // MOE_ROWS).astype(jnp.int32)
    xs = _dispatch(x2, dest, pad_start, pad_end, n_blocks * MOE_ROWS)
    ids = jnp.arange(N_EXPERTS, dtype=jnp.int32)
    later_with_rows = (counts[None, :] > 0) & (ids[None, :] > ids[:, None])
    next_expert = jnp.min(jnp.where(later_with_rows, ids[None, :], N_EXPERTS), axis=1).astype(jnp.int32)
    yb = _expert_blocks(xs, block_expert, n_used, next_expert, w_gate, w_up, w_down, layer)
    return _combine(h1, yb, dest, gate_t.T, mod, cond_of_tile, TOKEN_TILE, split_rows)


def kernel(x_prompt, x_sample, cache_na_k, cache_na_v, cache_gqa_k, cache_gqa_v, cache_diff_k, cache_diff_v, state_hgrn, c, c_ctx, w_mod, b_mod, norm1, norm2, w_in, w_out, hg_lb_logits, hg_onorm, na_qn, na_kn, na_rpb, gqa_qn, gqa_kn, df_qn, df_kn, df_lam, df_subln, router_w, router_b, w_gate, w_up, w_down):
    n_ctx, ctx_len, d = x_prompt.shape
    n_lat, lat_len, _ = x_sample.shape
    depth = w_in.shape[0]
    t_ctx = n_ctx * ctx_len
    assert t_ctx % lat_len == 0 and lat_len % GRID_W == 0 and lat_len // GRID_W >= WIN_ROWS
    tm = next(m for m in (1024, 512, 256) if t_ctx % m == 0 and lat_len % m == 0)
    tm2 = min(tm, 512)
    lat_block0 = t_ctx // lat_len

    def cond_tile(tile_rows):
        def cond_of_tile(i):
            return jnp.where(i < t_ctx // tile_rows, 0, 1 + (i - t_ctx // tile_rows) // (lat_len // tile_rows))
        return cond_of_tile

    sm = jax.nn.softmax(hg_lb_logits.astype(F32), axis=0)
    lower = jnp.cumsum(sm, axis=0) - sm[0:1]
    mod_all = _modulation(jnp.concatenate([c_ctx[None, :], c], axis=0), w_mod, b_mod)
    mod_all = mod_all.reshape(depth, 1 + n_lat, 6, d)
    hgrn_consts = _hgrn_constants(HGRN_CHUNK)
    rope_c = _rope_tables(lat_len, HEAD_W)
    rope_d = _rope_tables(lat_len, DF_DQK)
    past = cache_diff_k.shape[4]
    cache_diff_k2 = cache_diff_k.transpose(0, 1, 2, 4, 3, 5).reshape(n_lat, depth, N_HEADS, past, HEAD_W)

    h_ctx, h_lat = x_prompt.reshape(t_ctx, d), x_sample.reshape(n_lat * lat_len, d)
    caches, states = [], None
    for layer in range(depth):
        mod = mod_all[layer]
        lam_init = 0.8 - 0.6 * math.exp(-0.3 * layer)
        proj = _input_projection(h_ctx, h_lat, mod, norm1[layer], w_in[layer].astype(BF16), cond_tile(tm), tm)
        mix_a, states = _hgrn(proj, 0, n_ctx, ctx_len, lower[layer], hg_onorm[layer], hgrn_consts, None, None,
                              states, layer, depth)
        gains = (na_qn[layer], na_kn[layer], gqa_qn[layer], gqa_kn[layer], df_qn[layer], df_kn[layer],
                 df_subln[layer], df_lam[layer])
        mix_b, mix_c, mix_d, *caches = _context_attention(proj, n_ctx, ctx_len, gains, lam_init, layer, depth,
                                                          caches)
        mix_a, _ = _hgrn(proj, lat_block0, n_lat, lat_len, lower[layer], hg_onorm[layer], hgrn_consts,
                         state_hgrn, mix_a, None, layer, depth)
        mix_b = _latent_na(proj, lat_block0, n_lat, lat_len, cache_na_k, cache_na_v, layer,
                           _na_bias(na_rpb[layer]), na_qn[layer], na_kn[layer], mix_b)
        mix_c = _latent_gqa(proj, lat_block0, n_lat, lat_len, cache_gqa_k, cache_gqa_v, layer, rope_c,
                            gqa_qn[layer], gqa_kn[layer], mix_c)
        mix_d = _latent_diff(proj, lat_block0, n_lat, lat_len, cache_diff_k2, cache_diff_v, layer, rope_d,
                             df_qn[layer], df_kn[layer], df_subln[layer], df_lam[layer], lam_init, mix_d)
        h1, x2, idx_t, gate_t = _output_projection((mix_a, mix_b, mix_c, mix_d), w_out[layer].astype(BF16), h_ctx,
                                                   h_lat, mod, norm2[layer], router_w, router_b, cond_tile(tm2), tm2)
        h_ctx, h_lat = _moe(h1, x2, idx_t, gate_t, mod, w_gate, w_up, w_down, layer, cond_tile(TOKEN_TILE), t_ctx)
    y_prompt = h_ctx.reshape(n_ctx, ctx_len, d)
    y_sample = h_lat.reshape(n_lat, lat_len, d)
    return (y_prompt, y_sample, *caches, states)
```

```python
import functools
import math

import numpy as np
import jax
import jax.numpy as jnp
from jax import lax
from jax.experimental import pallas as pl
from jax.experimental.pallas import tpu as pltpu

D_MODEL = 2048
GRID_W = 64
GROUP_W = D_MODEL // 4
N_HEADS = 4
HEAD_W = GROUP_W // N_HEADS
SUBLANES = 8
GQA_KV_HEADS = 2
DF_DQK = HEAD_W // 2
WIN_ROWS = 8
WIN_COLS = 16
N_EXPERTS = 16
N_EXP_GROUPS = 4
EXP_PER_GROUP = N_EXPERTS // N_EXP_GROUPS
EXPERT_FF = D_MODEL // 4
ROPE_THETA = 10000.0
EPS = 1e-6
NEG_INF = -1e30
LOG2E = math.log2(math.e)
IN_WIDTH = 13 * GROUP_W

COL_A_Q, COL_A_FF, COL_A_FB, COL_A_I, COL_A_G = 0, 4, 8, 12, 16
COL_B_Q, COL_B_K, COL_B_V = 20, 24, 28
COL_C_Q, COL_C_K, COL_C_V = 32, 36, 38
COL_D_Q, COL_D_K, COL_D_V = 40, 44, 48

HGRN_CHUNK = 128
HGRN_UNROLL = 4
HGRN_INPUT_VMEM = 24 * 1024 * 1024
CTX_KV_HEADS_PER_STEP = 1
MOE_ROWS = 256
TOKEN_TILE = 512
VMEM_LIMIT = 48 * 1024 * 1024
VMEM_LIMIT_PROJ = 56 * 1024 * 1024

F32 = jnp.float32
BF16 = jnp.bfloat16
ANY_SPEC = pl.BlockSpec(memory_space=pl.ANY)


def _params(*sem, vmem=VMEM_LIMIT):
    return pltpu.CompilerParams(dimension_semantics=sem, vmem_limit_bytes=vmem)


def _sigmoid(x):
    return 1.0 / (1.0 + jnp.exp(-x))


def _silu(x):
    return x * _sigmoid(x)


def _rms(x, gain, n):
    return x * lax.rsqrt(jnp.sum(x * x, axis=-1, keepdims=True) * (1.0 / n) + EPS) * gain


def _dot(a, b):
    return jnp.dot(a, b, preferred_element_type=F32)


def _dot_nt(a, b):
    return lax.dot_general(a, b, (((1,), (1,)), ((), ())), preferred_element_type=F32)


def _dot_tn(a, b):
    return lax.dot_general(a, b, (((0,), (0,)), ((), ())), preferred_element_type=F32)


def _pack_bf16_pairs(x):
    k = x.shape[1] // 2
    lo = lax.bitcast_convert_type(x[:, :k].astype(BF16).astype(F32), jnp.uint32) >> 16
    hi = lax.bitcast_convert_type(x[:, k:].astype(BF16).astype(F32), jnp.uint32)
    return hi | lo


def _unpack_bf16_pairs(w):
    lo = lax.bitcast_convert_type(w << 16, F32)
    hi = lax.bitcast_convert_type(w & jnp.uint32(0xFFFF0000), F32)
    return jnp.concatenate([lo, hi], axis=1)


def _aligned(x, m):
    return x if isinstance(x, int) else pl.multiple_of(x, m)


def _alias_kwargs(n_inputs, prev, first_out):
    return ([ANY_SPEC] * len(prev), list(prev), {n_inputs + k: first_out + k for k in range(len(prev))})


def _mod_body(cond_ref, w_ref, b_ref, o_ref):
    w = w_ref[...]
    for c in range(cond_ref.shape[0]):
        s = _silu(cond_ref[c])
        o_ref[c:c + 1, :] = jnp.sum(w * s, axis=0, keepdims=True) + b_ref[...]


def _modulation(cond, w_mod, b_mod):
    depth, d, n6 = w_mod.shape
    nc = cond.shape[0]
    tn = 1024
    return pl.pallas_call(
        _mod_body,
        grid=(depth, n6 // tn),
        in_specs=[pl.BlockSpec((nc, d, 1), lambda l, j: (0, 0, 0)),
                  pl.BlockSpec((None, d, tn), lambda l, j: (l, 0, j)),
                  pl.BlockSpec((None, 1, tn), lambda l, j: (l, 0, j))],
        out_specs=pl.BlockSpec((None, nc, tn), lambda l, j: (l, 0, j)),
        out_shape=jax.ShapeDtypeStruct((depth, nc, n6), F32),
        compiler_params=_params("parallel", "parallel"),
        name="adaln_modulation",
    )(cond[:, :, None], w_mod, b_mod[:, None, :])


def _group_specs(tile, d, ctx_tiles):
    return [pl.BlockSpec((tile, d), lambda i, *_: (jnp.minimum(i, ctx_tiles - 1), 0)),
            pl.BlockSpec((tile, d), lambda i, *_: (jnp.maximum(i - ctx_tiles, 0), 0))]


def _inproj_body(hc_ref, hl_ref, mod_ref, n1_ref, w_ref, o_ref, xn_ref, *, ctx_tiles):
    def normalise(h_ref):
        y = _rms(h_ref[...], n1_ref[...], D_MODEL)
        xn_ref[...] = (y * (1.0 + mod_ref[1:2, :]) + mod_ref[0:1, :]).astype(BF16)

    first = pl.program_id(1) == 0
    is_ctx = pl.program_id(0) < ctx_tiles
    pl.when(first & is_ctx)(lambda: normalise(hc_ref))
    pl.when(first & jnp.logical_not(is_ctx))(lambda: normalise(hl_ref))
    o_ref[...] = _dot(xn_ref[...], w_ref[...])


def _input_projection(h_ctx, h_lat, mod, norm1, w_in_bf16, cond_of_tile, tm):
    d = h_ctx.shape[1]
    t = h_ctx.shape[0] + h_lat.shape[0]
    ctx_tiles = h_ctx.shape[0] // tm
    n = w_in_bf16.shape[1]
    tn = 512
    return pl.pallas_call(
        functools.partial(_inproj_body, ctx_tiles=ctx_tiles),
        grid=(t // tm, n // tn),
        in_specs=_group_specs(tm, d, ctx_tiles) + [
                  pl.BlockSpec((None, 6, d), lambda i, j: (cond_of_tile(i), 0, 0)),
                  pl.BlockSpec((1, d), lambda i, j: (0, 0)),
                  pl.BlockSpec((d, tn), lambda i, j: (0, j))],
        out_specs=pl.BlockSpec((tm, tn), lambda i, j: (i, j)),
        out_shape=jax.ShapeDtypeStruct((t, n), F32),
        scratch_shapes=[pltpu.VMEM((tm, d), BF16)],
        compiler_params=_params("parallel", "arbitrary", vmem=VMEM_LIMIT_PROJ),
        name="norm_modulate_in_proj",
    )(h_ctx, h_lat, mod, norm1[None, :], w_in_bf16)


def _hgrn_constants(c):
    nl = int(math.log2(c))
    idx = np.arange(c)
    e = np.zeros((nl + 2, c, c), np.float32)
    m = np.zeros((nl + 1, c, c), np.float32)
    e[0] = idx[None, :] <= idx[:, None]
    e[1] = idx[None, :] > idx[:, None]
    m[0] = np.eye(c)
    for li in range(nl):
        s = c >> (li + 1)
        parent = idx // (2 * s)
        right = (idx % (2 * s)) >= s
        ref = parent * 2 * s + s - 1
        for i in range(c):
            if right[i]:
                e[2 + li, i, ref[i] + 1:i + 1] = 1.0
            else:
                e[2 + li, i, i + 1:ref[i] + 1] = 1.0
        m[1 + li] = right[:, None] & ~right[None, :] & (parent[:, None] == parent[None, :])
    keep = [0, 1] + [2 + li for li in range(nl) if (c >> (li + 1)) < SUBLANES]
    e = e[keep]
    e2 = np.stack([e, e[:, ::-1, ::-1]]).reshape(2, len(keep) * c, c)
    m2 = np.stack([m, m[:, ::-1, ::-1]])
    return jnp.asarray(e2, BF16), jnp.asarray(m2, F32)


def _hgrn_body(*refs, seq, chunk, unroll, heads, has_s0, emit_state, n_alias):
    q_ref, ff_ref, fb_ref, i_ref, g_ref, lb_ref, on_ref, e_ref, m_ref = refs[:9]
    pos = 9
    s0_ref = None
    if has_s0:
        s0_ref = refs[pos]
        pos += 1
    pos += n_alias
    o_ref = refs[pos]
    pos += 1
    if emit_state:
        st_ref = refs[pos]
        pos += 1
    of_ref, ob_ref = refs[pos], refs[pos + 1]
    c = chunk
    n_chunks = seq // c
    assert seq % c == 0 and n_chunks % unroll == 0
    n_levels = m_ref.shape[1] - 1
    gate_refs = (ff_ref, fb_ref)
    out_refs = (of_ref, ob_ref)

    def chunk_step(c0, d, hh, st):
        rows = pl.ds(c0, c)
        lanes = slice(hh * HEAD_W, (hh + 1) * HEAD_W)
        lb = lb_ref[d:d + 1, lanes]
        f = lb + (1.0 - lb) * _sigmoid(gate_refs[d][rows, lanes])
        g = jnp.log2(f)
        k = 1.0 - f
        q = _silu(q_ref[rows, lanes])
        v = i_ref[rows, lanes].astype(BF16)
        g_hi = g.astype(BF16)
        g_lo = (g - g_hi.astype(F32)).astype(BF16)
        g2 = _dot(e_ref[d], jnp.concatenate([g_hi, g_lo], axis=1))
        gsum = g2[:, 0:HEAD_W] + g2[:, HEAD_W:2 * HEAD_W]
        cum = gsum[0:c]
        x_cum = jnp.exp2(cum)
        x_tail = jnp.exp2(gsum[c:2 * c])
        row = lax.broadcasted_iota(jnp.int32, (c, HEAD_W), 0)
        s = m_ref[d, 0] * _dot_nt(q.astype(BF16), k.astype(BF16))
        n_matmul_levels = 0
        for lv in range(n_levels):
            half = c >> (lv + 1)
            if half >= SUBLANES:
                ref_row = half - 1 if d == 0 else half
                ref = jnp.concatenate(
                    [jnp.broadcast_to(cum[b * 2 * half + ref_row:b * 2 * half + ref_row + 1, :], (2 * half, HEAD_W))
                     for b in range(c // (2 * half))], axis=0)
                near = ((row % (2 * half)) >= half) == (d == 0)
                x_l = jnp.exp2(jnp.where(near, cum - ref, ref - cum))
            else:
                x_l = jnp.exp2(gsum[(2 + n_matmul_levels) * c:(3 + n_matmul_levels) * c])
                n_matmul_levels += 1
            s = s + m_ref[d, 1 + lv] * _dot_nt((q * x_l).astype(BF16), (k * x_l).astype(BF16))
        o = _dot_nt((q * x_cum).astype(BF16), st.astype(BF16)) + _dot(s.astype(BF16), v)
        out_refs[d][rows, lanes] = o
        total = x_cum[c - 1:c, :] if d == 0 else x_cum[0:1, :]
        return st * total + _dot_tn(v, (k * x_tail).astype(BF16))

    if has_s0:
        states0 = tuple(s0_ref[d, hh].T for hh in range(heads) for d in range(2))
    else:
        states0 = tuple(jnp.zeros((HEAD_W, HEAD_W), F32) for _ in range(2 * heads))

    def loop(t, states):
        states = list(states)
        for u in range(unroll):
            j = t * unroll + u
            for hh in range(heads):
                states[2 * hh] = chunk_step(_aligned(j * c, c), 0, hh, states[2 * hh])
                states[2 * hh + 1] = chunk_step(_aligned((n_chunks - 1 - j) * c, c), 1, hh, states[2 * hh + 1])
        return tuple(states)

    if n_chunks == unroll:
        states = loop(0, states0)
    else:
        states = lax.fori_loop(0, n_chunks // unroll, loop, states0)
    for hh in range(heads):
        lanes = slice(hh * HEAD_W, (hh + 1) * HEAD_W)
        o = of_ref[:, lanes] + ob_ref[:, lanes]
        o_ref[:, lanes] = (_rms(o, on_ref[...], HEAD_W) * _silu(g_ref[:, lanes])).astype(o_ref.dtype)
        if emit_state:
            st_ref[0, hh] = states[2 * hh].T
            st_ref[1, hh] = states[2 * hh + 1].T


def _hgrn(proj, row_block0, n_seq, seq, lower, onorm, consts, s0, mixed_prev, state_prev, layer, depth):
    e_mat, masks = consts
    latent = s0 is not None
    hps = next(n for n in (4, 2, 1) if 5 * 2 * seq * n * HEAD_W * 4 <= HGRN_INPUT_VMEM)
    width = hps * HEAD_W

    def col(cb):
        return pl.BlockSpec((seq, width), lambda b, h, cb=cb: (row_block0 + b, cb // hps + h))

    state_spec = pl.BlockSpec((None, None, 2, hps, HEAD_W, HEAD_W), lambda b, h: (b, layer, 0, h, 0, 0))
    in_specs = [col(COL_A_Q), col(COL_A_FF), col(COL_A_FB), col(COL_A_I), col(COL_A_G),
                pl.BlockSpec((2, width), lambda b, h: (0, h)),
                pl.BlockSpec((1, HEAD_W), lambda b, h: (0, 0)),
                pl.BlockSpec(e_mat.shape, lambda b, h: (0, 0, 0)),
                pl.BlockSpec(masks.shape, lambda b, h: (0, 0, 0, 0))]
    args = [proj, proj, proj, proj, proj, lower, onorm[None, :], e_mat, masks]
    if latent:
        in_specs.append(state_spec)
        args.append(s0)
        prev = [mixed_prev]
    else:
        prev = [] if state_prev is None else [state_prev]
    alias_specs, alias_args, aliases = _alias_kwargs(len(args), prev, 0 if latent else 1)
    out_specs = [pl.BlockSpec((seq, width), lambda b, h: (row_block0 + b, h))]
    out_shape = [jax.ShapeDtypeStruct((proj.shape[0], GROUP_W), BF16)]
    if not latent:
        out_specs.append(state_spec)
        out_shape.append(jax.ShapeDtypeStruct((n_seq, depth, 2, N_HEADS, HEAD_W, HEAD_W), F32))
    res = pl.pallas_call(
        functools.partial(_hgrn_body, seq=seq, chunk=HGRN_CHUNK, unroll=min(HGRN_UNROLL, seq // HGRN_CHUNK),
                          heads=hps, has_s0=latent,
                          emit_state=not latent, n_alias=len(prev)),
        grid=(n_seq, N_HEADS // hps),
        in_specs=in_specs + alias_specs, out_specs=out_specs, out_shape=out_shape,
        input_output_aliases=aliases,
        scratch_shapes=[pltpu.VMEM((seq, width), F32), pltpu.VMEM((seq, width), F32)],
        compiler_params=_params("parallel", "parallel"),
        name="hgrn2_latent" if latent else "hgrn2_context",
    )(*args, *alias_args)
    return (res[0], None) if latent else res


def _with_ones(v):
    return jnp.concatenate([v, jnp.ones_like(v)], axis=1)


def _softmax_pv(scores, values1):
    mx = functools.reduce(jnp.maximum, [jnp.max(s, axis=-1, keepdims=True) for s in scores])
    acc = functools.reduce(lambda a, b: a + b,
                           [_dot(jnp.exp2(s - mx).astype(BF16), v) for s, v in zip(scores, values1)])
    return acc[:, 0:HEAD_W] / acc[:, HEAD_W:HEAD_W + 1]


def _diff_pv(s0, s1, lam, values1):
    return _softmax_pv([s0], [values1]) - lam * _softmax_pv([s1], [values1])


def _lane_lt(shape, n):
    return lax.broadcasted_iota(jnp.int32, shape, len(shape) - 1) < n


def _rms_head(x, gain):
    return _rms(x, gain, HEAD_W)


def _rms_halves(x, gain2):
    lo = _lane_lt(x.shape, DF_DQK)
    sq = x * x
    ss_lo = jnp.sum(jnp.where(lo, sq, 0.0), axis=-1, keepdims=True)
    ss_hi = jnp.sum(sq, axis=-1, keepdims=True) - ss_lo
    inv = jnp.where(lo, lax.rsqrt(ss_lo * (1.0 / DF_DQK) + EPS), lax.rsqrt(ss_hi * (1.0 / DF_DQK) + EPS))
    return x * inv * gain2


def _lambda(lam_ref, lam_init):
    l = lam_ref[...]
    return (jnp.exp(jnp.sum(l[0:1] * l[1:2], axis=-1, keepdims=True))
            - jnp.exp(jnp.sum(l[2:3] * l[3:4], axis=-1, keepdims=True)) + lam_init)


N_CTX_ATTN_INPUTS = 17


def _ctx_attn_body(*refs, lam_init, kv_heads):
    (bq_ref, bk_ref, bv_ref, cq_ref, ck_ref, cv_ref, dq_ref, dk_ref, dv_ref,
     naq_ref, nak_ref, gq_ref, gk_ref, dfq_ref, dfk_ref, sub_ref, lam_ref) = refs[:N_CTX_ATTN_INPUTS]
    ob_ref, oc_ref, od_ref, kb_ref, vb_ref, kc_ref, vc_ref, kd_ref, vd_ref = refs[-9:]
    group = N_HEADS // GQA_KV_HEADS
    scale = LOG2E * HEAD_W ** -0.5
    lam = _lambda(lam_ref, lam_init)
    for n in range(kv_heads):
        kv_lanes = slice(n * HEAD_W, (n + 1) * HEAD_W)
        kc = _rms_head(ck_ref[:, kv_lanes], gk_ref[...])
        vc = cv_ref[:, kv_lanes]
        kc_ref[n] = kc
        vc_ref[n] = vc
        kc16 = kc.astype(BF16)
        vc1 = _with_ones(vc.astype(BF16))
        for g in range(group):
            h = n * group + g
            lanes = slice(h * HEAD_W, (h + 1) * HEAD_W)
            qc = (_rms_head(cq_ref[:, lanes], gq_ref[...]) * scale).astype(BF16)
            oc_ref[:, lanes] = _softmax_pv([_dot_nt(qc, kc16)], [vc1]).astype(oc_ref.dtype)
            kb = _rms_head(bk_ref[:, lanes], nak_ref[...])
            vb = bv_ref[:, lanes]
            kb_ref[h] = kb
            vb_ref[h] = vb
            qb = (_rms_head(bq_ref[:, lanes], naq_ref[...]) * scale).astype(BF16)
            ob_ref[:, lanes] = _softmax_pv([_dot_nt(qb, kb.astype(BF16))],
                                           [_with_ones(vb.astype(BF16))]).astype(ob_ref.dtype)
            kd = _rms_halves(dk_ref[:, lanes], dfk_ref[...])
            vd = dv_ref[:, lanes]
            kd_ref[h, 0] = kd[:, 0:DF_DQK]
            kd_ref[h, 1] = kd[:, DF_DQK:2 * DF_DQK]
            vd_ref[h] = vd
            qd = (_rms_halves(dq_ref[:, lanes], dfq_ref[...]) * (LOG2E * DF_DQK ** -0.5)).astype(BF16)
            lo = _lane_lt(kd.shape, DF_DQK)
            s0 = _dot_nt(qd, jnp.where(lo, kd, 0.0).astype(BF16))
            s1 = _dot_nt(qd, jnp.where(lo, 0.0, kd).astype(BF16))
            od = _diff_pv(s0, s1, lam, _with_ones(vd.astype(BF16)))
            od_ref[:, lanes] = (_rms_head(od, sub_ref[...]) * (1.0 - lam_init)).astype(od_ref.dtype)


def _context_attention(proj, n_seq, seq, gains, lam_init, layer, depth, caches_prev):
    na_qn, na_kn, gqa_qn, gqa_kn, df_qn, df_kn, df_subln, df_lam = gains
    group = N_HEADS // GQA_KV_HEADS
    kvs = CTX_KV_HEADS_PER_STEP
    n_heads = kvs * group
    width = n_heads * HEAD_W

    def heads(cb):
        return pl.BlockSpec((seq, width), lambda b, n, cb=cb: (b, cb // n_heads + n))

    def kv_head(cb):
        return pl.BlockSpec((seq, kvs * HEAD_W), lambda b, n, cb=cb: (b, cb // kvs + n))

    vec = pl.BlockSpec((1, HEAD_W), lambda b, n: (0, 0))
    cache_heads = pl.BlockSpec((None, None, n_heads, seq, HEAD_W), lambda b, n: (b, layer, n, 0, 0))
    cache_kv = pl.BlockSpec((None, None, kvs, seq, HEAD_W), lambda b, n: (b, layer, n, 0, 0))
    mixed = pl.BlockSpec((seq, width), lambda b, n: (b, n))
    mixed_shape = jax.ShapeDtypeStruct((proj.shape[0], GROUP_W), BF16)
    cache4 = jax.ShapeDtypeStruct((n_seq, depth, N_HEADS, seq, HEAD_W), F32)
    cache2 = jax.ShapeDtypeStruct((n_seq, depth, GQA_KV_HEADS, seq, HEAD_W), F32)
    cache_dk = jax.ShapeDtypeStruct((n_seq, depth, N_HEADS, 2, seq, DF_DQK), F32)
    args = [proj] * 9 + [na_qn[None, :], na_kn[None, :], gqa_qn[None, :], gqa_kn[None, :],
                         jnp.tile(df_qn, 2)[None, :], jnp.tile(df_kn, 2)[None, :], df_subln[None, :], df_lam]
    assert len(args) == N_CTX_ATTN_INPUTS
    alias_specs, alias_args, aliases = _alias_kwargs(len(args), caches_prev, 3)
    return pl.pallas_call(
        functools.partial(_ctx_attn_body, lam_init=lam_init, kv_heads=kvs),
        grid=(n_seq, GQA_KV_HEADS // kvs),
        in_specs=[heads(COL_B_Q), heads(COL_B_K), heads(COL_B_V),
                  heads(COL_C_Q), kv_head(COL_C_K), kv_head(COL_C_V),
                  heads(COL_D_Q), heads(COL_D_K), heads(COL_D_V),
                  vec, vec, vec, vec, vec, vec, vec,
                  pl.BlockSpec((4, DF_DQK), lambda b, n: (0, 0))] + alias_specs,
        out_specs=[mixed, mixed, mixed, cache_heads, cache_heads, cache_kv, cache_kv,
                   pl.BlockSpec((None, None, n_heads, 2, seq, DF_DQK), lambda b, n: (b, layer, n, 0, 0, 0)),
                   cache_heads],
        out_shape=[mixed_shape, mixed_shape, mixed_shape, cache4, cache4, cache2, cache2, cache_dk, cache4],
        input_output_aliases=aliases,
        compiler_params=_params("parallel", "parallel"),
        name="context_attention",
    )(*args, *alias_args)


def _rope_tables(n_tokens, rot_dim):
    t = np.arange(n_tokens)
    row = (t // GRID_W).astype(np.float32)
    col = (t % GRID_W).astype(np.float32)
    n_freq = rot_dim // 4
    inv = (np.float32(ROPE_THETA) ** (-np.arange(n_freq, dtype=np.float32) / np.float32(n_freq))).astype(np.float32)
    ang = np.concatenate([row[:, None] * inv, col[:, None] * inv], axis=-1).astype(np.float32)
    cos, sin, zero = np.cos(ang), np.sin(ang), np.zeros_like(ang)
    reps = HEAD_W // rot_dim
    a = np.tile(np.concatenate([cos, cos], axis=-1), (1, reps))
    b = np.tile(np.concatenate([-sin, zero], axis=-1), (1, reps))
    c = np.tile(np.concatenate([zero, sin], axis=-1), (1, reps))
    return jnp.asarray(np.stack([a, b, c]), F32)


def _rope(x, tab_ref, half):
    return (x * tab_ref[0] + pltpu.roll(x, HEAD_W - half, 1) * tab_ref[1]
            + pltpu.roll(x, half, 1) * tab_ref[2])


def _na_body(q_ref, k_ref, v_ref, ck_ref, cv_ref, bias_ref, qn_ref, kn_ref, prev_ref, o_ref,
             qs_ref, ks_ref, vs_ref, *, seq):
    del prev_ref
    rows = seq // GRID_W
    n_win = WIN_ROWS * GRID_W
    qs_ref[...] = (_rms_head(q_ref[...], qn_ref[...]) * (LOG2E * HEAD_W ** -0.5)).astype(BF16)
    ks_ref[...] = _rms_head(k_ref[...], kn_ref[...]).astype(BF16)
    vs_ref[...] = _with_ones(v_ref[...].astype(BF16))
    ck = ck_ref[...].astype(BF16)
    cv = _with_ones(cv_ref[...].astype(BF16))

    def row_step(r, carry):
        start = jnp.clip(r - WIN_ROWS // 2, 0, rows - WIN_ROWS)
        win = pl.ds(pl.multiple_of(start * GRID_W, GRID_W), n_win)
        qrows = pl.ds(pl.multiple_of(r * GRID_W, GRID_W), GRID_W)
        q = qs_ref[qrows, :]
        s_win = _dot_nt(q, ks_ref[win, :]) + bias_ref[start - r + (WIN_ROWS - 1)]
        s_ctx = _dot_nt(q, ck)
        o_ref[qrows, :] = _softmax_pv([s_win, s_ctx], [vs_ref[win, :], cv]).astype(o_ref.dtype)
        return carry

    lax.fori_loop(0, rows, row_step, 0, unroll=min(rows, 32))


def _na_bias(rpb):
    col = np.arange(GRID_W)
    col_start = np.clip(col - WIN_COLS // 2, 0, GRID_W - WIN_COLS)
    col_ok = (col[None, :] >= col_start[:, None]) & (col[None, :] < col_start[:, None] + WIN_COLS)
    dc = np.clip(col[None, :] - col[:, None] + WIN_COLS - 1, 0, 2 * WIN_COLS - 2).reshape(-1)
    onehot = (np.arange(2 * WIN_COLS - 1)[:, None] == dc[None, :]).astype(np.float32)
    per_dr = jnp.einsum('hdc,cn->hdn', rpb.astype(F32), jnp.asarray(onehot), precision=lax.Precision.HIGHEST)
    per_dr = jnp.where(col_ok[None, None], LOG2E * per_dr.reshape(rpb.shape[0], -1, GRID_W, GRID_W), NEG_INF)
    wins = jnp.stack([per_dr[:, o:o + WIN_ROWS] for o in range(WIN_ROWS)], axis=1)
    return wins.transpose(0, 1, 3, 2, 4).reshape(rpb.shape[0], WIN_ROWS, GRID_W, WIN_ROWS * GRID_W)


def _latent_na(proj, row_block0, n_seq, seq, cache_k, cache_v, layer, bias, na_qn, na_kn, mixed_prev):
    past = cache_k.shape[3]

    def col(cb):
        return pl.BlockSpec((seq, HEAD_W), lambda b, h, cb=cb: (row_block0 + b, cb + h))

    cache = pl.BlockSpec((None, None, None, past, HEAD_W), lambda b, h: (b, layer, h, 0, 0))
    vec = pl.BlockSpec((1, HEAD_W), lambda b, h: (0, 0))
    return pl.pallas_call(
        functools.partial(_na_body, seq=seq),
        grid=(n_seq, N_HEADS),
        in_specs=[col(COL_B_Q), col(COL_B_K), col(COL_B_V), cache, cache,
                  pl.BlockSpec((None, WIN_ROWS, GRID_W, WIN_ROWS * GRID_W), lambda b, h: (h, 0, 0, 0)),
                  vec, vec, ANY_SPEC],
        out_specs=pl.BlockSpec((seq, HEAD_W), lambda b, h: (row_block0 + b, h)),
        out_shape=jax.ShapeDtypeStruct(mixed_prev.shape, BF16),
        input_output_aliases={8: 0},
        scratch_shapes=[pltpu.VMEM((seq, HEAD_W), BF16), pltpu.VMEM((seq, HEAD_W), BF16),
                        pltpu.VMEM((seq, 2 * HEAD_W), BF16)],
        compiler_params=_params("parallel", "parallel"),
        name="latent_neighbourhood_attention",
    )(proj, proj, proj, cache_k, cache_v, bias, na_qn[None, :], na_kn[None, :], mixed_prev)


def _gqa_body(q_ref, k_ref, v_ref, ck_ref, cv_ref, rope_ref, qn_ref, kn_ref, prev_ref, o_ref,
              qs_ref, ks_ref, vs_ref, *, seq, tq):
    del prev_ref
    group = N_HEADS // GQA_KV_HEADS
    half = HEAD_W // 2
    ks_ref[0:seq, :] = _rope(_rms_head(k_ref[...], kn_ref[...]), rope_ref, half).astype(BF16)
    ks_ref[seq:, :] = ck_ref[...].astype(BF16)
    vs_ref[0:seq, :] = _with_ones(v_ref[...].astype(BF16))
    vs_ref[seq:, :] = _with_ones(cv_ref[...].astype(BF16))
    for g in range(group):
        q = _rms_head(q_ref[:, g * HEAD_W:(g + 1) * HEAD_W], qn_ref[...]) * (LOG2E * HEAD_W ** -0.5)
        qs_ref[g] = _rope(q, rope_ref, half).astype(BF16)
    kk = ks_ref[...]
    vv = vs_ref[...]
    for g in range(group):
        def q_step(i, carry, g=g):
            qrows = pl.ds(pl.multiple_of(i * tq, tq), tq)
            o = _softmax_pv([_dot_nt(qs_ref[g, qrows, :], kk)], [vv])
            o_ref[qrows, g * HEAD_W:(g + 1) * HEAD_W] = o.astype(o_ref.dtype)
            return carry

        lax.fori_loop(0, seq // tq, q_step, 0, unroll=min(seq // tq, 8))


def _latent_gqa(proj, row_block0, n_seq, seq, cache_k, cache_v, layer, rope, gqa_qn, gqa_kn, mixed_prev):
    past = cache_k.shape[3]
    group = N_HEADS // GQA_KV_HEADS
    tq = 256
    cache = pl.BlockSpec((None, None, None, past, HEAD_W), lambda b, n: (b, layer, n, 0, 0))
    vec = pl.BlockSpec((1, HEAD_W), lambda b, n: (0, 0))
    return pl.pallas_call(
        functools.partial(_gqa_body, seq=seq, tq=tq),
        grid=(n_seq, GQA_KV_HEADS),
        in_specs=[pl.BlockSpec((seq, group * HEAD_W), lambda b, n: (row_block0 + b, COL_C_Q // group + n)),
                  pl.BlockSpec((seq, HEAD_W), lambda b, n: (row_block0 + b, COL_C_K + n)),
                  pl.BlockSpec((seq, HEAD_W), lambda b, n: (row_block0 + b, COL_C_V + n)),
                  cache, cache,
                  pl.BlockSpec((3, seq, HEAD_W), lambda b, n: (0, 0, 0)),
                  vec, vec, ANY_SPEC],
        out_specs=pl.BlockSpec((seq, group * HEAD_W), lambda b, n: (row_block0 + b, n)),
        out_shape=jax.ShapeDtypeStruct(mixed_prev.shape, BF16),
        input_output_aliases={8: 0},
        scratch_shapes=[pltpu.VMEM((group, seq, HEAD_W), BF16),
                        pltpu.VMEM((seq + past, HEAD_W), BF16),
                        pltpu.VMEM((seq + past, 2 * HEAD_W), BF16)],
        compiler_params=_params("parallel", "parallel"),
        name="latent_gqa_attention",
    )(proj, proj, proj, cache_k, cache_v, rope, gqa_qn[None, :], gqa_kn[None, :], mixed_prev)


def _diff_body(q_ref, k_ref, v_ref, ck_ref, cv_ref, rope_ref, qn_ref, kn_ref, sub_ref, lam_ref, prev_ref, o_ref,
               qs_ref, k0_ref, k1_ref, vs_ref, *, seq, tq, lam_init):
    del prev_ref
    half = DF_DQK // 2
    k = _rope(_rms_halves(k_ref[...], kn_ref[...]), rope_ref, half)
    lo = _lane_lt(k.shape, DF_DQK)
    k0_ref[0:seq, :] = jnp.where(lo, k, 0.0).astype(BF16)
    k1_ref[0:seq, :] = jnp.where(lo, 0.0, k).astype(BF16)
    ck = ck_ref[...]
    lo_c = _lane_lt(ck.shape, DF_DQK)
    k0_ref[seq:, :] = jnp.where(lo_c, ck, 0.0).astype(BF16)
    k1_ref[seq:, :] = jnp.where(lo_c, 0.0, ck).astype(BF16)
    vs_ref[0:seq, :] = _with_ones(v_ref[...].astype(BF16))
    vs_ref[seq:, :] = _with_ones(cv_ref[...].astype(BF16))
    q = _rms_halves(q_ref[...], qn_ref[...]) * (LOG2E * DF_DQK ** -0.5)
    qs_ref[...] = _rope(q, rope_ref, half).astype(BF16)
    lam = _lambda(lam_ref, lam_init)
    k0 = k0_ref[...]
    k1 = k1_ref[...]
    vv = vs_ref[...]

    def q_step(i, carry):
        qrows = pl.ds(pl.multiple_of(i * tq, tq), tq)
        qb = qs_ref[qrows, :]
        o = _diff_pv(_dot_nt(qb, k0), _dot_nt(qb, k1), lam, vv)
        o_ref[qrows, :] = (_rms_head(o, sub_ref[...]) * (1.0 - lam_init)).astype(o_ref.dtype)
        return carry

    lax.fori_loop(0, seq // tq, q_step, 0, unroll=min(seq // tq, 4))


def _latent_diff(proj, row_block0, n_seq, seq, cache_k2, cache_v, layer, rope, df_qn, df_kn, df_subln, df_lam,
                 lam_init, mixed_prev):
    past = cache_k2.shape[3]
    tq = 256

    def col(cb):
        return pl.BlockSpec((seq, HEAD_W), lambda b, h, cb=cb: (row_block0 + b, cb + h))

    cache = pl.BlockSpec((None, None, None, past, HEAD_W), lambda b, h: (b, layer, h, 0, 0))
    vec = pl.BlockSpec((1, HEAD_W), lambda b, h: (0, 0))
    kv_scratch = pltpu.VMEM((seq + past, HEAD_W), BF16)
    return pl.pallas_call(
        functools.partial(_diff_body, seq=seq, tq=tq, lam_init=lam_init),
        grid=(n_seq, N_HEADS),
        in_specs=[col(COL_D_Q), col(COL_D_K), col(COL_D_V), cache, cache,
                  pl.BlockSpec((3, seq, HEAD_W), lambda b, h: (0, 0, 0)),
                  vec, vec, vec, pl.BlockSpec((4, DF_DQK), lambda b, h: (0, 0)), ANY_SPEC],
        out_specs=pl.BlockSpec((seq, HEAD_W), lambda b, h: (row_block0 + b, h)),
        out_shape=jax.ShapeDtypeStruct(mixed_prev.shape, BF16),
        input_output_aliases={10: 0},
        scratch_shapes=[pltpu.VMEM((seq, HEAD_W), BF16), kv_scratch, kv_scratch,
                        pltpu.VMEM((seq + past, 2 * HEAD_W), BF16)],
        compiler_params=_params("parallel", "parallel"),
        name="latent_diff_attention",
    )(proj, proj, proj, cache_k2, cache_v, rope, jnp.tile(df_qn, 2)[None, :], jnp.tile(df_kn, 2)[None, :],
      df_subln[None, :], df_lam, mixed_prev)


def _first_max(vals):
    best = vals[0]
    idx = jnp.zeros(best.shape, jnp.int32)
    for i in range(1, len(vals)):
        better = vals[i] > best
        best = jnp.where(better, vals[i], best)
        idx = jnp.where(better, i, idx)
    return best, idx


def _pick(vals, idx):
    out = vals[0]
    for i in range(1, len(vals)):
        out = jnp.where(idx == i, vals[i], out)
    return out


def _outproj_body(ma_ref, mb_ref, mc_ref, md_ref, w_ref, hc_ref, hl_ref, mod_ref, n2_ref, rw_ref, rb_ref,
                  h1_ref, x2_ref, idx_ref, gate_ref, mixed_ref, *, ctx_tiles):
    for g, m_ref in enumerate((ma_ref, mb_ref, mc_ref, md_ref)):
        mixed_ref[:, g * GROUP_W:(g + 1) * GROUP_W] = m_ref[...]
    y = mod_ref[2:3, :] * _dot(mixed_ref[...], w_ref[...])

    def residual(h_ref):
        h1_ref[...] = h_ref[...] + y

    is_ctx = pl.program_id(0) < ctx_tiles
    pl.when(is_ctx)(lambda: residual(hc_ref))
    pl.when(jnp.logical_not(is_ctx))(lambda: residual(hl_ref))
    h1 = h1_ref[...]
    x2 = _rms(h1, n2_ref[...], D_MODEL) * (1.0 + mod_ref[4:5, :]) + mod_ref[3:4, :]
    x2_ref[...] = _pack_bf16_pairs(x2)
    x_hi = x2.astype(BF16)
    x_lo = (x2 - x_hi.astype(F32)).astype(BF16)
    acc = _dot(x_hi, rw_ref[...])
    logits_tok = acc[:, 0:HEAD_W] + acc[:, HEAD_W:2 * HEAD_W] + _dot(x_lo, rw_ref[:, 0:HEAD_W])
    logits = logits_tok.T[0:N_EXPERTS, :]
    aff_all = _sigmoid(logits)
    sel_all = aff_all + rb_ref[...]
    aff = [aff_all[e:e + 1, :] for e in range(N_EXPERTS)]
    sel = [sel_all[e:e + 1, :] for e in range(N_EXPERTS)]
    neg = jnp.full(sel[0].shape, -jnp.inf, F32)
    scores = []
    for g in range(N_EXP_GROUPS):
        grp = sel[g * EXP_PER_GROUP:(g + 1) * EXP_PER_GROUP]
        m1, i1 = _first_max(grp)
        m2, _ = _first_max([jnp.where(i1 == j, neg, grp[j]) for j in range(EXP_PER_GROUP)])
        scores.append(m1 + m2)
    _, g_best = _first_max(scores)
    in_sel = [_pick([sel[g * EXP_PER_GROUP + j] for g in range(N_EXP_GROUPS)], g_best)
              for j in range(EXP_PER_GROUP)]
    in_aff = [_pick([aff[g * EXP_PER_GROUP + j] for g in range(N_EXP_GROUPS)], g_best)
              for j in range(EXP_PER_GROUP)]
    _, l1 = _first_max(in_sel)
    _, l2 = _first_max([jnp.where(l1 == j, neg, in_sel[j]) for j in range(EXP_PER_GROUP)])
    w1 = _pick(in_aff, l1)
    w2 = _pick(in_aff, l2)
    idx_ref[0:1, :] = g_best * EXP_PER_GROUP + l1
    idx_ref[1:2, :] = g_best * EXP_PER_GROUP + l2
    gate_ref[0:1, :] = w1 / (w1 + w2)
    gate_ref[1:2, :] = w2 / (w1 + w2)


def _output_projection(mixed4, w_out_bf16, h_ctx, h_lat, mod, norm2, router_w, router_b, cond_of_tile, tm):
    d = h_ctx.shape[1]
    t = h_ctx.shape[0] + h_lat.shape[0]
    ctx_tiles = h_ctx.shape[0] // tm
    slab = pl.BlockSpec((tm, GROUP_W), lambda i: (i, 0))
    rw_hi = router_w.astype(BF16)
    rw_lo = (router_w - rw_hi.astype(F32)).astype(BF16)
    pad = ((0, 0), (0, HEAD_W - N_EXPERTS))
    router_split = jnp.concatenate([jnp.pad(rw_hi, pad), jnp.pad(rw_lo, pad)], axis=1)
    return pl.pallas_call(
        functools.partial(_outproj_body, ctx_tiles=ctx_tiles),
        grid=(t // tm,),
        in_specs=[slab, slab, slab, slab,
                  pl.BlockSpec((d, d), lambda i: (0, 0))] + _group_specs(tm, d, ctx_tiles) + [
                  pl.BlockSpec((None, 6, d), lambda i: (cond_of_tile(i), 0, 0)),
                  pl.BlockSpec((1, d), lambda i: (0, 0)),
                  pl.BlockSpec((d, 2 * HEAD_W), lambda i: (0, 0)),
                  pl.BlockSpec((N_EXPERTS, 1), lambda i: (0, 0))],
        out_specs=[pl.BlockSpec((tm, d), lambda i: (i, 0)),
                   pl.BlockSpec((tm, d // 2), lambda i: (i, 0)),
                   pl.BlockSpec((2, tm), lambda i: (0, i)),
                   pl.BlockSpec((2, tm), lambda i: (0, i))],
        out_shape=[jax.ShapeDtypeStruct((t, d), F32), jax.ShapeDtypeStruct((t, d // 2), jnp.uint32),
                   jax.ShapeDtypeStruct((2, t), jnp.int32), jax.ShapeDtypeStruct((2, t), F32)],
        scratch_shapes=[pltpu.VMEM((tm, d), BF16)],
        compiler_params=_params("parallel", vmem=VMEM_LIMIT_PROJ),
        name="out_proj_residual_router",
    )(*mixed4, w_out_bf16, h_ctx, h_lat, mod, norm2[None, :], router_split, router_b[:, None])


def _vmem_row(ref, base, u):
    return ref.at[pl.ds(base, SUBLANES), :].at[pl.ds(u, 1), :]


def _for_row_groups(n_rows, fn):
    def body(g, carry):
        base = pl.multiple_of(g * SUBLANES, SUBLANES)
        for u in range(SUBLANES):
            fn(base, u)
        return carry

    lax.fori_loop(0, n_rows // SUBLANES, body, 0)


def _rows_wait(src_hbm, dst, sem, n_rows):
    pltpu.make_async_copy(src_hbm.at[pl.ds(0, n_rows), :], dst.at[pl.ds(0, n_rows), :], sem).wait()


def _dispatch_body(dest_ref, ps_ref, pe_ref, x_ref, xs_hbm, zero_ref, sem, zsem, *, n_tok, tile):
    i = pl.program_id(0)

    def zero_copy(e):
        first = pl.multiple_of(pe_ref[e] - MOE_ROWS, MOE_ROWS)
        return pltpu.make_async_copy(zero_ref, xs_hbm.at[pl.ds(first, MOE_ROWS), :], zsem)

    @pl.when(i == 0)
    def _():
        zero_ref[...] = jnp.zeros(zero_ref.shape, zero_ref.dtype)
        for e in range(N_EXPERTS):
            @pl.when(pe_ref[e] > ps_ref[e])
            def _(e=e):
                zero_copy(e).start()
        for e in range(N_EXPERTS):
            @pl.when(pe_ref[e] > ps_ref[e])
            def _(e=e):
                zero_copy(e).wait()

    for k in range(2):
        def scatter_row(base, u, k=k):
            row = dest_ref[k * n_tok + i * tile + base + u]
            pltpu.make_async_copy(_vmem_row(x_ref, base, u), xs_hbm.at[pl.ds(row, 1), :], sem).start()

        _for_row_groups(tile, scatter_row)
    for k in range(2):
        _rows_wait(x_ref, xs_hbm, sem, tile)


def _dispatch(x2, dest, pad_start, pad_end, n_rows):
    t, d = x2.shape
    tile = TOKEN_TILE
    grid_spec = pltpu.PrefetchScalarGridSpec(
        num_scalar_prefetch=3,
        grid=(t // tile,),
        in_specs=[pl.BlockSpec((tile, d), lambda i, dst, ps, pe: (i, 0))],
        out_specs=ANY_SPEC,
        scratch_shapes=[pltpu.VMEM((MOE_ROWS, d), x2.dtype), pltpu.SemaphoreType.DMA, pltpu.SemaphoreType.DMA],
    )
    return pl.pallas_call(
        functools.partial(_dispatch_body, n_tok=t, tile=tile),
        grid_spec=grid_spec,
        out_shape=jax.ShapeDtypeStruct((n_rows, d), x2.dtype),
        compiler_params=_params("arbitrary"),
        name="moe_dispatch",
    )(dest, pad_start, pad_end, x2)


def _expert_body(be_ref, nb_ref, nx_ref, x_ref, wg_hbm, wu_hbm, wd_hbm, o_ref,
                 stage_g, stage_u, stage_d, wg_bf, wu_bf, wd_bf, sem, *, layer):
    i = pl.program_id(0)

    def weight_copies(e):
        return (pltpu.make_async_copy(wg_hbm.at[layer, e], stage_g, sem.at[0]),
                pltpu.make_async_copy(wu_hbm.at[layer, e], stage_u, sem.at[1]),
                pltpu.make_async_copy(wd_hbm.at[layer, e], stage_d, sem.at[2]))

    @pl.when(i < nb_ref[0])
    def _():
        e = be_ref[i]

        @pl.when(i == 0)
        def _():
            for c in weight_copies(e):
                c.start()

        @pl.when((i == 0) | (e != be_ref[jnp.maximum(i - 1, 0)]))
        def _():
            for c in weight_copies(e):
                c.wait()
            wg_bf[...] = stage_g[...].astype(BF16)
            wu_bf[...] = stage_u[...].astype(BF16)
            wd_bf[...] = stage_d[...].astype(BF16)
            nxt = nx_ref[e]

            @pl.when(nxt < N_EXPERTS)
            def _():
                for c in weight_copies(nxt):
                    c.start()

        x = _unpack_bf16_pairs(x_ref[...]).astype(BF16)
        hdn = _silu(_dot(x, wg_bf[...])) * _dot(x, wu_bf[...])
        o_ref[...] = _pack_bf16_pairs(_dot(hdn.astype(BF16), wd_bf[...]))

    @pl.when(i >= nb_ref[0])
    def _():
        o_ref[...] = jnp.zeros(o_ref.shape, o_ref.dtype)


def _expert_blocks(xs, block_expert, n_used, next_expert, w_gate, w_up, w_down, layer):
    n_rows, half = xs.shape
    d = 2 * half
    ff = w_gate.shape[-1]
    grid_spec = pltpu.PrefetchScalarGridSpec(
        num_scalar_prefetch=3,
        grid=(n_rows // MOE_ROWS,),
        in_specs=[pl.BlockSpec((MOE_ROWS, half), lambda i, be, nb, nx: (jnp.minimum(i, nb[0] - 1), 0)),
                  ANY_SPEC, ANY_SPEC, ANY_SPEC],
        out_specs=pl.BlockSpec((MOE_ROWS, half), lambda i, be, nb, nx: (i, 0)),
        scratch_shapes=[pltpu.VMEM((d, ff), F32), pltpu.VMEM((d, ff), F32), pltpu.VMEM((ff, d), F32),
                        pltpu.VMEM((d, ff), BF16), pltpu.VMEM((d, ff), BF16), pltpu.VMEM((ff, d), BF16),
                        pltpu.SemaphoreType.DMA((3,))],
    )
    return pl.pallas_call(
        functools.partial(_expert_body, layer=layer),
        grid_spec=grid_spec,
        out_shape=jax.ShapeDtypeStruct((n_rows, half), jnp.uint32),
        compiler_params=_params("arbitrary"),
        name="moe_expert_blocks",
    )(block_expert, n_used, next_expert, xs, w_gate, w_up, w_down)


def _combine_body(dest_ref, h_ref, gate_ref, y_hbm, mod_ref, *rest, n_tok, tile, ctx_tiles):
    out_refs, (ybuf, sem) = rest[:-2], rest[-2:]
    i = pl.program_id(0)
    n_tiles = pl.num_programs(0)

    def start(blk, slot):
        for k in range(2):
            def gather_row(base, u, k=k):
                row = dest_ref[k * n_tok + blk * tile + base + u]
                pltpu.make_async_copy(y_hbm.at[pl.ds(row, 1), :], _vmem_row(ybuf.at[slot, k], base, u),
                                      sem.at[slot]).start()

            _for_row_groups(tile, gather_row)

    @pl.when(i == 0)
    def _():
        start(0, 0)

    @pl.when(i + 1 < n_tiles)
    def _():
        start(i + 1, (i + 1) % 2)

    slot = i % 2
    for k in range(2):
        _rows_wait(y_hbm, ybuf.at[slot, k], sem.at[slot], tile)
    gate = gate_ref[...]
    y0 = _unpack_bf16_pairs(ybuf[slot, 0])
    y1 = _unpack_bf16_pairs(ybuf[slot, 1])
    out = h_ref[...] + mod_ref[5:6, :] * (gate[:, 0:1] * y0 + gate[:, 1:2] * y1)
    @pl.when(i < ctx_tiles)
    def _():
        out_refs[0][...] = out

    @pl.when(i >= ctx_tiles)
    def _():
        out_refs[1][...] = out


def _combine(h1, yb, dest, gates, mod, cond_of_tile, tile, split_rows):
    t, d = h1.shape
    row_tile = pl.BlockSpec((tile, d), lambda i, dst: (i, 0))
    ctx_tiles = split_rows // tile
    out_specs = _group_specs(tile, d, ctx_tiles)
    out_shape = [jax.ShapeDtypeStruct((split_rows, d), F32), jax.ShapeDtypeStruct((t - split_rows, d), F32)]
    grid_spec = pltpu.PrefetchScalarGridSpec(
        num_scalar_prefetch=1,
        grid=(t // tile,),
        in_specs=[row_tile,
                  pl.BlockSpec((tile, 2), lambda i, dst: (i, 0)),
                  ANY_SPEC,
                  pl.BlockSpec((None, 6, d), lambda i, dst: (cond_of_tile(i), 0, 0))],
        out_specs=out_specs,
        scratch_shapes=[pltpu.VMEM((2, 2, tile, yb.shape[1]), yb.dtype), pltpu.SemaphoreType.DMA((2,))],
    )
    return pl.pallas_call(
        functools.partial(_combine_body, n_tok=t, tile=tile, ctx_tiles=ctx_tiles),
        grid_spec=grid_spec,
        out_shape=out_shape,
        compiler_params=_params("arbitrary"),
        name="moe_gated_residual",
    )(dest, h1, gates, yb, mod)


def _moe(h1, x2, idx_t, gate_t, mod, w_gate, w_up, w_down, layer, cond_of_tile, split_rows):
    t, d = h1.shape
    n = 2 * t
    experts = idx_t.reshape(n)
    onehot = (experts[:, None] == jnp.arange(N_EXPERTS, dtype=jnp.int32)[None, :]).astype(BF16)
    blocks = onehot.reshape(n // TOKEN_TILE, TOKEN_TILE, N_EXPERTS)
    tri = jnp.asarray(np.tril(np.ones((TOKEN_TILE, TOKEN_TILE), np.float32)), BF16)
    within = jnp.einsum('ij,bjk->bik', tri, blocks, preferred_element_type=F32)
    block_total = within[:, -1, :]
    block_first = jnp.cumsum(block_total, axis=0) - block_total
    counts = (block_first[-1] + block_total[-1]).astype(jnp.int32)
    before = (within + block_first[:, None, :]).reshape(n, N_EXPERTS) - 1.0
    rank = jnp.sum(before * onehot.astype(F32), axis=1).astype(jnp.int32)
    padded = (counts + MOE_ROWS - 1) // MOE_ROWS * MOE_ROWS
    pad_end = jnp.cumsum(padded).astype(jnp.int32)
    pad_start = pad_end - padded
    dest = (pad_start[experts] + rank).astype(jnp.int32)
    n_blocks = (n + N_EXPERTS * (MOE_ROWS - 1) + MOE_ROWS - 1) // MOE_ROWS
    block_first_row = jnp.arange(n_blocks, dtype=jnp.int32) * MOE_ROWS
    block_expert = jnp.minimum(jnp.sum((pad_end[None, :] <= block_first_row[:, None]).astype(jnp.int32), axis=1),
                               N_EXPERTS - 1)
    n_used = (pad_end[-1:] // MOE_ROWS).astype(jnp.int32)
    xs = _dispatch(x2, dest, pad_start, pad_end, n_blocks * MOE_ROWS)
    ids = jnp.arange(N_EXPERTS, dtype=jnp.int32)
    later_with_rows = (counts[None, :] > 0) & (ids[None, :] > ids[:, None])
    next_expert = jnp.min(jnp.where(later_with_rows, ids[None, :], N_EXPERTS), axis=1).astype(jnp.int32)
    yb = _expert_blocks(xs, block_expert, n_used, next_expert, w_gate, w_up, w_down, layer)
    return _combine(h1, yb, dest, gate_t.T, mod, cond_of_tile, TOKEN_TILE, split_rows)


def kernel(x_prompt, x_sample, cache_na_k, cache_na_v, cache_gqa_k, cache_gqa_v, cache_diff_k, cache_diff_v, state_hgrn, c, c_ctx, w_mod, b_mod, norm1, norm2, w_in, w_out, hg_lb_logits, hg_onorm, na_qn, na_kn, na_rpb, gqa_qn, gqa_kn, df_qn, df_kn, df_lam, df_subln, router_w, router_b, w_gate, w_up, w_down):
    n_ctx, ctx_len, d = x_prompt.shape
    n_lat, lat_len, _ = x_sample.shape
    depth = w_in.shape[0]
    t_ctx = n_ctx * ctx_len
    assert t_ctx % lat_len == 0 and lat_len % GRID_W == 0 and lat_len // GRID_W >= WIN_ROWS
    tm = next(m for m in (1024, 512, 256) if t_ctx % m == 0 and lat_len % m == 0)
    tm2 = min(tm, 512)
    lat_block0 = t_ctx // lat_len

    def cond_tile(tile_rows):
        def cond_of_tile(i):
            return jnp.where(i < t_ctx // tile_rows, 0, 1 + (i - t_ctx // tile_rows) // (lat_len // tile_rows))
        return cond_of_tile

    sm = jax.nn.softmax(hg_lb_logits.astype(F32), axis=0)
    lower = jnp.cumsum(sm, axis=0) - sm[0:1]
    mod_all = _modulation(jnp.concatenate([c_ctx[None, :], c], axis=0), w_mod, b_mod)
    mod_all = mod_all.reshape(depth, 1 + n_lat, 6, d)
    hgrn_consts = _hgrn_constants(HGRN_CHUNK)
    rope_c = _rope_tables(lat_len, HEAD_W)
    rope_d = _rope_tables(lat_len, DF_DQK)
    past = cache_diff_k.shape[4]
    cache_diff_k2 = cache_diff_k.transpose(0, 1, 2, 4, 3, 5).reshape(n_lat, depth, N_HEADS, past, HEAD_W)

    h_ctx, h_lat = x_prompt.reshape(t_ctx, d), x_sample.reshape(n_lat * lat_len, d)
    caches, states = [], None
    for layer in range(depth):
        mod = mod_all[layer]
        lam_init = 0.8 - 0.6 * math.exp(-0.3 * layer)
        proj = _input_projection(h_ctx, h_lat, mod, norm1[layer], w_in[layer].astype(BF16), cond_tile(tm), tm)
        mix_a, states = _hgrn(proj, 0, n_ctx, ctx_len, lower[layer], hg_onorm[layer], hgrn_consts, None, None,
                              states, layer, depth)
        gains = (na_qn[layer], na_kn[layer], gqa_qn[layer], gqa_kn[layer], df_qn[layer], df_kn[layer],
                 df_subln[layer], df_lam[layer])
        mix_b, mix_c, mix_d, *caches = _context_attention(proj, n_ctx, ctx_len, gains, lam_init, layer, depth,
                                                          caches)
        mix_a, _ = _hgrn(proj, lat_block0, n_lat, lat_len, lower[layer], hg_onorm[layer], hgrn_consts,
                         state_hgrn, mix_a, None, layer, depth)
        mix_b = _latent_na(proj, lat_block0, n_lat, lat_len, cache_na_k, cache_na_v, layer,
                           _na_bias(na_rpb[layer]), na_qn[layer], na_kn[layer], mix_b)
        mix_c = _latent_gqa(proj, lat_block0, n_lat, lat_len, cache_gqa_k, cache_gqa_v, layer, rope_c,
                            gqa_qn[layer], gqa_kn[layer], mix_c)
        mix_d = _latent_diff(proj, lat_block0, n_lat, lat_len, cache_diff_k2, cache_diff_v, layer, rope_d,
                             df_qn[layer], df_kn[layer], df_subln[layer], df_lam[layer], lam_init, mix_d)
        h1, x2, idx_t, gate_t = _output_projection((mix_a, mix_b, mix_c, mix_d), w_out[layer].astype(BF16), h_ctx,
                                                   h_lat, mod, norm2[layer], router_w, router_b, cond_tile(tm2), tm2)
        h_ctx, h_lat = _moe(h1, x2, idx_t, gate_t, mod, w_gate, w_up, w_down, layer, cond_tile(TOKEN_TILE), t_ctx)
    y_prompt = h_ctx.reshape(n_ctx, ctx_len, d)
    y_sample = h_lat.reshape(n_lat, lat_len, d)
    return (y_prompt, y_sample, *caches, states)
```

```python
import functools
import math

import numpy as np
import jax
import jax.numpy as jnp
from jax import lax
from jax.experimental import pallas as pl
from jax.experimental.pallas import tpu as pltpu

D_MODEL = 2048
GRID_W = 64
GROUP_W = D_MODEL // 4
N_HEADS = 4
HEAD_W = GROUP_W // N_HEADS
SUBLANES = 8
GQA_KV_HEADS = 2
DF_DQK = HEAD_W // 2
WIN_ROWS = 8
WIN_COLS = 16
N_EXPERTS = 16
N_EXP_GROUPS = 4
EXP_PER_GROUP = N_EXPERTS // N_EXP_GROUPS
EXPERT_FF = D_MODEL // 4
ROPE_THETA = 10000.0
EPS = 1e-6
NEG_INF = -1e30
LOG2E = math.log2(math.e)
IN_WIDTH = 13 * GROUP_W

COL_A_Q, COL_A_FF, COL_A_FB, COL_A_I, COL_A_G = 0, 4, 8, 12, 16
COL_B_Q, COL_B_K, COL_B_V = 20, 24, 28
COL_C_Q, COL_C_K, COL_C_V = 32, 36, 38
COL_D_Q, COL_D_K, COL_D_V = 40, 44, 48

HGRN_CHUNK = 128
HGRN_UNROLL = 4
HGRN_INPUT_VMEM = 24 * 1024 * 1024
CTX_KV_HEADS_PER_STEP = 1
MOE_ROWS = 256
DISPATCH_TILES = (1024, 512, 256)
COMBINE_TILE = 256
VMEM_LIMIT = 48 * 1024 * 1024
VMEM_LIMIT_PROJ = 56 * 1024 * 1024

F32 = jnp.float32
BF16 = jnp.bfloat16
ANY_SPEC = pl.BlockSpec(memory_space=pl.ANY)


def _params(*sem, vmem=VMEM_LIMIT):
    return pltpu.CompilerParams(dimension_semantics=sem, vmem_limit_bytes=vmem)


def _sigmoid(x):
    return 1.0 / (1.0 + jnp.exp(-x))


def _silu(x):
    return x * _sigmoid(x)


def _rms(x, gain, n):
    return x * lax.rsqrt(jnp.sum(x * x, axis=-1, keepdims=True) * (1.0 / n) + EPS) * gain


def _dot(a, b):
    return jnp.dot(a, b, preferred_element_type=F32)


def _dot_nt(a, b):
    return lax.dot_general(a, b, (((1,), (1,)), ((), ())), preferred_element_type=F32)


def _dot_tn(a, b):
    return lax.dot_general(a, b, (((0,), (0,)), ((), ())), preferred_element_type=F32)


def _pack_bf16_pairs(x):
    k = x.shape[1] // 2
    lo = lax.bitcast_convert_type(x[:, :k].astype(BF16).astype(F32), jnp.uint32) >> 16
    hi = lax.bitcast_convert_type(x[:, k:].astype(BF16).astype(F32), jnp.uint32)
    return hi | lo


def _unpack_bf16_pairs(w):
    lo = lax.bitcast_convert_type(w << 16, F32)
    hi = lax.bitcast_convert_type(w & jnp.uint32(0xFFFF0000), F32)
    return jnp.concatenate([lo, hi], axis=1)


def _aligned(x, m):
    return x if isinstance(x, int) else pl.multiple_of(x, m)


def _alias_kwargs(n_inputs, prev, first_out):
    return ([ANY_SPEC] * len(prev), list(prev), {n_inputs + k: first_out + k for k in range(len(prev))})


def _mod_body(cond_ref, w_ref, b_ref, o_ref):
    w = w_ref[...]
    for c in range(cond_ref.shape[0]):
        s = _silu(cond_ref[c])
        o_ref[c:c + 1, :] = jnp.sum(w * s, axis=0, keepdims=True) + b_ref[...]


def _modulation(cond, w_mod, b_mod):
    depth, d, n6 = w_mod.shape
    nc = cond.shape[0]
    tn = 1024
    return pl.pallas_call(
        _mod_body,
        grid=(depth, n6 // tn),
        in_specs=[pl.BlockSpec((nc, d, 1), lambda l, j: (0, 0, 0)),
                  pl.BlockSpec((None, d, tn), lambda l, j: (l, 0, j)),
                  pl.BlockSpec((None, 1, tn), lambda l, j: (l, 0, j))],
        out_specs=pl.BlockSpec((None, nc, tn), lambda l, j: (l, 0, j)),
        out_shape=jax.ShapeDtypeStruct((depth, nc, n6), F32),
        compiler_params=_params("parallel", "parallel"),
        name="adaln_modulation",
    )(cond[:, :, None], w_mod, b_mod[:, None, :])


def _group_specs(tile, d, ctx_tiles):
    return [pl.BlockSpec((tile, d), lambda i, *_: (jnp.minimum(i, ctx_tiles - 1), 0)),
            pl.BlockSpec((tile, d), lambda i, *_: (jnp.maximum(i - ctx_tiles, 0), 0))]


INPROJ_NORM_CHUNKS = 8


def _inproj_body(h0_ref, mod0_ref, hc_ref, hl_ref, mod_ref, n1_ref, w_ref, o_ref, xn_ref, *, ctx_tiles, chunk):
    i = pl.program_id(0)
    j = pl.program_id(1)

    def normalise(h, mod):
        return (_rms(h, n1_ref[...], D_MODEL) * (1.0 + mod[1:2, :]) + mod[0:1, :]).astype(BF16)

    @pl.when((i == 0) & (j == 0))
    def _():
        xn_ref[0] = normalise(h0_ref[...], mod0_ref[...])

    nxt = i + 1
    rows = pl.ds(pl.multiple_of(jnp.minimum(j, INPROJ_NORM_CHUNKS - 1) * chunk, chunk), chunk)
    xn_ref[nxt % 2, rows, :] = normalise(jnp.where(nxt < ctx_tiles, hc_ref[...], hl_ref[...]), mod_ref[...])
    o_ref[...] = _dot(xn_ref[i % 2], w_ref[...])


def _input_projection(h_ctx, h_lat, mod, norm1, w_in_bf16, cond_of_tile, tm):
    d = h_ctx.shape[1]
    t = h_ctx.shape[0] + h_lat.shape[0]
    n_tiles, ctx_tiles, lat_tiles = t // tm, h_ctx.shape[0] // tm, h_lat.shape[0] // tm
    n = w_in_bf16.shape[1]
    tn = 512
    pieces = INPROJ_NORM_CHUNKS
    chunk = tm // pieces
    assert n // tn >= pieces and ctx_tiles >= 1

    def piece(j):
        return jnp.minimum(j, pieces - 1)

    def nxt(i):
        return jnp.minimum(i + 1, n_tiles - 1)

    ctx_chunk = lambda i, j: (jnp.where(nxt(i) < ctx_tiles, nxt(i) * pieces + piece(j), ctx_tiles * pieces - 1), 0)
    lat_chunk = lambda i, j: (jnp.where(nxt(i) >= ctx_tiles, (nxt(i) - ctx_tiles) * pieces + piece(j), 0), 0)
    return pl.pallas_call(
        functools.partial(_inproj_body, ctx_tiles=ctx_tiles, chunk=chunk),
        grid=(n_tiles, n // tn),
        in_specs=[pl.BlockSpec((tm, d), lambda i, j: (0, 0)),
                  pl.BlockSpec((None, 6, d), lambda i, j: (cond_of_tile(0), 0, 0)),
                  pl.BlockSpec((chunk, d), ctx_chunk),
                  pl.BlockSpec((chunk, d), lat_chunk),
                  pl.BlockSpec((None, 6, d), lambda i, j: (cond_of_tile(nxt(i)), 0, 0)),
                  pl.BlockSpec((1, d), lambda i, j: (0, 0)),
                  pl.BlockSpec((d, tn), lambda i, j: (0, j))],
        out_specs=pl.BlockSpec((tm, tn), lambda i, j: (i, j)),
        out_shape=jax.ShapeDtypeStruct((t, n), F32),
        scratch_shapes=[pltpu.VMEM((2, tm, d), BF16)],
        compiler_params=_params("arbitrary", "arbitrary"),
        name="norm_modulate_in_proj",
    )(h_ctx, mod, h_ctx, h_lat, mod, norm1[None, :], w_in_bf16)


def _hgrn_constants(c):
    nl = int(math.log2(c))
    idx = np.arange(c)
    e = np.zeros((nl + 2, c, c), np.float32)
    m = np.zeros((nl + 1, c, c), np.float32)
    e[0] = idx[None, :] <= idx[:, None]
    e[1] = idx[None, :] > idx[:, None]
    m[0] = np.eye(c)
    for li in range(nl):
        s = c >> (li + 1)
        parent = idx // (2 * s)
        right = (idx % (2 * s)) >= s
        ref = parent * 2 * s + s - 1
        for i in range(c):
            if right[i]:
                e[2 + li, i, ref[i] + 1:i + 1] = 1.0
            else:
                e[2 + li, i, i + 1:ref[i] + 1] = 1.0
        m[1 + li] = right[:, None] & ~right[None, :] & (parent[:, None] == parent[None, :])
    keep = [0, 1] + [2 + li for li in range(nl) if (c >> (li + 1)) < SUBLANES]
    e = e[keep]
    e2 = np.stack([e, e[:, ::-1, ::-1]]).reshape(2, len(keep) * c, c)
    m2 = np.stack([m, m[:, ::-1, ::-1]])
    return jnp.asarray(e2, BF16), jnp.asarray(m2, F32)


def _hgrn_body(*refs, seq, chunk, unroll, heads, has_s0, emit_state, n_alias):
    q_ref, ff_ref, fb_ref, i_ref, g_ref, lb_ref, on_ref, e_ref, m_ref = refs[:9]
    pos = 9
    s0_ref = None
    if has_s0:
        s0_ref = refs[pos]
        pos += 1
    pos += n_alias
    o_ref = refs[pos]
    pos += 1
    if emit_state:
        st_ref = refs[pos]
        pos += 1
    of_ref, ob_ref = refs[pos], refs[pos + 1]
    c = chunk
    n_chunks = seq // c
    assert seq % c == 0 and n_chunks % unroll == 0
    n_levels = m_ref.shape[1] - 1
    gate_refs = (ff_ref, fb_ref)
    out_refs = (of_ref, ob_ref)

    def chunk_step(c0, d, hh, st):
        rows = pl.ds(c0, c)
        lanes = slice(hh * HEAD_W, (hh + 1) * HEAD_W)
        lb = lb_ref[d:d + 1, lanes]
        f = lb + (1.0 - lb) * _sigmoid(gate_refs[d][rows, lanes])
        g = jnp.log2(f)
        k = 1.0 - f
        q = _silu(q_ref[rows, lanes])
        v = i_ref[rows, lanes].astype(BF16)
        g_hi = g.astype(BF16)
        g_lo = (g - g_hi.astype(F32)).astype(BF16)
        g2 = _dot(e_ref[d], jnp.concatenate([g_hi, g_lo], axis=1))
        gsum = g2[:, 0:HEAD_W] + g2[:, HEAD_W:2 * HEAD_W]
        cum = gsum[0:c]
        x_cum = jnp.exp2(cum)
        x_tail = jnp.exp2(gsum[c:2 * c])
        row = lax.broadcasted_iota(jnp.int32, (c, HEAD_W), 0)
        s = m_ref[d, 0] * _dot_nt(q.astype(BF16), k.astype(BF16))
        n_matmul_levels = 0
        for lv in range(n_levels):
            half = c >> (lv + 1)
            if half >= SUBLANES:
                ref_row = half - 1 if d == 0 else half
                ref = jnp.concatenate(
                    [jnp.broadcast_to(cum[b * 2 * half + ref_row:b * 2 * half + ref_row + 1, :], (2 * half, HEAD_W))
                     for b in range(c // (2 * half))], axis=0)
                near = ((row % (2 * half)) >= half) == (d == 0)
                x_l = jnp.exp2(jnp.where(near, cum - ref, ref - cum))
            else:
                x_l = jnp.exp2(gsum[(2 + n_matmul_levels) * c:(3 + n_matmul_levels) * c])
                n_matmul_levels += 1
            s = s + m_ref[d, 1 + lv] * _dot_nt((q * x_l).astype(BF16), (k * x_l).astype(BF16))
        o = _dot_nt((q * x_cum).astype(BF16), st.astype(BF16)) + _dot(s.astype(BF16), v)
        out_refs[d][rows, lanes] = o
        total = x_cum[c - 1:c, :] if d == 0 else x_cum[0:1, :]
        return st * total + _dot_tn(v, (k * x_tail).astype(BF16))

    if has_s0:
        states0 = tuple(s0_ref[d, hh].T for hh in range(heads) for d in range(2))
    else:
        states0 = tuple(jnp.zeros((HEAD_W, HEAD_W), F32) for _ in range(2 * heads))

    def loop(t, states):
        states = list(states)
        for u in range(unroll):
            j = t * unroll + u
            for hh in range(heads):
                states[2 * hh] = chunk_step(_aligned(j * c, c), 0, hh, states[2 * hh])
                states[2 * hh + 1] = chunk_step(_aligned((n_chunks - 1 - j) * c, c), 1, hh, states[2 * hh + 1])
        return tuple(states)

    if n_chunks == unroll:
        states = loop(0, states0)
    else:
        states = lax.fori_loop(0, n_chunks // unroll, loop, states0)
    for hh in range(heads):
        lanes = slice(hh * HEAD_W, (hh + 1) * HEAD_W)
        o = of_ref[:, lanes] + ob_ref[:, lanes]
        o_ref[:, lanes] = (_rms(o, on_ref[...], HEAD_W) * _silu(g_ref[:, lanes])).astype(o_ref.dtype)
        if emit_state:
            st_ref[0, hh] = states[2 * hh].T
            st_ref[1, hh] = states[2 * hh + 1].T


def _hgrn(proj, row_block0, n_seq, seq, lower, onorm, consts, s0, mixed_prev, state_prev, layer, depth):
    e_mat, masks = consts
    latent = s0 is not None
    hps = next(n for n in (4, 2, 1) if 5 * 2 * seq * n * HEAD_W * 4 <= HGRN_INPUT_VMEM)
    width = hps * HEAD_W

    def col(cb):
        return pl.BlockSpec((seq, width), lambda b, h, cb=cb: (row_block0 + b, cb // hps + h))

    state_spec = pl.BlockSpec((None, None, 2, hps, HEAD_W, HEAD_W), lambda b, h: (b, layer, 0, h, 0, 0))
    in_specs = [col(COL_A_Q), col(COL_A_FF), col(COL_A_FB), col(COL_A_I), col(COL_A_G),
                pl.BlockSpec((2, width), lambda b, h: (0, h)),
                pl.BlockSpec((1, HEAD_W), lambda b, h: (0, 0)),
                pl.BlockSpec(e_mat.shape, lambda b, h: (0, 0, 0)),
                pl.BlockSpec(masks.shape, lambda b, h: (0, 0, 0, 0))]
    args = [proj, proj, proj, proj, proj, lower, onorm[None, :], e_mat, masks]
    if latent:
        in_specs.append(state_spec)
        args.append(s0)
        prev = [mixed_prev]
    else:
        prev = [] if state_prev is None else [state_prev]
    alias_specs, alias_args, aliases = _alias_kwargs(len(args), prev, 0 if latent else 1)
    out_specs = [pl.BlockSpec((seq, width), lambda b, h: (row_block0 + b, h))]
    out_shape = [jax.ShapeDtypeStruct((proj.shape[0], GROUP_W), BF16)]
    if not latent:
        out_specs.append(state_spec)
        out_shape.append(jax.ShapeDtypeStruct((n_seq, depth, 2, N_HEADS, HEAD_W, HEAD_W), F32))
    res = pl.pallas_call(
        functools.partial(_hgrn_body, seq=seq, chunk=HGRN_CHUNK, unroll=min(HGRN_UNROLL, seq // HGRN_CHUNK),
                          heads=hps, has_s0=latent,
                          emit_state=not latent, n_alias=len(prev)),
        grid=(n_seq, N_HEADS // hps),
        in_specs=in_specs + alias_specs, out_specs=out_specs, out_shape=out_shape,
        input_output_aliases=aliases,
        scratch_shapes=[pltpu.VMEM((seq, width), F32), pltpu.VMEM((seq, width), F32)],
        compiler_params=_params("parallel", "parallel"),
        name="hgrn2_latent" if latent else "hgrn2_context",
    )(*args, *alias_args)
    return (res[0], None) if latent else res


def _with_ones(v):
    return jnp.concatenate([v, jnp.ones_like(v)], axis=1)


def _softmax_pv(scores, values1):
    mx = functools.reduce(jnp.maximum, [jnp.max(s, axis=-1, keepdims=True) for s in scores])
    acc = functools.reduce(lambda a, b: a + b,
                           [_dot(jnp.exp2(s - mx).astype(BF16), v) for s, v in zip(scores, values1)])
    return acc[:, 0:HEAD_W] / acc[:, HEAD_W:HEAD_W + 1]


def _diff_pv(s0, s1, lam, values1):
    return _softmax_pv([s0], [values1]) - lam * _softmax_pv([s1], [values1])


def _lane_lt(shape, n):
    return lax.broadcasted_iota(jnp.int32, shape, len(shape) - 1) < n


def _rms_head(x, gain):
    return _rms(x, gain, HEAD_W)


def _rms_halves(x, gain2):
    lo = _lane_lt(x.shape, DF_DQK)
    sq = x * x
    ss_lo = jnp.sum(jnp.where(lo, sq, 0.0), axis=-1, keepdims=True)
    ss_hi = jnp.sum(sq, axis=-1, keepdims=True) - ss_lo
    inv = jnp.where(lo, lax.rsqrt(ss_lo * (1.0 / DF_DQK) + EPS), lax.rsqrt(ss_hi * (1.0 / DF_DQK) + EPS))
    return x * inv * gain2


def _lambda(lam_ref, lam_init):
    l = lam_ref[...]
    return (jnp.exp(jnp.sum(l[0:1] * l[1:2], axis=-1, keepdims=True))
            - jnp.exp(jnp.sum(l[2:3] * l[3:4], axis=-1, keepdims=True)) + lam_init)


N_CTX_ATTN_INPUTS = 17


def _ctx_attn_body(*refs, lam_init, kv_heads):
    (bq_ref, bk_ref, bv_ref, cq_ref, ck_ref, cv_ref, dq_ref, dk_ref, dv_ref,
     naq_ref, nak_ref, gq_ref, gk_ref, dfq_ref, dfk_ref, sub_ref, lam_ref) = refs[:N_CTX_ATTN_INPUTS]
    ob_ref, oc_ref, od_ref, kb_ref, vb_ref, kc_ref, vc_ref, kd_ref, vd_ref = refs[-9:]
    group = N_HEADS // GQA_KV_HEADS
    scale = LOG2E * HEAD_W ** -0.5
    lam = _lambda(lam_ref, lam_init)
    for n in range(kv_heads):
        kv_lanes = slice(n * HEAD_W, (n + 1) * HEAD_W)
        kc = _rms_head(ck_ref[:, kv_lanes], gk_ref[...])
        vc = cv_ref[:, kv_lanes]
        kc_ref[n] = kc
        vc_ref[n] = vc
        kc16 = kc.astype(BF16)
        vc1 = _with_ones(vc.astype(BF16))
        for g in range(group):
            h = n * group + g
            lanes = slice(h * HEAD_W, (h + 1) * HEAD_W)
            qc = (_rms_head(cq_ref[:, lanes], gq_ref[...]) * scale).astype(BF16)
            oc_ref[:, lanes] = _softmax_pv([_dot_nt(qc, kc16)], [vc1]).astype(oc_ref.dtype)
            kb = _rms_head(bk_ref[:, lanes], nak_ref[...])
            vb = bv_ref[:, lanes]
            kb_ref[h] = kb
            vb_ref[h] = vb
            qb = (_rms_head(bq_ref[:, lanes], naq_ref[...]) * scale).astype(BF16)
            ob_ref[:, lanes] = _softmax_pv([_dot_nt(qb, kb.astype(BF16))],
                                           [_with_ones(vb.astype(BF16))]).astype(ob_ref.dtype)
            kd = _rms_halves(dk_ref[:, lanes], dfk_ref[...])
            vd = dv_ref[:, lanes]
            kd_ref[h, 0] = kd[:, 0:DF_DQK]
            kd_ref[h, 1] = kd[:, DF_DQK:2 * DF_DQK]
            vd_ref[h] = vd
            qd = (_rms_halves(dq_ref[:, lanes], dfq_ref[...]) * (LOG2E * DF_DQK ** -0.5)).astype(BF16)
            lo = _lane_lt(kd.shape, DF_DQK)
            s0 = _dot_nt(qd, jnp.where(lo, kd, 0.0).astype(BF16))
            s1 = _dot_nt(qd, jnp.where(lo, 0.0, kd).astype(BF16))
            od = _diff_pv(s0, s1, lam, _with_ones(vd.astype(BF16)))
            od_ref[:, lanes] = (_rms_head(od, sub_ref[...]) * (1.0 - lam_init)).astype(od_ref.dtype)


def _context_attention(proj, n_seq, seq, gains, lam_init, layer, depth, caches_prev):
    na_qn, na_kn, gqa_qn, gqa_kn, df_qn, df_kn, df_subln, df_lam = gains
    group = N_HEADS // GQA_KV_HEADS
    kvs = CTX_KV_HEADS_PER_STEP
    n_heads = kvs * group
    width = n_heads * HEAD_W

    def heads(cb):
        return pl.BlockSpec((seq, width), lambda b, n, cb=cb: (b, cb // n_heads + n))

    def kv_head(cb):
        return pl.BlockSpec((seq, kvs * HEAD_W), lambda b, n, cb=cb: (b, cb // kvs + n))

    vec = pl.BlockSpec((1, HEAD_W), lambda b, n: (0, 0))
    cache_heads = pl.BlockSpec((None, None, n_heads, seq, HEAD_W), lambda b, n: (b, layer, n, 0, 0))
    cache_kv = pl.BlockSpec((None, None, kvs, seq, HEAD_W), lambda b, n: (b, layer, n, 0, 0))
    mixed = pl.BlockSpec((seq, width), lambda b, n: (b, n))
    mixed_shape = jax.ShapeDtypeStruct((proj.shape[0], GROUP_W), BF16)
    cache4 = jax.ShapeDtypeStruct((n_seq, depth, N_HEADS, seq, HEAD_W), F32)
    cache2 = jax.ShapeDtypeStruct((n_seq, depth, GQA_KV_HEADS, seq, HEAD_W), F32)
    cache_dk = jax.ShapeDtypeStruct((n_seq, depth, N_HEADS, 2, seq, DF_DQK), F32)
    args = [proj] * 9 + [na_qn[None, :], na_kn[None, :], gqa_qn[None, :], gqa_kn[None, :],
                         jnp.tile(df_qn, 2)[None, :], jnp.tile(df_kn, 2)[None, :], df_subln[None, :], df_lam]
    assert len(args) == N_CTX_ATTN_INPUTS
    alias_specs, alias_args, aliases = _alias_kwargs(len(args), caches_prev, 3)
    return pl.pallas_call(
        functools.partial(_ctx_attn_body, lam_init=lam_init, kv_heads=kvs),
        grid=(n_seq, GQA_KV_HEADS // kvs),
        in_specs=[heads(COL_B_Q), heads(COL_B_K), heads(COL_B_V),
                  heads(COL_C_Q), kv_head(COL_C_K), kv_head(COL_C_V),
                  heads(COL_D_Q), heads(COL_D_K), heads(COL_D_V),
                  vec, vec, vec, vec, vec, vec, vec,
                  pl.BlockSpec((4, DF_DQK), lambda b, n: (0, 0))] + alias_specs,
        out_specs=[mixed, mixed, mixed, cache_heads, cache_heads, cache_kv, cache_kv,
                   pl.BlockSpec((None, None, n_heads, 2, seq, DF_DQK), lambda b, n: (b, layer, n, 0, 0, 0)),
                   cache_heads],
        out_shape=[mixed_shape, mixed_shape, mixed_shape, cache4, cache4, cache2, cache2, cache_dk, cache4],
        input_output_aliases=aliases,
        compiler_params=_params("parallel", "parallel"),
        name="context_attention",
    )(*args, *alias_args)


def _rope_tables(n_tokens, rot_dim):
    t = np.arange(n_tokens)
    row = (t // GRID_W).astype(np.float32)
    col = (t % GRID_W).astype(np.float32)
    n_freq = rot_dim // 4
    inv = (np.float32(ROPE_THETA) ** (-np.arange(n_freq, dtype=np.float32) / np.float32(n_freq))).astype(np.float32)
    ang = np.concatenate([row[:, None] * inv, col[:, None] * inv], axis=-1).astype(np.float32)
    cos, sin, zero = np.cos(ang), np.sin(ang), np.zeros_like(ang)
    reps = HEAD_W // rot_dim
    a = np.tile(np.concatenate([cos, cos], axis=-1), (1, reps))
    b = np.tile(np.concatenate([-sin, zero], axis=-1), (1, reps))
    c = np.tile(np.concatenate([zero, sin], axis=-1), (1, reps))
    return jnp.asarray(np.stack([a, b, c]), F32)


def _rope(x, tab_ref, half):
    return (x * tab_ref[0] + pltpu.roll(x, HEAD_W - half, 1) * tab_ref[1]
            + pltpu.roll(x, half, 1) * tab_ref[2])


def _na_body(q_ref, k_ref, v_ref, ck_ref, cv_ref, bias_ref, qn_ref, kn_ref, prev_ref, o_ref,
             qs_ref, ks_ref, vs_ref, *, seq):
    del prev_ref
    rows = seq // GRID_W
    n_win = WIN_ROWS * GRID_W
    qs_ref[...] = (_rms_head(q_ref[...], qn_ref[...]) * (LOG2E * HEAD_W ** -0.5)).astype(BF16)
    ks_ref[...] = _rms_head(k_ref[...], kn_ref[...]).astype(BF16)
    vs_ref[...] = _with_ones(v_ref[...].astype(BF16))
    ck = ck_ref[...].astype(BF16)
    cv = _with_ones(cv_ref[...].astype(BF16))

    def row_step(r, carry):
        start = jnp.clip(r - WIN_ROWS // 2, 0, rows - WIN_ROWS)
        win = pl.ds(pl.multiple_of(start * GRID_W, GRID_W), n_win)
        qrows = pl.ds(pl.multiple_of(r * GRID_W, GRID_W), GRID_W)
        q = qs_ref[qrows, :]
        s_win = _dot_nt(q, ks_ref[win, :]) + bias_ref[start - r + (WIN_ROWS - 1)]
        s_ctx = _dot_nt(q, ck)
        o_ref[qrows, :] = _softmax_pv([s_win, s_ctx], [vs_ref[win, :], cv]).astype(o_ref.dtype)
        return carry

    lax.fori_loop(0, rows, row_step, 0, unroll=min(rows, 32))


def _na_bias(rpb):
    col = np.arange(GRID_W)
    col_start = np.clip(col - WIN_COLS // 2, 0, GRID_W - WIN_COLS)
    col_ok = (col[None, :] >= col_start[:, None]) & (col[None, :] < col_start[:, None] + WIN_COLS)
    dc = np.clip(col[None, :] - col[:, None] + WIN_COLS - 1, 0, 2 * WIN_COLS - 2).reshape(-1)
    onehot = (np.arange(2 * WIN_COLS - 1)[:, None] == dc[None, :]).astype(np.float32)
    per_dr = jnp.einsum('hdc,cn->hdn', rpb.astype(F32), jnp.asarray(onehot), precision=lax.Precision.HIGHEST)
    per_dr = jnp.where(col_ok[None, None], LOG2E * per_dr.reshape(rpb.shape[0], -1, GRID_W, GRID_W), NEG_INF)
    wins = jnp.stack([per_dr[:, o:o + WIN_ROWS] for o in range(WIN_ROWS)], axis=1)
    return wins.transpose(0, 1, 3, 2, 4).reshape(rpb.shape[0], WIN_ROWS, GRID_W, WIN_ROWS * GRID_W)


def _latent_na(proj, row_block0, n_seq, seq, cache_k, cache_v, layer, bias, na_qn, na_kn, mixed_prev):
    past = cache_k.shape[3]

    def col(cb):
        return pl.BlockSpec((seq, HEAD_W), lambda b, h, cb=cb: (row_block0 + b, cb + h))

    cache = pl.BlockSpec((None, None, None, past, HEAD_W), lambda b, h: (b, layer, h, 0, 0))
    vec = pl.BlockSpec((1, HEAD_W), lambda b, h: (0, 0))
    return pl.pallas_call(
        functools.partial(_na_body, seq=seq),
        grid=(n_seq, N_HEADS),
        in_specs=[col(COL_B_Q), col(COL_B_K), col(COL_B_V), cache, cache,
                  pl.BlockSpec((None, WIN_ROWS, GRID_W, WIN_ROWS * GRID_W), lambda b, h: (h, 0, 0, 0)),
                  vec, vec, ANY_SPEC],
        out_specs=pl.BlockSpec((seq, HEAD_W), lambda b, h: (row_block0 + b, h)),
        out_shape=jax.ShapeDtypeStruct(mixed_prev.shape, BF16),
        input_output_aliases={8: 0},
        scratch_shapes=[pltpu.VMEM((seq, HEAD_W), BF16), pltpu.VMEM((seq, HEAD_W), BF16),
                        pltpu.VMEM((seq, 2 * HEAD_W), BF16)],
        compiler_params=_params("parallel", "parallel"),
        name="latent_neighbourhood_attention",
    )(proj, proj, proj, cache_k, cache_v, bias, na_qn[None, :], na_kn[None, :], mixed_prev)


def _gqa_body(q_ref, k_ref, v_ref, ck_ref, cv_ref, rope_ref, qn_ref, kn_ref, prev_ref, o_ref,
              qs_ref, ks_ref, vs_ref, *, seq, tq):
    del prev_ref
    group = N_HEADS // GQA_KV_HEADS
    half = HEAD_W // 2
    ks_ref[0:seq, :] = _rope(_rms_head(k_ref[...], kn_ref[...]), rope_ref, half).astype(BF16)
    ks_ref[seq:, :] = ck_ref[...].astype(BF16)
    vs_ref[0:seq, :] = _with_ones(v_ref[...].astype(BF16))
    vs_ref[seq:, :] = _with_ones(cv_ref[...].astype(BF16))
    for g in range(group):
        q = _rms_head(q_ref[:, g * HEAD_W:(g + 1) * HEAD_W], qn_ref[...]) * (LOG2E * HEAD_W ** -0.5)
        qs_ref[g] = _rope(q, rope_ref, half).astype(BF16)
    kk = ks_ref[...]
    vv = vs_ref[...]
    for g in range(group):
        def q_step(i, carry, g=g):
            qrows = pl.ds(pl.multiple_of(i * tq, tq), tq)
            o = _softmax_pv([_dot_nt(qs_ref[g, qrows, :], kk)], [vv])
            o_ref[qrows, g * HEAD_W:(g + 1) * HEAD_W] = o.astype(o_ref.dtype)
            return carry

        lax.fori_loop(0, seq // tq, q_step, 0, unroll=min(seq // tq, 8))


def _latent_gqa(proj, row_block0, n_seq, seq, cache_k, cache_v, layer, rope, gqa_qn, gqa_kn, mixed_prev):
    past = cache_k.shape[3]
    group = N_HEADS // GQA_KV_HEADS
    tq = 256
    cache = pl.BlockSpec((None, None, None, past, HEAD_W), lambda b, n: (b, layer, n, 0, 0))
    vec = pl.BlockSpec((1, HEAD_W), lambda b, n: (0, 0))
    return pl.pallas_call(
        functools.partial(_gqa_body, seq=seq, tq=tq),
        grid=(n_seq, GQA_KV_HEADS),
        in_specs=[pl.BlockSpec((seq, group * HEAD_W), lambda b, n: (row_block0 + b, COL_C_Q // group + n)),
                  pl.BlockSpec((seq, HEAD_W), lambda b, n: (row_block0 + b, COL_C_K + n)),
                  pl.BlockSpec((seq, HEAD_W), lambda b, n: (row_block0 + b, COL_C_V + n)),
                  cache, cache,
                  pl.BlockSpec((3, seq, HEAD_W), lambda b, n: (0, 0, 0)),
                  vec, vec, ANY_SPEC],
        out_specs=pl.BlockSpec((seq, group * HEAD_W), lambda b, n: (row_block0 + b, n)),
        out_shape=jax.ShapeDtypeStruct(mixed_prev.shape, BF16),
        input_output_aliases={8: 0},
        scratch_shapes=[pltpu.VMEM((group, seq, HEAD_W), BF16),
                        pltpu.VMEM((seq + past, HEAD_W), BF16),
                        pltpu.VMEM((seq + past, 2 * HEAD_W), BF16)],
        compiler_params=_params("parallel", "parallel"),
        name="latent_gqa_attention",
    )(proj, proj, proj, cache_k, cache_v, rope, gqa_qn[None, :], gqa_kn[None, :], mixed_prev)


def _diff_body(q_ref, k_ref, v_ref, ck_ref, cv_ref, rope_ref, qn_ref, kn_ref, sub_ref, lam_ref, prev_ref, o_ref,
               qs_ref, k0_ref, k1_ref, vs_ref, *, seq, tq, lam_init):
    del prev_ref
    half = DF_DQK // 2
    k = _rope(_rms_halves(k_ref[...], kn_ref[...]), rope_ref, half)
    lo = _lane_lt(k.shape, DF_DQK)
    k0_ref[0:seq, :] = jnp.where(lo, k, 0.0).astype(BF16)
    k1_ref[0:seq, :] = jnp.where(lo, 0.0, k).astype(BF16)
    ck = ck_ref[...]
    lo_c = _lane_lt(ck.shape, DF_DQK)
    k0_ref[seq:, :] = jnp.where(lo_c, ck, 0.0).astype(BF16)
    k1_ref[seq:, :] = jnp.where(lo_c, 0.0, ck).astype(BF16)
    vs_ref[0:seq, :] = _with_ones(v_ref[...].astype(BF16))
    vs_ref[seq:, :] = _with_ones(cv_ref[...].astype(BF16))
    q = _rms_halves(q_ref[...], qn_ref[...]) * (LOG2E * DF_DQK ** -0.5)
    qs_ref[...] = _rope(q, rope_ref, half).astype(BF16)
    lam = _lambda(lam_ref, lam_init)
    k0 = k0_ref[...]
    k1 = k1_ref[...]
    vv = vs_ref[...]

    def q_step(i, carry):
        qrows = pl.ds(pl.multiple_of(i * tq, tq), tq)
        qb = qs_ref[qrows, :]
        o = _diff_pv(_dot_nt(qb, k0), _dot_nt(qb, k1), lam, vv)
        o_ref[qrows, :] = (_rms_head(o, sub_ref[...]) * (1.0 - lam_init)).astype(o_ref.dtype)
        return carry

    lax.fori_loop(0, seq // tq, q_step, 0, unroll=min(seq // tq, 4))


def _latent_diff(proj, row_block0, n_seq, seq, cache_k2, cache_v, layer, rope, df_qn, df_kn, df_subln, df_lam,
                 lam_init, mixed_prev):
    past = cache_k2.shape[3]
    tq = 256

    def col(cb):
        return pl.BlockSpec((seq, HEAD_W), lambda b, h, cb=cb: (row_block0 + b, cb + h))

    cache = pl.BlockSpec((None, None, None, past, HEAD_W), lambda b, h: (b, layer, h, 0, 0))
    vec = pl.BlockSpec((1, HEAD_W), lambda b, h: (0, 0))
    kv_scratch = pltpu.VMEM((seq + past, HEAD_W), BF16)
    return pl.pallas_call(
        functools.partial(_diff_body, seq=seq, tq=tq, lam_init=lam_init),
        grid=(n_seq, N_HEADS),
        in_specs=[col(COL_D_Q), col(COL_D_K), col(COL_D_V), cache, cache,
                  pl.BlockSpec((3, seq, HEAD_W), lambda b, h: (0, 0, 0)),
                  vec, vec, vec, pl.BlockSpec((4, DF_DQK), lambda b, h: (0, 0)), ANY_SPEC],
        out_specs=pl.BlockSpec((seq, HEAD_W), lambda b, h: (row_block0 + b, h)),
        out_shape=jax.ShapeDtypeStruct(mixed_prev.shape, BF16),
        input_output_aliases={10: 0},
        scratch_shapes=[pltpu.VMEM((seq, HEAD_W), BF16), kv_scratch, kv_scratch,
                        pltpu.VMEM((seq + past, 2 * HEAD_W), BF16)],
        compiler_params=_params("parallel", "parallel"),
        name="latent_diff_attention",
    )(proj, proj, proj, cache_k2, cache_v, rope, jnp.tile(df_qn, 2)[None, :], jnp.tile(df_kn, 2)[None, :],
      df_subln[None, :], df_lam, mixed_prev)


def _first_max(vals):
    best = vals[0]
    idx = jnp.zeros(best.shape, jnp.int32)
    for i in range(1, len(vals)):
        better = vals[i] > best
        best = jnp.where(better, vals[i], best)
        idx = jnp.where(better, i, idx)
    return best, idx


def _pick(vals, idx):
    out = vals[0]
    for i in range(1, len(vals)):
        out = jnp.where(idx == i, vals[i], out)
    return out


def _outproj_body(ma_ref, mb_ref, mc_ref, md_ref, w_ref, hc_ref, hl_ref, mod_ref, n2_ref, rw_ref, rb_ref,
                  h1_ref, x2_ref, idx_ref, gate_ref, mixed_ref, *, ctx_tiles):
    for g, m_ref in enumerate((ma_ref, mb_ref, mc_ref, md_ref)):
        mixed_ref[:, g * GROUP_W:(g + 1) * GROUP_W] = m_ref[...]
    y = mod_ref[2:3, :] * _dot(mixed_ref[...], w_ref[...])

    def residual(h_ref):
        h1_ref[...] = h_ref[...] + y

    is_ctx = pl.program_id(0) < ctx_tiles
    pl.when(is_ctx)(lambda: residual(hc_ref))
    pl.when(jnp.logical_not(is_ctx))(lambda: residual(hl_ref))
    h1 = h1_ref[...]
    x2 = _rms(h1, n2_ref[...], D_MODEL) * (1.0 + mod_ref[4:5, :]) + mod_ref[3:4, :]
    x2_ref[...] = _pack_bf16_pairs(x2)
    x_hi = x2.astype(BF16)
    x_lo = (x2 - x_hi.astype(F32)).astype(BF16)
    acc = _dot(x_hi, rw_ref[...])
    logits_tok = acc[:, 0:HEAD_W] + acc[:, HEAD_W:2 * HEAD_W] + _dot(x_lo, rw_ref[:, 0:HEAD_W])
    logits = logits_tok.T[0:N_EXPERTS, :]
    aff_all = _sigmoid(logits)
    sel_all = aff_all + rb_ref[...]
    aff = [aff_all[e:e + 1, :] for e in range(N_EXPERTS)]
    sel = [sel_all[e:e + 1, :] for e in range(N_EXPERTS)]
    neg = jnp.full(sel[0].shape, -jnp.inf, F32)
    scores = []
    for g in range(N_EXP_GROUPS):
        grp = sel[g * EXP_PER_GROUP:(g + 1) * EXP_PER_GROUP]
        m1, i1 = _first_max(grp)
        m2, _ = _first_max([jnp.where(i1 == j, neg, grp[j]) for j in range(EXP_PER_GROUP)])
        scores.append(m1 + m2)
    _, g_best = _first_max(scores)
    in_sel = [_pick([sel[g * EXP_PER_GROUP + j] for g in range(N_EXP_GROUPS)], g_best)
              for j in range(EXP_PER_GROUP)]
    in_aff = [_pick([aff[g * EXP_PER_GROUP + j] for g in range(N_EXP_GROUPS)], g_best)
              for j in range(EXP_PER_GROUP)]
    _, l1 = _first_max(in_sel)
    _, l2 = _first_max([jnp.where(l1 == j, neg, in_sel[j]) for j in range(EXP_PER_GROUP)])
    w1 = _pick(in_aff, l1)
    w2 = _pick(in_aff, l2)
    idx_ref[0:1, :] = g_best * EXP_PER_GROUP + l1
    idx_ref[1:2, :] = g_best * EXP_PER_GROUP + l2
    gate_ref[0:1, :] = w1 / (w1 + w2)
    gate_ref[1:2, :] = w2 / (w1 + w2)


def _output_projection(mixed4, w_out_bf16, h_ctx, h_lat, mod, norm2, router_w, router_b, cond_of_tile, tm):
    d = h_ctx.shape[1]
    t = h_ctx.shape[0] + h_lat.shape[0]
    ctx_tiles = h_ctx.shape[0] // tm
    slab = pl.BlockSpec((tm, GROUP_W), lambda i: (i, 0))
    rw_hi = router_w.astype(BF16)
    rw_lo = (router_w - rw_hi.astype(F32)).astype(BF16)
    pad = ((0, 0), (0, HEAD_W - N_EXPERTS))
    router_split = jnp.concatenate([jnp.pad(rw_hi, pad), jnp.pad(rw_lo, pad)], axis=1)
    return pl.pallas_call(
        functools.partial(_outproj_body, ctx_tiles=ctx_tiles),
        grid=(t // tm,),
        in_specs=[slab, slab, slab, slab,
                  pl.BlockSpec((d, d), lambda i: (0, 0))] + _group_specs(tm, d, ctx_tiles) + [
                  pl.BlockSpec((None, 6, d), lambda i: (cond_of_tile(i), 0, 0)),
                  pl.BlockSpec((1, d), lambda i: (0, 0)),
                  pl.BlockSpec((d, 2 * HEAD_W), lambda i: (0, 0)),
                  pl.BlockSpec((N_EXPERTS, 1), lambda i: (0, 0))],
        out_specs=[pl.BlockSpec((tm, d), lambda i: (i, 0)),
                   pl.BlockSpec((tm, d // 2), lambda i: (i, 0)),
                   pl.BlockSpec((2, tm), lambda i: (0, i)),
                   pl.BlockSpec((2, tm), lambda i: (0, i))],
        out_shape=[jax.ShapeDtypeStruct((t, d), F32), jax.ShapeDtypeStruct((t, d // 2), jnp.uint32),
                   jax.ShapeDtypeStruct((2, t), jnp.int32), jax.ShapeDtypeStruct((2, t), F32)],
        scratch_shapes=[pltpu.VMEM((tm, d), BF16)],
        compiler_params=_params("parallel", vmem=VMEM_LIMIT_PROJ),
        name="out_proj_residual_router",
    )(*mixed4, w_out_bf16, h_ctx, h_lat, mod, norm2[None, :], router_split, router_b[:, None])


def _vmem_row(ref, base, u):
    return ref.at[pl.ds(base, SUBLANES), :].at[pl.ds(u, 1), :]


def _for_row_groups(n_rows, fn):
    def body(g, carry):
        base = pl.multiple_of(g * SUBLANES, SUBLANES)
        for u in range(SUBLANES):
            fn(base, u)
        return carry

    lax.fori_loop(0, n_rows // SUBLANES, body, 0)


def _rows_wait(src_hbm, dst, sem, n_rows):
    pltpu.make_async_copy(src_hbm.at[pl.ds(0, n_rows), :], dst.at[pl.ds(0, n_rows), :], sem).wait()


def _dispatch_body(dest_ref, ps_ref, pe_ref, x_ref, xs_hbm, zero_ref, sem, zsem, *, n_tok, tile):
    i = pl.program_id(0)

    def zero_copy(e):
        first = pl.multiple_of(pe_ref[e] - MOE_ROWS, MOE_ROWS)
        return pltpu.make_async_copy(zero_ref, xs_hbm.at[pl.ds(first, MOE_ROWS), :], zsem)

    @pl.when(i == 0)
    def _():
        zero_ref[...] = jnp.zeros(zero_ref.shape, zero_ref.dtype)
        for e in range(N_EXPERTS):
            @pl.when(pe_ref[e] > ps_ref[e])
            def _(e=e):
                zero_copy(e).start()
        for e in range(N_EXPERTS):
            @pl.when(pe_ref[e] > ps_ref[e])
            def _(e=e):
                zero_copy(e).wait()

    for k in range(2):
        def scatter_row(base, u, k=k):
            row = dest_ref[k * n_tok + i * tile + base + u]
            pltpu.make_async_copy(_vmem_row(x_ref, base, u), xs_hbm.at[pl.ds(row, 1), :], sem).start()

        _for_row_groups(tile, scatter_row)
    for k in range(2):
        _rows_wait(x_ref, xs_hbm, sem, tile)


def _dispatch(x2, dest, pad_start, pad_end, n_rows):
    t, d = x2.shape
    tile = next(m for m in DISPATCH_TILES if t % m == 0)
    grid_spec = pltpu.PrefetchScalarGridSpec(
        num_scalar_prefetch=3,
        grid=(t // tile,),
        in_specs=[pl.BlockSpec((tile, d), lambda i, dst, ps, pe: (i, 0))],
        out_specs=ANY_SPEC,
        scratch_shapes=[pltpu.VMEM((MOE_ROWS, d), x2.dtype), pltpu.SemaphoreType.DMA, pltpu.SemaphoreType.DMA],
    )
    return pl.pallas_call(
        functools.partial(_dispatch_body, n_tok=t, tile=tile),
        grid_spec=grid_spec,
        out_shape=jax.ShapeDtypeStruct((n_rows, d), x2.dtype),
        compiler_params=_params("arbitrary"),
        name="moe_dispatch",
    )(dest, pad_start, pad_end, x2)


def _expert_body(be_ref, nb_ref, nx_ref, x_ref, wg_hbm, wu_hbm, wd_hbm, o_ref,
                 stage_g, stage_u, stage_d, wg_bf, wu_bf, wd_bf, sem, *, layer):
    i = pl.program_id(0)

    def weight_copies(e):
        return (pltpu.make_async_copy(wg_hbm.at[layer, e], stage_g, sem.at[0]),
                pltpu.make_async_copy(wu_hbm.at[layer, e], stage_u, sem.at[1]),
                pltpu.make_async_copy(wd_hbm.at[layer, e], stage_d, sem.at[2]))

    @pl.when(i < nb_ref[0])
    def _():
        e = be_ref[i]

        @pl.when(i == 0)
        def _():
            for c in weight_copies(e):
                c.start()

        @pl.when((i == 0) | (e != be_ref[jnp.maximum(i - 1, 0)]))
        def _():
            for c in weight_copies(e):
                c.wait()
            wg_bf[...] = stage_g[...].astype(BF16)
            wu_bf[...] = stage_u[...].astype(BF16)
            wd_bf[...] = stage_d[...].astype(BF16)
            nxt = nx_ref[e]

            @pl.when(nxt < N_EXPERTS)
            def _():
                for c in weight_copies(nxt):
                    c.start()

        x = _unpack_bf16_pairs(x_ref[...]).astype(BF16)
        hdn = _silu(_dot(x, wg_bf[...])) * _dot(x, wu_bf[...])
        o_ref[...] = _pack_bf16_pairs(_dot(hdn.astype(BF16), wd_bf[...]))

    @pl.when(i >= nb_ref[0])
    def _():
        o_ref[...] = jnp.zeros(o_ref.shape, o_ref.dtype)


def _expert_blocks(xs, block_expert, n_used, next_expert, w_gate, w_up, w_down, layer):
    n_rows, half = xs.shape
    d = 2 * half
    ff = w_gate.shape[-1]
    grid_spec = pltpu.PrefetchScalarGridSpec(
        num_scalar_prefetch=3,
        grid=(n_rows // MOE_ROWS,),
        in_specs=[pl.BlockSpec((MOE_ROWS, half), lambda i, be, nb, nx: (jnp.minimum(i, nb[0] - 1), 0)),
                  ANY_SPEC, ANY_SPEC, ANY_SPEC],
        out_specs=pl.BlockSpec((MOE_ROWS, half), lambda i, be, nb, nx: (i, 0)),
        scratch_shapes=[pltpu.VMEM((d, ff), F32), pltpu.VMEM((d, ff), F32), pltpu.VMEM((ff, d), F32),
                        pltpu.VMEM((d, ff), BF16), pltpu.VMEM((d, ff), BF16), pltpu.VMEM((ff, d), BF16),
                        pltpu.SemaphoreType.DMA((3,))],
    )
    return pl.pallas_call(
        functools.partial(_expert_body, layer=layer),
        grid_spec=grid_spec,
        out_shape=jax.ShapeDtypeStruct((n_rows, half), jnp.uint32),
        compiler_params=_params("arbitrary"),
        name="moe_expert_blocks",
    )(block_expert, n_used, next_expert, xs, w_gate, w_up, w_down)


def _combine_body(dest_ref, h_ref, gate_ref, y_hbm, mod_ref, *rest, n_tok, tile, ctx_tiles):
    out_refs, (ybuf, sem) = rest[:-2], rest[-2:]
    i = pl.program_id(0)
    n_tiles = pl.num_programs(0)

    def start(blk, slot):
        for k in range(2):
            def gather_row(base, u, k=k):
                row = dest_ref[k * n_tok + blk * tile + base + u]
                pltpu.make_async_copy(y_hbm.at[pl.ds(row, 1), :], _vmem_row(ybuf.at[slot, k], base, u),
                                      sem.at[slot]).start()

            _for_row_groups(tile, gather_row)

    @pl.when(i == 0)
    def _():
        start(0, 0)

    @pl.when(i + 1 < n_tiles)
    def _():
        start(i + 1, (i + 1) % 2)

    slot = i % 2
    for k in range(2):
        _rows_wait(y_hbm, ybuf.at[slot, k], sem.at[slot], tile)
    gate = gate_ref[...]
    y0 = _unpack_bf16_pairs(ybuf[slot, 0])
    y1 = _unpack_bf16_pairs(ybuf[slot, 1])
    out = h_ref[...] + mod_ref[5:6, :] * (gate[:, 0:1] * y0 + gate[:, 1:2] * y1)
    @pl.when(i < ctx_tiles)
    def _():
        out_refs[0][...] = out

    @pl.when(i >= ctx_tiles)
    def _():
        out_refs[1][...] = out


def _combine(h1, yb, dest, gates, mod, cond_of_tile, tile, split_rows):
    t, d = h1.shape
    row_tile = pl.BlockSpec((tile, d), lambda i, dst: (i, 0))
    ctx_tiles = split_rows // tile
    out_specs = _group_specs(tile, d, ctx_tiles)
    out_shape = [jax.ShapeDtypeStruct((split_rows, d), F32), jax.ShapeDtypeStruct((t - split_rows, d), F32)]
    grid_spec = pltpu.PrefetchScalarGridSpec(
        num_scalar_prefetch=1,
        grid=(t // tile,),
        in_specs=[row_tile,
                  pl.BlockSpec((tile, 2), lambda i, dst: (i, 0)),
                  ANY_SPEC,
                  pl.BlockSpec((None, 6, d), lambda i, dst: (cond_of_tile(i), 0, 0))],
        out_specs=out_specs,
        scratch_shapes=[pltpu.VMEM((2, 2, tile, yb.shape[1]), yb.dtype), pltpu.SemaphoreType.DMA((2,))],
    )
    return pl.pallas_call(
        functools.partial(_combine_body, n_tok=t, tile=tile, ctx_tiles=ctx_tiles),
        grid_spec=grid_spec,
        out_shape=out_shape,
        compiler_params=_params("arbitrary"),
        name="moe_gated_residual",
    )(dest, h1, gates, yb, mod)


def _moe(h1, x2, idx_t, gate_t, mod, w_gate, w_up, w_down, layer, cond_of_tile, split_rows):
    t, d = h1.shape
    n = 2 * t
    experts = idx_t.reshape(n)
    onehot = (experts[:, None] == jnp.arange(N_EXPERTS, dtype=jnp.int32)[None, :]).astype(BF16)
    blocks = onehot.reshape(n // COMBINE_TILE, COMBINE_TILE, N_EXPERTS)
    tri = jnp.asarray(np.tril(np.ones((COMBINE_TILE, COMBINE_TILE), np.float32)), BF16)
    within = jnp.einsum('ij,bjk->bik', tri, blocks, preferred_element_type=F32)
    block_total = within[:, -1, :]
    block_first = jnp.cumsum(block_total, axis=0) - block_total
    counts = (block_first[-1] + block_total[-1]).astype(jnp.int32)
    before = (within + block_first[:, None, :]).reshape(n, N_EXPERTS) - 1.0
    rank = jnp.sum(before * onehot.astype(F32), axis=1).astype(jnp.int32)
    padded = (counts + MOE_ROWS - 1) // MOE_ROWS * MOE_ROWS
    pad_end = jnp.cumsum(padded).astype(jnp.int32)
    pad_start = pad_end - padded
    dest = (pad_start[experts] + rank).astype(jnp.int32)
    n_blocks = (n + N_EXPERTS * (MOE_ROWS - 1) + MOE_ROWS - 1) // MOE_ROWS
    block_first_row = jnp.arange(n_blocks, dtype=jnp.int32) * MOE_ROWS
    block_expert = jnp.minimum(jnp.sum((pad_end[None, :] <= block_first_row[:, None]).astype(jnp.int32), axis=1),
                               N_EXPERTS - 1)
    n_used = (pad_end[-1:] // MOE_ROWS).astype(jnp.int32)
    xs = _dispatch(x2, dest, pad_start, pad_end, n_blocks * MOE_ROWS)
    ids = jnp.arange(N_EXPERTS, dtype=jnp.int32)
    later_with_rows = (counts[None, :] > 0) & (ids[None, :] > ids[:, None])
    next_expert = jnp.min(jnp.where(later_with_rows, ids[None, :], N_EXPERTS), axis=1).astype(jnp.int32)
    yb = _expert_blocks(xs, block_expert, n_used, next_expert, w_gate, w_up, w_down, layer)
    return _combine(h1, yb, dest, gate_t.T, mod, cond_of_tile, COMBINE_TILE, split_rows)


def kernel(x_prompt, x_sample, cache_na_k, cache_na_v, cache_gqa_k, cache_gqa_v, cache_diff_k, cache_diff_v, state_hgrn, c, c_ctx, w_mod, b_mod, norm1, norm2, w_in, w_out, hg_lb_logits, hg_onorm, na_qn, na_kn, na_rpb, gqa_qn, gqa_kn, df_qn, df_kn, df_lam, df_subln, router_w, router_b, w_gate, w_up, w_down):
    n_ctx, ctx_len, d = x_prompt.shape
    n_lat, lat_len, _ = x_sample.shape
    depth = w_in.shape[0]
    t_ctx = n_ctx * ctx_len
    assert t_ctx % lat_len == 0 and lat_len % GRID_W == 0 and lat_len // GRID_W >= WIN_ROWS
    tm = next(m for m in (1024, 512, 256) if t_ctx % m == 0 and lat_len % m == 0)
    tm2 = min(tm, 512)
    lat_block0 = t_ctx // lat_len

    def cond_tile(tile_rows):
        def cond_of_tile(i):
            return jnp.where(i < t_ctx // tile_rows, 0, 1 + (i - t_ctx // tile_rows) // (lat_len // tile_rows))
        return cond_of_tile

    sm = jax.nn.softmax(hg_lb_logits.astype(F32), axis=0)
    lower = jnp.cumsum(sm, axis=0) - sm[0:1]
    mod_all = _modulation(jnp.concatenate([c_ctx[None, :], c], axis=0), w_mod, b_mod)
    mod_all = mod_all.reshape(depth, 1 + n_lat, 6, d)
    hgrn_consts = _hgrn_constants(HGRN_CHUNK)
    rope_c = _rope_tables(lat_len, HEAD_W)
    rope_d = _rope_tables(lat_len, DF_DQK)
    past = cache_diff_k.shape[4]
    cache_diff_k2 = cache_diff_k.transpose(0, 1, 2, 4, 3, 5).reshape(n_lat, depth, N_HEADS, past, HEAD_W)

    h_ctx, h_lat = x_prompt.reshape(t_ctx, d), x_sample.reshape(n_lat * lat_len, d)
    caches, states = [], None
    for layer in range(depth):
        mod = mod_all[layer]
        lam_init = 0.8 - 0.6 * math.exp(-0.3 * layer)
        proj = _input_projection(h_ctx, h_lat, mod, norm1[layer], w_in[layer].astype(BF16), cond_tile(tm), tm)
        mix_a, states = _hgrn(proj, 0, n_ctx, ctx_len, lower[layer], hg_onorm[layer], hgrn_consts, None, None,
                              states, layer, depth)
        gains = (na_qn[layer], na_kn[layer], gqa_qn[layer], gqa_kn[layer], df_qn[layer], df_kn[layer],
                 df_subln[layer], df_lam[layer])
        mix_b, mix_c, mix_d, *caches = _context_attention(proj, n_ctx, ctx_len, gains, lam_init, layer, depth,
                                                          caches)
        mix_a, _ = _hgrn(proj, lat_block0, n_lat, lat_len, lower[layer], hg_onorm[layer], hgrn_consts,
                         state_hgrn, mix_a, None, layer, depth)
        mix_b = _latent_na(proj, lat_block0, n_lat, lat_len, cache_na_k, cache_na_v, layer,
                           _na_bias(na_rpb[layer]), na_qn[layer], na_kn[layer], mix_b)
        mix_c = _latent_gqa(proj, lat_block0, n_lat, lat_len, cache_gqa_k, cache_gqa_v, layer, rope_c,
                            gqa_qn[layer], gqa_kn[layer], mix_c)
        mix_d = _latent_diff(proj, lat_block0, n_lat, lat_len, cache_diff_k2, cache_diff_v, layer, rope_d,
                             df_qn[layer], df_kn[layer], df_subln[layer], df_lam[layer], lam_init, mix_d)
        h1, x2, idx_t, gate_t = _output_projection((mix_a, mix_b, mix_c, mix_d), w_out[layer].astype(BF16), h_ctx,
                                                   h_lat, mod, norm2[layer], router_w, router_b, cond_tile(tm2), tm2)
        h_ctx, h_lat = _moe(h1, x2, idx_t, gate_t, mod, w_gate, w_up, w_down, layer, cond_tile(COMBINE_TILE), t_ctx)
    y_prompt = h_ctx.reshape(n_ctx, ctx_len, d)
    y_sample = h_lat.reshape(n_lat, lat_len, d)
    return (y_prompt, y_sample, *caches, states)
```

```python
import functools
import math

import numpy as np
import jax
import jax.numpy as jnp
from jax import lax
from jax.experimental import pallas as pl
from jax.experimental.pallas import tpu as pltpu

D_MODEL = 2048
GRID_W = 64
GROUP_W = D_MODEL // 4
N_HEADS = 4
HEAD_W = GROUP_W // N_HEADS
SUBLANES = 8
GQA_KV_HEADS = 2
DF_DQK = HEAD_W // 2
WIN_ROWS = 8
WIN_COLS = 16
N_EXPERTS = 16
N_EXP_GROUPS = 4
EXP_PER_GROUP = N_EXPERTS // N_EXP_GROUPS
EXPERT_FF = D_MODEL // 4
ROPE_THETA = 10000.0
EPS = 1e-6
NEG_INF = -1e30
LOG2E = math.log2(math.e)
IN_WIDTH = 13 * GROUP_W

COL_A_Q, COL_A_FF, COL_A_FB, COL_A_I, COL_A_G = 0, 4, 8, 12, 16
COL_B_Q, COL_B_K, COL_B_V = 20, 24, 28
COL_C_Q, COL_C_K, COL_C_V = 32, 36, 38
COL_D_Q, COL_D_K, COL_D_V = 40, 44, 48

HGRN_CHUNK = 128
HGRN_UNROLL = 4
HGRN_INPUT_VMEM = 24 * 1024 * 1024
CTX_KV_HEADS_PER_STEP = 1
MOE_ROWS = 256
DISPATCH_TILES = (1024, 512, 256)
COMBINE_TILE = 256
VMEM_LIMIT = 48 * 1024 * 1024
VMEM_LIMIT_PROJ = 56 * 1024 * 1024

F32 = jnp.float32
BF16 = jnp.bfloat16
ANY_SPEC = pl.BlockSpec(memory_space=pl.ANY)


def _params(*sem, vmem=VMEM_LIMIT):
    return pltpu.CompilerParams(dimension_semantics=sem, vmem_limit_bytes=vmem)


def _sigmoid(x):
    return 1.0 / (1.0 + jnp.exp(-x))


def _silu(x):
    return x * _sigmoid(x)


def _rms(x, gain, n):
    return x * lax.rsqrt(jnp.sum(x * x, axis=-1, keepdims=True) * (1.0 / n) + EPS) * gain


def _dot(a, b):
    return jnp.dot(a, b, preferred_element_type=F32)


def _dot_nt(a, b):
    return lax.dot_general(a, b, (((1,), (1,)), ((), ())), preferred_element_type=F32)


def _dot_tn(a, b):
    return lax.dot_general(a, b, (((0,), (0,)), ((), ())), preferred_element_type=F32)


def _pack_bf16_pairs(x):
    k = x.shape[1] // 2
    lo = lax.bitcast_convert_type(x[:, :k].astype(BF16).astype(F32), jnp.uint32) >> 16
    hi = lax.bitcast_convert_type(x[:, k:].astype(BF16).astype(F32), jnp.uint32)
    return hi | lo


def _unpack_bf16_pairs(w):
    lo = lax.bitcast_convert_type(w << 16, F32)
    hi = lax.bitcast_convert_type(w & jnp.uint32(0xFFFF0000), F32)
    return jnp.concatenate([lo, hi], axis=1)


def _aligned(x, m):
    return x if isinstance(x, int) else pl.multiple_of(x, m)


def _alias_kwargs(n_inputs, prev, first_out):
    return ([ANY_SPEC] * len(prev), list(prev), {n_inputs + k: first_out + k for k in range(len(prev))})


def _mod_body(cond_ref, w_ref, b_ref, o_ref):
    w = w_ref[...]
    for c in range(cond_ref.shape[0]):
        s = _silu(cond_ref[c])
        o_ref[c:c + 1, :] = jnp.sum(w * s, axis=0, keepdims=True) + b_ref[...]


def _modulation(cond, w_mod, b_mod):
    depth, d, n6 = w_mod.shape
    nc = cond.shape[0]
    tn = 1024
    return pl.pallas_call(
        _mod_body,
        grid=(depth, n6 // tn),
        in_specs=[pl.BlockSpec((nc, d, 1), lambda l, j: (0, 0, 0)),
                  pl.BlockSpec((None, d, tn), lambda l, j: (l, 0, j)),
                  pl.BlockSpec((None, 1, tn), lambda l, j: (l, 0, j))],
        out_specs=pl.BlockSpec((None, nc, tn), lambda l, j: (l, 0, j)),
        out_shape=jax.ShapeDtypeStruct((depth, nc, n6), F32),
        compiler_params=_params("parallel", "parallel"),
        name="adaln_modulation",
    )(cond[:, :, None], w_mod, b_mod[:, None, :])


def _group_specs(tile, d, ctx_tiles):
    return [pl.BlockSpec((tile, d), lambda i, *_: (jnp.minimum(i, ctx_tiles - 1), 0)),
            pl.BlockSpec((tile, d), lambda i, *_: (jnp.maximum(i - ctx_tiles, 0), 0))]


INPROJ_NORM_CHUNKS = 8


def _inproj_body(h0_ref, mod0_ref, hc_ref, hl_ref, mod_ref, n1_ref, w_ref, o_ref, xn_ref, *, ctx_tiles, chunk):
    i = pl.program_id(0)
    j = pl.program_id(1)

    def normalise(h, mod):
        return (_rms(h, n1_ref[...], D_MODEL) * (1.0 + mod[1:2, :]) + mod[0:1, :]).astype(BF16)

    @pl.when((i == 0) & (j == 0))
    def _():
        xn_ref[0] = normalise(h0_ref[...], mod0_ref[...])

    nxt = i + 1
    rows = pl.ds(pl.multiple_of(jnp.minimum(j, INPROJ_NORM_CHUNKS - 1) * chunk, chunk), chunk)
    xn_ref[nxt % 2, rows, :] = normalise(jnp.where(nxt < ctx_tiles, hc_ref[...], hl_ref[...]), mod_ref[...])
    o_ref[...] = _dot(xn_ref[i % 2], w_ref[...].astype(BF16))


def _input_projection(h_ctx, h_lat, mod, norm1, w_in, layer, cond_of_tile, tm):
    d = h_ctx.shape[1]
    t = h_ctx.shape[0] + h_lat.shape[0]
    n_tiles, ctx_tiles, lat_tiles = t // tm, h_ctx.shape[0] // tm, h_lat.shape[0] // tm
    n = w_in.shape[2]
    tn = 512
    pieces = INPROJ_NORM_CHUNKS
    chunk = tm // pieces
    assert n // tn >= pieces and ctx_tiles >= 1

    def piece(j):
        return jnp.minimum(j, pieces - 1)

    def nxt(i):
        return jnp.minimum(i + 1, n_tiles - 1)

    ctx_chunk = lambda i, j: (jnp.where(nxt(i) < ctx_tiles, nxt(i) * pieces + piece(j), ctx_tiles * pieces - 1), 0)
    lat_chunk = lambda i, j: (jnp.where(nxt(i) >= ctx_tiles, (nxt(i) - ctx_tiles) * pieces + piece(j), 0), 0)
    return pl.pallas_call(
        functools.partial(_inproj_body, ctx_tiles=ctx_tiles, chunk=chunk),
        grid=(n_tiles, n // tn),
        in_specs=[pl.BlockSpec((tm, d), lambda i, j: (0, 0)),
                  pl.BlockSpec((None, 6, d), lambda i, j: (cond_of_tile(0), 0, 0)),
                  pl.BlockSpec((chunk, d), ctx_chunk),
                  pl.BlockSpec((chunk, d), lat_chunk),
                  pl.BlockSpec((None, 6, d), lambda i, j: (cond_of_tile(nxt(i)), 0, 0)),
                  pl.BlockSpec((1, d), lambda i, j: (0, 0)),
                  pl.BlockSpec((None, d, tn), lambda i, j: (layer, 0, j))],
        out_specs=pl.BlockSpec((tm, tn), lambda i, j: (i, j)),
        out_shape=jax.ShapeDtypeStruct((t, n), F32),
        scratch_shapes=[pltpu.VMEM((2, tm, d), BF16)],
        compiler_params=_params("arbitrary", "arbitrary"),
        name="norm_modulate_in_proj",
    )(h_ctx, mod, h_ctx, h_lat, mod, norm1[None, :], w_in)


def _hgrn_constants(c):
    nl = int(math.log2(c))
    idx = np.arange(c)
    e = np.zeros((nl + 2, c, c), np.float32)
    m = np.zeros((nl + 1, c, c), np.float32)
    e[0] = idx[None, :] <= idx[:, None]
    e[1] = idx[None, :] > idx[:, None]
    m[0] = np.eye(c)
    for li in range(nl):
        s = c >> (li + 1)
        parent = idx // (2 * s)
        right = (idx % (2 * s)) >= s
        ref = parent * 2 * s + s - 1
        for i in range(c):
            if right[i]:
                e[2 + li, i, ref[i] + 1:i + 1] = 1.0
            else:
                e[2 + li, i, i + 1:ref[i] + 1] = 1.0
        m[1 + li] = right[:, None] & ~right[None, :] & (parent[:, None] == parent[None, :])
    keep = [0, 1] + [2 + li for li in range(nl) if (c >> (li + 1)) < SUBLANES]
    e = e[keep]
    e2 = np.stack([e, e[:, ::-1, ::-1]]).reshape(2, len(keep) * c, c)
    m2 = np.stack([m, m[:, ::-1, ::-1]])
    return jnp.asarray(e2, BF16), jnp.asarray(m2, F32)


def _hgrn_body(*refs, seq, chunk, unroll, heads, has_s0, emit_state, n_alias):
    q_ref, ff_ref, fb_ref, i_ref, g_ref, lb_ref, on_ref, e_ref, m_ref = refs[:9]
    pos = 9
    s0_ref = None
    if has_s0:
        s0_ref = refs[pos]
        pos += 1
    pos += n_alias
    o_ref = refs[pos]
    pos += 1
    if emit_state:
        st_ref = refs[pos]
        pos += 1
    of_ref, ob_ref = refs[pos], refs[pos + 1]
    c = chunk
    n_chunks = seq // c
    assert seq % c == 0 and n_chunks % unroll == 0
    n_levels = m_ref.shape[1] - 1
    gate_refs = (ff_ref, fb_ref)
    out_refs = (of_ref, ob_ref)

    def chunk_step(c0, d, hh, st):
        rows = pl.ds(c0, c)
        lanes = slice(hh * HEAD_W, (hh + 1) * HEAD_W)
        lb = lb_ref[d:d + 1, lanes]
        f = lb + (1.0 - lb) * _sigmoid(gate_refs[d][rows, lanes])
        g = jnp.log2(f)
        k = 1.0 - f
        q = _silu(q_ref[rows, lanes])
        v = i_ref[rows, lanes].astype(BF16)
        g_hi = g.astype(BF16)
        g_lo = (g - g_hi.astype(F32)).astype(BF16)
        g2 = _dot(e_ref[d], jnp.concatenate([g_hi, g_lo], axis=1))
        gsum = g2[:, 0:HEAD_W] + g2[:, HEAD_W:2 * HEAD_W]
        cum = gsum[0:c]
        x_cum = jnp.exp2(cum)
        x_tail = jnp.exp2(gsum[c:2 * c])
        row = lax.broadcasted_iota(jnp.int32, (c, HEAD_W), 0)
        s = m_ref[d, 0] * _dot_nt(q.astype(BF16), k.astype(BF16))
        n_matmul_levels = 0
        for lv in range(n_levels):
            half = c >> (lv + 1)
            if half >= SUBLANES:
                ref_row = half - 1 if d == 0 else half
                ref = jnp.concatenate(
                    [jnp.broadcast_to(cum[b * 2 * half + ref_row:b * 2 * half + ref_row + 1, :], (2 * half, HEAD_W))
                     for b in range(c // (2 * half))], axis=0)
                near = ((row % (2 * half)) >= half) == (d == 0)
                x_l = jnp.exp2(jnp.where(near, cum - ref, ref - cum))
            else:
                x_l = jnp.exp2(gsum[(2 + n_matmul_levels) * c:(3 + n_matmul_levels) * c])
                n_matmul_levels += 1
            s = s + m_ref[d, 1 + lv] * _dot_nt((q * x_l).astype(BF16), (k * x_l).astype(BF16))
        o = _dot_nt((q * x_cum).astype(BF16), st.astype(BF16)) + _dot(s.astype(BF16), v)
        out_refs[d][rows, lanes] = o
        total = x_cum[c - 1:c, :] if d == 0 else x_cum[0:1, :]
        return st * total + _dot_tn(v, (k * x_tail).astype(BF16))

    if has_s0:
        states0 = tuple(s0_ref[d, hh].T for hh in range(heads) for d in range(2))
    else:
        states0 = tuple(jnp.zeros((HEAD_W, HEAD_W), F32) for _ in range(2 * heads))

    def loop(t, states):
        states = list(states)
        for u in range(unroll):
            j = t * unroll + u
            for hh in range(heads):
                states[2 * hh] = chunk_step(_aligned(j * c, c), 0, hh, states[2 * hh])
                states[2 * hh + 1] = chunk_step(_aligned((n_chunks - 1 - j) * c, c), 1, hh, states[2 * hh + 1])
        return tuple(states)

    if n_chunks == unroll:
        states = loop(0, states0)
    else:
        states = lax.fori_loop(0, n_chunks // unroll, loop, states0)
    for hh in range(heads):
        lanes = slice(hh * HEAD_W, (hh + 1) * HEAD_W)
        o = of_ref[:, lanes] + ob_ref[:, lanes]
        o_ref[:, lanes] = (_rms(o, on_ref[...], HEAD_W) * _silu(g_ref[:, lanes])).astype(o_ref.dtype)
        if emit_state:
            st_ref[0, hh] = states[2 * hh].T
            st_ref[1, hh] = states[2 * hh + 1].T


def _hgrn(proj, row_block0, n_seq, seq, lower, onorm, consts, s0, mixed_prev, state_prev, layer, depth):
    e_mat, masks = consts
    latent = s0 is not None
    hps = next(n for n in (4, 2, 1) if 5 * 2 * seq * n * HEAD_W * 4 <= HGRN_INPUT_VMEM)
    width = hps * HEAD_W

    def col(cb):
        return pl.BlockSpec((seq, width), lambda b, h, cb=cb: (row_block0 + b, cb // hps + h))

    state_spec = pl.BlockSpec((None, None, 2, hps, HEAD_W, HEAD_W), lambda b, h: (b, layer, 0, h, 0, 0))
    in_specs = [col(COL_A_Q), col(COL_A_FF), col(COL_A_FB), col(COL_A_I), col(COL_A_G),
                pl.BlockSpec((2, width), lambda b, h: (0, h)),
                pl.BlockSpec((1, HEAD_W), lambda b, h: (0, 0)),
                pl.BlockSpec(e_mat.shape, lambda b, h: (0, 0, 0)),
                pl.BlockSpec(masks.shape, lambda b, h: (0, 0, 0, 0))]
    args = [proj, proj, proj, proj, proj, lower, onorm[None, :], e_mat, masks]
    if latent:
        in_specs.append(state_spec)
        args.append(s0)
        prev = [mixed_prev]
    else:
        prev = [] if state_prev is None else [state_prev]
    alias_specs, alias_args, aliases = _alias_kwargs(len(args), prev, 0 if latent else 1)
    out_specs = [pl.BlockSpec((seq, width), lambda b, h: (row_block0 + b, h))]
    out_shape = [jax.ShapeDtypeStruct((proj.shape[0], GROUP_W), BF16)]
    if not latent:
        out_specs.append(state_spec)
        out_shape.append(jax.ShapeDtypeStruct((n_seq, depth, 2, N_HEADS, HEAD_W, HEAD_W), F32))
    res = pl.pallas_call(
        functools.partial(_hgrn_body, seq=seq, chunk=HGRN_CHUNK, unroll=min(HGRN_UNROLL, seq // HGRN_CHUNK),
                          heads=hps, has_s0=latent,
                          emit_state=not latent, n_alias=len(prev)),
        grid=(n_seq, N_HEADS // hps),
        in_specs=in_specs + alias_specs, out_specs=out_specs, out_shape=out_shape,
        input_output_aliases=aliases,
        scratch_shapes=[pltpu.VMEM((seq, width), F32), pltpu.VMEM((seq, width), F32)],
        compiler_params=_params("parallel", "parallel"),
        name="hgrn2_latent" if latent else "hgrn2_context",
    )(*args, *alias_args)
    return (res[0], None) if latent else res


def _with_ones(v):
    return jnp.concatenate([v, jnp.ones_like(v)], axis=1)


def _softmax_pv(scores, values1):
    mx = functools.reduce(jnp.maximum, [jnp.max(s, axis=-1, keepdims=True) for s in scores])
    acc = functools.reduce(lambda a, b: a + b,
                           [_dot(jnp.exp2(s - mx).astype(BF16), v) for s, v in zip(scores, values1)])
    return acc[:, 0:HEAD_W] / acc[:, HEAD_W:HEAD_W + 1]


def _diff_pv(s0, s1, lam, values1):
    return _softmax_pv([s0], [values1]) - lam * _softmax_pv([s1], [values1])


def _lane_lt(shape, n):
    return lax.broadcasted_iota(jnp.int32, shape, len(shape) - 1) < n


def _rms_head(x, gain):
    return _rms(x, gain, HEAD_W)


def _rms_halves(x, gain2):
    lo = _lane_lt(x.shape, DF_DQK)
    sq = x * x
    ss_lo = jnp.sum(jnp.where(lo, sq, 0.0), axis=-1, keepdims=True)
    ss_hi = jnp.sum(sq, axis=-1, keepdims=True) - ss_lo
    inv = jnp.where(lo, lax.rsqrt(ss_lo * (1.0 / DF_DQK) + EPS), lax.rsqrt(ss_hi * (1.0 / DF_DQK) + EPS))
    return x * inv * gain2


def _lambda(lam_ref, lam_init):
    l = lam_ref[...]
    return (jnp.exp(jnp.sum(l[0:1] * l[1:2], axis=-1, keepdims=True))
            - jnp.exp(jnp.sum(l[2:3] * l[3:4], axis=-1, keepdims=True)) + lam_init)


N_CTX_ATTN_INPUTS = 17


def _ctx_attn_body(*refs, lam_init, kv_heads):
    (bq_ref, bk_ref, bv_ref, cq_ref, ck_ref, cv_ref, dq_ref, dk_ref, dv_ref,
     naq_ref, nak_ref, gq_ref, gk_ref, dfq_ref, dfk_ref, sub_ref, lam_ref) = refs[:N_CTX_ATTN_INPUTS]
    ob_ref, oc_ref, od_ref, kb_ref, vb_ref, kc_ref, vc_ref, kd_ref, vd_ref = refs[-9:]
    group = N_HEADS // GQA_KV_HEADS
    scale = LOG2E * HEAD_W ** -0.5
    lam = _lambda(lam_ref, lam_init)
    for n in range(kv_heads):
        kv_lanes = slice(n * HEAD_W, (n + 1) * HEAD_W)
        kc = _rms_head(ck_ref[:, kv_lanes], gk_ref[...])
        vc = cv_ref[:, kv_lanes]
        kc_ref[n] = kc
        vc_ref[n] = vc
        kc16 = kc.astype(BF16)
        vc1 = _with_ones(vc.astype(BF16))
        for g in range(group):
            h = n * group + g
            lanes = slice(h * HEAD_W, (h + 1) * HEAD_W)
            qc = (_rms_head(cq_ref[:, lanes], gq_ref[...]) * scale).astype(BF16)
            oc_ref[:, lanes] = _softmax_pv([_dot_nt(qc, kc16)], [vc1]).astype(oc_ref.dtype)
            kb = _rms_head(bk_ref[:, lanes], nak_ref[...])
            vb = bv_ref[:, lanes]
            kb_ref[h] = kb
            vb_ref[h] = vb
            qb = (_rms_head(bq_ref[:, lanes], naq_ref[...]) * scale).astype(BF16)
            ob_ref[:, lanes] = _softmax_pv([_dot_nt(qb, kb.astype(BF16))],
                                           [_with_ones(vb.astype(BF16))]).astype(ob_ref.dtype)
            kd = _rms_halves(dk_ref[:, lanes], dfk_ref[...])
            vd = dv_ref[:, lanes]
            kd_ref[h, 0] = kd[:, 0:DF_DQK]
            kd_ref[h, 1] = kd[:, DF_DQK:2 * DF_DQK]
            vd_ref[h] = vd
            qd = (_rms_halves(dq_ref[:, lanes], dfq_ref[...]) * (LOG2E * DF_DQK ** -0.5)).astype(BF16)
            lo = _lane_lt(kd.shape, DF_DQK)
            s0 = _dot_nt(qd, jnp.where(lo, kd, 0.0).astype(BF16))
            s1 = _dot_nt(qd, jnp.where(lo, 0.0, kd).astype(BF16))
            od = _diff_pv(s0, s1, lam, _with_ones(vd.astype(BF16)))
            od_ref[:, lanes] = (_rms_head(od, sub_ref[...]) * (1.0 - lam_init)).astype(od_ref.dtype)


def _context_attention(proj, n_seq, seq, gains, lam_init, layer, depth, caches_prev):
    na_qn, na_kn, gqa_qn, gqa_kn, df_qn, df_kn, df_subln, df_lam = gains
    group = N_HEADS // GQA_KV_HEADS
    kvs = CTX_KV_HEADS_PER_STEP
    n_heads = kvs * group
    width = n_heads * HEAD_W

    def heads(cb):
        return pl.BlockSpec((seq, width), lambda b, n, cb=cb: (b, cb // n_heads + n))

    def kv_head(cb):
        return pl.BlockSpec((seq, kvs * HEAD_W), lambda b, n, cb=cb: (b, cb // kvs + n))

    vec = pl.BlockSpec((1, HEAD_W), lambda b, n: (0, 0))
    cache_heads = pl.BlockSpec((None, None, n_heads, seq, HEAD_W), lambda b, n: (b, layer, n, 0, 0))
    cache_kv = pl.BlockSpec((None, None, kvs, seq, HEAD_W), lambda b, n: (b, layer, n, 0, 0))
    mixed = pl.BlockSpec((seq, width), lambda b, n: (b, n))
    mixed_shape = jax.ShapeDtypeStruct((proj.shape[0], GROUP_W), BF16)
    cache4 = jax.ShapeDtypeStruct((n_seq, depth, N_HEADS, seq, HEAD_W), F32)
    cache2 = jax.ShapeDtypeStruct((n_seq, depth, GQA_KV_HEADS, seq, HEAD_W), F32)
    cache_dk = jax.ShapeDtypeStruct((n_seq, depth, N_HEADS, 2, seq, DF_DQK), F32)
    args = [proj] * 9 + [na_qn[None, :], na_kn[None, :], gqa_qn[None, :], gqa_kn[None, :],
                         jnp.tile(df_qn, 2)[None, :], jnp.tile(df_kn, 2)[None, :], df_subln[None, :], df_lam]
    assert len(args) == N_CTX_ATTN_INPUTS
    alias_specs, alias_args, aliases = _alias_kwargs(len(args), caches_prev, 3)
    return pl.pallas_call(
        functools.partial(_ctx_attn_body, lam_init=lam_init, kv_heads=kvs),
        grid=(n_seq, GQA_KV_HEADS // kvs),
        in_specs=[heads(COL_B_Q), heads(COL_B_K), heads(COL_B_V),
                  heads(COL_C_Q), kv_head(COL_C_K), kv_head(COL_C_V),
                  heads(COL_D_Q), heads(COL_D_K), heads(COL_D_V),
                  vec, vec, vec, vec, vec, vec, vec,
                  pl.BlockSpec((4, DF_DQK), lambda b, n: (0, 0))] + alias_specs,
        out_specs=[mixed, mixed, mixed, cache_heads, cache_heads, cache_kv, cache_kv,
                   pl.BlockSpec((None, None, n_heads, 2, seq, DF_DQK), lambda b, n: (b, layer, n, 0, 0, 0)),
                   cache_heads],
        out_shape=[mixed_shape, mixed_shape, mixed_shape, cache4, cache4, cache2, cache2, cache_dk, cache4],
        input_output_aliases=aliases,
        compiler_params=_params("parallel", "parallel"),
        name="context_attention",
    )(*args, *alias_args)


def _rope_tables(n_tokens, rot_dim):
    t = np.arange(n_tokens)
    row = (t // GRID_W).astype(np.float32)
    col = (t % GRID_W).astype(np.float32)
    n_freq = rot_dim // 4
    inv = (np.float32(ROPE_THETA) ** (-np.arange(n_freq, dtype=np.float32) / np.float32(n_freq))).astype(np.float32)
    ang = np.concatenate([row[:, None] * inv, col[:, None] * inv], axis=-1).astype(np.float32)
    cos, sin, zero = np.cos(ang), np.sin(ang), np.zeros_like(ang)
    reps = HEAD_W // rot_dim
    a = np.tile(np.concatenate([cos, cos], axis=-1), (1, reps))
    b = np.tile(np.concatenate([-sin, zero], axis=-1), (1, reps))
    c = np.tile(np.concatenate([zero, sin], axis=-1), (1, reps))
    return jnp.asarray(np.stack([a, b, c]), F32)


def _rope(x, tab_ref, half):
    return (x * tab_ref[0] + pltpu.roll(x, HEAD_W - half, 1) * tab_ref[1]
            + pltpu.roll(x, half, 1) * tab_ref[2])


def _na_body(q_ref, k_ref, v_ref, ck_ref, cv_ref, bias_ref, qn_ref, kn_ref, prev_ref, o_ref,
             qs_ref, ks_ref, vs_ref, *, seq):
    del prev_ref
    rows = seq // GRID_W
    n_win = WIN_ROWS * GRID_W
    qs_ref[...] = (_rms_head(q_ref[...], qn_ref[...]) * (LOG2E * HEAD_W ** -0.5)).astype(BF16)
    ks_ref[...] = _rms_head(k_ref[...], kn_ref[...]).astype(BF16)
    vs_ref[...] = _with_ones(v_ref[...].astype(BF16))
    ck = ck_ref[...].astype(BF16)
    cv = _with_ones(cv_ref[...].astype(BF16))

    def row_step(r, carry):
        start = jnp.clip(r - WIN_ROWS // 2, 0, rows - WIN_ROWS)
        win = pl.ds(pl.multiple_of(start * GRID_W, GRID_W), n_win)
        qrows = pl.ds(pl.multiple_of(r * GRID_W, GRID_W), GRID_W)
        q = qs_ref[qrows, :]
        s_win = _dot_nt(q, ks_ref[win, :]) + bias_ref[start - r + (WIN_ROWS - 1)]
        s_ctx = _dot_nt(q, ck)
        o_ref[qrows, :] = _softmax_pv([s_win, s_ctx], [vs_ref[win, :], cv]).astype(o_ref.dtype)
        return carry

    lax.fori_loop(0, rows, row_step, 0, unroll=min(rows, 32))


def _na_bias(rpb):
    col = np.arange(GRID_W)
    col_start = np.clip(col - WIN_COLS // 2, 0, GRID_W - WIN_COLS)
    col_ok = (col[None, :] >= col_start[:, None]) & (col[None, :] < col_start[:, None] + WIN_COLS)
    dc = np.clip(col[None, :] - col[:, None] + WIN_COLS - 1, 0, 2 * WIN_COLS - 2).reshape(-1)
    onehot = (np.arange(2 * WIN_COLS - 1)[:, None] == dc[None, :]).astype(np.float32)
    per_dr = jnp.einsum('hdc,cn->hdn', rpb.astype(F32), jnp.asarray(onehot), precision=lax.Precision.HIGHEST)
    per_dr = jnp.where(col_ok[None, None], LOG2E * per_dr.reshape(rpb.shape[0], -1, GRID_W, GRID_W), NEG_INF)
    wins = jnp.stack([per_dr[:, o:o + WIN_ROWS] for o in range(WIN_ROWS)], axis=1)
    return wins.transpose(0, 1, 3, 2, 4).reshape(rpb.shape[0], WIN_ROWS, GRID_W, WIN_ROWS * GRID_W)


def _latent_na(proj, row_block0, n_seq, seq, cache_k, cache_v, layer, bias, na_qn, na_kn, mixed_prev):
    past = cache_k.shape[3]

    def col(cb):
        return pl.BlockSpec((seq, HEAD_W), lambda b, h, cb=cb: (row_block0 + b, cb + h))

    cache = pl.BlockSpec((None, None, None, past, HEAD_W), lambda b, h: (b, layer, h, 0, 0))
    vec = pl.BlockSpec((1, HEAD_W), lambda b, h: (0, 0))
    return pl.pallas_call(
        functools.partial(_na_body, seq=seq),
        grid=(n_seq, N_HEADS),
        in_specs=[col(COL_B_Q), col(COL_B_K), col(COL_B_V), cache, cache,
                  pl.BlockSpec((None, WIN_ROWS, GRID_W, WIN_ROWS * GRID_W), lambda b, h: (h, 0, 0, 0)),
                  vec, vec, ANY_SPEC],
        out_specs=pl.BlockSpec((seq, HEAD_W), lambda b, h: (row_block0 + b, h)),
        out_shape=jax.ShapeDtypeStruct(mixed_prev.shape, BF16),
        input_output_aliases={8: 0},
        scratch_shapes=[pltpu.VMEM((seq, HEAD_W), BF16), pltpu.VMEM((seq, HEAD_W), BF16),
                        pltpu.VMEM((seq, 2 * HEAD_W), BF16)],
        compiler_params=_params("parallel", "parallel"),
        name="latent_neighbourhood_attention",
    )(proj, proj, proj, cache_k, cache_v, bias, na_qn[None, :], na_kn[None, :], mixed_prev)


def _gqa_body(q_ref, k_ref, v_ref, ck_ref, cv_ref, rope_ref, qn_ref, kn_ref, prev_ref, o_ref,
              qs_ref, ks_ref, vs_ref, *, seq, tq):
    del prev_ref
    group = N_HEADS // GQA_KV_HEADS
    half = HEAD_W // 2
    ks_ref[0:seq, :] = _rope(_rms_head(k_ref[...], kn_ref[...]), rope_ref, half).astype(BF16)
    ks_ref[seq:, :] = ck_ref[...].astype(BF16)
    vs_ref[0:seq, :] = _with_ones(v_ref[...].astype(BF16))
    vs_ref[seq:, :] = _with_ones(cv_ref[...].astype(BF16))
    for g in range(group):
        q = _rms_head(q_ref[:, g * HEAD_W:(g + 1) * HEAD_W], qn_ref[...]) * (LOG2E * HEAD_W ** -0.5)
        qs_ref[g] = _rope(q, rope_ref, half).astype(BF16)
    kk = ks_ref[...]
    vv = vs_ref[...]
    for g in range(group):
        def q_step(i, carry, g=g):
            qrows = pl.ds(pl.multiple_of(i * tq, tq), tq)
            o = _softmax_pv([_dot_nt(qs_ref[g, qrows, :], kk)], [vv])
            o_ref[qrows, g * HEAD_W:(g + 1) * HEAD_W] = o.astype(o_ref.dtype)
            return carry

        lax.fori_loop(0, seq // tq, q_step, 0, unroll=min(seq // tq, 8))


def _latent_gqa(proj, row_block0, n_seq, seq, cache_k, cache_v, layer, rope, gqa_qn, gqa_kn, mixed_prev):
    past = cache_k.shape[3]
    group = N_HEADS // GQA_KV_HEADS
    tq = 256
    cache = pl.BlockSpec((None, None, None, past, HEAD_W), lambda b, n: (b, layer, n, 0, 0))
    vec = pl.BlockSpec((1, HEAD_W), lambda b, n: (0, 0))
    return pl.pallas_call(
        functools.partial(_gqa_body, seq=seq, tq=tq),
        grid=(n_seq, GQA_KV_HEADS),
        in_specs=[pl.BlockSpec((seq, group * HEAD_W), lambda b, n: (row_block0 + b, COL_C_Q // group + n)),
                  pl.BlockSpec((seq, HEAD_W), lambda b, n: (row_block0 + b, COL_C_K + n)),
                  pl.BlockSpec((seq, HEAD_W), lambda b, n: (row_block0 + b, COL_C_V + n)),
                  cache, cache,
                  pl.BlockSpec((3, seq, HEAD_W), lambda b, n: (0, 0, 0)),
                  vec, vec, ANY_SPEC],
        out_specs=pl.BlockSpec((seq, group * HEAD_W), lambda b, n: (row_block0 + b, n)),
        out_shape=jax.ShapeDtypeStruct(mixed_prev.shape, BF16),
        input_output_aliases={8: 0},
        scratch_shapes=[pltpu.VMEM((group, seq, HEAD_W), BF16),
                        pltpu.VMEM((seq + past, HEAD_W), BF16),
                        pltpu.VMEM((seq + past, 2 * HEAD_W), BF16)],
        compiler_params=_params("parallel", "parallel"),
        name="latent_gqa_attention",
    )(proj, proj, proj, cache_k, cache_v, rope, gqa_qn[None, :], gqa_kn[None, :], mixed_prev)


def _diff_body(q_ref, k_ref, v_ref, ck_ref, cv_ref, rope_ref, qn_ref, kn_ref, sub_ref, lam_ref, prev_ref, o_ref,
               qs_ref, k0_ref, k1_ref, vs_ref, *, seq, tq, lam_init):
    del prev_ref
    half = DF_DQK // 2
    k = _rope(_rms_halves(k_ref[...], kn_ref[...]), rope_ref, half)
    lo = _lane_lt(k.shape, DF_DQK)
    k0_ref[0:seq, :] = jnp.where(lo, k, 0.0).astype(BF16)
    k1_ref[0:seq, :] = jnp.where(lo, 0.0, k).astype(BF16)
    ck = ck_ref[...]
    lo_c = _lane_lt(ck.shape, DF_DQK)
    k0_ref[seq:, :] = jnp.where(lo_c, ck, 0.0).astype(BF16)
    k1_ref[seq:, :] = jnp.where(lo_c, 0.0, ck).astype(BF16)
    vs_ref[0:seq, :] = _with_ones(v_ref[...].astype(BF16))
    vs_ref[seq:, :] = _with_ones(cv_ref[...].astype(BF16))
    q = _rms_halves(q_ref[...], qn_ref[...]) * (LOG2E * DF_DQK ** -0.5)
    qs_ref[...] = _rope(q, rope_ref, half).astype(BF16)
    lam = _lambda(lam_ref, lam_init)
    k0 = k0_ref[...]
    k1 = k1_ref[...]
    vv = vs_ref[...]

    def q_step(i, carry):
        qrows = pl.ds(pl.multiple_of(i * tq, tq), tq)
        qb = qs_ref[qrows, :]
        o = _diff_pv(_dot_nt(qb, k0), _dot_nt(qb, k1), lam, vv)
        o_ref[qrows, :] = (_rms_head(o, sub_ref[...]) * (1.0 - lam_init)).astype(o_ref.dtype)
        return carry

    lax.fori_loop(0, seq // tq, q_step, 0, unroll=min(seq // tq, 4))


def _latent_diff(proj, row_block0, n_seq, seq, cache_k2, cache_v, layer, rope, df_qn, df_kn, df_subln, df_lam,
                 lam_init, mixed_prev):
    past = cache_k2.shape[3]
    tq = 256

    def col(cb):
        return pl.BlockSpec((seq, HEAD_W), lambda b, h, cb=cb: (row_block0 + b, cb + h))

    cache = pl.BlockSpec((None, None, None, past, HEAD_W), lambda b, h: (b, layer, h, 0, 0))
    vec = pl.BlockSpec((1, HEAD_W), lambda b, h: (0, 0))
    kv_scratch = pltpu.VMEM((seq + past, HEAD_W), BF16)
    return pl.pallas_call(
        functools.partial(_diff_body, seq=seq, tq=tq, lam_init=lam_init),
        grid=(n_seq, N_HEADS),
        in_specs=[col(COL_D_Q), col(COL_D_K), col(COL_D_V), cache, cache,
                  pl.BlockSpec((3, seq, HEAD_W), lambda b, h: (0, 0, 0)),
                  vec, vec, vec, pl.BlockSpec((4, DF_DQK), lambda b, h: (0, 0)), ANY_SPEC],
        out_specs=pl.BlockSpec((seq, HEAD_W), lambda b, h: (row_block0 + b, h)),
        out_shape=jax.ShapeDtypeStruct(mixed_prev.shape, BF16),
        input_output_aliases={10: 0},
        scratch_shapes=[pltpu.VMEM((seq, HEAD_W), BF16), kv_scratch, kv_scratch,
                        pltpu.VMEM((seq + past, 2 * HEAD_W), BF16)],
        compiler_params=_params("parallel", "parallel"),
        name="latent_diff_attention",
    )(proj, proj, proj, cache_k2, cache_v, rope, jnp.tile(df_qn, 2)[None, :], jnp.tile(df_kn, 2)[None, :],
      df_subln[None, :], df_lam, mixed_prev)


def _first_max(vals):
    best = vals[0]
    idx = jnp.zeros(best.shape, jnp.int32)
    for i in range(1, len(vals)):
        better = vals[i] > best
        best = jnp.where(better, vals[i], best)
        idx = jnp.where(better, i, idx)
    return best, idx


def _pick(vals, idx):
    out = vals[0]
    for i in range(1, len(vals)):
        out = jnp.where(idx == i, vals[i], out)
    return out


def _outproj_body(ma_ref, mb_ref, mc_ref, md_ref, w_ref, hc_ref, hl_ref, mod_ref, n2_ref, rw_ref, rb_ref,
                  h1_ref, x2_ref, idx_ref, gate_ref, mixed_ref, *, ctx_tiles):
    for g, m_ref in enumerate((ma_ref, mb_ref, mc_ref, md_ref)):
        mixed_ref[:, g * GROUP_W:(g + 1) * GROUP_W] = m_ref[...]
    y = mod_ref[2:3, :] * _dot(mixed_ref[...], w_ref[...])

    def residual(h_ref):
        h1_ref[...] = h_ref[...] + y

    is_ctx = pl.program_id(0) < ctx_tiles
    pl.when(is_ctx)(lambda: residual(hc_ref))
    pl.when(jnp.logical_not(is_ctx))(lambda: residual(hl_ref))
    h1 = h1_ref[...]
    x2 = _rms(h1, n2_ref[...], D_MODEL) * (1.0 + mod_ref[4:5, :]) + mod_ref[3:4, :]
    x2_ref[...] = _pack_bf16_pairs(x2)
    x_hi = x2.astype(BF16)
    x_lo = (x2 - x_hi.astype(F32)).astype(BF16)
    acc = _dot(x_hi, rw_ref[...])
    logits_tok = acc[:, 0:HEAD_W] + acc[:, HEAD_W:2 * HEAD_W] + _dot(x_lo, rw_ref[:, 0:HEAD_W])
    logits = logits_tok.T[0:N_EXPERTS, :]
    aff_all = _sigmoid(logits)
    sel_all = aff_all + rb_ref[...]
    aff = [aff_all[e:e + 1, :] for e in range(N_EXPERTS)]
    sel = [sel_all[e:e + 1, :] for e in range(N_EXPERTS)]
    neg = jnp.full(sel[0].shape, -jnp.inf, F32)
    scores = []
    for g in range(N_EXP_GROUPS):
        grp = sel[g * EXP_PER_GROUP:(g + 1) * EXP_PER_GROUP]
        m1, i1 = _first_max(grp)
        m2, _ = _first_max([jnp.where(i1 == j, neg, grp[j]) for j in range(EXP_PER_GROUP)])
        scores.append(m1 + m2)
    _, g_best = _first_max(scores)
    in_sel = [_pick([sel[g * EXP_PER_GROUP + j] for g in range(N_EXP_GROUPS)], g_best)
              for j in range(EXP_PER_GROUP)]
    in_aff = [_pick([aff[g * EXP_PER_GROUP + j] for g in range(N_EXP_GROUPS)], g_best)
              for j in range(EXP_PER_GROUP)]
    _, l1 = _first_max(in_sel)
    _, l2 = _first_max([jnp.where(l1 == j, neg, in_sel[j]) for j in range(EXP_PER_GROUP)])
    w1 = _pick(in_aff, l1)
    w2 = _pick(in_aff, l2)
    idx_ref[0:1, :] = g_best * EXP_PER_GROUP + l1
    idx_ref[1:2, :] = g_best * EXP_PER_GROUP + l2
    gate_ref[0:1, :] = w1 / (w1 + w2)
    gate_ref[1:2, :] = w2 / (w1 + w2)


def _output_projection(mixed4, w_out_bf16, h_ctx, h_lat, mod, norm2, router_w, router_b, cond_of_tile, tm):
    d = h_ctx.shape[1]
    t = h_ctx.shape[0] + h_lat.shape[0]
    ctx_tiles = h_ctx.shape[0] // tm
    slab = pl.BlockSpec((tm, GROUP_W), lambda i: (i, 0))
    rw_hi = router_w.astype(BF16)
    rw_lo = (router_w - rw_hi.astype(F32)).astype(BF16)
    pad = ((0, 0), (0, HEAD_W - N_EXPERTS))
    router_split = jnp.concatenate([jnp.pad(rw_hi, pad), jnp.pad(rw_lo, pad)], axis=1)
    return pl.pallas_call(
        functools.partial(_outproj_body, ctx_tiles=ctx_tiles),
        grid=(t // tm,),
        in_specs=[slab, slab, slab, slab,
                  pl.BlockSpec((d, d), lambda i: (0, 0))] + _group_specs(tm, d, ctx_tiles) + [
                  pl.BlockSpec((None, 6, d), lambda i: (cond_of_tile(i), 0, 0)),
                  pl.BlockSpec((1, d), lambda i: (0, 0)),
                  pl.BlockSpec((d, 2 * HEAD_W), lambda i: (0, 0)),
                  pl.BlockSpec((N_EXPERTS, 1), lambda i: (0, 0))],
        out_specs=[pl.BlockSpec((tm, d), lambda i: (i, 0)),
                   pl.BlockSpec((tm, d // 2), lambda i: (i, 0)),
                   pl.BlockSpec((2, tm), lambda i: (0, i)),
                   pl.BlockSpec((2, tm), lambda i: (0, i))],
        out_shape=[jax.ShapeDtypeStruct((t, d), F32), jax.ShapeDtypeStruct((t, d // 2), jnp.uint32),
                   jax.ShapeDtypeStruct((2, t), jnp.int32), jax.ShapeDtypeStruct((2, t), F32)],
        scratch_shapes=[pltpu.VMEM((tm, d), BF16)],
        compiler_params=_params("parallel", vmem=VMEM_LIMIT_PROJ),
        name="out_proj_residual_router",
    )(*mixed4, w_out_bf16, h_ctx, h_lat, mod, norm2[None, :], router_split, router_b[:, None])


def _vmem_row(ref, base, u):
    return ref.at[pl.ds(base, SUBLANES), :].at[pl.ds(u, 1), :]


def _for_row_groups(n_rows, fn):
    def body(g, carry):
        base = pl.multiple_of(g * SUBLANES, SUBLANES)
        for u in range(SUBLANES):
            fn(base, u)
        return carry

    lax.fori_loop(0, n_rows // SUBLANES, body, 0)


def _rows_wait(src_hbm, dst, sem, n_rows):
    pltpu.make_async_copy(src_hbm.at[pl.ds(0, n_rows), :], dst.at[pl.ds(0, n_rows), :], sem).wait()


def _dispatch_body(dest_ref, ps_ref, pe_ref, x_ref, xs_hbm, zero_ref, sem, zsem, *, n_tok, tile):
    i = pl.program_id(0)

    def zero_copy(e):
        first = pl.multiple_of(pe_ref[e] - MOE_ROWS, MOE_ROWS)
        return pltpu.make_async_copy(zero_ref, xs_hbm.at[pl.ds(first, MOE_ROWS), :], zsem)

    @pl.when(i == 0)
    def _():
        zero_ref[...] = jnp.zeros(zero_ref.shape, zero_ref.dtype)
        for e in range(N_EXPERTS):
            @pl.when(pe_ref[e] > ps_ref[e])
            def _(e=e):
                zero_copy(e).start()
        for e in range(N_EXPERTS):
            @pl.when(pe_ref[e] > ps_ref[e])
            def _(e=e):
                zero_copy(e).wait()

    for k in range(2):
        def scatter_row(base, u, k=k):
            row = dest_ref[k * n_tok + i * tile + base + u]
            pltpu.make_async_copy(_vmem_row(x_ref, base, u), xs_hbm.at[pl.ds(row, 1), :], sem).start()

        _for_row_groups(tile, scatter_row)
    for k in range(2):
        _rows_wait(x_ref, xs_hbm, sem, tile)


def _dispatch(x2, dest, pad_start, pad_end, n_rows):
    t, d = x2.shape
    tile = next(m for m in DISPATCH_TILES if t % m == 0)
    grid_spec = pltpu.PrefetchScalarGridSpec(
        num_scalar_prefetch=3,
        grid=(t // tile,),
        in_specs=[pl.BlockSpec((tile, d), lambda i, dst, ps, pe: (i, 0))],
        out_specs=ANY_SPEC,
        scratch_shapes=[pltpu.VMEM((MOE_ROWS, d), x2.dtype), pltpu.SemaphoreType.DMA, pltpu.SemaphoreType.DMA],
    )
    return pl.pallas_call(
        functools.partial(_dispatch_body, n_tok=t, tile=tile),
        grid_spec=grid_spec,
        out_shape=jax.ShapeDtypeStruct((n_rows, d), x2.dtype),
        compiler_params=_params("arbitrary"),
        name="moe_dispatch",
    )(dest, pad_start, pad_end, x2)


def _expert_body(be_ref, nb_ref, nx_ref, x_ref, wg_hbm, wu_hbm, wd_hbm, o_ref,
                 stage_g, stage_u, stage_d, wg_bf, wu_bf, wd_bf, sem, *, layer):
    i = pl.program_id(0)

    def weight_copies(e):
        return (pltpu.make_async_copy(wg_hbm.at[layer, e], stage_g, sem.at[0]),
                pltpu.make_async_copy(wu_hbm.at[layer, e], stage_u, sem.at[1]),
                pltpu.make_async_copy(wd_hbm.at[layer, e], stage_d, sem.at[2]))

    @pl.when(i < nb_ref[0])
    def _():
        e = be_ref[i]

        @pl.when(i == 0)
        def _():
            for c in weight_copies(e):
                c.start()

        @pl.when((i == 0) | (e != be_ref[jnp.maximum(i - 1, 0)]))
        def _():
            for c in weight_copies(e):
                c.wait()
            wg_bf[...] = stage_g[...].astype(BF16)
            wu_bf[...] = stage_u[...].astype(BF16)
            wd_bf[...] = stage_d[...].astype(BF16)
            nxt = nx_ref[e]

            @pl.when(nxt < N_EXPERTS)
            def _():
                for c in weight_copies(nxt):
                    c.start()

        x = _unpack_bf16_pairs(x_ref[...]).astype(BF16)
        hdn = _silu(_dot(x, wg_bf[...])) * _dot(x, wu_bf[...])
        o_ref[...] = _pack_bf16_pairs(_dot(hdn.astype(BF16), wd_bf[...]))

    @pl.when(i >= nb_ref[0])
    def _():
        o_ref[...] = jnp.zeros(o_ref.shape, o_ref.dtype)


def _expert_blocks(xs, block_expert, n_used, next_expert, w_gate, w_up, w_down, layer):
    n_rows, half = xs.shape
    d = 2 * half
    ff = w_gate.shape[-1]
    grid_spec = pltpu.PrefetchScalarGridSpec(
        num_scalar_prefetch=3,
        grid=(n_rows // MOE_ROWS,),
        in_specs=[pl.BlockSpec((MOE_ROWS, half), lambda i, be, nb, nx: (jnp.minimum(i, nb[0] - 1), 0)),
                  ANY_SPEC, ANY_SPEC, ANY_SPEC],
        out_specs=pl.BlockSpec((MOE_ROWS, half), lambda i, be, nb, nx: (i, 0)),
        scratch_shapes=[pltpu.VMEM((d, ff), F32), pltpu.VMEM((d, ff), F32), pltpu.VMEM((ff, d), F32),
                        pltpu.VMEM((d, ff), BF16), pltpu.VMEM((d, ff), BF16), pltpu.VMEM((ff, d), BF16),
                        pltpu.SemaphoreType.DMA((3,))],
    )
    return pl.pallas_call(
        functools.partial(_expert_body, layer=layer),
        grid_spec=grid_spec,
        out_shape=jax.ShapeDtypeStruct((n_rows, half), jnp.uint32),
        compiler_params=_params("arbitrary"),
        name="moe_expert_blocks",
    )(block_expert, n_used, next_expert, xs, w_gate, w_up, w_down)


def _combine_body(dest_ref, h_ref, gate_ref, y_hbm, mod_ref, *rest, n_tok, tile, ctx_tiles):
    out_refs, (ybuf, sem) = rest[:-2], rest[-2:]
    i = pl.program_id(0)
    n_tiles = pl.num_programs(0)

    def start(blk, slot):
        for k in range(2):
            def gather_row(base, u, k=k):
                row = dest_ref[k * n_tok + blk * tile + base + u]
                pltpu.make_async_copy(y_hbm.at[pl.ds(row, 1), :], _vmem_row(ybuf.at[slot, k], base, u),
                                      sem.at[slot]).start()

            _for_row_groups(tile, gather_row)

    @pl.when(i == 0)
    def _():
        start(0, 0)

    @pl.when(i + 1 < n_tiles)
    def _():
        start(i + 1, (i + 1) % 2)

    slot = i % 2
    for k in range(2):
        _rows_wait(y_hbm, ybuf.at[slot, k], sem.at[slot], tile)
    gate = gate_ref[...]
    y0 = _unpack_bf16_pairs(ybuf[slot, 0])
    y1 = _unpack_bf16_pairs(ybuf[slot, 1])
    out = h_ref[...] + mod_ref[5:6, :] * (gate[:, 0:1] * y0 + gate[:, 1:2] * y1)
    @pl.when(i < ctx_tiles)
    def _():
        out_refs[0][...] = out

    @pl.when(i >= ctx_tiles)
    def _():
        out_refs[1][...] = out


def _combine(h1, yb, dest, gates, mod, cond_of_tile, tile, split_rows):
    t, d = h1.shape
    row_tile = pl.BlockSpec((tile, d), lambda i, dst: (i, 0))
    ctx_tiles = split_rows // tile
    out_specs = _group_specs(tile, d, ctx_tiles)
    out_shape = [jax.ShapeDtypeStruct((split_rows, d), F32), jax.ShapeDtypeStruct((t - split_rows, d), F32)]
    grid_spec = pltpu.PrefetchScalarGridSpec(
        num_scalar_prefetch=1,
        grid=(t // tile,),
        in_specs=[row_tile,
                  pl.BlockSpec((tile, 2), lambda i, dst: (i, 0)),
                  ANY_SPEC,
                  pl.BlockSpec((None, 6, d), lambda i, dst: (cond_of_tile(i), 0, 0))],
        out_specs=out_specs,
        scratch_shapes=[pltpu.VMEM((2, 2, tile, yb.shape[1]), yb.dtype), pltpu.SemaphoreType.DMA((2,))],
    )
    return pl.pallas_call(
        functools.partial(_combine_body, n_tok=t, tile=tile, ctx_tiles=ctx_tiles),
        grid_spec=grid_spec,
        out_shape=out_shape,
        compiler_params=_params("arbitrary"),
        name="moe_gated_residual",
    )(dest, h1, gates, yb, mod)


def _moe(h1, x2, idx_t, gate_t, mod, w_gate, w_up, w_down, layer, cond_of_tile, split_rows):
    t, d = h1.shape
    n = 2 * t
    experts = idx_t.reshape(n)
    onehot = (experts[:, None] == jnp.arange(N_EXPERTS, dtype=jnp.int32)[None, :]).astype(BF16)
    blocks = onehot.reshape(n // COMBINE_TILE, COMBINE_TILE, N_EXPERTS)
    tri = jnp.asarray(np.tril(np.ones((COMBINE_TILE, COMBINE_TILE), np.float32)), BF16)
    within = jnp.einsum('ij,bjk->bik', tri, blocks, preferred_element_type=F32)
    block_total = within[:, -1, :]
    block_first = jnp.cumsum(block_total, axis=0) - block_total
    counts = (block_first[-1] + block_total[-1]).astype(jnp.int32)
    before = (within + block_first[:, None, :]).reshape(n, N_EXPERTS) - 1.0
    rank = jnp.sum(before * onehot.astype(F32), axis=1).astype(jnp.int32)
    padded = (counts + MOE_ROWS - 1) // MOE_ROWS * MOE_ROWS
    pad_end = jnp.cumsum(padded).astype(jnp.int32)
    pad_start = pad_end - padded
    dest = (pad_start[experts] + rank).astype(jnp.int32)
    n_blocks = (n + N_EXPERTS * (MOE_ROWS - 1) + MOE_ROWS - 1) // MOE_ROWS
    block_first_row = jnp.arange(n_blocks, dtype=jnp.int32) * MOE_ROWS
    block_expert = jnp.minimum(jnp.sum((pad_end[None, :] <= block_first_row[:, None]).astype(jnp.int32), axis=1),
                               N_EXPERTS - 1)
    n_used = (pad_end[-1:] // MOE_ROWS).astype(jnp.int32)
    xs = _dispatch(x2, dest, pad_start, pad_end, n_blocks * MOE_ROWS)
    ids = jnp.arange(N_EXPERTS, dtype=jnp.int32)
    later_with_rows = (counts[None, :] > 0) & (ids[None, :] > ids[:, None])
    next_expert = jnp.min(jnp.where(later_with_rows, ids[None, :], N_EXPERTS), axis=1).astype(jnp.int32)
    yb = _expert_blocks(xs, block_expert, n_used, next_expert, w_gate, w_up, w_down, layer)
    return _combine(h1, yb, dest, gate_t.T, mod, cond_of_tile, COMBINE_TILE, split_rows)


def kernel(x_prompt, x_sample, cache_na_k, cache_na_v, cache_gqa_k, cache_gqa_v, cache_diff_k, cache_diff_v, state_hgrn, c, c_ctx, w_mod, b_mod, norm1, norm2, w_in, w_out, hg_lb_logits, hg_onorm, na_qn, na_kn, na_rpb, gqa_qn, gqa_kn, df_qn, df_kn, df_lam, df_subln, router_w, router_b, w_gate, w_up, w_down):
    n_ctx, ctx_len, d = x_prompt.shape
    n_lat, lat_len, _ = x_sample.shape
    depth = w_in.shape[0]
    t_ctx = n_ctx * ctx_len
    assert t_ctx % lat_len == 0 and lat_len % GRID_W == 0 and lat_len // GRID_W >= WIN_ROWS
    tm = next(m for m in (1024, 512, 256) if t_ctx % m == 0 and lat_len % m == 0)
    tm2 = min(tm, 512)
    lat_block0 = t_ctx // lat_len

    def cond_tile(tile_rows):
        def cond_of_tile(i):
            return jnp.where(i < t_ctx // tile_rows, 0, 1 + (i - t_ctx // tile_rows) // (lat_len // tile_rows))
        return cond_of_tile

    sm = jax.nn.softmax(hg_lb_logits.astype(F32), axis=0)
    lower = jnp.cumsum(sm, axis=0) - sm[0:1]
    mod_all = _modulation(jnp.concatenate([c_ctx[None, :], c], axis=0), w_mod, b_mod)
    mod_all = mod_all.reshape(depth, 1 + n_lat, 6, d)
    hgrn_consts = _hgrn_constants(HGRN_CHUNK)
    rope_c = _rope_tables(lat_len, HEAD_W)
    rope_d = _rope_tables(lat_len, DF_DQK)
    past = cache_diff_k.shape[4]
    cache_diff_k2 = cache_diff_k.transpose(0, 1, 2, 4, 3, 5).reshape(n_lat, depth, N_HEADS, past, HEAD_W)

    h_ctx, h_lat = x_prompt.reshape(t_ctx, d), x_sample.reshape(n_lat * lat_len, d)
    caches, states = [], None
    for layer in range(depth):
        mod = mod_all[layer]
        lam_init = 0.8 - 0.6 * math.exp(-0.3 * layer)
        proj = _input_projection(h_ctx, h_lat, mod, norm1[layer], w_in, layer, cond_tile(tm), tm)
        mix_a, states = _hgrn(proj, 0, n_ctx, ctx_len, lower[layer], hg_onorm[layer], hgrn_consts, None, None,
                              states, layer, depth)
        gains = (na_qn[layer], na_kn[layer], gqa_qn[layer], gqa_kn[layer], df_qn[layer], df_kn[layer],
                 df_subln[layer], df_lam[layer])
        mix_b, mix_c, mix_d, *caches = _context_attention(proj, n_ctx, ctx_len, gains, lam_init, layer, depth,
                                                          caches)
        mix_a, _ = _hgrn(proj, lat_block0, n_lat, lat_len, lower[layer], hg_onorm[layer], hgrn_consts,
                         state_hgrn, mix_a, None, layer, depth)
        mix_b = _latent_na(proj, lat_block0, n_lat, lat_len, cache_na_k, cache_na_v, layer,
                           _na_bias(na_rpb[layer]), na_qn[layer], na_kn[layer], mix_b)
        mix_c = _latent_gqa(proj, lat_block0, n_lat, lat_len, cache_gqa_k, cache_gqa_v, layer, rope_c,
                            gqa_qn[layer], gqa_kn[layer], mix_c)
        mix_d = _latent_diff(proj, lat_block0, n_lat, lat_len, cache_diff_k2, cache_diff_v, layer, rope_d,
                             df_qn[layer], df_kn[layer], df_subln[layer], df_lam[layer], lam_init, mix_d)
        h1, x2, idx_t, gate_t = _output_projection((mix_a, mix_b, mix_c, mix_d), w_out[layer].astype(BF16), h_ctx,
                                                   h_lat, mod, norm2[layer], router_w, router_b, cond_tile(tm2), tm2)
        h_ctx, h_lat = _moe(h1, x2, idx_t, gate_t, mod, w_gate, w_up, w_down, layer, cond_tile(COMBINE_TILE), t_ctx)
    y_prompt = h_ctx.reshape(n_ctx, ctx_len, d)
    y_sample = h_lat.reshape(n_lat, lat_len, d)
    return (y_prompt, y_sample, *caches, states)
```

```python
import functools
import math

import numpy as np
import jax
import jax.numpy as jnp
from jax import lax
from jax.experimental import pallas as pl
from jax.experimental.pallas import tpu as pltpu

D_MODEL = 2048
GRID_W = 64
GROUP_W = D_MODEL // 4
N_HEADS = 4
HEAD_W = GROUP_W // N_HEADS
SUBLANES = 8
GQA_KV_HEADS = 2
DF_DQK = HEAD_W // 2
WIN_ROWS = 8
WIN_COLS = 16
N_EXPERTS = 16
N_EXP_GROUPS = 4
EXP_PER_GROUP = N_EXPERTS // N_EXP_GROUPS
EXPERT_FF = D_MODEL // 4
ROPE_THETA = 10000.0
EPS = 1e-6
NEG_INF = -1e30
LOG2E = math.log2(math.e)
IN_WIDTH = 13 * GROUP_W

COL_A_Q, COL_A_FF, COL_A_FB, COL_A_I, COL_A_G = 0, 4, 8, 12, 16
COL_B_Q, COL_B_K, COL_B_V = 20, 24, 28
COL_C_Q, COL_C_K, COL_C_V = 32, 36, 38
COL_D_Q, COL_D_K, COL_D_V = 40, 44, 48

HGRN_CHUNK = 128
HGRN_UNROLL = 4
HGRN_INPUT_VMEM = 24 * 1024 * 1024
CTX_KV_HEADS_PER_STEP = 1
MOE_ROWS = 256
DISPATCH_TILES = (1024, 512, 256)
COMBINE_TILE = 256
VMEM_LIMIT = 48 * 1024 * 1024
VMEM_LIMIT_PROJ = 56 * 1024 * 1024

F32 = jnp.float32
BF16 = jnp.bfloat16
ANY_SPEC = pl.BlockSpec(memory_space=pl.ANY)


def _params(*sem, vmem=VMEM_LIMIT):
    return pltpu.CompilerParams(dimension_semantics=sem, vmem_limit_bytes=vmem)


def _sigmoid(x):
    return 1.0 / (1.0 + jnp.exp(-x))


def _silu(x):
    return x * _sigmoid(x)


def _rms(x, gain, n):
    return x * lax.rsqrt(jnp.sum(x * x, axis=-1, keepdims=True) * (1.0 / n) + EPS) * gain


def _dot(a, b):
    return jnp.dot(a, b, preferred_element_type=F32)


def _dot_nt(a, b):
    return lax.dot_general(a, b, (((1,), (1,)), ((), ())), preferred_element_type=F32)


def _dot_tn(a, b):
    return lax.dot_general(a, b, (((0,), (0,)), ((), ())), preferred_element_type=F32)


def _pack_bf16_pairs(x):
    k = x.shape[1] // 2
    lo = lax.bitcast_convert_type(x[:, :k].astype(BF16).astype(F32), jnp.uint32) >> 16
    hi = lax.bitcast_convert_type(x[:, k:].astype(BF16).astype(F32), jnp.uint32)
    return hi | lo


def _unpack_bf16_pairs(w):
    lo = lax.bitcast_convert_type(w << 16, F32)
    hi = lax.bitcast_convert_type(w & jnp.uint32(0xFFFF0000), F32)
    return jnp.concatenate([lo, hi], axis=1)


def _aligned(x, m):
    return x if isinstance(x, int) else pl.multiple_of(x, m)


def _alias_kwargs(n_inputs, prev, first_out):
    return ([ANY_SPEC] * len(prev), list(prev), {n_inputs + k: first_out + k for k in range(len(prev))})


def _mod_body(cond_ref, w_ref, b_ref, o_ref):
    w = w_ref[...]
    for c in range(cond_ref.shape[0]):
        s = _silu(cond_ref[c])
        o_ref[c:c + 1, :] = jnp.sum(w * s, axis=0, keepdims=True) + b_ref[...]


def _modulation(cond, w_mod, b_mod):
    depth, d, n6 = w_mod.shape
    nc = cond.shape[0]
    tn = 1024
    return pl.pallas_call(
        _mod_body,
        grid=(depth, n6 // tn),
        in_specs=[pl.BlockSpec((nc, d, 1), lambda l, j: (0, 0, 0)),
                  pl.BlockSpec((None, d, tn), lambda l, j: (l, 0, j)),
                  pl.BlockSpec((None, 1, tn), lambda l, j: (l, 0, j))],
        out_specs=pl.BlockSpec((None, nc, tn), lambda l, j: (l, 0, j)),
        out_shape=jax.ShapeDtypeStruct((depth, nc, n6), F32),
        compiler_params=_params("parallel", "parallel"),
        name="adaln_modulation",
    )(cond[:, :, None], w_mod, b_mod[:, None, :])


def _group_specs(tile, d, ctx_tiles):
    return [pl.BlockSpec((tile, d), lambda i, *_: (jnp.minimum(i, ctx_tiles - 1), 0)),
            pl.BlockSpec((tile, d), lambda i, *_: (jnp.maximum(i - ctx_tiles, 0), 0))]


INPROJ_NORM_CHUNKS = 8


def _inproj_body(h0_ref, mod0_ref, hc_ref, hl_ref, mod_ref, n1_ref, w_ref, o_ref, xn_ref, *, ctx_tiles, chunk):
    i = pl.program_id(0)
    j = pl.program_id(1)

    def normalise(h, mod):
        return (_rms(h, n1_ref[...], D_MODEL) * (1.0 + mod[1:2, :]) + mod[0:1, :]).astype(BF16)

    @pl.when((i == 0) & (j == 0))
    def _():
        xn_ref[0] = normalise(h0_ref[...], mod0_ref[...])

    nxt = i + 1
    rows = pl.ds(pl.multiple_of(jnp.minimum(j, INPROJ_NORM_CHUNKS - 1) * chunk, chunk), chunk)
    xn_ref[nxt % 2, rows, :] = normalise(jnp.where(nxt < ctx_tiles, hc_ref[...], hl_ref[...]), mod_ref[...])
    o_ref[...] = _dot(xn_ref[i % 2], w_ref[...])


def _input_projection(h_ctx, h_lat, mod, norm1, w_in_bf16, cond_of_tile, tm):
    d = h_ctx.shape[1]
    t = h_ctx.shape[0] + h_lat.shape[0]
    n_tiles, ctx_tiles, lat_tiles = t // tm, h_ctx.shape[0] // tm, h_lat.shape[0] // tm
    n = w_in_bf16.shape[1]
    tn = 512
    pieces = INPROJ_NORM_CHUNKS
    chunk = tm // pieces
    assert n // tn >= pieces and ctx_tiles >= 1

    def piece(j):
        return jnp.minimum(j, pieces - 1)

    def nxt(i):
        return jnp.minimum(i + 1, n_tiles - 1)

    ctx_chunk = lambda i, j: (jnp.where(nxt(i) < ctx_tiles, nxt(i) * pieces + piece(j), ctx_tiles * pieces - 1), 0)
    lat_chunk = lambda i, j: (jnp.where(nxt(i) >= ctx_tiles, (nxt(i) - ctx_tiles) * pieces + piece(j), 0), 0)
    return pl.pallas_call(
        functools.partial(_inproj_body, ctx_tiles=ctx_tiles, chunk=chunk),
        grid=(n_tiles, n // tn),
        in_specs=[pl.BlockSpec((tm, d), lambda i, j: (0, 0)),
                  pl.BlockSpec((None, 6, d), lambda i, j: (cond_of_tile(0), 0, 0)),
                  pl.BlockSpec((chunk, d), ctx_chunk),
                  pl.BlockSpec((chunk, d), lat_chunk),
                  pl.BlockSpec((None, 6, d), lambda i, j: (cond_of_tile(nxt(i)), 0, 0)),
                  pl.BlockSpec((1, d), lambda i, j: (0, 0)),
                  pl.BlockSpec((d, tn), lambda i, j: (0, j))],
        out_specs=pl.BlockSpec((tm, tn), lambda i, j: (i, j)),
        out_shape=jax.ShapeDtypeStruct((t, n), F32),
        scratch_shapes=[pltpu.VMEM((2, tm, d), BF16)],
        compiler_params=_params("arbitrary", "arbitrary"),
        name="norm_modulate_in_proj",
    )(h_ctx, mod, h_ctx, h_lat, mod, norm1[None, :], w_in_bf16)


def _hgrn_constants(c):
    nl = int(math.log2(c))
    idx = np.arange(c)
    e = np.zeros((nl + 2, c, c), np.float32)
    m = np.zeros((nl + 1, c, c), np.float32)
    e[0] = idx[None, :] <= idx[:, None]
    e[1] = idx[None, :] > idx[:, None]
    m[0] = np.eye(c)
    for li in range(nl):
        s = c >> (li + 1)
        parent = idx // (2 * s)
        right = (idx % (2 * s)) >= s
        ref = parent * 2 * s + s - 1
        for i in range(c):
            if right[i]:
                e[2 + li, i, ref[i] + 1:i + 1] = 1.0
            else:
                e[2 + li, i, i + 1:ref[i] + 1] = 1.0
        m[1 + li] = right[:, None] & ~right[None, :] & (parent[:, None] == parent[None, :])
    keep = [0, 1] + [2 + li for li in range(nl) if (c >> (li + 1)) < SUBLANES]
    e = e[keep]
    e2 = np.stack([e, e[:, ::-1, ::-1]]).reshape(2, len(keep) * c, c)
    m2 = np.stack([m, m[:, ::-1, ::-1]])
    return jnp.asarray(e2, BF16), jnp.asarray(m2, F32)


def _hgrn_body(*refs, seq, chunk, unroll, heads, has_s0, emit_state, n_alias):
    q_ref, ff_ref, fb_ref, i_ref, g_ref, lb_ref, on_ref, e_ref, m_ref = refs[:9]
    pos = 9
    s0_ref = None
    if has_s0:
        s0_ref = refs[pos]
        pos += 1
    pos += n_alias
    o_ref = refs[pos]
    pos += 1
    if emit_state:
        st_ref = refs[pos]
        pos += 1
    of_ref, ob_ref = refs[pos], refs[pos + 1]
    c = chunk
    n_chunks = seq // c
    assert seq % c == 0 and n_chunks % unroll == 0
    n_levels = m_ref.shape[1] - 1
    gate_refs = (ff_ref, fb_ref)
    out_refs = (of_ref, ob_ref)

    def chunk_step(c0, d, hh, st):
        rows = pl.ds(c0, c)
        lanes = slice(hh * HEAD_W, (hh + 1) * HEAD_W)
        lb = lb_ref[d:d + 1, lanes]
        f = lb + (1.0 - lb) * _sigmoid(gate_refs[d][rows, lanes])
        g = jnp.log2(f)
        k = 1.0 - f
        q = _silu(q_ref[rows, lanes])
        v = i_ref[rows, lanes].astype(BF16)
        g_hi = g.astype(BF16)
        g_lo = (g - g_hi.astype(F32)).astype(BF16)
        g2 = _dot(e_ref[d], jnp.concatenate([g_hi, g_lo], axis=1))
        gsum = g2[:, 0:HEAD_W] + g2[:, HEAD_W:2 * HEAD_W]
        cum = gsum[0:c]
        x_cum = jnp.exp2(cum)
        x_tail = jnp.exp2(gsum[c:2 * c])
        row = lax.broadcasted_iota(jnp.int32, (c, HEAD_W), 0)
        s = m_ref[d, 0] * _dot_nt(q.astype(BF16), k.astype(BF16))
        n_matmul_levels = 0
        for lv in range(n_levels):
            half = c >> (lv + 1)
            if half >= SUBLANES:
                ref_row = half - 1 if d == 0 else half
                ref = jnp.concatenate(
                    [jnp.broadcast_to(cum[b * 2 * half + ref_row:b * 2 * half + ref_row + 1, :], (2 * half, HEAD_W))
                     for b in range(c // (2 * half))], axis=0)
                near = ((row % (2 * half)) >= half) == (d == 0)
                x_l = jnp.exp2(jnp.where(near, cum - ref, ref - cum))
            else:
                x_l = jnp.exp2(gsum[(2 + n_matmul_levels) * c:(3 + n_matmul_levels) * c])
                n_matmul_levels += 1
            s = s + m_ref[d, 1 + lv] * _dot_nt((q * x_l).astype(BF16), (k * x_l).astype(BF16))
        o = _dot_nt((q * x_cum).astype(BF16), st.astype(BF16)) + _dot(s.astype(BF16), v)
        out_refs[d][rows, lanes] = o
        total = x_cum[c - 1:c, :] if d == 0 else x_cum[0:1, :]
        return st * total + _dot_tn(v, (k * x_tail).astype(BF16))

    if has_s0:
        states0 = tuple(s0_ref[d, hh].T for hh in range(heads) for d in range(2))
    else:
        states0 = tuple(jnp.zeros((HEAD_W, HEAD_W), F32) for _ in range(2 * heads))

    def loop(t, states):
        states = list(states)
        for u in range(unroll):
            j = t * unroll + u
            for hh in range(heads):
                states[2 * hh] = chunk_step(_aligned(j * c, c), 0, hh, states[2 * hh])
                states[2 * hh + 1] = chunk_step(_aligned((n_chunks - 1 - j) * c, c), 1, hh, states[2 * hh + 1])
        return tuple(states)

    if n_chunks == unroll:
        states = loop(0, states0)
    else:
        states = lax.fori_loop(0, n_chunks // unroll, loop, states0)
    for hh in range(heads):
        lanes = slice(hh * HEAD_W, (hh + 1) * HEAD_W)
        o = of_ref[:, lanes] + ob_ref[:, lanes]
        o_ref[:, lanes] = (_rms(o, on_ref[...], HEAD_W) * _silu(g_ref[:, lanes])).astype(o_ref.dtype)
        if emit_state:
            st_ref[0, hh] = states[2 * hh].T
            st_ref[1, hh] = states[2 * hh + 1].T


def _hgrn(proj, row_block0, n_seq, seq, lower, onorm, consts, s0, mixed_prev, state_prev, layer, depth):
    e_mat, masks = consts
    latent = s0 is not None
    hps = next(n for n in (4, 2, 1) if 5 * 2 * seq * n * HEAD_W * 4 <= HGRN_INPUT_VMEM)
    width = hps * HEAD_W

    def col(cb):
        return pl.BlockSpec((seq, width), lambda b, h, cb=cb: (row_block0 + b, cb // hps + h))

    state_spec = pl.BlockSpec((None, None, 2, hps, HEAD_W, HEAD_W), lambda b, h: (b, layer, 0, h, 0, 0))
    in_specs = [col(COL_A_Q), col(COL_A_FF), col(COL_A_FB), col(COL_A_I), col(COL_A_G),
                pl.BlockSpec((2, width), lambda b, h: (0, h)),
                pl.BlockSpec((1, HEAD_W), lambda b, h: (0, 0)),
                pl.BlockSpec(e_mat.shape, lambda b, h: (0, 0, 0)),
                pl.BlockSpec(masks.shape, lambda b, h: (0, 0, 0, 0))]
    args = [proj, proj, proj, proj, proj, lower, onorm[None, :], e_mat, masks]
    if latent:
        in_specs.append(state_spec)
        args.append(s0)
        prev = [mixed_prev]
    else:
        prev = [] if state_prev is None else [state_prev]
    alias_specs, alias_args, aliases = _alias_kwargs(len(args), prev, 0 if latent else 1)
    out_specs = [pl.BlockSpec((seq, width), lambda b, h: (row_block0 + b, h))]
    out_shape = [jax.ShapeDtypeStruct((proj.shape[0], GROUP_W), BF16)]
    if not latent:
        out_specs.append(state_spec)
        out_shape.append(jax.ShapeDtypeStruct((n_seq, depth, 2, N_HEADS, HEAD_W, HEAD_W), F32))
    res = pl.pallas_call(
        functools.partial(_hgrn_body, seq=seq, chunk=HGRN_CHUNK, unroll=min(HGRN_UNROLL, seq // HGRN_CHUNK),
                          heads=hps, has_s0=latent,
                          emit_state=not latent, n_alias=len(prev)),
        grid=(n_seq, N_HEADS // hps),
        in_specs=in_specs + alias_specs, out_specs=out_specs, out_shape=out_shape,
        input_output_aliases=aliases,
        scratch_shapes=[pltpu.VMEM((seq, width), F32), pltpu.VMEM((seq, width), F32)],
        compiler_params=_params("parallel", "parallel"),
        name="hgrn2_latent" if latent else "hgrn2_context",
    )(*args, *alias_args)
    return (res[0], None) if latent else res


def _with_ones(v):
    return jnp.concatenate([v, jnp.ones_like(v)], axis=1)


def _softmax_pv(scores, values1):
    mx = functools.reduce(jnp.maximum, [jnp.max(s, axis=-1, keepdims=True) for s in scores])
    acc = functools.reduce(lambda a, b: a + b,
                           [_dot(jnp.exp2(s - mx).astype(BF16), v) for s, v in zip(scores, values1)])
    return acc[:, 0:HEAD_W] / acc[:, HEAD_W:HEAD_W + 1]


def _diff_pv(s0, s1, lam, values1):
    return _softmax_pv([s0], [values1]) - lam * _softmax_pv([s1], [values1])


def _lane_lt(shape, n):
    return lax.broadcasted_iota(jnp.int32, shape, len(shape) - 1) < n


def _rms_head(x, gain):
    return _rms(x, gain, HEAD_W)


def _rms_halves(x, gain2):
    lo = _lane_lt(x.shape, DF_DQK)
    sq = x * x
    ss_lo = jnp.sum(jnp.where(lo, sq, 0.0), axis=-1, keepdims=True)
    ss_hi = jnp.sum(sq, axis=-1, keepdims=True) - ss_lo
    inv = jnp.where(lo, lax.rsqrt(ss_lo * (1.0 / DF_DQK) + EPS), lax.rsqrt(ss_hi * (1.0 / DF_DQK) + EPS))
    return x * inv * gain2


def _lambda(lam_ref, lam_init):
    l = lam_ref[...]
    return (jnp.exp(jnp.sum(l[0:1] * l[1:2], axis=-1, keepdims=True))
            - jnp.exp(jnp.sum(l[2:3] * l[3:4], axis=-1, keepdims=True)) + lam_init)


N_CTX_ATTN_INPUTS = 17


def _ctx_attn_body(*refs, lam_init, kv_heads):
    (bq_ref, bk_ref, bv_ref, cq_ref, ck_ref, cv_ref, dq_ref, dk_ref, dv_ref,
     naq_ref, nak_ref, gq_ref, gk_ref, dfq_ref, dfk_ref, sub_ref, lam_ref) = refs[:N_CTX_ATTN_INPUTS]
    ob_ref, oc_ref, od_ref, kb_ref, vb_ref, kc_ref, vc_ref, kd_ref, vd_ref = refs[-9:]
    group = N_HEADS // GQA_KV_HEADS
    scale = LOG2E * HEAD_W ** -0.5
    lam = _lambda(lam_ref, lam_init)
    for n in range(kv_heads):
        kv_lanes = slice(n * HEAD_W, (n + 1) * HEAD_W)
        kc = _rms_head(ck_ref[:, kv_lanes], gk_ref[...])
        vc = cv_ref[:, kv_lanes]
        kc_ref[n] = kc
        vc_ref[n] = vc
        kc16 = kc.astype(BF16)
        vc1 = _with_ones(vc.astype(BF16))
        for g in range(group):
            h = n * group + g
            lanes = slice(h * HEAD_W, (h + 1) * HEAD_W)
            qc = (_rms_head(cq_ref[:, lanes], gq_ref[...]) * scale).astype(BF16)
            oc_ref[:, lanes] = _softmax_pv([_dot_nt(qc, kc16)], [vc1]).astype(oc_ref.dtype)
            kb = _rms_head(bk_ref[:, lanes], nak_ref[...])
            vb = bv_ref[:, lanes]
            kb_ref[h] = kb
            vb_ref[h] = vb
            qb = (_rms_head(bq_ref[:, lanes], naq_ref[...]) * scale).astype(BF16)
            ob_ref[:, lanes] = _softmax_pv([_dot_nt(qb, kb.astype(BF16))],
                                           [_with_ones(vb.astype(BF16))]).astype(ob_ref.dtype)
            kd = _rms_halves(dk_ref[:, lanes], dfk_ref[...])
            vd = dv_ref[:, lanes]
            kd_ref[h, 0] = kd[:, 0:DF_DQK]
            kd_ref[h, 1] = kd[:, DF_DQK:2 * DF_DQK]
            vd_ref[h] = vd
            qd = (_rms_halves(dq_ref[:, lanes], dfq_ref[...]) * (LOG2E * DF_DQK ** -0.5)).astype(BF16)
            lo = _lane_lt(kd.shape, DF_DQK)
            s0 = _dot_nt(qd, jnp.where(lo, kd, 0.0).astype(BF16))
            s1 = _dot_nt(qd, jnp.where(lo, 0.0, kd).astype(BF16))
            od = _diff_pv(s0, s1, lam, _with_ones(vd.astype(BF16)))
            od_ref[:, lanes] = (_rms_head(od, sub_ref[...]) * (1.0 - lam_init)).astype(od_ref.dtype)


def _context_attention(proj, n_seq, seq, gains, lam_init, layer, depth, caches_prev):
    na_qn, na_kn, gqa_qn, gqa_kn, df_qn, df_kn, df_subln, df_lam = gains
    group = N_HEADS // GQA_KV_HEADS
    kvs = CTX_KV_HEADS_PER_STEP
    n_heads = kvs * group
    width = n_heads * HEAD_W

    def heads(cb):
        return pl.BlockSpec((seq, width), lambda b, n, cb=cb: (b, cb // n_heads + n))

    def kv_head(cb):
        return pl.BlockSpec((seq, kvs * HEAD_W), lambda b, n, cb=cb: (b, cb // kvs + n))

    vec = pl.BlockSpec((1, HEAD_W), lambda b, n: (0, 0))
    cache_heads = pl.BlockSpec((None, None, n_heads, seq, HEAD_W), lambda b, n: (b, layer, n, 0, 0))
    cache_kv = pl.BlockSpec((None, None, kvs, seq, HEAD_W), lambda b, n: (b, layer, n, 0, 0))
    mixed = pl.BlockSpec((seq, width), lambda b, n: (b, n))
    mixed_shape = jax.ShapeDtypeStruct((proj.shape[0], GROUP_W), BF16)
    cache4 = jax.ShapeDtypeStruct((n_seq, depth, N_HEADS, seq, HEAD_W), F32)
    cache2 = jax.ShapeDtypeStruct((n_seq, depth, GQA_KV_HEADS, seq, HEAD_W), F32)
    cache_dk = jax.ShapeDtypeStruct((n_seq, depth, N_HEADS, 2, seq, DF_DQK), F32)
    args = [proj] * 9 + [na_qn[None, :], na_kn[None, :], gqa_qn[None, :], gqa_kn[None, :],
                         jnp.tile(df_qn, 2)[None, :], jnp.tile(df_kn, 2)[None, :], df_subln[None, :], df_lam]
    assert len(args) == N_CTX_ATTN_INPUTS
    alias_specs, alias_args, aliases = _alias_kwargs(len(args), caches_prev, 3)
    return pl.pallas_call(
        functools.partial(_ctx_attn_body, lam_init=lam_init, kv_heads=kvs),
        grid=(n_seq, GQA_KV_HEADS // kvs),
        in_specs=[heads(COL_B_Q), heads(COL_B_K), heads(COL_B_V),
                  heads(COL_C_Q), kv_head(COL_C_K), kv_head(COL_C_V),
                  heads(COL_D_Q), heads(COL_D_K), heads(COL_D_V),
                  vec, vec, vec, vec, vec, vec, vec,
                  pl.BlockSpec((4, DF_DQK), lambda b, n: (0, 0))] + alias_specs,
        out_specs=[mixed, mixed, mixed, cache_heads, cache_heads, cache_kv, cache_kv,
                   pl.BlockSpec((None, None, n_heads, 2, seq, DF_DQK), lambda b, n: (b, layer, n, 0, 0, 0)),
                   cache_heads],
        out_shape=[mixed_shape, mixed_shape, mixed_shape, cache4, cache4, cache2, cache2, cache_dk, cache4],
        input_output_aliases=aliases,
        compiler_params=_params("parallel", "parallel"),
        name="context_attention",
    )(*args, *alias_args)


def _rope_tables(n_tokens, rot_dim):
    t = np.arange(n_tokens)
    row = (t // GRID_W).astype(np.float32)
    col = (t % GRID_W).astype(np.float32)
    n_freq = rot_dim // 4
    inv = (np.float32(ROPE_THETA) ** (-np.arange(n_freq, dtype=np.float32) / np.float32(n_freq))).astype(np.float32)
    ang = np.concatenate([row[:, None] * inv, col[:, None] * inv], axis=-1).astype(np.float32)
    cos, sin, zero = np.cos(ang), np.sin(ang), np.zeros_like(ang)
    reps = HEAD_W // rot_dim
    a = np.tile(np.concatenate([cos, cos], axis=-1), (1, reps))
    b = np.tile(np.concatenate([-sin, zero], axis=-1), (1, reps))
    c = np.tile(np.concatenate([zero, sin], axis=-1), (1, reps))
    return jnp.asarray(np.stack([a, b, c]), F32)


def _rope(x, tab_ref, half):
    return (x * tab_ref[0] + pltpu.roll(x, HEAD_W - half, 1) * tab_ref[1]
            + pltpu.roll(x, half, 1) * tab_ref[2])


def _na_body(q_ref, k_ref, v_ref, ck_ref, cv_ref, bias_ref, qn_ref, kn_ref, prev_ref, o_ref,
             qs_ref, ks_ref, vs_ref, *, seq):
    del prev_ref
    rows = seq // GRID_W
    n_win = WIN_ROWS * GRID_W
    qs_ref[...] = (_rms_head(q_ref[...], qn_ref[...]) * (LOG2E * HEAD_W ** -0.5)).astype(BF16)
    ks_ref[...] = _rms_head(k_ref[...], kn_ref[...]).astype(BF16)
    vs_ref[...] = _with_ones(v_ref[...].astype(BF16))
    ck = ck_ref[...].astype(BF16)
    cv = _with_ones(cv_ref[...].astype(BF16))

    def row_step(r, carry):
        start = jnp.clip(r - WIN_ROWS // 2, 0, rows - WIN_ROWS)
        win = pl.ds(pl.multiple_of(start * GRID_W, GRID_W), n_win)
        qrows = pl.ds(pl.multiple_of(r * GRID_W, GRID_W), GRID_W)
        q = qs_ref[qrows, :]
        s_win = _dot_nt(q, ks_ref[win, :]) + bias_ref[start - r + (WIN_ROWS - 1)]
        s_ctx = _dot_nt(q, ck)
        o_ref[qrows, :] = _softmax_pv([s_win, s_ctx], [vs_ref[win, :], cv]).astype(o_ref.dtype)
        return carry

    lax.fori_loop(0, rows, row_step, 0, unroll=min(rows, 32))


def _na_bias(rpb):
    col = np.arange(GRID_W)
    col_start = np.clip(col - WIN_COLS // 2, 0, GRID_W - WIN_COLS)
    col_ok = (col[None, :] >= col_start[:, None]) & (col[None, :] < col_start[:, None] + WIN_COLS)
    dc = np.clip(col[None, :] - col[:, None] + WIN_COLS - 1, 0, 2 * WIN_COLS - 2).reshape(-1)
    onehot = (np.arange(2 * WIN_COLS - 1)[:, None] == dc[None, :]).astype(np.float32)
    per_dr = jnp.einsum('hdc,cn->hdn', rpb.astype(F32), jnp.asarray(onehot), precision=lax.Precision.HIGHEST)
    per_dr = jnp.where(col_ok[None, None], LOG2E * per_dr.reshape(rpb.shape[0], -1, GRID_W, GRID_W), NEG_INF)
    wins = jnp.stack([per_dr[:, o:o + WIN_ROWS] for o in range(WIN_ROWS)], axis=1)
    return wins.transpose(0, 1, 3, 2, 4).reshape(rpb.shape[0], WIN_ROWS, GRID_W, WIN_ROWS * GRID_W)


def _latent_na(proj, row_block0, n_seq, seq, cache_k, cache_v, layer, bias, na_qn, na_kn, mixed_prev):
    past = cache_k.shape[3]

    def col(cb):
        return pl.BlockSpec((seq, HEAD_W), lambda b, h, cb=cb: (row_block0 + b, cb + h))

    cache = pl.BlockSpec((None, None, None, past, HEAD_W), lambda b, h: (b, layer, h, 0, 0))
    vec = pl.BlockSpec((1, HEAD_W), lambda b, h: (0, 0))
    return pl.pallas_call(
        functools.partial(_na_body, seq=seq),
        grid=(n_seq, N_HEADS),
        in_specs=[col(COL_B_Q), col(COL_B_K), col(COL_B_V), cache, cache,
                  pl.BlockSpec((None, WIN_ROWS, GRID_W, WIN_ROWS * GRID_W), lambda b, h: (h, 0, 0, 0)),
                  vec, vec, ANY_SPEC],
        out_specs=pl.BlockSpec((seq, HEAD_W), lambda b, h: (row_block0 + b, h)),
        out_shape=jax.ShapeDtypeStruct(mixed_prev.shape, BF16),
        input_output_aliases={8: 0},
        scratch_shapes=[pltpu.VMEM((seq, HEAD_W), BF16), pltpu.VMEM((seq, HEAD_W), BF16),
                        pltpu.VMEM((seq, 2 * HEAD_W), BF16)],
        compiler_params=_params("parallel", "parallel"),
        name="latent_neighbourhood_attention",
    )(proj, proj, proj, cache_k, cache_v, bias, na_qn[None, :], na_kn[None, :], mixed_prev)


def _gqa_body(q_ref, k_ref, v_ref, ck_ref, cv_ref, rope_ref, qn_ref, kn_ref, prev_ref, o_ref,
              qs_ref, ks_ref, vs_ref, *, seq, tq):
    del prev_ref
    group = N_HEADS // GQA_KV_HEADS
    half = HEAD_W // 2
    ks_ref[0:seq, :] = _rope(_rms_head(k_ref[...], kn_ref[...]), rope_ref, half).astype(BF16)
    ks_ref[seq:, :] = ck_ref[...].astype(BF16)
    vs_ref[0:seq, :] = _with_ones(v_ref[...].astype(BF16))
    vs_ref[seq:, :] = _with_ones(cv_ref[...].astype(BF16))
    for g in range(group):
        q = _rms_head(q_ref[:, g * HEAD_W:(g + 1) * HEAD_W], qn_ref[...]) * (LOG2E * HEAD_W ** -0.5)
        qs_ref[g] = _rope(q, rope_ref, half).astype(BF16)
    kk = ks_ref[...]
    vv = vs_ref[...]
    for g in range(group):
        def q_step(i, carry, g=g):
            qrows = pl.ds(pl.multiple_of(i * tq, tq), tq)
            o = _softmax_pv([_dot_nt(qs_ref[g, qrows, :], kk)], [vv])
            o_ref[qrows, g * HEAD_W:(g + 1) * HEAD_W] = o.astype(o_ref.dtype)
            return carry

        lax.fori_loop(0, seq // tq, q_step, 0, unroll=min(seq // tq, 8))


def _latent_gqa(proj, row_block0, n_seq, seq, cache_k, cache_v, layer, rope, gqa_qn, gqa_kn, mixed_prev):
    past = cache_k.shape[3]
    group = N_HEADS // GQA_KV_HEADS
    tq = 256
    cache = pl.BlockSpec((None, None, None, past, HEAD_W), lambda b, n: (b, layer, n, 0, 0))
    vec = pl.BlockSpec((1, HEAD_W), lambda b, n: (0, 0))
    return pl.pallas_call(
        functools.partial(_gqa_body, seq=seq, tq=tq),
        grid=(n_seq, GQA_KV_HEADS),
        in_specs=[pl.BlockSpec((seq, group * HEAD_W), lambda b, n: (row_block0 + b, COL_C_Q // group + n)),
                  pl.BlockSpec((seq, HEAD_W), lambda b, n: (row_block0 + b, COL_C_K + n)),
                  pl.BlockSpec((seq, HEAD_W), lambda b, n: (row_block0 + b, COL_C_V + n)),
                  cache, cache,
                  pl.BlockSpec((3, seq, HEAD_W), lambda b, n: (0, 0, 0)),
                  vec, vec, ANY_SPEC],
        out_specs=pl.BlockSpec((seq, group * HEAD_W), lambda b, n: (row_block0 + b, n)),
        out_shape=jax.ShapeDtypeStruct(mixed_prev.shape, BF16),
        input_output_aliases={8: 0},
        scratch_shapes=[pltpu.VMEM((group, seq, HEAD_W), BF16),
                        pltpu.VMEM((seq + past, HEAD_W), BF16),
                        pltpu.VMEM((seq + past, 2 * HEAD_W), BF16)],
        compiler_params=_params("parallel", "parallel"),
        name="latent_gqa_attention",
    )(proj, proj, proj, cache_k, cache_v, rope, gqa_qn[None, :], gqa_kn[None, :], mixed_prev)


def _diff_body(q_ref, k_ref, v_ref, ck_ref, cv_ref, rope_ref, qn_ref, kn_ref, sub_ref, lam_ref, prev_ref, o_ref,
               qs_ref, k0_ref, k1_ref, vs_ref, *, seq, tq, lam_init):
    del prev_ref
    half = DF_DQK // 2
    k = _rope(_rms_halves(k_ref[...], kn_ref[...]), rope_ref, half)
    lo = _lane_lt(k.shape, DF_DQK)
    k0_ref[0:seq, :] = jnp.where(lo, k, 0.0).astype(BF16)
    k1_ref[0:seq, :] = jnp.where(lo, 0.0, k).astype(BF16)
    ck = ck_ref[...]
    lo_c = _lane_lt(ck.shape, DF_DQK)
    k0_ref[seq:, :] = jnp.where(lo_c, ck, 0.0).astype(BF16)
    k1_ref[seq:, :] = jnp.where(lo_c, 0.0, ck).astype(BF16)
    vs_ref[0:seq, :] = _with_ones(v_ref[...].astype(BF16))
    vs_ref[seq:, :] = _with_ones(cv_ref[...].astype(BF16))
    q = _rms_halves(q_ref[...], qn_ref[...]) * (LOG2E * DF_DQK ** -0.5)
    qs_ref[...] = _rope(q, rope_ref, half).astype(BF16)
    lam = _lambda(lam_ref, lam_init)
    k0 = k0_ref[...]
    k1 = k1_ref[...]
    vv = vs_ref[...]

    def q_step(i, carry):
        qrows = pl.ds(pl.multiple_of(i * tq, tq), tq)
        qb = qs_ref[qrows, :]
        o = _diff_pv(_dot_nt(qb, k0), _dot_nt(qb, k1), lam, vv)
        o_ref[qrows, :] = (_rms_head(o, sub_ref[...]) * (1.0 - lam_init)).astype(o_ref.dtype)
        return carry

    lax.fori_loop(0, seq // tq, q_step, 0, unroll=min(seq // tq, 4))


def _latent_diff(proj, row_block0, n_seq, seq, cache_k2, cache_v, layer, rope, df_qn, df_kn, df_subln, df_lam,
                 lam_init, mixed_prev):
    past = cache_k2.shape[3]
    tq = 256

    def col(cb):
        return pl.BlockSpec((seq, HEAD_W), lambda b, h, cb=cb: (row_block0 + b, cb + h))

    cache = pl.BlockSpec((None, None, None, past, HEAD_W), lambda b, h: (b, layer, h, 0, 0))
    vec = pl.BlockSpec((1, HEAD_W), lambda b, h: (0, 0))
    kv_scratch = pltpu.VMEM((seq + past, HEAD_W), BF16)
    return pl.pallas_call(
        functools.partial(_diff_body, seq=seq, tq=tq, lam_init=lam_init),
        grid=(n_seq, N_HEADS),
        in_specs=[col(COL_D_Q), col(COL_D_K), col(COL_D_V), cache, cache,
                  pl.BlockSpec((3, seq, HEAD_W), lambda b, h: (0, 0, 0)),
                  vec, vec, vec, pl.BlockSpec((4, DF_DQK), lambda b, h: (0, 0)), ANY_SPEC],
        out_specs=pl.BlockSpec((seq, HEAD_W), lambda b, h: (row_block0 + b, h)),
        out_shape=jax.ShapeDtypeStruct(mixed_prev.shape, BF16),
        input_output_aliases={10: 0},
        scratch_shapes=[pltpu.VMEM((seq, HEAD_W), BF16), kv_scratch, kv_scratch,
                        pltpu.VMEM((seq + past, 2 * HEAD_W), BF16)],
        compiler_params=_params("parallel", "parallel"),
        name="latent_diff_attention",
    )(proj, proj, proj, cache_k2, cache_v, rope, jnp.tile(df_qn, 2)[None, :], jnp.tile(df_kn, 2)[None, :],
      df_subln[None, :], df_lam, mixed_prev)


def _first_max(vals):
    best = vals[0]
    idx = jnp.zeros(best.shape, jnp.int32)
    for i in range(1, len(vals)):
        better = vals[i] > best
        best = jnp.where(better, vals[i], best)
        idx = jnp.where(better, i, idx)
    return best, idx


def _pick(vals, idx):
    out = vals[0]
    for i in range(1, len(vals)):
        out = jnp.where(idx == i, vals[i], out)
    return out


def _outproj_body(ma_ref, mb_ref, mc_ref, md_ref, w_ref, hc_ref, hl_ref, mod_ref, n2_ref, rw_ref, rb_ref,
                  h1_ref, x2_ref, idx_ref, gate_ref, mixed_ref, *, ctx_tiles):
    for g, m_ref in enumerate((ma_ref, mb_ref, mc_ref, md_ref)):
        mixed_ref[:, g * GROUP_W:(g + 1) * GROUP_W] = m_ref[...]
    y = mod_ref[2:3, :] * _dot(mixed_ref[...], w_ref[...])

    def residual(h_ref):
        h1_ref[...] = h_ref[...] + y

    is_ctx = pl.program_id(0) < ctx_tiles
    pl.when(is_ctx)(lambda: residual(hc_ref))
    pl.when(jnp.logical_not(is_ctx))(lambda: residual(hl_ref))
    h1 = h1_ref[...]
    x2 = _rms(h1, n2_ref[...], D_MODEL) * (1.0 + mod_ref[4:5, :]) + mod_ref[3:4, :]
    x2_ref[...] = _pack_bf16_pairs(x2)
    x_hi = x2.astype(BF16)
    x_lo = (x2 - x_hi.astype(F32)).astype(BF16)
    acc = _dot(x_hi, rw_ref[...])
    logits_tok = acc[:, 0:HEAD_W] + acc[:, HEAD_W:2 * HEAD_W] + _dot(x_lo, rw_ref[:, 0:HEAD_W])
    logits = logits_tok.T[0:N_EXPERTS, :]
    aff_all = _sigmoid(logits)
    sel_all = aff_all + rb_ref[...]
    aff = [aff_all[e:e + 1, :] for e in range(N_EXPERTS)]
    sel = [sel_all[e:e + 1, :] for e in range(N_EXPERTS)]
    neg = jnp.full(sel[0].shape, -jnp.inf, F32)
    scores = []
    for g in range(N_EXP_GROUPS):
        grp = sel[g * EXP_PER_GROUP:(g + 1) * EXP_PER_GROUP]
        m1, i1 = _first_max(grp)
        m2, _ = _first_max([jnp.where(i1 == j, neg, grp[j]) for j in range(EXP_PER_GROUP)])
        scores.append(m1 + m2)
    _, g_best = _first_max(scores)
    in_sel = [_pick([sel[g * EXP_PER_GROUP + j] for g in range(N_EXP_GROUPS)], g_best)
              for j in range(EXP_PER_GROUP)]
    in_aff = [_pick([aff[g * EXP_PER_GROUP + j] for g in range(N_EXP_GROUPS)], g_best)
              for j in range(EXP_PER_GROUP)]
    _, l1 = _first_max(in_sel)
    _, l2 = _first_max([jnp.where(l1 == j, neg, in_sel[j]) for j in range(EXP_PER_GROUP)])
    w1 = _pick(in_aff, l1)
    w2 = _pick(in_aff, l2)
    idx_ref[0:1, :] = g_best * EXP_PER_GROUP + l1
    idx_ref[1:2, :] = g_best * EXP_PER_GROUP + l2
    gate_ref[0:1, :] = w1 / (w1 + w2)
    gate_ref[1:2, :] = w2 / (w1 + w2)


def _output_projection(mixed4, w_out_bf16, h_ctx, h_lat, mod, norm2, router_w, router_b, cond_of_tile, tm):
    d = h_ctx.shape[1]
    t = h_ctx.shape[0] + h_lat.shape[0]
    ctx_tiles = h_ctx.shape[0] // tm
    slab = pl.BlockSpec((tm, GROUP_W), lambda i: (i, 0))
    rw_hi = router_w.astype(BF16)
    rw_lo = (router_w - rw_hi.astype(F32)).astype(BF16)
    pad = ((0, 0), (0, HEAD_W - N_EXPERTS))
    router_split = jnp.concatenate([jnp.pad(rw_hi, pad), jnp.pad(rw_lo, pad)], axis=1)
    return pl.pallas_call(
        functools.partial(_outproj_body, ctx_tiles=ctx_tiles),
        grid=(t // tm,),
        in_specs=[slab, slab, slab, slab,
                  pl.BlockSpec((d, d), lambda i: (0, 0))] + _group_specs(tm, d, ctx_tiles) + [
                  pl.BlockSpec((None, 6, d), lambda i: (cond_of_tile(i), 0, 0)),
                  pl.BlockSpec((1, d), lambda i: (0, 0)),
                  pl.BlockSpec((d, 2 * HEAD_W), lambda i: (0, 0)),
                  pl.BlockSpec((N_EXPERTS, 1), lambda i: (0, 0))],
        out_specs=[pl.BlockSpec((tm, d), lambda i: (i, 0)),
                   pl.BlockSpec((tm, d // 2), lambda i: (i, 0)),
                   pl.BlockSpec((2, tm), lambda i: (0, i)),
                   pl.BlockSpec((2, tm), lambda i: (0, i))],
        out_shape=[jax.ShapeDtypeStruct((t, d), F32), jax.ShapeDtypeStruct((t, d // 2), jnp.uint32),
                   jax.ShapeDtypeStruct((2, t), jnp.int32), jax.ShapeDtypeStruct((2, t), F32)],
        scratch_shapes=[pltpu.VMEM((tm, d), BF16)],
        compiler_params=_params("parallel", vmem=VMEM_LIMIT_PROJ),
        name="out_proj_residual_router",
    )(*mixed4, w_out_bf16, h_ctx, h_lat, mod, norm2[None, :], router_split, router_b[:, None])


def _vmem_row(ref, base, u):
    return ref.at[pl.ds(base, SUBLANES), :].at[pl.ds(u, 1), :]


def _for_row_groups(n_rows, fn):
    def body(g, carry):
        base = pl.multiple_of(g * SUBLANES, SUBLANES)
        for u in range(SUBLANES):
            fn(base, u)
        return carry

    lax.fori_loop(0, n_rows // SUBLANES, body, 0)


def _rows_wait(src_hbm, dst, sem, n_rows):
    pltpu.make_async_copy(src_hbm.at[pl.ds(0, n_rows), :], dst.at[pl.ds(0, n_rows), :], sem).wait()


def _dispatch_body(dest_ref, ps_ref, pe_ref, x_ref, xs_hbm, zero_ref, sem, zsem, *, n_tok, tile):
    i = pl.program_id(0)

    def zero_copy(e):
        first = pl.multiple_of(pe_ref[e] - MOE_ROWS, MOE_ROWS)
        return pltpu.make_async_copy(zero_ref, xs_hbm.at[pl.ds(first, MOE_ROWS), :], zsem)

    @pl.when(i == 0)
    def _():
        zero_ref[...] = jnp.zeros(zero_ref.shape, zero_ref.dtype)
        for e in range(N_EXPERTS):
            @pl.when(pe_ref[e] > ps_ref[e])
            def _(e=e):
                zero_copy(e).start()
        for e in range(N_EXPERTS):
            @pl.when(pe_ref[e] > ps_ref[e])
            def _(e=e):
                zero_copy(e).wait()

    for k in range(2):
        def scatter_row(base, u, k=k):
            row = dest_ref[k * n_tok + i * tile + base + u]
            pltpu.make_async_copy(_vmem_row(x_ref, base, u), xs_hbm.at[pl.ds(row, 1), :], sem).start()

        _for_row_groups(tile, scatter_row)
    for k in range(2):
        _rows_wait(x_ref, xs_hbm, sem, tile)


def _dispatch(x2, dest, pad_start, pad_end, n_rows):
    t, d = x2.shape
    tile = next(m for m in DISPATCH_TILES if t % m == 0)
    grid_spec = pltpu.PrefetchScalarGridSpec(
        num_scalar_prefetch=3,
        grid=(t // tile,),
        in_specs=[pl.BlockSpec((tile, d), lambda i, dst, ps, pe: (i, 0))],
        out_specs=ANY_SPEC,
        scratch_shapes=[pltpu.VMEM((MOE_ROWS, d), x2.dtype), pltpu.SemaphoreType.DMA, pltpu.SemaphoreType.DMA],
    )
    return pl.pallas_call(
        functools.partial(_dispatch_body, n_tok=t, tile=tile),
        grid_spec=grid_spec,
        out_shape=jax.ShapeDtypeStruct((n_rows, d), x2.dtype),
        compiler_params=_params("arbitrary"),
        name="moe_dispatch",
    )(dest, pad_start, pad_end, x2)


def _expert_body(be_ref, nb_ref, nx_ref, x_ref, wg_hbm, wu_hbm, wd_hbm, o_ref,
                 stage_g, stage_u, stage_d, wg_bf, wu_bf, wd_bf, sem, *, layer):
    i = pl.program_id(0)

    def weight_copies(e):
        return (pltpu.make_async_copy(wg_hbm.at[layer, e], stage_g, sem.at[0]),
                pltpu.make_async_copy(wu_hbm.at[layer, e], stage_u, sem.at[1]),
                pltpu.make_async_copy(wd_hbm.at[layer, e], stage_d, sem.at[2]))

    @pl.when(i < nb_ref[0])
    def _():
        e = be_ref[i]

        @pl.when(i == 0)
        def _():
            for c in weight_copies(e):
                c.start()

        @pl.when((i == 0) | (e != be_ref[jnp.maximum(i - 1, 0)]))
        def _():
            for c in weight_copies(e):
                c.wait()
            wg_bf[...] = stage_g[...].astype(BF16)
            wu_bf[...] = stage_u[...].astype(BF16)
            wd_bf[...] = stage_d[...].astype(BF16)
            nxt = nx_ref[e]

            @pl.when(nxt < N_EXPERTS)
            def _():
                for c in weight_copies(nxt):
                    c.start()

        x = _unpack_bf16_pairs(x_ref[...]).astype(BF16)
        hdn = _silu(_dot(x, wg_bf[...])) * _dot(x, wu_bf[...])
        o_ref[...] = _pack_bf16_pairs(_dot(hdn.astype(BF16), wd_bf[...]))

    @pl.when(i >= nb_ref[0])
    def _():
        o_ref[...] = jnp.zeros(o_ref.shape, o_ref.dtype)


def _expert_blocks(xs, block_expert, n_used, next_expert, w_gate, w_up, w_down, layer):
    n_rows, half = xs.shape
    d = 2 * half
    ff = w_gate.shape[-1]
    grid_spec = pltpu.PrefetchScalarGridSpec(
        num_scalar_prefetch=3,
        grid=(n_rows // MOE_ROWS,),
        in_specs=[pl.BlockSpec((MOE_ROWS, half), lambda i, be, nb, nx: (jnp.minimum(i, nb[0] - 1), 0)),
                  ANY_SPEC, ANY_SPEC, ANY_SPEC],
        out_specs=pl.BlockSpec((MOE_ROWS, half), lambda i, be, nb, nx: (i, 0)),
        scratch_shapes=[pltpu.VMEM((d, ff), F32), pltpu.VMEM((d, ff), F32), pltpu.VMEM((ff, d), F32),
                        pltpu.VMEM((d, ff), BF16), pltpu.VMEM((d, ff), BF16), pltpu.VMEM((ff, d), BF16),
                        pltpu.SemaphoreType.DMA((3,))],
    )
    return pl.pallas_call(
        functools.partial(_expert_body, layer=layer),
        grid_spec=grid_spec,
        out_shape=jax.ShapeDtypeStruct((n_rows, half), jnp.uint32),
        compiler_params=_params("arbitrary"),
        name="moe_expert_blocks",
    )(block_expert, n_used, next_expert, xs, w_gate, w_up, w_down)


def _combine_body(dest_ref, h_ref, gate_ref, y_hbm, mod_ref, *rest, n_tok, tile, ctx_tiles):
    out_refs, (ybuf, sem) = rest[:-2], rest[-2:]
    i = pl.program_id(0)
    n_tiles = pl.num_programs(0)

    def start(blk, slot):
        for k in range(2):
            def gather_row(base, u, k=k):
                row = dest_ref[k * n_tok + blk * tile + base + u]
                pltpu.make_async_copy(y_hbm.at[pl.ds(row, 1), :], _vmem_row(ybuf.at[slot, k], base, u),
                                      sem.at[slot]).start()

            _for_row_groups(tile, gather_row)

    @pl.when(i == 0)
    def _():
        start(0, 0)

    @pl.when(i + 1 < n_tiles)
    def _():
        start(i + 1, (i + 1) % 2)

    slot = i % 2
    for k in range(2):
        _rows_wait(y_hbm, ybuf.at[slot, k], sem.at[slot], tile)
    gate = gate_ref[...]
    y0 = _unpack_bf16_pairs(ybuf[slot, 0])
    y1 = _unpack_bf16_pairs(ybuf[slot, 1])
    out = h_ref[...] + mod_ref[5:6, :] * (gate[:, 0:1] * y0 + gate[:, 1:2] * y1)
    @pl.when(i < ctx_tiles)
    def _():
        out_refs[0][...] = out

    @pl.when(i >= ctx_tiles)
    def _():
        out_refs[1][...] = out


def _combine(h1, yb, dest, gates, mod, cond_of_tile, tile, split_rows):
    t, d = h1.shape
    row_tile = pl.BlockSpec((tile, d), lambda i, dst: (i, 0))
    ctx_tiles = split_rows // tile
    out_specs = _group_specs(tile, d, ctx_tiles)
    out_shape = [jax.ShapeDtypeStruct((split_rows, d), F32), jax.ShapeDtypeStruct((t - split_rows, d), F32)]
    grid_spec = pltpu.PrefetchScalarGridSpec(
        num_scalar_prefetch=1,
        grid=(t // tile,),
        in_specs=[row_tile,
                  pl.BlockSpec((tile, 2), lambda i, dst: (i, 0)),
                  ANY_SPEC,
                  pl.BlockSpec((None, 6, d), lambda i, dst: (cond_of_tile(i), 0, 0))],
        out_specs=out_specs,
        scratch_shapes=[pltpu.VMEM((2, 2, tile, yb.shape[1]), yb.dtype), pltpu.SemaphoreType.DMA((2,))],
    )
    return pl.pallas_call(
        functools.partial(_combine_body, n_tok=t, tile=tile, ctx_tiles=ctx_tiles),
        grid_spec=grid_spec,
        out_shape=out_shape,
        compiler_params=_params("arbitrary"),
        name="moe_gated_residual",
    )(dest, h1, gates, yb, mod)


def _moe(h1, x2, idx_t, gate_t, mod, w_gate, w_up, w_down, layer, cond_of_tile, split_rows):
    t, d = h1.shape
    n = 2 * t
    experts = idx_t.reshape(n)
    onehot = (experts[:, None] == jnp.arange(N_EXPERTS, dtype=jnp.int32)[None, :]).astype(BF16)
    blocks = onehot.reshape(n // COMBINE_TILE, COMBINE_TILE, N_EXPERTS)
    tri = jnp.asarray(np.tril(np.ones((COMBINE_TILE, COMBINE_TILE), np.float32)), BF16)
    within = jnp.einsum('ij,bjk->bik', tri, blocks, preferred_element_type=F32)
    block_total = within[:, -1, :]
    block_first = jnp.cumsum(block_total, axis=0) - block_total
    counts = (block_first[-1] + block_total[-1]).astype(jnp.int32)
    before = (within + block_first[:, None, :]).reshape(n, N_EXPERTS) - 1.0
    rank = jnp.sum(before * onehot.astype(F32), axis=1).astype(jnp.int32)
    padded = (counts + MOE_ROWS - 1) // MOE_ROWS * MOE_ROWS
    pad_end = jnp.cumsum(padded).astype(jnp.int32)
    pad_start = pad_end - padded
    dest = (pad_start[experts] + rank).astype(jnp.int32)
    n_blocks = (n + N_EXPERTS * (MOE_ROWS - 1) + MOE_ROWS - 1) // MOE_ROWS
    block_first_row = jnp.arange(n_blocks, dtype=jnp.int32) * MOE_ROWS
    block_expert = jnp.minimum(jnp.sum((pad_end[None, :] <= block_first_row[:, None]).astype(jnp.int32), axis=1),
                               N_EXPERTS - 1)
    n_used = (pad_end[-1:] // MOE_ROWS).astype(jnp.int32)
    xs = _dispatch(x2, dest, pad_start, pad_end, n_blocks * MOE_ROWS)
    ids = jnp.arange(N_EXPERTS, dtype=jnp.int32)
    later_with_rows = (counts[None, :] > 0) & (ids[None, :] > ids[:, None])
    next_expert = jnp.min(jnp.where(later_with_rows, ids[None, :], N_EXPERTS), axis=1).astype(jnp.int32)
    yb = _expert_blocks(xs, block_expert, n_used, next_expert, w_gate, w_up, w_down, layer)
    return _combine(h1, yb, dest, gate_t.T, mod, cond_of_tile, COMBINE_TILE, split_rows)


def kernel(x_prompt, x_sample, cache_na_k, cache_na_v, cache_gqa_k, cache_gqa_v, cache_diff_k, cache_diff_v, state_hgrn, c, c_ctx, w_mod, b_mod, norm1, norm2, w_in, w_out, hg_lb_logits, hg_onorm, na_qn, na_kn, na_rpb, gqa_qn, gqa_kn, df_qn, df_kn, df_lam, df_subln, router_w, router_b, w_gate, w_up, w_down):
    n_ctx, ctx_len, d = x_prompt.shape
    n_lat, lat_len, _ = x_sample.shape
    depth = w_in.shape[0]
    t_ctx = n_ctx * ctx_len
    assert t_ctx % lat_len == 0 and lat_len % GRID_W == 0 and lat_len // GRID_W >= WIN_ROWS
    tm = next(m for m in (1024, 512, 256) if t_ctx % m == 0 and lat_len % m == 0)
    tm2 = min(tm, 512)
    lat_block0 = t_ctx // lat_len

    def cond_tile(tile_rows):
        def cond_of_tile(i):
            return jnp.where(i < t_ctx // tile_rows, 0, 1 + (i - t_ctx // tile_rows) // (lat_len // tile_rows))
        return cond_of_tile

    sm = jax.nn.softmax(hg_lb_logits.astype(F32), axis=0)
    lower = jnp.cumsum(sm, axis=0) - sm[0:1]
    mod_all = _modulation(jnp.concatenate([c_ctx[None, :], c], axis=0), w_mod, b_mod)
    mod_all = mod_all.reshape(depth, 1 + n_lat, 6, d)
    hgrn_consts = _hgrn_constants(HGRN_CHUNK)
    rope_c = _rope_tables(lat_len, HEAD_W)
    rope_d = _rope_tables(lat_len, DF_DQK)
    past = cache_diff_k.shape[4]
    cache_diff_k2 = cache_diff_k.transpose(0, 1, 2, 4, 3, 5).reshape(n_lat, depth, N_HEADS, past, HEAD_W)

    h_ctx, h_lat = x_prompt.reshape(t_ctx, d), x_sample.reshape(n_lat * lat_len, d)
    caches, states = [], None
    for layer in range(depth):
        mod = mod_all[layer]
        lam_init = 0.8 - 0.6 * math.exp(-0.3 * layer)
        proj = _input_projection(h_ctx, h_lat, mod, norm1[layer], w_in[layer].astype(BF16), cond_tile(tm), tm)
        mix_a, states = _hgrn(proj, 0, n_ctx, ctx_len, lower[layer], hg_onorm[layer], hgrn_consts, None, None,
                              states, layer, depth)
        gains = (na_qn[layer], na_kn[layer], gqa_qn[layer], gqa_kn[layer], df_qn[layer], df_kn[layer],
                 df_subln[layer], df_lam[layer])
        mix_b, mix_c, mix_d, *caches = _context_attention(proj, n_ctx, ctx_len, gains, lam_init, layer, depth,
                                                          caches)
        mix_a, _ = _hgrn(proj, lat_block0, n_lat, lat_len, lower[layer], hg_onorm[layer], hgrn_consts,
                         state_hgrn, mix_a, None, layer, depth)
        mix_b = _latent_na(proj, lat_block0, n_lat, lat_len, cache_na_k, cache_na_v, layer,
                           _na_bias(na_rpb[layer]), na_qn[layer], na_kn[layer], mix_b)
        mix_c = _latent_gqa(proj, lat_block0, n_lat, lat_len, cache_gqa_k, cache_gqa_v, layer, rope_c,
                            gqa_qn[layer], gqa_kn[layer], mix_c)
        mix_d = _latent_diff(proj, lat_block0, n_lat, lat_len, cache_diff_k2, cache_diff_v, layer, rope_d,
                             df_qn[layer], df_kn[layer], df_subln[layer], df_lam[layer], lam_init, mix_d)
        h1, x2, idx_t, gate_t = _output_projection((mix_a, mix_b, mix_c, mix_d), w_out[layer].astype(BF16), h_ctx,
                                                   h_lat, mod, norm2[layer], router_w, router_b, cond_tile(tm2), tm2)
        h_ctx, h_lat = _moe(h1, x2, idx_t, gate_t, mod, w_gate, w_up, w_down, layer, cond_tile(COMBINE_TILE), t_ctx)
    y_prompt = h_ctx.reshape(n_ctx, ctx_len, d)
    y_sample = h_lat.reshape(n_lat, lat_len, d)
    return (y_prompt, y_sample, *caches, states)
```

```python
import functools
import math

import numpy as np
import jax
import jax.numpy as jnp
from jax import lax
from jax.experimental import pallas as pl
from jax.experimental.pallas import tpu as pltpu

D_MODEL = 2048
GRID_W = 64
GROUP_W = D_MODEL // 4
N_HEADS = 4
HEAD_W = GROUP_W // N_HEADS
SUBLANES = 8
GQA_KV_HEADS = 2
DF_DQK = HEAD_W // 2
WIN_ROWS = 8
WIN_COLS = 16
N_EXPERTS = 16
N_EXP_GROUPS = 4
EXP_PER_GROUP = N_EXPERTS // N_EXP_GROUPS
EXPERT_FF = D_MODEL // 4
ROPE_THETA = 10000.0
EPS = 1e-6
NEG_INF = -1e30
LOG2E = math.log2(math.e)
IN_WIDTH = 13 * GROUP_W

COL_A_Q, COL_A_FF, COL_A_FB, COL_A_I, COL_A_G = 0, 4, 8, 12, 16
COL_B_Q, COL_B_K, COL_B_V = 20, 24, 28
COL_C_Q, COL_C_K, COL_C_V = 32, 36, 38
COL_D_Q, COL_D_K, COL_D_V = 40, 44, 48

HGRN_CHUNK = 128
HGRN_UNROLL = 4
HGRN_INPUT_VMEM = 24 * 1024 * 1024
CTX_KV_HEADS_PER_STEP = 1
MOE_ROWS = 256
DISPATCH_TILES = (1024, 512, 256)
COMBINE_TILE = 256
OUTPROJ_PARTS = 2
VMEM_LIMIT = 48 * 1024 * 1024
VMEM_LIMIT_PROJ = 56 * 1024 * 1024

F32 = jnp.float32
BF16 = jnp.bfloat16
ANY_SPEC = pl.BlockSpec(memory_space=pl.ANY)


def _params(*sem, vmem=VMEM_LIMIT):
    return pltpu.CompilerParams(dimension_semantics=sem, vmem_limit_bytes=vmem)


def _sigmoid(x):
    return 1.0 / (1.0 + jnp.exp(-x))


def _silu(x):
    return x * _sigmoid(x)


def _rms(x, gain, n):
    return x * lax.rsqrt(jnp.sum(x * x, axis=-1, keepdims=True) * (1.0 / n) + EPS) * gain


def _dot(a, b):
    return jnp.dot(a, b, preferred_element_type=F32)


def _dot_nt(a, b):
    return lax.dot_general(a, b, (((1,), (1,)), ((), ())), preferred_element_type=F32)


def _dot_tn(a, b):
    return lax.dot_general(a, b, (((0,), (0,)), ((), ())), preferred_element_type=F32)


def _pack_bf16_pairs(x):
    k = x.shape[1] // 2
    lo = lax.bitcast_convert_type(x[:, :k].astype(BF16).astype(F32), jnp.uint32) >> 16
    hi = lax.bitcast_convert_type(x[:, k:].astype(BF16).astype(F32), jnp.uint32)
    return hi | lo


def _unpack_bf16_pairs(w):
    lo = lax.bitcast_convert_type(w << 16, F32)
    hi = lax.bitcast_convert_type(w & jnp.uint32(0xFFFF0000), F32)
    return jnp.concatenate([lo, hi], axis=1)


def _aligned(x, m):
    return x if isinstance(x, int) else pl.multiple_of(x, m)


def _alias_kwargs(n_inputs, prev, first_out):
    return ([ANY_SPEC] * len(prev), list(prev), {n_inputs + k: first_out + k for k in range(len(prev))})


def _mod_body(cond_ref, w_ref, b_ref, o_ref):
    w = w_ref[...]
    for c in range(cond_ref.shape[0]):
        s = _silu(cond_ref[c])
        o_ref[c:c + 1, :] = jnp.sum(w * s, axis=0, keepdims=True) + b_ref[...]


def _modulation(cond, w_mod, b_mod):
    depth, d, n6 = w_mod.shape
    nc = cond.shape[0]
    tn = 1024
    return pl.pallas_call(
        _mod_body,
        grid=(depth, n6 // tn),
        in_specs=[pl.BlockSpec((nc, d, 1), lambda l, j: (0, 0, 0)),
                  pl.BlockSpec((None, d, tn), lambda l, j: (l, 0, j)),
                  pl.BlockSpec((None, 1, tn), lambda l, j: (l, 0, j))],
        out_specs=pl.BlockSpec((None, nc, tn), lambda l, j: (l, 0, j)),
        out_shape=jax.ShapeDtypeStruct((depth, nc, n6), F32),
        compiler_params=_params("parallel", "parallel"),
        name="adaln_modulation",
    )(cond[:, :, None], w_mod, b_mod[:, None, :])


def _group_specs(tile, d, ctx_tiles):
    return [pl.BlockSpec((tile, d), lambda i, *_: (jnp.minimum(i, ctx_tiles - 1), 0)),
            pl.BlockSpec((tile, d), lambda i, *_: (jnp.maximum(i - ctx_tiles, 0), 0))]


INPROJ_NORM_CHUNKS = 8


def _inproj_body(h0_ref, mod0_ref, hc_ref, hl_ref, mod_ref, n1_ref, w_ref, o_ref, xn_ref, *, ctx_tiles, chunk):
    i = pl.program_id(0)
    j = pl.program_id(1)

    def normalise(h, mod):
        return (_rms(h, n1_ref[...], D_MODEL) * (1.0 + mod[1:2, :]) + mod[0:1, :]).astype(BF16)

    @pl.when((i == 0) & (j == 0))
    def _():
        xn_ref[0] = normalise(h0_ref[...], mod0_ref[...])

    nxt = i + 1
    rows = pl.ds(pl.multiple_of(jnp.minimum(j, INPROJ_NORM_CHUNKS - 1) * chunk, chunk), chunk)
    xn_ref[nxt % 2, rows, :] = normalise(jnp.where(nxt < ctx_tiles, hc_ref[...], hl_ref[...]), mod_ref[...])
    o_ref[...] = _dot(xn_ref[i % 2], w_ref[...])


def _input_projection(h_ctx, h_lat, mod, norm1, w_in_bf16, cond_of_tile, tm):
    d = h_ctx.shape[1]
    t = h_ctx.shape[0] + h_lat.shape[0]
    n_tiles, ctx_tiles, lat_tiles = t // tm, h_ctx.shape[0] // tm, h_lat.shape[0] // tm
    n = w_in_bf16.shape[1]
    tn = 512
    pieces = INPROJ_NORM_CHUNKS
    chunk = tm // pieces
    assert n // tn >= pieces and ctx_tiles >= 1

    def piece(j):
        return jnp.minimum(j, pieces - 1)

    def nxt(i):
        return jnp.minimum(i + 1, n_tiles - 1)

    ctx_chunk = lambda i, j: (jnp.where(nxt(i) < ctx_tiles, nxt(i) * pieces + piece(j), ctx_tiles * pieces - 1), 0)
    lat_chunk = lambda i, j: (jnp.where(nxt(i) >= ctx_tiles, (nxt(i) - ctx_tiles) * pieces + piece(j), 0), 0)
    return pl.pallas_call(
        functools.partial(_inproj_body, ctx_tiles=ctx_tiles, chunk=chunk),
        grid=(n_tiles, n // tn),
        in_specs=[pl.BlockSpec((tm, d), lambda i, j: (0, 0)),
                  pl.BlockSpec((None, 6, d), lambda i, j: (cond_of_tile(0), 0, 0)),
                  pl.BlockSpec((chunk, d), ctx_chunk),
                  pl.BlockSpec((chunk, d), lat_chunk),
                  pl.BlockSpec((None, 6, d), lambda i, j: (cond_of_tile(nxt(i)), 0, 0)),
                  pl.BlockSpec((1, d), lambda i, j: (0, 0)),
                  pl.BlockSpec((d, tn), lambda i, j: (0, j))],
        out_specs=pl.BlockSpec((tm, tn), lambda i, j: (i, j)),
        out_shape=jax.ShapeDtypeStruct((t, n), F32),
        scratch_shapes=[pltpu.VMEM((2, tm, d), BF16)],
        compiler_params=_params("arbitrary", "arbitrary"),
        name="norm_modulate_in_proj",
    )(h_ctx, mod, h_ctx, h_lat, mod, norm1[None, :], w_in_bf16)


def _hgrn_constants(c):
    nl = int(math.log2(c))
    idx = np.arange(c)
    e = np.zeros((nl + 2, c, c), np.float32)
    m = np.zeros((nl + 1, c, c), np.float32)
    e[0] = idx[None, :] <= idx[:, None]
    e[1] = idx[None, :] > idx[:, None]
    m[0] = np.eye(c)
    for li in range(nl):
        s = c >> (li + 1)
        parent = idx // (2 * s)
        right = (idx % (2 * s)) >= s
        ref = parent * 2 * s + s - 1
        for i in range(c):
            if right[i]:
                e[2 + li, i, ref[i] + 1:i + 1] = 1.0
            else:
                e[2 + li, i, i + 1:ref[i] + 1] = 1.0
        m[1 + li] = right[:, None] & ~right[None, :] & (parent[:, None] == parent[None, :])
    keep = [0, 1] + [2 + li for li in range(nl) if (c >> (li + 1)) < SUBLANES]
    e = e[keep]
    e2 = np.stack([e, e[:, ::-1, ::-1]]).reshape(2, len(keep) * c, c)
    m2 = np.stack([m, m[:, ::-1, ::-1]])
    return jnp.asarray(e2, BF16), jnp.asarray(m2, F32)


def _hgrn_body(*refs, seq, chunk, unroll, heads, has_s0, emit_state, n_alias):
    q_ref, ff_ref, fb_ref, i_ref, g_ref, lb_ref, on_ref, e_ref, m_ref = refs[:9]
    pos = 9
    s0_ref = None
    if has_s0:
        s0_ref = refs[pos]
        pos += 1
    pos += n_alias
    o_ref = refs[pos]
    pos += 1
    if emit_state:
        st_ref = refs[pos]
        pos += 1
    of_ref, ob_ref = refs[pos], refs[pos + 1]
    c = chunk
    n_chunks = seq // c
    assert seq % c == 0 and n_chunks % unroll == 0
    n_levels = m_ref.shape[1] - 1
    gate_refs = (ff_ref, fb_ref)
    out_refs = (of_ref, ob_ref)

    def chunk_step(c0, d, hh, st):
        rows = pl.ds(c0, c)
        lanes = slice(hh * HEAD_W, (hh + 1) * HEAD_W)
        lb = lb_ref[d:d + 1, lanes]
        f = lb + (1.0 - lb) * _sigmoid(gate_refs[d][rows, lanes])
        g = jnp.log2(f)
        k = 1.0 - f
        q = _silu(q_ref[rows, lanes])
        v = i_ref[rows, lanes].astype(BF16)
        g_hi = g.astype(BF16)
        g_lo = (g - g_hi.astype(F32)).astype(BF16)
        g2 = _dot(e_ref[d], jnp.concatenate([g_hi, g_lo], axis=1))
        gsum = g2[:, 0:HEAD_W] + g2[:, HEAD_W:2 * HEAD_W]
        cum = gsum[0:c]
        x_cum = jnp.exp2(cum)
        x_tail = jnp.exp2(gsum[c:2 * c])
        row = lax.broadcasted_iota(jnp.int32, (c, HEAD_W), 0)
        s = m_ref[d, 0] * _dot_nt(q.astype(BF16), k.astype(BF16))
        n_matmul_levels = 0
        for lv in range(n_levels):
            half = c >> (lv + 1)
            if half >= SUBLANES:
                ref_row = half - 1 if d == 0 else half
                ref = jnp.concatenate(
                    [jnp.broadcast_to(cum[b * 2 * half + ref_row:b * 2 * half + ref_row + 1, :], (2 * half, HEAD_W))
                     for b in range(c // (2 * half))], axis=0)
                near = ((row % (2 * half)) >= half) == (d == 0)
                x_l = jnp.exp2(jnp.where(near, cum - ref, ref - cum))
            else:
                x_l = jnp.exp2(gsum[(2 + n_matmul_levels) * c:(3 + n_matmul_levels) * c])
                n_matmul_levels += 1
            s = s + m_ref[d, 1 + lv] * _dot_nt((q * x_l).astype(BF16), (k * x_l).astype(BF16))
        o = _dot_nt((q * x_cum).astype(BF16), st.astype(BF16)) + _dot(s.astype(BF16), v)
        out_refs[d][rows, lanes] = o
        total = x_cum[c - 1:c, :] if d == 0 else x_cum[0:1, :]
        return st * total + _dot_tn(v, (k * x_tail).astype(BF16))

    if has_s0:
        states0 = tuple(s0_ref[d, hh].T for hh in range(heads) for d in range(2))
    else:
        states0 = tuple(jnp.zeros((HEAD_W, HEAD_W), F32) for _ in range(2 * heads))

    def loop(t, states):
        states = list(states)
        for u in range(unroll):
            j = t * unroll + u
            for hh in range(heads):
                states[2 * hh] = chunk_step(_aligned(j * c, c), 0, hh, states[2 * hh])
                states[2 * hh + 1] = chunk_step(_aligned((n_chunks - 1 - j) * c, c), 1, hh, states[2 * hh + 1])
        return tuple(states)

    if n_chunks == unroll:
        states = loop(0, states0)
    else:
        states = lax.fori_loop(0, n_chunks // unroll, loop, states0)
    for hh in range(heads):
        lanes = slice(hh * HEAD_W, (hh + 1) * HEAD_W)
        o = of_ref[:, lanes] + ob_ref[:, lanes]
        o_ref[:, lanes] = (_rms(o, on_ref[...], HEAD_W) * _silu(g_ref[:, lanes])).astype(o_ref.dtype)
        if emit_state:
            st_ref[0, hh] = states[2 * hh].T
            st_ref[1, hh] = states[2 * hh + 1].T


def _hgrn(proj, row_block0, n_seq, seq, lower, onorm, consts, s0, mixed_prev, state_prev, layer, depth):
    e_mat, masks = consts
    latent = s0 is not None
    hps = next(n for n in (4, 2, 1) if 5 * 2 * seq * n * HEAD_W * 4 <= HGRN_INPUT_VMEM)
    width = hps * HEAD_W

    def col(cb):
        return pl.BlockSpec((seq, width), lambda b, h, cb=cb: (row_block0 + b, cb // hps + h))

    state_spec = pl.BlockSpec((None, None, 2, hps, HEAD_W, HEAD_W), lambda b, h: (b, layer, 0, h, 0, 0))
    in_specs = [col(COL_A_Q), col(COL_A_FF), col(COL_A_FB), col(COL_A_I), col(COL_A_G),
                pl.BlockSpec((2, width), lambda b, h: (0, h)),
                pl.BlockSpec((1, HEAD_W), lambda b, h: (0, 0)),
                pl.BlockSpec(e_mat.shape, lambda b, h: (0, 0, 0)),
                pl.BlockSpec(masks.shape, lambda b, h: (0, 0, 0, 0))]
    args = [proj, proj, proj, proj, proj, lower, onorm[None, :], e_mat, masks]
    if latent:
        in_specs.append(state_spec)
        args.append(s0)
        prev = [mixed_prev]
    else:
        prev = [] if state_prev is None else [state_prev]
    alias_specs, alias_args, aliases = _alias_kwargs(len(args), prev, 0 if latent else 1)
    out_specs = [pl.BlockSpec((seq, width), lambda b, h: (row_block0 + b, h))]
    out_shape = [jax.ShapeDtypeStruct((proj.shape[0], GROUP_W), BF16)]
    if not latent:
        out_specs.append(state_spec)
        out_shape.append(jax.ShapeDtypeStruct((n_seq, depth, 2, N_HEADS, HEAD_W, HEAD_W), F32))
    res = pl.pallas_call(
        functools.partial(_hgrn_body, seq=seq, chunk=HGRN_CHUNK, unroll=min(HGRN_UNROLL, seq // HGRN_CHUNK),
                          heads=hps, has_s0=latent,
                          emit_state=not latent, n_alias=len(prev)),
        grid=(n_seq, N_HEADS // hps),
        in_specs=in_specs + alias_specs, out_specs=out_specs, out_shape=out_shape,
        input_output_aliases=aliases,
        scratch_shapes=[pltpu.VMEM((seq, width), F32), pltpu.VMEM((seq, width), F32)],
        compiler_params=_params("parallel", "parallel"),
        name="hgrn2_latent" if latent else "hgrn2_context",
    )(*args, *alias_args)
    return (res[0], None) if latent else res


def _with_ones(v):
    return jnp.concatenate([v, jnp.ones_like(v)], axis=1)


def _softmax_pv(scores, values1):
    mx = functools.reduce(jnp.maximum, [jnp.max(s, axis=-1, keepdims=True) for s in scores])
    acc = functools.reduce(lambda a, b: a + b,
                           [_dot(jnp.exp2(s - mx).astype(BF16), v) for s, v in zip(scores, values1)])
    return acc[:, 0:HEAD_W] / acc[:, HEAD_W:HEAD_W + 1]


def _diff_pv(s0, s1, lam, values1):
    return _softmax_pv([s0], [values1]) - lam * _softmax_pv([s1], [values1])


def _lane_lt(shape, n):
    return lax.broadcasted_iota(jnp.int32, shape, len(shape) - 1) < n


def _rms_head(x, gain):
    return _rms(x, gain, HEAD_W)


def _rms_halves(x, gain2):
    lo = _lane_lt(x.shape, DF_DQK)
    sq = x * x
    ss_lo = jnp.sum(jnp.where(lo, sq, 0.0), axis=-1, keepdims=True)
    ss_hi = jnp.sum(sq, axis=-1, keepdims=True) - ss_lo
    inv = jnp.where(lo, lax.rsqrt(ss_lo * (1.0 / DF_DQK) + EPS), lax.rsqrt(ss_hi * (1.0 / DF_DQK) + EPS))
    return x * inv * gain2


def _lambda(lam_ref, lam_init):
    l = lam_ref[...]
    return (jnp.exp(jnp.sum(l[0:1] * l[1:2], axis=-1, keepdims=True))
            - jnp.exp(jnp.sum(l[2:3] * l[3:4], axis=-1, keepdims=True)) + lam_init)


N_CTX_ATTN_INPUTS = 17


def _ctx_attn_body(*refs, lam_init, kv_heads):
    (bq_ref, bk_ref, bv_ref, cq_ref, ck_ref, cv_ref, dq_ref, dk_ref, dv_ref,
     naq_ref, nak_ref, gq_ref, gk_ref, dfq_ref, dfk_ref, sub_ref, lam_ref) = refs[:N_CTX_ATTN_INPUTS]
    ob_ref, oc_ref, od_ref, kb_ref, vb_ref, kc_ref, vc_ref, kd_ref, vd_ref = refs[-9:]
    group = N_HEADS // GQA_KV_HEADS
    scale = LOG2E * HEAD_W ** -0.5
    lam = _lambda(lam_ref, lam_init)
    for n in range(kv_heads):
        kv_lanes = slice(n * HEAD_W, (n + 1) * HEAD_W)
        kc = _rms_head(ck_ref[:, kv_lanes], gk_ref[...])
        vc = cv_ref[:, kv_lanes]
        kc_ref[n] = kc
        vc_ref[n] = vc
        kc16 = kc.astype(BF16)
        vc1 = _with_ones(vc.astype(BF16))
        for g in range(group):
            h = n * group + g
            lanes = slice(h * HEAD_W, (h + 1) * HEAD_W)
            qc = (_rms_head(cq_ref[:, lanes], gq_ref[...]) * scale).astype(BF16)
            oc_ref[:, lanes] = _softmax_pv([_dot_nt(qc, kc16)], [vc1]).astype(oc_ref.dtype)
            kb = _rms_head(bk_ref[:, lanes], nak_ref[...])
            vb = bv_ref[:, lanes]
            kb_ref[h] = kb
            vb_ref[h] = vb
            qb = (_rms_head(bq_ref[:, lanes], naq_ref[...]) * scale).astype(BF16)
            ob_ref[:, lanes] = _softmax_pv([_dot_nt(qb, kb.astype(BF16))],
                                           [_with_ones(vb.astype(BF16))]).astype(ob_ref.dtype)
            kd = _rms_halves(dk_ref[:, lanes], dfk_ref[...])
            vd = dv_ref[:, lanes]
            kd_ref[h, 0] = kd[:, 0:DF_DQK]
            kd_ref[h, 1] = kd[:, DF_DQK:2 * DF_DQK]
            vd_ref[h] = vd
            qd = (_rms_halves(dq_ref[:, lanes], dfq_ref[...]) * (LOG2E * DF_DQK ** -0.5)).astype(BF16)
            lo = _lane_lt(kd.shape, DF_DQK)
            s0 = _dot_nt(qd, jnp.where(lo, kd, 0.0).astype(BF16))
            s1 = _dot_nt(qd, jnp.where(lo, 0.0, kd).astype(BF16))
            od = _diff_pv(s0, s1, lam, _with_ones(vd.astype(BF16)))
            od_ref[:, lanes] = (_rms_head(od, sub_ref[...]) * (1.0 - lam_init)).astype(od_ref.dtype)


def _context_attention(proj, n_seq, seq, gains, lam_init, layer, depth, caches_prev):
    na_qn, na_kn, gqa_qn, gqa_kn, df_qn, df_kn, df_subln, df_lam = gains
    group = N_HEADS // GQA_KV_HEADS
    kvs = CTX_KV_HEADS_PER_STEP
    n_heads = kvs * group
    width = n_heads * HEAD_W

    def heads(cb):
        return pl.BlockSpec((seq, width), lambda b, n, cb=cb: (b, cb // n_heads + n))

    def kv_head(cb):
        return pl.BlockSpec((seq, kvs * HEAD_W), lambda b, n, cb=cb: (b, cb // kvs + n))

    vec = pl.BlockSpec((1, HEAD_W), lambda b, n: (0, 0))
    cache_heads = pl.BlockSpec((None, None, n_heads, seq, HEAD_W), lambda b, n: (b, layer, n, 0, 0))
    cache_kv = pl.BlockSpec((None, None, kvs, seq, HEAD_W), lambda b, n: (b, layer, n, 0, 0))
    mixed = pl.BlockSpec((seq, width), lambda b, n: (b, n))
    mixed_shape = jax.ShapeDtypeStruct((proj.shape[0], GROUP_W), BF16)
    cache4 = jax.ShapeDtypeStruct((n_seq, depth, N_HEADS, seq, HEAD_W), F32)
    cache2 = jax.ShapeDtypeStruct((n_seq, depth, GQA_KV_HEADS, seq, HEAD_W), F32)
    cache_dk = jax.ShapeDtypeStruct((n_seq, depth, N_HEADS, 2, seq, DF_DQK), F32)
    args = [proj] * 9 + [na_qn[None, :], na_kn[None, :], gqa_qn[None, :], gqa_kn[None, :],
                         jnp.tile(df_qn, 2)[None, :], jnp.tile(df_kn, 2)[None, :], df_subln[None, :], df_lam]
    assert len(args) == N_CTX_ATTN_INPUTS
    alias_specs, alias_args, aliases = _alias_kwargs(len(args), caches_prev, 3)
    return pl.pallas_call(
        functools.partial(_ctx_attn_body, lam_init=lam_init, kv_heads=kvs),
        grid=(n_seq, GQA_KV_HEADS // kvs),
        in_specs=[heads(COL_B_Q), heads(COL_B_K), heads(COL_B_V),
                  heads(COL_C_Q), kv_head(COL_C_K), kv_head(COL_C_V),
                  heads(COL_D_Q), heads(COL_D_K), heads(COL_D_V),
                  vec, vec, vec, vec, vec, vec, vec,
                  pl.BlockSpec((4, DF_DQK), lambda b, n: (0, 0))] + alias_specs,
        out_specs=[mixed, mixed, mixed, cache_heads, cache_heads, cache_kv, cache_kv,
                   pl.BlockSpec((None, None, n_heads, 2, seq, DF_DQK), lambda b, n: (b, layer, n, 0, 0, 0)),
                   cache_heads],
        out_shape=[mixed_shape, mixed_shape, mixed_shape, cache4, cache4, cache2, cache2, cache_dk, cache4],
        input_output_aliases=aliases,
        compiler_params=_params("parallel", "parallel"),
        name="context_attention",
    )(*args, *alias_args)


def _rope_tables(n_tokens, rot_dim):
    t = np.arange(n_tokens)
    row = (t // GRID_W).astype(np.float32)
    col = (t % GRID_W).astype(np.float32)
    n_freq = rot_dim // 4
    inv = (np.float32(ROPE_THETA) ** (-np.arange(n_freq, dtype=np.float32) / np.float32(n_freq))).astype(np.float32)
    ang = np.concatenate([row[:, None] * inv, col[:, None] * inv], axis=-1).astype(np.float32)
    cos, sin, zero = np.cos(ang), np.sin(ang), np.zeros_like(ang)
    reps = HEAD_W // rot_dim
    a = np.tile(np.concatenate([cos, cos], axis=-1), (1, reps))
    b = np.tile(np.concatenate([-sin, zero], axis=-1), (1, reps))
    c = np.tile(np.concatenate([zero, sin], axis=-1), (1, reps))
    return jnp.asarray(np.stack([a, b, c]), F32)


def _rope(x, tab_ref, half):
    return (x * tab_ref[0] + pltpu.roll(x, HEAD_W - half, 1) * tab_ref[1]
            + pltpu.roll(x, half, 1) * tab_ref[2])


def _na_body(q_ref, k_ref, v_ref, ck_ref, cv_ref, bias_ref, qn_ref, kn_ref, prev_ref, o_ref,
             qs_ref, ks_ref, vs_ref, *, seq):
    del prev_ref
    rows = seq // GRID_W
    n_win = WIN_ROWS * GRID_W
    qs_ref[...] = (_rms_head(q_ref[...], qn_ref[...]) * (LOG2E * HEAD_W ** -0.5)).astype(BF16)
    ks_ref[...] = _rms_head(k_ref[...], kn_ref[...]).astype(BF16)
    vs_ref[...] = _with_ones(v_ref[...].astype(BF16))
    ck = ck_ref[...].astype(BF16)
    cv = _with_ones(cv_ref[...].astype(BF16))

    def row_step(r, carry):
        start = jnp.clip(r - WIN_ROWS // 2, 0, rows - WIN_ROWS)
        win = pl.ds(pl.multiple_of(start * GRID_W, GRID_W), n_win)
        qrows = pl.ds(pl.multiple_of(r * GRID_W, GRID_W), GRID_W)
        q = qs_ref[qrows, :]
        s_win = _dot_nt(q, ks_ref[win, :]) + bias_ref[start - r + (WIN_ROWS - 1)]
        s_ctx = _dot_nt(q, ck)
        o_ref[qrows, :] = _softmax_pv([s_win, s_ctx], [vs_ref[win, :], cv]).astype(o_ref.dtype)
        return carry

    lax.fori_loop(0, rows, row_step, 0, unroll=min(rows, 32))


def _na_bias(rpb):
    col = np.arange(GRID_W)
    col_start = np.clip(col - WIN_COLS // 2, 0, GRID_W - WIN_COLS)
    col_ok = (col[None, :] >= col_start[:, None]) & (col[None, :] < col_start[:, None] + WIN_COLS)
    dc = np.clip(col[None, :] - col[:, None] + WIN_COLS - 1, 0, 2 * WIN_COLS - 2).reshape(-1)
    onehot = (np.arange(2 * WIN_COLS - 1)[:, None] == dc[None, :]).astype(np.float32)
    per_dr = jnp.einsum('hdc,cn->hdn', rpb.astype(F32), jnp.asarray(onehot), precision=lax.Precision.HIGHEST)
    per_dr = jnp.where(col_ok[None, None], LOG2E * per_dr.reshape(rpb.shape[0], -1, GRID_W, GRID_W), NEG_INF)
    wins = jnp.stack([per_dr[:, o:o + WIN_ROWS] for o in range(WIN_ROWS)], axis=1)
    return wins.transpose(0, 1, 3, 2, 4).reshape(rpb.shape[0], WIN_ROWS, GRID_W, WIN_ROWS * GRID_W)


def _latent_na(proj, row_block0, n_seq, seq, cache_k, cache_v, layer, bias, na_qn, na_kn, mixed_prev):
    past = cache_k.shape[3]

    def col(cb):
        return pl.BlockSpec((seq, HEAD_W), lambda b, h, cb=cb: (row_block0 + b, cb + h))

    cache = pl.BlockSpec((None, None, None, past, HEAD_W), lambda b, h: (b, layer, h, 0, 0))
    vec = pl.BlockSpec((1, HEAD_W), lambda b, h: (0, 0))
    return pl.pallas_call(
        functools.partial(_na_body, seq=seq),
        grid=(n_seq, N_HEADS),
        in_specs=[col(COL_B_Q), col(COL_B_K), col(COL_B_V), cache, cache,
                  pl.BlockSpec((None, WIN_ROWS, GRID_W, WIN_ROWS * GRID_W), lambda b, h: (h, 0, 0, 0)),
                  vec, vec, ANY_SPEC],
        out_specs=pl.BlockSpec((seq, HEAD_W), lambda b, h: (row_block0 + b, h)),
        out_shape=jax.ShapeDtypeStruct(mixed_prev.shape, BF16),
        input_output_aliases={8: 0},
        scratch_shapes=[pltpu.VMEM((seq, HEAD_W), BF16), pltpu.VMEM((seq, HEAD_W), BF16),
                        pltpu.VMEM((seq, 2 * HEAD_W), BF16)],
        compiler_params=_params("parallel", "parallel"),
        name="latent_neighbourhood_attention",
    )(proj, proj, proj, cache_k, cache_v, bias, na_qn[None, :], na_kn[None, :], mixed_prev)


def _gqa_body(q_ref, k_ref, v_ref, ck_ref, cv_ref, rope_ref, qn_ref, kn_ref, prev_ref, o_ref,
              qs_ref, ks_ref, vs_ref, *, seq, tq):
    del prev_ref
    group = N_HEADS // GQA_KV_HEADS
    half = HEAD_W // 2
    ks_ref[0:seq, :] = _rope(_rms_head(k_ref[...], kn_ref[...]), rope_ref, half).astype(BF16)
    ks_ref[seq:, :] = ck_ref[...].astype(BF16)
    vs_ref[0:seq, :] = _with_ones(v_ref[...].astype(BF16))
    vs_ref[seq:, :] = _with_ones(cv_ref[...].astype(BF16))
    for g in range(group):
        q = _rms_head(q_ref[:, g * HEAD_W:(g + 1) * HEAD_W], qn_ref[...]) * (LOG2E * HEAD_W ** -0.5)
        qs_ref[g] = _rope(q, rope_ref, half).astype(BF16)
    kk = ks_ref[...]
    vv = vs_ref[...]
    for g in range(group):
        def q_step(i, carry, g=g):
            qrows = pl.ds(pl.multiple_of(i * tq, tq), tq)
            o = _softmax_pv([_dot_nt(qs_ref[g, qrows, :], kk)], [vv])
            o_ref[qrows, g * HEAD_W:(g + 1) * HEAD_W] = o.astype(o_ref.dtype)
            return carry

        lax.fori_loop(0, seq // tq, q_step, 0, unroll=min(seq // tq, 8))


def _latent_gqa(proj, row_block0, n_seq, seq, cache_k, cache_v, layer, rope, gqa_qn, gqa_kn, mixed_prev):
    past = cache_k.shape[3]
    group = N_HEADS // GQA_KV_HEADS
    tq = 256
    cache = pl.BlockSpec((None, None, None, past, HEAD_W), lambda b, n: (b, layer, n, 0, 0))
    vec = pl.BlockSpec((1, HEAD_W), lambda b, n: (0, 0))
    return pl.pallas_call(
        functools.partial(_gqa_body, seq=seq, tq=tq),
        grid=(n_seq, GQA_KV_HEADS),
        in_specs=[pl.BlockSpec((seq, group * HEAD_W), lambda b, n: (row_block0 + b, COL_C_Q // group + n)),
                  pl.BlockSpec((seq, HEAD_W), lambda b, n: (row_block0 + b, COL_C_K + n)),
                  pl.BlockSpec((seq, HEAD_W), lambda b, n: (row_block0 + b, COL_C_V + n)),
                  cache, cache,
                  pl.BlockSpec((3, seq, HEAD_W), lambda b, n: (0, 0, 0)),
                  vec, vec, ANY_SPEC],
        out_specs=pl.BlockSpec((seq, group * HEAD_W), lambda b, n: (row_block0 + b, n)),
        out_shape=jax.ShapeDtypeStruct(mixed_prev.shape, BF16),
        input_output_aliases={8: 0},
        scratch_shapes=[pltpu.VMEM((group, seq, HEAD_W), BF16),
                        pltpu.VMEM((seq + past, HEAD_W), BF16),
                        pltpu.VMEM((seq + past, 2 * HEAD_W), BF16)],
        compiler_params=_params("parallel", "parallel"),
        name="latent_gqa_attention",
    )(proj, proj, proj, cache_k, cache_v, rope, gqa_qn[None, :], gqa_kn[None, :], mixed_prev)


def _diff_body(q_ref, k_ref, v_ref, ck_ref, cv_ref, rope_ref, qn_ref, kn_ref, sub_ref, lam_ref, prev_ref, o_ref,
               qs_ref, k0_ref, k1_ref, vs_ref, *, seq, tq, lam_init):
    del prev_ref
    half = DF_DQK // 2
    k = _rope(_rms_halves(k_ref[...], kn_ref[...]), rope_ref, half)
    lo = _lane_lt(k.shape, DF_DQK)
    k0_ref[0:seq, :] = jnp.where(lo, k, 0.0).astype(BF16)
    k1_ref[0:seq, :] = jnp.where(lo, 0.0, k).astype(BF16)
    ck = ck_ref[...]
    lo_c = _lane_lt(ck.shape, DF_DQK)
    k0_ref[seq:, :] = jnp.where(lo_c, ck, 0.0).astype(BF16)
    k1_ref[seq:, :] = jnp.where(lo_c, 0.0, ck).astype(BF16)
    vs_ref[0:seq, :] = _with_ones(v_ref[...].astype(BF16))
    vs_ref[seq:, :] = _with_ones(cv_ref[...].astype(BF16))
    q = _rms_halves(q_ref[...], qn_ref[...]) * (LOG2E * DF_DQK ** -0.5)
    qs_ref[...] = _rope(q, rope_ref, half).astype(BF16)
    lam = _lambda(lam_ref, lam_init)
    k0 = k0_ref[...]
    k1 = k1_ref[...]
    vv = vs_ref[...]

    def q_step(i, carry):
        qrows = pl.ds(pl.multiple_of(i * tq, tq), tq)
        qb = qs_ref[qrows, :]
        o = _diff_pv(_dot_nt(qb, k0), _dot_nt(qb, k1), lam, vv)
        o_ref[qrows, :] = (_rms_head(o, sub_ref[...]) * (1.0 - lam_init)).astype(o_ref.dtype)
        return carry

    lax.fori_loop(0, seq // tq, q_step, 0, unroll=min(seq // tq, 4))


def _latent_diff(proj, row_block0, n_seq, seq, cache_k2, cache_v, layer, rope, df_qn, df_kn, df_subln, df_lam,
                 lam_init, mixed_prev):
    past = cache_k2.shape[3]
    tq = 256

    def col(cb):
        return pl.BlockSpec((seq, HEAD_W), lambda b, h, cb=cb: (row_block0 + b, cb + h))

    cache = pl.BlockSpec((None, None, None, past, HEAD_W), lambda b, h: (b, layer, h, 0, 0))
    vec = pl.BlockSpec((1, HEAD_W), lambda b, h: (0, 0))
    kv_scratch = pltpu.VMEM((seq + past, HEAD_W), BF16)
    return pl.pallas_call(
        functools.partial(_diff_body, seq=seq, tq=tq, lam_init=lam_init),
        grid=(n_seq, N_HEADS),
        in_specs=[col(COL_D_Q), col(COL_D_K), col(COL_D_V), cache, cache,
                  pl.BlockSpec((3, seq, HEAD_W), lambda b, h: (0, 0, 0)),
                  vec, vec, vec, pl.BlockSpec((4, DF_DQK), lambda b, h: (0, 0)), ANY_SPEC],
        out_specs=pl.BlockSpec((seq, HEAD_W), lambda b, h: (row_block0 + b, h)),
        out_shape=jax.ShapeDtypeStruct(mixed_prev.shape, BF16),
        input_output_aliases={10: 0},
        scratch_shapes=[pltpu.VMEM((seq, HEAD_W), BF16), kv_scratch, kv_scratch,
                        pltpu.VMEM((seq + past, 2 * HEAD_W), BF16)],
        compiler_params=_params("parallel", "parallel"),
        name="latent_diff_attention",
    )(proj, proj, proj, cache_k2, cache_v, rope, jnp.tile(df_qn, 2)[None, :], jnp.tile(df_kn, 2)[None, :],
      df_subln[None, :], df_lam, mixed_prev)


def _first_max(vals):
    best = vals[0]
    idx = jnp.zeros(best.shape, jnp.int32)
    for i in range(1, len(vals)):
        better = vals[i] > best
        best = jnp.where(better, vals[i], best)
        idx = jnp.where(better, i, idx)
    return best, idx


def _pick(vals, idx):
    out = vals[0]
    for i in range(1, len(vals)):
        out = jnp.where(idx == i, vals[i], out)
    return out


def _outproj_body(ma_ref, mb_ref, mc_ref, md_ref, w_ref, hc_ref, hl_ref, mod_ref, n2_ref, rw_ref, rb_ref,
                  h1_ref, x2_ref, idx_ref, gate_ref, mixed_ref, *, ctx_tiles):
    for g, m_ref in enumerate((ma_ref, mb_ref, mc_ref, md_ref)):
        mixed_ref[:, g * GROUP_W:(g + 1) * GROUP_W] = m_ref[...]
    is_ctx = pl.program_id(0) < ctx_tiles
    part = mixed_ref.shape[0] // OUTPROJ_PARTS
    for p in range(OUTPROJ_PARTS):
        rows = slice(p * part, (p + 1) * part)
        y = mod_ref[2:3, :] * _dot(mixed_ref[rows, :], w_ref[...])
        h1 = jnp.where(is_ctx, hc_ref[rows, :], hl_ref[rows, :]) + y
        h1_ref[rows, :] = h1
        idx, gate = _route(h1, mod_ref, n2_ref, rw_ref, rb_ref, x2_ref.at[rows, :])
        idx_ref[:, rows] = idx
        gate_ref[:, rows] = gate


def _route(h1, mod_ref, n2_ref, rw_ref, rb_ref, x2_ref):
    x2 = _rms(h1, n2_ref[...], D_MODEL) * (1.0 + mod_ref[4:5, :]) + mod_ref[3:4, :]
    x2_ref[...] = _pack_bf16_pairs(x2)
    x_hi = x2.astype(BF16)
    x_lo = (x2 - x_hi.astype(F32)).astype(BF16)
    acc = _dot(x_hi, rw_ref[...])
    logits_tok = acc[:, 0:HEAD_W] + acc[:, HEAD_W:2 * HEAD_W] + _dot(x_lo, rw_ref[:, 0:HEAD_W])
    logits = logits_tok.T[0:N_EXPERTS, :]
    aff_all = _sigmoid(logits)
    sel_all = aff_all + rb_ref[...]
    aff = [aff_all[e:e + 1, :] for e in range(N_EXPERTS)]
    sel = [sel_all[e:e + 1, :] for e in range(N_EXPERTS)]
    neg = jnp.full(sel[0].shape, -jnp.inf, F32)
    scores = []
    for g in range(N_EXP_GROUPS):
        grp = sel[g * EXP_PER_GROUP:(g + 1) * EXP_PER_GROUP]
        m1, i1 = _first_max(grp)
        m2, _ = _first_max([jnp.where(i1 == j, neg, grp[j]) for j in range(EXP_PER_GROUP)])
        scores.append(m1 + m2)
    _, g_best = _first_max(scores)
    in_sel = [_pick([sel[g * EXP_PER_GROUP + j] for g in range(N_EXP_GROUPS)], g_best)
              for j in range(EXP_PER_GROUP)]
    in_aff = [_pick([aff[g * EXP_PER_GROUP + j] for g in range(N_EXP_GROUPS)], g_best)
              for j in range(EXP_PER_GROUP)]
    _, l1 = _first_max(in_sel)
    _, l2 = _first_max([jnp.where(l1 == j, neg, in_sel[j]) for j in range(EXP_PER_GROUP)])
    w1 = _pick(in_aff, l1)
    w2 = _pick(in_aff, l2)
    idx = jnp.concatenate([g_best * EXP_PER_GROUP + l1, g_best * EXP_PER_GROUP + l2], axis=0)
    gate = jnp.concatenate([w1 / (w1 + w2), w2 / (w1 + w2)], axis=0)
    return idx, gate


def _output_projection(mixed4, w_out_bf16, h_ctx, h_lat, mod, norm2, router_w, router_b, cond_of_tile, tm):
    d = h_ctx.shape[1]
    t = h_ctx.shape[0] + h_lat.shape[0]
    ctx_tiles = h_ctx.shape[0] // tm
    slab = pl.BlockSpec((tm, GROUP_W), lambda i: (i, 0))
    rw_hi = router_w.astype(BF16)
    rw_lo = (router_w - rw_hi.astype(F32)).astype(BF16)
    pad = ((0, 0), (0, HEAD_W - N_EXPERTS))
    router_split = jnp.concatenate([jnp.pad(rw_hi, pad), jnp.pad(rw_lo, pad)], axis=1)
    return pl.pallas_call(
        functools.partial(_outproj_body, ctx_tiles=ctx_tiles),
        grid=(t // tm,),
        in_specs=[slab, slab, slab, slab,
                  pl.BlockSpec((d, d), lambda i: (0, 0))] + _group_specs(tm, d, ctx_tiles) + [
                  pl.BlockSpec((None, 6, d), lambda i: (cond_of_tile(i), 0, 0)),
                  pl.BlockSpec((1, d), lambda i: (0, 0)),
                  pl.BlockSpec((d, 2 * HEAD_W), lambda i: (0, 0)),
                  pl.BlockSpec((N_EXPERTS, 1), lambda i: (0, 0))],
        out_specs=[pl.BlockSpec((tm, d), lambda i: (i, 0)),
                   pl.BlockSpec((tm, d // 2), lambda i: (i, 0)),
                   pl.BlockSpec((2, tm), lambda i: (0, i)),
                   pl.BlockSpec((2, tm), lambda i: (0, i))],
        out_shape=[jax.ShapeDtypeStruct((t, d), F32), jax.ShapeDtypeStruct((t, d // 2), jnp.uint32),
                   jax.ShapeDtypeStruct((2, t), jnp.int32), jax.ShapeDtypeStruct((2, t), F32)],
        scratch_shapes=[pltpu.VMEM((tm, d), BF16)],
        compiler_params=_params("parallel", vmem=VMEM_LIMIT_PROJ),
        name="out_proj_residual_router",
    )(*mixed4, w_out_bf16, h_ctx, h_lat, mod, norm2[None, :], router_split, router_b[:, None])


def _vmem_row(ref, base, u):
    return ref.at[pl.ds(base, SUBLANES), :].at[pl.ds(u, 1), :]


def _for_row_groups(n_rows, fn):
    def body(g, carry):
        base = pl.multiple_of(g * SUBLANES, SUBLANES)
        for u in range(SUBLANES):
            fn(base, u)
        return carry

    lax.fori_loop(0, n_rows // SUBLANES, body, 0)


def _rows_wait(src_hbm, dst, sem, n_rows):
    pltpu.make_async_copy(src_hbm.at[pl.ds(0, n_rows), :], dst.at[pl.ds(0, n_rows), :], sem).wait()


def _dispatch_body(dest_ref, ps_ref, pe_ref, x_ref, xs_hbm, zero_ref, sem, zsem, *, n_tok, tile):
    i = pl.program_id(0)

    def zero_copy(e):
        first = pl.multiple_of(pe_ref[e] - MOE_ROWS, MOE_ROWS)
        return pltpu.make_async_copy(zero_ref, xs_hbm.at[pl.ds(first, MOE_ROWS), :], zsem)

    @pl.when(i == 0)
    def _():
        zero_ref[...] = jnp.zeros(zero_ref.shape, zero_ref.dtype)
        for e in range(N_EXPERTS):
            @pl.when(pe_ref[e] > ps_ref[e])
            def _(e=e):
                zero_copy(e).start()
        for e in range(N_EXPERTS):
            @pl.when(pe_ref[e] > ps_ref[e])
            def _(e=e):
                zero_copy(e).wait()

    for k in range(2):
        def scatter_row(base, u, k=k):
            row = dest_ref[k * n_tok + i * tile + base + u]
            pltpu.make_async_copy(_vmem_row(x_ref, base, u), xs_hbm.at[pl.ds(row, 1), :], sem).start()

        _for_row_groups(tile, scatter_row)
    for k in range(2):
        _rows_wait(x_ref, xs_hbm, sem, tile)


def _dispatch(x2, dest, pad_start, pad_end, n_rows):
    t, d = x2.shape
    tile = next(m for m in DISPATCH_TILES if t % m == 0)
    grid_spec = pltpu.PrefetchScalarGridSpec(
        num_scalar_prefetch=3,
        grid=(t // tile,),
        in_specs=[pl.BlockSpec((tile, d), lambda i, dst, ps, pe: (i, 0))],
        out_specs=ANY_SPEC,
        scratch_shapes=[pltpu.VMEM((MOE_ROWS, d), x2.dtype), pltpu.SemaphoreType.DMA, pltpu.SemaphoreType.DMA],
    )
    return pl.pallas_call(
        functools.partial(_dispatch_body, n_tok=t, tile=tile),
        grid_spec=grid_spec,
        out_shape=jax.ShapeDtypeStruct((n_rows, d), x2.dtype),
        compiler_params=_params("arbitrary"),
        name="moe_dispatch",
    )(dest, pad_start, pad_end, x2)


def _expert_body(be_ref, nb_ref, nx_ref, x_ref, wg_hbm, wu_hbm, wd_hbm, o_ref,
                 stage_g, stage_u, stage_d, wg_bf, wu_bf, wd_bf, sem, *, layer):
    i = pl.program_id(0)

    def weight_copies(e):
        return (pltpu.make_async_copy(wg_hbm.at[layer, e], stage_g, sem.at[0]),
                pltpu.make_async_copy(wu_hbm.at[layer, e], stage_u, sem.at[1]),
                pltpu.make_async_copy(wd_hbm.at[layer, e], stage_d, sem.at[2]))

    @pl.when(i < nb_ref[0])
    def _():
        e = be_ref[i]

        @pl.when(i == 0)
        def _():
            for c in weight_copies(e):
                c.start()

        @pl.when((i == 0) | (e != be_ref[jnp.maximum(i - 1, 0)]))
        def _():
            for c in weight_copies(e):
                c.wait()
            wg_bf[...] = stage_g[...].astype(BF16)
            wu_bf[...] = stage_u[...].astype(BF16)
            wd_bf[...] = stage_d[...].astype(BF16)
            nxt = nx_ref[e]

            @pl.when(nxt < N_EXPERTS)
            def _():
                for c in weight_copies(nxt):
                    c.start()

        x = _unpack_bf16_pairs(x_ref[...]).astype(BF16)
        hdn = _silu(_dot(x, wg_bf[...])) * _dot(x, wu_bf[...])
        o_ref[...] = _pack_bf16_pairs(_dot(hdn.astype(BF16), wd_bf[...]))

    @pl.when(i >= nb_ref[0])
    def _():
        o_ref[...] = jnp.zeros(o_ref.shape, o_ref.dtype)


def _expert_blocks(xs, block_expert, n_used, next_expert, w_gate, w_up, w_down, layer):
    n_rows, half = xs.shape
    d = 2 * half
    ff = w_gate.shape[-1]
    grid_spec = pltpu.PrefetchScalarGridSpec(
        num_scalar_prefetch=3,
        grid=(n_rows // MOE_ROWS,),
        in_specs=[pl.BlockSpec((MOE_ROWS, half), lambda i, be, nb, nx: (jnp.minimum(i, nb[0] - 1), 0)),
                  ANY_SPEC, ANY_SPEC, ANY_SPEC],
        out_specs=pl.BlockSpec((MOE_ROWS, half), lambda i, be, nb, nx: (i, 0)),
        scratch_shapes=[pltpu.VMEM((d, ff), F32), pltpu.VMEM((d, ff), F32), pltpu.VMEM((ff, d), F32),
                        pltpu.VMEM((d, ff), BF16), pltpu.VMEM((d, ff), BF16), pltpu.VMEM((ff, d), BF16),
                        pltpu.SemaphoreType.DMA((3,))],
    )
    return pl.pallas_call(
        functools.partial(_expert_body, layer=layer),
        grid_spec=grid_spec,
        out_shape=jax.ShapeDtypeStruct((n_rows, half), jnp.uint32),
        compiler_params=_params("arbitrary"),
        name="moe_expert_blocks",
    )(block_expert, n_used, next_expert, xs, w_gate, w_up, w_down)


def _combine_body(dest_ref, h_ref, gate_ref, y_hbm, mod_ref, *rest, n_tok, tile, ctx_tiles):
    out_refs, (ybuf, sem) = rest[:-2], rest[-2:]
    i = pl.program_id(0)
    n_tiles = pl.num_programs(0)

    def start(blk, slot):
        for k in range(2):
            def gather_row(base, u, k=k):
                row = dest_ref[k * n_tok + blk * tile + base + u]
                pltpu.make_async_copy(y_hbm.at[pl.ds(row, 1), :], _vmem_row(ybuf.at[slot, k], base, u),
                                      sem.at[slot]).start()

            _for_row_groups(tile, gather_row)

    @pl.when(i == 0)
    def _():
        start(0, 0)

    @pl.when(i + 1 < n_tiles)
    def _():
        start(i + 1, (i + 1) % 2)

    slot = i % 2
    for k in range(2):
        _rows_wait(y_hbm, ybuf.at[slot, k], sem.at[slot], tile)
    gate = gate_ref[...]
    y0 = _unpack_bf16_pairs(ybuf[slot, 0])
    y1 = _unpack_bf16_pairs(ybuf[slot, 1])
    out = h_ref[...] + mod_ref[5:6, :] * (gate[:, 0:1] * y0 + gate[:, 1:2] * y1)
    @pl.when(i < ctx_tiles)
    def _():
        out_refs[0][...] = out

    @pl.when(i >= ctx_tiles)
    def _():
        out_refs[1][...] = out


def _combine(h1, yb, dest, gates, mod, cond_of_tile, tile, split_rows):
    t, d = h1.shape
    row_tile = pl.BlockSpec((tile, d), lambda i, dst: (i, 0))
    ctx_tiles = split_rows // tile
    out_specs = _group_specs(tile, d, ctx_tiles)
    out_shape = [jax.ShapeDtypeStruct((split_rows, d), F32), jax.ShapeDtypeStruct((t - split_rows, d), F32)]
    grid_spec = pltpu.PrefetchScalarGridSpec(
        num_scalar_prefetch=1,
        grid=(t // tile,),
        in_specs=[row_tile,
                  pl.BlockSpec((tile, 2), lambda i, dst: (i, 0)),
                  ANY_SPEC,
                  pl.BlockSpec((None, 6, d), lambda i, dst: (cond_of_tile(i), 0, 0))],
        out_specs=out_specs,
        scratch_shapes=[pltpu.VMEM((2, 2, tile, yb.shape[1]), yb.dtype), pltpu.SemaphoreType.DMA((2,))],
    )
    return pl.pallas_call(
        functools.partial(_combine_body, n_tok=t, tile=tile, ctx_tiles=ctx_tiles),
        grid_spec=grid_spec,
        out_shape=out_shape,
        compiler_params=_params("arbitrary"),
        name="moe_gated_residual",
    )(dest, h1, gates, yb, mod)


def _moe(h1, x2, idx_t, gate_t, mod, w_gate, w_up, w_down, layer, cond_of_tile, split_rows):
    t, d = h1.shape
    n = 2 * t
    experts = idx_t.reshape(n)
    onehot = (experts[:, None] == jnp.arange(N_EXPERTS, dtype=jnp.int32)[None, :]).astype(BF16)
    blocks = onehot.reshape(n // COMBINE_TILE, COMBINE_TILE, N_EXPERTS)
    tri = jnp.asarray(np.tril(np.ones((COMBINE_TILE, COMBINE_TILE), np.float32)), BF16)
    within = jnp.einsum('ij,bjk->bik', tri, blocks, preferred_element_type=F32)
    block_total = within[:, -1, :]
    block_first = jnp.cumsum(block_total, axis=0) - block_total
    counts = (block_first[-1] + block_total[-1]).astype(jnp.int32)
    before = (within + block_first[:, None, :]).reshape(n, N_EXPERTS) - 1.0
    rank = jnp.sum(before * onehot.astype(F32), axis=1).astype(jnp.int32)
    padded = (counts + MOE_ROWS - 1) // MOE_ROWS * MOE_ROWS
    pad_end = jnp.cumsum(padded).astype(jnp.int32)
    pad_start = pad_end - padded
    dest = (pad_start[experts] + rank).astype(jnp.int32)
    n_blocks = (n + N_EXPERTS * (MOE_ROWS - 1) + MOE_ROWS - 1) // MOE_ROWS
    block_first_row = jnp.arange(n_blocks, dtype=jnp.int32) * MOE_ROWS
    block_expert = jnp.minimum(jnp.sum((pad_end[None, :] <= block_first_row[:, None]).astype(jnp.int32), axis=1),
                               N_EXPERTS - 1)
    n_used = (pad_end[-1:] // MOE_ROWS).astype(jnp.int32)
    xs = _dispatch(x2, dest, pad_start, pad_end, n_blocks * MOE_ROWS)
    ids = jnp.arange(N_EXPERTS, dtype=jnp.int32)
    later_with_rows = (counts[None, :] > 0) & (ids[None, :] > ids[:, None])
    next_expert = jnp.min(jnp.where(later_with_rows, ids[None, :], N_EXPERTS), axis=1).astype(jnp.int32)
    yb = _expert_blocks(xs, block_expert, n_used, next_expert, w_gate, w_up, w_down, layer)
    return _combine(h1, yb, dest, gate_t.T, mod, cond_of_tile, COMBINE_TILE, split_rows)


def kernel(x_prompt, x_sample, cache_na_k, cache_na_v, cache_gqa_k, cache_gqa_v, cache_diff_k, cache_diff_v, state_hgrn, c, c_ctx, w_mod, b_mod, norm1, norm2, w_in, w_out, hg_lb_logits, hg_onorm, na_qn, na_kn, na_rpb, gqa_qn, gqa_kn, df_qn, df_kn, df_lam, df_subln, router_w, router_b, w_gate, w_up, w_down):
    n_ctx, ctx_len, d = x_prompt.shape
    n_lat, lat_len, _ = x_sample.shape
    depth = w_in.shape[0]
    t_ctx = n_ctx * ctx_len
    assert t_ctx % lat_len == 0 and lat_len % GRID_W == 0 and lat_len // GRID_W >= WIN_ROWS
    tm = next(m for m in (1024, 512, 256) if t_ctx % m == 0 and lat_len % m == 0)
    tm2 = min(tm, 512)
    lat_block0 = t_ctx // lat_len

    def cond_tile(tile_rows):
        def cond_of_tile(i):
            return jnp.where(i < t_ctx // tile_rows, 0, 1 + (i - t_ctx // tile_rows) // (lat_len // tile_rows))
        return cond_of_tile

    sm = jax.nn.softmax(hg_lb_logits.astype(F32), axis=0)
    lower = jnp.cumsum(sm, axis=0) - sm[0:1]
    mod_all = _modulation(jnp.concatenate([c_ctx[None, :], c], axis=0), w_mod, b_mod)
    mod_all = mod_all.reshape(depth, 1 + n_lat, 6, d)
    hgrn_consts = _hgrn_constants(HGRN_CHUNK)
    rope_c = _rope_tables(lat_len, HEAD_W)
    rope_d = _rope_tables(lat_len, DF_DQK)
    past = cache_diff_k.shape[4]
    cache_diff_k2 = cache_diff_k.transpose(0, 1, 2, 4, 3, 5).reshape(n_lat, depth, N_HEADS, past, HEAD_W)

    h_ctx, h_lat = x_prompt.reshape(t_ctx, d), x_sample.reshape(n_lat * lat_len, d)
    caches, states = [], None
    for layer in range(depth):
        mod = mod_all[layer]
        lam_init = 0.8 - 0.6 * math.exp(-0.3 * layer)
        proj = _input_projection(h_ctx, h_lat, mod, norm1[layer], w_in[layer].astype(BF16), cond_tile(tm), tm)
        mix_a, states = _hgrn(proj, 0, n_ctx, ctx_len, lower[layer], hg_onorm[layer], hgrn_consts, None, None,
                              states, layer, depth)
        gains = (na_qn[layer], na_kn[layer], gqa_qn[layer], gqa_kn[layer], df_qn[layer], df_kn[layer],
                 df_subln[layer], df_lam[layer])
        mix_b, mix_c, mix_d, *caches = _context_attention(proj, n_ctx, ctx_len, gains, lam_init, layer, depth,
                                                          caches)
        mix_a, _ = _hgrn(proj, lat_block0, n_lat, lat_len, lower[layer], hg_onorm[layer], hgrn_consts,
                         state_hgrn, mix_a, None, layer, depth)
        mix_b = _latent_na(proj, lat_block0, n_lat, lat_len, cache_na_k, cache_na_v, layer,
                           _na_bias(na_rpb[layer]), na_qn[layer], na_kn[layer], mix_b)
        mix_c = _latent_gqa(proj, lat_block0, n_lat, lat_len, cache_gqa_k, cache_gqa_v, layer, rope_c,
                            gqa_qn[layer], gqa_kn[layer], mix_c)
        mix_d = _latent_diff(proj, lat_block0, n_lat, lat_len, cache_diff_k2, cache_diff_v, layer, rope_d,
                             df_qn[layer], df_kn[layer], df_subln[layer], df_lam[layer], lam_init, mix_d)
        h1, x2, idx_t, gate_t = _output_projection((mix_a, mix_b, mix_c, mix_d), w_out[layer].astype(BF16), h_ctx,
                                                   h_lat, mod, norm2[layer], router_w, router_b, cond_tile(tm2), tm2)
        h_ctx, h_lat = _moe(h1, x2, idx_t, gate_t, mod, w_gate, w_up, w_down, layer, cond_tile(COMBINE_TILE), t_ctx)
    y_prompt = h_ctx.reshape(n_ctx, ctx_len, d)
    y_sample = h_lat.reshape(n_lat, lat_len, d)
    return (y_prompt, y_sample, *caches, states)
```

```python
import functools
import math

import numpy as np
import jax
import jax.numpy as jnp
from jax import lax
from jax.experimental import pallas as pl
from jax.experimental.pallas import tpu as pltpu

D_MODEL = 2048
GRID_W = 64
GROUP_W = D_MODEL // 4
N_HEADS = 4
HEAD_W = GROUP_W // N_HEADS
SUBLANES = 8
GQA_KV_HEADS = 2
DF_DQK = HEAD_W // 2
WIN_ROWS = 8
WIN_COLS = 16
N_EXPERTS = 16
N_EXP_GROUPS = 4
EXP_PER_GROUP = N_EXPERTS // N_EXP_GROUPS
EXPERT_FF = D_MODEL // 4
ROPE_THETA = 10000.0
EPS = 1e-6
NEG_INF = -1e30
LOG2E = math.log2(math.e)
IN_WIDTH = 13 * GROUP_W

COL_A_Q, COL_A_FF, COL_A_FB, COL_A_I, COL_A_G = 0, 4, 8, 12, 16
COL_B_Q, COL_B_K, COL_B_V = 20, 24, 28
COL_C_Q, COL_C_K, COL_C_V = 32, 36, 38
COL_D_Q, COL_D_K, COL_D_V = 40, 44, 48

HGRN_CHUNK = 128
HGRN_UNROLL = 4
HGRN_INPUT_VMEM = 24 * 1024 * 1024
CTX_KV_HEADS_PER_STEP = 1
MOE_ROWS = 256
DISPATCH_TILES = (1024, 512, 256)
COMBINE_TILE = 256
N_DMA_PRIORITIES = 2
VMEM_LIMIT = 48 * 1024 * 1024
VMEM_LIMIT_PROJ = 56 * 1024 * 1024

F32 = jnp.float32
BF16 = jnp.bfloat16
ANY_SPEC = pl.BlockSpec(memory_space=pl.ANY)


def _params(*sem, vmem=VMEM_LIMIT):
    return pltpu.CompilerParams(dimension_semantics=sem, vmem_limit_bytes=vmem)


def _sigmoid(x):
    return 1.0 / (1.0 + jnp.exp(-x))


def _silu(x):
    return x * _sigmoid(x)


def _rms(x, gain, n):
    return x * lax.rsqrt(jnp.sum(x * x, axis=-1, keepdims=True) * (1.0 / n) + EPS) * gain


def _dot(a, b):
    return jnp.dot(a, b, preferred_element_type=F32)


def _dot_nt(a, b):
    return lax.dot_general(a, b, (((1,), (1,)), ((), ())), preferred_element_type=F32)


def _dot_tn(a, b):
    return lax.dot_general(a, b, (((0,), (0,)), ((), ())), preferred_element_type=F32)


def _pack_bf16_pairs(x):
    k = x.shape[1] // 2
    lo = lax.bitcast_convert_type(x[:, :k].astype(BF16).astype(F32), jnp.uint32) >> 16
    hi = lax.bitcast_convert_type(x[:, k:].astype(BF16).astype(F32), jnp.uint32)
    return hi | lo


def _unpack_bf16_pairs(w):
    lo = lax.bitcast_convert_type(w << 16, F32)
    hi = lax.bitcast_convert_type(w & jnp.uint32(0xFFFF0000), F32)
    return jnp.concatenate([lo, hi], axis=1)


def _aligned(x, m):
    return x if isinstance(x, int) else pl.multiple_of(x, m)


def _alias_kwargs(n_inputs, prev, first_out):
    return ([ANY_SPEC] * len(prev), list(prev), {n_inputs + k: first_out + k for k in range(len(prev))})


def _mod_body(cond_ref, w_ref, b_ref, o_ref):
    w = w_ref[...]
    for c in range(cond_ref.shape[0]):
        s = _silu(cond_ref[c])
        o_ref[c:c + 1, :] = jnp.sum(w * s, axis=0, keepdims=True) + b_ref[...]


def _modulation(cond, w_mod, b_mod):
    depth, d, n6 = w_mod.shape
    nc = cond.shape[0]
    tn = 1024
    return pl.pallas_call(
        _mod_body,
        grid=(depth, n6 // tn),
        in_specs=[pl.BlockSpec((nc, d, 1), lambda l, j: (0, 0, 0)),
                  pl.BlockSpec((None, d, tn), lambda l, j: (l, 0, j)),
                  pl.BlockSpec((None, 1, tn), lambda l, j: (l, 0, j))],
        out_specs=pl.BlockSpec((None, nc, tn), lambda l, j: (l, 0, j)),
        out_shape=jax.ShapeDtypeStruct((depth, nc, n6), F32),
        compiler_params=_params("parallel", "parallel"),
        name="adaln_modulation",
    )(cond[:, :, None], w_mod, b_mod[:, None, :])


def _group_specs(tile, d, ctx_tiles):
    return [pl.BlockSpec((tile, d), lambda i, *_: (jnp.minimum(i, ctx_tiles - 1), 0)),
            pl.BlockSpec((tile, d), lambda i, *_: (jnp.maximum(i - ctx_tiles, 0), 0))]


INPROJ_NORM_CHUNKS = 8


def _inproj_body(h0_ref, mod0_ref, hc_ref, hl_ref, mod_ref, n1_ref, w_ref, o_ref, xn_ref, *, ctx_tiles, chunk):
    i = pl.program_id(0)
    j = pl.program_id(1)

    def normalise(h, mod):
        return (_rms(h, n1_ref[...], D_MODEL) * (1.0 + mod[1:2, :]) + mod[0:1, :]).astype(BF16)

    @pl.when((i == 0) & (j == 0))
    def _():
        xn_ref[0] = normalise(h0_ref[...], mod0_ref[...])

    nxt = i + 1
    rows = pl.ds(pl.multiple_of(jnp.minimum(j, INPROJ_NORM_CHUNKS - 1) * chunk, chunk), chunk)
    xn_ref[nxt % 2, rows, :] = normalise(jnp.where(nxt < ctx_tiles, hc_ref[...], hl_ref[...]), mod_ref[...])
    o_ref[...] = _dot(xn_ref[i % 2], w_ref[...])


def _input_projection(h_ctx, h_lat, mod, norm1, w_in_bf16, cond_of_tile, tm):
    d = h_ctx.shape[1]
    t = h_ctx.shape[0] + h_lat.shape[0]
    n_tiles, ctx_tiles, lat_tiles = t // tm, h_ctx.shape[0] // tm, h_lat.shape[0] // tm
    n = w_in_bf16.shape[1]
    tn = 512
    pieces = INPROJ_NORM_CHUNKS
    chunk = tm // pieces
    assert n // tn >= pieces and ctx_tiles >= 1

    def piece(j):
        return jnp.minimum(j, pieces - 1)

    def nxt(i):
        return jnp.minimum(i + 1, n_tiles - 1)

    ctx_chunk = lambda i, j: (jnp.where(nxt(i) < ctx_tiles, nxt(i) * pieces + piece(j), ctx_tiles * pieces - 1), 0)
    lat_chunk = lambda i, j: (jnp.where(nxt(i) >= ctx_tiles, (nxt(i) - ctx_tiles) * pieces + piece(j), 0), 0)
    return pl.pallas_call(
        functools.partial(_inproj_body, ctx_tiles=ctx_tiles, chunk=chunk),
        grid=(n_tiles, n // tn),
        in_specs=[pl.BlockSpec((tm, d), lambda i, j: (0, 0)),
                  pl.BlockSpec((None, 6, d), lambda i, j: (cond_of_tile(0), 0, 0)),
                  pl.BlockSpec((chunk, d), ctx_chunk),
                  pl.BlockSpec((chunk, d), lat_chunk),
                  pl.BlockSpec((None, 6, d), lambda i, j: (cond_of_tile(nxt(i)), 0, 0)),
                  pl.BlockSpec((1, d), lambda i, j: (0, 0)),
                  pl.BlockSpec((d, tn), lambda i, j: (0, j))],
        out_specs=pl.BlockSpec((tm, tn), lambda i, j: (i, j)),
        out_shape=jax.ShapeDtypeStruct((t, n), F32),
        scratch_shapes=[pltpu.VMEM((2, tm, d), BF16)],
        compiler_params=_params("arbitrary", "arbitrary"),
        name="norm_modulate_in_proj",
    )(h_ctx, mod, h_ctx, h_lat, mod, norm1[None, :], w_in_bf16)


def _hgrn_constants(c):
    nl = int(math.log2(c))
    idx = np.arange(c)
    e = np.zeros((nl + 2, c, c), np.float32)
    m = np.zeros((nl + 1, c, c), np.float32)
    e[0] = idx[None, :] <= idx[:, None]
    e[1] = idx[None, :] > idx[:, None]
    m[0] = np.eye(c)
    for li in range(nl):
        s = c >> (li + 1)
        parent = idx // (2 * s)
        right = (idx % (2 * s)) >= s
        ref = parent * 2 * s + s - 1
        for i in range(c):
            if right[i]:
                e[2 + li, i, ref[i] + 1:i + 1] = 1.0
            else:
                e[2 + li, i, i + 1:ref[i] + 1] = 1.0
        m[1 + li] = right[:, None] & ~right[None, :] & (parent[:, None] == parent[None, :])
    keep = [0, 1] + [2 + li for li in range(nl) if (c >> (li + 1)) < SUBLANES]
    e = e[keep]
    e2 = np.stack([e, e[:, ::-1, ::-1]]).reshape(2, len(keep) * c, c)
    m2 = np.stack([m, m[:, ::-1, ::-1]])
    return jnp.asarray(e2, BF16), jnp.asarray(m2, F32)


def _hgrn_body(*refs, seq, chunk, unroll, heads, has_s0, emit_state, n_alias):
    q_ref, ff_ref, fb_ref, i_ref, g_ref, lb_ref, on_ref, e_ref, m_ref = refs[:9]
    pos = 9
    s0_ref = None
    if has_s0:
        s0_ref = refs[pos]
        pos += 1
    pos += n_alias
    o_ref = refs[pos]
    pos += 1
    if emit_state:
        st_ref = refs[pos]
        pos += 1
    of_ref, ob_ref = refs[pos], refs[pos + 1]
    c = chunk
    n_chunks = seq // c
    assert seq % c == 0 and n_chunks % unroll == 0
    n_levels = m_ref.shape[1] - 1
    gate_refs = (ff_ref, fb_ref)
    out_refs = (of_ref, ob_ref)

    def chunk_step(c0, d, hh, st):
        rows = pl.ds(c0, c)
        lanes = slice(hh * HEAD_W, (hh + 1) * HEAD_W)
        lb = lb_ref[d:d + 1, lanes]
        f = lb + (1.0 - lb) * _sigmoid(gate_refs[d][rows, lanes])
        g = jnp.log2(f)
        k = 1.0 - f
        q = _silu(q_ref[rows, lanes])
        v = i_ref[rows, lanes].astype(BF16)
        g_hi = g.astype(BF16)
        g_lo = (g - g_hi.astype(F32)).astype(BF16)
        g2 = _dot(e_ref[d], jnp.concatenate([g_hi, g_lo], axis=1))
        gsum = g2[:, 0:HEAD_W] + g2[:, HEAD_W:2 * HEAD_W]
        cum = gsum[0:c]
        x_cum = jnp.exp2(cum)
        x_tail = jnp.exp2(gsum[c:2 * c])
        row = lax.broadcasted_iota(jnp.int32, (c, HEAD_W), 0)
        s = m_ref[d, 0] * _dot_nt(q.astype(BF16), k.astype(BF16))
        n_matmul_levels = 0
        for lv in range(n_levels):
            half = c >> (lv + 1)
            if half >= SUBLANES:
                ref_row = half - 1 if d == 0 else half
                ref = jnp.concatenate(
                    [jnp.broadcast_to(cum[b * 2 * half + ref_row:b * 2 * half + ref_row + 1, :], (2 * half, HEAD_W))
                     for b in range(c // (2 * half))], axis=0)
                near = ((row % (2 * half)) >= half) == (d == 0)
                x_l = jnp.exp2(jnp.where(near, cum - ref, ref - cum))
            else:
                x_l = jnp.exp2(gsum[(2 + n_matmul_levels) * c:(3 + n_matmul_levels) * c])
                n_matmul_levels += 1
            s = s + m_ref[d, 1 + lv] * _dot_nt((q * x_l).astype(BF16), (k * x_l).astype(BF16))
        o = _dot_nt((q * x_cum).astype(BF16), st.astype(BF16)) + _dot(s.astype(BF16), v)
        out_refs[d][rows, lanes] = o
        total = x_cum[c - 1:c, :] if d == 0 else x_cum[0:1, :]
        return st * total + _dot_tn(v, (k * x_tail).astype(BF16))

    if has_s0:
        states0 = tuple(s0_ref[d, hh].T for hh in range(heads) for d in range(2))
    else:
        states0 = tuple(jnp.zeros((HEAD_W, HEAD_W), F32) for _ in range(2 * heads))

    def loop(t, states):
        states = list(states)
        for u in range(unroll):
            j = t * unroll + u
            for hh in range(heads):
                states[2 * hh] = chunk_step(_aligned(j * c, c), 0, hh, states[2 * hh])
                states[2 * hh + 1] = chunk_step(_aligned((n_chunks - 1 - j) * c, c), 1, hh, states[2 * hh + 1])
        return tuple(states)

    if n_chunks == unroll:
        states = loop(0, states0)
    else:
        states = lax.fori_loop(0, n_chunks // unroll, loop, states0)
    for hh in range(heads):
        lanes = slice(hh * HEAD_W, (hh + 1) * HEAD_W)
        o = of_ref[:, lanes] + ob_ref[:, lanes]
        o_ref[:, lanes] = (_rms(o, on_ref[...], HEAD_W) * _silu(g_ref[:, lanes])).astype(o_ref.dtype)
        if emit_state:
            st_ref[0, hh] = states[2 * hh].T
            st_ref[1, hh] = states[2 * hh + 1].T


def _hgrn(proj, row_block0, n_seq, seq, lower, onorm, consts, s0, mixed_prev, state_prev, layer, depth):
    e_mat, masks = consts
    latent = s0 is not None
    hps = next(n for n in (4, 2, 1) if 5 * 2 * seq * n * HEAD_W * 4 <= HGRN_INPUT_VMEM)
    width = hps * HEAD_W

    def col(cb):
        return pl.BlockSpec((seq, width), lambda b, h, cb=cb: (row_block0 + b, cb // hps + h))

    state_spec = pl.BlockSpec((None, None, 2, hps, HEAD_W, HEAD_W), lambda b, h: (b, layer, 0, h, 0, 0))
    in_specs = [col(COL_A_Q), col(COL_A_FF), col(COL_A_FB), col(COL_A_I), col(COL_A_G),
                pl.BlockSpec((2, width), lambda b, h: (0, h)),
                pl.BlockSpec((1, HEAD_W), lambda b, h: (0, 0)),
                pl.BlockSpec(e_mat.shape, lambda b, h: (0, 0, 0)),
                pl.BlockSpec(masks.shape, lambda b, h: (0, 0, 0, 0))]
    args = [proj, proj, proj, proj, proj, lower, onorm[None, :], e_mat, masks]
    if latent:
        in_specs.append(state_spec)
        args.append(s0)
        prev = [mixed_prev]
    else:
        prev = [] if state_prev is None else [state_prev]
    alias_specs, alias_args, aliases = _alias_kwargs(len(args), prev, 0 if latent else 1)
    out_specs = [pl.BlockSpec((seq, width), lambda b, h: (row_block0 + b, h))]
    out_shape = [jax.ShapeDtypeStruct((proj.shape[0], GROUP_W), BF16)]
    if not latent:
        out_specs.append(state_spec)
        out_shape.append(jax.ShapeDtypeStruct((n_seq, depth, 2, N_HEADS, HEAD_W, HEAD_W), F32))
    res = pl.pallas_call(
        functools.partial(_hgrn_body, seq=seq, chunk=HGRN_CHUNK, unroll=min(HGRN_UNROLL, seq // HGRN_CHUNK),
                          heads=hps, has_s0=latent,
                          emit_state=not latent, n_alias=len(prev)),
        grid=(n_seq, N_HEADS // hps),
        in_specs=in_specs + alias_specs, out_specs=out_specs, out_shape=out_shape,
        input_output_aliases=aliases,
        scratch_shapes=[pltpu.VMEM((seq, width), F32), pltpu.VMEM((seq, width), F32)],
        compiler_params=_params("parallel", "parallel"),
        name="hgrn2_latent" if latent else "hgrn2_context",
    )(*args, *alias_args)
    return (res[0], None) if latent else res


def _with_ones(v):
    return jnp.concatenate([v, jnp.ones_like(v)], axis=1)


def _softmax_pv(scores, values1):
    mx = functools.reduce(jnp.maximum, [jnp.max(s, axis=-1, keepdims=True) for s in scores])
    acc = functools.reduce(lambda a, b: a + b,
                           [_dot(jnp.exp2(s - mx).astype(BF16), v) for s, v in zip(scores, values1)])
    return acc[:, 0:HEAD_W] / acc[:, HEAD_W:HEAD_W + 1]


def _diff_pv(s0, s1, lam, values1):
    return _softmax_pv([s0], [values1]) - lam * _softmax_pv([s1], [values1])


def _lane_lt(shape, n):
    return lax.broadcasted_iota(jnp.int32, shape, len(shape) - 1) < n


def _rms_head(x, gain):
    return _rms(x, gain, HEAD_W)


def _rms_halves(x, gain2):
    lo = _lane_lt(x.shape, DF_DQK)
    sq = x * x
    ss_lo = jnp.sum(jnp.where(lo, sq, 0.0), axis=-1, keepdims=True)
    ss_hi = jnp.sum(sq, axis=-1, keepdims=True) - ss_lo
    inv = jnp.where(lo, lax.rsqrt(ss_lo * (1.0 / DF_DQK) + EPS), lax.rsqrt(ss_hi * (1.0 / DF_DQK) + EPS))
    return x * inv * gain2


def _lambda(lam_ref, lam_init):
    l = lam_ref[...]
    return (jnp.exp(jnp.sum(l[0:1] * l[1:2], axis=-1, keepdims=True))
            - jnp.exp(jnp.sum(l[2:3] * l[3:4], axis=-1, keepdims=True)) + lam_init)


N_CTX_ATTN_INPUTS = 17


def _ctx_attn_body(*refs, lam_init, kv_heads):
    (bq_ref, bk_ref, bv_ref, cq_ref, ck_ref, cv_ref, dq_ref, dk_ref, dv_ref,
     naq_ref, nak_ref, gq_ref, gk_ref, dfq_ref, dfk_ref, sub_ref, lam_ref) = refs[:N_CTX_ATTN_INPUTS]
    ob_ref, oc_ref, od_ref, kb_ref, vb_ref, kc_ref, vc_ref, kd_ref, vd_ref = refs[-9:]
    group = N_HEADS // GQA_KV_HEADS
    scale = LOG2E * HEAD_W ** -0.5
    lam = _lambda(lam_ref, lam_init)
    for n in range(kv_heads):
        kv_lanes = slice(n * HEAD_W, (n + 1) * HEAD_W)
        kc = _rms_head(ck_ref[:, kv_lanes], gk_ref[...])
        vc = cv_ref[:, kv_lanes]
        kc_ref[n] = kc
        vc_ref[n] = vc
        kc16 = kc.astype(BF16)
        vc1 = _with_ones(vc.astype(BF16))
        for g in range(group):
            h = n * group + g
            lanes = slice(h * HEAD_W, (h + 1) * HEAD_W)
            qc = (_rms_head(cq_ref[:, lanes], gq_ref[...]) * scale).astype(BF16)
            oc_ref[:, lanes] = _softmax_pv([_dot_nt(qc, kc16)], [vc1]).astype(oc_ref.dtype)
            kb = _rms_head(bk_ref[:, lanes], nak_ref[...])
            vb = bv_ref[:, lanes]
            kb_ref[h] = kb
            vb_ref[h] = vb
            qb = (_rms_head(bq_ref[:, lanes], naq_ref[...]) * scale).astype(BF16)
            ob_ref[:, lanes] = _softmax_pv([_dot_nt(qb, kb.astype(BF16))],
                                           [_with_ones(vb.astype(BF16))]).astype(ob_ref.dtype)
            kd = _rms_halves(dk_ref[:, lanes], dfk_ref[...])
            vd = dv_ref[:, lanes]
            kd_ref[h, 0] = kd[:, 0:DF_DQK]
            kd_ref[h, 1] = kd[:, DF_DQK:2 * DF_DQK]
            vd_ref[h] = vd
            qd = (_rms_halves(dq_ref[:, lanes], dfq_ref[...]) * (LOG2E * DF_DQK ** -0.5)).astype(BF16)
            lo = _lane_lt(kd.shape, DF_DQK)
            s0 = _dot_nt(qd, jnp.where(lo, kd, 0.0).astype(BF16))
            s1 = _dot_nt(qd, jnp.where(lo, 0.0, kd).astype(BF16))
            od = _diff_pv(s0, s1, lam, _with_ones(vd.astype(BF16)))
            od_ref[:, lanes] = (_rms_head(od, sub_ref[...]) * (1.0 - lam_init)).astype(od_ref.dtype)


def _context_attention(proj, n_seq, seq, gains, lam_init, layer, depth, caches_prev):
    na_qn, na_kn, gqa_qn, gqa_kn, df_qn, df_kn, df_subln, df_lam = gains
    group = N_HEADS // GQA_KV_HEADS
    kvs = CTX_KV_HEADS_PER_STEP
    n_heads = kvs * group
    width = n_heads * HEAD_W

    def heads(cb):
        return pl.BlockSpec((seq, width), lambda b, n, cb=cb: (b, cb // n_heads + n))

    def kv_head(cb):
        return pl.BlockSpec((seq, kvs * HEAD_W), lambda b, n, cb=cb: (b, cb // kvs + n))

    vec = pl.BlockSpec((1, HEAD_W), lambda b, n: (0, 0))
    cache_heads = pl.BlockSpec((None, None, n_heads, seq, HEAD_W), lambda b, n: (b, layer, n, 0, 0))
    cache_kv = pl.BlockSpec((None, None, kvs, seq, HEAD_W), lambda b, n: (b, layer, n, 0, 0))
    mixed = pl.BlockSpec((seq, width), lambda b, n: (b, n))
    mixed_shape = jax.ShapeDtypeStruct((proj.shape[0], GROUP_W), BF16)
    cache4 = jax.ShapeDtypeStruct((n_seq, depth, N_HEADS, seq, HEAD_W), F32)
    cache2 = jax.ShapeDtypeStruct((n_seq, depth, GQA_KV_HEADS, seq, HEAD_W), F32)
    cache_dk = jax.ShapeDtypeStruct((n_seq, depth, N_HEADS, 2, seq, DF_DQK), F32)
    args = [proj] * 9 + [na_qn[None, :], na_kn[None, :], gqa_qn[None, :], gqa_kn[None, :],
                         jnp.tile(df_qn, 2)[None, :], jnp.tile(df_kn, 2)[None, :], df_subln[None, :], df_lam]
    assert len(args) == N_CTX_ATTN_INPUTS
    alias_specs, alias_args, aliases = _alias_kwargs(len(args), caches_prev, 3)
    return pl.pallas_call(
        functools.partial(_ctx_attn_body, lam_init=lam_init, kv_heads=kvs),
        grid=(n_seq, GQA_KV_HEADS // kvs),
        in_specs=[heads(COL_B_Q), heads(COL_B_K), heads(COL_B_V),
                  heads(COL_C_Q), kv_head(COL_C_K), kv_head(COL_C_V),
                  heads(COL_D_Q), heads(COL_D_K), heads(COL_D_V),
                  vec, vec, vec, vec, vec, vec, vec,
                  pl.BlockSpec((4, DF_DQK), lambda b, n: (0, 0))] + alias_specs,
        out_specs=[mixed, mixed, mixed, cache_heads, cache_heads, cache_kv, cache_kv,
                   pl.BlockSpec((None, None, n_heads, 2, seq, DF_DQK), lambda b, n: (b, layer, n, 0, 0, 0)),
                   cache_heads],
        out_shape=[mixed_shape, mixed_shape, mixed_shape, cache4, cache4, cache2, cache2, cache_dk, cache4],
        input_output_aliases=aliases,
        compiler_params=_params("parallel", "parallel"),
        name="context_attention",
    )(*args, *alias_args)


def _rope_tables(n_tokens, rot_dim):
    t = np.arange(n_tokens)
    row = (t // GRID_W).astype(np.float32)
    col = (t % GRID_W).astype(np.float32)
    n_freq = rot_dim // 4
    inv = (np.float32(ROPE_THETA) ** (-np.arange(n_freq, dtype=np.float32) / np.float32(n_freq))).astype(np.float32)
    ang = np.concatenate([row[:, None] * inv, col[:, None] * inv], axis=-1).astype(np.float32)
    cos, sin, zero = np.cos(ang), np.sin(ang), np.zeros_like(ang)
    reps = HEAD_W // rot_dim
    a = np.tile(np.concatenate([cos, cos], axis=-1), (1, reps))
    b = np.tile(np.concatenate([-sin, zero], axis=-1), (1, reps))
    c = np.tile(np.concatenate([zero, sin], axis=-1), (1, reps))
    return jnp.asarray(np.stack([a, b, c]), F32)


def _rope(x, tab_ref, half):
    return (x * tab_ref[0] + pltpu.roll(x, HEAD_W - half, 1) * tab_ref[1]
            + pltpu.roll(x, half, 1) * tab_ref[2])


def _na_body(q_ref, k_ref, v_ref, ck_ref, cv_ref, bias_ref, qn_ref, kn_ref, prev_ref, o_ref,
             qs_ref, ks_ref, vs_ref, *, seq):
    del prev_ref
    rows = seq // GRID_W
    n_win = WIN_ROWS * GRID_W
    qs_ref[...] = (_rms_head(q_ref[...], qn_ref[...]) * (LOG2E * HEAD_W ** -0.5)).astype(BF16)
    ks_ref[...] = _rms_head(k_ref[...], kn_ref[...]).astype(BF16)
    vs_ref[...] = _with_ones(v_ref[...].astype(BF16))
    ck = ck_ref[...].astype(BF16)
    cv = _with_ones(cv_ref[...].astype(BF16))

    def row_step(r, carry):
        start = jnp.clip(r - WIN_ROWS // 2, 0, rows - WIN_ROWS)
        win = pl.ds(pl.multiple_of(start * GRID_W, GRID_W), n_win)
        qrows = pl.ds(pl.multiple_of(r * GRID_W, GRID_W), GRID_W)
        q = qs_ref[qrows, :]
        s_win = _dot_nt(q, ks_ref[win, :]) + bias_ref[start - r + (WIN_ROWS - 1)]
        s_ctx = _dot_nt(q, ck)
        o_ref[qrows, :] = _softmax_pv([s_win, s_ctx], [vs_ref[win, :], cv]).astype(o_ref.dtype)
        return carry

    lax.fori_loop(0, rows, row_step, 0, unroll=min(rows, 32))


def _na_bias(rpb):
    col = np.arange(GRID_W)
    col_start = np.clip(col - WIN_COLS // 2, 0, GRID_W - WIN_COLS)
    col_ok = (col[None, :] >= col_start[:, None]) & (col[None, :] < col_start[:, None] + WIN_COLS)
    dc = np.clip(col[None, :] - col[:, None] + WIN_COLS - 1, 0, 2 * WIN_COLS - 2).reshape(-1)
    onehot = (np.arange(2 * WIN_COLS - 1)[:, None] == dc[None, :]).astype(np.float32)
    per_dr = jnp.einsum('hdc,cn->hdn', rpb.astype(F32), jnp.asarray(onehot), precision=lax.Precision.HIGHEST)
    per_dr = jnp.where(col_ok[None, None], LOG2E * per_dr.reshape(rpb.shape[0], -1, GRID_W, GRID_W), NEG_INF)
    wins = jnp.stack([per_dr[:, o:o + WIN_ROWS] for o in range(WIN_ROWS)], axis=1)
    return wins.transpose(0, 1, 3, 2, 4).reshape(rpb.shape[0], WIN_ROWS, GRID_W, WIN_ROWS * GRID_W)


def _latent_na(proj, row_block0, n_seq, seq, cache_k, cache_v, layer, bias, na_qn, na_kn, mixed_prev):
    past = cache_k.shape[3]

    def col(cb):
        return pl.BlockSpec((seq, HEAD_W), lambda b, h, cb=cb: (row_block0 + b, cb + h))

    cache = pl.BlockSpec((None, None, None, past, HEAD_W), lambda b, h: (b, layer, h, 0, 0))
    vec = pl.BlockSpec((1, HEAD_W), lambda b, h: (0, 0))
    return pl.pallas_call(
        functools.partial(_na_body, seq=seq),
        grid=(n_seq, N_HEADS),
        in_specs=[col(COL_B_Q), col(COL_B_K), col(COL_B_V), cache, cache,
                  pl.BlockSpec((None, WIN_ROWS, GRID_W, WIN_ROWS * GRID_W), lambda b, h: (h, 0, 0, 0)),
                  vec, vec, ANY_SPEC],
        out_specs=pl.BlockSpec((seq, HEAD_W), lambda b, h: (row_block0 + b, h)),
        out_shape=jax.ShapeDtypeStruct(mixed_prev.shape, BF16),
        input_output_aliases={8: 0},
        scratch_shapes=[pltpu.VMEM((seq, HEAD_W), BF16), pltpu.VMEM((seq, HEAD_W), BF16),
                        pltpu.VMEM((seq, 2 * HEAD_W), BF16)],
        compiler_params=_params("parallel", "parallel"),
        name="latent_neighbourhood_attention",
    )(proj, proj, proj, cache_k, cache_v, bias, na_qn[None, :], na_kn[None, :], mixed_prev)


def _gqa_body(q_ref, k_ref, v_ref, ck_ref, cv_ref, rope_ref, qn_ref, kn_ref, prev_ref, o_ref,
              qs_ref, ks_ref, vs_ref, *, seq, tq):
    del prev_ref
    group = N_HEADS // GQA_KV_HEADS
    half = HEAD_W // 2
    ks_ref[0:seq, :] = _rope(_rms_head(k_ref[...], kn_ref[...]), rope_ref, half).astype(BF16)
    ks_ref[seq:, :] = ck_ref[...].astype(BF16)
    vs_ref[0:seq, :] = _with_ones(v_ref[...].astype(BF16))
    vs_ref[seq:, :] = _with_ones(cv_ref[...].astype(BF16))
    for g in range(group):
        q = _rms_head(q_ref[:, g * HEAD_W:(g + 1) * HEAD_W], qn_ref[...]) * (LOG2E * HEAD_W ** -0.5)
        qs_ref[g] = _rope(q, rope_ref, half).astype(BF16)
    kk = ks_ref[...]
    vv = vs_ref[...]
    for g in range(group):
        def q_step(i, carry, g=g):
            qrows = pl.ds(pl.multiple_of(i * tq, tq), tq)
            o = _softmax_pv([_dot_nt(qs_ref[g, qrows, :], kk)], [vv])
            o_ref[qrows, g * HEAD_W:(g + 1) * HEAD_W] = o.astype(o_ref.dtype)
            return carry

        lax.fori_loop(0, seq // tq, q_step, 0, unroll=min(seq // tq, 8))


def _latent_gqa(proj, row_block0, n_seq, seq, cache_k, cache_v, layer, rope, gqa_qn, gqa_kn, mixed_prev):
    past = cache_k.shape[3]
    group = N_HEADS // GQA_KV_HEADS
    tq = 256
    cache = pl.BlockSpec((None, None, None, past, HEAD_W), lambda b, n: (b, layer, n, 0, 0))
    vec = pl.BlockSpec((1, HEAD_W), lambda b, n: (0, 0))
    return pl.pallas_call(
        functools.partial(_gqa_body, seq=seq, tq=tq),
        grid=(n_seq, GQA_KV_HEADS),
        in_specs=[pl.BlockSpec((seq, group * HEAD_W), lambda b, n: (row_block0 + b, COL_C_Q // group + n)),
                  pl.BlockSpec((seq, HEAD_W), lambda b, n: (row_block0 + b, COL_C_K + n)),
                  pl.BlockSpec((seq, HEAD_W), lambda b, n: (row_block0 + b, COL_C_V + n)),
                  cache, cache,
                  pl.BlockSpec((3, seq, HEAD_W), lambda b, n: (0, 0, 0)),
                  vec, vec, ANY_SPEC],
        out_specs=pl.BlockSpec((seq, group * HEAD_W), lambda b, n: (row_block0 + b, n)),
        out_shape=jax.ShapeDtypeStruct(mixed_prev.shape, BF16),
        input_output_aliases={8: 0},
        scratch_shapes=[pltpu.VMEM((group, seq, HEAD_W), BF16),
                        pltpu.VMEM((seq + past, HEAD_W), BF16),
                        pltpu.VMEM((seq + past, 2 * HEAD_W), BF16)],
        compiler_params=_params("parallel", "parallel"),
        name="latent_gqa_attention",
    )(proj, proj, proj, cache_k, cache_v, rope, gqa_qn[None, :], gqa_kn[None, :], mixed_prev)


def _diff_body(q_ref, k_ref, v_ref, ck_ref, cv_ref, rope_ref, qn_ref, kn_ref, sub_ref, lam_ref, prev_ref, o_ref,
               qs_ref, k0_ref, k1_ref, vs_ref, *, seq, tq, lam_init):
    del prev_ref
    half = DF_DQK // 2
    k = _rope(_rms_halves(k_ref[...], kn_ref[...]), rope_ref, half)
    lo = _lane_lt(k.shape, DF_DQK)
    k0_ref[0:seq, :] = jnp.where(lo, k, 0.0).astype(BF16)
    k1_ref[0:seq, :] = jnp.where(lo, 0.0, k).astype(BF16)
    ck = ck_ref[...]
    lo_c = _lane_lt(ck.shape, DF_DQK)
    k0_ref[seq:, :] = jnp.where(lo_c, ck, 0.0).astype(BF16)
    k1_ref[seq:, :] = jnp.where(lo_c, 0.0, ck).astype(BF16)
    vs_ref[0:seq, :] = _with_ones(v_ref[...].astype(BF16))
    vs_ref[seq:, :] = _with_ones(cv_ref[...].astype(BF16))
    q = _rms_halves(q_ref[...], qn_ref[...]) * (LOG2E * DF_DQK ** -0.5)
    qs_ref[...] = _rope(q, rope_ref, half).astype(BF16)
    lam = _lambda(lam_ref, lam_init)
    k0 = k0_ref[...]
    k1 = k1_ref[...]
    vv = vs_ref[...]

    def q_step(i, carry):
        qrows = pl.ds(pl.multiple_of(i * tq, tq), tq)
        qb = qs_ref[qrows, :]
        o = _diff_pv(_dot_nt(qb, k0), _dot_nt(qb, k1), lam, vv)
        o_ref[qrows, :] = (_rms_head(o, sub_ref[...]) * (1.0 - lam_init)).astype(o_ref.dtype)
        return carry

    lax.fori_loop(0, seq // tq, q_step, 0, unroll=min(seq // tq, 4))


def _latent_diff(proj, row_block0, n_seq, seq, cache_k2, cache_v, layer, rope, df_qn, df_kn, df_subln, df_lam,
                 lam_init, mixed_prev):
    past = cache_k2.shape[3]
    tq = 256

    def col(cb):
        return pl.BlockSpec((seq, HEAD_W), lambda b, h, cb=cb: (row_block0 + b, cb + h))

    cache = pl.BlockSpec((None, None, None, past, HEAD_W), lambda b, h: (b, layer, h, 0, 0))
    vec = pl.BlockSpec((1, HEAD_W), lambda b, h: (0, 0))
    kv_scratch = pltpu.VMEM((seq + past, HEAD_W), BF16)
    return pl.pallas_call(
        functools.partial(_diff_body, seq=seq, tq=tq, lam_init=lam_init),
        grid=(n_seq, N_HEADS),
        in_specs=[col(COL_D_Q), col(COL_D_K), col(COL_D_V), cache, cache,
                  pl.BlockSpec((3, seq, HEAD_W), lambda b, h: (0, 0, 0)),
                  vec, vec, vec, pl.BlockSpec((4, DF_DQK), lambda b, h: (0, 0)), ANY_SPEC],
        out_specs=pl.BlockSpec((seq, HEAD_W), lambda b, h: (row_block0 + b, h)),
        out_shape=jax.ShapeDtypeStruct(mixed_prev.shape, BF16),
        input_output_aliases={10: 0},
        scratch_shapes=[pltpu.VMEM((seq, HEAD_W), BF16), kv_scratch, kv_scratch,
                        pltpu.VMEM((seq + past, 2 * HEAD_W), BF16)],
        compiler_params=_params("parallel", "parallel"),
        name="latent_diff_attention",
    )(proj, proj, proj, cache_k2, cache_v, rope, jnp.tile(df_qn, 2)[None, :], jnp.tile(df_kn, 2)[None, :],
      df_subln[None, :], df_lam, mixed_prev)


def _first_max(vals):
    best = vals[0]
    idx = jnp.zeros(best.shape, jnp.int32)
    for i in range(1, len(vals)):
        better = vals[i] > best
        best = jnp.where(better, vals[i], best)
        idx = jnp.where(better, i, idx)
    return best, idx


def _pick(vals, idx):
    out = vals[0]
    for i in range(1, len(vals)):
        out = jnp.where(idx == i, vals[i], out)
    return out


def _outproj_body(ma_ref, mb_ref, mc_ref, md_ref, w_ref, hc_ref, hl_ref, mod_ref, n2_ref, rw_ref, rb_ref,
                  h1_ref, x2_ref, idx_ref, gate_ref, mixed_ref, *, ctx_tiles):
    for g, m_ref in enumerate((ma_ref, mb_ref, mc_ref, md_ref)):
        mixed_ref[:, g * GROUP_W:(g + 1) * GROUP_W] = m_ref[...]
    y = mod_ref[2:3, :] * _dot(mixed_ref[...], w_ref[...])

    def residual(h_ref):
        h1_ref[...] = h_ref[...] + y

    is_ctx = pl.program_id(0) < ctx_tiles
    pl.when(is_ctx)(lambda: residual(hc_ref))
    pl.when(jnp.logical_not(is_ctx))(lambda: residual(hl_ref))
    h1 = h1_ref[...]
    x2 = _rms(h1, n2_ref[...], D_MODEL) * (1.0 + mod_ref[4:5, :]) + mod_ref[3:4, :]
    x2_ref[...] = _pack_bf16_pairs(x2)
    x_hi = x2.astype(BF16)
    x_lo = (x2 - x_hi.astype(F32)).astype(BF16)
    acc = _dot(x_hi, rw_ref[...])
    logits_tok = acc[:, 0:HEAD_W] + acc[:, HEAD_W:2 * HEAD_W] + _dot(x_lo, rw_ref[:, 0:HEAD_W])
    logits = logits_tok.T[0:N_EXPERTS, :]
    aff_all = _sigmoid(logits)
    sel_all = aff_all + rb_ref[...]
    aff = [aff_all[e:e + 1, :] for e in range(N_EXPERTS)]
    sel = [sel_all[e:e + 1, :] for e in range(N_EXPERTS)]
    neg = jnp.full(sel[0].shape, -jnp.inf, F32)
    scores = []
    for g in range(N_EXP_GROUPS):
        grp = sel[g * EXP_PER_GROUP:(g + 1) * EXP_PER_GROUP]
        m1, i1 = _first_max(grp)
        m2, _ = _first_max([jnp.where(i1 == j, neg, grp[j]) for j in range(EXP_PER_GROUP)])
        scores.append(m1 + m2)
    _, g_best = _first_max(scores)
    in_sel = [_pick([sel[g * EXP_PER_GROUP + j] for g in range(N_EXP_GROUPS)], g_best)
              for j in range(EXP_PER_GROUP)]
    in_aff = [_pick([aff[g * EXP_PER_GROUP + j] for g in range(N_EXP_GROUPS)], g_best)
              for j in range(EXP_PER_GROUP)]
    _, l1 = _first_max(in_sel)
    _, l2 = _first_max([jnp.where(l1 == j, neg, in_sel[j]) for j in range(EXP_PER_GROUP)])
    w1 = _pick(in_aff, l1)
    w2 = _pick(in_aff, l2)
    idx_ref[0:1, :] = g_best * EXP_PER_GROUP + l1
    idx_ref[1:2, :] = g_best * EXP_PER_GROUP + l2
    gate_ref[0:1, :] = w1 / (w1 + w2)
    gate_ref[1:2, :] = w2 / (w1 + w2)


def _output_projection(mixed4, w_out_bf16, h_ctx, h_lat, mod, norm2, router_w, router_b, cond_of_tile, tm):
    d = h_ctx.shape[1]
    t = h_ctx.shape[0] + h_lat.shape[0]
    ctx_tiles = h_ctx.shape[0] // tm
    slab = pl.BlockSpec((tm, GROUP_W), lambda i: (i, 0))
    rw_hi = router_w.astype(BF16)
    rw_lo = (router_w - rw_hi.astype(F32)).astype(BF16)
    pad = ((0, 0), (0, HEAD_W - N_EXPERTS))
    router_split = jnp.concatenate([jnp.pad(rw_hi, pad), jnp.pad(rw_lo, pad)], axis=1)
    return pl.pallas_call(
        functools.partial(_outproj_body, ctx_tiles=ctx_tiles),
        grid=(t // tm,),
        in_specs=[slab, slab, slab, slab,
                  pl.BlockSpec((d, d), lambda i: (0, 0))] + _group_specs(tm, d, ctx_tiles) + [
                  pl.BlockSpec((None, 6, d), lambda i: (cond_of_tile(i), 0, 0)),
                  pl.BlockSpec((1, d), lambda i: (0, 0)),
                  pl.BlockSpec((d, 2 * HEAD_W), lambda i: (0, 0)),
                  pl.BlockSpec((N_EXPERTS, 1), lambda i: (0, 0))],
        out_specs=[pl.BlockSpec((tm, d), lambda i: (i, 0)),
                   pl.BlockSpec((tm, d // 2), lambda i: (i, 0)),
                   pl.BlockSpec((2, tm), lambda i: (0, i)),
                   pl.BlockSpec((2, tm), lambda i: (0, i))],
        out_shape=[jax.ShapeDtypeStruct((t, d), F32), jax.ShapeDtypeStruct((t, d // 2), jnp.uint32),
                   jax.ShapeDtypeStruct((2, t), jnp.int32), jax.ShapeDtypeStruct((2, t), F32)],
        scratch_shapes=[pltpu.VMEM((tm, d), BF16)],
        compiler_params=_params("parallel", vmem=VMEM_LIMIT_PROJ),
        name="out_proj_residual_router",
    )(*mixed4, w_out_bf16, h_ctx, h_lat, mod, norm2[None, :], router_split, router_b[:, None])


def _vmem_row(ref, base, u):
    return ref.at[pl.ds(base, SUBLANES), :].at[pl.ds(u, 1), :]


def _for_row_groups(n_rows, fn):
    def body(g, carry):
        base = pl.multiple_of(g * SUBLANES, SUBLANES)
        for u in range(SUBLANES):
            fn(base, u)
        return carry

    lax.fori_loop(0, n_rows // SUBLANES, body, 0)


def _rows_wait(src_hbm, dst, sem, n_rows):
    pltpu.make_async_copy(src_hbm.at[pl.ds(0, n_rows), :], dst.at[pl.ds(0, n_rows), :], sem).wait()


def _dispatch_body(dest_ref, ps_ref, pe_ref, x_ref, xs_hbm, zero_ref, sem, zsem, *, n_tok, tile):
    i = pl.program_id(0)

    def zero_copy(e):
        first = pl.multiple_of(pe_ref[e] - MOE_ROWS, MOE_ROWS)
        return pltpu.make_async_copy(zero_ref, xs_hbm.at[pl.ds(first, MOE_ROWS), :], zsem)

    @pl.when(i == 0)
    def _():
        zero_ref[...] = jnp.zeros(zero_ref.shape, zero_ref.dtype)
        for e in range(N_EXPERTS):
            @pl.when(pe_ref[e] > ps_ref[e])
            def _(e=e):
                zero_copy(e).start()
        for e in range(N_EXPERTS):
            @pl.when(pe_ref[e] > ps_ref[e])
            def _(e=e):
                zero_copy(e).wait()

    for k in range(2):
        def scatter_row(base, u, k=k):
            row = dest_ref[k * n_tok + i * tile + base + u]
            pltpu.make_async_copy(_vmem_row(x_ref, base, u), xs_hbm.at[pl.ds(row, 1), :], sem).start(
                priority=u % N_DMA_PRIORITIES)

        _for_row_groups(tile, scatter_row)
    for k in range(2):
        _rows_wait(x_ref, xs_hbm, sem, tile)


def _dispatch(x2, dest, pad_start, pad_end, n_rows):
    t, d = x2.shape
    tile = next(m for m in DISPATCH_TILES if t % m == 0)
    grid_spec = pltpu.PrefetchScalarGridSpec(
        num_scalar_prefetch=3,
        grid=(t // tile,),
        in_specs=[pl.BlockSpec((tile, d), lambda i, dst, ps, pe: (i, 0))],
        out_specs=ANY_SPEC,
        scratch_shapes=[pltpu.VMEM((MOE_ROWS, d), x2.dtype), pltpu.SemaphoreType.DMA, pltpu.SemaphoreType.DMA],
    )
    return pl.pallas_call(
        functools.partial(_dispatch_body, n_tok=t, tile=tile),
        grid_spec=grid_spec,
        out_shape=jax.ShapeDtypeStruct((n_rows, d), x2.dtype),
        compiler_params=_params("arbitrary"),
        name="moe_dispatch",
    )(dest, pad_start, pad_end, x2)


def _expert_body(be_ref, nb_ref, nx_ref, x_ref, wg_hbm, wu_hbm, wd_hbm, o_ref,
                 stage_g, stage_u, stage_d, wg_bf, wu_bf, wd_bf, sem, *, layer):
    i = pl.program_id(0)

    def weight_copies(e):
        return (pltpu.make_async_copy(wg_hbm.at[layer, e], stage_g, sem.at[0]),
                pltpu.make_async_copy(wu_hbm.at[layer, e], stage_u, sem.at[1]),
                pltpu.make_async_copy(wd_hbm.at[layer, e], stage_d, sem.at[2]))

    @pl.when(i < nb_ref[0])
    def _():
        e = be_ref[i]

        @pl.when(i == 0)
        def _():
            for c in weight_copies(e):
                c.start()

        @pl.when((i == 0) | (e != be_ref[jnp.maximum(i - 1, 0)]))
        def _():
            for c in weight_copies(e):
                c.wait()
            wg_bf[...] = stage_g[...].astype(BF16)
            wu_bf[...] = stage_u[...].astype(BF16)
            wd_bf[...] = stage_d[...].astype(BF16)
            nxt = nx_ref[e]

            @pl.when(nxt < N_EXPERTS)
            def _():
                for c in weight_copies(nxt):
                    c.start()

        x = _unpack_bf16_pairs(x_ref[...]).astype(BF16)
        hdn = _silu(_dot(x, wg_bf[...])) * _dot(x, wu_bf[...])
        o_ref[...] = _pack_bf16_pairs(_dot(hdn.astype(BF16), wd_bf[...]))

    @pl.when(i >= nb_ref[0])
    def _():
        o_ref[...] = jnp.zeros(o_ref.shape, o_ref.dtype)


def _expert_blocks(xs, block_expert, n_used, next_expert, w_gate, w_up, w_down, layer):
    n_rows, half = xs.shape
    d = 2 * half
    ff = w_gate.shape[-1]
    grid_spec = pltpu.PrefetchScalarGridSpec(
        num_scalar_prefetch=3,
        grid=(n_rows // MOE_ROWS,),
        in_specs=[pl.BlockSpec((MOE_ROWS, half), lambda i, be, nb, nx: (jnp.minimum(i, nb[0] - 1), 0)),
                  ANY_SPEC, ANY_SPEC, ANY_SPEC],
        out_specs=pl.BlockSpec((MOE_ROWS, half), lambda i, be, nb, nx: (i, 0)),
        scratch_shapes=[pltpu.VMEM((d, ff), F32), pltpu.VMEM((d, ff), F32), pltpu.VMEM((ff, d), F32),
                        pltpu.VMEM((d, ff), BF16), pltpu.VMEM((d, ff), BF16), pltpu.VMEM((ff, d), BF16),
                        pltpu.SemaphoreType.DMA((3,))],
    )
    return pl.pallas_call(
        functools.partial(_expert_body, layer=layer),
        grid_spec=grid_spec,
        out_shape=jax.ShapeDtypeStruct((n_rows, half), jnp.uint32),
        compiler_params=_params("arbitrary"),
        name="moe_expert_blocks",
    )(block_expert, n_used, next_expert, xs, w_gate, w_up, w_down)


def _combine_body(dest_ref, h_ref, gate_ref, y_hbm, mod_ref, *rest, n_tok, tile, ctx_tiles):
    out_refs, (ybuf, sem) = rest[:-2], rest[-2:]
    i = pl.program_id(0)
    n_tiles = pl.num_programs(0)

    def start(blk, slot):
        for k in range(2):
            def gather_row(base, u, k=k):
                row = dest_ref[k * n_tok + blk * tile + base + u]
                pltpu.make_async_copy(y_hbm.at[pl.ds(row, 1), :], _vmem_row(ybuf.at[slot, k], base, u),
                                      sem.at[slot]).start(priority=u % N_DMA_PRIORITIES)

            _for_row_groups(tile, gather_row)

    @pl.when(i == 0)
    def _():
        start(0, 0)

    @pl.when(i + 1 < n_tiles)
    def _():
        start(i + 1, (i + 1) % 2)

    slot = i % 2
    for k in range(2):
        _rows_wait(y_hbm, ybuf.at[slot, k], sem.at[slot], tile)
    gate = gate_ref[...]
    y0 = _unpack_bf16_pairs(ybuf[slot, 0])
    y1 = _unpack_bf16_pairs(ybuf[slot, 1])
    out = h_ref[...] + mod_ref[5:6, :] * (gate[:, 0:1] * y0 + gate[:, 1:2] * y1)
    @pl.when(i < ctx_tiles)
    def _():
        out_refs[0][...] = out

    @pl.when(i >= ctx_tiles)
    def _():
        out_refs[1][...] = out


def _combine(h1, yb, dest, gates, mod, cond_of_tile, tile, split_rows):
    t, d = h1.shape
    row_tile = pl.BlockSpec((tile, d), lambda i, dst: (i, 0))
    ctx_tiles = split_rows // tile
    out_specs = _group_specs(tile, d, ctx_tiles)
    out_shape = [jax.ShapeDtypeStruct((split_rows, d), F32), jax.ShapeDtypeStruct((t - split_rows, d), F32)]
    grid_spec = pltpu.PrefetchScalarGridSpec(
        num_scalar_prefetch=1,
        grid=(t // tile,),
        in_specs=[row_tile,
                  pl.BlockSpec((tile, 2), lambda i, dst: (i, 0)),
                  ANY_SPEC,
                  pl.BlockSpec((None, 6, d), lambda i, dst: (cond_of_tile(i), 0, 0))],
        out_specs=out_specs,
        scratch_shapes=[pltpu.VMEM((2, 2, tile, yb.shape[1]), yb.dtype), pltpu.SemaphoreType.DMA((2,))],
    )
    return pl.pallas_call(
        functools.partial(_combine_body, n_tok=t, tile=tile, ctx_tiles=ctx_tiles),
        grid_spec=grid_spec,
        out_shape=out_shape,
        compiler_params=_params("arbitrary"),
        name="moe_gated_residual",
    )(dest, h1, gates, yb, mod)


def _moe(h1, x2, idx_t, gate_t, mod, w_gate, w_up, w_down, layer, cond_of_tile, split_rows):
    t, d = h1.shape
    n = 2 * t
    experts = idx_t.reshape(n)
    onehot = (experts[:, None] == jnp.arange(N_EXPERTS, dtype=jnp.int32)[None, :]).astype(BF16)
    blocks = onehot.reshape(n // COMBINE_TILE, COMBINE_TILE, N_EXPERTS)
    tri = jnp.asarray(np.tril(np.ones((COMBINE_TILE, COMBINE_TILE), np.float32)), BF16)
    within = jnp.einsum('ij,bjk->bik', tri, blocks, preferred_element_type=F32)
    block_total = within[:, -1, :]
    block_first = jnp.cumsum(block_total, axis=0) - block_total
    counts = (block_first[-1] + block_total[-1]).astype(jnp.int32)
    before = (within + block_first[:, None, :]).reshape(n, N_EXPERTS) - 1.0
    rank = jnp.sum(before * onehot.astype(F32), axis=1).astype(jnp.int32)
    padded = (counts + MOE_ROWS - 1) // MOE_ROWS * MOE_ROWS
    pad_end = jnp.cumsum(padded).astype(jnp.int32)
    pad_start = pad_end - padded
    dest = (pad_start[experts] + rank).astype(jnp.int32)
    n_blocks = (n + N_EXPERTS * (MOE_ROWS - 1) + MOE_ROWS - 1) // MOE_ROWS
    block_first_row = jnp.arange(n_blocks, dtype=jnp.int32) * MOE_ROWS
    block_expert = jnp.minimum(jnp.sum((pad_end[None, :] <= block_first_row[:, None]).astype(jnp.int32), axis=1),
                               N_EXPERTS - 1)
    n_used = (pad_end[-1:] // MOE_ROWS).astype(jnp.int32)
    xs = _dispatch(x2, dest, pad_start, pad_end, n_blocks * MOE_ROWS)
    ids = jnp.arange(N_EXPERTS, dtype=jnp.int32)
    later_with_rows = (counts[None, :] > 0) & (ids[None, :] > ids[:, None])
    next_expert = jnp.min(jnp.where(later_with_rows, ids[None, :], N_EXPERTS), axis=1).astype(jnp.int32)
    yb = _expert_blocks(xs, block_expert, n_used, next_expert, w_gate, w_up, w_down, layer)
    return _combine(h1, yb, dest, gate_t.T, mod, cond_of_tile, COMBINE_TILE, split_rows)


def kernel(x_prompt, x_sample, cache_na_k, cache_na_v, cache_gqa_k, cache_gqa_v, cache_diff_k, cache_diff_v, state_hgrn, c, c_ctx, w_mod, b_mod, norm1, norm2, w_in, w_out, hg_lb_logits, hg_onorm, na_qn, na_kn, na_rpb, gqa_qn, gqa_kn, df_qn, df_kn, df_lam, df_subln, router_w, router_b, w_gate, w_up, w_down):
    n_ctx, ctx_len, d = x_prompt.shape
    n_lat, lat_len, _ = x_sample.shape
    depth = w_in.shape[0]
    t_ctx = n_ctx * ctx_len
    assert t_ctx % lat_len == 0 and lat_len % GRID_W == 0 and lat_len // GRID_W >= WIN_ROWS
    tm = next(m for m in (1024, 512, 256) if t_ctx % m == 0 and lat_len % m == 0)
    tm2 = min(tm, 512)
    lat_block0 = t_ctx // lat_len

    def cond_tile(tile_rows):
        def cond_of_tile(i):
            return jnp.where(i < t_ctx // tile_rows, 0, 1 + (i - t_ctx // tile_rows) // (lat_len // tile_rows))
        return cond_of_tile

    sm = jax.nn.softmax(hg_lb_logits.astype(F32), axis=0)
    lower = jnp.cumsum(sm, axis=0) - sm[0:1]
    mod_all = _modulation(jnp.concatenate([c_ctx[None, :], c], axis=0), w_mod, b_mod)
    mod_all = mod_all.reshape(depth, 1 + n_lat, 6, d)
    hgrn_consts = _hgrn_constants(HGRN_CHUNK)
    rope_c = _rope_tables(lat_len, HEAD_W)
    rope_d = _rope_tables(lat_len, DF_DQK)
    past = cache_diff_k.shape[4]
    cache_diff_k2 = cache_diff_k.transpose(0, 1, 2, 4, 3, 5).reshape(n_lat, depth, N_HEADS, past, HEAD_W)

    h_ctx, h_lat = x_prompt.reshape(t_ctx, d), x_sample.reshape(n_lat * lat_len, d)
    caches, states = [], None
    for layer in range(depth):
        mod = mod_all[layer]
        lam_init = 0.8 - 0.6 * math.exp(-0.3 * layer)
        proj = _input_projection(h_ctx, h_lat, mod, norm1[layer], w_in[layer].astype(BF16), cond_tile(tm), tm)
        mix_a, states = _hgrn(proj, 0, n_ctx, ctx_len, lower[layer], hg_onorm[layer], hgrn_consts, None, None,
                              states, layer, depth)
        gains = (na_qn[layer], na_kn[layer], gqa_qn[layer], gqa_kn[layer], df_qn[layer], df_kn[layer],
                 df_subln[layer], df_lam[layer])
        mix_b, mix_c, mix_d, *caches = _context_attention(proj, n_ctx, ctx_len, gains, lam_init, layer, depth,
                                                          caches)
        mix_a, _ = _hgrn(proj, lat_block0, n_lat, lat_len, lower[layer], hg_onorm[layer], hgrn_consts,
                         state_hgrn, mix_a, None, layer, depth)
        mix_b = _latent_na(proj, lat_block0, n_lat, lat_len, cache_na_k, cache_na_v, layer,
                           _na_bias(na_rpb[layer]), na_qn[layer], na_kn[layer], mix_b)
        mix_c = _latent_gqa(proj, lat_block0, n_lat, lat_len, cache_gqa_k, cache_gqa_v, layer, rope_c,
                            gqa_qn[layer], gqa_kn[layer], mix_c)
        mix_d = _latent_diff(proj, lat_block0, n_lat, lat_len, cache_diff_k2, cache_diff_v, layer, rope_d,
                             df_qn[layer], df_kn[layer], df_subln[layer], df_lam[layer], lam_init, mix_d)
        h1, x2, idx_t, gate_t = _output_projection((mix_a, mix_b, mix_c, mix_d), w_out[layer].astype(BF16), h_ctx,
                                                   h_lat, mod, norm2[layer], router_w, router_b, cond_tile(tm2), tm2)
        h_ctx, h_lat = _moe(h1, x2, idx_t, gate_t, mod, w_gate, w_up, w_down, layer, cond_tile(COMBINE_TILE), t_ctx)
    y_prompt = h_ctx.reshape(n_ctx, ctx_len, d)
    y_sample = h_lat.reshape(n_lat, lat_len, d)
    return (y_prompt, y_sample, *caches, states)
```

```python
import functools
import math

import numpy as np
import jax
import jax.numpy as jnp
from jax import lax
from jax.experimental import pallas as pl
from jax.experimental.pallas import tpu as pltpu

D_MODEL = 2048
GRID_W = 64
GROUP_W = D_MODEL // 4
N_HEADS = 4
HEAD_W = GROUP_W // N_HEADS
SUBLANES = 8
GQA_KV_HEADS = 2
DF_DQK = HEAD_W // 2
WIN_ROWS = 8
WIN_COLS = 16
N_EXPERTS = 16
N_EXP_GROUPS = 4
EXP_PER_GROUP = N_EXPERTS // N_EXP_GROUPS
EXPERT_FF = D_MODEL // 4
ROPE_THETA = 10000.0
EPS = 1e-6
NEG_INF = -1e30
LOG2E = math.log2(math.e)
IN_WIDTH = 13 * GROUP_W

COL_A_Q, COL_A_FF, COL_A_FB, COL_A_I, COL_A_G = 0, 4, 8, 12, 16
COL_B_Q, COL_B_K, COL_B_V = 20, 24, 28
COL_C_Q, COL_C_K, COL_C_V = 32, 36, 38
COL_D_Q, COL_D_K, COL_D_V = 40, 44, 48

HGRN_CHUNK = 128
HGRN_UNROLL = 4
HGRN_INPUT_VMEM = 24 * 1024 * 1024
CTX_KV_HEADS_PER_STEP = 1
MOE_ROWS = 256
DISPATCH_TILES = (1024, 512, 256)
COMBINE_TILE = 256
VMEM_LIMIT = 48 * 1024 * 1024
VMEM_LIMIT_PROJ = 56 * 1024 * 1024

F32 = jnp.float32
BF16 = jnp.bfloat16
ANY_SPEC = pl.BlockSpec(memory_space=pl.ANY)


def _params(*sem, vmem=VMEM_LIMIT):
    return pltpu.CompilerParams(dimension_semantics=sem, vmem_limit_bytes=vmem)


def _sigmoid(x):
    return 1.0 / (1.0 + jnp.exp(-x))


def _silu(x):
    return x * _sigmoid(x)


def _rms(x, gain, n):
    return x * lax.rsqrt(jnp.sum(x * x, axis=-1, keepdims=True) * (1.0 / n) + EPS) * gain


def _dot(a, b):
    return jnp.dot(a, b, preferred_element_type=F32)


def _dot_nt(a, b):
    return lax.dot_general(a, b, (((1,), (1,)), ((), ())), preferred_element_type=F32)


def _dot_tn(a, b):
    return lax.dot_general(a, b, (((0,), (0,)), ((), ())), preferred_element_type=F32)


def _pack_bf16_pairs(x):
    k = x.shape[1] // 2
    lo = lax.bitcast_convert_type(x[:, :k].astype(BF16).astype(F32), jnp.uint32) >> 16
    hi = lax.bitcast_convert_type(x[:, k:].astype(BF16).astype(F32), jnp.uint32)
    return hi | lo


def _unpack_bf16_pairs(w):
    lo = lax.bitcast_convert_type(w << 16, F32)
    hi = lax.bitcast_convert_type(w & jnp.uint32(0xFFFF0000), F32)
    return jnp.concatenate([lo, hi], axis=1)


def _aligned(x, m):
    return x if isinstance(x, int) else pl.multiple_of(x, m)


def _alias_kwargs(n_inputs, prev, first_out):
    return ([ANY_SPEC] * len(prev), list(prev), {n_inputs + k: first_out + k for k in range(len(prev))})


def _mod_body(cond_ref, w_ref, b_ref, o_ref):
    w = w_ref[...]
    for c in range(cond_ref.shape[0]):
        s = _silu(cond_ref[c])
        o_ref[c:c + 1, :] = jnp.sum(w * s, axis=0, keepdims=True) + b_ref[...]


def _modulation(cond, w_mod, b_mod):
    depth, d, n6 = w_mod.shape
    nc = cond.shape[0]
    tn = 1024
    return pl.pallas_call(
        _mod_body,
        grid=(depth, n6 // tn),
        in_specs=[pl.BlockSpec((nc, d, 1), lambda l, j: (0, 0, 0)),
                  pl.BlockSpec((None, d, tn), lambda l, j: (l, 0, j)),
                  pl.BlockSpec((None, 1, tn), lambda l, j: (l, 0, j))],
        out_specs=pl.BlockSpec((None, nc, tn), lambda l, j: (l, 0, j)),
        out_shape=jax.ShapeDtypeStruct((depth, nc, n6), F32),
        compiler_params=_params("parallel", "parallel"),
        name="adaln_modulation",
    )(cond[:, :, None], w_mod, b_mod[:, None, :])


def _group_specs(tile, d, ctx_tiles):
    return [pl.BlockSpec((tile, d), lambda i, *_: (jnp.minimum(i, ctx_tiles - 1), 0)),
            pl.BlockSpec((tile, d), lambda i, *_: (jnp.maximum(i - ctx_tiles, 0), 0))]


INPROJ_NORM_CHUNKS = 8


def _inproj_body(h0_ref, mod0_ref, hc_ref, hl_ref, mod_ref, n1_ref, w_ref, o_ref, xn_ref, *, ctx_tiles, chunk):
    i = pl.program_id(0)
    j = pl.program_id(1)

    def normalise(h, mod):
        return (_rms(h, n1_ref[...], D_MODEL) * (1.0 + mod[1:2, :]) + mod[0:1, :]).astype(BF16)

    @pl.when((i == 0) & (j == 0))
    def _():
        xn_ref[0] = normalise(h0_ref[...], mod0_ref[...])

    nxt = i + 1
    rows = pl.ds(pl.multiple_of(jnp.minimum(j, INPROJ_NORM_CHUNKS - 1) * chunk, chunk), chunk)
    xn_ref[nxt % 2, rows, :] = normalise(jnp.where(nxt < ctx_tiles, hc_ref[...], hl_ref[...]), mod_ref[...])
    o_ref[...] = _dot(xn_ref[i % 2], w_ref[...])


def _input_projection(h_ctx, h_lat, mod, norm1, w_in_bf16, cond_of_tile, tm):
    d = h_ctx.shape[1]
    t = h_ctx.shape[0] + h_lat.shape[0]
    n_tiles, ctx_tiles, lat_tiles = t // tm, h_ctx.shape[0] // tm, h_lat.shape[0] // tm
    n = w_in_bf16.shape[1]
    tn = 512
    pieces = INPROJ_NORM_CHUNKS
    chunk = tm // pieces
    assert n // tn >= pieces and ctx_tiles >= 1

    def piece(j):
        return jnp.minimum(j, pieces - 1)

    def nxt(i):
        return jnp.minimum(i + 1, n_tiles - 1)

    ctx_chunk = lambda i, j: (jnp.where(nxt(i) < ctx_tiles, nxt(i) * pieces + piece(j), ctx_tiles * pieces - 1), 0)
    lat_chunk = lambda i, j: (jnp.where(nxt(i) >= ctx_tiles, (nxt(i) - ctx_tiles) * pieces + piece(j), 0), 0)
    return pl.pallas_call(
        functools.partial(_inproj_body, ctx_tiles=ctx_tiles, chunk=chunk),
        grid=(n_tiles, n // tn),
        in_specs=[pl.BlockSpec((tm, d), lambda i, j: (0, 0)),
                  pl.BlockSpec((None, 6, d), lambda i, j: (cond_of_tile(0), 0, 0)),
                  pl.BlockSpec((chunk, d), ctx_chunk),
                  pl.BlockSpec((chunk, d), lat_chunk),
                  pl.BlockSpec((None, 6, d), lambda i, j: (cond_of_tile(nxt(i)), 0, 0)),
                  pl.BlockSpec((1, d), lambda i, j: (0, 0)),
                  pl.BlockSpec((d, tn), lambda i, j: (0, j))],
        out_specs=pl.BlockSpec((tm, tn), lambda i, j: (i, j)),
        out_shape=jax.ShapeDtypeStruct((t, n), F32),
        scratch_shapes=[pltpu.VMEM((2, tm, d), BF16)],
        compiler_params=_params("arbitrary", "arbitrary"),
        name="norm_modulate_in_proj",
    )(h_ctx, mod, h_ctx, h_lat, mod, norm1[None, :], w_in_bf16)


def _hgrn_constants(c):
    nl = int(math.log2(c))
    idx = np.arange(c)
    e = np.zeros((nl + 2, c, c), np.float32)
    m = np.zeros((nl + 1, c, c), np.float32)
    e[0] = idx[None, :] <= idx[:, None]
    e[1] = idx[None, :] > idx[:, None]
    m[0] = np.eye(c)
    for li in range(nl):
        s = c >> (li + 1)
        parent = idx // (2 * s)
        right = (idx % (2 * s)) >= s
        ref = parent * 2 * s + s - 1
        for i in range(c):
            if right[i]:
                e[2 + li, i, ref[i] + 1:i + 1] = 1.0
            else:
                e[2 + li, i, i + 1:ref[i] + 1] = 1.0
        m[1 + li] = right[:, None] & ~right[None, :] & (parent[:, None] == parent[None, :])
    keep = [0, 1] + [2 + li for li in range(nl) if (c >> (li + 1)) < SUBLANES]
    e = e[keep]
    e2 = np.stack([e, e[:, ::-1, ::-1]]).reshape(2, len(keep) * c, c)
    m2 = np.stack([m, m[:, ::-1, ::-1]])
    return jnp.asarray(e2, BF16), jnp.asarray(m2, F32)


def _hgrn_body(*refs, seq, chunk, unroll, heads, has_s0, emit_state, n_alias):
    q_ref, ff_ref, fb_ref, i_ref, g_ref, lb_ref, on_ref, e_ref, m_ref = refs[:9]
    pos = 9
    s0_ref = None
    if has_s0:
        s0_ref = refs[pos]
        pos += 1
    pos += n_alias
    o_ref = refs[pos]
    pos += 1
    if emit_state:
        st_ref = refs[pos]
        pos += 1
    of_ref, ob_ref = refs[pos], refs[pos + 1]
    c = chunk
    n_chunks = seq // c
    assert seq % c == 0 and n_chunks % unroll == 0
    n_levels = m_ref.shape[1] - 1
    gate_refs = (ff_ref, fb_ref)
    out_refs = (of_ref, ob_ref)

    def chunk_step(c0, d, hh, st):
        rows = pl.ds(c0, c)
        lanes = slice(hh * HEAD_W, (hh + 1) * HEAD_W)
        lb = lb_ref[d:d + 1, lanes]
        f = lb + (1.0 - lb) * _sigmoid(gate_refs[d][rows, lanes])
        g = jnp.log2(f)
        k = 1.0 - f
        q = _silu(q_ref[rows, lanes])
        v = i_ref[rows, lanes].astype(BF16)
        g_hi = g.astype(BF16)
        g_lo = (g - g_hi.astype(F32)).astype(BF16)
        g2 = _dot(e_ref[d], jnp.concatenate([g_hi, g_lo], axis=1))
        gsum = g2[:, 0:HEAD_W] + g2[:, HEAD_W:2 * HEAD_W]
        cum = gsum[0:c]
        x_cum = jnp.exp2(cum)
        x_tail = jnp.exp2(gsum[c:2 * c])
        row = lax.broadcasted_iota(jnp.int32, (c, HEAD_W), 0)
        s = m_ref[d, 0] * _dot_nt(q.astype(BF16), k.astype(BF16))
        n_matmul_levels = 0
        for lv in range(n_levels):
            half = c >> (lv + 1)
            if half >= SUBLANES:
                ref_row = half - 1 if d == 0 else half
                ref = jnp.concatenate(
                    [jnp.broadcast_to(cum[b * 2 * half + ref_row:b * 2 * half + ref_row + 1, :], (2 * half, HEAD_W))
                     for b in range(c // (2 * half))], axis=0)
                near = ((row % (2 * half)) >= half) == (d == 0)
                x_l = jnp.exp2(jnp.where(near, cum - ref, ref - cum))
            else:
                x_l = jnp.exp2(gsum[(2 + n_matmul_levels) * c:(3 + n_matmul_levels) * c])
                n_matmul_levels += 1
            s = s + m_ref[d, 1 + lv] * _dot_nt((q * x_l).astype(BF16), (k * x_l).astype(BF16))
        o = _dot_nt((q * x_cum).astype(BF16), st.astype(BF16)) + _dot(s.astype(BF16), v)
        out_refs[d][rows, lanes] = o
        total = x_cum[c - 1:c, :] if d == 0 else x_cum[0:1, :]
        return st * total + _dot_tn(v, (k * x_tail).astype(BF16))

    if has_s0:
        states0 = tuple(s0_ref[d, hh].T for hh in range(heads) for d in range(2))
    else:
        states0 = tuple(jnp.zeros((HEAD_W, HEAD_W), F32) for _ in range(2 * heads))

    def loop(t, states):
        states = list(states)
        for u in range(unroll):
            j = t * unroll + u
            for hh in range(heads):
                states[2 * hh] = chunk_step(_aligned(j * c, c), 0, hh, states[2 * hh])
                states[2 * hh + 1] = chunk_step(_aligned((n_chunks - 1 - j) * c, c), 1, hh, states[2 * hh + 1])
        return tuple(states)

    if n_chunks == unroll:
        states = loop(0, states0)
    else:
        states = lax.fori_loop(0, n_chunks // unroll, loop, states0)
    for hh in range(heads):
        lanes = slice(hh * HEAD_W, (hh + 1) * HEAD_W)
        o = of_ref[:, lanes] + ob_ref[:, lanes]
        o_ref[:, lanes] = (_rms(o, on_ref[...], HEAD_W) * _silu(g_ref[:, lanes])).astype(o_ref.dtype)
        if emit_state:
            st_ref[0, hh] = states[2 * hh].T
            st_ref[1, hh] = states[2 * hh + 1].T


def _hgrn(proj, row_block0, n_seq, seq, lower, onorm, consts, s0, mixed_prev, state_prev, layer, depth):
    e_mat, masks = consts
    latent = s0 is not None
    hps = next(n for n in (4, 2, 1) if 5 * 2 * seq * n * HEAD_W * 4 <= HGRN_INPUT_VMEM)
    width = hps * HEAD_W

    def col(cb):
        return pl.BlockSpec((seq, width), lambda b, h, cb=cb: (row_block0 + b, cb // hps + h))

    state_spec = pl.BlockSpec((None, None, 2, hps, HEAD_W, HEAD_W), lambda b, h: (b, layer, 0, h, 0, 0))
    in_specs = [col(COL_A_Q), col(COL_A_FF), col(COL_A_FB), col(COL_A_I), col(COL_A_G),
                pl.BlockSpec((2, width), lambda b, h: (0, h)),
                pl.BlockSpec((1, HEAD_W), lambda b, h: (0, 0)),
                pl.BlockSpec(e_mat.shape, lambda b, h: (0, 0, 0)),
                pl.BlockSpec(masks.shape, lambda b, h: (0, 0, 0, 0))]
    args = [proj, proj, proj, proj, proj, lower, onorm[None, :], e_mat, masks]
    if latent:
        in_specs.append(state_spec)
        args.append(s0)
        prev = [mixed_prev]
    else:
        prev = [] if state_prev is None else [state_prev]
    alias_specs, alias_args, aliases = _alias_kwargs(len(args), prev, 0 if latent else 1)
    out_specs = [pl.BlockSpec((seq, width), lambda b, h: (row_block0 + b, h))]
    out_shape = [jax.ShapeDtypeStruct((proj.shape[0], GROUP_W), BF16)]
    if not latent:
        out_specs.append(state_spec)
        out_shape.append(jax.ShapeDtypeStruct((n_seq, depth, 2, N_HEADS, HEAD_W, HEAD_W), F32))
    res = pl.pallas_call(
        functools.partial(_hgrn_body, seq=seq, chunk=HGRN_CHUNK, unroll=min(HGRN_UNROLL, seq // HGRN_CHUNK),
                          heads=hps, has_s0=latent,
                          emit_state=not latent, n_alias=len(prev)),
        grid=(n_seq, N_HEADS // hps),
        in_specs=in_specs + alias_specs, out_specs=out_specs, out_shape=out_shape,
        input_output_aliases=aliases,
        scratch_shapes=[pltpu.VMEM((seq, width), F32), pltpu.VMEM((seq, width), F32)],
        compiler_params=_params("parallel", "parallel"),
        name="hgrn2_latent" if latent else "hgrn2_context",
    )(*args, *alias_args)
    return (res[0], None) if latent else res


def _with_ones(v):
    return jnp.concatenate([v, jnp.ones_like(v)], axis=1)


def _softmax_pv(scores, values1):
    mx = functools.reduce(jnp.maximum, [jnp.max(s, axis=-1, keepdims=True) for s in scores])
    acc = functools.reduce(lambda a, b: a + b,
                           [_dot(jnp.exp2(s - mx).astype(BF16), v) for s, v in zip(scores, values1)])
    return acc[:, 0:HEAD_W] / acc[:, HEAD_W:HEAD_W + 1]


def _diff_pv(s0, s1, lam, values1):
    return _softmax_pv([s0], [values1]) - lam * _softmax_pv([s1], [values1])


def _lane_lt(shape, n):
    return lax.broadcasted_iota(jnp.int32, shape, len(shape) - 1) < n


def _rms_head(x, gain):
    return _rms(x, gain, HEAD_W)


def _rms_halves(x, gain2):
    lo = _lane_lt(x.shape, DF_DQK)
    sq = x * x
    ss_lo = jnp.sum(jnp.where(lo, sq, 0.0), axis=-1, keepdims=True)
    ss_hi = jnp.sum(sq, axis=-1, keepdims=True) - ss_lo
    inv = jnp.where(lo, lax.rsqrt(ss_lo * (1.0 / DF_DQK) + EPS), lax.rsqrt(ss_hi * (1.0 / DF_DQK) + EPS))
    return x * inv * gain2


def _lambda(lam_ref, lam_init):
    l = lam_ref[...]
    return (jnp.exp(jnp.sum(l[0:1] * l[1:2], axis=-1, keepdims=True))
            - jnp.exp(jnp.sum(l[2:3] * l[3:4], axis=-1, keepdims=True)) + lam_init)


N_CTX_ATTN_INPUTS = 17


def _ctx_attn_body(*refs, lam_init, kv_heads):
    (bq_ref, bk_ref, bv_ref, cq_ref, ck_ref, cv_ref, dq_ref, dk_ref, dv_ref,
     naq_ref, nak_ref, gq_ref, gk_ref, dfq_ref, dfk_ref, sub_ref, lam_ref) = refs[:N_CTX_ATTN_INPUTS]
    ob_ref, oc_ref, od_ref, kb_ref, vb_ref, kc_ref, vc_ref, kd_ref, vd_ref = refs[-9:]
    group = N_HEADS // GQA_KV_HEADS
    scale = LOG2E * HEAD_W ** -0.5
    lam = _lambda(lam_ref, lam_init)
    for n in range(kv_heads):
        kv_lanes = slice(n * HEAD_W, (n + 1) * HEAD_W)
        kc = _rms_head(ck_ref[:, kv_lanes], gk_ref[...])
        vc = cv_ref[:, kv_lanes]
        kc_ref[n] = kc
        vc_ref[n] = vc
        kc16 = kc.astype(BF16)
        vc1 = _with_ones(vc.astype(BF16))
        for g in range(group):
            h = n * group + g
            lanes = slice(h * HEAD_W, (h + 1) * HEAD_W)
            qc = (_rms_head(cq_ref[:, lanes], gq_ref[...]) * scale).astype(BF16)
            oc_ref[:, lanes] = _softmax_pv([_dot_nt(qc, kc16)], [vc1]).astype(oc_ref.dtype)
            kb = _rms_head(bk_ref[:, lanes], nak_ref[...])
            vb = bv_ref[:, lanes]
            kb_ref[h] = kb
            vb_ref[h] = vb
            qb = (_rms_head(bq_ref[:, lanes], naq_ref[...]) * scale).astype(BF16)
            ob_ref[:, lanes] = _softmax_pv([_dot_nt(qb, kb.astype(BF16))],
                                           [_with_ones(vb.astype(BF16))]).astype(ob_ref.dtype)
            kd = _rms_halves(dk_ref[:, lanes], dfk_ref[...])
            vd = dv_ref[:, lanes]
            kd_ref[h, 0] = kd[:, 0:DF_DQK]
            kd_ref[h, 1] = kd[:, DF_DQK:2 * DF_DQK]
            vd_ref[h] = vd
            qd = (_rms_halves(dq_ref[:, lanes], dfq_ref[...]) * (LOG2E * DF_DQK ** -0.5)).astype(BF16)
            lo = _lane_lt(kd.shape, DF_DQK)
            s0 = _dot_nt(qd, jnp.where(lo, kd, 0.0).astype(BF16))
            s1 = _dot_nt(qd, jnp.where(lo, 0.0, kd).astype(BF16))
            od = _diff_pv(s0, s1, lam, _with_ones(vd.astype(BF16)))
            od_ref[:, lanes] = (_rms_head(od, sub_ref[...]) * (1.0 - lam_init)).astype(od_ref.dtype)


def _context_attention(proj, n_seq, seq, gains, lam_init, layer, depth, caches_prev):
    na_qn, na_kn, gqa_qn, gqa_kn, df_qn, df_kn, df_subln, df_lam = gains
    group = N_HEADS // GQA_KV_HEADS
    kvs = CTX_KV_HEADS_PER_STEP
    n_heads = kvs * group
    width = n_heads * HEAD_W

    def heads(cb):
        return pl.BlockSpec((seq, width), lambda b, n, cb=cb: (b, cb // n_heads + n))

    def kv_head(cb):
        return pl.BlockSpec((seq, kvs * HEAD_W), lambda b, n, cb=cb: (b, cb // kvs + n))

    vec = pl.BlockSpec((1, HEAD_W), lambda b, n: (0, 0))
    cache_heads = pl.BlockSpec((None, None, n_heads, seq, HEAD_W), lambda b, n: (b, layer, n, 0, 0))
    cache_kv = pl.BlockSpec((None, None, kvs, seq, HEAD_W), lambda b, n: (b, layer, n, 0, 0))
    mixed = pl.BlockSpec((seq, width), lambda b, n: (b, n))
    mixed_shape = jax.ShapeDtypeStruct((proj.shape[0], GROUP_W), BF16)
    cache4 = jax.ShapeDtypeStruct((n_seq, depth, N_HEADS, seq, HEAD_W), F32)
    cache2 = jax.ShapeDtypeStruct((n_seq, depth, GQA_KV_HEADS, seq, HEAD_W), F32)
    cache_dk = jax.ShapeDtypeStruct((n_seq, depth, N_HEADS, 2, seq, DF_DQK), F32)
    args = [proj] * 9 + [na_qn[None, :], na_kn[None, :], gqa_qn[None, :], gqa_kn[None, :],
                         jnp.tile(df_qn, 2)[None, :], jnp.tile(df_kn, 2)[None, :], df_subln[None, :], df_lam]
    assert len(args) == N_CTX_ATTN_INPUTS
    alias_specs, alias_args, aliases = _alias_kwargs(len(args), caches_prev, 3)
    return pl.pallas_call(
        functools.partial(_ctx_attn_body, lam_init=lam_init, kv_heads=kvs),
        grid=(n_seq, GQA_KV_HEADS // kvs),
        in_specs=[heads(COL_B_Q), heads(COL_B_K), heads(COL_B_V),
                  heads(COL_C_Q), kv_head(COL_C_K), kv_head(COL_C_V),
                  heads(COL_D_Q), heads(COL_D_K), heads(COL_D_V),
                  vec, vec, vec, vec, vec, vec, vec,
                  pl.BlockSpec((4, DF_DQK), lambda b, n: (0, 0))] + alias_specs,
        out_specs=[mixed, mixed, mixed, cache_heads, cache_heads, cache_kv, cache_kv,
                   pl.BlockSpec((None, None, n_heads, 2, seq, DF_DQK), lambda b, n: (b, layer, n, 0, 0, 0)),
                   cache_heads],
        out_shape=[mixed_shape, mixed_shape, mixed_shape, cache4, cache4, cache2, cache2, cache_dk, cache4],
        input_output_aliases=aliases,
        compiler_params=_params("parallel", "parallel"),
        name="context_attention",
    )(*args, *alias_args)


N_HGRN_INPUTS = 9
N_CTX_GROUP_OUTPUTS = 11


def _ctx_group_body(*refs, seq, lam_init, n_alias):
    hg_in = refs[:N_HGRN_INPUTS]
    at_in = refs[N_HGRN_INPUTS:N_HGRN_INPUTS + N_CTX_ATTN_INPUTS]
    first_out = N_HGRN_INPUTS + N_CTX_ATTN_INPUTS + n_alias
    outs = refs[first_out:first_out + N_CTX_GROUP_OUTPUTS]
    scratch = refs[first_out + N_CTX_GROUP_OUTPUTS:]
    _hgrn_body(*hg_in, outs[0], outs[1], *scratch, seq=seq, chunk=HGRN_CHUNK,
               unroll=min(HGRN_UNROLL, seq // HGRN_CHUNK), heads=N_HEADS, has_s0=False, emit_state=True, n_alias=0)
    _ctx_attn_body(*at_in, *outs[2:], lam_init=lam_init, kv_heads=GQA_KV_HEADS)


def _context_group(proj, n_seq, seq, lower, onorm, consts, gains, lam_init, layer, depth, state_prev, caches_prev):
    e_mat, masks = consts
    na_qn, na_kn, gqa_qn, gqa_kn, df_qn, df_kn, df_subln, df_lam = gains

    def heads(cb):
        return pl.BlockSpec((seq, GROUP_W), lambda b, cb=cb: (b, cb // N_HEADS))

    def kv_heads(cb):
        return pl.BlockSpec((seq, GQA_KV_HEADS * HEAD_W), lambda b, cb=cb: (b, cb // GQA_KV_HEADS))

    def const(shape):
        return pl.BlockSpec(shape, lambda b: (0,) * len(shape))

    vec = const((1, HEAD_W))
    in_specs = [heads(COL_A_Q), heads(COL_A_FF), heads(COL_A_FB), heads(COL_A_I), heads(COL_A_G),
                const((2, GROUP_W)), vec, const(e_mat.shape), const(masks.shape),
                heads(COL_B_Q), heads(COL_B_K), heads(COL_B_V),
                heads(COL_C_Q), kv_heads(COL_C_K), kv_heads(COL_C_V),
                heads(COL_D_Q), heads(COL_D_K), heads(COL_D_V),
                vec, vec, vec, vec, vec, vec, vec, const((4, DF_DQK))]
    args = [proj] * 5 + [lower, onorm[None, :], e_mat, masks] + [proj] * 9 + [
        na_qn[None, :], na_kn[None, :], gqa_qn[None, :], gqa_kn[None, :],
        jnp.tile(df_qn, 2)[None, :], jnp.tile(df_kn, 2)[None, :], df_subln[None, :], df_lam]
    assert len(args) == N_HGRN_INPUTS + N_CTX_ATTN_INPUTS
    prev = [] if state_prev is None else [state_prev] + list(caches_prev)
    aliases = {len(args) + k: out for k, out in enumerate([1, 5, 6, 7, 8, 9, 10][:len(prev)])}
    mixed = pl.BlockSpec((seq, GROUP_W), lambda b: (b, 0))
    mixed_shape = jax.ShapeDtypeStruct((proj.shape[0], GROUP_W), BF16)
    cache_heads = pl.BlockSpec((None, None, N_HEADS, seq, HEAD_W), lambda b: (b, layer, 0, 0, 0))
    cache_kv = pl.BlockSpec((None, None, GQA_KV_HEADS, seq, HEAD_W), lambda b: (b, layer, 0, 0, 0))
    cache4 = jax.ShapeDtypeStruct((n_seq, depth, N_HEADS, seq, HEAD_W), F32)
    cache2 = jax.ShapeDtypeStruct((n_seq, depth, GQA_KV_HEADS, seq, HEAD_W), F32)
    out_specs = [mixed,
                 pl.BlockSpec((None, None, 2, N_HEADS, HEAD_W, HEAD_W), lambda b: (b, layer, 0, 0, 0, 0)),
                 mixed, mixed, mixed, cache_heads, cache_heads, cache_kv, cache_kv,
                 pl.BlockSpec((None, None, N_HEADS, 2, seq, DF_DQK), lambda b: (b, layer, 0, 0, 0, 0)),
                 cache_heads]
    out_shape = [mixed_shape, jax.ShapeDtypeStruct((n_seq, depth, 2, N_HEADS, HEAD_W, HEAD_W), F32),
                 mixed_shape, mixed_shape, mixed_shape, cache4, cache4, cache2, cache2,
                 jax.ShapeDtypeStruct((n_seq, depth, N_HEADS, 2, seq, DF_DQK), F32), cache4]
    assert len(out_specs) == N_CTX_GROUP_OUTPUTS
    return pl.pallas_call(
        functools.partial(_ctx_group_body, seq=seq, lam_init=lam_init, n_alias=len(prev)),
        grid=(n_seq,),
        in_specs=in_specs + [ANY_SPEC] * len(prev), out_specs=out_specs, out_shape=out_shape,
        input_output_aliases=aliases,
        scratch_shapes=[pltpu.VMEM((seq, GROUP_W), F32), pltpu.VMEM((seq, GROUP_W), F32)],
        compiler_params=_params("parallel"),
        name="context_mixers",
    )(*args, *prev)


def _rope_tables(n_tokens, rot_dim):
    t = np.arange(n_tokens)
    row = (t // GRID_W).astype(np.float32)
    col = (t % GRID_W).astype(np.float32)
    n_freq = rot_dim // 4
    inv = (np.float32(ROPE_THETA) ** (-np.arange(n_freq, dtype=np.float32) / np.float32(n_freq))).astype(np.float32)
    ang = np.concatenate([row[:, None] * inv, col[:, None] * inv], axis=-1).astype(np.float32)
    cos, sin, zero = np.cos(ang), np.sin(ang), np.zeros_like(ang)
    reps = HEAD_W // rot_dim
    a = np.tile(np.concatenate([cos, cos], axis=-1), (1, reps))
    b = np.tile(np.concatenate([-sin, zero], axis=-1), (1, reps))
    c = np.tile(np.concatenate([zero, sin], axis=-1), (1, reps))
    return jnp.asarray(np.stack([a, b, c]), F32)


def _rope(x, tab_ref, half):
    return (x * tab_ref[0] + pltpu.roll(x, HEAD_W - half, 1) * tab_ref[1]
            + pltpu.roll(x, half, 1) * tab_ref[2])


def _na_body(q_ref, k_ref, v_ref, ck_ref, cv_ref, bias_ref, qn_ref, kn_ref, prev_ref, o_ref,
             qs_ref, ks_ref, vs_ref, *, seq):
    del prev_ref
    rows = seq // GRID_W
    n_win = WIN_ROWS * GRID_W
    qs_ref[...] = (_rms_head(q_ref[...], qn_ref[...]) * (LOG2E * HEAD_W ** -0.5)).astype(BF16)
    ks_ref[...] = _rms_head(k_ref[...], kn_ref[...]).astype(BF16)
    vs_ref[...] = _with_ones(v_ref[...].astype(BF16))
    ck = ck_ref[...].astype(BF16)
    cv = _with_ones(cv_ref[...].astype(BF16))

    def row_step(r, carry):
        start = jnp.clip(r - WIN_ROWS // 2, 0, rows - WIN_ROWS)
        win = pl.ds(pl.multiple_of(start * GRID_W, GRID_W), n_win)
        qrows = pl.ds(pl.multiple_of(r * GRID_W, GRID_W), GRID_W)
        q = qs_ref[qrows, :]
        s_win = _dot_nt(q, ks_ref[win, :]) + bias_ref[start - r + (WIN_ROWS - 1)]
        s_ctx = _dot_nt(q, ck)
        o_ref[qrows, :] = _softmax_pv([s_win, s_ctx], [vs_ref[win, :], cv]).astype(o_ref.dtype)
        return carry

    lax.fori_loop(0, rows, row_step, 0, unroll=min(rows, 32))


def _na_bias(rpb):
    col = np.arange(GRID_W)
    col_start = np.clip(col - WIN_COLS // 2, 0, GRID_W - WIN_COLS)
    col_ok = (col[None, :] >= col_start[:, None]) & (col[None, :] < col_start[:, None] + WIN_COLS)
    dc = np.clip(col[None, :] - col[:, None] + WIN_COLS - 1, 0, 2 * WIN_COLS - 2).reshape(-1)
    onehot = (np.arange(2 * WIN_COLS - 1)[:, None] == dc[None, :]).astype(np.float32)
    per_dr = jnp.einsum('hdc,cn->hdn', rpb.astype(F32), jnp.asarray(onehot), precision=lax.Precision.HIGHEST)
    per_dr = jnp.where(col_ok[None, None], LOG2E * per_dr.reshape(rpb.shape[0], -1, GRID_W, GRID_W), NEG_INF)
    wins = jnp.stack([per_dr[:, o:o + WIN_ROWS] for o in range(WIN_ROWS)], axis=1)
    return wins.transpose(0, 1, 3, 2, 4).reshape(rpb.shape[0], WIN_ROWS, GRID_W, WIN_ROWS * GRID_W)


def _latent_na(proj, row_block0, n_seq, seq, cache_k, cache_v, layer, bias, na_qn, na_kn, mixed_prev):
    past = cache_k.shape[3]

    def col(cb):
        return pl.BlockSpec((seq, HEAD_W), lambda b, h, cb=cb: (row_block0 + b, cb + h))

    cache = pl.BlockSpec((None, None, None, past, HEAD_W), lambda b, h: (b, layer, h, 0, 0))
    vec = pl.BlockSpec((1, HEAD_W), lambda b, h: (0, 0))
    return pl.pallas_call(
        functools.partial(_na_body, seq=seq),
        grid=(n_seq, N_HEADS),
        in_specs=[col(COL_B_Q), col(COL_B_K), col(COL_B_V), cache, cache,
                  pl.BlockSpec((None, WIN_ROWS, GRID_W, WIN_ROWS * GRID_W), lambda b, h: (h, 0, 0, 0)),
                  vec, vec, ANY_SPEC],
        out_specs=pl.BlockSpec((seq, HEAD_W), lambda b, h: (row_block0 + b, h)),
        out_shape=jax.ShapeDtypeStruct(mixed_prev.shape, BF16),
        input_output_aliases={8: 0},
        scratch_shapes=[pltpu.VMEM((seq, HEAD_W), BF16), pltpu.VMEM((seq, HEAD_W), BF16),
                        pltpu.VMEM((seq, 2 * HEAD_W), BF16)],
        compiler_params=_params("parallel", "parallel"),
        name="latent_neighbourhood_attention",
    )(proj, proj, proj, cache_k, cache_v, bias, na_qn[None, :], na_kn[None, :], mixed_prev)


def _gqa_body(q_ref, k_ref, v_ref, ck_ref, cv_ref, rope_ref, qn_ref, kn_ref, prev_ref, o_ref,
              qs_ref, ks_ref, vs_ref, *, seq, tq):
    del prev_ref
    group = N_HEADS // GQA_KV_HEADS
    half = HEAD_W // 2
    ks_ref[0:seq, :] = _rope(_rms_head(k_ref[...], kn_ref[...]), rope_ref, half).astype(BF16)
    ks_ref[seq:, :] = ck_ref[...].astype(BF16)
    vs_ref[0:seq, :] = _with_ones(v_ref[...].astype(BF16))
    vs_ref[seq:, :] = _with_ones(cv_ref[...].astype(BF16))
    for g in range(group):
        q = _rms_head(q_ref[:, g * HEAD_W:(g + 1) * HEAD_W], qn_ref[...]) * (LOG2E * HEAD_W ** -0.5)
        qs_ref[g] = _rope(q, rope_ref, half).astype(BF16)
    kk = ks_ref[...]
    vv = vs_ref[...]
    for g in range(group):
        def q_step(i, carry, g=g):
            qrows = pl.ds(pl.multiple_of(i * tq, tq), tq)
            o = _softmax_pv([_dot_nt(qs_ref[g, qrows, :], kk)], [vv])
            o_ref[qrows, g * HEAD_W:(g + 1) * HEAD_W] = o.astype(o_ref.dtype)
            return carry

        lax.fori_loop(0, seq // tq, q_step, 0, unroll=min(seq // tq, 8))


def _latent_gqa(proj, row_block0, n_seq, seq, cache_k, cache_v, layer, rope, gqa_qn, gqa_kn, mixed_prev):
    past = cache_k.shape[3]
    group = N_HEADS // GQA_KV_HEADS
    tq = 256
    cache = pl.BlockSpec((None, None, None, past, HEAD_W), lambda b, n: (b, layer, n, 0, 0))
    vec = pl.BlockSpec((1, HEAD_W), lambda b, n: (0, 0))
    return pl.pallas_call(
        functools.partial(_gqa_body, seq=seq, tq=tq),
        grid=(n_seq, GQA_KV_HEADS),
        in_specs=[pl.BlockSpec((seq, group * HEAD_W), lambda b, n: (row_block0 + b, COL_C_Q // group + n)),
                  pl.BlockSpec((seq, HEAD_W), lambda b, n: (row_block0 + b, COL_C_K + n)),
                  pl.BlockSpec((seq, HEAD_W), lambda b, n: (row_block0 + b, COL_C_V + n)),
                  cache, cache,
                  pl.BlockSpec((3, seq, HEAD_W), lambda b, n: (0, 0, 0)),
                  vec, vec, ANY_SPEC],
        out_specs=pl.BlockSpec((seq, group * HEAD_W), lambda b, n: (row_block0 + b, n)),
        out_shape=jax.ShapeDtypeStruct(mixed_prev.shape, BF16),
        input_output_aliases={8: 0},
        scratch_shapes=[pltpu.VMEM((group, seq, HEAD_W), BF16),
                        pltpu.VMEM((seq + past, HEAD_W), BF16),
                        pltpu.VMEM((seq + past, 2 * HEAD_W), BF16)],
        compiler_params=_params("parallel", "parallel"),
        name="latent_gqa_attention",
    )(proj, proj, proj, cache_k, cache_v, rope, gqa_qn[None, :], gqa_kn[None, :], mixed_prev)


def _diff_body(q_ref, k_ref, v_ref, ck_ref, cv_ref, rope_ref, qn_ref, kn_ref, sub_ref, lam_ref, prev_ref, o_ref,
               qs_ref, k0_ref, k1_ref, vs_ref, *, seq, tq, lam_init):
    del prev_ref
    half = DF_DQK // 2
    k = _rope(_rms_halves(k_ref[...], kn_ref[...]), rope_ref, half)
    lo = _lane_lt(k.shape, DF_DQK)
    k0_ref[0:seq, :] = jnp.where(lo, k, 0.0).astype(BF16)
    k1_ref[0:seq, :] = jnp.where(lo, 0.0, k).astype(BF16)
    ck = ck_ref[...]
    lo_c = _lane_lt(ck.shape, DF_DQK)
    k0_ref[seq:, :] = jnp.where(lo_c, ck, 0.0).astype(BF16)
    k1_ref[seq:, :] = jnp.where(lo_c, 0.0, ck).astype(BF16)
    vs_ref[0:seq, :] = _with_ones(v_ref[...].astype(BF16))
    vs_ref[seq:, :] = _with_ones(cv_ref[...].astype(BF16))
    q = _rms_halves(q_ref[...], qn_ref[...]) * (LOG2E * DF_DQK ** -0.5)
    qs_ref[...] = _rope(q, rope_ref, half).astype(BF16)
    lam = _lambda(lam_ref, lam_init)
    k0 = k0_ref[...]
    k1 = k1_ref[...]
    vv = vs_ref[...]

    def q_step(i, carry):
        qrows = pl.ds(pl.multiple_of(i * tq, tq), tq)
        qb = qs_ref[qrows, :]
        o = _diff_pv(_dot_nt(qb, k0), _dot_nt(qb, k1), lam, vv)
        o_ref[qrows, :] = (_rms_head(o, sub_ref[...]) * (1.0 - lam_init)).astype(o_ref.dtype)
        return carry

    lax.fori_loop(0, seq // tq, q_step, 0, unroll=min(seq // tq, 4))


def _latent_diff(proj, row_block0, n_seq, seq, cache_k2, cache_v, layer, rope, df_qn, df_kn, df_subln, df_lam,
                 lam_init, mixed_prev):
    past = cache_k2.shape[3]
    tq = 256

    def col(cb):
        return pl.BlockSpec((seq, HEAD_W), lambda b, h, cb=cb: (row_block0 + b, cb + h))

    cache = pl.BlockSpec((None, None, None, past, HEAD_W), lambda b, h: (b, layer, h, 0, 0))
    vec = pl.BlockSpec((1, HEAD_W), lambda b, h: (0, 0))
    kv_scratch = pltpu.VMEM((seq + past, HEAD_W), BF16)
    return pl.pallas_call(
        functools.partial(_diff_body, seq=seq, tq=tq, lam_init=lam_init),
        grid=(n_seq, N_HEADS),
        in_specs=[col(COL_D_Q), col(COL_D_K), col(COL_D_V), cache, cache,
                  pl.BlockSpec((3, seq, HEAD_W), lambda b, h: (0, 0, 0)),
                  vec, vec, vec, pl.BlockSpec((4, DF_DQK), lambda b, h: (0, 0)), ANY_SPEC],
        out_specs=pl.BlockSpec((seq, HEAD_W), lambda b, h: (row_block0 + b, h)),
        out_shape=jax.ShapeDtypeStruct(mixed_prev.shape, BF16),
        input_output_aliases={10: 0},
        scratch_shapes=[pltpu.VMEM((seq, HEAD_W), BF16), kv_scratch, kv_scratch,
                        pltpu.VMEM((seq + past, 2 * HEAD_W), BF16)],
        compiler_params=_params("parallel", "parallel"),
        name="latent_diff_attention",
    )(proj, proj, proj, cache_k2, cache_v, rope, jnp.tile(df_qn, 2)[None, :], jnp.tile(df_kn, 2)[None, :],
      df_subln[None, :], df_lam, mixed_prev)


def _first_max(vals):
    best = vals[0]
    idx = jnp.zeros(best.shape, jnp.int32)
    for i in range(1, len(vals)):
        better = vals[i] > best
        best = jnp.where(better, vals[i], best)
        idx = jnp.where(better, i, idx)
    return best, idx


def _pick(vals, idx):
    out = vals[0]
    for i in range(1, len(vals)):
        out = jnp.where(idx == i, vals[i], out)
    return out


def _outproj_body(ma_ref, mb_ref, mc_ref, md_ref, w_ref, hc_ref, hl_ref, mod_ref, n2_ref, rw_ref, rb_ref,
                  h1_ref, x2_ref, idx_ref, gate_ref, mixed_ref, *, ctx_tiles):
    for g, m_ref in enumerate((ma_ref, mb_ref, mc_ref, md_ref)):
        mixed_ref[:, g * GROUP_W:(g + 1) * GROUP_W] = m_ref[...]
    y = mod_ref[2:3, :] * _dot(mixed_ref[...], w_ref[...])

    def residual(h_ref):
        h1_ref[...] = h_ref[...] + y

    is_ctx = pl.program_id(0) < ctx_tiles
    pl.when(is_ctx)(lambda: residual(hc_ref))
    pl.when(jnp.logical_not(is_ctx))(lambda: residual(hl_ref))
    h1 = h1_ref[...]
    x2 = _rms(h1, n2_ref[...], D_MODEL) * (1.0 + mod_ref[4:5, :]) + mod_ref[3:4, :]
    x2_ref[...] = _pack_bf16_pairs(x2)
    x_hi = x2.astype(BF16)
    x_lo = (x2 - x_hi.astype(F32)).astype(BF16)
    acc = _dot(x_hi, rw_ref[...])
    logits_tok = acc[:, 0:HEAD_W] + acc[:, HEAD_W:2 * HEAD_W] + _dot(x_lo, rw_ref[:, 0:HEAD_W])
    logits = logits_tok.T[0:N_EXPERTS, :]
    aff_all = _sigmoid(logits)
    sel_all = aff_all + rb_ref[...]
    aff = [aff_all[e:e + 1, :] for e in range(N_EXPERTS)]
    sel = [sel_all[e:e + 1, :] for e in range(N_EXPERTS)]
    neg = jnp.full(sel[0].shape, -jnp.inf, F32)
    scores = []
    for g in range(N_EXP_GROUPS):
        grp = sel[g * EXP_PER_GROUP:(g + 1) * EXP_PER_GROUP]
        m1, i1 = _first_max(grp)
        m2, _ = _first_max([jnp.where(i1 == j, neg, grp[j]) for j in range(EXP_PER_GROUP)])
        scores.append(m1 + m2)
    _, g_best = _first_max(scores)
    in_sel = [_pick([sel[g * EXP_PER_GROUP + j] for g in range(N_EXP_GROUPS)], g_best)
              for j in range(EXP_PER_GROUP)]
    in_aff = [_pick([aff[g * EXP_PER_GROUP + j] for g in range(N_EXP_GROUPS)], g_best)
              for j in range(EXP_PER_GROUP)]
    _, l1 = _first_max(in_sel)
    _, l2 = _first_max([jnp.where(l1 == j, neg, in_sel[j]) for j in range(EXP_PER_GROUP)])
    w1 = _pick(in_aff, l1)
    w2 = _pick(in_aff, l2)
    idx_ref[0:1, :] = g_best * EXP_PER_GROUP + l1
    idx_ref[1:2, :] = g_best * EXP_PER_GROUP + l2
    gate_ref[0:1, :] = w1 / (w1 + w2)
    gate_ref[1:2, :] = w2 / (w1 + w2)


def _output_projection(mixed4, w_out_bf16, h_ctx, h_lat, mod, norm2, router_w, router_b, cond_of_tile, tm):
    d = h_ctx.shape[1]
    t = h_ctx.shape[0] + h_lat.shape[0]
    ctx_tiles = h_ctx.shape[0] // tm
    slab = pl.BlockSpec((tm, GROUP_W), lambda i: (i, 0))
    rw_hi = router_w.astype(BF16)
    rw_lo = (router_w - rw_hi.astype(F32)).astype(BF16)
    pad = ((0, 0), (0, HEAD_W - N_EXPERTS))
    router_split = jnp.concatenate([jnp.pad(rw_hi, pad), jnp.pad(rw_lo, pad)], axis=1)
    return pl.pallas_call(
        functools.partial(_outproj_body, ctx_tiles=ctx_tiles),
        grid=(t // tm,),
        in_specs=[slab, slab, slab, slab,
                  pl.BlockSpec((d, d), lambda i: (0, 0))] + _group_specs(tm, d, ctx_tiles) + [
                  pl.BlockSpec((None, 6, d), lambda i: (cond_of_tile(i), 0, 0)),
                  pl.BlockSpec((1, d), lambda i: (0, 0)),
                  pl.BlockSpec((d, 2 * HEAD_W), lambda i: (0, 0)),
                  pl.BlockSpec((N_EXPERTS, 1), lambda i: (0, 0))],
        out_specs=[pl.BlockSpec((tm, d), lambda i: (i, 0)),
                   pl.BlockSpec((tm, d // 2), lambda i: (i, 0)),
                   pl.BlockSpec((2, tm), lambda i: (0, i)),
                   pl.BlockSpec((2, tm), lambda i: (0, i))],
        out_shape=[jax.ShapeDtypeStruct((t, d), F32), jax.ShapeDtypeStruct((t, d // 2), jnp.uint32),
                   jax.ShapeDtypeStruct((2, t), jnp.int32), jax.ShapeDtypeStruct((2, t), F32)],
        scratch_shapes=[pltpu.VMEM((tm, d), BF16)],
        compiler_params=_params("parallel", vmem=VMEM_LIMIT_PROJ),
        name="out_proj_residual_router",
    )(*mixed4, w_out_bf16, h_ctx, h_lat, mod, norm2[None, :], router_split, router_b[:, None])


def _vmem_row(ref, base, u):
    return ref.at[pl.ds(base, SUBLANES), :].at[pl.ds(u, 1), :]


def _for_row_groups(n_rows, fn):
    def body(g, carry):
        base = pl.multiple_of(g * SUBLANES, SUBLANES)
        for u in range(SUBLANES):
            fn(base, u)
        return carry

    lax.fori_loop(0, n_rows // SUBLANES, body, 0)


def _rows_wait(src_hbm, dst, sem, n_rows):
    pltpu.make_async_copy(src_hbm.at[pl.ds(0, n_rows), :], dst.at[pl.ds(0, n_rows), :], sem).wait()


def _dispatch_body(dest_ref, ps_ref, pe_ref, x_ref, xs_hbm, zero_ref, sem, zsem, *, n_tok, tile):
    i = pl.program_id(0)

    def zero_copy(e):
        first = pl.multiple_of(pe_ref[e] - MOE_ROWS, MOE_ROWS)
        return pltpu.make_async_copy(zero_ref, xs_hbm.at[pl.ds(first, MOE_ROWS), :], zsem)

    @pl.when(i == 0)
    def _():
        zero_ref[...] = jnp.zeros(zero_ref.shape, zero_ref.dtype)
        for e in range(N_EXPERTS):
            @pl.when(pe_ref[e] > ps_ref[e])
            def _(e=e):
                zero_copy(e).start()
        for e in range(N_EXPERTS):
            @pl.when(pe_ref[e] > ps_ref[e])
            def _(e=e):
                zero_copy(e).wait()

    for k in range(2):
        def scatter_row(base, u, k=k):
            row = dest_ref[k * n_tok + i * tile + base + u]
            pltpu.make_async_copy(_vmem_row(x_ref, base, u), xs_hbm.at[pl.ds(row, 1), :], sem).start()

        _for_row_groups(tile, scatter_row)
    for k in range(2):
        _rows_wait(x_ref, xs_hbm, sem, tile)


def _dispatch(x2, dest, pad_start, pad_end, n_rows):
    t, d = x2.shape
    tile = next(m for m in DISPATCH_TILES if t % m == 0)
    grid_spec = pltpu.PrefetchScalarGridSpec(
        num_scalar_prefetch=3,
        grid=(t // tile,),
        in_specs=[pl.BlockSpec((tile, d), lambda i, dst, ps, pe: (i, 0))],
        out_specs=ANY_SPEC,
        scratch_shapes=[pltpu.VMEM((MOE_ROWS, d), x2.dtype), pltpu.SemaphoreType.DMA, pltpu.SemaphoreType.DMA],
    )
    return pl.pallas_call(
        functools.partial(_dispatch_body, n_tok=t, tile=tile),
        grid_spec=grid_spec,
        out_shape=jax.ShapeDtypeStruct((n_rows, d), x2.dtype),
        compiler_params=_params("arbitrary"),
        name="moe_dispatch",
    )(dest, pad_start, pad_end, x2)


def _expert_body(be_ref, nb_ref, nx_ref, x_ref, wg_hbm, wu_hbm, wd_hbm, o_ref,
                 stage_g, stage_u, stage_d, wg_bf, wu_bf, wd_bf, sem, *, layer):
    i = pl.program_id(0)

    def weight_copies(e):
        return (pltpu.make_async_copy(wg_hbm.at[layer, e], stage_g, sem.at[0]),
                pltpu.make_async_copy(wu_hbm.at[layer, e], stage_u, sem.at[1]),
                pltpu.make_async_copy(wd_hbm.at[layer, e], stage_d, sem.at[2]))

    @pl.when(i < nb_ref[0])
    def _():
        e = be_ref[i]

        @pl.when(i == 0)
        def _():
            for c in weight_copies(e):
                c.start()

        @pl.when((i == 0) | (e != be_ref[jnp.maximum(i - 1, 0)]))
        def _():
            for c in weight_copies(e):
                c.wait()
            wg_bf[...] = stage_g[...].astype(BF16)
            wu_bf[...] = stage_u[...].astype(BF16)
            wd_bf[...] = stage_d[...].astype(BF16)
            nxt = nx_ref[e]

            @pl.when(nxt < N_EXPERTS)
            def _():
                for c in weight_copies(nxt):
                    c.start()

        x = _unpack_bf16_pairs(x_ref[...]).astype(BF16)
        hdn = _silu(_dot(x, wg_bf[...])) * _dot(x, wu_bf[...])
        o_ref[...] = _pack_bf16_pairs(_dot(hdn.astype(BF16), wd_bf[...]))

    @pl.when(i >= nb_ref[0])
    def _():
        o_ref[...] = jnp.zeros(o_ref.shape, o_ref.dtype)


def _expert_blocks(xs, block_expert, n_used, next_expert, w_gate, w_up, w_down, layer):
    n_rows, half = xs.shape
    d = 2 * half
    ff = w_gate.shape[-1]
    grid_spec = pltpu.PrefetchScalarGridSpec(
        num_scalar_prefetch=3,
        grid=(n_rows // MOE_ROWS,),
        in_specs=[pl.BlockSpec((MOE_ROWS, half), lambda i, be, nb, nx: (jnp.minimum(i, nb[0] - 1), 0)),
                  ANY_SPEC, ANY_SPEC, ANY_SPEC],
        out_specs=pl.BlockSpec((MOE_ROWS, half), lambda i, be, nb, nx: (i, 0)),
        scratch_shapes=[pltpu.VMEM((d, ff), F32), pltpu.VMEM((d, ff), F32), pltpu.VMEM((ff, d), F32),
                        pltpu.VMEM((d, ff), BF16), pltpu.VMEM((d, ff), BF16), pltpu.VMEM((ff, d), BF16),
                        pltpu.SemaphoreType.DMA((3,))],
    )
    return pl.pallas_call(
        functools.partial(_expert_body, layer=layer),
        grid_spec=grid_spec,
        out_shape=jax.ShapeDtypeStruct((n_rows, half), jnp.uint32),
        compiler_params=_params("arbitrary"),
        name="moe_expert_blocks",
    )(block_expert, n_used, next_expert, xs, w_gate, w_up, w_down)


def _combine_body(dest_ref, h_ref, gate_ref, y_hbm, mod_ref, *rest, n_tok, tile, ctx_tiles):
    out_refs, (ybuf, sem) = rest[:-2], rest[-2:]
    i = pl.program_id(0)
    n_tiles = pl.num_programs(0)

    def start(blk, slot):
        for k in range(2):
            def gather_row(base, u, k=k):
                row = dest_ref[k * n_tok + blk * tile + base + u]
                pltpu.make_async_copy(y_hbm.at[pl.ds(row, 1), :], _vmem_row(ybuf.at[slot, k], base, u),
                                      sem.at[slot]).start()

            _for_row_groups(tile, gather_row)

    @pl.when(i == 0)
    def _():
        start(0, 0)

    @pl.when(i + 1 < n_tiles)
    def _():
        start(i + 1, (i + 1) % 2)

    slot = i % 2
    for k in range(2):
        _rows_wait(y_hbm, ybuf.at[slot, k], sem.at[slot], tile)
    gate = gate_ref[...]
    y0 = _unpack_bf16_pairs(ybuf[slot, 0])
    y1 = _unpack_bf16_pairs(ybuf[slot, 1])
    out = h_ref[...] + mod_ref[5:6, :] * (gate[:, 0:1] * y0 + gate[:, 1:2] * y1)
    @pl.when(i < ctx_tiles)
    def _():
        out_refs[0][...] = out

    @pl.when(i >= ctx_tiles)
    def _():
        out_refs[1][...] = out


def _combine(h1, yb, dest, gates, mod, cond_of_tile, tile, split_rows):
    t, d = h1.shape
    row_tile = pl.BlockSpec((tile, d), lambda i, dst: (i, 0))
    ctx_tiles = split_rows // tile
    out_specs = _group_specs(tile, d, ctx_tiles)
    out_shape = [jax.ShapeDtypeStruct((split_rows, d), F32), jax.ShapeDtypeStruct((t - split_rows, d), F32)]
    grid_spec = pltpu.PrefetchScalarGridSpec(
        num_scalar_prefetch=1,
        grid=(t // tile,),
        in_specs=[row_tile,
                  pl.BlockSpec((tile, 2), lambda i, dst: (i, 0)),
                  ANY_SPEC,
                  pl.BlockSpec((None, 6, d), lambda i, dst: (cond_of_tile(i), 0, 0))],
        out_specs=out_specs,
        scratch_shapes=[pltpu.VMEM((2, 2, tile, yb.shape[1]), yb.dtype), pltpu.SemaphoreType.DMA((2,))],
    )
    return pl.pallas_call(
        functools.partial(_combine_body, n_tok=t, tile=tile, ctx_tiles=ctx_tiles),
        grid_spec=grid_spec,
        out_shape=out_shape,
        compiler_params=_params("arbitrary"),
        name="moe_gated_residual",
    )(dest, h1, gates, yb, mod)


def _moe(h1, x2, idx_t, gate_t, mod, w_gate, w_up, w_down, layer, cond_of_tile, split_rows):
    t, d = h1.shape
    n = 2 * t
    experts = idx_t.reshape(n)
    onehot = (experts[:, None] == jnp.arange(N_EXPERTS, dtype=jnp.int32)[None, :]).astype(BF16)
    blocks = onehot.reshape(n // COMBINE_TILE, COMBINE_TILE, N_EXPERTS)
    tri = jnp.asarray(np.tril(np.ones((COMBINE_TILE, COMBINE_TILE), np.float32)), BF16)
    within = jnp.einsum('ij,bjk->bik', tri, blocks, preferred_element_type=F32)
    block_total = within[:, -1, :]
    block_first = jnp.cumsum(block_total, axis=0) - block_total
    counts = (block_first[-1] + block_total[-1]).astype(jnp.int32)
    before = (within + block_first[:, None, :]).reshape(n, N_EXPERTS) - 1.0
    rank = jnp.sum(before * onehot.astype(F32), axis=1).astype(jnp.int32)
    padded = (counts + MOE_ROWS - 1) // MOE_ROWS * MOE_ROWS
    pad_end = jnp.cumsum(padded).astype(jnp.int32)
    pad_start = pad_end - padded
    dest = (pad_start[experts] + rank).astype(jnp.int32)
    n_blocks = (n + N_EXPERTS * (MOE_ROWS - 1) + MOE_ROWS - 1) // MOE_ROWS
    block_first_row = jnp.arange(n_blocks, dtype=jnp.int32) * MOE_ROWS
    block_expert = jnp.minimum(jnp.sum((pad_end[None, :] <= block_first_row[:, None]).astype(jnp.int32), axis=1),
                               N_EXPERTS - 1)
    n_used = (pad_end[-1:] // MOE_ROWS).astype(jnp.int32)
    xs = _dispatch(x2, dest, pad_start, pad_end, n_blocks * MOE_ROWS)
    ids = jnp.arange(N_EXPERTS, dtype=jnp.int32)
    later_with_rows = (counts[None, :] > 0) & (ids[None, :] > ids[:, None])
    next_expert = jnp.min(jnp.where(later_with_rows, ids[None, :], N_EXPERTS), axis=1).astype(jnp.int32)
    yb = _expert_blocks(xs, block_expert, n_used, next_expert, w_gate, w_up, w_down, layer)
    return _combine(h1, yb, dest, gate_t.T, mod, cond_of_tile, COMBINE_TILE, split_rows)


def kernel(x_prompt, x_sample, cache_na_k, cache_na_v, cache_gqa_k, cache_gqa_v, cache_diff_k, cache_diff_v, state_hgrn, c, c_ctx, w_mod, b_mod, norm1, norm2, w_in, w_out, hg_lb_logits, hg_onorm, na_qn, na_kn, na_rpb, gqa_qn, gqa_kn, df_qn, df_kn, df_lam, df_subln, router_w, router_b, w_gate, w_up, w_down):
    n_ctx, ctx_len, d = x_prompt.shape
    n_lat, lat_len, _ = x_sample.shape
    depth = w_in.shape[0]
    t_ctx = n_ctx * ctx_len
    assert t_ctx % lat_len == 0 and lat_len % GRID_W == 0 and lat_len // GRID_W >= WIN_ROWS
    tm = next(m for m in (1024, 512, 256) if t_ctx % m == 0 and lat_len % m == 0)
    tm2 = min(tm, 512)
    lat_block0 = t_ctx // lat_len

    def cond_tile(tile_rows):
        def cond_of_tile(i):
            return jnp.where(i < t_ctx // tile_rows, 0, 1 + (i - t_ctx // tile_rows) // (lat_len // tile_rows))
        return cond_of_tile

    sm = jax.nn.softmax(hg_lb_logits.astype(F32), axis=0)
    lower = jnp.cumsum(sm, axis=0) - sm[0:1]
    mod_all = _modulation(jnp.concatenate([c_ctx[None, :], c], axis=0), w_mod, b_mod)
    mod_all = mod_all.reshape(depth, 1 + n_lat, 6, d)
    hgrn_consts = _hgrn_constants(HGRN_CHUNK)
    rope_c = _rope_tables(lat_len, HEAD_W)
    rope_d = _rope_tables(lat_len, DF_DQK)
    past = cache_diff_k.shape[4]
    cache_diff_k2 = cache_diff_k.transpose(0, 1, 2, 4, 3, 5).reshape(n_lat, depth, N_HEADS, past, HEAD_W)

    h_ctx, h_lat = x_prompt.reshape(t_ctx, d), x_sample.reshape(n_lat * lat_len, d)
    caches, states = [], None
    for layer in range(depth):
        mod = mod_all[layer]
        lam_init = 0.8 - 0.6 * math.exp(-0.3 * layer)
        proj = _input_projection(h_ctx, h_lat, mod, norm1[layer], w_in[layer].astype(BF16), cond_tile(tm), tm)
        gains = (na_qn[layer], na_kn[layer], gqa_qn[layer], gqa_kn[layer], df_qn[layer], df_kn[layer],
                 df_subln[layer], df_lam[layer])
        mix_a, states, mix_b, mix_c, mix_d, *caches = _context_group(
            proj, n_ctx, ctx_len, lower[layer], hg_onorm[layer], hgrn_consts, gains, lam_init, layer, depth,
            states, caches)
        mix_a, _ = _hgrn(proj, lat_block0, n_lat, lat_len, lower[layer], hg_onorm[layer], hgrn_consts,
                         state_hgrn, mix_a, None, layer, depth)
        mix_b = _latent_na(proj, lat_block0, n_lat, lat_len, cache_na_k, cache_na_v, layer,
                           _na_bias(na_rpb[layer]), na_qn[layer], na_kn[layer], mix_b)
        mix_c = _latent_gqa(proj, lat_block0, n_lat, lat_len, cache_gqa_k, cache_gqa_v, layer, rope_c,
                            gqa_qn[layer], gqa_kn[layer], mix_c)
        mix_d = _latent_diff(proj, lat_block0, n_lat, lat_len, cache_diff_k2, cache_diff_v, layer, rope_d,
                             df_qn[layer], df_kn[layer], df_subln[layer], df_lam[layer], lam_init, mix_d)
        h1, x2, idx_t, gate_t = _output_projection((mix_a, mix_b, mix_c, mix_d), w_out[layer].astype(BF16), h_ctx,
                                                   h_lat, mod, norm2[layer], router_w, router_b, cond_tile(tm2), tm2)
        h_ctx, h_lat = _moe(h1, x2, idx_t, gate_t, mod, w_gate, w_up, w_down, layer, cond_tile(COMBINE_TILE), t_ctx)
    y_prompt = h_ctx.reshape(n_ctx, ctx_len, d)
    y_sample = h_lat.reshape(n_lat, lat_len, d)
    return (y_prompt, y_sample, *caches, states)
```

```python
import functools
import math

import numpy as np
import jax
import jax.numpy as jnp
from jax import lax
from jax.experimental import pallas as pl
from jax.experimental.pallas import tpu as pltpu

D_MODEL = 2048
GRID_W = 64
GROUP_W = D_MODEL // 4
N_HEADS = 4
HEAD_W = GROUP_W // N_HEADS
SUBLANES = 8
GQA_KV_HEADS = 2
DF_DQK = HEAD_W // 2
WIN_ROWS = 8
WIN_COLS = 16
N_EXPERTS = 16
N_EXP_GROUPS = 4
EXP_PER_GROUP = N_EXPERTS // N_EXP_GROUPS
EXPERT_FF = D_MODEL // 4
ROPE_THETA = 10000.0
EPS = 1e-6
NEG_INF = -1e30
LOG2E = math.log2(math.e)
IN_WIDTH = 13 * GROUP_W

COL_A_Q, COL_A_FF, COL_A_FB, COL_A_I, COL_A_G = 0, 4, 8, 12, 16
COL_B_Q, COL_B_K, COL_B_V = 20, 24, 28
COL_C_Q, COL_C_K, COL_C_V = 32, 36, 38
COL_D_Q, COL_D_K, COL_D_V = 40, 44, 48

HGRN_CHUNK = 128
HGRN_UNROLL = 4
HGRN_INPUT_VMEM = 24 * 1024 * 1024
MOE_ROWS = 256
DISPATCH_TILES = (1024, 512, 256)
COMBINE_TILE = 256
VMEM_LIMIT = 48 * 1024 * 1024
VMEM_LIMIT_PROJ = 56 * 1024 * 1024

F32 = jnp.float32
BF16 = jnp.bfloat16
ANY_SPEC = pl.BlockSpec(memory_space=pl.ANY)


def _params(*sem, vmem=VMEM_LIMIT):
    return pltpu.CompilerParams(dimension_semantics=sem, vmem_limit_bytes=vmem)


def _sigmoid(x):
    return 1.0 / (1.0 + jnp.exp(-x))


def _silu(x):
    return x * _sigmoid(x)


def _rms(x, gain, n):
    return x * lax.rsqrt(jnp.sum(x * x, axis=-1, keepdims=True) * (1.0 / n) + EPS) * gain


def _dot(a, b):
    return jnp.dot(a, b, preferred_element_type=F32)


def _dot_nt(a, b):
    return lax.dot_general(a, b, (((1,), (1,)), ((), ())), preferred_element_type=F32)


def _dot_tn(a, b):
    return lax.dot_general(a, b, (((0,), (0,)), ((), ())), preferred_element_type=F32)


def _pack_bf16_pairs(x):
    k = x.shape[1] // 2
    lo = lax.bitcast_convert_type(x[:, :k].astype(BF16).astype(F32), jnp.uint32) >> 16
    hi = lax.bitcast_convert_type(x[:, k:].astype(BF16).astype(F32), jnp.uint32)
    return hi | lo


def _unpack_bf16_pairs(w):
    lo = lax.bitcast_convert_type(w << 16, F32)
    hi = lax.bitcast_convert_type(w & jnp.uint32(0xFFFF0000), F32)
    return jnp.concatenate([lo, hi], axis=1)


def _aligned(x, m):
    return x if isinstance(x, int) else pl.multiple_of(x, m)


def _alias_kwargs(n_inputs, prev, first_out):
    return ([ANY_SPEC] * len(prev), list(prev), {n_inputs + k: first_out + k for k in range(len(prev))})


def _mod_body(cond_ref, w_ref, b_ref, o_ref):
    w = w_ref[...]
    for c in range(cond_ref.shape[0]):
        s = _silu(cond_ref[c])
        o_ref[c:c + 1, :] = jnp.sum(w * s, axis=0, keepdims=True) + b_ref[...]


def _modulation(cond, w_mod, b_mod):
    depth, d, n6 = w_mod.shape
    nc = cond.shape[0]
    tn = 1024
    return pl.pallas_call(
        _mod_body,
        grid=(depth, n6 // tn),
        in_specs=[pl.BlockSpec((nc, d, 1), lambda l, j: (0, 0, 0)),
                  pl.BlockSpec((None, d, tn), lambda l, j: (l, 0, j)),
                  pl.BlockSpec((None, 1, tn), lambda l, j: (l, 0, j))],
        out_specs=pl.BlockSpec((None, nc, tn), lambda l, j: (l, 0, j)),
        out_shape=jax.ShapeDtypeStruct((depth, nc, n6), F32),
        compiler_params=_params("parallel", "parallel"),
        name="adaln_modulation",
    )(cond[:, :, None], w_mod, b_mod[:, None, :])


def _group_specs(tile, d, ctx_tiles):
    return [pl.BlockSpec((tile, d), lambda i, *_: (jnp.minimum(i, ctx_tiles - 1), 0)),
            pl.BlockSpec((tile, d), lambda i, *_: (jnp.maximum(i - ctx_tiles, 0), 0))]


INPROJ_NORM_CHUNKS = 8


def _inproj_body(h0_ref, mod0_ref, hc_ref, hl_ref, mod_ref, n1_ref, w_ref, o_ref, xn_ref, *, ctx_tiles, chunk):
    i = pl.program_id(0)
    j = pl.program_id(1)

    def normalise(h, mod):
        return (_rms(h, n1_ref[...], D_MODEL) * (1.0 + mod[1:2, :]) + mod[0:1, :]).astype(BF16)

    @pl.when((i == 0) & (j == 0))
    def _():
        xn_ref[0] = normalise(h0_ref[...], mod0_ref[...])

    nxt = i + 1
    rows = pl.ds(pl.multiple_of(jnp.minimum(j, INPROJ_NORM_CHUNKS - 1) * chunk, chunk), chunk)
    xn_ref[nxt % 2, rows, :] = normalise(jnp.where(nxt < ctx_tiles, hc_ref[...], hl_ref[...]), mod_ref[...])
    o_ref[...] = _dot(xn_ref[i % 2], w_ref[...])


def _input_projection(h_ctx, h_lat, mod, norm1, w_in_bf16, cond_of_tile, tm):
    d = h_ctx.shape[1]
    t = h_ctx.shape[0] + h_lat.shape[0]
    n_tiles, ctx_tiles, lat_tiles = t // tm, h_ctx.shape[0] // tm, h_lat.shape[0] // tm
    n = w_in_bf16.shape[1]
    tn = 512
    pieces = INPROJ_NORM_CHUNKS
    chunk = tm // pieces
    assert n // tn >= pieces and ctx_tiles >= 1

    def piece(j):
        return jnp.minimum(j, pieces - 1)

    def nxt(i):
        return jnp.minimum(i + 1, n_tiles - 1)

    ctx_chunk = lambda i, j: (jnp.where(nxt(i) < ctx_tiles, nxt(i) * pieces + piece(j), ctx_tiles * pieces - 1), 0)
    lat_chunk = lambda i, j: (jnp.where(nxt(i) >= ctx_tiles, (nxt(i) - ctx_tiles) * pieces + piece(j), 0), 0)
    return pl.pallas_call(
        functools.partial(_inproj_body, ctx_tiles=ctx_tiles, chunk=chunk),
        grid=(n_tiles, n // tn),
        in_specs=[pl.BlockSpec((tm, d), lambda i, j: (0, 0)),
                  pl.BlockSpec((None, 6, d), lambda i, j: (cond_of_tile(0), 0, 0)),
                  pl.BlockSpec((chunk, d), ctx_chunk),
                  pl.BlockSpec((chunk, d), lat_chunk),
                  pl.BlockSpec((None, 6, d), lambda i, j: (cond_of_tile(nxt(i)), 0, 0)),
                  pl.BlockSpec((1, d), lambda i, j: (0, 0)),
                  pl.BlockSpec((d, tn), lambda i, j: (0, j))],
        out_specs=pl.BlockSpec((tm, tn), lambda i, j: (i, j)),
        out_shape=jax.ShapeDtypeStruct((t, n), F32),
        scratch_shapes=[pltpu.VMEM((2, tm, d), BF16)],
        compiler_params=_params("arbitrary", "arbitrary"),
        name="norm_modulate_in_proj",
    )(h_ctx, mod, h_ctx, h_lat, mod, norm1[None, :], w_in_bf16)


def _hgrn_constants(c):
    nl = int(math.log2(c))
    idx = np.arange(c)
    e = np.zeros((nl + 2, c, c), np.float32)
    m = np.zeros((nl + 1, c, c), np.float32)
    e[0] = idx[None, :] <= idx[:, None]
    e[1] = idx[None, :] > idx[:, None]
    m[0] = np.eye(c)
    for li in range(nl):
        s = c >> (li + 1)
        parent = idx // (2 * s)
        right = (idx % (2 * s)) >= s
        ref = parent * 2 * s + s - 1
        for i in range(c):
            if right[i]:
                e[2 + li, i, ref[i] + 1:i + 1] = 1.0
            else:
                e[2 + li, i, i + 1:ref[i] + 1] = 1.0
        m[1 + li] = right[:, None] & ~right[None, :] & (parent[:, None] == parent[None, :])
    keep = [0, 1] + [2 + li for li in range(nl) if (c >> (li + 1)) < SUBLANES]
    e = e[keep]
    e2 = np.stack([e, e[:, ::-1, ::-1]]).reshape(2, len(keep) * c, c)
    m2 = np.stack([m, m[:, ::-1, ::-1]])
    return jnp.asarray(e2, BF16), jnp.asarray(m2, F32)


def _hgrn_body(*refs, seq, chunk, unroll, heads, has_s0, emit_state, n_alias):
    q_ref, ff_ref, fb_ref, i_ref, g_ref, lb_ref, on_ref, e_ref, m_ref = refs[:9]
    pos = 9
    s0_ref = None
    if has_s0:
        s0_ref = refs[pos]
        pos += 1
    pos += n_alias
    o_ref = refs[pos]
    pos += 1
    if emit_state:
        st_ref = refs[pos]
        pos += 1
    of_ref, ob_ref = refs[pos], refs[pos + 1]
    c = chunk
    n_chunks = seq // c
    assert seq % c == 0 and n_chunks % unroll == 0
    n_levels = m_ref.shape[1] - 1
    gate_refs = (ff_ref, fb_ref)
    out_refs = (of_ref, ob_ref)

    def chunk_step(c0, d, hh, st):
        rows = pl.ds(c0, c)
        lanes = slice(hh * HEAD_W, (hh + 1) * HEAD_W)
        lb = lb_ref[d:d + 1, lanes]
        f = lb + (1.0 - lb) * _sigmoid(gate_refs[d][rows, lanes])
        g = jnp.log2(f)
        k = 1.0 - f
        q = _silu(q_ref[rows, lanes])
        v = i_ref[rows, lanes].astype(BF16)
        g_hi = g.astype(BF16)
        g_lo = (g - g_hi.astype(F32)).astype(BF16)
        g2 = _dot(e_ref[d], jnp.concatenate([g_hi, g_lo], axis=1))
        gsum = g2[:, 0:HEAD_W] + g2[:, HEAD_W:2 * HEAD_W]
        cum = gsum[0:c]
        x_cum = jnp.exp2(cum)
        x_tail = jnp.exp2(gsum[c:2 * c])
        row = lax.broadcasted_iota(jnp.int32, (c, HEAD_W), 0)
        s = m_ref[d, 0] * _dot_nt(q.astype(BF16), k.astype(BF16))
        n_matmul_levels = 0
        for lv in range(n_levels):
            half = c >> (lv + 1)
            if half >= SUBLANES:
                ref_row = half - 1 if d == 0 else half
                ref = jnp.concatenate(
                    [jnp.broadcast_to(cum[b * 2 * half + ref_row:b * 2 * half + ref_row + 1, :], (2 * half, HEAD_W))
                     for b in range(c // (2 * half))], axis=0)
                near = ((row % (2 * half)) >= half) == (d == 0)
                x_l = jnp.exp2(jnp.where(near, cum - ref, ref - cum))
            else:
                x_l = jnp.exp2(gsum[(2 + n_matmul_levels) * c:(3 + n_matmul_levels) * c])
                n_matmul_levels += 1
            s = s + m_ref[d, 1 + lv] * _dot_nt((q * x_l).astype(BF16), (k * x_l).astype(BF16))
        o = _dot_nt((q * x_cum).astype(BF16), st.astype(BF16)) + _dot(s.astype(BF16), v)
        out_refs[d][rows, lanes] = o
        total = x_cum[c - 1:c, :] if d == 0 else x_cum[0:1, :]
        return st * total + _dot_tn(v, (k * x_tail).astype(BF16))

    if has_s0:
        states0 = tuple(s0_ref[d, hh].T for hh in range(heads) for d in range(2))
    else:
        states0 = tuple(jnp.zeros((HEAD_W, HEAD_W), F32) for _ in range(2 * heads))

    def loop(t, states):
        states = list(states)
        for u in range(unroll):
            j = t * unroll + u
            for hh in range(heads):
                states[2 * hh] = chunk_step(_aligned(j * c, c), 0, hh, states[2 * hh])
                states[2 * hh + 1] = chunk_step(_aligned((n_chunks - 1 - j) * c, c), 1, hh, states[2 * hh + 1])
        return tuple(states)

    if n_chunks == unroll:
        states = loop(0, states0)
    else:
        states = lax.fori_loop(0, n_chunks // unroll, loop, states0)
    for hh in range(heads):
        lanes = slice(hh * HEAD_W, (hh + 1) * HEAD_W)
        o = of_ref[:, lanes] + ob_ref[:, lanes]
        o_ref[:, lanes] = (_rms(o, on_ref[...], HEAD_W) * _silu(g_ref[:, lanes])).astype(o_ref.dtype)
        if emit_state:
            st_ref[0, hh] = states[2 * hh].T
            st_ref[1, hh] = states[2 * hh + 1].T


def _hgrn(proj, row_block0, n_seq, seq, lower, onorm, consts, s0, mixed_prev, state_prev, layer, depth):
    e_mat, masks = consts
    latent = s0 is not None
    hps = next(n for n in (4, 2, 1) if 5 * 2 * seq * n * HEAD_W * 4 <= HGRN_INPUT_VMEM)
    width = hps * HEAD_W

    def col(cb):
        return pl.BlockSpec((seq, width), lambda b, h, cb=cb: (row_block0 + b, cb // hps + h))

    state_spec = pl.BlockSpec((None, None, 2, hps, HEAD_W, HEAD_W), lambda b, h: (b, layer, 0, h, 0, 0))
    in_specs = [col(COL_A_Q), col(COL_A_FF), col(COL_A_FB), col(COL_A_I), col(COL_A_G),
                pl.BlockSpec((2, width), lambda b, h: (0, h)),
                pl.BlockSpec((1, HEAD_W), lambda b, h: (0, 0)),
                pl.BlockSpec(e_mat.shape, lambda b, h: (0, 0, 0)),
                pl.BlockSpec(masks.shape, lambda b, h: (0, 0, 0, 0))]
    args = [proj, proj, proj, proj, proj, lower, onorm[None, :], e_mat, masks]
    if latent:
        in_specs.append(state_spec)
        args.append(s0)
        prev = [mixed_prev]
    else:
        prev = [] if state_prev is None else [state_prev]
    alias_specs, alias_args, aliases = _alias_kwargs(len(args), prev, 0 if latent else 1)
    out_specs = [pl.BlockSpec((seq, width), lambda b, h: (row_block0 + b, h))]
    out_shape = [jax.ShapeDtypeStruct((proj.shape[0], GROUP_W), BF16)]
    if not latent:
        out_specs.append(state_spec)
        out_shape.append(jax.ShapeDtypeStruct((n_seq, depth, 2, N_HEADS, HEAD_W, HEAD_W), F32))
    res = pl.pallas_call(
        functools.partial(_hgrn_body, seq=seq, chunk=HGRN_CHUNK, unroll=min(HGRN_UNROLL, seq // HGRN_CHUNK),
                          heads=hps, has_s0=latent,
                          emit_state=not latent, n_alias=len(prev)),
        grid=(n_seq, N_HEADS // hps),
        in_specs=in_specs + alias_specs, out_specs=out_specs, out_shape=out_shape,
        input_output_aliases=aliases,
        scratch_shapes=[pltpu.VMEM((seq, width), F32), pltpu.VMEM((seq, width), F32)],
        compiler_params=_params("parallel", "parallel"),
        name="hgrn2_latent" if latent else "hgrn2_context",
    )(*args, *alias_args)
    return (res[0], None) if latent else res


def _with_ones(v):
    return jnp.concatenate([v, jnp.ones_like(v)], axis=1)


def _softmax_pv(scores, values1):
    mx = functools.reduce(jnp.maximum, [jnp.max(s, axis=-1, keepdims=True) for s in scores])
    acc = functools.reduce(lambda a, b: a + b,
                           [_dot(jnp.exp2(s - mx).astype(BF16), v) for s, v in zip(scores, values1)])
    return acc[:, 0:HEAD_W] / acc[:, HEAD_W:HEAD_W + 1]


def _diff_pv(s0, s1, lam, values1):
    return _softmax_pv([s0], [values1]) - lam * _softmax_pv([s1], [values1])


def _lane_lt(shape, n):
    return lax.broadcasted_iota(jnp.int32, shape, len(shape) - 1) < n


def _rms_head(x, gain):
    return _rms(x, gain, HEAD_W)


def _rms_halves(x, gain2):
    lo = _lane_lt(x.shape, DF_DQK)
    sq = x * x
    ss_lo = jnp.sum(jnp.where(lo, sq, 0.0), axis=-1, keepdims=True)
    ss_hi = jnp.sum(sq, axis=-1, keepdims=True) - ss_lo
    inv = jnp.where(lo, lax.rsqrt(ss_lo * (1.0 / DF_DQK) + EPS), lax.rsqrt(ss_hi * (1.0 / DF_DQK) + EPS))
    return x * inv * gain2


def _lambda(lam_ref, lam_init):
    l = lam_ref[...]
    return (jnp.exp(jnp.sum(l[0:1] * l[1:2], axis=-1, keepdims=True))
            - jnp.exp(jnp.sum(l[2:3] * l[3:4], axis=-1, keepdims=True)) + lam_init)


N_CTX_ATTN_INPUTS = 17


def _ctx_attn_body(*refs, lam_init, kv_heads):
    (bq_ref, bk_ref, bv_ref, cq_ref, ck_ref, cv_ref, dq_ref, dk_ref, dv_ref,
     naq_ref, nak_ref, gq_ref, gk_ref, dfq_ref, dfk_ref, sub_ref, lam_ref) = refs[:N_CTX_ATTN_INPUTS]
    ob_ref, oc_ref, od_ref, kb_ref, vb_ref, kc_ref, vc_ref, kd_ref, vd_ref = refs[-9:]
    group = N_HEADS // GQA_KV_HEADS
    scale = LOG2E * HEAD_W ** -0.5
    lam = _lambda(lam_ref, lam_init)
    for n in range(kv_heads):
        kv_lanes = slice(n * HEAD_W, (n + 1) * HEAD_W)
        kc = _rms_head(ck_ref[:, kv_lanes], gk_ref[...])
        vc = cv_ref[:, kv_lanes]
        kc_ref[n] = kc
        vc_ref[n] = vc
        kc16 = kc.astype(BF16)
        vc1 = _with_ones(vc.astype(BF16))
        for g in range(group):
            h = n * group + g
            lanes = slice(h * HEAD_W, (h + 1) * HEAD_W)
            qc = (_rms_head(cq_ref[:, lanes], gq_ref[...]) * scale).astype(BF16)
            oc_ref[:, lanes] = _softmax_pv([_dot_nt(qc, kc16)], [vc1]).astype(oc_ref.dtype)
            kb = _rms_head(bk_ref[:, lanes], nak_ref[...])
            vb = bv_ref[:, lanes]
            kb_ref[h] = kb
            vb_ref[h] = vb
            qb = (_rms_head(bq_ref[:, lanes], naq_ref[...]) * scale).astype(BF16)
            ob_ref[:, lanes] = _softmax_pv([_dot_nt(qb, kb.astype(BF16))],
                                           [_with_ones(vb.astype(BF16))]).astype(ob_ref.dtype)
            kd = _rms_halves(dk_ref[:, lanes], dfk_ref[...])
            vd = dv_ref[:, lanes]
            kd_ref[h, 0] = kd[:, 0:DF_DQK]
            kd_ref[h, 1] = kd[:, DF_DQK:2 * DF_DQK]
            vd_ref[h] = vd
            qd = (_rms_halves(dq_ref[:, lanes], dfq_ref[...]) * (LOG2E * DF_DQK ** -0.5)).astype(BF16)
            lo = _lane_lt(kd.shape, DF_DQK)
            s0 = _dot_nt(qd, jnp.where(lo, kd, 0.0).astype(BF16))
            s1 = _dot_nt(qd, jnp.where(lo, 0.0, kd).astype(BF16))
            od = _diff_pv(s0, s1, lam, _with_ones(vd.astype(BF16)))
            od_ref[:, lanes] = (_rms_head(od, sub_ref[...]) * (1.0 - lam_init)).astype(od_ref.dtype)


N_HGRN_INPUTS = 9
N_CTX_GROUP_OUTPUTS = 11


def _ctx_group_body(*refs, seq, lam_init, n_alias):
    hg_in = refs[:N_HGRN_INPUTS]
    at_in = refs[N_HGRN_INPUTS:N_HGRN_INPUTS + N_CTX_ATTN_INPUTS]
    first_out = N_HGRN_INPUTS + N_CTX_ATTN_INPUTS + n_alias
    outs = refs[first_out:first_out + N_CTX_GROUP_OUTPUTS]
    scratch = refs[first_out + N_CTX_GROUP_OUTPUTS:]
    _hgrn_body(*hg_in, outs[0], outs[1], *scratch, seq=seq, chunk=HGRN_CHUNK,
               unroll=min(HGRN_UNROLL, seq // HGRN_CHUNK), heads=N_HEADS, has_s0=False, emit_state=True, n_alias=0)
    _ctx_attn_body(*at_in, *outs[2:], lam_init=lam_init, kv_heads=GQA_KV_HEADS)


def _context_group(proj, n_seq, seq, lower, onorm, consts, gains, lam_init, layer, depth, state_prev, caches_prev):
    e_mat, masks = consts
    na_qn, na_kn, gqa_qn, gqa_kn, df_qn, df_kn, df_subln, df_lam = gains

    def heads(cb):
        return pl.BlockSpec((seq, GROUP_W), lambda b, cb=cb: (b, cb // N_HEADS))

    def kv_heads(cb):
        return pl.BlockSpec((seq, GQA_KV_HEADS * HEAD_W), lambda b, cb=cb: (b, cb // GQA_KV_HEADS))

    def const(shape):
        return pl.BlockSpec(shape, lambda b: (0,) * len(shape))

    vec = const((1, HEAD_W))
    in_specs = [heads(COL_A_Q), heads(COL_A_FF), heads(COL_A_FB), heads(COL_A_I), heads(COL_A_G),
                const((2, GROUP_W)), vec, const(e_mat.shape), const(masks.shape),
                heads(COL_B_Q), heads(COL_B_K), heads(COL_B_V),
                heads(COL_C_Q), kv_heads(COL_C_K), kv_heads(COL_C_V),
                heads(COL_D_Q), heads(COL_D_K), heads(COL_D_V),
                vec, vec, vec, vec, vec, vec, vec, const((4, DF_DQK))]
    args = [proj] * 5 + [lower, onorm[None, :], e_mat, masks] + [proj] * 9 + [
        na_qn[None, :], na_kn[None, :], gqa_qn[None, :], gqa_kn[None, :],
        jnp.tile(df_qn, 2)[None, :], jnp.tile(df_kn, 2)[None, :], df_subln[None, :], df_lam]
    assert len(args) == N_HGRN_INPUTS + N_CTX_ATTN_INPUTS
    prev = [] if state_prev is None else [state_prev] + list(caches_prev)
    aliases = {len(args) + k: out for k, out in enumerate([1, 5, 6, 7, 8, 9, 10][:len(prev)])}
    mixed = pl.BlockSpec((seq, GROUP_W), lambda b: (b, 0))
    mixed_shape = jax.ShapeDtypeStruct((proj.shape[0], GROUP_W), BF16)
    cache_heads = pl.BlockSpec((None, None, N_HEADS, seq, HEAD_W), lambda b: (b, layer, 0, 0, 0))
    cache_kv = pl.BlockSpec((None, None, GQA_KV_HEADS, seq, HEAD_W), lambda b: (b, layer, 0, 0, 0))
    cache4 = jax.ShapeDtypeStruct((n_seq, depth, N_HEADS, seq, HEAD_W), F32)
    cache2 = jax.ShapeDtypeStruct((n_seq, depth, GQA_KV_HEADS, seq, HEAD_W), F32)
    out_specs = [mixed,
                 pl.BlockSpec((None, None, 2, N_HEADS, HEAD_W, HEAD_W), lambda b: (b, layer, 0, 0, 0, 0)),
                 mixed, mixed, mixed, cache_heads, cache_heads, cache_kv, cache_kv,
                 pl.BlockSpec((None, None, N_HEADS, 2, seq, DF_DQK), lambda b: (b, layer, 0, 0, 0, 0)),
                 cache_heads]
    out_shape = [mixed_shape, jax.ShapeDtypeStruct((n_seq, depth, 2, N_HEADS, HEAD_W, HEAD_W), F32),
                 mixed_shape, mixed_shape, mixed_shape, cache4, cache4, cache2, cache2,
                 jax.ShapeDtypeStruct((n_seq, depth, N_HEADS, 2, seq, DF_DQK), F32), cache4]
    assert len(out_specs) == N_CTX_GROUP_OUTPUTS
    return pl.pallas_call(
        functools.partial(_ctx_group_body, seq=seq, lam_init=lam_init, n_alias=len(prev)),
        grid=(n_seq,),
        in_specs=in_specs + [ANY_SPEC] * len(prev), out_specs=out_specs, out_shape=out_shape,
        input_output_aliases=aliases,
        scratch_shapes=[pltpu.VMEM((seq, GROUP_W), F32), pltpu.VMEM((seq, GROUP_W), F32)],
        compiler_params=_params("parallel"),
        name="context_mixers",
    )(*args, *prev)


def _rope_tables(n_tokens, rot_dim):
    t = np.arange(n_tokens)
    row = (t // GRID_W).astype(np.float32)
    col = (t % GRID_W).astype(np.float32)
    n_freq = rot_dim // 4
    inv = (np.float32(ROPE_THETA) ** (-np.arange(n_freq, dtype=np.float32) / np.float32(n_freq))).astype(np.float32)
    ang = np.concatenate([row[:, None] * inv, col[:, None] * inv], axis=-1).astype(np.float32)
    cos, sin, zero = np.cos(ang), np.sin(ang), np.zeros_like(ang)
    reps = HEAD_W // rot_dim
    a = np.tile(np.concatenate([cos, cos], axis=-1), (1, reps))
    b = np.tile(np.concatenate([-sin, zero], axis=-1), (1, reps))
    c = np.tile(np.concatenate([zero, sin], axis=-1), (1, reps))
    return jnp.asarray(np.stack([a, b, c]), F32)


def _rope(x, tab_ref, half):
    return (x * tab_ref[0] + pltpu.roll(x, HEAD_W - half, 1) * tab_ref[1]
            + pltpu.roll(x, half, 1) * tab_ref[2])


def _na_body(q_ref, k_ref, v_ref, ck_ref, cv_ref, bias_ref, qn_ref, kn_ref, prev_ref, o_ref,
             qs_ref, ks_ref, vs_ref, *, seq):
    del prev_ref
    rows = seq // GRID_W
    n_win = WIN_ROWS * GRID_W
    qs_ref[...] = (_rms_head(q_ref[...], qn_ref[...]) * (LOG2E * HEAD_W ** -0.5)).astype(BF16)
    ks_ref[...] = _rms_head(k_ref[...], kn_ref[...]).astype(BF16)
    vs_ref[...] = _with_ones(v_ref[...].astype(BF16))
    ck = ck_ref[...].astype(BF16)
    cv = _with_ones(cv_ref[...].astype(BF16))

    def row_step(r, carry):
        start = jnp.clip(r - WIN_ROWS // 2, 0, rows - WIN_ROWS)
        win = pl.ds(pl.multiple_of(start * GRID_W, GRID_W), n_win)
        qrows = pl.ds(pl.multiple_of(r * GRID_W, GRID_W), GRID_W)
        q = qs_ref[qrows, :]
        s_win = _dot_nt(q, ks_ref[win, :]) + bias_ref[start - r + (WIN_ROWS - 1)]
        s_ctx = _dot_nt(q, ck)
        o_ref[qrows, :] = _softmax_pv([s_win, s_ctx], [vs_ref[win, :], cv]).astype(o_ref.dtype)
        return carry

    lax.fori_loop(0, rows, row_step, 0, unroll=min(rows, 32))


def _na_bias(rpb):
    col = np.arange(GRID_W)
    col_start = np.clip(col - WIN_COLS // 2, 0, GRID_W - WIN_COLS)
    col_ok = (col[None, :] >= col_start[:, None]) & (col[None, :] < col_start[:, None] + WIN_COLS)
    dc = np.clip(col[None, :] - col[:, None] + WIN_COLS - 1, 0, 2 * WIN_COLS - 2).reshape(-1)
    onehot = (np.arange(2 * WIN_COLS - 1)[:, None] == dc[None, :]).astype(np.float32)
    per_dr = jnp.einsum('hdc,cn->hdn', rpb.astype(F32), jnp.asarray(onehot), precision=lax.Precision.HIGHEST)
    per_dr = jnp.where(col_ok[None, None], LOG2E * per_dr.reshape(rpb.shape[0], -1, GRID_W, GRID_W), NEG_INF)
    wins = jnp.stack([per_dr[:, o:o + WIN_ROWS] for o in range(WIN_ROWS)], axis=1)
    return wins.transpose(0, 1, 3, 2, 4).reshape(rpb.shape[0], WIN_ROWS, GRID_W, WIN_ROWS * GRID_W)


def _latent_na(proj, row_block0, n_seq, seq, cache_k, cache_v, layer, bias, na_qn, na_kn, mixed_prev):
    past = cache_k.shape[3]

    def col(cb):
        return pl.BlockSpec((seq, HEAD_W), lambda b, h, cb=cb: (row_block0 + b, cb + h))

    cache = pl.BlockSpec((None, None, None, past, HEAD_W), lambda b, h: (b, layer, h, 0, 0))
    vec = pl.BlockSpec((1, HEAD_W), lambda b, h: (0, 0))
    return pl.pallas_call(
        functools.partial(_na_body, seq=seq),
        grid=(n_seq, N_HEADS),
        in_specs=[col(COL_B_Q), col(COL_B_K), col(COL_B_V), cache, cache,
                  pl.BlockSpec((None, WIN_ROWS, GRID_W, WIN_ROWS * GRID_W), lambda b, h: (h, 0, 0, 0)),
                  vec, vec, ANY_SPEC],
        out_specs=pl.BlockSpec((seq, HEAD_W), lambda b, h: (row_block0 + b, h)),
        out_shape=jax.ShapeDtypeStruct(mixed_prev.shape, BF16),
        input_output_aliases={8: 0},
        scratch_shapes=[pltpu.VMEM((seq, HEAD_W), BF16), pltpu.VMEM((seq, HEAD_W), BF16),
                        pltpu.VMEM((seq, 2 * HEAD_W), BF16)],
        compiler_params=_params("parallel", "parallel"),
        name="latent_neighbourhood_attention",
    )(proj, proj, proj, cache_k, cache_v, bias, na_qn[None, :], na_kn[None, :], mixed_prev)


def _gqa_body(q_ref, k_ref, v_ref, ck_ref, cv_ref, rope_ref, qn_ref, kn_ref, prev_ref, o_ref,
              qs_ref, ks_ref, vs_ref, *, seq, tq):
    del prev_ref
    group = N_HEADS // GQA_KV_HEADS
    half = HEAD_W // 2
    ks_ref[0:seq, :] = _rope(_rms_head(k_ref[...], kn_ref[...]), rope_ref, half).astype(BF16)
    ks_ref[seq:, :] = ck_ref[...].astype(BF16)
    vs_ref[0:seq, :] = _with_ones(v_ref[...].astype(BF16))
    vs_ref[seq:, :] = _with_ones(cv_ref[...].astype(BF16))
    for g in range(group):
        q = _rms_head(q_ref[:, g * HEAD_W:(g + 1) * HEAD_W], qn_ref[...]) * (LOG2E * HEAD_W ** -0.5)
        qs_ref[g] = _rope(q, rope_ref, half).astype(BF16)
    kk = ks_ref[...]
    vv = vs_ref[...]
    for g in range(group):
        def q_step(i, carry, g=g):
            qrows = pl.ds(pl.multiple_of(i * tq, tq), tq)
            o = _softmax_pv([_dot_nt(qs_ref[g, qrows, :], kk)], [vv])
            o_ref[qrows, g * HEAD_W:(g + 1) * HEAD_W] = o.astype(o_ref.dtype)
            return carry

        lax.fori_loop(0, seq // tq, q_step, 0, unroll=min(seq // tq, 8))


def _latent_gqa(proj, row_block0, n_seq, seq, cache_k, cache_v, layer, rope, gqa_qn, gqa_kn, mixed_prev):
    past = cache_k.shape[3]
    group = N_HEADS // GQA_KV_HEADS
    tq = 256
    cache = pl.BlockSpec((None, None, None, past, HEAD_W), lambda b, n: (b, layer, n, 0, 0))
    vec = pl.BlockSpec((1, HEAD_W), lambda b, n: (0, 0))
    return pl.pallas_call(
        functools.partial(_gqa_body, seq=seq, tq=tq),
        grid=(n_seq, GQA_KV_HEADS),
        in_specs=[pl.BlockSpec((seq, group * HEAD_W), lambda b, n: (row_block0 + b, COL_C_Q // group + n)),
                  pl.BlockSpec((seq, HEAD_W), lambda b, n: (row_block0 + b, COL_C_K + n)),
                  pl.BlockSpec((seq, HEAD_W), lambda b, n: (row_block0 + b, COL_C_V + n)),
                  cache, cache,
                  pl.BlockSpec((3, seq, HEAD_W), lambda b, n: (0, 0, 0)),
                  vec, vec, ANY_SPEC],
        out_specs=pl.BlockSpec((seq, group * HEAD_W), lambda b, n: (row_block0 + b, n)),
        out_shape=jax.ShapeDtypeStruct(mixed_prev.shape, BF16),
        input_output_aliases={8: 0},
        scratch_shapes=[pltpu.VMEM((group, seq, HEAD_W), BF16),
                        pltpu.VMEM((seq + past, HEAD_W), BF16),
                        pltpu.VMEM((seq + past, 2 * HEAD_W), BF16)],
        compiler_params=_params("parallel", "parallel"),
        name="latent_gqa_attention",
    )(proj, proj, proj, cache_k, cache_v, rope, gqa_qn[None, :], gqa_kn[None, :], mixed_prev)


def _diff_body(q_ref, k_ref, v_ref, ck_ref, cv_ref, rope_ref, qn_ref, kn_ref, sub_ref, lam_ref, prev_ref, o_ref,
               qs_ref, k0_ref, k1_ref, vs_ref, *, seq, tq, lam_init):
    del prev_ref
    half = DF_DQK // 2
    k = _rope(_rms_halves(k_ref[...], kn_ref[...]), rope_ref, half)
    lo = _lane_lt(k.shape, DF_DQK)
    k0_ref[0:seq, :] = jnp.where(lo, k, 0.0).astype(BF16)
    k1_ref[0:seq, :] = jnp.where(lo, 0.0, k).astype(BF16)
    ck = ck_ref[...]
    lo_c = _lane_lt(ck.shape, DF_DQK)
    k0_ref[seq:, :] = jnp.where(lo_c, ck, 0.0).astype(BF16)
    k1_ref[seq:, :] = jnp.where(lo_c, 0.0, ck).astype(BF16)
    vs_ref[0:seq, :] = _with_ones(v_ref[...].astype(BF16))
    vs_ref[seq:, :] = _with_ones(cv_ref[...].astype(BF16))
    q = _rms_halves(q_ref[...], qn_ref[...]) * (LOG2E * DF_DQK ** -0.5)
    qs_ref[...] = _rope(q, rope_ref, half).astype(BF16)
    lam = _lambda(lam_ref, lam_init)
    k0 = k0_ref[...]
    k1 = k1_ref[...]
    vv = vs_ref[...]

    def q_step(i, carry):
        qrows = pl.ds(pl.multiple_of(i * tq, tq), tq)
        qb = qs_ref[qrows, :]
        o = _diff_pv(_dot_nt(qb, k0), _dot_nt(qb, k1), lam, vv)
        o_ref[qrows, :] = (_rms_head(o, sub_ref[...]) * (1.0 - lam_init)).astype(o_ref.dtype)
        return carry

    lax.fori_loop(0, seq // tq, q_step, 0, unroll=min(seq // tq, 4))


def _latent_diff(proj, row_block0, n_seq, seq, cache_k2, cache_v, layer, rope, df_qn, df_kn, df_subln, df_lam,
                 lam_init, mixed_prev):
    past = cache_k2.shape[3]
    tq = 256

    def col(cb):
        return pl.BlockSpec((seq, HEAD_W), lambda b, h, cb=cb: (row_block0 + b, cb + h))

    cache = pl.BlockSpec((None, None, None, past, HEAD_W), lambda b, h: (b, layer, h, 0, 0))
    vec = pl.BlockSpec((1, HEAD_W), lambda b, h: (0, 0))
    kv_scratch = pltpu.VMEM((seq + past, HEAD_W), BF16)
    return pl.pallas_call(
        functools.partial(_diff_body, seq=seq, tq=tq, lam_init=lam_init),
        grid=(n_seq, N_HEADS),
        in_specs=[col(COL_D_Q), col(COL_D_K), col(COL_D_V), cache, cache,
                  pl.BlockSpec((3, seq, HEAD_W), lambda b, h: (0, 0, 0)),
                  vec, vec, vec, pl.BlockSpec((4, DF_DQK), lambda b, h: (0, 0)), ANY_SPEC],
        out_specs=pl.BlockSpec((seq, HEAD_W), lambda b, h: (row_block0 + b, h)),
        out_shape=jax.ShapeDtypeStruct(mixed_prev.shape, BF16),
        input_output_aliases={10: 0},
        scratch_shapes=[pltpu.VMEM((seq, HEAD_W), BF16), kv_scratch, kv_scratch,
                        pltpu.VMEM((seq + past, 2 * HEAD_W), BF16)],
        compiler_params=_params("parallel", "parallel"),
        name="latent_diff_attention",
    )(proj, proj, proj, cache_k2, cache_v, rope, jnp.tile(df_qn, 2)[None, :], jnp.tile(df_kn, 2)[None, :],
      df_subln[None, :], df_lam, mixed_prev)


def _first_max(vals):
    best = vals[0]
    idx = jnp.zeros(best.shape, jnp.int32)
    for i in range(1, len(vals)):
        better = vals[i] > best
        best = jnp.where(better, vals[i], best)
        idx = jnp.where(better, i, idx)
    return best, idx


def _pick(vals, idx):
    out = vals[0]
    for i in range(1, len(vals)):
        out = jnp.where(idx == i, vals[i], out)
    return out


def _outproj_body(ma_ref, mb_ref, mc_ref, md_ref, w_ref, hc_ref, hl_ref, mod_ref, n2_ref, rw_ref, rb_ref,
                  h1_ref, x2_ref, idx_ref, gate_ref, mixed_ref, *, ctx_tiles):
    for g, m_ref in enumerate((ma_ref, mb_ref, mc_ref, md_ref)):
        mixed_ref[:, g * GROUP_W:(g + 1) * GROUP_W] = m_ref[...]
    y = mod_ref[2:3, :] * _dot(mixed_ref[...], w_ref[...])

    def residual(h_ref):
        h1_ref[...] = h_ref[...] + y

    is_ctx = pl.program_id(0) < ctx_tiles
    pl.when(is_ctx)(lambda: residual(hc_ref))
    pl.when(jnp.logical_not(is_ctx))(lambda: residual(hl_ref))
    h1 = h1_ref[...]
    x2 = _rms(h1, n2_ref[...], D_MODEL) * (1.0 + mod_ref[4:5, :]) + mod_ref[3:4, :]
    x2_ref[...] = _pack_bf16_pairs(x2)
    x_hi = x2.astype(BF16)
    x_lo = (x2 - x_hi.astype(F32)).astype(BF16)
    acc = _dot(x_hi, rw_ref[...])
    logits_tok = acc[:, 0:HEAD_W] + acc[:, HEAD_W:2 * HEAD_W] + _dot(x_lo, rw_ref[:, 0:HEAD_W])
    logits = logits_tok.T[0:N_EXPERTS, :]
    aff_all = _sigmoid(logits)
    sel_all = aff_all + rb_ref[...]
    aff = [aff_all[e:e + 1, :] for e in range(N_EXPERTS)]
    sel = [sel_all[e:e + 1, :] for e in range(N_EXPERTS)]
    neg = jnp.full(sel[0].shape, -jnp.inf, F32)
    scores = []
    for g in range(N_EXP_GROUPS):
        grp = sel[g * EXP_PER_GROUP:(g + 1) * EXP_PER_GROUP]
        m1, i1 = _first_max(grp)
        m2, _ = _first_max([jnp.where(i1 == j, neg, grp[j]) for j in range(EXP_PER_GROUP)])
        scores.append(m1 + m2)
    _, g_best = _first_max(scores)
    in_sel = [_pick([sel[g * EXP_PER_GROUP + j] for g in range(N_EXP_GROUPS)], g_best)
              for j in range(EXP_PER_GROUP)]
    in_aff = [_pick([aff[g * EXP_PER_GROUP + j] for g in range(N_EXP_GROUPS)], g_best)
              for j in range(EXP_PER_GROUP)]
    _, l1 = _first_max(in_sel)
    _, l2 = _first_max([jnp.where(l1 == j, neg, in_sel[j]) for j in range(EXP_PER_GROUP)])
    w1 = _pick(in_aff, l1)
    w2 = _pick(in_aff, l2)
    idx_ref[0:1, :] = g_best * EXP_PER_GROUP + l1
    idx_ref[1:2, :] = g_best * EXP_PER_GROUP + l2
    gate_ref[0:1, :] = w1 / (w1 + w2)
    gate_ref[1:2, :] = w2 / (w1 + w2)


def _output_projection(mixed4, w_out_bf16, h_ctx, h_lat, mod, norm2, router_w, router_b, cond_of_tile, tm):
    d = h_ctx.shape[1]
    t = h_ctx.shape[0] + h_lat.shape[0]
    ctx_tiles = h_ctx.shape[0] // tm
    slab = pl.BlockSpec((tm, GROUP_W), lambda i: (i, 0))
    rw_hi = router_w.astype(BF16)
    rw_lo = (router_w - rw_hi.astype(F32)).astype(BF16)
    pad = ((0, 0), (0, HEAD_W - N_EXPERTS))
    router_split = jnp.concatenate([jnp.pad(rw_hi, pad), jnp.pad(rw_lo, pad)], axis=1)
    return pl.pallas_call(
        functools.partial(_outproj_body, ctx_tiles=ctx_tiles),
        grid=(t // tm,),
        in_specs=[slab, slab, slab, slab,
                  pl.BlockSpec((d, d), lambda i: (0, 0))] + _group_specs(tm, d, ctx_tiles) + [
                  pl.BlockSpec((None, 6, d), lambda i: (cond_of_tile(i), 0, 0)),
                  pl.BlockSpec((1, d), lambda i: (0, 0)),
                  pl.BlockSpec((d, 2 * HEAD_W), lambda i: (0, 0)),
                  pl.BlockSpec((N_EXPERTS, 1), lambda i: (0, 0))],
        out_specs=[pl.BlockSpec((tm, d), lambda i: (i, 0)),
                   pl.BlockSpec((tm, d // 2), lambda i: (i, 0)),
                   pl.BlockSpec((2, tm), lambda i: (0, i)),
                   pl.BlockSpec((2, tm), lambda i: (0, i))],
        out_shape=[jax.ShapeDtypeStruct((t, d), F32), jax.ShapeDtypeStruct((t, d // 2), jnp.uint32),
                   jax.ShapeDtypeStruct((2, t), jnp.int32), jax.ShapeDtypeStruct((2, t), F32)],
        scratch_shapes=[pltpu.VMEM((tm, d), BF16)],
        compiler_params=_params("parallel", vmem=VMEM_LIMIT_PROJ),
        name="out_proj_residual_router",
    )(*mixed4, w_out_bf16, h_ctx, h_lat, mod, norm2[None, :], router_split, router_b[:, None])


def _vmem_row(ref, base, u):
    return ref.at[pl.ds(base, SUBLANES), :].at[pl.ds(u, 1), :]


def _for_row_groups(n_rows, fn):
    def body(g, carry):
        base = pl.multiple_of(g * SUBLANES, SUBLANES)
        for u in range(SUBLANES):
            fn(base, u)
        return carry

    lax.fori_loop(0, n_rows // SUBLANES, body, 0)


def _rows_wait(src_hbm, dst, sem, n_rows):
    pltpu.make_async_copy(src_hbm.at[pl.ds(0, n_rows), :], dst.at[pl.ds(0, n_rows), :], sem).wait()


def _dispatch_body(dest_ref, ps_ref, pe_ref, x_ref, xs_hbm, zero_ref, sem, zsem, *, n_tok, tile):
    i = pl.program_id(0)

    def zero_copy(e):
        first = pl.multiple_of(pe_ref[e] - MOE_ROWS, MOE_ROWS)
        return pltpu.make_async_copy(zero_ref, xs_hbm.at[pl.ds(first, MOE_ROWS), :], zsem)

    @pl.when(i == 0)
    def _():
        zero_ref[...] = jnp.zeros(zero_ref.shape, zero_ref.dtype)
        for e in range(N_EXPERTS):
            @pl.when(pe_ref[e] > ps_ref[e])
            def _(e=e):
                zero_copy(e).start()
        for e in range(N_EXPERTS):
            @pl.when(pe_ref[e] > ps_ref[e])
            def _(e=e):
                zero_copy(e).wait()

    for k in range(2):
        def scatter_row(base, u, k=k):
            row = dest_ref[k * n_tok + i * tile + base + u]
            pltpu.make_async_copy(_vmem_row(x_ref, base, u), xs_hbm.at[pl.ds(row, 1), :], sem).start()

        _for_row_groups(tile, scatter_row)
    for k in range(2):
        _rows_wait(x_ref, xs_hbm, sem, tile)


def _dispatch(x2, dest, pad_start, pad_end, n_rows):
    t, d = x2.shape
    tile = next(m for m in DISPATCH_TILES if t % m == 0)
    grid_spec = pltpu.PrefetchScalarGridSpec(
        num_scalar_prefetch=3,
        grid=(t // tile,),
        in_specs=[pl.BlockSpec((tile, d), lambda i, dst, ps, pe: (i, 0))],
        out_specs=ANY_SPEC,
        scratch_shapes=[pltpu.VMEM((MOE_ROWS, d), x2.dtype), pltpu.SemaphoreType.DMA, pltpu.SemaphoreType.DMA],
    )
    return pl.pallas_call(
        functools.partial(_dispatch_body, n_tok=t, tile=tile),
        grid_spec=grid_spec,
        out_shape=jax.ShapeDtypeStruct((n_rows, d), x2.dtype),
        compiler_params=_params("arbitrary"),
        name="moe_dispatch",
    )(dest, pad_start, pad_end, x2)


def _expert_body(be_ref, nb_ref, nx_ref, x_ref, wg_hbm, wu_hbm, wd_hbm, o_ref,
                 stage_g, stage_u, stage_d, wg_bf, wu_bf, wd_bf, sem, *, layer):
    i = pl.program_id(0)

    def weight_copies(e):
        return (pltpu.make_async_copy(wg_hbm.at[layer, e], stage_g, sem.at[0]),
                pltpu.make_async_copy(wu_hbm.at[layer, e], stage_u, sem.at[1]),
                pltpu.make_async_copy(wd_hbm.at[layer, e], stage_d, sem.at[2]))

    @pl.when(i < nb_ref[0])
    def _():
        e = be_ref[i]

        @pl.when(i == 0)
        def _():
            for c in weight_copies(e):
                c.start()

        @pl.when((i == 0) | (e != be_ref[jnp.maximum(i - 1, 0)]))
        def _():
            for c in weight_copies(e):
                c.wait()
            wg_bf[...] = stage_g[...].astype(BF16)
            wu_bf[...] = stage_u[...].astype(BF16)
            wd_bf[...] = stage_d[...].astype(BF16)
            nxt = nx_ref[e]

            @pl.when(nxt < N_EXPERTS)
            def _():
                for c in weight_copies(nxt):
                    c.start()

        x = _unpack_bf16_pairs(x_ref[...]).astype(BF16)
        hdn = _silu(_dot(x, wg_bf[...])) * _dot(x, wu_bf[...])
        o_ref[...] = _pack_bf16_pairs(_dot(hdn.astype(BF16), wd_bf[...]))

    @pl.when(i >= nb_ref[0])
    def _():
        o_ref[...] = jnp.zeros(o_ref.shape, o_ref.dtype)


def _expert_blocks(xs, block_expert, n_used, next_expert, w_gate, w_up, w_down, layer):
    n_rows, half = xs.shape
    d = 2 * half
    ff = w_gate.shape[-1]
    grid_spec = pltpu.PrefetchScalarGridSpec(
        num_scalar_prefetch=3,
        grid=(n_rows // MOE_ROWS,),
        in_specs=[pl.BlockSpec((MOE_ROWS, half), lambda i, be, nb, nx: (jnp.minimum(i, nb[0] - 1), 0)),
                  ANY_SPEC, ANY_SPEC, ANY_SPEC],
        out_specs=pl.BlockSpec((MOE_ROWS, half), lambda i, be, nb, nx: (i, 0)),
        scratch_shapes=[pltpu.VMEM((d, ff), F32), pltpu.VMEM((d, ff), F32), pltpu.VMEM((ff, d), F32),
                        pltpu.VMEM((d, ff), BF16), pltpu.VMEM((d, ff), BF16), pltpu.VMEM((ff, d), BF16),
                        pltpu.SemaphoreType.DMA((3,))],
    )
    return pl.pallas_call(
        functools.partial(_expert_body, layer=layer),
        grid_spec=grid_spec,
        out_shape=jax.ShapeDtypeStruct((n_rows, half), jnp.uint32),
        compiler_params=_params("arbitrary"),
        name="moe_expert_blocks",
    )(block_expert, n_used, next_expert, xs, w_gate, w_up, w_down)


def _combine_body(dest_ref, h_ref, gate_ref, y_hbm, mod_ref, *rest, n_tok, tile, ctx_tiles):
    out_refs, (ybuf, sem) = rest[:-2], rest[-2:]
    i = pl.program_id(0)
    n_tiles = pl.num_programs(0)

    def start(blk, slot):
        for k in range(2):
            def gather_row(base, u, k=k):
                row = dest_ref[k * n_tok + blk * tile + base + u]
                pltpu.make_async_copy(y_hbm.at[pl.ds(row, 1), :], _vmem_row(ybuf.at[slot, k], base, u),
                                      sem.at[slot]).start()

            _for_row_groups(tile, gather_row)

    @pl.when(i == 0)
    def _():
        start(0, 0)

    @pl.when(i + 1 < n_tiles)
    def _():
        start(i + 1, (i + 1) % 2)

    slot = i % 2
    for k in range(2):
        _rows_wait(y_hbm, ybuf.at[slot, k], sem.at[slot], tile)
    gate = gate_ref[...]
    y0 = _unpack_bf16_pairs(ybuf[slot, 0])
    y1 = _unpack_bf16_pairs(ybuf[slot, 1])
    out = h_ref[...] + mod_ref[5:6, :] * (gate[:, 0:1] * y0 + gate[:, 1:2] * y1)
    @pl.when(i < ctx_tiles)
    def _():
        out_refs[0][...] = out

    @pl.when(i >= ctx_tiles)
    def _():
        out_refs[1][...] = out


def _combine(h1, yb, dest, gates, mod, cond_of_tile, tile, split_rows):
    t, d = h1.shape
    row_tile = pl.BlockSpec((tile, d), lambda i, dst: (i, 0))
    ctx_tiles = split_rows // tile
    out_specs = _group_specs(tile, d, ctx_tiles)
    out_shape = [jax.ShapeDtypeStruct((split_rows, d), F32), jax.ShapeDtypeStruct((t - split_rows, d), F32)]
    grid_spec = pltpu.PrefetchScalarGridSpec(
        num_scalar_prefetch=1,
        grid=(t // tile,),
        in_specs=[row_tile,
                  pl.BlockSpec((tile, 2), lambda i, dst: (i, 0)),
                  ANY_SPEC,
                  pl.BlockSpec((None, 6, d), lambda i, dst: (cond_of_tile(i), 0, 0))],
        out_specs=out_specs,
        scratch_shapes=[pltpu.VMEM((2, 2, tile, yb.shape[1]), yb.dtype), pltpu.SemaphoreType.DMA((2,))],
    )
    return pl.pallas_call(
        functools.partial(_combine_body, n_tok=t, tile=tile, ctx_tiles=ctx_tiles),
        grid_spec=grid_spec,
        out_shape=out_shape,
        compiler_params=_params("arbitrary"),
        name="moe_gated_residual",
    )(dest, h1, gates, yb, mod)


def _moe(h1, x2, idx_t, gate_t, mod, w_gate, w_up, w_down, layer, cond_of_tile, split_rows):
    t, d = h1.shape
    n = 2 * t
    experts = idx_t.reshape(n)
    onehot = (experts[:, None] == jnp.arange(N_EXPERTS, dtype=jnp.int32)[None, :]).astype(BF16)
    blocks = onehot.reshape(n // COMBINE_TILE, COMBINE_TILE, N_EXPERTS)
    tri = jnp.asarray(np.tril(np.ones((COMBINE_TILE, COMBINE_TILE), np.float32)), BF16)
    within = jnp.einsum('ij,bjk->bik', tri, blocks, preferred_element_type=F32)
    block_total = within[:, -1, :]
    block_first = jnp.cumsum(block_total, axis=0) - block_total
    counts = (block_first[-1] + block_total[-1]).astype(jnp.int32)
    before = (within + block_first[:, None, :]).reshape(n, N_EXPERTS) - 1.0
    rank = jnp.sum(before * onehot.astype(F32), axis=1).astype(jnp.int32)
    padded = (counts + MOE_ROWS - 1) // MOE_ROWS * MOE_ROWS
    pad_end = jnp.cumsum(padded).astype(jnp.int32)
    pad_start = pad_end - padded
    dest = (pad_start[experts] + rank).astype(jnp.int32)
    n_blocks = (n + N_EXPERTS * (MOE_ROWS - 1) + MOE_ROWS - 1) // MOE_ROWS
    block_first_row = jnp.arange(n_blocks, dtype=jnp.int32) * MOE_ROWS
    block_expert = jnp.minimum(jnp.sum((pad_end[None, :] <= block_first_row[:, None]).astype(jnp.int32), axis=1),
                               N_EXPERTS - 1)
    n_used = (pad_end[-1:] // MOE_ROWS).astype(jnp.int32)
    xs = _dispatch(x2, dest, pad_start, pad_end, n_blocks * MOE_ROWS)
    ids = jnp.arange(N_EXPERTS, dtype=jnp.int32)
    later_with_rows = (counts[None, :] > 0) & (ids[None, :] > ids[:, None])
    next_expert = jnp.min(jnp.where(later_with_rows, ids[None, :], N_EXPERTS), axis=1).astype(jnp.int32)
    yb = _expert_blocks(xs, block_expert, n_used, next_expert, w_gate, w_up, w_down, layer)
    return _combine(h1, yb, dest, gate_t.T, mod, cond_of_tile, COMBINE_TILE, split_rows)


def kernel(x_prompt, x_sample, cache_na_k, cache_na_v, cache_gqa_k, cache_gqa_v, cache_diff_k, cache_diff_v, state_hgrn, c, c_ctx, w_mod, b_mod, norm1, norm2, w_in, w_out, hg_lb_logits, hg_onorm, na_qn, na_kn, na_rpb, gqa_qn, gqa_kn, df_qn, df_kn, df_lam, df_subln, router_w, router_b, w_gate, w_up, w_down):
    n_ctx, ctx_len, d = x_prompt.shape
    n_lat, lat_len, _ = x_sample.shape
    depth = w_in.shape[0]
    t_ctx = n_ctx * ctx_len
    assert t_ctx % lat_len == 0 and lat_len % GRID_W == 0 and lat_len // GRID_W >= WIN_ROWS
    tm = next(m for m in (1024, 512, 256) if t_ctx % m == 0 and lat_len % m == 0)
    tm2 = min(tm, 512)
    lat_block0 = t_ctx // lat_len

    def cond_tile(tile_rows):
        def cond_of_tile(i):
            return jnp.where(i < t_ctx // tile_rows, 0, 1 + (i - t_ctx // tile_rows) // (lat_len // tile_rows))
        return cond_of_tile

    sm = jax.nn.softmax(hg_lb_logits.astype(F32), axis=0)
    lower = jnp.cumsum(sm, axis=0) - sm[0:1]
    mod_all = _modulation(jnp.concatenate([c_ctx[None, :], c], axis=0), w_mod, b_mod)
    mod_all = mod_all.reshape(depth, 1 + n_lat, 6, d)
    hgrn_consts = _hgrn_constants(HGRN_CHUNK)
    rope_c = _rope_tables(lat_len, HEAD_W)
    rope_d = _rope_tables(lat_len, DF_DQK)
    past = cache_diff_k.shape[4]
    cache_diff_k2 = cache_diff_k.transpose(0, 1, 2, 4, 3, 5).reshape(n_lat, depth, N_HEADS, past, HEAD_W)

    h_ctx, h_lat = x_prompt.reshape(t_ctx, d), x_sample.reshape(n_lat * lat_len, d)
    caches, states = [], None
    for layer in range(depth):
        mod = mod_all[layer]
        lam_init = 0.8 - 0.6 * math.exp(-0.3 * layer)
        proj = _input_projection(h_ctx, h_lat, mod, norm1[layer], w_in[layer].astype(BF16), cond_tile(tm), tm)
        gains = (na_qn[layer], na_kn[layer], gqa_qn[layer], gqa_kn[layer], df_qn[layer], df_kn[layer],
                 df_subln[layer], df_lam[layer])
        mix_a, states, mix_b, mix_c, mix_d, *caches = _context_group(
            proj, n_ctx, ctx_len, lower[layer], hg_onorm[layer], hgrn_consts, gains, lam_init, layer, depth,
            states, caches)
        mix_a, _ = _hgrn(proj, lat_block0, n_lat, lat_len, lower[layer], hg_onorm[layer], hgrn_consts,
                         state_hgrn, mix_a, None, layer, depth)
        mix_b = _latent_na(proj, lat_block0, n_lat, lat_len, cache_na_k, cache_na_v, layer,
                           _na_bias(na_rpb[layer]), na_qn[layer], na_kn[layer], mix_b)
        mix_c = _latent_gqa(proj, lat_block0, n_lat, lat_len, cache_gqa_k, cache_gqa_v, layer, rope_c,
                            gqa_qn[layer], gqa_kn[layer], mix_c)
        mix_d = _latent_diff(proj, lat_block0, n_lat, lat_len, cache_diff_k2, cache_diff_v, layer, rope_d,
                             df_qn[layer], df_kn[layer], df_subln[layer], df_lam[layer], lam_init, mix_d)
        h1, x2, idx_t, gate_t = _output_projection((mix_a, mix_b, mix_c, mix_d), w_out[layer].astype(BF16), h_ctx,
                                                   h_lat, mod, norm2[layer], router_w, router_b, cond_tile(tm2), tm2)
        h_ctx, h_lat = _moe(h1, x2, idx_t, gate_t, mod, w_gate, w_up, w_down, layer, cond_tile(COMBINE_TILE), t_ctx)
    y_prompt = h_ctx.reshape(n_ctx, ctx_len, d)
    y_sample = h_lat.reshape(n_lat, lat_len, d)
    return (y_prompt, y_sample, *caches, states)
```

```python
import functools
import math

import numpy as np
import jax
import jax.numpy as jnp
from jax import lax
from jax.experimental import pallas as pl
from jax.experimental.pallas import tpu as pltpu

D_MODEL = 2048
GRID_W = 64
GROUP_W = D_MODEL // 4
N_HEADS = 4
HEAD_W = GROUP_W // N_HEADS
SUBLANES = 8
GQA_KV_HEADS = 2
DF_DQK = HEAD_W // 2
WIN_ROWS = 8
WIN_COLS = 16
N_EXPERTS = 16
N_EXP_GROUPS = 4
EXP_PER_GROUP = N_EXPERTS // N_EXP_GROUPS
EXPERT_FF = D_MODEL // 4
ROPE_THETA = 10000.0
EPS = 1e-6
NEG_INF = -1e30
LOG2E = math.log2(math.e)
IN_WIDTH = 13 * GROUP_W

COL_A_Q, COL_A_FF, COL_A_FB, COL_A_I, COL_A_G = 0, 4, 8, 12, 16
COL_B_Q, COL_B_K, COL_B_V = 20, 24, 28
COL_C_Q, COL_C_K, COL_C_V = 32, 36, 38
COL_D_Q, COL_D_K, COL_D_V = 40, 44, 48

HGRN_CHUNK = 128
HGRN_UNROLL = 4
HGRN_INPUT_VMEM = 24 * 1024 * 1024
CTX_KV_HEADS_PER_STEP = 1
MOE_ROWS = 256
DISPATCH_TILES = (1024, 512, 256)
COMBINE_TILE = 256
VMEM_LIMIT = 48 * 1024 * 1024
VMEM_LIMIT_PROJ = 56 * 1024 * 1024

F32 = jnp.float32
BF16 = jnp.bfloat16
ANY_SPEC = pl.BlockSpec(memory_space=pl.ANY)


def _params(*sem, vmem=VMEM_LIMIT):
    return pltpu.CompilerParams(dimension_semantics=sem, vmem_limit_bytes=vmem)


def _sigmoid(x):
    return 1.0 / (1.0 + jnp.exp(-x))


def _silu(x):
    return x * _sigmoid(x)


def _rms(x, gain, n):
    return x * lax.rsqrt(jnp.sum(x * x, axis=-1, keepdims=True) * (1.0 / n) + EPS) * gain


def _dot(a, b):
    return jnp.dot(a, b, preferred_element_type=F32)


def _dot_nt(a, b):
    return lax.dot_general(a, b, (((1,), (1,)), ((), ())), preferred_element_type=F32)


def _dot_tn(a, b):
    return lax.dot_general(a, b, (((0,), (0,)), ((), ())), preferred_element_type=F32)


def _pack_bf16_pairs(x):
    k = x.shape[1] // 2
    lo = lax.bitcast_convert_type(x[:, :k].astype(BF16).astype(F32), jnp.uint32) >> 16
    hi = lax.bitcast_convert_type(x[:, k:].astype(BF16).astype(F32), jnp.uint32)
    return hi | lo


def _unpack_bf16_pairs(w):
    lo = lax.bitcast_convert_type(w << 16, F32)
    hi = lax.bitcast_convert_type(w & jnp.uint32(0xFFFF0000), F32)
    return jnp.concatenate([lo, hi], axis=1)


def _aligned(x, m):
    return x if isinstance(x, int) else pl.multiple_of(x, m)


def _alias_kwargs(n_inputs, prev, first_out):
    return ([ANY_SPEC] * len(prev), list(prev), {n_inputs + k: first_out + k for k in range(len(prev))})


def _mod_body(cond_ref, w_ref, b_ref, o_ref):
    w = w_ref[...]
    for c in range(cond_ref.shape[0]):
        s = _silu(cond_ref[c])
        o_ref[c:c + 1, :] = jnp.sum(w * s, axis=0, keepdims=True) + b_ref[...]


def _modulation(cond, w_mod, b_mod):
    depth, d, n6 = w_mod.shape
    nc = cond.shape[0]
    tn = 1024
    return pl.pallas_call(
        _mod_body,
        grid=(depth, n6 // tn),
        in_specs=[pl.BlockSpec((nc, d, 1), lambda l, j: (0, 0, 0)),
                  pl.BlockSpec((None, d, tn), lambda l, j: (l, 0, j)),
                  pl.BlockSpec((None, 1, tn), lambda l, j: (l, 0, j))],
        out_specs=pl.BlockSpec((None, nc, tn), lambda l, j: (l, 0, j)),
        out_shape=jax.ShapeDtypeStruct((depth, nc, n6), F32),
        compiler_params=_params("parallel", "parallel"),
        name="adaln_modulation",
    )(cond[:, :, None], w_mod, b_mod[:, None, :])


def _group_specs(tile, d, ctx_tiles):
    return [pl.BlockSpec((tile, d), lambda i, *_: (jnp.minimum(i, ctx_tiles - 1), 0)),
            pl.BlockSpec((tile, d), lambda i, *_: (jnp.maximum(i - ctx_tiles, 0), 0))]


INPROJ_NORM_CHUNKS = 8


def _inproj_body(h0_ref, mod0_ref, hc_ref, hl_ref, mod_ref, n1_ref, w_ref, o_ref, xn_ref, *, ctx_tiles, chunk):
    i = pl.program_id(0)
    j = pl.program_id(1)

    def normalise(h, mod):
        return (_rms(h, n1_ref[...], D_MODEL) * (1.0 + mod[1:2, :]) + mod[0:1, :]).astype(BF16)

    @pl.when((i == 0) & (j == 0))
    def _():
        xn_ref[0] = normalise(h0_ref[...], mod0_ref[...])

    nxt = i + 1
    rows = pl.ds(pl.multiple_of(jnp.minimum(j, INPROJ_NORM_CHUNKS - 1) * chunk, chunk), chunk)
    xn_ref[nxt % 2, rows, :] = normalise(jnp.where(nxt < ctx_tiles, hc_ref[...], hl_ref[...]), mod_ref[...])
    o_ref[...] = _dot(xn_ref[i % 2], w_ref[...])


def _input_projection(h_ctx, h_lat, mod, norm1, w_in_bf16, cond_of_tile, tm):
    d = h_ctx.shape[1]
    t = h_ctx.shape[0] + h_lat.shape[0]
    n_tiles, ctx_tiles, lat_tiles = t // tm, h_ctx.shape[0] // tm, h_lat.shape[0] // tm
    n = w_in_bf16.shape[1]
    tn = 512
    pieces = INPROJ_NORM_CHUNKS
    chunk = tm // pieces
    assert n // tn >= pieces and ctx_tiles >= 1

    def piece(j):
        return jnp.minimum(j, pieces - 1)

    def nxt(i):
        return jnp.minimum(i + 1, n_tiles - 1)

    ctx_chunk = lambda i, j: (jnp.where(nxt(i) < ctx_tiles, nxt(i) * pieces + piece(j), ctx_tiles * pieces - 1), 0)
    lat_chunk = lambda i, j: (jnp.where(nxt(i) >= ctx_tiles, (nxt(i) - ctx_tiles) * pieces + piece(j), 0), 0)
    return pl.pallas_call(
        functools.partial(_inproj_body, ctx_tiles=ctx_tiles, chunk=chunk),
        grid=(n_tiles, n // tn),
        in_specs=[pl.BlockSpec((tm, d), lambda i, j: (0, 0)),
                  pl.BlockSpec((None, 6, d), lambda i, j: (cond_of_tile(0), 0, 0)),
                  pl.BlockSpec((chunk, d), ctx_chunk),
                  pl.BlockSpec((chunk, d), lat_chunk),
                  pl.BlockSpec((None, 6, d), lambda i, j: (cond_of_tile(nxt(i)), 0, 0)),
                  pl.BlockSpec((1, d), lambda i, j: (0, 0)),
                  pl.BlockSpec((d, tn), lambda i, j: (0, j))],
        out_specs=pl.BlockSpec((tm, tn), lambda i, j: (i, j)),
        out_shape=jax.ShapeDtypeStruct((t, n), F32),
        scratch_shapes=[pltpu.VMEM((2, tm, d), BF16)],
        compiler_params=_params("arbitrary", "arbitrary"),
        name="norm_modulate_in_proj",
    )(h_ctx, mod, h_ctx, h_lat, mod, norm1[None, :], w_in_bf16)


def _hgrn_constants(c):
    nl = int(math.log2(c))
    idx = np.arange(c)
    e = np.zeros((nl + 2, c, c), np.float32)
    m = np.zeros((nl + 1, c, c), np.float32)
    e[0] = idx[None, :] <= idx[:, None]
    e[1] = idx[None, :] > idx[:, None]
    m[0] = np.eye(c)
    for li in range(nl):
        s = c >> (li + 1)
        parent = idx // (2 * s)
        right = (idx % (2 * s)) >= s
        ref = parent * 2 * s + s - 1
        for i in range(c):
            if right[i]:
                e[2 + li, i, ref[i] + 1:i + 1] = 1.0
            else:
                e[2 + li, i, i + 1:ref[i] + 1] = 1.0
        m[1 + li] = right[:, None] & ~right[None, :] & (parent[:, None] == parent[None, :])
    keep = [0, 1] + [2 + li for li in range(nl) if (c >> (li + 1)) < SUBLANES]
    e = e[keep]
    e2 = np.stack([e, e[:, ::-1, ::-1]]).reshape(2, len(keep) * c, c)
    m2 = np.stack([m, m[:, ::-1, ::-1]])
    return jnp.asarray(e2, BF16), jnp.asarray(m2, F32)


def _hgrn_body(*refs, seq, chunk, unroll, heads, has_s0, emit_state, n_alias):
    q_ref, ff_ref, fb_ref, i_ref, g_ref, lb_ref, on_ref, e_ref, m_ref = refs[:9]
    pos = 9
    s0_ref = None
    if has_s0:
        s0_ref = refs[pos]
        pos += 1
    pos += n_alias
    o_ref = refs[pos]
    pos += 1
    if emit_state:
        st_ref = refs[pos]
        pos += 1
    of_ref, ob_ref = refs[pos], refs[pos + 1]
    c = chunk
    n_chunks = seq // c
    assert seq % c == 0 and n_chunks % unroll == 0
    n_levels = m_ref.shape[1] - 1
    gate_refs = (ff_ref, fb_ref)
    out_refs = (of_ref, ob_ref)

    def chunk_step(c0, d, hh, st):
        rows = pl.ds(c0, c)
        lanes = slice(hh * HEAD_W, (hh + 1) * HEAD_W)
        lb = lb_ref[d:d + 1, lanes]
        f = lb + (1.0 - lb) * _sigmoid(gate_refs[d][rows, lanes])
        g = jnp.log2(f)
        k = 1.0 - f
        q = _silu(q_ref[rows, lanes])
        v = i_ref[rows, lanes].astype(BF16)
        g_hi = g.astype(BF16)
        g_lo = (g - g_hi.astype(F32)).astype(BF16)
        g2 = _dot(e_ref[d], jnp.concatenate([g_hi, g_lo], axis=1))
        gsum = g2[:, 0:HEAD_W] + g2[:, HEAD_W:2 * HEAD_W]
        cum = gsum[0:c]
        x_cum = jnp.exp2(cum)
        x_tail = jnp.exp2(gsum[c:2 * c])
        row = lax.broadcasted_iota(jnp.int32, (c, HEAD_W), 0)
        s = m_ref[d, 0] * _dot_nt(q.astype(BF16), k.astype(BF16))
        n_matmul_levels = 0
        for lv in range(n_levels):
            half = c >> (lv + 1)
            if half >= SUBLANES:
                ref_row = half - 1 if d == 0 else half
                ref = jnp.concatenate(
                    [jnp.broadcast_to(cum[b * 2 * half + ref_row:b * 2 * half + ref_row + 1, :], (2 * half, HEAD_W))
                     for b in range(c // (2 * half))], axis=0)
                near = ((row % (2 * half)) >= half) == (d == 0)
                x_l = jnp.exp2(jnp.where(near, cum - ref, ref - cum))
            else:
                x_l = jnp.exp2(gsum[(2 + n_matmul_levels) * c:(3 + n_matmul_levels) * c])
                n_matmul_levels += 1
            s = s + m_ref[d, 1 + lv] * _dot_nt((q * x_l).astype(BF16), (k * x_l).astype(BF16))
        o = _dot_nt((q * x_cum).astype(BF16), st.astype(BF16)) + _dot(s.astype(BF16), v)
        out_refs[d][rows, lanes] = o
        total = x_cum[c - 1:c, :] if d == 0 else x_cum[0:1, :]
        return st * total + _dot_tn(v, (k * x_tail).astype(BF16))

    if has_s0:
        states0 = tuple(s0_ref[d, hh].T for hh in range(heads) for d in range(2))
    else:
        states0 = tuple(jnp.zeros((HEAD_W, HEAD_W), F32) for _ in range(2 * heads))

    def loop(t, states):
        states = list(states)
        for u in range(unroll):
            j = t * unroll + u
            for hh in range(heads):
                states[2 * hh] = chunk_step(_aligned(j * c, c), 0, hh, states[2 * hh])
                states[2 * hh + 1] = chunk_step(_aligned((n_chunks - 1 - j) * c, c), 1, hh, states[2 * hh + 1])
        return tuple(states)

    if n_chunks == unroll:
        states = loop(0, states0)
    else:
        states = lax.fori_loop(0, n_chunks // unroll, loop, states0)
    for hh in range(heads):
        lanes = slice(hh * HEAD_W, (hh + 1) * HEAD_W)
        o = of_ref[:, lanes] + ob_ref[:, lanes]
        o_ref[:, lanes] = (_rms(o, on_ref[...], HEAD_W) * _silu(g_ref[:, lanes])).astype(o_ref.dtype)
        if emit_state:
            st_ref[0, hh] = states[2 * hh].T
            st_ref[1, hh] = states[2 * hh + 1].T


def _hgrn(proj, row_block0, n_seq, seq, lower, onorm, consts, s0, mixed_prev, state_prev, layer, depth):
    e_mat, masks = consts
    latent = s0 is not None
    hps = next(n for n in (4, 2, 1) if 5 * 2 * seq * n * HEAD_W * 4 <= HGRN_INPUT_VMEM)
    width = hps * HEAD_W

    def col(cb):
        return pl.BlockSpec((seq, width), lambda b, h, cb=cb: (row_block0 + b, cb // hps + h))

    state_spec = pl.BlockSpec((None, None, 2, hps, HEAD_W, HEAD_W), lambda b, h: (b, layer, 0, h, 0, 0))
    in_specs = [col(COL_A_Q), col(COL_A_FF), col(COL_A_FB), col(COL_A_I), col(COL_A_G),
                pl.BlockSpec((2, width), lambda b, h: (0, h)),
                pl.BlockSpec((1, HEAD_W), lambda b, h: (0, 0)),
                pl.BlockSpec(e_mat.shape, lambda b, h: (0, 0, 0)),
                pl.BlockSpec(masks.shape, lambda b, h: (0, 0, 0, 0))]
    args = [proj, proj, proj, proj, proj, lower, onorm[None, :], e_mat, masks]
    if latent:
        in_specs.append(state_spec)
        args.append(s0)
        prev = [mixed_prev]
    else:
        prev = [] if state_prev is None else [state_prev]
    alias_specs, alias_args, aliases = _alias_kwargs(len(args), prev, 0 if latent else 1)
    out_specs = [pl.BlockSpec((seq, width), lambda b, h: (row_block0 + b, h))]
    out_shape = [jax.ShapeDtypeStruct((proj.shape[0], GROUP_W), BF16)]
    if not latent:
        out_specs.append(state_spec)
        out_shape.append(jax.ShapeDtypeStruct((n_seq, depth, 2, N_HEADS, HEAD_W, HEAD_W), F32))
    res = pl.pallas_call(
        functools.partial(_hgrn_body, seq=seq, chunk=HGRN_CHUNK, unroll=min(HGRN_UNROLL, seq // HGRN_CHUNK),
                          heads=hps, has_s0=latent,
                          emit_state=not latent, n_alias=len(prev)),
        grid=(n_seq, N_HEADS // hps),
        in_specs=in_specs + alias_specs, out_specs=out_specs, out_shape=out_shape,
        input_output_aliases=aliases,
        scratch_shapes=[pltpu.VMEM((seq, width), F32), pltpu.VMEM((seq, width), F32)],
        compiler_params=_params("parallel", "parallel"),
        name="hgrn2_latent" if latent else "hgrn2_context",
    )(*args, *alias_args)
    return (res[0], None) if latent else res


def _with_ones(v):
    return jnp.concatenate([v, jnp.ones_like(v)], axis=1)


def _softmax_pv(scores, values1):
    mx = functools.reduce(jnp.maximum, [jnp.max(s, axis=-1, keepdims=True) for s in scores])
    acc = functools.reduce(lambda a, b: a + b,
                           [_dot(jnp.exp2(s - mx).astype(BF16), v) for s, v in zip(scores, values1)])
    return acc[:, 0:HEAD_W] / acc[:, HEAD_W:HEAD_W + 1]


def _diff_pv(s0, s1, lam, values1):
    return _softmax_pv([s0], [values1]) - lam * _softmax_pv([s1], [values1])


def _lane_lt(shape, n):
    return lax.broadcasted_iota(jnp.int32, shape, len(shape) - 1) < n


def _rms_head(x, gain):
    return _rms(x, gain, HEAD_W)


def _rms_halves(x, gain2):
    lo = _lane_lt(x.shape, DF_DQK)
    sq = x * x
    ss_lo = jnp.sum(jnp.where(lo, sq, 0.0), axis=-1, keepdims=True)
    ss_hi = jnp.sum(sq, axis=-1, keepdims=True) - ss_lo
    inv = jnp.where(lo, lax.rsqrt(ss_lo * (1.0 / DF_DQK) + EPS), lax.rsqrt(ss_hi * (1.0 / DF_DQK) + EPS))
    return x * inv * gain2


def _lambda(lam_ref, lam_init):
    l = lam_ref[...]
    return (jnp.exp(jnp.sum(l[0:1] * l[1:2], axis=-1, keepdims=True))
            - jnp.exp(jnp.sum(l[2:3] * l[3:4], axis=-1, keepdims=True)) + lam_init)


N_CTX_ATTN_INPUTS = 17


def _ctx_attn_body(*refs, lam_init, kv_heads):
    (bq_ref, bk_ref, bv_ref, cq_ref, ck_ref, cv_ref, dq_ref, dk_ref, dv_ref,
     naq_ref, nak_ref, gq_ref, gk_ref, dfq_ref, dfk_ref, sub_ref, lam_ref) = refs[:N_CTX_ATTN_INPUTS]
    ob_ref, oc_ref, od_ref, kb_ref, vb_ref, kc_ref, vc_ref, kd_ref, vd_ref = refs[-9:]
    group = N_HEADS // GQA_KV_HEADS
    scale = LOG2E * HEAD_W ** -0.5
    lam = _lambda(lam_ref, lam_init)
    for n in range(kv_heads):
        kv_lanes = slice(n * HEAD_W, (n + 1) * HEAD_W)
        kc = _rms_head(ck_ref[:, kv_lanes], gk_ref[...])
        vc = cv_ref[:, kv_lanes]
        kc_ref[n] = kc
        vc_ref[n] = vc
        kc16 = kc.astype(BF16)
        vc1 = _with_ones(vc.astype(BF16))
        for g in range(group):
            h = n * group + g
            lanes = slice(h * HEAD_W, (h + 1) * HEAD_W)
            qc = (_rms_head(cq_ref[:, lanes], gq_ref[...]) * scale).astype(BF16)
            oc_ref[:, lanes] = _softmax_pv([_dot_nt(qc, kc16)], [vc1]).astype(oc_ref.dtype)
            kb = _rms_head(bk_ref[:, lanes], nak_ref[...])
            vb = bv_ref[:, lanes]
            kb_ref[h] = kb
            vb_ref[h] = vb
            qb = (_rms_head(bq_ref[:, lanes], naq_ref[...]) * scale).astype(BF16)
            ob_ref[:, lanes] = _softmax_pv([_dot_nt(qb, kb.astype(BF16))],
                                           [_with_ones(vb.astype(BF16))]).astype(ob_ref.dtype)
            kd = _rms_halves(dk_ref[:, lanes], dfk_ref[...])
            vd = dv_ref[:, lanes]
            kd_ref[h, 0] = kd[:, 0:DF_DQK]
            kd_ref[h, 1] = kd[:, DF_DQK:2 * DF_DQK]
            vd_ref[h] = vd
            qd = (_rms_halves(dq_ref[:, lanes], dfq_ref[...]) * (LOG2E * DF_DQK ** -0.5)).astype(BF16)
            lo = _lane_lt(kd.shape, DF_DQK)
            s0 = _dot_nt(qd, jnp.where(lo, kd, 0.0).astype(BF16))
            s1 = _dot_nt(qd, jnp.where(lo, 0.0, kd).astype(BF16))
            od = _diff_pv(s0, s1, lam, _with_ones(vd.astype(BF16)))
            od_ref[:, lanes] = (_rms_head(od, sub_ref[...]) * (1.0 - lam_init)).astype(od_ref.dtype)


def _context_attention(proj, n_seq, seq, gains, lam_init, layer, depth, caches_prev):
    na_qn, na_kn, gqa_qn, gqa_kn, df_qn, df_kn, df_subln, df_lam = gains
    group = N_HEADS // GQA_KV_HEADS
    kvs = CTX_KV_HEADS_PER_STEP
    n_heads = kvs * group
    width = n_heads * HEAD_W

    def heads(cb):
        return pl.BlockSpec((seq, width), lambda b, n, cb=cb: (b, cb // n_heads + n))

    def kv_head(cb):
        return pl.BlockSpec((seq, kvs * HEAD_W), lambda b, n, cb=cb: (b, cb // kvs + n))

    vec = pl.BlockSpec((1, HEAD_W), lambda b, n: (0, 0))
    cache_heads = pl.BlockSpec((None, None, n_heads, seq, HEAD_W), lambda b, n: (b, layer, n, 0, 0))
    cache_kv = pl.BlockSpec((None, None, kvs, seq, HEAD_W), lambda b, n: (b, layer, n, 0, 0))
    mixed = pl.BlockSpec((seq, width), lambda b, n: (b, n))
    mixed_shape = jax.ShapeDtypeStruct((proj.shape[0], GROUP_W), BF16)
    cache4 = jax.ShapeDtypeStruct((n_seq, depth, N_HEADS, seq, HEAD_W), F32)
    cache2 = jax.ShapeDtypeStruct((n_seq, depth, GQA_KV_HEADS, seq, HEAD_W), F32)
    cache_dk = jax.ShapeDtypeStruct((n_seq, depth, N_HEADS, 2, seq, DF_DQK), F32)
    args = [proj] * 9 + [na_qn[None, :], na_kn[None, :], gqa_qn[None, :], gqa_kn[None, :],
                         jnp.tile(df_qn, 2)[None, :], jnp.tile(df_kn, 2)[None, :], df_subln[None, :], df_lam]
    assert len(args) == N_CTX_ATTN_INPUTS
    alias_specs, alias_args, aliases = _alias_kwargs(len(args), caches_prev, 3)
    return pl.pallas_call(
        functools.partial(_ctx_attn_body, lam_init=lam_init, kv_heads=kvs),
        grid=(n_seq, GQA_KV_HEADS // kvs),
        in_specs=[heads(COL_B_Q), heads(COL_B_K), heads(COL_B_V),
                  heads(COL_C_Q), kv_head(COL_C_K), kv_head(COL_C_V),
                  heads(COL_D_Q), heads(COL_D_K), heads(COL_D_V),
                  vec, vec, vec, vec, vec, vec, vec,
                  pl.BlockSpec((4, DF_DQK), lambda b, n: (0, 0))] + alias_specs,
        out_specs=[mixed, mixed, mixed, cache_heads, cache_heads, cache_kv, cache_kv,
                   pl.BlockSpec((None, None, n_heads, 2, seq, DF_DQK), lambda b, n: (b, layer, n, 0, 0, 0)),
                   cache_heads],
        out_shape=[mixed_shape, mixed_shape, mixed_shape, cache4, cache4, cache2, cache2, cache_dk, cache4],
        input_output_aliases=aliases,
        compiler_params=_params("parallel", "parallel"),
        name="context_attention",
    )(*args, *alias_args)


N_HGRN_INPUTS = 9
N_CTX_GROUP_OUTPUTS = 11


def _ctx_group_body(*refs, seq, lam_init, n_alias):
    hg_in = refs[:N_HGRN_INPUTS]
    at_in = refs[N_HGRN_INPUTS:N_HGRN_INPUTS + N_CTX_ATTN_INPUTS]
    first_out = N_HGRN_INPUTS + N_CTX_ATTN_INPUTS + n_alias
    outs = refs[first_out:first_out + N_CTX_GROUP_OUTPUTS]
    scratch = refs[first_out + N_CTX_GROUP_OUTPUTS:]
    _ctx_attn_body(*at_in, *outs[2:], lam_init=lam_init, kv_heads=GQA_KV_HEADS)
    _hgrn_body(*hg_in, outs[0], outs[1], *scratch, seq=seq, chunk=HGRN_CHUNK,
               unroll=min(HGRN_UNROLL, seq // HGRN_CHUNK), heads=N_HEADS, has_s0=False, emit_state=True, n_alias=0)


def _context_group(proj, n_seq, seq, lower, onorm, consts, gains, lam_init, layer, depth, state_prev, caches_prev):
    e_mat, masks = consts
    na_qn, na_kn, gqa_qn, gqa_kn, df_qn, df_kn, df_subln, df_lam = gains

    def heads(cb):
        return pl.BlockSpec((seq, GROUP_W), lambda b, cb=cb: (b, cb // N_HEADS))

    def kv_heads(cb):
        return pl.BlockSpec((seq, GQA_KV_HEADS * HEAD_W), lambda b, cb=cb: (b, cb // GQA_KV_HEADS))

    def const(shape):
        return pl.BlockSpec(shape, lambda b: (0,) * len(shape))

    vec = const((1, HEAD_W))
    in_specs = [heads(COL_A_Q), heads(COL_A_FF), heads(COL_A_FB), heads(COL_A_I), heads(COL_A_G),
                const((2, GROUP_W)), vec, const(e_mat.shape), const(masks.shape),
                heads(COL_B_Q), heads(COL_B_K), heads(COL_B_V),
                heads(COL_C_Q), kv_heads(COL_C_K), kv_heads(COL_C_V),
                heads(COL_D_Q), heads(COL_D_K), heads(COL_D_V),
                vec, vec, vec, vec, vec, vec, vec, const((4, DF_DQK))]
    args = [proj] * 5 + [lower, onorm[None, :], e_mat, masks] + [proj] * 9 + [
        na_qn[None, :], na_kn[None, :], gqa_qn[None, :], gqa_kn[None, :],
        jnp.tile(df_qn, 2)[None, :], jnp.tile(df_kn, 2)[None, :], df_subln[None, :], df_lam]
    assert len(args) == N_HGRN_INPUTS + N_CTX_ATTN_INPUTS
    prev = [] if state_prev is None else [state_prev] + list(caches_prev)
    aliases = {len(args) + k: out for k, out in enumerate([1, 5, 6, 7, 8, 9, 10][:len(prev)])}
    mixed = pl.BlockSpec((seq, GROUP_W), lambda b: (b, 0))
    mixed_shape = jax.ShapeDtypeStruct((proj.shape[0], GROUP_W), BF16)
    cache_heads = pl.BlockSpec((None, None, N_HEADS, seq, HEAD_W), lambda b: (b, layer, 0, 0, 0))
    cache_kv = pl.BlockSpec((None, None, GQA_KV_HEADS, seq, HEAD_W), lambda b: (b, layer, 0, 0, 0))
    cache4 = jax.ShapeDtypeStruct((n_seq, depth, N_HEADS, seq, HEAD_W), F32)
    cache2 = jax.ShapeDtypeStruct((n_seq, depth, GQA_KV_HEADS, seq, HEAD_W), F32)
    out_specs = [mixed,
                 pl.BlockSpec((None, None, 2, N_HEADS, HEAD_W, HEAD_W), lambda b: (b, layer, 0, 0, 0, 0)),
                 mixed, mixed, mixed, cache_heads, cache_heads, cache_kv, cache_kv,
                 pl.BlockSpec((None, None, N_HEADS, 2, seq, DF_DQK), lambda b: (b, layer, 0, 0, 0, 0)),
                 cache_heads]
    out_shape = [mixed_shape, jax.ShapeDtypeStruct((n_seq, depth, 2, N_HEADS, HEAD_W, HEAD_W), F32),
                 mixed_shape, mixed_shape, mixed_shape, cache4, cache4, cache2, cache2,
                 jax.ShapeDtypeStruct((n_seq, depth, N_HEADS, 2, seq, DF_DQK), F32), cache4]
    assert len(out_specs) == N_CTX_GROUP_OUTPUTS
    return pl.pallas_call(
        functools.partial(_ctx_group_body, seq=seq, lam_init=lam_init, n_alias=len(prev)),
        grid=(n_seq,),
        in_specs=in_specs + [ANY_SPEC] * len(prev), out_specs=out_specs, out_shape=out_shape,
        input_output_aliases=aliases,
        scratch_shapes=[pltpu.VMEM((seq, GROUP_W), F32), pltpu.VMEM((seq, GROUP_W), F32)],
        compiler_params=_params("parallel"),
        name="context_mixers",
    )(*args, *prev)


def _rope_tables(n_tokens, rot_dim):
    t = np.arange(n_tokens)
    row = (t // GRID_W).astype(np.float32)
    col = (t % GRID_W).astype(np.float32)
    n_freq = rot_dim // 4
    inv = (np.float32(ROPE_THETA) ** (-np.arange(n_freq, dtype=np.float32) / np.float32(n_freq))).astype(np.float32)
    ang = np.concatenate([row[:, None] * inv, col[:, None] * inv], axis=-1).astype(np.float32)
    cos, sin, zero = np.cos(ang), np.sin(ang), np.zeros_like(ang)
    reps = HEAD_W // rot_dim
    a = np.tile(np.concatenate([cos, cos], axis=-1), (1, reps))
    b = np.tile(np.concatenate([-sin, zero], axis=-1), (1, reps))
    c = np.tile(np.concatenate([zero, sin], axis=-1), (1, reps))
    return jnp.asarray(np.stack([a, b, c]), F32)


def _rope(x, tab_ref, half):
    return (x * tab_ref[0] + pltpu.roll(x, HEAD_W - half, 1) * tab_ref[1]
            + pltpu.roll(x, half, 1) * tab_ref[2])


def _na_body(q_ref, k_ref, v_ref, ck_ref, cv_ref, bias_ref, qn_ref, kn_ref, prev_ref, o_ref,
             qs_ref, ks_ref, vs_ref, *, seq):
    del prev_ref
    rows = seq // GRID_W
    n_win = WIN_ROWS * GRID_W
    qs_ref[...] = (_rms_head(q_ref[...], qn_ref[...]) * (LOG2E * HEAD_W ** -0.5)).astype(BF16)
    ks_ref[...] = _rms_head(k_ref[...], kn_ref[...]).astype(BF16)
    vs_ref[...] = _with_ones(v_ref[...].astype(BF16))
    ck = ck_ref[...].astype(BF16)
    cv = _with_ones(cv_ref[...].astype(BF16))

    def row_step(r, carry):
        start = jnp.clip(r - WIN_ROWS // 2, 0, rows - WIN_ROWS)
        win = pl.ds(pl.multiple_of(start * GRID_W, GRID_W), n_win)
        qrows = pl.ds(pl.multiple_of(r * GRID_W, GRID_W), GRID_W)
        q = qs_ref[qrows, :]
        s_win = _dot_nt(q, ks_ref[win, :]) + bias_ref[start - r + (WIN_ROWS - 1)]
        s_ctx = _dot_nt(q, ck)
        o_ref[qrows, :] = _softmax_pv([s_win, s_ctx], [vs_ref[win, :], cv]).astype(o_ref.dtype)
        return carry

    lax.fori_loop(0, rows, row_step, 0, unroll=min(rows, 32))


def _na_bias(rpb):
    col = np.arange(GRID_W)
    col_start = np.clip(col - WIN_COLS // 2, 0, GRID_W - WIN_COLS)
    col_ok = (col[None, :] >= col_start[:, None]) & (col[None, :] < col_start[:, None] + WIN_COLS)
    dc = np.clip(col[None, :] - col[:, None] + WIN_COLS - 1, 0, 2 * WIN_COLS - 2).reshape(-1)
    onehot = (np.arange(2 * WIN_COLS - 1)[:, None] == dc[None, :]).astype(np.float32)
    per_dr = jnp.einsum('hdc,cn->hdn', rpb.astype(F32), jnp.asarray(onehot), precision=lax.Precision.HIGHEST)
    per_dr = jnp.where(col_ok[None, None], LOG2E * per_dr.reshape(rpb.shape[0], -1, GRID_W, GRID_W), NEG_INF)
    wins = jnp.stack([per_dr[:, o:o + WIN_ROWS] for o in range(WIN_ROWS)], axis=1)
    return wins.transpose(0, 1, 3, 2, 4).reshape(rpb.shape[0], WIN_ROWS, GRID_W, WIN_ROWS * GRID_W)


def _latent_na(proj, row_block0, n_seq, seq, cache_k, cache_v, layer, bias, na_qn, na_kn, mixed_prev):
    past = cache_k.shape[3]

    def col(cb):
        return pl.BlockSpec((seq, HEAD_W), lambda b, h, cb=cb: (row_block0 + b, cb + h))

    cache = pl.BlockSpec((None, None, None, past, HEAD_W), lambda b, h: (b, layer, h, 0, 0))
    vec = pl.BlockSpec((1, HEAD_W), lambda b, h: (0, 0))
    return pl.pallas_call(
        functools.partial(_na_body, seq=seq),
        grid=(n_seq, N_HEADS),
        in_specs=[col(COL_B_Q), col(COL_B_K), col(COL_B_V), cache, cache,
                  pl.BlockSpec((None, WIN_ROWS, GRID_W, WIN_ROWS * GRID_W), lambda b, h: (h, 0, 0, 0)),
                  vec, vec, ANY_SPEC],
        out_specs=pl.BlockSpec((seq, HEAD_W), lambda b, h: (row_block0 + b, h)),
        out_shape=jax.ShapeDtypeStruct(mixed_prev.shape, BF16),
        input_output_aliases={8: 0},
        scratch_shapes=[pltpu.VMEM((seq, HEAD_W), BF16), pltpu.VMEM((seq, HEAD_W), BF16),
                        pltpu.VMEM((seq, 2 * HEAD_W), BF16)],
        compiler_params=_params("parallel", "parallel"),
        name="latent_neighbourhood_attention",
    )(proj, proj, proj, cache_k, cache_v, bias, na_qn[None, :], na_kn[None, :], mixed_prev)


def _gqa_body(q_ref, k_ref, v_ref, ck_ref, cv_ref, rope_ref, qn_ref, kn_ref, prev_ref, o_ref,
              qs_ref, ks_ref, vs_ref, *, seq, tq):
    del prev_ref
    group = N_HEADS // GQA_KV_HEADS
    half = HEAD_W // 2
    ks_ref[0:seq, :] = _rope(_rms_head(k_ref[...], kn_ref[...]), rope_ref, half).astype(BF16)
    ks_ref[seq:, :] = ck_ref[...].astype(BF16)
    vs_ref[0:seq, :] = _with_ones(v_ref[...].astype(BF16))
    vs_ref[seq:, :] = _with_ones(cv_ref[...].astype(BF16))
    for g in range(group):
        q = _rms_head(q_ref[:, g * HEAD_W:(g + 1) * HEAD_W], qn_ref[...]) * (LOG2E * HEAD_W ** -0.5)
        qs_ref[g] = _rope(q, rope_ref, half).astype(BF16)
    kk = ks_ref[...]
    vv = vs_ref[...]
    for g in range(group):
        def q_step(i, carry, g=g):
            qrows = pl.ds(pl.multiple_of(i * tq, tq), tq)
            o = _softmax_pv([_dot_nt(qs_ref[g, qrows, :], kk)], [vv])
            o_ref[qrows, g * HEAD_W:(g + 1) * HEAD_W] = o.astype(o_ref.dtype)
            return carry

        lax.fori_loop(0, seq // tq, q_step, 0, unroll=min(seq // tq, 8))


def _latent_gqa(proj, row_block0, n_seq, seq, cache_k, cache_v, layer, rope, gqa_qn, gqa_kn, mixed_prev):
    past = cache_k.shape[3]
    group = N_HEADS // GQA_KV_HEADS
    tq = 256
    cache = pl.BlockSpec((None, None, None, past, HEAD_W), lambda b, n: (b, layer, n, 0, 0))
    vec = pl.BlockSpec((1, HEAD_W), lambda b, n: (0, 0))
    return pl.pallas_call(
        functools.partial(_gqa_body, seq=seq, tq=tq),
        grid=(n_seq, GQA_KV_HEADS),
        in_specs=[pl.BlockSpec((seq, group * HEAD_W), lambda b, n: (row_block0 + b, COL_C_Q // group + n)),
                  pl.BlockSpec((seq, HEAD_W), lambda b, n: (row_block0 + b, COL_C_K + n)),
                  pl.BlockSpec((seq, HEAD_W), lambda b, n: (row_block0 + b, COL_C_V + n)),
                  cache, cache,
                  pl.BlockSpec((3, seq, HEAD_W), lambda b, n: (0, 0, 0)),
                  vec, vec, ANY_SPEC],
        out_specs=pl.BlockSpec((seq, group * HEAD_W), lambda b, n: (row_block0 + b, n)),
        out_shape=jax.ShapeDtypeStruct(mixed_prev.shape, BF16),
        input_output_aliases={8: 0},
        scratch_shapes=[pltpu.VMEM((group, seq, HEAD_W), BF16),
                        pltpu.VMEM((seq + past, HEAD_W), BF16),
                        pltpu.VMEM((seq + past, 2 * HEAD_W), BF16)],
        compiler_params=_params("parallel", "parallel"),
        name="latent_gqa_attention",
    )(proj, proj, proj, cache_k, cache_v, rope, gqa_qn[None, :], gqa_kn[None, :], mixed_prev)


def _diff_body(q_ref, k_ref, v_ref, ck_ref, cv_ref, rope_ref, qn_ref, kn_ref, sub_ref, lam_ref, prev_ref, o_ref,
               qs_ref, k0_ref, k1_ref, vs_ref, *, seq, tq, lam_init):
    del prev_ref
    half = DF_DQK // 2
    k = _rope(_rms_halves(k_ref[...], kn_ref[...]), rope_ref, half)
    lo = _lane_lt(k.shape, DF_DQK)
    k0_ref[0:seq, :] = jnp.where(lo, k, 0.0).astype(BF16)
    k1_ref[0:seq, :] = jnp.where(lo, 0.0, k).astype(BF16)
    ck = ck_ref[...]
    lo_c = _lane_lt(ck.shape, DF_DQK)
    k0_ref[seq:, :] = jnp.where(lo_c, ck, 0.0).astype(BF16)
    k1_ref[seq:, :] = jnp.where(lo_c, 0.0, ck).astype(BF16)
    vs_ref[0:seq, :] = _with_ones(v_ref[...].astype(BF16))
    vs_ref[seq:, :] = _with_ones(cv_ref[...].astype(BF16))
    q = _rms_halves(q_ref[...], qn_ref[...]) * (LOG2E * DF_DQK ** -0.5)
    qs_ref[...] = _rope(q, rope_ref, half).astype(BF16)
    lam = _lambda(lam_ref, lam_init)
    k0 = k0_ref[...]
    k1 = k1_ref[...]
    vv = vs_ref[...]

    def q_step(i, carry):
        qrows = pl.ds(pl.multiple_of(i * tq, tq), tq)
        qb = qs_ref[qrows, :]
        o = _diff_pv(_dot_nt(qb, k0), _dot_nt(qb, k1), lam, vv)
        o_ref[qrows, :] = (_rms_head(o, sub_ref[...]) * (1.0 - lam_init)).astype(o_ref.dtype)
        return carry

    lax.fori_loop(0, seq // tq, q_step, 0, unroll=min(seq // tq, 4))


def _latent_diff(proj, row_block0, n_seq, seq, cache_k2, cache_v, layer, rope, df_qn, df_kn, df_subln, df_lam,
                 lam_init, mixed_prev):
    past = cache_k2.shape[3]
    tq = 256

    def col(cb):
        return pl.BlockSpec((seq, HEAD_W), lambda b, h, cb=cb: (row_block0 + b, cb + h))

    cache = pl.BlockSpec((None, None, None, past, HEAD_W), lambda b, h: (b, layer, h, 0, 0))
    vec = pl.BlockSpec((1, HEAD_W), lambda b, h: (0, 0))
    kv_scratch = pltpu.VMEM((seq + past, HEAD_W), BF16)
    return pl.pallas_call(
        functools.partial(_diff_body, seq=seq, tq=tq, lam_init=lam_init),
        grid=(n_seq, N_HEADS),
        in_specs=[col(COL_D_Q), col(COL_D_K), col(COL_D_V), cache, cache,
                  pl.BlockSpec((3, seq, HEAD_W), lambda b, h: (0, 0, 0)),
                  vec, vec, vec, pl.BlockSpec((4, DF_DQK), lambda b, h: (0, 0)), ANY_SPEC],
        out_specs=pl.BlockSpec((seq, HEAD_W), lambda b, h: (row_block0 + b, h)),
        out_shape=jax.ShapeDtypeStruct(mixed_prev.shape, BF16),
        input_output_aliases={10: 0},
        scratch_shapes=[pltpu.VMEM((seq, HEAD_W), BF16), kv_scratch, kv_scratch,
                        pltpu.VMEM((seq + past, 2 * HEAD_W), BF16)],
        compiler_params=_params("parallel", "parallel"),
        name="latent_diff_attention",
    )(proj, proj, proj, cache_k2, cache_v, rope, jnp.tile(df_qn, 2)[None, :], jnp.tile(df_kn, 2)[None, :],
      df_subln[None, :], df_lam, mixed_prev)


def _first_max(vals):
    best = vals[0]
    idx = jnp.zeros(best.shape, jnp.int32)
    for i in range(1, len(vals)):
        better = vals[i] > best
        best = jnp.where(better, vals[i], best)
        idx = jnp.where(better, i, idx)
    return best, idx


def _pick(vals, idx):
    out = vals[0]
    for i in range(1, len(vals)):
        out = jnp.where(idx == i, vals[i], out)
    return out


def _outproj_body(ma_ref, mb_ref, mc_ref, md_ref, w_ref, hc_ref, hl_ref, mod_ref, n2_ref, rw_ref, rb_ref,
                  h1_ref, x2_ref, idx_ref, gate_ref, mixed_ref, *, ctx_tiles):
    for g, m_ref in enumerate((ma_ref, mb_ref, mc_ref, md_ref)):
        mixed_ref[:, g * GROUP_W:(g + 1) * GROUP_W] = m_ref[...]
    y = mod_ref[2:3, :] * _dot(mixed_ref[...], w_ref[...])

    def residual(h_ref):
        h1_ref[...] = h_ref[...] + y

    is_ctx = pl.program_id(0) < ctx_tiles
    pl.when(is_ctx)(lambda: residual(hc_ref))
    pl.when(jnp.logical_not(is_ctx))(lambda: residual(hl_ref))
    h1 = h1_ref[...]
    x2 = _rms(h1, n2_ref[...], D_MODEL) * (1.0 + mod_ref[4:5, :]) + mod_ref[3:4, :]
    x2_ref[...] = _pack_bf16_pairs(x2)
    x_hi = x2.astype(BF16)
    x_lo = (x2 - x_hi.astype(F32)).astype(BF16)
    acc = _dot(x_hi, rw_ref[...])
    logits_tok = acc[:, 0:HEAD_W] + acc[:, HEAD_W:2 * HEAD_W] + _dot(x_lo, rw_ref[:, 0:HEAD_W])
    logits = logits_tok.T[0:N_EXPERTS, :]
    aff_all = _sigmoid(logits)
    sel_all = aff_all + rb_ref[...]
    aff = [aff_all[e:e + 1, :] for e in range(N_EXPERTS)]
    sel = [sel_all[e:e + 1, :] for e in range(N_EXPERTS)]
    neg = jnp.full(sel[0].shape, -jnp.inf, F32)
    scores = []
    for g in range(N_EXP_GROUPS):
        grp = sel[g * EXP_PER_GROUP:(g + 1) * EXP_PER_GROUP]
        m1, i1 = _first_max(grp)
        m2, _ = _first_max([jnp.where(i1 == j, neg, grp[j]) for j in range(EXP_PER_GROUP)])
        scores.append(m1 + m2)
    _, g_best = _first_max(scores)
    in_sel = [_pick([sel[g * EXP_PER_GROUP + j] for g in range(N_EXP_GROUPS)], g_best)
              for j in range(EXP_PER_GROUP)]
    in_aff = [_pick([aff[g * EXP_PER_GROUP + j] for g in range(N_EXP_GROUPS)], g_best)
              for j in range(EXP_PER_GROUP)]
    _, l1 = _first_max(in_sel)
    _, l2 = _first_max([jnp.where(l1 == j, neg, in_sel[j]) for j in range(EXP_PER_GROUP)])
    w1 = _pick(in_aff, l1)
    w2 = _pick(in_aff, l2)
    idx_ref[0:1, :] = g_best * EXP_PER_GROUP + l1
    idx_ref[1:2, :] = g_best * EXP_PER_GROUP + l2
    gate_ref[0:1, :] = w1 / (w1 + w2)
    gate_ref[1:2, :] = w2 / (w1 + w2)


def _output_projection(mixed4, w_out_bf16, h_ctx, h_lat, mod, norm2, router_w, router_b, cond_of_tile, tm):
    d = h_ctx.shape[1]
    t = h_ctx.shape[0] + h_lat.shape[0]
    ctx_tiles = h_ctx.shape[0] // tm
    slab = pl.BlockSpec((tm, GROUP_W), lambda i: (i, 0))
    rw_hi = router_w.astype(BF16)
    rw_lo = (router_w - rw_hi.astype(F32)).astype(BF16)
    pad = ((0, 0), (0, HEAD_W - N_EXPERTS))
    router_split = jnp.concatenate([jnp.pad(rw_hi, pad), jnp.pad(rw_lo, pad)], axis=1)
    return pl.pallas_call(
        functools.partial(_outproj_body, ctx_tiles=ctx_tiles),
        grid=(t // tm,),
        in_specs=[slab, slab, slab, slab,
                  pl.BlockSpec((d, d), lambda i: (0, 0))] + _group_specs(tm, d, ctx_tiles) + [
                  pl.BlockSpec((None, 6, d), lambda i: (cond_of_tile(i), 0, 0)),
                  pl.BlockSpec((1, d), lambda i: (0, 0)),
                  pl.BlockSpec((d, 2 * HEAD_W), lambda i: (0, 0)),
                  pl.BlockSpec((N_EXPERTS, 1), lambda i: (0, 0))],
        out_specs=[pl.BlockSpec((tm, d), lambda i: (i, 0)),
                   pl.BlockSpec((tm, d // 2), lambda i: (i, 0)),
                   pl.BlockSpec((2, tm), lambda i: (0, i)),
                   pl.BlockSpec((2, tm), lambda i: (0, i))],
        out_shape=[jax.ShapeDtypeStruct((t, d), F32), jax.ShapeDtypeStruct((t, d // 2), jnp.uint32),
                   jax.ShapeDtypeStruct((2, t), jnp.int32), jax.ShapeDtypeStruct((2, t), F32)],
        scratch_shapes=[pltpu.VMEM((tm, d), BF16)],
        compiler_params=_params("parallel", vmem=VMEM_LIMIT_PROJ),
        name="out_proj_residual_router",
    )(*mixed4, w_out_bf16, h_ctx, h_lat, mod, norm2[None, :], router_split, router_b[:, None])


def _vmem_row(ref, base, u):
    return ref.at[pl.ds(base, SUBLANES), :].at[pl.ds(u, 1), :]


def _for_row_groups(n_rows, fn):
    def body(g, carry):
        base = pl.multiple_of(g * SUBLANES, SUBLANES)
        for u in range(SUBLANES):
            fn(base, u)
        return carry

    lax.fori_loop(0, n_rows // SUBLANES, body, 0)


def _rows_wait(src_hbm, dst, sem, n_rows):
    pltpu.make_async_copy(src_hbm.at[pl.ds(0, n_rows), :], dst.at[pl.ds(0, n_rows), :], sem).wait()


def _dispatch_body(dest_ref, ps_ref, pe_ref, x_ref, xs_hbm, zero_ref, sem, zsem, *, n_tok, tile):
    i = pl.program_id(0)

    def zero_copy(e):
        first = pl.multiple_of(pe_ref[e] - MOE_ROWS, MOE_ROWS)
        return pltpu.make_async_copy(zero_ref, xs_hbm.at[pl.ds(first, MOE_ROWS), :], zsem)

    @pl.when(i == 0)
    def _():
        zero_ref[...] = jnp.zeros(zero_ref.shape, zero_ref.dtype)
        for e in range(N_EXPERTS):
            @pl.when(pe_ref[e] > ps_ref[e])
            def _(e=e):
                zero_copy(e).start()
        for e in range(N_EXPERTS):
            @pl.when(pe_ref[e] > ps_ref[e])
            def _(e=e):
                zero_copy(e).wait()

    for k in range(2):
        def scatter_row(base, u, k=k):
            row = dest_ref[k * n_tok + i * tile + base + u]
            pltpu.make_async_copy(_vmem_row(x_ref, base, u), xs_hbm.at[pl.ds(row, 1), :], sem).start()

        _for_row_groups(tile, scatter_row)
    for k in range(2):
        _rows_wait(x_ref, xs_hbm, sem, tile)


def _dispatch(x2, dest, pad_start, pad_end, n_rows):
    t, d = x2.shape
    tile = next(m for m in DISPATCH_TILES if t % m == 0)
    grid_spec = pltpu.PrefetchScalarGridSpec(
        num_scalar_prefetch=3,
        grid=(t // tile,),
        in_specs=[pl.BlockSpec((tile, d), lambda i, dst, ps, pe: (i, 0))],
        out_specs=ANY_SPEC,
        scratch_shapes=[pltpu.VMEM((MOE_ROWS, d), x2.dtype), pltpu.SemaphoreType.DMA, pltpu.SemaphoreType.DMA],
    )
    return pl.pallas_call(
        functools.partial(_dispatch_body, n_tok=t, tile=tile),
        grid_spec=grid_spec,
        out_shape=jax.ShapeDtypeStruct((n_rows, d), x2.dtype),
        compiler_params=_params("arbitrary"),
        name="moe_dispatch",
    )(dest, pad_start, pad_end, x2)


def _expert_body(be_ref, nb_ref, nx_ref, x_ref, wg_hbm, wu_hbm, wd_hbm, o_ref,
                 stage_g, stage_u, stage_d, wg_bf, wu_bf, wd_bf, sem, *, layer):
    i = pl.program_id(0)

    def weight_copies(e):
        return (pltpu.make_async_copy(wg_hbm.at[layer, e], stage_g, sem.at[0]),
                pltpu.make_async_copy(wu_hbm.at[layer, e], stage_u, sem.at[1]),
                pltpu.make_async_copy(wd_hbm.at[layer, e], stage_d, sem.at[2]))

    @pl.when(i < nb_ref[0])
    def _():
        e = be_ref[i]

        @pl.when(i == 0)
        def _():
            for c in weight_copies(e):
                c.start()

        @pl.when((i == 0) | (e != be_ref[jnp.maximum(i - 1, 0)]))
        def _():
            for c in weight_copies(e):
                c.wait()
            wg_bf[...] = stage_g[...].astype(BF16)
            wu_bf[...] = stage_u[...].astype(BF16)
            wd_bf[...] = stage_d[...].astype(BF16)
            nxt = nx_ref[e]

            @pl.when(nxt < N_EXPERTS)
            def _():
                for c in weight_copies(nxt):
                    c.start()

        x = _unpack_bf16_pairs(x_ref[...]).astype(BF16)
        hdn = _silu(_dot(x, wg_bf[...])) * _dot(x, wu_bf[...])
        o_ref[...] = _pack_bf16_pairs(_dot(hdn.astype(BF16), wd_bf[...]))

    @pl.when(i >= nb_ref[0])
    def _():
        o_ref[...] = jnp.zeros(o_ref.shape, o_ref.dtype)


def _expert_blocks(xs, block_expert, n_used, next_expert, w_gate, w_up, w_down, layer):
    n_rows, half = xs.shape
    d = 2 * half
    ff = w_gate.shape[-1]
    grid_spec = pltpu.PrefetchScalarGridSpec(
        num_scalar_prefetch=3,
        grid=(n_rows // MOE_ROWS,),
        in_specs=[pl.BlockSpec((MOE_ROWS, half), lambda i, be, nb, nx: (jnp.minimum(i, nb[0] - 1), 0)),
                  ANY_SPEC, ANY_SPEC, ANY_SPEC],
        out_specs=pl.BlockSpec((MOE_ROWS, half), lambda i, be, nb, nx: (i, 0)),
        scratch_shapes=[pltpu.VMEM((d, ff), F32), pltpu.VMEM((d, ff), F32), pltpu.VMEM((ff, d), F32),
                        pltpu.VMEM((d, ff), BF16), pltpu.VMEM((d, ff), BF16), pltpu.VMEM((ff, d), BF16),
                        pltpu.SemaphoreType.DMA((3,))],
    )
    return pl.pallas_call(
        functools.partial(_expert_body, layer=layer),
        grid_spec=grid_spec,
        out_shape=jax.ShapeDtypeStruct((n_rows, half), jnp.uint32),
        compiler_params=_params("arbitrary"),
        name="moe_expert_blocks",
    )(block_expert, n_used, next_expert, xs, w_gate, w_up, w_down)


def _combine_body(dest_ref, h_ref, gate_ref, y_hbm, mod_ref, *rest, n_tok, tile, ctx_tiles):
    out_refs, (ybuf, sem) = rest[:-2], rest[-2:]
    i = pl.program_id(0)
    n_tiles = pl.num_programs(0)

    def start(blk, slot):
        for k in range(2):
            def gather_row(base, u, k=k):
                row = dest_ref[k * n_tok + blk * tile + base + u]
                pltpu.make_async_copy(y_hbm.at[pl.ds(row, 1), :], _vmem_row(ybuf.at[slot, k], base, u),
                                      sem.at[slot]).start()

            _for_row_groups(tile, gather_row)

    @pl.when(i == 0)
    def _():
        start(0, 0)

    @pl.when(i + 1 < n_tiles)
    def _():
        start(i + 1, (i + 1) % 2)

    slot = i % 2
    for k in range(2):
        _rows_wait(y_hbm, ybuf.at[slot, k], sem.at[slot], tile)
    gate = gate_ref[...]
    y0 = _unpack_bf16_pairs(ybuf[slot, 0])
    y1 = _unpack_bf16_pairs(ybuf[slot, 1])
    out = h_ref[...] + mod_ref[5:6, :] * (gate[:, 0:1] * y0 + gate[:, 1:2] * y1)
    @pl.when(i < ctx_tiles)
    def _():
        out_refs[0][...] = out

    @pl.when(i >= ctx_tiles)
    def _():
        out_refs[1][...] = out


def _combine(h1, yb, dest, gates, mod, cond_of_tile, tile, split_rows):
    t, d = h1.shape
    row_tile = pl.BlockSpec((tile, d), lambda i, dst: (i, 0))
    ctx_tiles = split_rows // tile
    out_specs = _group_specs(tile, d, ctx_tiles)
    out_shape = [jax.ShapeDtypeStruct((split_rows, d), F32), jax.ShapeDtypeStruct((t - split_rows, d), F32)]
    grid_spec = pltpu.PrefetchScalarGridSpec(
        num_scalar_prefetch=1,
        grid=(t // tile,),
        in_specs=[row_tile,
                  pl.BlockSpec((tile, 2), lambda i, dst: (i, 0)),
                  ANY_SPEC,
                  pl.BlockSpec((None, 6, d), lambda i, dst: (cond_of_tile(i), 0, 0))],
        out_specs=out_specs,
        scratch_shapes=[pltpu.VMEM((2, 2, tile, yb.shape[1]), yb.dtype), pltpu.SemaphoreType.DMA((2,))],
    )
    return pl.pallas_call(
        functools.partial(_combine_body, n_tok=t, tile=tile, ctx_tiles=ctx_tiles),
        grid_spec=grid_spec,
        out_shape=out_shape,
        compiler_params=_params("arbitrary"),
        name="moe_gated_residual",
    )(dest, h1, gates, yb, mod)


def _moe(h1, x2, idx_t, gate_t, mod, w_gate, w_up, w_down, layer, cond_of_tile, split_rows):
    t, d = h1.shape
    n = 2 * t
    experts = idx_t.reshape(n)
    onehot = (experts[:, None] == jnp.arange(N_EXPERTS, dtype=jnp.int32)[None, :]).astype(BF16)
    blocks = onehot.reshape(n // COMBINE_TILE, COMBINE_TILE, N_EXPERTS)
    tri = jnp.asarray(np.tril(np.ones((COMBINE_TILE, COMBINE_TILE), np.float32)), BF16)
    within = jnp.einsum('ij,bjk->bik', tri, blocks, preferred_element_type=F32)
    block_total = within[:, -1, :]
    block_first = jnp.cumsum(block_total, axis=0) - block_total
    counts = (block_first[-1] + block_total[-1]).astype(jnp.int32)
    before = (within + block_first[:, None, :]).reshape(n, N_EXPERTS) - 1.0
    rank = jnp.sum(before * onehot.astype(F32), axis=1).astype(jnp.int32)
    padded = (counts + MOE_ROWS - 1) // MOE_ROWS * MOE_ROWS
    pad_end = jnp.cumsum(padded).astype(jnp.int32)
    pad_start = pad_end - padded
    dest = (pad_start[experts] + rank).astype(jnp.int32)
    n_blocks = (n + N_EXPERTS * (MOE_ROWS - 1) + MOE_ROWS - 1) // MOE_ROWS
    block_first_row = jnp.arange(n_blocks, dtype=jnp.int32) * MOE_ROWS
    block_expert = jnp.minimum(jnp.sum((pad_end[None, :] <= block_first_row[:, None]).astype(jnp.int32), axis=1),
                               N_EXPERTS - 1)
    n_used = (pad_end[-1:] // MOE_ROWS).astype(jnp.int32)
    xs = _dispatch(x2, dest, pad_start, pad_end, n_blocks * MOE_ROWS)
    ids = jnp.arange(N_EXPERTS, dtype=jnp.int32)
    later_with_rows = (counts[None, :] > 0) & (ids[None, :] > ids[:, None])
    next_expert = jnp.min(jnp.where(later_with_rows, ids[None, :], N_EXPERTS), axis=1).astype(jnp.int32)
    yb = _expert_blocks(xs, block_expert, n_used, next_expert, w_gate, w_up, w_down, layer)
    return _combine(h1, yb, dest, gate_t.T, mod, cond_of_tile, COMBINE_TILE, split_rows)


def kernel(x_prompt, x_sample, cache_na_k, cache_na_v, cache_gqa_k, cache_gqa_v, cache_diff_k, cache_diff_v, state_hgrn, c, c_ctx, w_mod, b_mod, norm1, norm2, w_in, w_out, hg_lb_logits, hg_onorm, na_qn, na_kn, na_rpb, gqa_qn, gqa_kn, df_qn, df_kn, df_lam, df_subln, router_w, router_b, w_gate, w_up, w_down):
    n_ctx, ctx_len, d = x_prompt.shape
    n_lat, lat_len, _ = x_sample.shape
    depth = w_in.shape[0]
    t_ctx = n_ctx * ctx_len
    assert t_ctx % lat_len == 0 and lat_len % GRID_W == 0 and lat_len // GRID_W >= WIN_ROWS
    tm = next(m for m in (1024, 512, 256) if t_ctx % m == 0 and lat_len % m == 0)
    tm2 = min(tm, 512)
    lat_block0 = t_ctx // lat_len

    def cond_tile(tile_rows):
        def cond_of_tile(i):
            return jnp.where(i < t_ctx // tile_rows, 0, 1 + (i - t_ctx // tile_rows) // (lat_len // tile_rows))
        return cond_of_tile

    sm = jax.nn.softmax(hg_lb_logits.astype(F32), axis=0)
    lower = jnp.cumsum(sm, axis=0) - sm[0:1]
    mod_all = _modulation(jnp.concatenate([c_ctx[None, :], c], axis=0), w_mod, b_mod)
    mod_all = mod_all.reshape(depth, 1 + n_lat, 6, d)
    hgrn_consts = _hgrn_constants(HGRN_CHUNK)
    rope_c = _rope_tables(lat_len, HEAD_W)
    rope_d = _rope_tables(lat_len, DF_DQK)
    past = cache_diff_k.shape[4]
    cache_diff_k2 = cache_diff_k.transpose(0, 1, 2, 4, 3, 5).reshape(n_lat, depth, N_HEADS, past, HEAD_W)

    h_ctx, h_lat = x_prompt.reshape(t_ctx, d), x_sample.reshape(n_lat * lat_len, d)
    caches, states = [], None
    for layer in range(depth):
        mod = mod_all[layer]
        lam_init = 0.8 - 0.6 * math.exp(-0.3 * layer)
        proj = _input_projection(h_ctx, h_lat, mod, norm1[layer], w_in[layer].astype(BF16), cond_tile(tm), tm)
        gains = (na_qn[layer], na_kn[layer], gqa_qn[layer], gqa_kn[layer], df_qn[layer], df_kn[layer],
                 df_subln[layer], df_lam[layer])
        mix_a, states, mix_b, mix_c, mix_d, *caches = _context_group(
            proj, n_ctx, ctx_len, lower[layer], hg_onorm[layer], hgrn_consts, gains, lam_init, layer, depth,
            states, caches)
        mix_a, _ = _hgrn(proj, lat_block0, n_lat, lat_len, lower[layer], hg_onorm[layer], hgrn_consts,
                         state_hgrn, mix_a, None, layer, depth)
        mix_b = _latent_na(proj, lat_block0, n_lat, lat_len, cache_na_k, cache_na_v, layer,
                           _na_bias(na_rpb[layer]), na_qn[layer], na_kn[layer], mix_b)
        mix_c = _latent_gqa(proj, lat_block0, n_lat, lat_len, cache_gqa_k, cache_gqa_v, layer, rope_c,
                            gqa_qn[layer], gqa_kn[layer], mix_c)
        mix_d = _latent_diff(proj, lat_block0, n_lat, lat_len, cache_diff_k2, cache_diff_v, layer, rope_d,
                             df_qn[layer], df_kn[layer], df_subln[layer], df_lam[layer], lam_init, mix_d)
        h1, x2, idx_t, gate_t = _output_projection((mix_a, mix_b, mix_c, mix_d), w_out[layer].astype(BF16), h_ctx,
                                                   h_lat, mod, norm2[layer], router_w, router_b, cond_tile(tm2), tm2)
        h_ctx, h_lat = _moe(h1, x2, idx_t, gate_t, mod, w_gate, w_up, w_down, layer, cond_tile(COMBINE_TILE), t_ctx)
    y_prompt = h_ctx.reshape(n_ctx, ctx_len, d)
    y_sample = h_lat.reshape(n_lat, lat_len, d)
    return (y_prompt, y_sample, *caches, states)
```
